```python
import math
import jax, jax.numpy as jnp
from jax import lax
import numpy as np

D_MODEL = 1024
BATCH = 4
SEQ = 4096
DEPTH = 1

MEM_LEN = 256
EPS = 1e-6
HY_WIDTH = 768
HY_ORDER = 2
HY_SHORT = 3
HY_EMB = 33
HY_BANDS = (HY_EMB - 1) // 2
HY_FFN = 64
HY_FAST_PCT = 0.3
HY_SLOW_PCT = 1.5
HY_TARGET = 1e-2
FILTER_SCALE = 0.03
HEAD_DIM = 64
HEADS_PER_GROUP = 4
DIL_PAIRS = ((128, 1), (512, 4), (2048, 16))
N_GROUPS = len(DIL_PAIRS)
ATTN_HEADS = N_GROUPS * HEADS_PER_GROUP
ATTN_WIDTH = ATTN_HEADS * HEAD_DIM
ATTN_OUT = HEADS_PER_GROUP * HEAD_DIM
ROPE_THETA = 10000.0
IN_COLS = 3 * HY_WIDTH + 3 * ATTN_WIDTH
X_HEADS = 4
X_HEAD_DIM = 128
N_EXPERTS = 16
EC_FACTOR = 2
EXPERT_FF = 2048

kernel_name = "hybrid_hyena_dilated_attn_ec_moe_encoder"


def rmsnorm(x, g):
    xf = x.astype(jnp.float32)
    y = xf * lax.rsqrt(jnp.mean(xf * xf, axis=-1, keepdims=True) + EPS)
    return (y * g.astype(jnp.float32)).astype(x.dtype)


def rope(x, pos):
    dh = x.shape[-1]
    inv = ROPE_THETA ** (-jnp.arange(0, dh, 2, dtype=jnp.float32) / dh)
    ang = pos.astype(jnp.float32)[:, None] * inv[None, :]
    cos = jnp.cos(ang)[None, :, None, :]
    sin = jnp.sin(ang)[None, :, None, :]
    xf = x.astype(jnp.float32)
    x1, x2 = xf[..., : dh // 2], xf[..., dh // 2:]
    return jnp.concatenate([x1 * cos - x2 * sin, x2 * cos + x1 * sin], axis=-1).astype(x.dtype)


def short_conv_centred(u, w, b):
    up = jnp.pad(u, ((0, 0), (1, 1), (0, 0)))
    return up[:, :-2] * w[0] + up[:, 1:-1] * w[1] + up[:, 2:] * w[2] + b


def hyena_filters(L, w1, b1, f1, w2, b2, f2, w3):
    f32 = jnp.float32
    t = jnp.linspace(0.0, 1.0, L, dtype=f32)[:, None]
    grid = 2.0 * math.pi * jnp.arange(L, dtype=f32)[:, None] / L
    bands = jnp.linspace(1e-4, HY_BANDS - 1, HY_BANDS, dtype=f32)[None, :]
    emb = jnp.concatenate([t, jnp.cos(bands * grid), -jnp.sin(bands * grid)], axis=-1)
    h = jnp.sin(f1.astype(f32) * (emb @ w1.astype(f32) + b1.astype(f32)))
    h = jnp.sin(f2.astype(f32) * (h @ w2.astype(f32) + b2.astype(f32)))
    h = h @ w3.astype(f32)
    max_decay = math.log(HY_TARGET) / HY_FAST_PCT
    min_decay = math.log(HY_TARGET) / HY_SLOW_PCT
    deltas = jnp.linspace(min_decay, max_decay, HY_WIDTH, dtype=f32)
    decay = jnp.exp(-t * jnp.abs(deltas)[None, :])
    return h.reshape(L, HY_ORDER, 2, HY_WIDTH) * decay[:, None, None, :]


def bidir_fftconv(z, h_fwd, h_bwd, skip):
    L = z.shape[1]
    k = jnp.concatenate([h_fwd, jnp.zeros_like(h_fwd[:1]), h_bwd[:0:-1]], axis=0)
    zf = z.astype(jnp.float32)
    Z = jnp.fft.rfft(zf, n=2 * L, axis=1)
    K = jnp.fft.rfft(k, n=2 * L, axis=0)
    y = jnp.fft.irfft(Z * K[None], n=2 * L, axis=1)[:, :L]
    return (y + zf * skip.astype(jnp.float32)).astype(z.dtype)


def dilated_band_attention(q, k, v, dilation, n_side):
    B, S, H, dh = q.shape
    Lr = S // dilation
    blk = n_side
    nb = -(-Lr // blk)
    Lp = nb * blk

    def to_res(t):
        t = t.reshape(B, Lr, dilation, H, dh).transpose(0, 2, 3, 1, 4)
        return jnp.pad(t, ((0, 0), (0, 0), (0, 0), (0, Lp - Lr), (0, 0)))

    def windows(t):
        tp = jnp.pad(t, ((0, 0), (0, 0), (0, 0), (blk, blk), (0, 0))).reshape(B, dilation, H, nb + 2, blk, dh)
        return jnp.concatenate([tp[:, :, :, :-2], tp[:, :, :, 1:-1], tp[:, :, :, 2:]], axis=4)

    qb = to_res(q).reshape(B, dilation, H, nb, blk, dh)
    kw = windows(to_res(k))
    vw = windows(to_res(v))
    qi = jnp.arange(nb)[:, None] * blk + jnp.arange(blk)[None, :]
    ki = (jnp.arange(nb)[:, None] - 1) * blk + jnp.arange(3 * blk)[None, :]
    rel = ki[:, None, :] - qi[:, :, None]
    valid = (jnp.abs(rel) <= n_side) & (ki[:, None, :] >= 0) & (ki[:, None, :] < Lr)
    s = jnp.einsum('bdhnqc,bdhnkc->bdhnqk', qb, kw, preferred_element_type=jnp.float32) * (dh ** -0.5)
    s = jnp.where(valid, s, -1e30)
    lse = jax.nn.logsumexp(s, axis=-1, keepdims=True)
    p = jnp.exp(s - lse).astype(v.dtype)
    o = jnp.einsum('bdhnqk,bdhnkc->bdhnqc', p, vw)
    o = o.reshape(B, dilation, H, Lp, dh)[:, :, :, :Lr].transpose(0, 3, 1, 2, 4).reshape(B, S, H, dh)
    lse = lse[..., 0].reshape(B, dilation, H, Lp)[..., :Lr].transpose(0, 3, 1, 2).reshape(B, S, H)
    return o, lse


def setup_inputs(seed: int = 0) -> dict:
    key = jax.random.key(seed)
    ks = jax.random.split(key, 36)
    f32 = jnp.float32
    D, W, F, E = D_MODEL, HY_WIDTH, EXPERT_FF, N_EXPERTS

    def nrm(k, shape, scale):
        return jax.random.normal(k, shape, f32) * scale

    return {
        "x": nrm(ks[0], (BATCH, SEQ, D), 1.0),
        "mem": nrm(ks[1], (BATCH, MEM_LEN, D), 1.0),
        "ln_mix_g": 1.0 + nrm(ks[2], (D,), 0.01),
        "w_in": nrm(ks[3], (D, IN_COLS), D ** -0.5),
        "hy_conv_w": nrm(ks[4], (HY_SHORT, 3 * W), HY_SHORT ** -0.5),
        "hy_conv_b": nrm(ks[5], (3 * W,), 0.02),
        "filt_w1": nrm(ks[6], (HY_EMB, HY_FFN), HY_EMB ** -0.5),
        "filt_b1": nrm(ks[7], (HY_FFN,), 0.02),
        "filt_freq1": 1.0 + nrm(ks[8], (HY_FFN,), 0.01),
        "filt_w2": nrm(ks[9], (HY_FFN, HY_FFN), HY_FFN ** -0.5),
        "filt_b2": nrm(ks[10], (HY_FFN,), 0.02),
        "filt_freq2": 1.0 + nrm(ks[11], (HY_FFN,), 0.01),
        "filt_w3": nrm(ks[12], (HY_FFN, 2 * HY_ORDER * W), FILTER_SCALE * HY_FFN ** -0.5),
        "hy_skip": nrm(ks[13], (HY_ORDER, W), 1.0),
        "w_up_hy": nrm(ks[14], (W, D), W ** -0.5),
        "w_up_attn": nrm(ks[15], (ATTN_OUT, D), ATTN_OUT ** -0.5),
        "w_gate": nrm(ks[16], (D, 2 * D), D ** -0.5),
        "b_gate": nrm(ks[17], (2 * D,), 0.01),
        "w_out": nrm(ks[18], (D, D), D ** -0.5),
        "ln_x_g": 1.0 + nrm(ks[19], (D,), 0.01),
        "ln_mem_g": 1.0 + nrm(ks[20], (D,), 0.01),
        "w_q_x": nrm(ks[21], (D, X_HEADS * X_HEAD_DIM), D ** -0.5),
        "w_kv_mem": nrm(ks[22], (D, 2 * X_HEADS * X_HEAD_DIM), D ** -0.5),
        "w_o_x": nrm(ks[23], (X_HEADS * X_HEAD_DIM, D), (X_HEADS * X_HEAD_DIM) ** -0.5),
        "ln_moe_g": 1.0 + nrm(ks[24], (D,), 0.01),
        "w_router": nrm(ks[25], (D, E), D ** -0.5),
        "w_e_gate": nrm(ks[26], (E, D, F), D ** -0.5),
        "w_e_up": nrm(ks[27], (E, D, F), D ** -0.5),
        "w_e_down": nrm(ks[28], (E, F, D), F ** -0.5),
        "ln_f_g": 1.0 + nrm(ks[29], (D,), 0.01),
    }


def reference(x, mem, ln_mix_g, w_in, hy_conv_w, hy_conv_b, filt_w1, filt_b1, filt_freq1,
              filt_w2, filt_b2, filt_freq2, filt_w3, hy_skip, w_up_hy, w_up_attn, w_gate, b_gate,
              w_out, ln_x_g, ln_mem_g, w_q_x, w_kv_mem, w_o_x, ln_moe_g, w_router, w_e_gate,
              w_e_up, w_e_down, ln_f_g):
    B, S, D = x.shape
    M = mem.shape[1]
    W = HY_WIDTH
    pos = jnp.arange(S)
    for _ in range(DEPTH):
        h = rmsnorm(x, ln_mix_g)
        proj = h @ w_in
        p_hy = proj[..., : 3 * W]
        p_q, p_k, p_v = jnp.split(proj[..., 3 * W:], 3, axis=-1)

        u = short_conv_centred(p_hy, hy_conv_w, hy_conv_b)
        z, g1, g2 = jnp.split(u, 3, axis=-1)
        filt = hyena_filters(S, filt_w1, filt_b1, filt_freq1, filt_w2, filt_b2, filt_freq2, filt_w3)
        for o, gate in enumerate((g1, g2)):
            z = gate * bidir_fftconv(z, filt[:, o, 0], filt[:, o, 1], hy_skip[o])
        y_hy = z @ w_up_hy

        q = rope(p_q.reshape(B, S, ATTN_HEADS, HEAD_DIM), pos).reshape(B, S, N_GROUPS, HEADS_PER_GROUP, HEAD_DIM)
        k = rope(p_k.reshape(B, S, ATTN_HEADS, HEAD_DIM), pos).reshape(B, S, N_GROUPS, HEADS_PER_GROUP, HEAD_DIM)
        v = p_v.reshape(B, S, N_GROUPS, HEADS_PER_GROUP, HEAD_DIM)
        outs, lses = [], []
        for g, (window, dil) in enumerate(DIL_PAIRS):
            o_g, l_g = dilated_band_attention(q[:, :, g], k[:, :, g], v[:, :, g], dil, window // (2 * dil))
            outs.append(o_g)
            lses.append(l_g)
        wts = jax.nn.softmax(jnp.stack(lses, axis=0), axis=0).astype(x.dtype)
        attn = jnp.einsum('gbsh,gbshc->bshc', wts, jnp.stack(outs, axis=0)).reshape(B, S, ATTN_OUT)
        y_at = attn @ w_up_attn

        gates = jax.nn.sigmoid((h @ w_gate + b_gate).astype(jnp.float32)).astype(x.dtype)
        g_hy, g_at = jnp.split(gates, 2, axis=-1)
        x = x + (g_hy * y_hy + g_at * y_at) @ w_out

        hx = rmsnorm(x, ln_x_g)
        mn = rmsnorm(mem, ln_mem_g)
        qx = (hx @ w_q_x).reshape(B, S, X_HEADS, X_HEAD_DIM)
        kx, vx = jnp.split((mn @ w_kv_mem).reshape(B, M, 2 * X_HEADS, X_HEAD_DIM), 2, axis=2)
        sx = jnp.einsum('bqhc,bkhc->bhqk', qx, kx, preferred_element_type=jnp.float32) * (X_HEAD_DIM ** -0.5)
        px = jax.nn.softmax(sx, axis=-1).astype(x.dtype)
        ox = jnp.einsum('bhqk,bkhc->bqhc', px, vx).reshape(B, S, X_HEADS * X_HEAD_DIM)
        x = x + ox @ w_o_x

        hm = rmsnorm(x, ln_moe_g)
        aff = jax.nn.softmax((hm @ w_router).astype(jnp.float32), axis=-1)
        cap = max(1, EC_FACTOR * S // N_EXPERTS)
        gate_v, idx = lax.top_k(aff.transpose(0, 2, 1), cap)
        flat = (jnp.arange(B)[:, None, None] * S + idx).reshape(-1)
        hflat = hm.reshape(B * S, D)
        xe = hflat[flat].reshape(B, N_EXPERTS, cap, D)
        a = jnp.einsum('becd,edf->becf', xe, w_e_gate)
        b_ = jnp.einsum('becd,edf->becf', xe, w_e_up)
        ye = jnp.einsum('becf,efd->becd', jax.nn.silu(a) * b_, w_e_down) * gate_v[..., None].astype(x.dtype)
        y_moe = jnp.zeros((B * S, D), x.dtype).at[flat].add(ye.reshape(-1, D))
        x = x + y_moe.reshape(B, S, D)
    return rmsnorm(x, ln_f_g)
```

```python
import functools
import math

import numpy as np
import jax
import jax.numpy as jnp
from jax import lax
from jax.experimental import pallas as pl
from jax.experimental.pallas import tpu as pltpu

F32 = jnp.float32
BF16 = jnp.bfloat16
HIGHEST = lax.Precision.HIGHEST

EPS = 1e-6
HY_WIDTH = 768
HY_BANDS = 16
HY_FFN = 64
HY_FAST_PCT = 0.3
HY_SLOW_PCT = 1.5
HY_TARGET = 1e-2
HEAD_DIM = 64
HEADS_PER_GROUP = 4
DIL_PAIRS = ((128, 1), (512, 4), (2048, 16))
N_GROUPS = len(DIL_PAIRS)
ATTN_WIDTH = N_GROUPS * HEADS_PER_GROUP * HEAD_DIM
ATTN_OUT = HEADS_PER_GROUP * HEAD_DIM
ROPE_THETA = 10000.0
X_HEADS = 4
X_HEAD_DIM = 128
N_EXPERTS = 16
EC_FACTOR = 2

LANES = 128
SUBLANES = 8
VMEM_LIMIT = 56 * 1024 * 1024

FFT_N2 = 128
HY_CB = 128


def _cparams(sem, vmem=VMEM_LIMIT):
    return pltpu.CompilerParams(dimension_semantics=sem, vmem_limit_bytes=vmem)


def _rms(x, g):
    return x * lax.rsqrt(jnp.mean(x * x, axis=-1, keepdims=True) + EPS) * g


def _dot(a, b):
    return jnp.dot(a, b, preferred_element_type=F32)


def _dot_hi(a, b):
    return jnp.dot(a, b, precision=HIGHEST, preferred_element_type=F32)


def _dot_t(a, b):
    return lax.dot_general(a, b, (((1,), (1,)), ((), ())), preferred_element_type=F32)


def _proj_body(x_ref, g_ref, w_ref, bg_ref, cos_ref, sin_ref,
               phy_ref, q_ref, k_ref, v_ref, gate_ref, *, hyc, aw):
    h = _rms(x_ref[...], g_ref[...]).astype(BF16)
    phy_ref[...] = _dot(h, w_ref[:, :hyc])
    cos = cos_ref[...]
    sin = sin_ref[...]
    tm = cos.shape[0]
    lane = lax.broadcasted_iota(jnp.int32, (tm, LANES), 1)
    first = (lane % HEAD_DIM) < (HEAD_DIM // 2)

    def rope_store(col0, out_ref, scale):
        t = _dot(h, w_ref[:, col0:col0 + aw])
        for j in range(aw // LANES):
            tj = t[:, j * LANES:(j + 1) * LANES]
            partner = jnp.where(first, pltpu.roll(tj, LANES - HEAD_DIM // 2, 1),
                                pltpu.roll(tj, HEAD_DIM // 2, 1))
            out_ref[:, j * LANES:(j + 1) * LANES] = ((tj * cos + partner * sin) * scale).astype(BF16)

    rope_store(hyc, q_ref, HEAD_DIM ** -0.5)
    rope_store(hyc + aw, k_ref, 1.0)
    v_ref[...] = _dot(h, w_ref[:, hyc + 2 * aw:hyc + 3 * aw]).astype(BF16)
    gate_ref[...] = jax.nn.sigmoid(_dot(h, w_ref[:, hyc + 3 * aw:]) + bg_ref[...]).astype(BF16)


def _proj(x2d, g, w_all, b_gate, cos_t, sin_t, seq, tm=256):
    n, d = x2d.shape
    hyc = 3 * HY_WIDTH
    aw = ATTN_WIDTH
    gd = w_all.shape[1] - hyc - 3 * aw
    nseq = seq // tm
    row = lambda i: (i, 0)
    const = lambda i: (0, 0)
    return pl.pallas_call(
        functools.partial(_proj_body, hyc=hyc, aw=aw),
        grid=(n // tm,),
        in_specs=[
            pl.BlockSpec((tm, d), row),
            pl.BlockSpec((1, d), const),
            pl.BlockSpec(w_all.shape, const, pipeline_mode=pl.Buffered(1)),
            pl.BlockSpec((1, gd), const),
            pl.BlockSpec((tm, LANES), lambda i: (i % nseq, 0)),
            pl.BlockSpec((tm, LANES), lambda i: (i % nseq, 0)),
        ],
        out_specs=[
            pl.BlockSpec((tm, hyc), row),
            pl.BlockSpec((tm, aw), row),
            pl.BlockSpec((tm, aw), row),
            pl.BlockSpec((tm, aw), row),
            pl.BlockSpec((tm, gd), row),
        ],
        out_shape=[
            jax.ShapeDtypeStruct((n, hyc), F32),
            jax.ShapeDtypeStruct((n, aw), BF16),
            jax.ShapeDtypeStruct((n, aw), BF16),
            jax.ShapeDtypeStruct((n, aw), BF16),
            jax.ShapeDtypeStruct((n, gd), BF16),
        ],
        compiler_params=_cparams(("arbitrary",)),
        name="proj",
    )(x2d, g, w_all, b_gate, cos_t, sin_t)


def _filt_body(t_ref, grid_ref, bands_ref, w1t_ref, w1c_ref, w1s_ref, b1_ref, f1_ref,
               w2_ref, b2_ref, f2_ref, feat_ref):
    ang = bands_ref[...] * grid_ref[...]
    pre = (t_ref[...] * w1t_ref[...] + _dot_hi(jnp.cos(ang), w1c_ref[...])
           + _dot_hi(-jnp.sin(ang), w1s_ref[...]))
    h = jnp.sin(f1_ref[...] * (pre + b1_ref[...]))
    feat_ref[...] = jnp.sin(f2_ref[...] * (_dot_hi(h, w2_ref[...]) + b2_ref[...]))


def _filt(t_col, grid_col, bands, w1, b1, f1, w2, b2, f2):
    length = t_col.shape[0]
    nb = bands.shape[1]
    args = (t_col, grid_col, bands, w1[0:1], w1[1:1 + nb], w1[1 + nb:], b1[None], f1[None],
            w2, b2[None], f2[None])
    return pl.pallas_call(
        _filt_body,
        out_shape=jax.ShapeDtypeStruct((length, HY_FFN), F32),
        compiler_params=_cparams(None),
        name="filt",
    )(*args)


def _fft_tables(length):
    n = 2 * length
    n2 = FFT_N2
    n1 = n // n2
    h1 = n1 // 2
    k1 = np.arange(n1)[:, None]
    g_sig = np.zeros((n2, 2 * n1, 2 * h1))
    g_flt = np.zeros((n2, 2 * n1, 2 * h1))
    g_inv = np.zeros((n2, 2 * h1, 2 * n1))
    j = np.arange(h1)[None, :]
    for r in range(n2):
        th = 2 * np.pi * (k1 * (n2 * j + r) % n) / n
        gr, gi = np.cos(th), -np.sin(th)
        g_sig[r, 0::2, :h1] = gr
        g_sig[r, 0::2, h1:] = -gi
        g_sig[r, 1::2, :h1] = gi
        g_sig[r, 1::2, h1:] = gr
        g_flt[r, 0::2, :h1] = gr
        g_flt[r, 1::2, :h1] = gi
        off = (n2 - r) % n2
        m = off + n2 * j
        thb = 2 * np.pi * (k1 * m % n) / n
        live = (m >= 1).astype(np.float64)
        g_flt[r, 0::2, h1:] = np.cos(thb) * live
        g_flt[r, 1::2, h1:] = np.sin(thb) * live
        wr, wi = (np.cos(th) / n).T, (np.sin(th) / n).T
        g_inv[r, :h1, 0::2] = wr
        g_inv[r, :h1, 1::2] = -wi
        g_inv[r, h1:, 0::2] = wi
        g_inv[r, h1:, 1::2] = wr
    a = np.arange(n2)
    th2 = 2 * np.pi * (np.outer(a, a) % n2) / n2
    c2, s2 = np.cos(th2), np.sin(th2)
    f_fwd = np.block([[c2, s2], [-s2, c2]])
    f_inv = np.block([[c2, -s2], [s2, c2]])
    cast = lambda z: jnp.asarray(z, dtype=F32).astype(BF16)
    return cast(g_sig), cast(g_flt), cast(g_inv), cast(f_fwd), cast(f_inv)


def _fft_stage1(src_a_ref, src_b_ref, tab_ref, work_ref, n1, is_filter):
    n2 = FFT_N2
    h1 = n1 // 2

    def body(r, c):
        offb = jnp.where(r == 0, 0, n2 - r) if is_filter else r
        ra = src_a_ref[pl.ds(r, h1, stride=n2), :]
        rb = src_b_ref[pl.ds(offb, h1, stride=n2), :]
        rhs = jnp.concatenate([ra, rb], axis=0).astype(BF16)
        work_ref[pl.ds(r, 2 * n1, stride=n2), :] = _dot(tab_ref[r], rhs)
        return c

    lax.fori_loop(0, n2, body, 0)


def _hyfilt_body(t_ref, delta_ref, feat_ref, w3f_ref, w3b_ref, gflt_ref, ffwd_ref, khat_ref,
                 hf_ref, hb_ref, work_ref, *, length, rc):
    n2 = FFT_N2
    n1 = 2 * length // n2

    def gen(i, c):
        r0 = pl.multiple_of(i * rc, rc)
        decay = jnp.exp(-t_ref[pl.ds(r0, rc), :] * jnp.abs(delta_ref[...]))
        feat = feat_ref[pl.ds(r0, rc), :]
        hf_ref[pl.ds(r0, rc), :] = _dot_hi(feat, w3f_ref[...]) * decay
        hb_ref[pl.ds(r0, rc), :] = _dot_hi(feat, w3b_ref[...]) * decay
        return c

    lax.fori_loop(0, length // rc, gen, 0)
    _fft_stage1(hf_ref, hb_ref, gflt_ref, work_ref, n1, True)

    def stage2(k1, c):
        r0 = pl.multiple_of(k1 * (2 * n2), 2 * n2)
        blk = work_ref[pl.ds(r0, 2 * n2), :].astype(BF16)
        khat_ref[0, pl.ds(r0, 2 * n2), :] = _dot(ffwd_ref[...], blk).astype(BF16)
        return c

    lax.fori_loop(0, n1, stage2, 0)


def _hyfilt(t_col, delta, feat, w3, g_flt, f_fwd, rc=512):
    s = t_col.shape[0]
    cb = HY_CB
    ncb = HY_WIDTH // cb
    n2 = FFT_N2
    n1 = 2 * s // n2
    norder = w3.shape[1] // (2 * HY_WIDTH)
    one = pl.Buffered(1)

    def full(a):
        nd = a.ndim
        return pl.BlockSpec(a.shape, lambda o, c: (0,) * nd, pipeline_mode=one)

    return pl.pallas_call(
        functools.partial(_hyfilt_body, length=s, rc=rc),
        grid=(norder, ncb),
        in_specs=[full(t_col),
                  pl.BlockSpec((1, cb), lambda o, c: (0, c)),
                  full(feat),
                  pl.BlockSpec((HY_FFN, cb), lambda o, c: (0, 2 * ncb * o + c)),
                  pl.BlockSpec((HY_FFN, cb), lambda o, c: (0, 2 * ncb * o + ncb + c)),
                  full(g_flt), full(f_fwd)],
        out_specs=pl.BlockSpec((1, 2 * n1 * n2, cb), lambda o, c: (o, 0, c)),
        out_shape=jax.ShapeDtypeStruct((norder, 2 * n1 * n2, HY_WIDTH), BF16),
        scratch_shapes=[pltpu.VMEM((s, cb), F32), pltpu.VMEM((s, cb), F32),
                        pltpu.VMEM((2 * n1 * n2, cb), F32)],
        compiler_params=_cparams(("arbitrary", "arbitrary")),
        name="hyfilt",
    )(t_col, delta, feat, w3, w3, g_flt, f_fwd)


def _hyena_body(pz_ref, pg1_ref, pg2_ref, cwz_ref, cwg1_ref, cwg2_ref, cbz_ref, cbg1_ref, cbg2_ref,
                skip_ref, khat_ref, gsig_ref, ginv_ref, ffwd_ref, finv_ref,
                out_ref,
                za_ref, zb_ref, ya_ref, yb_ref, work_ref, *, length, rc):
    n2 = FFT_N2
    n1 = 2 * length // n2
    h1 = n1 // 2
    sub = SUBLANES

    def conv3(p_ref, b, r0, w_ref, bias_ref):
        p = p_ref[b, pl.ds(r0, rc), :]
        before = p_ref[b, pl.ds(pl.multiple_of(jnp.maximum(r0 - sub, 0), sub), sub), :][sub - 1:sub, :]
        after = p_ref[b, pl.ds(pl.multiple_of(jnp.minimum(r0 + rc, length - sub), sub), sub), :][0:1, :]
        before = jnp.where(r0 == 0, 0.0, before)
        after = jnp.where(r0 + rc == length, 0.0, after)
        row = lax.broadcasted_iota(jnp.int32, p.shape, 0)
        prev = jnp.where(row == 0, before, pltpu.roll(p, 1, 0))
        nxt = jnp.where(row == rc - 1, after, pltpu.roll(p, rc - 1, 0))
        return prev * w_ref[0:1, :] + p * w_ref[1:2, :] + nxt * w_ref[2:3, :] + bias_ref[...]

    def stage2(o):
        def body(k1, c):
            r0 = pl.multiple_of(k1 * (2 * n2), 2 * n2)
            blk = work_ref[pl.ds(r0, 2 * n2), :].astype(BF16)
            x = _dot(ffwd_ref[...], blk)
            kh = khat_ref[o, pl.ds(r0, 2 * n2), :].astype(F32)
            xr, xi = x[:n2], x[n2:]
            kr, ki = kh[:n2], kh[n2:]
            y = jnp.concatenate([xr * kr - xi * ki, xr * ki + xi * kr], axis=0).astype(BF16)
            work_ref[pl.ds(r0, 2 * n2), :] = _dot(finv_ref[...], y)
            return c
        lax.fori_loop(0, n1, body, 0)

    def stage3():
        def body(r, c):
            rhs = work_ref[pl.ds(r, 2 * n1, stride=n2), :].astype(BF16)
            res = _dot(ginv_ref[r], rhs)
            ya_ref[pl.ds(r, h1, stride=n2), :] = res[:h1]
            yb_ref[pl.ds(r, h1, stride=n2), :] = res[h1:]
            return c
        lax.fori_loop(0, n2, body, 0)

    def load_z(i, c):
        r0 = pl.multiple_of(i * rc, rc)
        za_ref[pl.ds(r0, rc), :] = conv3(pz_ref, 0, r0, cwz_ref, cbz_ref)
        zb_ref[pl.ds(r0, rc), :] = conv3(pz_ref, 1, r0, cwz_ref, cbz_ref)
        return c

    lax.fori_loop(0, length // rc, load_z, 0)
    for o, (pg_ref, cw_ref, cb_ref) in enumerate(((pg1_ref, cwg1_ref, cbg1_ref), (pg2_ref, cwg2_ref, cbg2_ref))):
        _fft_stage1(za_ref, zb_ref, gsig_ref, work_ref, n1, False)
        stage2(o)
        stage3()
        skip = skip_ref[o:o + 1, :]

        def gate(i, c):
            r0 = pl.multiple_of(i * rc, rc)
            rows = pl.ds(r0, rc)
            new_a = conv3(pg_ref, 0, r0, cw_ref, cb_ref) * (ya_ref[rows, :] + skip * za_ref[rows, :])
            new_b = conv3(pg_ref, 1, r0, cw_ref, cb_ref) * (yb_ref[rows, :] + skip * zb_ref[rows, :])
            if o == 0:
                za_ref[rows, :] = new_a
                zb_ref[rows, :] = new_b
            else:
                out_ref[0, rows, :] = new_a
                out_ref[1, rows, :] = new_b
            return c

        lax.fori_loop(0, length // rc, gate, 0)


def _hyena(p_hy, conv_w, conv_b, skip, khat, tables, rc=512):
    b, s, _ = p_hy.shape
    w = HY_WIDTH
    cb = HY_CB
    ncb = w // cb
    n2 = FFT_N2
    n1 = 2 * s // n2
    g_sig, _, g_inv, f_fwd, f_inv = tables
    one = pl.Buffered(1)

    def pspec(off):
        return pl.BlockSpec((2, s, cb), lambda c, p: (p, 0, off + c), pipeline_mode=one)

    def cwspec(off):
        return pl.BlockSpec((3, cb), lambda c, p: (0, off + c))

    def cbspec(off):
        return pl.BlockSpec((1, cb), lambda c, p: (0, off + c))

    def full(a):
        nd = a.ndim
        return pl.BlockSpec(a.shape, lambda c, p: (0,) * nd, pipeline_mode=one)

    in_specs = [
        pspec(0), pspec(ncb), pspec(2 * ncb),
        cwspec(0), cwspec(ncb), cwspec(2 * ncb),
        cbspec(0), cbspec(ncb), cbspec(2 * ncb),
        pl.BlockSpec((2, cb), lambda c, p: (0, c)),
        pl.BlockSpec((khat.shape[0], 2 * n1 * n2, cb), lambda c, p: (0, 0, c), pipeline_mode=one),
        full(g_sig), full(g_inv), full(f_fwd), full(f_inv),
    ]
    return pl.pallas_call(
        functools.partial(_hyena_body, length=s, rc=rc),
        grid=(ncb, b // 2),
        in_specs=in_specs,
        out_specs=pl.BlockSpec((2, s, cb), lambda c, p: (p, 0, c)),
        out_shape=jax.ShapeDtypeStruct((b, s, w), F32),
        scratch_shapes=[
            pltpu.VMEM((s, cb), F32), pltpu.VMEM((s, cb), F32),
            pltpu.VMEM((s, cb), F32), pltpu.VMEM((s, cb), F32),
            pltpu.VMEM((2 * n1 * n2, cb), F32),
        ],
        compiler_params=_cparams(("arbitrary", "arbitrary")),
        name="hyena",
    )(p_hy, p_hy, p_hy, conv_w, conv_w, conv_w, conv_b, conv_b, conv_b,
      skip, khat, g_sig, g_inv, f_fwd, f_inv)


def _attn_body(q_ref, k_ref, v_ref, o_ref, lse_ref, *, n_side, qb):
    lr = q_ref.shape[1]
    kw = qb + 2 * n_side
    lane_head = lax.broadcasted_iota(jnp.int32, (qb, ATTN_OUT), 1) // HEAD_DIM
    qi = lax.broadcasted_iota(jnp.int32, (qb, kw), 0)
    ki = lax.broadcasted_iota(jnp.int32, (qb, kw), 1)

    def body(i, c):
        q0 = pl.multiple_of(i * qb, qb)
        w0 = pl.multiple_of(jnp.clip(q0 - n_side, 0, lr - kw), n_side)
        q = q_ref[0, pl.ds(q0, qb), :]
        kwin = k_ref[0, pl.ds(w0, kw), :]
        vwin = v_ref[0, pl.ds(w0, kw), :]
        valid = jnp.abs((ki + w0) - (qi + q0)) <= n_side
        o = jnp.zeros((qb, ATTN_OUT), F32)
        lse = jnp.zeros((qb, ATTN_OUT), F32)
        for h in range(HEADS_PER_GROUP):
            mine = lane_head == h
            s = _dot_t(jnp.where(mine, q, jnp.zeros_like(q)), kwin)
            s = jnp.where(valid, s, -1e30)
            m = jnp.max(s, axis=-1, keepdims=True)
            p = jnp.exp(s - m)
            l = jnp.sum(p, axis=-1, keepdims=True)
            pv = _dot(p.astype(BF16), vwin)
            o = jnp.where(mine, pv / l, o)
            lse = jnp.where(mine, m + jnp.log(l), lse)
        o_ref[0, pl.ds(q0, qb), :] = o
        lse_ref[0, pl.ds(q0, qb), :] = lse
        return c

    lax.fori_loop(0, lr // qb, body, 0)


def _attn_group(q, k, v, g, dil, n_side, qb=128):
    b, s, aw = q.shape
    lr = s // dil
    ngc = aw // ATTN_OUT
    qv, kv, vv = (a.reshape(b, lr, dil * aw) for a in (q, k, v))
    in_spec = pl.BlockSpec((1, lr, ATTN_OUT), lambda bi, r: (bi, 0, r * ngc + g))
    out_spec = pl.BlockSpec((1, lr, ATTN_OUT), lambda bi, r: (bi, 0, r))
    o, lse = pl.pallas_call(
        functools.partial(_attn_body, n_side=n_side, qb=qb),
        grid=(b, dil),
        in_specs=[in_spec, in_spec, in_spec],
        out_specs=[out_spec, out_spec],
        out_shape=[jax.ShapeDtypeStruct((b, lr, dil * ATTN_OUT), F32)] * 2,
        compiler_params=_cparams(("arbitrary", "arbitrary")),
        name=f"attn{g}",
    )(qv, kv, vv)
    return o.reshape(b * s, ATTN_OUT), lse.reshape(b * s, ATTN_OUT)


def _memkv_body(mem_ref, g_ref, w_ref, kv_ref):
    mn = _rms(mem_ref[0], g_ref[...]).astype(BF16)
    kv_ref[0] = _dot(mn, w_ref[...]).astype(BF16)


def _memkv(mem, g, w_kv):
    b, m, d = mem.shape
    n = w_kv.shape[1]
    return pl.pallas_call(
        _memkv_body,
        grid=(b,),
        in_specs=[pl.BlockSpec((1, m, d), lambda i: (i, 0, 0)),
                  pl.BlockSpec((1, d), lambda i: (0, 0)),
                  pl.BlockSpec((d, n), lambda i: (0, 0))],
        out_specs=pl.BlockSpec((1, m, n), lambda i: (i, 0, 0)),
        out_shape=jax.ShapeDtypeStruct((b, m, n), BF16),
        compiler_params=_cparams(("arbitrary",)),
        name="memkv",
    )(mem, g, w_kv)


def _merge_body(x_ref, z_ref, o0_ref, o1_ref, o2_ref, l0_ref, l1_ref, l2_ref, gate_ref,
                wuh_ref, wua_ref, wout_ref, gx_ref, wq_ref, kv_ref, wo_ref, gm_ref, wr_ref,
                x2_ref, hm_ref, aff_ref):
    d = x_ref.shape[1]
    l0, l1, l2 = l0_ref[...], l1_ref[...], l2_ref[...]
    mx = jnp.maximum(jnp.maximum(l0, l1), l2)
    e0, e1, e2 = jnp.exp(l0 - mx), jnp.exp(l1 - mx), jnp.exp(l2 - mx)
    attn = (e0 * o0_ref[...] + e1 * o1_ref[...] + e2 * o2_ref[...]) / (e0 + e1 + e2)
    y_hy = _dot(z_ref[...].astype(BF16), wuh_ref[...])
    y_at = _dot(attn.astype(BF16), wua_ref[...])
    gates = gate_ref[...].astype(F32)
    mix = gates[:, :d] * y_hy + gates[:, d:] * y_at
    x1 = x_ref[...] + _dot(mix.astype(BF16), wout_ref[...])

    hx = _rms(x1, gx_ref[...]).astype(BF16)
    qx = _dot(hx, wq_ref[...]).astype(BF16)
    kv = kv_ref[0]
    xw = X_HEADS * X_HEAD_DIM
    heads = []
    for h in range(X_HEADS):
        sl = slice(h * X_HEAD_DIM, (h + 1) * X_HEAD_DIM)
        s = _dot_t(qx[:, sl], kv[:, sl]) * (X_HEAD_DIM ** -0.5)
        m = jnp.max(s, axis=-1, keepdims=True)
        p = jnp.exp(s - m)
        p = p / jnp.sum(p, axis=-1, keepdims=True)
        heads.append(_dot(p.astype(BF16), kv[:, xw + h * X_HEAD_DIM:xw + (h + 1) * X_HEAD_DIM]))
    ox = jnp.concatenate(heads, axis=-1).astype(BF16)
    x2 = x1 + _dot(ox, wo_ref[...])
    x2_ref[...] = x2

    hm = _rms(x2, gm_ref[...])
    hm_ref[...] = hm.astype(BF16)
    logits = _dot_hi(hm, wr_ref[...])
    lane = lax.broadcasted_iota(jnp.int32, logits.shape, 1)
    logits = jnp.where(lane < N_EXPERTS, logits, -1e30)
    m = jnp.max(logits, axis=-1, keepdims=True)
    p = jnp.exp(logits - m)
    aff_ref[...] = p / jnp.sum(p, axis=-1, keepdims=True)


def _merge(x2d, z2d, outs, lses, gates, wuh, wua, wout, gx, wq, kv, wo, gm, wr_pad, seq, tm=256):
    n, d = x2d.shape
    nseq = seq // tm
    row = lambda i: (i, 0)
    const = lambda i: (0, 0)

    def rspec(a):
        return pl.BlockSpec((tm, a.shape[1]), row)

    def cspec(a):
        return pl.BlockSpec(a.shape, const)

    in_arrays = [x2d, z2d, *outs, *lses, gates, wuh, wua, wout, gx, wq, kv, wo, gm, wr_pad]
    in_specs = ([rspec(a) for a in in_arrays[:9]] + [cspec(a) for a in in_arrays[9:14]]
                + [pl.BlockSpec((1,) + kv.shape[1:], lambda i: (i // nseq, 0, 0))]
                + [cspec(a) for a in in_arrays[15:]])
    return pl.pallas_call(
        _merge_body,
        grid=(n // tm,),
        in_specs=in_specs,
        out_specs=[pl.BlockSpec((tm, d), row), pl.BlockSpec((tm, d), row), pl.BlockSpec((tm, LANES), row)],
        out_shape=[jax.ShapeDtypeStruct((n, d), F32), jax.ShapeDtypeStruct((n, d), BF16),
                   jax.ShapeDtypeStruct((n, LANES), F32)],
        compiler_params=_cparams(("arbitrary",)),
        name="merge",
    )(*in_arrays)


def _topk_body(aff_ref, tri_ref, rank_ref, *, cap):
    a = aff_ref[0]
    e, s = a.shape

    def count(mask):
        return jnp.sum(jnp.where(mask, 1.0, 0.0), axis=-1, keepdims=True)

    def as_float(bits):
        return pltpu.bitcast(jnp.broadcast_to(bits, (e, LANES)), F32)[:, 0:1]

    def search(i, thr):
        cand = thr | (jnp.int32(1) << (30 - i))
        return jnp.where(count(a >= as_float(cand)) >= cap, cand, thr)

    thr = as_float(lax.fori_loop(0, 31, search, jnp.zeros((e, 1), jnp.int32)))
    gt = a > thr
    eq = a == thr
    need = cap - count(gt)

    def prefix_excl(mask):
        mf = jnp.where(mask, 1.0, 0.0)
        parts = []
        carry = jnp.zeros((e, 1), F32)
        for c in range(s // LANES):
            blk = mf[:, c * LANES:(c + 1) * LANES]
            inc = _dot(blk.astype(BF16), tri_ref[...])
            parts.append(inc - blk + carry)
            carry = carry + inc[:, LANES - 1:LANES]
        return jnp.concatenate(parts, axis=-1)

    sel = gt | (eq & (prefix_excl(eq) < need))
    rank_ref[0] = jnp.where(sel, prefix_excl(sel), -1.0)


def _topk(aff_t, cap):
    b, e, s = aff_t.shape
    tri = jnp.asarray(np.triu(np.ones((LANES, LANES))), dtype=BF16)
    return pl.pallas_call(
        functools.partial(_topk_body, cap=cap),
        grid=(b,),
        in_specs=[pl.BlockSpec((1, e, s), lambda i: (i, 0, 0)),
                  pl.BlockSpec((LANES, LANES), lambda i: (0, 0))],
        out_specs=pl.BlockSpec((1, e, s), lambda i: (i, 0, 0)),
        out_shape=jax.ShapeDtypeStruct((b, e, s), F32),
        compiler_params=_cparams(("arbitrary",)),
        name="topk",
    )(aff_t, tri)


def _gather_body(rank_ref, hm_ref, xe_ref, *, cap, tc):
    s = hm_ref.shape[1]
    slot = lax.broadcasted_iota(jnp.int32, (cap, tc), 0).astype(F32)
    acc = jnp.zeros((cap, hm_ref.shape[2]), F32)
    for c in range(s // tc):
        r = rank_ref[0, 0, :, c * tc:(c + 1) * tc]
        onehot = jnp.where(slot == r, 1.0, 0.0).astype(BF16)
        acc = acc + _dot(onehot, hm_ref[0, c * tc:(c + 1) * tc, :])
    xe_ref[0, 0] = acc.astype(BF16)


def _gather(rank, hm, cap, tc=1024):
    b, e, s = rank.shape
    d = hm.shape[2]
    return pl.pallas_call(
        functools.partial(_gather_body, cap=cap, tc=tc),
        grid=(b, e),
        in_specs=[pl.BlockSpec((1, 1, 1, s), lambda bi, ei: (bi, ei, 0, 0)),
                  pl.BlockSpec((1, s, d), lambda bi, ei: (bi, 0, 0))],
        out_specs=pl.BlockSpec((1, 1, cap, d), lambda bi, ei: (bi, ei, 0, 0)),
        out_shape=jax.ShapeDtypeStruct((b, e, cap, d), BF16),
        compiler_params=_cparams(("arbitrary", "arbitrary")),
        name="gather",
    )(rank.reshape(b, e, 1, s), hm)


def _ffn_body(xe_ref, wg_ref, wu_ref, wd_ref, ye_ref, acc_ref):
    f = pl.program_id(1)
    nb = xe_ref.shape[0]
    wg = wg_ref[0].astype(BF16)
    wu = wu_ref[0].astype(BF16)
    wd = wd_ref[0].astype(BF16)
    for b in range(nb):
        xe = xe_ref[b, 0]
        a = _dot(xe, wg)
        u = _dot(xe, wu)
        hsw = (a * jax.nn.sigmoid(a) * u).astype(BF16)
        part = _dot(hsw, wd)

        @pl.when(f == 0)
        def _():
            acc_ref[b] = part

        @pl.when(f != 0)
        def _():
            acc_ref[b] += part

    @pl.when(f == pl.num_programs(1) - 1)
    def _():
        for b in range(nb):
            ye_ref[b, 0] = acc_ref[b].astype(BF16)


def _ffn(xe, wg, wu, wd, ft=512):
    b, e, cap, d = xe.shape
    ff = wg.shape[2]
    return pl.pallas_call(
        _ffn_body,
        grid=(e, ff // ft),
        in_specs=[pl.BlockSpec((b, 1, cap, d), lambda ei, fi: (0, ei, 0, 0)),
                  pl.BlockSpec((1, d, ft), lambda ei, fi: (ei, 0, fi)),
                  pl.BlockSpec((1, d, ft), lambda ei, fi: (ei, 0, fi)),
                  pl.BlockSpec((1, ft, d), lambda ei, fi: (ei, fi, 0))],
        out_specs=pl.BlockSpec((b, 1, cap, d), lambda ei, fi: (0, ei, 0, 0)),
        out_shape=jax.ShapeDtypeStruct((b, e, cap, d), BF16),
        scratch_shapes=[pltpu.VMEM((b, cap, d), F32)],
        compiler_params=_cparams(("arbitrary", "arbitrary")),
        name="ffn",
    )(xe, wg, wu, wd)


def _scatter_body(rank_ref, aff_ref, ye_ref, x2_ref, g_ref, out_ref, *, cap):
    ts = x2_ref.shape[1]
    slot = lax.broadcasted_iota(jnp.int32, (ts, cap), 1).astype(F32)
    rank = rank_ref[0]
    aff = aff_ref[0]
    acc = x2_ref[0]
    for e in range(ye_ref.shape[1]):
        onehot = jnp.where(slot == rank[:, e:e + 1], 1.0, 0.0).astype(BF16)
        acc = acc + aff[:, e:e + 1] * _dot(onehot, ye_ref[0, e])
    out_ref[0] = _rms(acc, g_ref[...])


def _scatter(rank_t, aff, ye, x2, g, cap, ts=512):
    b, s, e = rank_t.shape
    d = x2.shape[2]
    return pl.pallas_call(
        functools.partial(_scatter_body, cap=cap),
        grid=(b, s // ts),
        in_specs=[pl.BlockSpec((1, ts, e), lambda bi, ti: (bi, ti, 0)),
                  pl.BlockSpec((1, ts, LANES), lambda bi, ti: (bi, ti, 0)),
                  pl.BlockSpec((1, e, cap, d), lambda bi, ti: (bi, 0, 0, 0), pipeline_mode=pl.Buffered(1)),
                  pl.BlockSpec((1, ts, d), lambda bi, ti: (bi, ti, 0)),
                  pl.BlockSpec((1, d), lambda bi, ti: (0, 0))],
        out_specs=pl.BlockSpec((1, ts, d), lambda bi, ti: (bi, ti, 0)),
        out_shape=jax.ShapeDtypeStruct((b, s, d), F32),
        compiler_params=_cparams(("arbitrary", "arbitrary")),
        name="scatter",
    )(rank_t, aff, ye, x2, g)


def kernel(x, mem, ln_mix_g, w_in, hy_conv_w, hy_conv_b, filt_w1, filt_b1, filt_freq1, filt_w2, filt_b2,
           filt_freq2, filt_w3, hy_skip, w_up_hy, w_up_attn, w_gate, b_gate, w_out, ln_x_g, ln_mem_g,
           w_q_x, w_kv_mem, w_o_x, ln_moe_g, w_router, w_e_gate, w_e_up, w_e_down, ln_f_g):
    b, s, d = x.shape
    n = b * s
    x2d = x.reshape(n, d)

    half = HEAD_DIM // 2
    inv = ROPE_THETA ** (-jnp.arange(0, HEAD_DIM, 2, dtype=F32) / HEAD_DIM)
    ang = jnp.arange(s, dtype=F32)[:, None] * inv[None, :]
    lane = np.arange(LANES)
    cos_t = jnp.cos(ang)[:, lane % half]
    sin_t = jnp.sin(ang)[:, lane % half] * jnp.asarray(np.where(lane % HEAD_DIM < half, -1.0, 1.0), F32)[None, :]

    w_all = jnp.concatenate([w_in, w_gate], axis=1).astype(BF16)
    p_hy, q, k, v, gates = _proj(x2d, ln_mix_g[None], w_all, b_gate[None], cos_t, sin_t, s)

    t_col = jnp.linspace(0.0, 1.0, s, dtype=F32)[:, None]
    grid_col = 2.0 * math.pi * jnp.arange(s, dtype=F32)[:, None] / s
    bands = jnp.linspace(1e-4, HY_BANDS - 1, HY_BANDS, dtype=F32)[None, :]
    feat = _filt(t_col, grid_col, bands, filt_w1, filt_b1, filt_freq1, filt_w2, filt_b2, filt_freq2)
    delta = jnp.linspace(math.log(HY_TARGET) / HY_SLOW_PCT, math.log(HY_TARGET) / HY_FAST_PCT,
                         HY_WIDTH, dtype=F32)[None, :]
    tables = _fft_tables(s)
    khat = _hyfilt(t_col, delta, feat, filt_w3, tables[1], tables[3])
    z = _hyena(p_hy.reshape(b, s, -1), hy_conv_w, hy_conv_b[None], hy_skip, khat, tables)

    q3, k3, v3 = (a.reshape(b, s, ATTN_WIDTH) for a in (q, k, v))
    outs, lses = [], []
    for g, (window, dil) in enumerate(DIL_PAIRS):
        o_g, l_g = _attn_group(q3, k3, v3, g, dil, window // (2 * dil))
        outs.append(o_g)
        lses.append(l_g)

    kv = _memkv(mem, ln_mem_g[None], w_kv_mem.astype(BF16))
    wr_pad = jnp.pad(w_router, ((0, 0), (0, LANES - N_EXPERTS)))
    x2, hm, aff = _merge(x2d, z.reshape(n, HY_WIDTH), outs, lses, gates,
                         w_up_hy.astype(BF16), w_up_attn.astype(BF16), w_out.astype(BF16),
                         ln_x_g[None], w_q_x.astype(BF16), kv, w_o_x.astype(BF16), ln_moe_g[None], wr_pad, s)

    cap = max(1, EC_FACTOR * s // N_EXPERTS)
    aff3 = aff.reshape(b, s, LANES)
    rank = _topk(aff3[:, :, :N_EXPERTS].transpose(0, 2, 1), cap)
    xe = _gather(rank, hm.reshape(b, s, d), cap)
    ye = _ffn(xe, w_e_gate, w_e_up, w_e_down)
    return _scatter(rank.transpose(0, 2, 1), aff3, ye, x2.reshape(b, s, d), ln_f_g[None], cap)
```

```python
import functools
import math

import numpy as np
import jax
import jax.numpy as jnp
from jax import lax
from jax.experimental import pallas as pl
from jax.experimental.pallas import tpu as pltpu

F32 = jnp.float32
BF16 = jnp.bfloat16
HIGHEST = lax.Precision.HIGHEST

EPS = 1e-6
HY_WIDTH = 768
HY_BANDS = 16
HY_FFN = 64
HY_FAST_PCT = 0.3
HY_SLOW_PCT = 1.5
HY_TARGET = 1e-2
HEAD_DIM = 64
HEADS_PER_GROUP = 4
DIL_PAIRS = ((128, 1), (512, 4), (2048, 16))
N_GROUPS = len(DIL_PAIRS)
ATTN_WIDTH = N_GROUPS * HEADS_PER_GROUP * HEAD_DIM
ATTN_OUT = HEADS_PER_GROUP * HEAD_DIM
ROPE_THETA = 10000.0
X_HEADS = 4
X_HEAD_DIM = 128
N_EXPERTS = 16
EC_FACTOR = 2

LANES = 128
SUBLANES = 8
VMEM_LIMIT = 56 * 1024 * 1024

FFT_N2 = 128
HY_CB = 128
FFT_UNROLL = 8


def _cparams(sem, vmem=VMEM_LIMIT):
    return pltpu.CompilerParams(dimension_semantics=sem, vmem_limit_bytes=vmem)


def _rms(x, g):
    return x * lax.rsqrt(jnp.mean(x * x, axis=-1, keepdims=True) + EPS) * g


def _dot(a, b):
    return jnp.dot(a, b, preferred_element_type=F32)


def _dot_hi(a, b):
    return jnp.dot(a, b, precision=HIGHEST, preferred_element_type=F32)


def _dot_t(a, b):
    return lax.dot_general(a, b, (((1,), (1,)), ((), ())), preferred_element_type=F32)


def _proj_body(x_ref, g_ref, w_ref, bg_ref, cos_ref, sin_ref,
               phy_ref, q_ref, k_ref, v_ref, gate_ref, *, hyc, aw):
    h = _rms(x_ref[...], g_ref[...]).astype(BF16)
    phy_ref[...] = _dot(h, w_ref[:, :hyc])
    cos = cos_ref[...]
    sin = sin_ref[...]
    tm = cos.shape[0]
    lane = lax.broadcasted_iota(jnp.int32, (tm, LANES), 1)
    first = (lane % HEAD_DIM) < (HEAD_DIM // 2)

    def rope_store(col0, out_ref, scale):
        t = _dot(h, w_ref[:, col0:col0 + aw])
        for j in range(aw // LANES):
            tj = t[:, j * LANES:(j + 1) * LANES]
            partner = jnp.where(first, pltpu.roll(tj, LANES - HEAD_DIM // 2, 1),
                                pltpu.roll(tj, HEAD_DIM // 2, 1))
            out_ref[:, j * LANES:(j + 1) * LANES] = ((tj * cos + partner * sin) * scale).astype(BF16)

    rope_store(hyc, q_ref, HEAD_DIM ** -0.5)
    rope_store(hyc + aw, k_ref, 1.0)
    v_ref[...] = _dot(h, w_ref[:, hyc + 2 * aw:hyc + 3 * aw]).astype(BF16)
    gate_ref[...] = jax.nn.sigmoid(_dot(h, w_ref[:, hyc + 3 * aw:]) + bg_ref[...]).astype(BF16)


def _proj(x2d, g, w_all, b_gate, cos_t, sin_t, seq, tm=256):
    n, d = x2d.shape
    hyc = 3 * HY_WIDTH
    aw = ATTN_WIDTH
    gd = w_all.shape[1] - hyc - 3 * aw
    nseq = seq // tm
    row = lambda i: (i, 0)
    const = lambda i: (0, 0)
    return pl.pallas_call(
        functools.partial(_proj_body, hyc=hyc, aw=aw),
        grid=(n // tm,),
        in_specs=[
            pl.BlockSpec((tm, d), row),
            pl.BlockSpec((1, d), const),
            pl.BlockSpec(w_all.shape, const, pipeline_mode=pl.Buffered(1)),
            pl.BlockSpec((1, gd), const),
            pl.BlockSpec((tm, LANES), lambda i: (i % nseq, 0)),
            pl.BlockSpec((tm, LANES), lambda i: (i % nseq, 0)),
        ],
        out_specs=[
            pl.BlockSpec((tm, hyc), row),
            pl.BlockSpec((tm, aw), row),
            pl.BlockSpec((tm, aw), row),
            pl.BlockSpec((tm, aw), row),
            pl.BlockSpec((tm, gd), row),
        ],
        out_shape=[
            jax.ShapeDtypeStruct((n, hyc), F32),
            jax.ShapeDtypeStruct((n, aw), BF16),
            jax.ShapeDtypeStruct((n, aw), BF16),
            jax.ShapeDtypeStruct((n, aw), BF16),
            jax.ShapeDtypeStruct((n, gd), BF16),
        ],
        compiler_params=_cparams(("arbitrary",)),
        name="proj",
    )(x2d, g, w_all, b_gate, cos_t, sin_t)


def _filt_body(t_ref, grid_ref, bands_ref, w1t_ref, w1c_ref, w1s_ref, b1_ref, f1_ref,
               w2_ref, b2_ref, f2_ref, feat_ref):
    ang = bands_ref[...] * grid_ref[...]
    pre = (t_ref[...] * w1t_ref[...] + _dot_hi(jnp.cos(ang), w1c_ref[...])
           + _dot_hi(-jnp.sin(ang), w1s_ref[...]))
    h = jnp.sin(f1_ref[...] * (pre + b1_ref[...]))
    feat_ref[...] = jnp.sin(f2_ref[...] * (_dot_hi(h, w2_ref[...]) + b2_ref[...]))


def _filt(t_col, grid_col, bands, w1, b1, f1, w2, b2, f2):
    length = t_col.shape[0]
    nb = bands.shape[1]
    args = (t_col, grid_col, bands, w1[0:1], w1[1:1 + nb], w1[1 + nb:], b1[None], f1[None],
            w2, b2[None], f2[None])
    return pl.pallas_call(
        _filt_body,
        out_shape=jax.ShapeDtypeStruct((length, HY_FFN), F32),
        compiler_params=_cparams(None),
        name="filt",
    )(*args)


def _fft_tables(length):
    n = 2 * length
    n2 = FFT_N2
    n1 = n // n2
    h1 = n1 // 2
    k1 = np.arange(n1)[:, None]
    g_sig = np.zeros((n2, 2 * n1, 2 * h1))
    g_flt = np.zeros((n2, 2 * n1, 2 * h1))
    g_inv = np.zeros((n2, 2 * h1, 2 * n1))
    j = np.arange(h1)[None, :]
    for r in range(n2):
        th = 2 * np.pi * (k1 * (n2 * j + r) % n) / n
        gr, gi = np.cos(th), -np.sin(th)
        g_sig[r, 0::2, :h1] = gr
        g_sig[r, 0::2, h1:] = -gi
        g_sig[r, 1::2, :h1] = gi
        g_sig[r, 1::2, h1:] = gr
        g_flt[r, 0::2, :h1] = gr
        g_flt[r, 1::2, :h1] = gi
        off = (n2 - r) % n2
        m = off + n2 * j
        thb = 2 * np.pi * (k1 * m % n) / n
        live = (m >= 1).astype(np.float64)
        g_flt[r, 0::2, h1:] = np.cos(thb) * live
        g_flt[r, 1::2, h1:] = np.sin(thb) * live
        wr, wi = (np.cos(th) / n).T, (np.sin(th) / n).T
        g_inv[r, :h1, 0::2] = wr
        g_inv[r, :h1, 1::2] = -wi
        g_inv[r, h1:, 0::2] = wi
        g_inv[r, h1:, 1::2] = wr
    a = np.arange(n2)
    th2 = 2 * np.pi * (np.outer(a, a) % n2) / n2
    c2, s2 = np.cos(th2), np.sin(th2)
    f_fwd = np.block([[c2, s2], [-s2, c2]])
    f_inv = np.block([[c2, -s2], [s2, c2]])
    cast = lambda z: jnp.asarray(z, dtype=F32).astype(BF16)
    return cast(g_sig), cast(g_flt), cast(g_inv), cast(f_fwd), cast(f_inv)


def _fft_stage1(src_a_ref, src_b_ref, tab_ref, work_ref, n1, is_filter):
    n2 = FFT_N2
    h1 = n1 // 2

    def body(r, c):
        offb = jnp.where(r == 0, 0, n2 - r) if is_filter else r
        ra = src_a_ref[pl.ds(r, h1, stride=n2), :]
        rb = src_b_ref[pl.ds(offb, h1, stride=n2), :]
        rhs = jnp.concatenate([ra, rb], axis=0).astype(BF16)
        work_ref[pl.ds(r, 2 * n1, stride=n2), :] = _dot(tab_ref[r], rhs)
        return c

    lax.fori_loop(0, n2, body, 0, unroll=FFT_UNROLL)


def _hyfilt_body(t_ref, delta_ref, feat_ref, w3f_ref, w3b_ref, gflt_ref, ffwd_ref, khat_ref,
                 hf_ref, hb_ref, work_ref, *, length, rc):
    n2 = FFT_N2
    n1 = 2 * length // n2

    def gen(i, c):
        r0 = pl.multiple_of(i * rc, rc)
        decay = jnp.exp(-t_ref[pl.ds(r0, rc), :] * jnp.abs(delta_ref[...]))
        feat = feat_ref[pl.ds(r0, rc), :]
        hf_ref[pl.ds(r0, rc), :] = _dot_hi(feat, w3f_ref[...]) * decay
        hb_ref[pl.ds(r0, rc), :] = _dot_hi(feat, w3b_ref[...]) * decay
        return c

    lax.fori_loop(0, length // rc, gen, 0)
    _fft_stage1(hf_ref, hb_ref, gflt_ref, work_ref, n1, True)

    def stage2(k1, c):
        r0 = pl.multiple_of(k1 * (2 * n2), 2 * n2)
        blk = work_ref[pl.ds(r0, 2 * n2), :].astype(BF16)
        khat_ref[0, pl.ds(r0, 2 * n2), :] = _dot(ffwd_ref[...], blk).astype(BF16)
        return c

    lax.fori_loop(0, n1, stage2, 0, unroll=FFT_UNROLL)


def _hyfilt(t_col, delta, feat, w3, g_flt, f_fwd, rc=512):
    s = t_col.shape[0]
    cb = HY_CB
    ncb = HY_WIDTH // cb
    n2 = FFT_N2
    n1 = 2 * s // n2
    norder = w3.shape[1] // (2 * HY_WIDTH)
    one = pl.Buffered(1)

    def full(a):
        nd = a.ndim
        return pl.BlockSpec(a.shape, lambda o, c: (0,) * nd, pipeline_mode=one)

    return pl.pallas_call(
        functools.partial(_hyfilt_body, length=s, rc=rc),
        grid=(norder, ncb),
        in_specs=[full(t_col),
                  pl.BlockSpec((1, cb), lambda o, c: (0, c)),
                  full(feat),
                  pl.BlockSpec((HY_FFN, cb), lambda o, c: (0, 2 * ncb * o + c)),
                  pl.BlockSpec((HY_FFN, cb), lambda o, c: (0, 2 * ncb * o + ncb + c)),
                  full(g_flt), full(f_fwd)],
        out_specs=pl.BlockSpec((1, 2 * n1 * n2, cb), lambda o, c: (o, 0, c)),
        out_shape=jax.ShapeDtypeStruct((norder, 2 * n1 * n2, HY_WIDTH), BF16),
        scratch_shapes=[pltpu.VMEM((s, cb), F32), pltpu.VMEM((s, cb), F32),
                        pltpu.VMEM((2 * n1 * n2, cb), F32)],
        compiler_params=_cparams(("arbitrary", "arbitrary")),
        name="hyfilt",
    )(t_col, delta, feat, w3, w3, g_flt, f_fwd)


def _hyena_body(pz_ref, pg1_ref, pg2_ref, cwz_ref, cwg1_ref, cwg2_ref, cbz_ref, cbg1_ref, cbg2_ref,
                skip_ref, khat_ref, gsig_ref, ginv_ref, ffwd_ref, finv_ref,
                out_ref,
                za_ref, zb_ref, ya_ref, yb_ref, work_ref, *, length, rc):
    n2 = FFT_N2
    n1 = 2 * length // n2
    h1 = n1 // 2
    sub = SUBLANES

    def conv3(p_ref, b, r0, w_ref, bias_ref):
        p = p_ref[b, pl.ds(r0, rc), :]
        before = p_ref[b, pl.ds(pl.multiple_of(jnp.maximum(r0 - sub, 0), sub), sub), :][sub - 1:sub, :]
        after = p_ref[b, pl.ds(pl.multiple_of(jnp.minimum(r0 + rc, length - sub), sub), sub), :][0:1, :]
        before = jnp.where(r0 == 0, 0.0, before)
        after = jnp.where(r0 + rc == length, 0.0, after)
        row = lax.broadcasted_iota(jnp.int32, p.shape, 0)
        prev = jnp.where(row == 0, before, pltpu.roll(p, 1, 0))
        nxt = jnp.where(row == rc - 1, after, pltpu.roll(p, rc - 1, 0))
        return prev * w_ref[0:1, :] + p * w_ref[1:2, :] + nxt * w_ref[2:3, :] + bias_ref[...]

    def stage2(o):
        def body(i, c):
            starts = [pl.multiple_of((i * FFT_UNROLL + u) * (2 * n2), 2 * n2) for u in range(FFT_UNROLL)]
            blks = [work_ref[pl.ds(r0, 2 * n2), :].astype(BF16) for r0 in starts]
            outs = []
            for r0, blk in zip(starts, blks):
                x = _dot(ffwd_ref[...], blk)
                kh = khat_ref[o, pl.ds(r0, 2 * n2), :].astype(F32)
                xr, xi = x[:n2], x[n2:]
                kr, ki = kh[:n2], kh[n2:]
                y = jnp.concatenate([xr * kr - xi * ki, xr * ki + xi * kr], axis=0).astype(BF16)
                outs.append(_dot(finv_ref[...], y))
            for r0, out in zip(starts, outs):
                work_ref[pl.ds(r0, 2 * n2), :] = out
            return c
        lax.fori_loop(0, n1 // FFT_UNROLL, body, 0)

    def stage3():
        def body(r, c):
            rhs = work_ref[pl.ds(r, 2 * n1, stride=n2), :].astype(BF16)
            res = _dot(ginv_ref[r], rhs)
            ya_ref[pl.ds(r, h1, stride=n2), :] = res[:h1]
            yb_ref[pl.ds(r, h1, stride=n2), :] = res[h1:]
            return c
        lax.fori_loop(0, n2, body, 0, unroll=FFT_UNROLL)

    def load_z(i, c):
        r0 = pl.multiple_of(i * rc, rc)
        za_ref[pl.ds(r0, rc), :] = conv3(pz_ref, 0, r0, cwz_ref, cbz_ref)
        zb_ref[pl.ds(r0, rc), :] = conv3(pz_ref, 1, r0, cwz_ref, cbz_ref)
        return c

    lax.fori_loop(0, length // rc, load_z, 0)
    for o, (pg_ref, cw_ref, cb_ref) in enumerate(((pg1_ref, cwg1_ref, cbg1_ref), (pg2_ref, cwg2_ref, cbg2_ref))):
        _fft_stage1(za_ref, zb_ref, gsig_ref, work_ref, n1, False)
        stage2(o)
        stage3()
        skip = skip_ref[o:o + 1, :]

        def gate(i, c):
            r0 = pl.multiple_of(i * rc, rc)
            rows = pl.ds(r0, rc)
            new_a = conv3(pg_ref, 0, r0, cw_ref, cb_ref) * (ya_ref[rows, :] + skip * za_ref[rows, :])
            new_b = conv3(pg_ref, 1, r0, cw_ref, cb_ref) * (yb_ref[rows, :] + skip * zb_ref[rows, :])
            if o == 0:
                za_ref[rows, :] = new_a
                zb_ref[rows, :] = new_b
            else:
                out_ref[0, rows, :] = new_a
                out_ref[1, rows, :] = new_b
            return c

        lax.fori_loop(0, length // rc, gate, 0)


def _hyena(p_hy, conv_w, conv_b, skip, khat, tables, rc=512):
    b, s, _ = p_hy.shape
    w = HY_WIDTH
    cb = HY_CB
    ncb = w // cb
    n2 = FFT_N2
    n1 = 2 * s // n2
    g_sig, _, g_inv, f_fwd, f_inv = tables
    one = pl.Buffered(1)

    def pspec(off):
        return pl.BlockSpec((2, s, cb), lambda c, p: (p, 0, off + c), pipeline_mode=one)

    def cwspec(off):
        return pl.BlockSpec((3, cb), lambda c, p: (0, off + c))

    def cbspec(off):
        return pl.BlockSpec((1, cb), lambda c, p: (0, off + c))

    def full(a):
        nd = a.ndim
        return pl.BlockSpec(a.shape, lambda c, p: (0,) * nd, pipeline_mode=one)

    in_specs = [
        pspec(0), pspec(ncb), pspec(2 * ncb),
        cwspec(0), cwspec(ncb), cwspec(2 * ncb),
        cbspec(0), cbspec(ncb), cbspec(2 * ncb),
        pl.BlockSpec((2, cb), lambda c, p: (0, c)),
        pl.BlockSpec((khat.shape[0], 2 * n1 * n2, cb), lambda c, p: (0, 0, c), pipeline_mode=one),
        full(g_sig), full(g_inv), full(f_fwd), full(f_inv),
    ]
    return pl.pallas_call(
        functools.partial(_hyena_body, length=s, rc=rc),
        grid=(ncb, b // 2),
        in_specs=in_specs,
        out_specs=pl.BlockSpec((2, s, cb), lambda c, p: (p, 0, c)),
        out_shape=jax.ShapeDtypeStruct((b, s, w), F32),
        scratch_shapes=[
            pltpu.VMEM((s, cb), F32), pltpu.VMEM((s, cb), F32),
            pltpu.VMEM((s, cb), F32), pltpu.VMEM((s, cb), F32),
            pltpu.VMEM((2 * n1 * n2, cb), F32),
        ],
        compiler_params=_cparams(("arbitrary", "arbitrary")),
        name="hyena",
    )(p_hy, p_hy, p_hy, conv_w, conv_w, conv_w, conv_b, conv_b, conv_b,
      skip, khat, g_sig, g_inv, f_fwd, f_inv)


def _attn_body(q_ref, k_ref, v_ref, o_ref, lse_ref, *, n_side, qb):
    lr = q_ref.shape[1]
    kw = qb + 2 * n_side
    lane_head = lax.broadcasted_iota(jnp.int32, (qb, ATTN_OUT), 1) // HEAD_DIM
    qi = lax.broadcasted_iota(jnp.int32, (qb, kw), 0)
    ki = lax.broadcasted_iota(jnp.int32, (qb, kw), 1)

    def body(i, c):
        q0 = pl.multiple_of(i * qb, qb)
        w0 = pl.multiple_of(jnp.clip(q0 - n_side, 0, lr - kw), n_side)
        q = q_ref[0, pl.ds(q0, qb), :]
        kwin = k_ref[0, pl.ds(w0, kw), :]
        vwin = v_ref[0, pl.ds(w0, kw), :]
        valid = jnp.abs((ki + w0) - (qi + q0)) <= n_side
        o = jnp.zeros((qb, ATTN_OUT), F32)
        lse = jnp.zeros((qb, ATTN_OUT), F32)
        for h in range(HEADS_PER_GROUP):
            mine = lane_head == h
            s = _dot_t(jnp.where(mine, q, jnp.zeros_like(q)), kwin)
            s = jnp.where(valid, s, -1e30)
            m = jnp.max(s, axis=-1, keepdims=True)
            p = jnp.exp(s - m)
            l = jnp.sum(p, axis=-1, keepdims=True)
            pv = _dot(p.astype(BF16), vwin)
            o = jnp.where(mine, pv / l, o)
            lse = jnp.where(mine, m + jnp.log(l), lse)
        o_ref[0, pl.ds(q0, qb), :] = o
        lse_ref[0, pl.ds(q0, qb), :] = lse
        return c

    lax.fori_loop(0, lr // qb, body, 0)


def _attn_group(q, k, v, g, dil, n_side, qb=128):
    b, s, aw = q.shape
    lr = s // dil
    ngc = aw // ATTN_OUT
    qv, kv, vv = (a.reshape(b, lr, dil * aw) for a in (q, k, v))
    in_spec = pl.BlockSpec((1, lr, ATTN_OUT), lambda bi, r: (bi, 0, r * ngc + g))
    out_spec = pl.BlockSpec((1, lr, ATTN_OUT), lambda bi, r: (bi, 0, r))
    o, lse = pl.pallas_call(
        functools.partial(_attn_body, n_side=n_side, qb=qb),
        grid=(b, dil),
        in_specs=[in_spec, in_spec, in_spec],
        out_specs=[out_spec, out_spec],
        out_shape=[jax.ShapeDtypeStruct((b, lr, dil * ATTN_OUT), F32)] * 2,
        compiler_params=_cparams(("arbitrary", "arbitrary")),
        name=f"attn{g}",
    )(qv, kv, vv)
    return o.reshape(b * s, ATTN_OUT), lse.reshape(b * s, ATTN_OUT)


def _memkv_body(mem_ref, g_ref, w_ref, kv_ref):
    mn = _rms(mem_ref[0], g_ref[...]).astype(BF16)
    kv_ref[0] = _dot(mn, w_ref[...]).astype(BF16)


def _memkv(mem, g, w_kv):
    b, m, d = mem.shape
    n = w_kv.shape[1]
    return pl.pallas_call(
        _memkv_body,
        grid=(b,),
        in_specs=[pl.BlockSpec((1, m, d), lambda i: (i, 0, 0)),
                  pl.BlockSpec((1, d), lambda i: (0, 0)),
                  pl.BlockSpec((d, n), lambda i: (0, 0))],
        out_specs=pl.BlockSpec((1, m, n), lambda i: (i, 0, 0)),
        out_shape=jax.ShapeDtypeStruct((b, m, n), BF16),
        compiler_params=_cparams(("arbitrary",)),
        name="memkv",
    )(mem, g, w_kv)


def _merge_body(x_ref, z_ref, o0_ref, o1_ref, o2_ref, l0_ref, l1_ref, l2_ref, gate_ref,
                wuh_ref, wua_ref, wout_ref, gx_ref, wq_ref, kv_ref, wo_ref, gm_ref, wr_ref,
                x2_ref, hm_ref, aff_ref):
    d = x_ref.shape[1]
    l0, l1, l2 = l0_ref[...], l1_ref[...], l2_ref[...]
    mx = jnp.maximum(jnp.maximum(l0, l1), l2)
    e0, e1, e2 = jnp.exp(l0 - mx), jnp.exp(l1 - mx), jnp.exp(l2 - mx)
    attn = (e0 * o0_ref[...] + e1 * o1_ref[...] + e2 * o2_ref[...]) / (e0 + e1 + e2)
    y_hy = _dot(z_ref[...].astype(BF16), wuh_ref[...])
    y_at = _dot(attn.astype(BF16), wua_ref[...])
    gates = gate_ref[...].astype(F32)
    mix = gates[:, :d] * y_hy + gates[:, d:] * y_at
    x1 = x_ref[...] + _dot(mix.astype(BF16), wout_ref[...])

    hx = _rms(x1, gx_ref[...]).astype(BF16)
    qx = _dot(hx, wq_ref[...]).astype(BF16)
    kv = kv_ref[0]
    xw = X_HEADS * X_HEAD_DIM
    heads = []
    for h in range(X_HEADS):
        sl = slice(h * X_HEAD_DIM, (h + 1) * X_HEAD_DIM)
        s = _dot_t(qx[:, sl], kv[:, sl]) * (X_HEAD_DIM ** -0.5)
        m = jnp.max(s, axis=-1, keepdims=True)
        p = jnp.exp(s - m)
        p = p / jnp.sum(p, axis=-1, keepdims=True)
        heads.append(_dot(p.astype(BF16), kv[:, xw + h * X_HEAD_DIM:xw + (h + 1) * X_HEAD_DIM]))
    ox = jnp.concatenate(heads, axis=-1).astype(BF16)
    x2 = x1 + _dot(ox, wo_ref[...])
    x2_ref[...] = x2

    hm = _rms(x2, gm_ref[...])
    hm_ref[...] = hm.astype(BF16)
    logits = _dot_hi(hm, wr_ref[...])
    lane = lax.broadcasted_iota(jnp.int32, logits.shape, 1)
    logits = jnp.where(lane < N_EXPERTS, logits, -1e30)
    m = jnp.max(logits, axis=-1, keepdims=True)
    p = jnp.exp(logits - m)
    aff_ref[...] = p / jnp.sum(p, axis=-1, keepdims=True)


def _merge(x2d, z2d, outs, lses, gates, wuh, wua, wout, gx, wq, kv, wo, gm, wr_pad, seq, tm=256):
    n, d = x2d.shape
    nseq = seq // tm
    row = lambda i: (i, 0)
    const = lambda i: (0, 0)

    def rspec(a):
        return pl.BlockSpec((tm, a.shape[1]), row)

    def cspec(a):
        return pl.BlockSpec(a.shape, const)

    in_arrays = [x2d, z2d, *outs, *lses, gates, wuh, wua, wout, gx, wq, kv, wo, gm, wr_pad]
    in_specs = ([rspec(a) for a in in_arrays[:9]] + [cspec(a) for a in in_arrays[9:14]]
                + [pl.BlockSpec((1,) + kv.shape[1:], lambda i: (i // nseq, 0, 0))]
                + [cspec(a) for a in in_arrays[15:]])
    return pl.pallas_call(
        _merge_body,
        grid=(n // tm,),
        in_specs=in_specs,
        out_specs=[pl.BlockSpec((tm, d), row), pl.BlockSpec((tm, d), row), pl.BlockSpec((tm, LANES), row)],
        out_shape=[jax.ShapeDtypeStruct((n, d), F32), jax.ShapeDtypeStruct((n, d), BF16),
                   jax.ShapeDtypeStruct((n, LANES), F32)],
        compiler_params=_cparams(("arbitrary",)),
        name="merge",
    )(*in_arrays)


def _topk_body(aff_ref, tri_ref, rank_ref, *, cap):
    a = aff_ref[0]
    e, s = a.shape

    def count(mask):
        return jnp.sum(jnp.where(mask, 1.0, 0.0), axis=-1, keepdims=True)

    def as_float(bits):
        return pltpu.bitcast(jnp.broadcast_to(bits, (e, LANES)), F32)[:, 0:1]

    def search(i, thr):
        cand = thr | (jnp.int32(1) << (30 - i))
        return jnp.where(count(a >= as_float(cand)) >= cap, cand, thr)

    thr = as_float(lax.fori_loop(0, 31, search, jnp.zeros((e, 1), jnp.int32)))
    gt = a > thr
    eq = a == thr
    need = cap - count(gt)

    def prefix_excl(mask):
        mf = jnp.where(mask, 1.0, 0.0)
        parts = []
        carry = jnp.zeros((e, 1), F32)
        for c in range(s // LANES):
            blk = mf[:, c * LANES:(c + 1) * LANES]
            inc = _dot(blk.astype(BF16), tri_ref[...])
            parts.append(inc - blk + carry)
            carry = carry + inc[:, LANES - 1:LANES]
        return jnp.concatenate(parts, axis=-1)

    sel = gt | (eq & (prefix_excl(eq) < need))
    rank_ref[0] = jnp.where(sel, prefix_excl(sel), -1.0)


def _topk(aff_t, cap):
    b, e, s = aff_t.shape
    tri = jnp.asarray(np.triu(np.ones((LANES, LANES))), dtype=BF16)
    return pl.pallas_call(
        functools.partial(_topk_body, cap=cap),
        grid=(b,),
        in_specs=[pl.BlockSpec((1, e, s), lambda i: (i, 0, 0)),
                  pl.BlockSpec((LANES, LANES), lambda i: (0, 0))],
        out_specs=pl.BlockSpec((1, e, s), lambda i: (i, 0, 0)),
        out_shape=jax.ShapeDtypeStruct((b, e, s), F32),
        compiler_params=_cparams(("arbitrary",)),
        name="topk",
    )(aff_t, tri)


def _gather_body(rank_ref, hm_ref, xe_ref, *, cap, tc):
    s = hm_ref.shape[1]
    slot = lax.broadcasted_iota(jnp.int32, (cap, tc), 0).astype(F32)
    acc = jnp.zeros((cap, hm_ref.shape[2]), F32)
    for c in range(s // tc):
        r = rank_ref[0, 0, :, c * tc:(c + 1) * tc]
        onehot = jnp.where(slot == r, 1.0, 0.0).astype(BF16)
        acc = acc + _dot(onehot, hm_ref[0, c * tc:(c + 1) * tc, :])
    xe_ref[0, 0] = acc.astype(BF16)


def _gather(rank, hm, cap, tc=1024):
    b, e, s = rank.shape
    d = hm.shape[2]
    return pl.pallas_call(
        functools.partial(_gather_body, cap=cap, tc=tc),
        grid=(b, e),
        in_specs=[pl.BlockSpec((1, 1, 1, s), lambda bi, ei: (bi, ei, 0, 0)),
                  pl.BlockSpec((1, s, d), lambda bi, ei: (bi, 0, 0))],
        out_specs=pl.BlockSpec((1, 1, cap, d), lambda bi, ei: (bi, ei, 0, 0)),
        out_shape=jax.ShapeDtypeStruct((b, e, cap, d), BF16),
        compiler_params=_cparams(("arbitrary", "arbitrary")),
        name="gather",
    )(rank.reshape(b, e, 1, s), hm)


def _ffn_body(xe_ref, wg_ref, wu_ref, wd_ref, ye_ref, acc_ref):
    f = pl.program_id(1)
    nb = xe_ref.shape[0]
    wg = wg_ref[0].astype(BF16)
    wu = wu_ref[0].astype(BF16)
    wd = wd_ref[0].astype(BF16)
    for b in range(nb):
        xe = xe_ref[b, 0]
        a = _dot(xe, wg)
        u = _dot(xe, wu)
        hsw = (a * jax.nn.sigmoid(a) * u).astype(BF16)
        part = _dot(hsw, wd)

        @pl.when(f == 0)
        def _():
            acc_ref[b] = part

        @pl.when(f != 0)
        def _():
            acc_ref[b] += part

    @pl.when(f == pl.num_programs(1) - 1)
    def _():
        for b in range(nb):
            ye_ref[b, 0] = acc_ref[b].astype(BF16)


def _ffn(xe, wg, wu, wd, ft=512):
    b, e, cap, d = xe.shape
    ff = wg.shape[2]
    return pl.pallas_call(
        _ffn_body,
        grid=(e, ff // ft),
        in_specs=[pl.BlockSpec((b, 1, cap, d), lambda ei, fi: (0, ei, 0, 0)),
                  pl.BlockSpec((1, d, ft), lambda ei, fi: (ei, 0, fi)),
                  pl.BlockSpec((1, d, ft), lambda ei, fi: (ei, 0, fi)),
                  pl.BlockSpec((1, ft, d), lambda ei, fi: (ei, fi, 0))],
        out_specs=pl.BlockSpec((b, 1, cap, d), lambda ei, fi: (0, ei, 0, 0)),
        out_shape=jax.ShapeDtypeStruct((b, e, cap, d), BF16),
        scratch_shapes=[pltpu.VMEM((b, cap, d), F32)],
        compiler_params=_cparams(("arbitrary", "arbitrary")),
        name="ffn",
    )(xe, wg, wu, wd)


def _scatter_body(rank_ref, aff_ref, ye_ref, x2_ref, g_ref, out_ref, *, cap):
    ts = x2_ref.shape[1]
    slot = lax.broadcasted_iota(jnp.int32, (ts, cap), 1).astype(F32)
    rank = rank_ref[0]
    aff = aff_ref[0]
    acc = x2_ref[0]
    for e in range(ye_ref.shape[1]):
        onehot = jnp.where(slot == rank[:, e:e + 1], 1.0, 0.0).astype(BF16)
        acc = acc + aff[:, e:e + 1] * _dot(onehot, ye_ref[0, e])
    out_ref[0] = _rms(acc, g_ref[...])


def _scatter(rank_t, aff, ye, x2, g, cap, ts=512):
    b, s, e = rank_t.shape
    d = x2.shape[2]
    return pl.pallas_call(
        functools.partial(_scatter_body, cap=cap),
        grid=(b, s // ts),
        in_specs=[pl.BlockSpec((1, ts, e), lambda bi, ti: (bi, ti, 0)),
                  pl.BlockSpec((1, ts, LANES), lambda bi, ti: (bi, ti, 0)),
                  pl.BlockSpec((1, e, cap, d), lambda bi, ti: (bi, 0, 0, 0), pipeline_mode=pl.Buffered(1)),
                  pl.BlockSpec((1, ts, d), lambda bi, ti: (bi, ti, 0)),
                  pl.BlockSpec((1, d), lambda bi, ti: (0, 0))],
        out_specs=pl.BlockSpec((1, ts, d), lambda bi, ti: (bi, ti, 0)),
        out_shape=jax.ShapeDtypeStruct((b, s, d), F32),
        compiler_params=_cparams(("arbitrary", "arbitrary")),
        name="scatter",
    )(rank_t, aff, ye, x2, g)


def kernel(x, mem, ln_mix_g, w_in, hy_conv_w, hy_conv_b, filt_w1, filt_b1, filt_freq1, filt_w2, filt_b2,
           filt_freq2, filt_w3, hy_skip, w_up_hy, w_up_attn, w_gate, b_gate, w_out, ln_x_g, ln_mem_g,
           w_q_x, w_kv_mem, w_o_x, ln_moe_g, w_router, w_e_gate, w_e_up, w_e_down, ln_f_g):
    b, s, d = x.shape
    n = b * s
    x2d = x.reshape(n, d)

    half = HEAD_DIM // 2
    inv = ROPE_THETA ** (-jnp.arange(0, HEAD_DIM, 2, dtype=F32) / HEAD_DIM)
    ang = jnp.arange(s, dtype=F32)[:, None] * inv[None, :]
    lane = np.arange(LANES)
    cos_t = jnp.cos(ang)[:, lane % half]
    sin_t = jnp.sin(ang)[:, lane % half] * jnp.asarray(np.where(lane % HEAD_DIM < half, -1.0, 1.0), F32)[None, :]

    w_all = jnp.concatenate([w_in, w_gate], axis=1).astype(BF16)
    p_hy, q, k, v, gates = _proj(x2d, ln_mix_g[None], w_all, b_gate[None], cos_t, sin_t, s)

    t_col = jnp.linspace(0.0, 1.0, s, dtype=F32)[:, None]
    grid_col = 2.0 * math.pi * jnp.arange(s, dtype=F32)[:, None] / s
    bands = jnp.linspace(1e-4, HY_BANDS - 1, HY_BANDS, dtype=F32)[None, :]
    feat = _filt(t_col, grid_col, bands, filt_w1, filt_b1, filt_freq1, filt_w2, filt_b2, filt_freq2)
    delta = jnp.linspace(math.log(HY_TARGET) / HY_SLOW_PCT, math.log(HY_TARGET) / HY_FAST_PCT,
                         HY_WIDTH, dtype=F32)[None, :]
    tables = _fft_tables(s)
    khat = _hyfilt(t_col, delta, feat, filt_w3, tables[1], tables[3])
    z = _hyena(p_hy.reshape(b, s, -1), hy_conv_w, hy_conv_b[None], hy_skip, khat, tables)

    q3, k3, v3 = (a.reshape(b, s, ATTN_WIDTH) for a in (q, k, v))
    outs, lses = [], []
    for g, (window, dil) in enumerate(DIL_PAIRS):
        o_g, l_g = _attn_group(q3, k3, v3, g, dil, window // (2 * dil))
        outs.append(o_g)
        lses.append(l_g)

    kv = _memkv(mem, ln_mem_g[None], w_kv_mem.astype(BF16))
    wr_pad = jnp.pad(w_router, ((0, 0), (0, LANES - N_EXPERTS)))
    x2, hm, aff = _merge(x2d, z.reshape(n, HY_WIDTH), outs, lses, gates,
                         w_up_hy.astype(BF16), w_up_attn.astype(BF16), w_out.astype(BF16),
                         ln_x_g[None], w_q_x.astype(BF16), kv, w_o_x.astype(BF16), ln_moe_g[None], wr_pad, s)

    cap = max(1, EC_FACTOR * s // N_EXPERTS)
    aff3 = aff.reshape(b, s, LANES)
    rank = _topk(aff3[:, :, :N_EXPERTS].transpose(0, 2, 1), cap)
    xe = _gather(rank, hm.reshape(b, s, d), cap)
    ye = _ffn(xe, w_e_gate, w_e_up, w_e_down)
    return _scatter(rank.transpose(0, 2, 1), aff3, ye, x2.reshape(b, s, d), ln_f_g[None], cap)
```

```python
import functools
import math

import numpy as np
import jax
import jax.numpy as jnp
from jax import lax
from jax.experimental import pallas as pl
from jax.experimental.pallas import tpu as pltpu

F32 = jnp.float32
BF16 = jnp.bfloat16
HIGHEST = lax.Precision.HIGHEST

EPS = 1e-6
HY_WIDTH = 768
HY_BANDS = 16
HY_FFN = 64
HY_FAST_PCT = 0.3
HY_SLOW_PCT = 1.5
HY_TARGET = 1e-2
HEAD_DIM = 64
HEADS_PER_GROUP = 4
DIL_PAIRS = ((128, 1), (512, 4), (2048, 16))
N_GROUPS = len(DIL_PAIRS)
ATTN_WIDTH = N_GROUPS * HEADS_PER_GROUP * HEAD_DIM
ATTN_OUT = HEADS_PER_GROUP * HEAD_DIM
ROPE_THETA = 10000.0
X_HEADS = 4
X_HEAD_DIM = 128
N_EXPERTS = 16
EC_FACTOR = 2

LANES = 128
SUBLANES = 8
VMEM_LIMIT = 56 * 1024 * 1024

FFT_N2 = 128
HY_CB = 128
FFT_UNROLL = 8
MOE_RANK_OFF = 16


def _cparams(sem, vmem=VMEM_LIMIT):
    return pltpu.CompilerParams(dimension_semantics=sem, vmem_limit_bytes=vmem)


def _rms(x, g):
    return x * lax.rsqrt(jnp.mean(x * x, axis=-1, keepdims=True) + EPS) * g


def _dot(a, b):
    return jnp.dot(a, b, preferred_element_type=F32)


def _dot_hi(a, b):
    return jnp.dot(a, b, precision=HIGHEST, preferred_element_type=F32)


def _dot_t(a, b):
    return lax.dot_general(a, b, (((1,), (1,)), ((), ())), preferred_element_type=F32)


def _proj_body(x_ref, g_ref, w_ref, bg_ref, cos_ref, sin_ref, *rest, hyc, aw):
    ng = N_GROUPS
    perm_refs = (None,) + rest[:ng - 1]
    phy_ref = rest[ng - 1]
    qkv_refs = rest[ng:ng + 3 * ng]
    gate_ref = rest[ng + 3 * ng]
    h = _rms(x_ref[...], g_ref[...]).astype(BF16)
    phy_ref[...] = _dot(h, w_ref[:, :hyc])
    cos = cos_ref[...]
    sin = sin_ref[...]
    tm = cos.shape[0]
    lane = lax.broadcasted_iota(jnp.int32, (tm, LANES), 1)
    first = (lane % HEAD_DIM) < (HEAD_DIM // 2)

    def rope(t, scale):
        chunks = []
        for j in range(aw // LANES):
            tj = t[:, j * LANES:(j + 1) * LANES]
            partner = jnp.where(first, pltpu.roll(tj, LANES - HEAD_DIM // 2, 1),
                                pltpu.roll(tj, HEAD_DIM // 2, 1))
            chunks.append(((tj * cos + partner * sin) * scale).astype(BF16))
        return chunks

    gl = ATTN_OUT // LANES
    q = rope(_dot(h, w_ref[:, hyc:hyc + aw]), HEAD_DIM ** -0.5)
    k = rope(_dot(h, w_ref[:, hyc + aw:hyc + 2 * aw]), 1.0)
    vf = _dot(h, w_ref[:, hyc + 2 * aw:hyc + 3 * aw]).astype(BF16)
    v = [vf[:, j * LANES:(j + 1) * LANES] for j in range(aw // LANES)]
    for g in range(ng):
        for i, t in enumerate((q, k, v)):
            out_ref = qkv_refs[3 * g + i]
            tg = jnp.concatenate(t[g * gl:(g + 1) * gl], axis=1)
            if perm_refs[g] is not None:
                tg = _dot(perm_refs[g][...], tg).astype(BF16)
            dil = out_ref.shape[1]
            rows = tm // dil
            for r in range(dil):
                out_ref[0, r] = tg[r * rows:(r + 1) * rows, :]
    gate_ref[...] = jax.nn.sigmoid(_dot(h, w_ref[:, hyc + 3 * aw:]) + bg_ref[...]).astype(BF16)


def _proj(x2d, g, w_all, b_gate, cos_t, sin_t, batch, seq, tm=256):
    n, d = x2d.shape
    hyc = 3 * HY_WIDTH
    aw = ATTN_WIDTH
    gd = w_all.shape[1] - hyc - 3 * aw
    nseq = seq // tm
    row = lambda i: (i, 0)
    const = lambda i: (0, 0)
    perms = []
    for _, dil in DIL_PAIRS[1:]:
        rows = tm // dil
        src = (np.arange(tm) % rows) * dil + np.arange(tm) // rows
        perms.append(jnp.asarray(np.eye(tm)[src], dtype=BF16))
    qkv_specs, qkv_shapes = [], []
    for _, dil in DIL_PAIRS:
        for _ in range(3):
            qkv_specs.append(pl.BlockSpec((1, dil, tm // dil, ATTN_OUT), lambda i: (i // nseq, 0, i % nseq, 0)))
            qkv_shapes.append(jax.ShapeDtypeStruct((batch, dil, seq // dil, ATTN_OUT), BF16))
    return pl.pallas_call(
        functools.partial(_proj_body, hyc=hyc, aw=aw),
        grid=(n // tm,),
        in_specs=[
            pl.BlockSpec((tm, d), row),
            pl.BlockSpec((1, d), const),
            pl.BlockSpec(w_all.shape, const, pipeline_mode=pl.Buffered(1)),
            pl.BlockSpec((1, gd), const),
            pl.BlockSpec((tm, LANES), lambda i: (i % nseq, 0)),
            pl.BlockSpec((tm, LANES), lambda i: (i % nseq, 0)),
        ] + [pl.BlockSpec((tm, tm), const) for _ in perms],
        out_specs=[pl.BlockSpec((tm, hyc), row)] + qkv_specs + [pl.BlockSpec((tm, gd), row)],
        out_shape=[jax.ShapeDtypeStruct((n, hyc), F32)] + qkv_shapes + [jax.ShapeDtypeStruct((n, gd), BF16)],
        compiler_params=_cparams(("arbitrary",)),
        name="proj",
    )(x2d, g, w_all, b_gate, cos_t, sin_t, *perms)


def _filt_body(t_ref, grid_ref, bands_ref, w1t_ref, w1c_ref, w1s_ref, b1_ref, f1_ref,
               w2_ref, b2_ref, f2_ref, feat_ref):
    ang = bands_ref[...] * grid_ref[...]
    pre = (t_ref[...] * w1t_ref[...] + _dot_hi(jnp.cos(ang), w1c_ref[...])
           + _dot_hi(-jnp.sin(ang), w1s_ref[...]))
    h = jnp.sin(f1_ref[...] * (pre + b1_ref[...]))
    feat_ref[...] = jnp.sin(f2_ref[...] * (_dot_hi(h, w2_ref[...]) + b2_ref[...]))


def _filt(t_col, grid_col, bands, w1, b1, f1, w2, b2, f2):
    length = t_col.shape[0]
    nb = bands.shape[1]
    args = (t_col, grid_col, bands, w1[0:1], w1[1:1 + nb], w1[1 + nb:], b1[None], f1[None],
            w2, b2[None], f2[None])
    return pl.pallas_call(
        _filt_body,
        out_shape=jax.ShapeDtypeStruct((length, HY_FFN), F32),
        compiler_params=_cparams(None),
        name="filt",
    )(*args)


def _fft_tables(length):
    n = 2 * length
    n2 = FFT_N2
    n1 = n // n2
    h1 = n1 // 2
    k1 = np.arange(n1)[:, None]
    g_sig = np.zeros((n2, 2 * n1, 2 * h1))
    g_flt = np.zeros((n2, 2 * n1, 2 * h1))
    g_inv = np.zeros((n2, 2 * h1, 2 * n1))
    j = np.arange(h1)[None, :]
    for r in range(n2):
        th = 2 * np.pi * (k1 * (n2 * j + r) % n) / n
        gr, gi = np.cos(th), -np.sin(th)
        g_sig[r, 0::2, :h1] = gr
        g_sig[r, 0::2, h1:] = -gi
        g_sig[r, 1::2, :h1] = gi
        g_sig[r, 1::2, h1:] = gr
        g_flt[r, 0::2, :h1] = gr
        g_flt[r, 1::2, :h1] = gi
        off = (n2 - r) % n2
        m = off + n2 * j
        thb = 2 * np.pi * (k1 * m % n) / n
        live = (m >= 1).astype(np.float64)
        g_flt[r, 0::2, h1:] = np.cos(thb) * live
        g_flt[r, 1::2, h1:] = np.sin(thb) * live
        wr, wi = (np.cos(th) / n).T, (np.sin(th) / n).T
        g_inv[r, :h1, 0::2] = wr
        g_inv[r, :h1, 1::2] = -wi
        g_inv[r, h1:, 0::2] = wi
        g_inv[r, h1:, 1::2] = wr
    a = np.arange(n2)
    th2 = 2 * np.pi * (np.outer(a, a) % n2) / n2
    c2, s2 = np.cos(th2), np.sin(th2)
    f_fwd = np.block([[c2, s2], [-s2, c2]])
    f_inv = np.block([[c2, -s2], [s2, c2]])
    cast = lambda z: jnp.asarray(z, dtype=F32).astype(BF16)
    return cast(g_sig), cast(g_flt), cast(g_inv), cast(f_fwd), cast(f_inv)


def _fft_stage1(src_a_ref, src_b_ref, tab_ref, work_ref, n1, is_filter):
    n2 = FFT_N2
    h1 = n1 // 2

    def body(r, c):
        offb = jnp.where(r == 0, 0, n2 - r) if is_filter else r
        ra = src_a_ref[pl.ds(r, h1, stride=n2), :]
        rb = src_b_ref[pl.ds(offb, h1, stride=n2), :]
        rhs = jnp.concatenate([ra, rb], axis=0).astype(BF16)
        work_ref[pl.ds(r, 2 * n1, stride=n2), :] = _dot(tab_ref[r], rhs)
        return c

    lax.fori_loop(0, n2, body, 0, unroll=FFT_UNROLL)


def _hyfilt_body(t_ref, delta_ref, feat_ref, w3f_ref, w3b_ref, gflt_ref, ffwd_ref, khat_ref,
                 hf_ref, hb_ref, work_ref, *, length, rc):
    n2 = FFT_N2
    n1 = 2 * length // n2

    def gen(i, c):
        r0 = pl.multiple_of(i * rc, rc)
        decay = jnp.exp(-t_ref[pl.ds(r0, rc), :] * jnp.abs(delta_ref[...]))
        feat = feat_ref[pl.ds(r0, rc), :]
        hf_ref[pl.ds(r0, rc), :] = _dot_hi(feat, w3f_ref[...]) * decay
        hb_ref[pl.ds(r0, rc), :] = _dot_hi(feat, w3b_ref[...]) * decay
        return c

    lax.fori_loop(0, length // rc, gen, 0)
    _fft_stage1(hf_ref, hb_ref, gflt_ref, work_ref, n1, True)

    def stage2(k1, c):
        r0 = pl.multiple_of(k1 * (2 * n2), 2 * n2)
        blk = work_ref[pl.ds(r0, 2 * n2), :].astype(BF16)
        khat_ref[0, pl.ds(r0, 2 * n2), :] = _dot(ffwd_ref[...], blk).astype(BF16)
        return c

    lax.fori_loop(0, n1, stage2, 0, unroll=FFT_UNROLL)


def _hyfilt(t_col, delta, feat, w3, g_flt, f_fwd, rc=512):
    s = t_col.shape[0]
    cb = HY_CB
    ncb = HY_WIDTH // cb
    n2 = FFT_N2
    n1 = 2 * s // n2
    norder = w3.shape[1] // (2 * HY_WIDTH)
    one = pl.Buffered(1)

    def full(a):
        nd = a.ndim
        return pl.BlockSpec(a.shape, lambda o, c: (0,) * nd, pipeline_mode=one)

    return pl.pallas_call(
        functools.partial(_hyfilt_body, length=s, rc=rc),
        grid=(norder, ncb),
        in_specs=[full(t_col),
                  pl.BlockSpec((1, cb), lambda o, c: (0, c)),
                  full(feat),
                  pl.BlockSpec((HY_FFN, cb), lambda o, c: (0, 2 * ncb * o + c)),
                  pl.BlockSpec((HY_FFN, cb), lambda o, c: (0, 2 * ncb * o + ncb + c)),
                  full(g_flt), full(f_fwd)],
        out_specs=pl.BlockSpec((1, 2 * n1 * n2, cb), lambda o, c: (o, 0, c)),
        out_shape=jax.ShapeDtypeStruct((norder, 2 * n1 * n2, HY_WIDTH), BF16),
        scratch_shapes=[pltpu.VMEM((s, cb), F32), pltpu.VMEM((s, cb), F32),
                        pltpu.VMEM((2 * n1 * n2, cb), F32)],
        compiler_params=_cparams(("arbitrary", "arbitrary")),
        name="hyfilt",
    )(t_col, delta, feat, w3, w3, g_flt, f_fwd)


def _hyena_body(pz_ref, pg1_ref, pg2_ref, cwz_ref, cwg1_ref, cwg2_ref, cbz_ref, cbg1_ref, cbg2_ref,
                skip_ref, khat_ref, gsig_ref, ginv_ref, ffwd_ref, finv_ref,
                out_ref,
                za_ref, zb_ref, ya_ref, yb_ref, work_ref, *, length, rc):
    n2 = FFT_N2
    n1 = 2 * length // n2
    h1 = n1 // 2
    sub = SUBLANES

    def conv3(p_ref, b, r0, w_ref, bias_ref):
        p = p_ref[b, pl.ds(r0, rc), :]
        before = p_ref[b, pl.ds(pl.multiple_of(jnp.maximum(r0 - sub, 0), sub), sub), :][sub - 1:sub, :]
        after = p_ref[b, pl.ds(pl.multiple_of(jnp.minimum(r0 + rc, length - sub), sub), sub), :][0:1, :]
        before = jnp.where(r0 == 0, 0.0, before)
        after = jnp.where(r0 + rc == length, 0.0, after)
        row = lax.broadcasted_iota(jnp.int32, p.shape, 0)
        prev = jnp.where(row == 0, before, pltpu.roll(p, 1, 0))
        nxt = jnp.where(row == rc - 1, after, pltpu.roll(p, rc - 1, 0))
        return prev * w_ref[0:1, :] + p * w_ref[1:2, :] + nxt * w_ref[2:3, :] + bias_ref[...]

    def stage2(o):
        def body(i, c):
            starts = [pl.multiple_of((i * FFT_UNROLL + u) * (2 * n2), 2 * n2) for u in range(FFT_UNROLL)]
            blks = [work_ref[pl.ds(r0, 2 * n2), :].astype(BF16) for r0 in starts]
            outs = []
            for r0, blk in zip(starts, blks):
                x = _dot(ffwd_ref[...], blk)
                kh = khat_ref[o, pl.ds(r0, 2 * n2), :].astype(F32)
                xr, xi = x[:n2], x[n2:]
                kr, ki = kh[:n2], kh[n2:]
                y = jnp.concatenate([xr * kr - xi * ki, xr * ki + xi * kr], axis=0).astype(BF16)
                outs.append(_dot(finv_ref[...], y))
            for r0, out in zip(starts, outs):
                work_ref[pl.ds(r0, 2 * n2), :] = out
            return c
        lax.fori_loop(0, n1 // FFT_UNROLL, body, 0)

    def stage3():
        def body(r, c):
            rhs = work_ref[pl.ds(r, 2 * n1, stride=n2), :].astype(BF16)
            res = _dot(ginv_ref[r], rhs)
            ya_ref[pl.ds(r, h1, stride=n2), :] = res[:h1]
            yb_ref[pl.ds(r, h1, stride=n2), :] = res[h1:]
            return c
        lax.fori_loop(0, n2, body, 0, unroll=FFT_UNROLL)

    def load_z(i, c):
        r0 = pl.multiple_of(i * rc, rc)
        za_ref[pl.ds(r0, rc), :] = conv3(pz_ref, 0, r0, cwz_ref, cbz_ref)
        zb_ref[pl.ds(r0, rc), :] = conv3(pz_ref, 1, r0, cwz_ref, cbz_ref)
        return c

    lax.fori_loop(0, length // rc, load_z, 0)
    for o, (pg_ref, cw_ref, cb_ref) in enumerate(((pg1_ref, cwg1_ref, cbg1_ref), (pg2_ref, cwg2_ref, cbg2_ref))):
        _fft_stage1(za_ref, zb_ref, gsig_ref, work_ref, n1, False)
        stage2(o)
        stage3()
        skip = skip_ref[o:o + 1, :]

        def gate(i, c):
            r0 = pl.multiple_of(i * rc, rc)
            rows = pl.ds(r0, rc)
            new_a = conv3(pg_ref, 0, r0, cw_ref, cb_ref) * (ya_ref[rows, :] + skip * za_ref[rows, :])
            new_b = conv3(pg_ref, 1, r0, cw_ref, cb_ref) * (yb_ref[rows, :] + skip * zb_ref[rows, :])
            if o == 0:
                za_ref[rows, :] = new_a
                zb_ref[rows, :] = new_b
            else:
                out_ref[0, rows, :] = new_a
                out_ref[1, rows, :] = new_b
            return c

        lax.fori_loop(0, length // rc, gate, 0)


def _hyena(p_hy, conv_w, conv_b, skip, khat, tables, rc=512):
    b, s, _ = p_hy.shape
    w = HY_WIDTH
    cb = HY_CB
    ncb = w // cb
    n2 = FFT_N2
    n1 = 2 * s // n2
    g_sig, _, g_inv, f_fwd, f_inv = tables
    one = pl.Buffered(1)

    def pspec(off):
        return pl.BlockSpec((2, s, cb), lambda c, p: (p, 0, off + c), pipeline_mode=one)

    def cwspec(off):
        return pl.BlockSpec((3, cb), lambda c, p: (0, off + c))

    def cbspec(off):
        return pl.BlockSpec((1, cb), lambda c, p: (0, off + c))

    def full(a):
        nd = a.ndim
        return pl.BlockSpec(a.shape, lambda c, p: (0,) * nd, pipeline_mode=one)

    in_specs = [
        pspec(0), pspec(ncb), pspec(2 * ncb),
        cwspec(0), cwspec(ncb), cwspec(2 * ncb),
        cbspec(0), cbspec(ncb), cbspec(2 * ncb),
        pl.BlockSpec((2, cb), lambda c, p: (0, c)),
        pl.BlockSpec((khat.shape[0], 2 * n1 * n2, cb), lambda c, p: (0, 0, c), pipeline_mode=one),
        full(g_sig), full(g_inv), full(f_fwd), full(f_inv),
    ]
    return pl.pallas_call(
        functools.partial(_hyena_body, length=s, rc=rc),
        grid=(ncb, b // 2),
        in_specs=in_specs,
        out_specs=pl.BlockSpec((2, s, cb), lambda c, p: (p, 0, c)),
        out_shape=jax.ShapeDtypeStruct((b, s, w), F32),
        scratch_shapes=[
            pltpu.VMEM((s, cb), F32), pltpu.VMEM((s, cb), F32),
            pltpu.VMEM((s, cb), F32), pltpu.VMEM((s, cb), F32),
            pltpu.VMEM((2 * n1 * n2, cb), F32),
        ],
        compiler_params=_cparams(("arbitrary", "arbitrary")),
        name="hyena",
    )(p_hy, p_hy, p_hy, conv_w, conv_w, conv_w, conv_b, conv_b, conv_b,
      skip, khat, g_sig, g_inv, f_fwd, f_inv)


def _attn_body(q_ref, k_ref, v_ref, o_ref, lse_ref, *, n_side, qb):
    lr = q_ref.shape[2]
    kw = qb + 2 * n_side
    lane_head = lax.broadcasted_iota(jnp.int32, (qb, ATTN_OUT), 1) // HEAD_DIM
    qi = lax.broadcasted_iota(jnp.int32, (qb, kw), 0)
    ki = lax.broadcasted_iota(jnp.int32, (qb, kw), 1)

    def body(i, c):
        q0 = pl.multiple_of(i * qb, qb)
        w0 = pl.multiple_of(jnp.clip(q0 - n_side, 0, lr - kw), n_side)
        q = q_ref[0, 0, pl.ds(q0, qb), :]
        kwin = k_ref[0, 0, pl.ds(w0, kw), :]
        vwin = v_ref[0, 0, pl.ds(w0, kw), :]
        valid = jnp.abs((ki + w0) - (qi + q0)) <= n_side
        o = jnp.zeros((qb, ATTN_OUT), F32)
        lse = jnp.zeros((qb, ATTN_OUT), F32)
        for h in range(HEADS_PER_GROUP):
            mine = lane_head == h
            s = _dot_t(jnp.where(mine, q, jnp.zeros_like(q)), kwin)
            s = jnp.where(valid, s, -1e30)
            m = jnp.max(s, axis=-1, keepdims=True)
            p = jnp.exp(s - m)
            l = jnp.sum(p, axis=-1, keepdims=True)
            pv = _dot(p.astype(BF16), vwin)
            o = jnp.where(mine, pv / l, o)
            lse = jnp.where(mine, m + jnp.log(l), lse)
        o_ref[0, pl.ds(q0, qb), :] = o
        lse_ref[0, pl.ds(q0, qb), :] = lse
        return c

    lax.fori_loop(0, lr // qb, body, 0, unroll=2)


def _attn_group(q, k, v, g, n_side, qb=128):
    b, dil, lr, _ = q.shape
    in_spec = pl.BlockSpec((1, 1, lr, ATTN_OUT), lambda bi, r: (bi, r, 0, 0))
    out_spec = pl.BlockSpec((1, lr, ATTN_OUT), lambda bi, r: (bi, 0, r))
    o, lse = pl.pallas_call(
        functools.partial(_attn_body, n_side=n_side, qb=qb),
        grid=(b, dil),
        in_specs=[in_spec, in_spec, in_spec],
        out_specs=[out_spec, out_spec],
        out_shape=[jax.ShapeDtypeStruct((b, lr, dil * ATTN_OUT), F32)] * 2,
        compiler_params=_cparams(("arbitrary", "arbitrary")),
        name=f"attn{g}",
    )(q, k, v)
    return o.reshape(b * lr * dil, ATTN_OUT), lse.reshape(b * lr * dil, ATTN_OUT)


def _memkv_body(mem_ref, g_ref, w_ref, kv_ref):
    mn = _rms(mem_ref[0], g_ref[...]).astype(BF16)
    kv_ref[0] = _dot(mn, w_ref[...]).astype(BF16)


def _memkv(mem, g, w_kv):
    b, m, d = mem.shape
    n = w_kv.shape[1]
    return pl.pallas_call(
        _memkv_body,
        grid=(b,),
        in_specs=[pl.BlockSpec((1, m, d), lambda i: (i, 0, 0)),
                  pl.BlockSpec((1, d), lambda i: (0, 0)),
                  pl.BlockSpec((d, n), lambda i: (0, 0))],
        out_specs=pl.BlockSpec((1, m, n), lambda i: (i, 0, 0)),
        out_shape=jax.ShapeDtypeStruct((b, m, n), BF16),
        compiler_params=_cparams(("arbitrary",)),
        name="memkv",
    )(mem, g, w_kv)


def _merge_body(x_ref, z_ref, o0_ref, o1_ref, o2_ref, l0_ref, l1_ref, l2_ref, gate_ref,
                wuh_ref, wua_ref, wout_ref, gx_ref, wq_ref, kv_ref, wo_ref, gm_ref, wrh_ref, wrl_ref,
                x2_ref, hm_ref, aff_ref, *, nsplit):
    rows = x_ref.shape[0] // nsplit
    for part in range(nsplit):
        sl = slice(part * rows, (part + 1) * rows)
        _merge_rows(sl, x_ref, z_ref, o0_ref, o1_ref, o2_ref, l0_ref, l1_ref, l2_ref, gate_ref,
                    wuh_ref, wua_ref, wout_ref, gx_ref, wq_ref, kv_ref, wo_ref, gm_ref, wrh_ref, wrl_ref,
                    x2_ref, hm_ref, aff_ref)


def _merge_rows(sl, x_ref, z_ref, o0_ref, o1_ref, o2_ref, l0_ref, l1_ref, l2_ref, gate_ref,
                wuh_ref, wua_ref, wout_ref, gx_ref, wq_ref, kv_ref, wo_ref, gm_ref, wrh_ref, wrl_ref,
                x2_ref, hm_ref, aff_ref):
    d = x_ref.shape[1]
    l0, l1, l2 = l0_ref[sl, :], l1_ref[sl, :], l2_ref[sl, :]
    mx = jnp.maximum(jnp.maximum(l0, l1), l2)
    e0, e1, e2 = jnp.exp(l0 - mx), jnp.exp(l1 - mx), jnp.exp(l2 - mx)
    attn = (e0 * o0_ref[sl, :] + e1 * o1_ref[sl, :] + e2 * o2_ref[sl, :]) / (e0 + e1 + e2)
    y_hy = _dot(z_ref[sl, :].astype(BF16), wuh_ref[...])
    y_at = _dot(attn.astype(BF16), wua_ref[...])
    gates = gate_ref[sl, :].astype(F32)
    mix = gates[:, :d] * y_hy + gates[:, d:] * y_at
    x1 = x_ref[sl, :] + _dot(mix.astype(BF16), wout_ref[...])

    hx = _rms(x1, gx_ref[...]).astype(BF16)
    qx = _dot(hx, wq_ref[...]).astype(BF16)
    kv = kv_ref[0]
    xw = X_HEADS * X_HEAD_DIM
    heads = []
    for h in range(X_HEADS):
        hc = slice(h * X_HEAD_DIM, (h + 1) * X_HEAD_DIM)
        s = _dot_t(qx[:, hc], kv[:, hc]) * (X_HEAD_DIM ** -0.5)
        m = jnp.max(s, axis=-1, keepdims=True)
        p = jnp.exp(s - m)
        p = p / jnp.sum(p, axis=-1, keepdims=True)
        heads.append(_dot(p.astype(BF16), kv[:, xw + h * X_HEAD_DIM:xw + (h + 1) * X_HEAD_DIM]))
    ox = jnp.concatenate(heads, axis=-1).astype(BF16)
    x2 = x1 + _dot(ox, wo_ref[...])
    x2_ref[sl, :] = x2

    hm = _rms(x2, gm_ref[...])
    hm_hi = hm.astype(BF16)
    hm_lo = (hm - hm_hi.astype(F32)).astype(BF16)
    hm_ref[sl, :] = hm_hi
    logits = _dot(hm_hi, wrh_ref[...]) + _dot(hm_lo, wrh_ref[...]) + _dot(hm_hi, wrl_ref[...])
    lane = lax.broadcasted_iota(jnp.int32, logits.shape, 1)
    logits = jnp.where(lane < N_EXPERTS, logits, -1e30)
    m = jnp.max(logits, axis=-1, keepdims=True)
    p = jnp.exp(logits - m)
    aff_ref[sl, :] = p / jnp.sum(p, axis=-1, keepdims=True)


def _merge(x2d, z2d, outs, lses, gates, wuh, wua, wout, gx, wq, kv, wo, gm, wr_hi, wr_lo, seq, tm=512, nsplit=2):
    n, d = x2d.shape
    nseq = seq // tm
    row = lambda i: (i, 0)
    const = lambda i: (0, 0)

    def rspec(a):
        return pl.BlockSpec((tm, a.shape[1]), row)

    def cspec(a):
        return pl.BlockSpec(a.shape, const)

    in_arrays = [x2d, z2d, *outs, *lses, gates, wuh, wua, wout, gx, wq, kv, wo, gm, wr_hi, wr_lo]
    in_specs = ([rspec(a) for a in in_arrays[:9]] + [cspec(a) for a in in_arrays[9:14]]
                + [pl.BlockSpec((1,) + kv.shape[1:], lambda i: (i // nseq, 0, 0))]
                + [cspec(a) for a in in_arrays[15:]])
    return pl.pallas_call(
        functools.partial(_merge_body, nsplit=nsplit),
        grid=(n // tm,),
        in_specs=in_specs,
        out_specs=[pl.BlockSpec((tm, d), row), pl.BlockSpec((tm, d), row), pl.BlockSpec((tm, LANES), row)],
        out_shape=[jax.ShapeDtypeStruct((n, d), F32), jax.ShapeDtypeStruct((n, d), BF16),
                   jax.ShapeDtypeStruct((n, LANES), F32)],
        compiler_params=_cparams(("arbitrary",)),
        name="merge",
    )(*in_arrays)


def _topk_body(aff_ref, tri_ref, rank_ref, bounds_ref, *, cap, slot_block, tok_block):
    a = aff_ref[0]
    e, s = a.shape

    def count(mask):
        return jnp.sum(jnp.where(mask, 1.0, 0.0), axis=-1, keepdims=True)

    def as_float(bits):
        return pltpu.bitcast(jnp.broadcast_to(bits, (e, LANES)), F32)[:, 0:1]

    def search(i, thr):
        cand = thr | (jnp.int32(1) << (30 - i))
        return jnp.where(count(a >= as_float(cand)) >= cap, cand, thr)

    thr = as_float(lax.fori_loop(0, 31, search, jnp.zeros((e, 1), jnp.int32)))
    gt = a > thr
    eq = a == thr
    need = cap - count(gt)

    def prefix_excl(mask):
        mf = jnp.where(mask, 1.0, 0.0)
        parts = []
        carry = jnp.zeros((e, 1), F32)
        for c in range(s // LANES):
            blk = mf[:, c * LANES:(c + 1) * LANES]
            inc = _dot(blk.astype(BF16), tri_ref[...])
            parts.append(inc - blk + carry)
            carry = carry + inc[:, LANES - 1:LANES]
        return jnp.concatenate(parts, axis=-1)

    sel = gt | (eq & (prefix_excl(eq) < need))
    excl = prefix_excl(sel)
    rank_ref[0] = jnp.where(sel, excl, -1.0)

    incl = excl + jnp.where(sel, 1.0, 0.0)
    tok = lax.broadcasted_iota(jnp.int32, (e, s), 1)
    lane = lax.broadcasted_iota(jnp.int32, (e, LANES), 1)
    bounds = jnp.zeros((e, LANES), F32)
    for i in range(cap // slot_block + 1):
        bounds = jnp.where(lane == i, count(incl <= float(i * slot_block)), bounds)
    for j in range(s // tok_block + 1):
        bounds = jnp.where(lane == MOE_RANK_OFF + j, count(sel & (tok < j * tok_block)), bounds)
    bounds_ref[0] = bounds.astype(jnp.int32)


def _topk(aff_t, cap, slot_block, tok_block):
    b, e, s = aff_t.shape
    assert cap // slot_block + 1 <= MOE_RANK_OFF and MOE_RANK_OFF + s // tok_block + 1 <= LANES
    tri = jnp.asarray(np.triu(np.ones((LANES, LANES))), dtype=BF16)
    return pl.pallas_call(
        functools.partial(_topk_body, cap=cap, slot_block=slot_block, tok_block=tok_block),
        grid=(b,),
        in_specs=[pl.BlockSpec((1, e, s), lambda i: (i, 0, 0)),
                  pl.BlockSpec((LANES, LANES), lambda i: (0, 0))],
        out_specs=[pl.BlockSpec((1, e, s), lambda i: (i, 0, 0)),
                   pl.BlockSpec((1, e, LANES), lambda i: (i, 0, 0))],
        out_shape=[jax.ShapeDtypeStruct((b, e, s), F32),
                   jax.ShapeDtypeStruct((b, e, LANES), jnp.int32)],
        compiler_params=_cparams(("arbitrary",)),
        name="topk",
    )(aff_t, tri)


def _gather_body(bounds_ref, rank_ref, hm_ref, xe_ref, acc_ref, *, cap, tc, sb):
    base = (pl.program_id(0) * pl.num_programs(1) + pl.program_id(1)) * LANES
    for i in range(cap // sb):
        t_lo = bounds_ref[base + i]
        t_hi = bounds_ref[base + i + 1]
        slot = (lax.broadcasted_iota(jnp.int32, (sb, tc), 0) + i * sb).astype(F32)
        acc_ref[...] = jnp.zeros_like(acc_ref)

        def body(c, carry):
            r = rank_ref[0, 0, pl.ds(c, 1), :]
            onehot = jnp.where(slot == r, 1.0, 0.0).astype(BF16)
            acc_ref[...] += _dot(onehot, hm_ref[0, pl.ds(pl.multiple_of(c * tc, tc), tc), :])
            return carry

        lax.fori_loop(t_lo // tc, (t_hi - 1) // tc + 1, body, 0)
        xe_ref[0, 0, i * sb:(i + 1) * sb, :] = acc_ref[...].astype(BF16)


def _gather(bounds, rank, hm, cap, tc=256, sb=128):
    b, e, s = rank.shape
    d = hm.shape[2]
    return pl.pallas_call(
        functools.partial(_gather_body, cap=cap, tc=tc, sb=sb),
        grid_spec=pltpu.PrefetchScalarGridSpec(
            num_scalar_prefetch=1,
            grid=(b, e),
            in_specs=[pl.BlockSpec((1, 1, s // tc, tc), lambda bi, ei, bnd: (bi, ei, 0, 0)),
                      pl.BlockSpec((1, s, d), lambda bi, ei, bnd: (bi, 0, 0))],
            out_specs=pl.BlockSpec((1, 1, cap, d), lambda bi, ei, bnd: (bi, ei, 0, 0)),
            scratch_shapes=[pltpu.VMEM((sb, d), F32)],
        ),
        out_shape=jax.ShapeDtypeStruct((b, e, cap, d), BF16),
        compiler_params=_cparams(("arbitrary", "arbitrary")),
        name="gather",
    )(bounds.reshape(-1), rank.reshape(b, e, s // tc, tc), hm)


def _ffn_body(xe_ref, wg_ref, wu_ref, wd_ref, ye_ref, acc_ref):
    f = pl.program_id(1)
    nb = xe_ref.shape[0]
    wg = wg_ref[0].astype(BF16)
    wu = wu_ref[0].astype(BF16)
    wd = wd_ref[0].astype(BF16)
    for b in range(nb):
        xe = xe_ref[b, 0]
        a = _dot(xe, wg)
        u = _dot(xe, wu)
        hsw = (a * jax.nn.sigmoid(a) * u).astype(BF16)
        part = _dot(hsw, wd)

        @pl.when(f == 0)
        def _():
            acc_ref[b] = part

        @pl.when(f != 0)
        def _():
            acc_ref[b] += part

    @pl.when(f == pl.num_programs(1) - 1)
    def _():
        for b in range(nb):
            ye_ref[b, 0] = acc_ref[b].astype(BF16)


def _ffn(xe, wg, wu, wd, ft=512):
    b, e, cap, d = xe.shape
    ff = wg.shape[2]
    return pl.pallas_call(
        _ffn_body,
        grid=(e, ff // ft),
        in_specs=[pl.BlockSpec((b, 1, cap, d), lambda ei, fi: (0, ei, 0, 0)),
                  pl.BlockSpec((1, d, ft), lambda ei, fi: (ei, 0, fi)),
                  pl.BlockSpec((1, d, ft), lambda ei, fi: (ei, 0, fi)),
                  pl.BlockSpec((1, ft, d), lambda ei, fi: (ei, fi, 0))],
        out_specs=pl.BlockSpec((b, 1, cap, d), lambda ei, fi: (0, ei, 0, 0)),
        out_shape=jax.ShapeDtypeStruct((b, e, cap, d), BF16),
        scratch_shapes=[pltpu.VMEM((b, cap, d), F32)],
        compiler_params=_cparams(("arbitrary", "arbitrary")),
        name="ffn",
    )(xe, wg, wu, wd)


def _scatter_body(bounds_ref, rank_ref, aff_ref, ye_ref, x2_ref, g_ref, out_ref, acc_ref, *, sb):
    ts = x2_ref.shape[1]
    ne = ye_ref.shape[1]
    bi = pl.program_id(0)
    ti = pl.program_id(1)
    rank = rank_ref[0]
    aff = aff_ref[0]
    acc_ref[...] = x2_ref[0]
    for e in range(ne):
        base = (bi * ne + e) * LANES + MOE_RANK_OFF + ti
        r_lo = bounds_ref[base]
        r_hi = bounds_ref[base + 1]
        rcol = rank[:, e:e + 1]
        gcol = aff[:, e:e + 1]

        def body(i, carry):
            slot = (lax.broadcasted_iota(jnp.int32, (ts, sb), 1) + i * sb).astype(F32)
            onehot = jnp.where(slot == rcol, 1.0, 0.0).astype(BF16)
            rows = ye_ref[0, e, pl.ds(pl.multiple_of(i * sb, sb), sb), :]
            acc_ref[...] += gcol * _dot(onehot, rows)
            return carry

        lax.fori_loop(r_lo // sb, jnp.where(r_hi > r_lo, (r_hi - 1) // sb + 1, r_lo // sb), body, 0)
    out_ref[0] = _rms(acc_ref[...], g_ref[...])


def _scatter(bounds, rank_t, aff, ye, x2, g, ts, sb=256):
    b, s, e = rank_t.shape
    cap, d = ye.shape[2], ye.shape[3]
    return pl.pallas_call(
        functools.partial(_scatter_body, sb=sb),
        grid_spec=pltpu.PrefetchScalarGridSpec(
            num_scalar_prefetch=1,
            grid=(b, s // ts),
            in_specs=[pl.BlockSpec((1, ts, e), lambda bi, ti, bnd: (bi, ti, 0)),
                      pl.BlockSpec((1, ts, LANES), lambda bi, ti, bnd: (bi, ti, 0)),
                      pl.BlockSpec((1, e, cap, d), lambda bi, ti, bnd: (bi, 0, 0, 0),
                                   pipeline_mode=pl.Buffered(1)),
                      pl.BlockSpec((1, ts, d), lambda bi, ti, bnd: (bi, ti, 0)),
                      pl.BlockSpec((1, d), lambda bi, ti, bnd: (0, 0))],
            out_specs=pl.BlockSpec((1, ts, d), lambda bi, ti, bnd: (bi, ti, 0)),
            scratch_shapes=[pltpu.VMEM((ts, d), F32)],
        ),
        out_shape=jax.ShapeDtypeStruct((b, s, d), F32),
        compiler_params=_cparams(("arbitrary", "arbitrary")),
        name="scatter",
    )(bounds.reshape(-1), rank_t, aff, ye, x2, g)


def kernel(x, mem, ln_mix_g, w_in, hy_conv_w, hy_conv_b, filt_w1, filt_b1, filt_freq1, filt_w2, filt_b2,
           filt_freq2, filt_w3, hy_skip, w_up_hy, w_up_attn, w_gate, b_gate, w_out, ln_x_g, ln_mem_g,
           w_q_x, w_kv_mem, w_o_x, ln_moe_g, w_router, w_e_gate, w_e_up, w_e_down, ln_f_g):
    b, s, d = x.shape
    n = b * s
    x2d = x.reshape(n, d)

    half = HEAD_DIM // 2
    inv = ROPE_THETA ** (-jnp.arange(0, HEAD_DIM, 2, dtype=F32) / HEAD_DIM)
    ang = jnp.arange(s, dtype=F32)[:, None] * inv[None, :]
    lane = np.arange(LANES)
    cos_t = jnp.cos(ang)[:, lane % half]
    sin_t = jnp.sin(ang)[:, lane % half] * jnp.asarray(np.where(lane % HEAD_DIM < half, -1.0, 1.0), F32)[None, :]

    w_all = jnp.concatenate([w_in, w_gate], axis=1).astype(BF16)
    p_hy, *qkv, gates = _proj(x2d, ln_mix_g[None], w_all, b_gate[None], cos_t, sin_t, b, s)

    t_col = jnp.linspace(0.0, 1.0, s, dtype=F32)[:, None]
    grid_col = 2.0 * math.pi * jnp.arange(s, dtype=F32)[:, None] / s
    bands = jnp.linspace(1e-4, HY_BANDS - 1, HY_BANDS, dtype=F32)[None, :]
    feat = _filt(t_col, grid_col, bands, filt_w1, filt_b1, filt_freq1, filt_w2, filt_b2, filt_freq2)
    delta = jnp.linspace(math.log(HY_TARGET) / HY_SLOW_PCT, math.log(HY_TARGET) / HY_FAST_PCT,
                         HY_WIDTH, dtype=F32)[None, :]
    tables = _fft_tables(s)
    khat = _hyfilt(t_col, delta, feat, filt_w3, tables[1], tables[3])
    z = _hyena(p_hy.reshape(b, s, -1), hy_conv_w, hy_conv_b[None], hy_skip, khat, tables)

    outs, lses = [], []
    for g, (window, dil) in enumerate(DIL_PAIRS):
        o_g, l_g = _attn_group(*qkv[3 * g:3 * g + 3], g, window // (2 * dil))
        outs.append(o_g)
        lses.append(l_g)

    kv = _memkv(mem, ln_mem_g[None], w_kv_mem.astype(BF16))
    wr_pad = jnp.pad(w_router, ((0, 0), (0, LANES - N_EXPERTS)))
    wr_hi = wr_pad.astype(BF16)
    wr_lo = (wr_pad - wr_hi.astype(F32)).astype(BF16)
    x2, hm, aff = _merge(x2d, z.reshape(n, HY_WIDTH), outs, lses, gates,
                         w_up_hy.astype(BF16), w_up_attn.astype(BF16), w_out.astype(BF16),
                         ln_x_g[None], w_q_x.astype(BF16), kv, w_o_x.astype(BF16), ln_moe_g[None],
                         wr_hi, wr_lo, s)

    cap = max(1, EC_FACTOR * s // N_EXPERTS)
    gather_slots, scatter_slots, scatter_tokens = 128, 256, 512
    aff3 = aff.reshape(b, s, LANES)
    rank, bounds = _topk(aff3[:, :, :N_EXPERTS].transpose(0, 2, 1), cap, gather_slots, scatter_tokens)
    xe = _gather(bounds, rank, hm.reshape(b, s, d), cap, sb=gather_slots)
    ye = _ffn(xe, w_e_gate, w_e_up, w_e_down)
    return _scatter(bounds, rank.transpose(0, 2, 1), aff3, ye, x2.reshape(b, s, d), ln_f_g[None],
                    scatter_tokens, sb=scatter_slots)
```

```python
import functools
import math

import numpy as np
import jax
import jax.numpy as jnp
from jax import lax
from jax.experimental import pallas as pl
from jax.experimental.pallas import tpu as pltpu

F32 = jnp.float32
BF16 = jnp.bfloat16
HIGHEST = lax.Precision.HIGHEST

EPS = 1e-6
HY_WIDTH = 768
HY_BANDS = 16
HY_FFN = 64
HY_FAST_PCT = 0.3
HY_SLOW_PCT = 1.5
HY_TARGET = 1e-2
HEAD_DIM = 64
HEADS_PER_GROUP = 4
DIL_PAIRS = ((128, 1), (512, 4), (2048, 16))
N_GROUPS = len(DIL_PAIRS)
ATTN_WIDTH = N_GROUPS * HEADS_PER_GROUP * HEAD_DIM
ATTN_OUT = HEADS_PER_GROUP * HEAD_DIM
ROPE_THETA = 10000.0
X_HEADS = 4
X_HEAD_DIM = 128
N_EXPERTS = 16
EC_FACTOR = 2

LANES = 128
SUBLANES = 8
VMEM_LIMIT = 56 * 1024 * 1024

FFT_N2 = 128
HY_CB = 128
FFT_UNROLL = 8
MOE_TOK_BLOCK = 512
MOE_WINDOW = 128
BF16_ROWS = 16


def _cparams(sem, vmem=VMEM_LIMIT):
    return pltpu.CompilerParams(dimension_semantics=sem, vmem_limit_bytes=vmem)


def _rms(x, g):
    return x * lax.rsqrt(jnp.mean(x * x, axis=-1, keepdims=True) + EPS) * g


def _dot(a, b):
    return jnp.dot(a, b, preferred_element_type=F32)


def _dot_hi(a, b):
    return jnp.dot(a, b, precision=HIGHEST, preferred_element_type=F32)


def _dot_t(a, b):
    return lax.dot_general(a, b, (((1,), (1,)), ((), ())), preferred_element_type=F32)


def _proj_body(x_ref, g_ref, w_ref, bg_ref, cos_ref, sin_ref, *rest, hyc, aw):
    ng = N_GROUPS
    perm_refs = (None,) + rest[:ng - 1]
    phy_ref = rest[ng - 1]
    qkv_refs = rest[ng:ng + 3 * ng]
    gate_ref = rest[ng + 3 * ng]
    h = _rms(x_ref[...], g_ref[...]).astype(BF16)
    phy_ref[...] = _dot(h, w_ref[:, :hyc])
    cos = cos_ref[...]
    sin = sin_ref[...]
    tm = cos.shape[0]
    lane = lax.broadcasted_iota(jnp.int32, (tm, LANES), 1)
    first = (lane % HEAD_DIM) < (HEAD_DIM // 2)

    def rope(t, scale):
        chunks = []
        for j in range(aw // LANES):
            tj = t[:, j * LANES:(j + 1) * LANES]
            partner = jnp.where(first, pltpu.roll(tj, LANES - HEAD_DIM // 2, 1),
                                pltpu.roll(tj, HEAD_DIM // 2, 1))
            chunks.append(((tj * cos + partner * sin) * scale).astype(BF16))
        return chunks

    gl = ATTN_OUT // LANES
    q = rope(_dot(h, w_ref[:, hyc:hyc + aw]), HEAD_DIM ** -0.5)
    k = rope(_dot(h, w_ref[:, hyc + aw:hyc + 2 * aw]), 1.0)
    vf = _dot(h, w_ref[:, hyc + 2 * aw:hyc + 3 * aw]).astype(BF16)
    v = [vf[:, j * LANES:(j + 1) * LANES] for j in range(aw // LANES)]
    for g in range(ng):
        for i, t in enumerate((q, k, v)):
            out_ref = qkv_refs[3 * g + i]
            tg = jnp.concatenate(t[g * gl:(g + 1) * gl], axis=1)
            if perm_refs[g] is not None:
                tg = _dot(perm_refs[g][...], tg).astype(BF16)
            dil = out_ref.shape[1]
            rows = tm // dil
            for r in range(dil):
                out_ref[0, r] = tg[r * rows:(r + 1) * rows, :]
    gate_ref[...] = jax.nn.sigmoid(_dot(h, w_ref[:, hyc + 3 * aw:]) + bg_ref[...]).astype(BF16)


def _proj(x2d, g, w_all, b_gate, cos_t, sin_t, batch, seq, tm=256):
    n, d = x2d.shape
    hyc = 3 * HY_WIDTH
    aw = ATTN_WIDTH
    gd = w_all.shape[1] - hyc - 3 * aw
    nseq = seq // tm
    row = lambda i: (i, 0)
    const = lambda i: (0, 0)
    perms = []
    for _, dil in DIL_PAIRS[1:]:
        rows = tm // dil
        src = (np.arange(tm) % rows) * dil + np.arange(tm) // rows
        perms.append(jnp.asarray(np.eye(tm)[src], dtype=BF16))
    qkv_specs, qkv_shapes = [], []
    for _, dil in DIL_PAIRS:
        for _ in range(3):
            qkv_specs.append(pl.BlockSpec((1, dil, tm // dil, ATTN_OUT), lambda i: (i // nseq, 0, i % nseq, 0)))
            qkv_shapes.append(jax.ShapeDtypeStruct((batch, dil, seq // dil, ATTN_OUT), BF16))
    return pl.pallas_call(
        functools.partial(_proj_body, hyc=hyc, aw=aw),
        grid=(n // tm,),
        in_specs=[
            pl.BlockSpec((tm, d), row),
            pl.BlockSpec((1, d), const),
            pl.BlockSpec(w_all.shape, const, pipeline_mode=pl.Buffered(1)),
            pl.BlockSpec((1, gd), const),
            pl.BlockSpec((tm, LANES), lambda i: (i % nseq, 0)),
            pl.BlockSpec((tm, LANES), lambda i: (i % nseq, 0)),
        ] + [pl.BlockSpec((tm, tm), const) for _ in perms],
        out_specs=[pl.BlockSpec((tm, hyc), row)] + qkv_specs + [pl.BlockSpec((tm, gd), row)],
        out_shape=[jax.ShapeDtypeStruct((n, hyc), F32)] + qkv_shapes + [jax.ShapeDtypeStruct((n, gd), BF16)],
        compiler_params=_cparams(("arbitrary",)),
        name="proj",
    )(x2d, g, w_all, b_gate, cos_t, sin_t, *perms)


def _filt_body(t_ref, grid_ref, bands_ref, w1t_ref, w1c_ref, w1s_ref, b1_ref, f1_ref,
               w2_ref, b2_ref, f2_ref, feat_ref):
    ang = bands_ref[...] * grid_ref[...]
    pre = (t_ref[...] * w1t_ref[...] + _dot_hi(jnp.cos(ang), w1c_ref[...])
           + _dot_hi(-jnp.sin(ang), w1s_ref[...]))
    h = jnp.sin(f1_ref[...] * (pre + b1_ref[...]))
    feat_ref[...] = jnp.sin(f2_ref[...] * (_dot_hi(h, w2_ref[...]) + b2_ref[...]))


def _filt(t_col, grid_col, bands, w1, b1, f1, w2, b2, f2):
    length = t_col.shape[0]
    nb = bands.shape[1]
    args = (t_col, grid_col, bands, w1[0:1], w1[1:1 + nb], w1[1 + nb:], b1[None], f1[None],
            w2, b2[None], f2[None])
    return pl.pallas_call(
        _filt_body,
        out_shape=jax.ShapeDtypeStruct((length, HY_FFN), F32),
        compiler_params=_cparams(None),
        name="filt",
    )(*args)


def _fft_tables(length):
    n = 2 * length
    n2 = FFT_N2
    n1 = n // n2
    h1 = n1 // 2
    k1 = np.arange(n1)[:, None]
    g_sig = np.zeros((n2, 2 * n1, 2 * h1))
    g_flt = np.zeros((n2, 2 * n1, 2 * h1))
    g_inv = np.zeros((n2, 2 * h1, 2 * n1))
    j = np.arange(h1)[None, :]
    for r in range(n2):
        th = 2 * np.pi * (k1 * (n2 * j + r) % n) / n
        gr, gi = np.cos(th), -np.sin(th)
        g_sig[r, 0::2, :h1] = gr
        g_sig[r, 0::2, h1:] = -gi
        g_sig[r, 1::2, :h1] = gi
        g_sig[r, 1::2, h1:] = gr
        g_flt[r, 0::2, :h1] = gr
        g_flt[r, 1::2, :h1] = gi
        off = (n2 - r) % n2
        m = off + n2 * j
        thb = 2 * np.pi * (k1 * m % n) / n
        live = (m >= 1).astype(np.float64)
        g_flt[r, 0::2, h1:] = np.cos(thb) * live
        g_flt[r, 1::2, h1:] = np.sin(thb) * live
        wr, wi = (np.cos(th) / n).T, (np.sin(th) / n).T
        g_inv[r, :h1, 0::2] = wr
        g_inv[r, :h1, 1::2] = -wi
        g_inv[r, h1:, 0::2] = wi
        g_inv[r, h1:, 1::2] = wr
    a = np.arange(n2)
    th2 = 2 * np.pi * (np.outer(a, a) % n2) / n2
    c2, s2 = np.cos(th2), np.sin(th2)
    f_fwd = np.block([[c2, s2], [-s2, c2]])
    f_inv = np.block([[c2, -s2], [s2, c2]])
    cast = lambda z: jnp.asarray(z, dtype=F32).astype(BF16)
    return cast(g_sig), cast(g_flt), cast(g_inv), cast(f_fwd), cast(f_inv)


def _fft_stage1(src_a_ref, src_b_ref, tab_ref, work_ref, n1, is_filter):
    n2 = FFT_N2
    h1 = n1 // 2

    def body(r, c):
        offb = jnp.where(r == 0, 0, n2 - r) if is_filter else r
        ra = src_a_ref[pl.ds(r, h1, stride=n2), :]
        rb = src_b_ref[pl.ds(offb, h1, stride=n2), :]
        rhs = jnp.concatenate([ra, rb], axis=0).astype(BF16)
        work_ref[pl.ds(r, 2 * n1, stride=n2), :] = _dot(tab_ref[r], rhs)
        return c

    lax.fori_loop(0, n2, body, 0, unroll=FFT_UNROLL)


def _hyfilt_body(t_ref, delta_ref, feat_ref, w3f_ref, w3b_ref, gflt_ref, ffwd_ref, khat_ref,
                 hf_ref, hb_ref, work_ref, *, length, rc):
    n2 = FFT_N2
    n1 = 2 * length // n2

    def gen(i, c):
        r0 = pl.multiple_of(i * rc, rc)
        decay = jnp.exp(-t_ref[pl.ds(r0, rc), :] * jnp.abs(delta_ref[...]))
        feat = feat_ref[pl.ds(r0, rc), :]
        hf_ref[pl.ds(r0, rc), :] = _dot_hi(feat, w3f_ref[...]) * decay
        hb_ref[pl.ds(r0, rc), :] = _dot_hi(feat, w3b_ref[...]) * decay
        return c

    lax.fori_loop(0, length // rc, gen, 0)
    _fft_stage1(hf_ref, hb_ref, gflt_ref, work_ref, n1, True)

    def stage2(k1, c):
        r0 = pl.multiple_of(k1 * (2 * n2), 2 * n2)
        blk = work_ref[pl.ds(r0, 2 * n2), :].astype(BF16)
        khat_ref[0, pl.ds(r0, 2 * n2), :] = _dot(ffwd_ref[...], blk).astype(BF16)
        return c

    lax.fori_loop(0, n1, stage2, 0, unroll=FFT_UNROLL)


def _hyfilt(t_col, delta, feat, w3, g_flt, f_fwd, rc=512):
    s = t_col.shape[0]
    cb = HY_CB
    ncb = HY_WIDTH // cb
    n2 = FFT_N2
    n1 = 2 * s // n2
    norder = w3.shape[1] // (2 * HY_WIDTH)
    one = pl.Buffered(1)

    def full(a):
        nd = a.ndim
        return pl.BlockSpec(a.shape, lambda o, c: (0,) * nd, pipeline_mode=one)

    return pl.pallas_call(
        functools.partial(_hyfilt_body, length=s, rc=rc),
        grid=(norder, ncb),
        in_specs=[full(t_col),
                  pl.BlockSpec((1, cb), lambda o, c: (0, c)),
                  full(feat),
                  pl.BlockSpec((HY_FFN, cb), lambda o, c: (0, 2 * ncb * o + c)),
                  pl.BlockSpec((HY_FFN, cb), lambda o, c: (0, 2 * ncb * o + ncb + c)),
                  full(g_flt), full(f_fwd)],
        out_specs=pl.BlockSpec((1, 2 * n1 * n2, cb), lambda o, c: (o, 0, c)),
        out_shape=jax.ShapeDtypeStruct((norder, 2 * n1 * n2, HY_WIDTH), BF16),
        scratch_shapes=[pltpu.VMEM((s, cb), F32), pltpu.VMEM((s, cb), F32),
                        pltpu.VMEM((2 * n1 * n2, cb), F32)],
        compiler_params=_cparams(("arbitrary", "arbitrary")),
        name="hyfilt",
    )(t_col, delta, feat, w3, w3, g_flt, f_fwd)


def _hyena_body(pz_ref, pg1_ref, pg2_ref, cwz_ref, cwg1_ref, cwg2_ref, cbz_ref, cbg1_ref, cbg2_ref,
                skip_ref, khat_ref, gsig_ref, ginv_ref, ffwd_ref, finv_ref,
                out_ref,
                za_ref, zb_ref, ya_ref, yb_ref, work_ref, *, length, rc):
    n2 = FFT_N2
    n1 = 2 * length // n2
    h1 = n1 // 2
    sub = SUBLANES

    def conv3(p_ref, b, r0, w_ref, bias_ref):
        p = p_ref[b, pl.ds(r0, rc), :]
        before = p_ref[b, pl.ds(pl.multiple_of(jnp.maximum(r0 - sub, 0), sub), sub), :][sub - 1:sub, :]
        after = p_ref[b, pl.ds(pl.multiple_of(jnp.minimum(r0 + rc, length - sub), sub), sub), :][0:1, :]
        before = jnp.where(r0 == 0, 0.0, before)
        after = jnp.where(r0 + rc == length, 0.0, after)
        row = lax.broadcasted_iota(jnp.int32, p.shape, 0)
        prev = jnp.where(row == 0, before, pltpu.roll(p, 1, 0))
        nxt = jnp.where(row == rc - 1, after, pltpu.roll(p, rc - 1, 0))
        return prev * w_ref[0:1, :] + p * w_ref[1:2, :] + nxt * w_ref[2:3, :] + bias_ref[...]

    def stage2(o):
        def body(i, c):
            starts = [pl.multiple_of((i * FFT_UNROLL + u) * (2 * n2), 2 * n2) for u in range(FFT_UNROLL)]
            blks = [work_ref[pl.ds(r0, 2 * n2), :].astype(BF16) for r0 in starts]
            outs = []
            for r0, blk in zip(starts, blks):
                x = _dot(ffwd_ref[...], blk)
                kh = khat_ref[o, pl.ds(r0, 2 * n2), :].astype(F32)
                xr, xi = x[:n2], x[n2:]
                kr, ki = kh[:n2], kh[n2:]
                y = jnp.concatenate([xr * kr - xi * ki, xr * ki + xi * kr], axis=0).astype(BF16)
                outs.append(_dot(finv_ref[...], y))
            for r0, out in zip(starts, outs):
                work_ref[pl.ds(r0, 2 * n2), :] = out
            return c
        lax.fori_loop(0, n1 // FFT_UNROLL, body, 0)

    def stage3():
        def body(r, c):
            rhs = work_ref[pl.ds(r, 2 * n1, stride=n2), :].astype(BF16)
            res = _dot(ginv_ref[r], rhs)
            ya_ref[pl.ds(r, h1, stride=n2), :] = res[:h1]
            yb_ref[pl.ds(r, h1, stride=n2), :] = res[h1:]
            return c
        lax.fori_loop(0, n2, body, 0, unroll=FFT_UNROLL)

    def load_z(i, c):
        r0 = pl.multiple_of(i * rc, rc)
        za_ref[pl.ds(r0, rc), :] = conv3(pz_ref, 0, r0, cwz_ref, cbz_ref)
        zb_ref[pl.ds(r0, rc), :] = conv3(pz_ref, 1, r0, cwz_ref, cbz_ref)
        return c

    lax.fori_loop(0, length // rc, load_z, 0)
    for o, (pg_ref, cw_ref, cb_ref) in enumerate(((pg1_ref, cwg1_ref, cbg1_ref), (pg2_ref, cwg2_ref, cbg2_ref))):
        _fft_stage1(za_ref, zb_ref, gsig_ref, work_ref, n1, False)
        stage2(o)
        stage3()
        skip = skip_ref[o:o + 1, :]

        def gate(i, c):
            r0 = pl.multiple_of(i * rc, rc)
            rows = pl.ds(r0, rc)
            new_a = conv3(pg_ref, 0, r0, cw_ref, cb_ref) * (ya_ref[rows, :] + skip * za_ref[rows, :])
            new_b = conv3(pg_ref, 1, r0, cw_ref, cb_ref) * (yb_ref[rows, :] + skip * zb_ref[rows, :])
            if o == 0:
                za_ref[rows, :] = new_a
                zb_ref[rows, :] = new_b
            else:
                out_ref[0, rows, :] = new_a
                out_ref[1, rows, :] = new_b
            return c

        lax.fori_loop(0, length // rc, gate, 0)


def _hyena(p_hy, conv_w, conv_b, skip, khat, tables, rc=512):
    b, s, _ = p_hy.shape
    w = HY_WIDTH
    cb = HY_CB
    ncb = w // cb
    n2 = FFT_N2
    n1 = 2 * s // n2
    g_sig, _, g_inv, f_fwd, f_inv = tables
    one = pl.Buffered(1)

    def pspec(off):
        return pl.BlockSpec((2, s, cb), lambda c, p: (p, 0, off + c), pipeline_mode=one)

    def cwspec(off):
        return pl.BlockSpec((3, cb), lambda c, p: (0, off + c))

    def cbspec(off):
        return pl.BlockSpec((1, cb), lambda c, p: (0, off + c))

    def full(a):
        nd = a.ndim
        return pl.BlockSpec(a.shape, lambda c, p: (0,) * nd, pipeline_mode=one)

    in_specs = [
        pspec(0), pspec(ncb), pspec(2 * ncb),
        cwspec(0), cwspec(ncb), cwspec(2 * ncb),
        cbspec(0), cbspec(ncb), cbspec(2 * ncb),
        pl.BlockSpec((2, cb), lambda c, p: (0, c)),
        pl.BlockSpec((khat.shape[0], 2 * n1 * n2, cb), lambda c, p: (0, 0, c), pipeline_mode=one),
        full(g_sig), full(g_inv), full(f_fwd), full(f_inv),
    ]
    return pl.pallas_call(
        functools.partial(_hyena_body, length=s, rc=rc),
        grid=(ncb, b // 2),
        in_specs=in_specs,
        out_specs=pl.BlockSpec((2, s, cb), lambda c, p: (p, 0, c)),
        out_shape=jax.ShapeDtypeStruct((b, s, w), F32),
        scratch_shapes=[
            pltpu.VMEM((s, cb), F32), pltpu.VMEM((s, cb), F32),
            pltpu.VMEM((s, cb), F32), pltpu.VMEM((s, cb), F32),
            pltpu.VMEM((2 * n1 * n2, cb), F32),
        ],
        compiler_params=_cparams(("arbitrary", "arbitrary")),
        name="hyena",
    )(p_hy, p_hy, p_hy, conv_w, conv_w, conv_w, conv_b, conv_b, conv_b,
      skip, khat, g_sig, g_inv, f_fwd, f_inv)


def _attn_body(q_ref, k_ref, v_ref, o_ref, lse_ref, *, n_side, qb):
    lr = q_ref.shape[2]
    kw = qb + 2 * n_side
    nh = HEADS_PER_GROUP
    lane_head = lax.broadcasted_iota(jnp.int32, (qb, ATTN_OUT), 1) // HEAD_DIM
    qi = lax.broadcasted_iota(jnp.int32, (nh * qb, kw), 0) % qb
    ki = lax.broadcasted_iota(jnp.int32, (nh * qb, kw), 1)

    def body(i, c):
        q0 = pl.multiple_of(i * qb, qb)
        w0 = pl.multiple_of(jnp.clip(q0 - n_side, 0, lr - kw), n_side)
        q = q_ref[0, 0, pl.ds(q0, qb), :]
        kwin = k_ref[0, 0, pl.ds(w0, kw), :]
        vwin = v_ref[0, 0, pl.ds(w0, kw), :]
        valid = jnp.abs((ki + w0) - (qi + q0)) <= n_side
        qs = jnp.concatenate([jnp.where(lane_head == h, q, jnp.zeros_like(q)) for h in range(nh)], axis=0)
        s = jnp.where(valid, _dot_t(qs, kwin), -1e30)
        m = jnp.max(s, axis=-1, keepdims=True)
        p = jnp.exp(s - m)
        l = jnp.sum(p, axis=-1, keepdims=True)
        pv = _dot(p.astype(BF16), vwin) / l
        lse_rows = m + jnp.log(l)
        o = jnp.zeros((qb, ATTN_OUT), F32)
        lse = jnp.zeros((qb, ATTN_OUT), F32)
        for h in range(nh):
            mine = lane_head == h
            o = jnp.where(mine, pv[h * qb:(h + 1) * qb], o)
            lse = jnp.where(mine, lse_rows[h * qb:(h + 1) * qb], lse)
        o_ref[0, pl.ds(q0, qb), :] = o
        lse_ref[0, pl.ds(q0, qb), :] = lse
        return c

    lax.fori_loop(0, lr // qb, body, 0, unroll=2)


def _attn_group(q, k, v, g, n_side, qb=128):
    b, dil, lr, _ = q.shape
    in_spec = pl.BlockSpec((1, 1, lr, ATTN_OUT), lambda bi, r: (bi, r, 0, 0))
    out_spec = pl.BlockSpec((1, lr, ATTN_OUT), lambda bi, r: (bi, 0, r))
    o, lse = pl.pallas_call(
        functools.partial(_attn_body, n_side=n_side, qb=qb),
        grid=(b, dil),
        in_specs=[in_spec, in_spec, in_spec],
        out_specs=[out_spec, out_spec],
        out_shape=[jax.ShapeDtypeStruct((b, lr, dil * ATTN_OUT), F32)] * 2,
        compiler_params=_cparams(("arbitrary", "arbitrary")),
        name=f"attn{g}",
    )(q, k, v)
    return o.reshape(b * lr * dil, ATTN_OUT), lse.reshape(b * lr * dil, ATTN_OUT)


def _memkv_body(mem_ref, g_ref, w_ref, kv_ref):
    mn = _rms(mem_ref[0], g_ref[...]).astype(BF16)
    kv_ref[0] = _dot(mn, w_ref[...]).astype(BF16)


def _memkv(mem, g, w_kv):
    b, m, d = mem.shape
    n = w_kv.shape[1]
    return pl.pallas_call(
        _memkv_body,
        grid=(b,),
        in_specs=[pl.BlockSpec((1, m, d), lambda i: (i, 0, 0)),
                  pl.BlockSpec((1, d), lambda i: (0, 0)),
                  pl.BlockSpec((d, n), lambda i: (0, 0))],
        out_specs=pl.BlockSpec((1, m, n), lambda i: (i, 0, 0)),
        out_shape=jax.ShapeDtypeStruct((b, m, n), BF16),
        compiler_params=_cparams(("arbitrary",)),
        name="memkv",
    )(mem, g, w_kv)


def _merge_body(x_ref, z_ref, o0_ref, o1_ref, o2_ref, l0_ref, l1_ref, l2_ref, gate_ref,
                wuh_ref, wua_ref, wout_ref, gx_ref, wq_ref, kv_ref, wo_ref, gm_ref, wrh_ref, wrl_ref,
                x2_ref, hm_ref, aff_ref, *, nsplit):
    rows = x_ref.shape[0] // nsplit
    for part in range(nsplit):
        sl = slice(part * rows, (part + 1) * rows)
        _merge_rows(sl, x_ref, z_ref, o0_ref, o1_ref, o2_ref, l0_ref, l1_ref, l2_ref, gate_ref,
                    wuh_ref, wua_ref, wout_ref, gx_ref, wq_ref, kv_ref, wo_ref, gm_ref, wrh_ref, wrl_ref,
                    x2_ref, hm_ref, aff_ref)


def _merge_rows(sl, x_ref, z_ref, o0_ref, o1_ref, o2_ref, l0_ref, l1_ref, l2_ref, gate_ref,
                wuh_ref, wua_ref, wout_ref, gx_ref, wq_ref, kv_ref, wo_ref, gm_ref, wrh_ref, wrl_ref,
                x2_ref, hm_ref, aff_ref):
    d = x_ref.shape[1]
    l0, l1, l2 = l0_ref[sl, :], l1_ref[sl, :], l2_ref[sl, :]
    mx = jnp.maximum(jnp.maximum(l0, l1), l2)
    e0, e1, e2 = jnp.exp(l0 - mx), jnp.exp(l1 - mx), jnp.exp(l2 - mx)
    attn = (e0 * o0_ref[sl, :] + e1 * o1_ref[sl, :] + e2 * o2_ref[sl, :]) / (e0 + e1 + e2)
    y_hy = _dot(z_ref[sl, :].astype(BF16), wuh_ref[...])
    y_at = _dot(attn.astype(BF16), wua_ref[...])
    gates = gate_ref[sl, :].astype(F32)
    mix = gates[:, :d] * y_hy + gates[:, d:] * y_at
    x1 = x_ref[sl, :] + _dot(mix.astype(BF16), wout_ref[...])

    hx = _rms(x1, gx_ref[...]).astype(BF16)
    qx = _dot(hx, wq_ref[...]).astype(BF16)
    kv = kv_ref[0]
    xw = X_HEADS * X_HEAD_DIM
    heads = []
    for h in range(X_HEADS):
        hc = slice(h * X_HEAD_DIM, (h + 1) * X_HEAD_DIM)
        s = _dot_t(qx[:, hc], kv[:, hc]) * (X_HEAD_DIM ** -0.5)
        m = jnp.max(s, axis=-1, keepdims=True)
        p = jnp.exp(s - m)
        p = p / jnp.sum(p, axis=-1, keepdims=True)
        heads.append(_dot(p.astype(BF16), kv[:, xw + h * X_HEAD_DIM:xw + (h + 1) * X_HEAD_DIM]))
    ox = jnp.concatenate(heads, axis=-1).astype(BF16)
    x2 = x1 + _dot(ox, wo_ref[...])
    x2_ref[sl, :] = x2

    hm = _rms(x2, gm_ref[...])
    hm_hi = hm.astype(BF16)
    hm_lo = (hm - hm_hi.astype(F32)).astype(BF16)
    hm_ref[sl, :] = hm_hi
    logits = _dot(hm_hi, wrh_ref[...]) + _dot(hm_lo, wrh_ref[...]) + _dot(hm_hi, wrl_ref[...])
    lane = lax.broadcasted_iota(jnp.int32, logits.shape, 1)
    logits = jnp.where(lane < N_EXPERTS, logits, -1e30)
    m = jnp.max(logits, axis=-1, keepdims=True)
    p = jnp.exp(logits - m)
    aff_ref[sl, :] = p / jnp.sum(p, axis=-1, keepdims=True)


def _merge(x2d, z2d, outs, lses, gates, wuh, wua, wout, gx, wq, kv, wo, gm, wr_hi, wr_lo, seq, tm=512, nsplit=2):
    n, d = x2d.shape
    nseq = seq // tm
    row = lambda i: (i, 0)
    const = lambda i: (0, 0)

    def rspec(a):
        return pl.BlockSpec((tm, a.shape[1]), row)

    def cspec(a):
        return pl.BlockSpec(a.shape, const)

    in_arrays = [x2d, z2d, *outs, *lses, gates, wuh, wua, wout, gx, wq, kv, wo, gm, wr_hi, wr_lo]
    in_specs = ([rspec(a) for a in in_arrays[:9]] + [cspec(a) for a in in_arrays[9:14]]
                + [pl.BlockSpec((1,) + kv.shape[1:], lambda i: (i // nseq, 0, 0))]
                + [cspec(a) for a in in_arrays[15:]])
    return pl.pallas_call(
        functools.partial(_merge_body, nsplit=nsplit),
        grid=(n // tm,),
        in_specs=in_specs,
        out_specs=[pl.BlockSpec((tm, d), row), pl.BlockSpec((tm, d), row), pl.BlockSpec((tm, LANES), row)],
        out_shape=[jax.ShapeDtypeStruct((n, d), F32), jax.ShapeDtypeStruct((n, d), BF16),
                   jax.ShapeDtypeStruct((n, LANES), F32)],
        compiler_params=_cparams(("arbitrary",)),
        name="merge",
    )(*in_arrays)


def _topk_body(aff_ref, tri_ref, rank_ref, bounds_ref, *, cap, tok_block):
    a = aff_ref[0]
    e, s = a.shape

    def count(mask):
        return jnp.sum(jnp.where(mask, 1.0, 0.0), axis=-1, keepdims=True)

    def as_float(bits):
        return pltpu.bitcast(jnp.broadcast_to(bits, (e, LANES)), F32)[:, 0:1]

    def search(i, thr):
        cand = thr | (jnp.int32(1) << (30 - i))
        return jnp.where(count(a >= as_float(cand)) >= cap, cand, thr)

    thr = as_float(lax.fori_loop(0, 31, search, jnp.zeros((e, 1), jnp.int32)))
    gt = a > thr
    eq = a == thr
    need = cap - count(gt)

    def prefix_excl(mask):
        mf = jnp.where(mask, 1.0, 0.0)
        parts = []
        carry = jnp.zeros((e, 1), F32)
        for c in range(s // LANES):
            blk = mf[:, c * LANES:(c + 1) * LANES]
            inc = _dot(blk.astype(BF16), tri_ref[...])
            parts.append(inc - blk + carry)
            carry = carry + inc[:, LANES - 1:LANES]
        return jnp.concatenate(parts, axis=-1)

    sel = gt | (eq & (prefix_excl(eq) < need))
    excl = prefix_excl(sel)
    rank_ref[0] = jnp.where(sel, excl, -1.0)

    tok = lax.broadcasted_iota(jnp.int32, (e, s), 1)
    lane = lax.broadcasted_iota(jnp.int32, (e, LANES), 1)
    bounds = jnp.zeros((e, LANES), F32)
    for j in range(s // tok_block + 1):
        bounds = jnp.where(lane == j, count(sel & (tok < j * tok_block)), bounds)
    bounds_ref[0] = bounds.astype(jnp.int32)


def _topk(aff_t, cap, tok_block):
    b, e, s = aff_t.shape
    assert s // tok_block + 1 <= LANES
    tri = jnp.asarray(np.triu(np.ones((LANES, LANES))), dtype=BF16)
    return pl.pallas_call(
        functools.partial(_topk_body, cap=cap, tok_block=tok_block),
        grid=(b,),
        in_specs=[pl.BlockSpec((1, e, s), lambda i: (i, 0, 0)),
                  pl.BlockSpec((LANES, LANES), lambda i: (0, 0))],
        out_specs=[pl.BlockSpec((1, e, s), lambda i: (i, 0, 0)),
                   pl.BlockSpec((1, e, LANES), lambda i: (i, 0, 0))],
        out_shape=[jax.ShapeDtypeStruct((b, e, s), F32),
                   jax.ShapeDtypeStruct((b, e, LANES), jnp.int32)],
        compiler_params=_cparams(("arbitrary",)),
        name="topk",
    )(aff_t, tri)


def _moe_windows(bounds_ref, base, ne, step):
    starts = []
    passes = jnp.int32(0)
    for e in range(ne):
        r_lo = bounds_ref[base + e * LANES + step]
        r_hi = bounds_ref[base + e * LANES + step + 1]
        ws = (r_lo // BF16_ROWS) * BF16_ROWS
        starts.append(ws)
        passes = jnp.maximum(passes, (r_hi - ws + MOE_WINDOW - 1) // MOE_WINDOW)
    return starts, passes


def _gather_body(bounds_ref, rank_ref, hm_ref, affh_ref, affl_ref, xe_ref, gs_ref, *, cap, eg):
    w = MOE_WINDOW
    ne, tc = rank_ref.shape[1], rank_ref.shape[2]
    bi = pl.program_id(0)
    ci = pl.program_id(1)

    @pl.when(ci == 0)
    def _():
        xe_ref[...] = jnp.zeros_like(xe_ref)
        gs_ref[...] = jnp.zeros_like(gs_ref)

    starts, passes = _moe_windows(bounds_ref, bi * ne * LANES, ne, ci)
    iota_w = lax.broadcasted_iota(jnp.int32, (w, 1), 0)

    def one_pass(p, carry):
        for g0 in range(0, ne, eg):
            lo = [starts[e] + p * w for e in range(g0, g0 + eg)]
            phys = [pl.multiple_of(jnp.minimum(v, cap - w), BF16_ROWS) for v in lo]
            rk = jnp.concatenate([jnp.broadcast_to(rank_ref[0, e:e + 1, :], (w, tc))
                                  for e in range(g0, g0 + eg)], axis=0)
            slot = jnp.concatenate([iota_w + v for v in phys], axis=0).astype(F32)
            lom = jnp.concatenate([jnp.zeros_like(iota_w) + v for v in lo], axis=0).astype(F32)
            onehot = jnp.where((rk == slot) & (slot >= lom), 1.0, 0.0).astype(BF16)
            res = _dot(onehot, hm_ref[0])
            resg = _dot(onehot, affh_ref[0]) + _dot(onehot, affl_ref[0])
            for k, e in enumerate(range(g0, g0 + eg)):
                rows = pl.ds(phys[k], w)
                xe_ref[0, e, rows, :] = (xe_ref[0, e, rows, :].astype(F32) + res[k * w:(k + 1) * w]).astype(BF16)
                gs_ref[0, e, rows, :] = gs_ref[0, e, rows, :] + resg[k * w:(k + 1) * w]
        return carry

    lax.fori_loop(0, passes, one_pass, 0)


def _gather(bounds, rank, hm, aff_hi, aff_lo, cap, eg=8):
    b, e, s = rank.shape
    d = hm.shape[2]
    tc = MOE_TOK_BLOCK
    one = pl.Buffered(1)
    return pl.pallas_call(
        functools.partial(_gather_body, cap=cap, eg=eg),
        grid_spec=pltpu.PrefetchScalarGridSpec(
            num_scalar_prefetch=1,
            grid=(b, s // tc),
            in_specs=[pl.BlockSpec((1, e, tc), lambda bi, ci, bnd: (bi, 0, ci)),
                      pl.BlockSpec((1, tc, d), lambda bi, ci, bnd: (bi, ci, 0)),
                      pl.BlockSpec((1, tc, LANES), lambda bi, ci, bnd: (bi, ci, 0)),
                      pl.BlockSpec((1, tc, LANES), lambda bi, ci, bnd: (bi, ci, 0))],
            out_specs=[pl.BlockSpec((1, e, cap, d), lambda bi, ci, bnd: (bi, 0, 0, 0), pipeline_mode=one),
                       pl.BlockSpec((1, e, cap, LANES), lambda bi, ci, bnd: (bi, 0, 0, 0), pipeline_mode=one)],
        ),
        out_shape=[jax.ShapeDtypeStruct((b, e, cap, d), BF16),
                   jax.ShapeDtypeStruct((b, e, cap, LANES), F32)],
        compiler_params=_cparams(("arbitrary", "arbitrary")),
        name="gather",
    )(bounds.reshape(-1), rank, hm, aff_hi, aff_lo)


def _ffn_body(xe_ref, gs_ref, wg_ref, wu_ref, wd_ref, ye_ref, acc_ref):
    f = pl.program_id(1)
    nb = xe_ref.shape[0]
    wg = wg_ref[0].astype(BF16)
    wu = wu_ref[0].astype(BF16)
    wd = wd_ref[0].astype(BF16)
    for b in range(nb):
        xe = xe_ref[b, 0]
        a = _dot(xe, wg)
        u = _dot(xe, wu)
        hsw = (a * jax.nn.sigmoid(a) * u).astype(BF16)
        part = _dot(hsw, wd)

        @pl.when(f == 0)
        def _():
            acc_ref[b] = part

        @pl.when(f != 0)
        def _():
            acc_ref[b] += part

    @pl.when(f == pl.num_programs(1) - 1)
    def _():
        lane = lax.broadcasted_iota(jnp.int32, gs_ref.shape[2:], 1)
        mine = lane == pl.program_id(0)
        for b in range(nb):
            gate = jnp.sum(jnp.where(mine, gs_ref[b, 0], 0.0), axis=-1, keepdims=True)
            ye_ref[b, 0] = (acc_ref[b] * gate).astype(BF16)


def _ffn(xe, gs, wg, wu, wd, ft=512):
    b, e, cap, d = xe.shape
    ff = wg.shape[2]
    return pl.pallas_call(
        _ffn_body,
        grid=(e, ff // ft),
        in_specs=[pl.BlockSpec((b, 1, cap, d), lambda ei, fi: (0, ei, 0, 0)),
                  pl.BlockSpec((b, 1, cap, LANES), lambda ei, fi: (0, ei, 0, 0)),
                  pl.BlockSpec((1, d, ft), lambda ei, fi: (ei, 0, fi)),
                  pl.BlockSpec((1, d, ft), lambda ei, fi: (ei, 0, fi)),
                  pl.BlockSpec((1, ft, d), lambda ei, fi: (ei, fi, 0))],
        out_specs=pl.BlockSpec((b, 1, cap, d), lambda ei, fi: (0, ei, 0, 0)),
        out_shape=jax.ShapeDtypeStruct((b, e, cap, d), BF16),
        scratch_shapes=[pltpu.VMEM((b, cap, d), F32)],
        compiler_params=_cparams(("arbitrary", "arbitrary")),
        name="ffn",
    )(xe, gs, wg, wu, wd)


def _scatter_body(bounds_ref, rank_ref, ye_ref, x2_ref, g_ref, out_ref, acc_ref, *, eg):
    w = MOE_WINDOW
    ts = x2_ref.shape[1]
    ne, cap = ye_ref.shape[1], ye_ref.shape[2]
    starts, passes = _moe_windows(bounds_ref, pl.program_id(0) * ne * LANES, ne, pl.program_id(1))
    rank = rank_ref[0]
    acc_ref[...] = x2_ref[0]
    iota_w = lax.broadcasted_iota(jnp.int32, (1, w), 1)

    def one_pass(p, carry):
        for g0 in range(0, ne, eg):
            onehots, rows = [], []
            for e in range(g0, g0 + eg):
                lo = starts[e] + p * w
                phys = pl.multiple_of(jnp.minimum(lo, cap - w), BF16_ROWS)
                slot = (iota_w + phys).astype(F32)
                hit = (rank[:, e:e + 1] == slot) & (slot >= lo.astype(F32))
                onehots.append(jnp.where(hit, 1.0, 0.0).astype(BF16))
                rows.append(ye_ref[0, e, pl.ds(phys, w), :])
            acc_ref[...] += _dot(jnp.concatenate(onehots, axis=1), jnp.concatenate(rows, axis=0))
        return carry

    lax.fori_loop(0, passes, one_pass, 0)
    out_ref[0] = _rms(acc_ref[...], g_ref[...])


def _scatter(bounds, rank_t, ye, x2, g, eg=8):
    b, s, e = rank_t.shape
    cap, d = ye.shape[2], ye.shape[3]
    ts = MOE_TOK_BLOCK
    return pl.pallas_call(
        functools.partial(_scatter_body, eg=eg),
        grid_spec=pltpu.PrefetchScalarGridSpec(
            num_scalar_prefetch=1,
            grid=(b, s // ts),
            in_specs=[pl.BlockSpec((1, ts, e), lambda bi, ti, bnd: (bi, ti, 0)),
                      pl.BlockSpec((1, e, cap, d), lambda bi, ti, bnd: (bi, 0, 0, 0),
                                   pipeline_mode=pl.Buffered(1)),
                      pl.BlockSpec((1, ts, d), lambda bi, ti, bnd: (bi, ti, 0)),
                      pl.BlockSpec((1, d), lambda bi, ti, bnd: (0, 0))],
            out_specs=pl.BlockSpec((1, ts, d), lambda bi, ti, bnd: (bi, ti, 0)),
            scratch_shapes=[pltpu.VMEM((ts, d), F32)],
        ),
        out_shape=jax.ShapeDtypeStruct((b, s, d), F32),
        compiler_params=_cparams(("arbitrary", "arbitrary")),
        name="scatter",
    )(bounds.reshape(-1), rank_t, ye, x2, g)


def kernel(x, mem, ln_mix_g, w_in, hy_conv_w, hy_conv_b, filt_w1, filt_b1, filt_freq1, filt_w2, filt_b2,
           filt_freq2, filt_w3, hy_skip, w_up_hy, w_up_attn, w_gate, b_gate, w_out, ln_x_g, ln_mem_g,
           w_q_x, w_kv_mem, w_o_x, ln_moe_g, w_router, w_e_gate, w_e_up, w_e_down, ln_f_g):
    b, s, d = x.shape
    n = b * s
    x2d = x.reshape(n, d)

    half = HEAD_DIM // 2
    inv = ROPE_THETA ** (-jnp.arange(0, HEAD_DIM, 2, dtype=F32) / HEAD_DIM)
    ang = jnp.arange(s, dtype=F32)[:, None] * inv[None, :]
    lane = np.arange(LANES)
    cos_t = jnp.cos(ang)[:, lane % half]
    sin_t = jnp.sin(ang)[:, lane % half] * jnp.asarray(np.where(lane % HEAD_DIM < half, -1.0, 1.0), F32)[None, :]

    w_all = jnp.concatenate([w_in, w_gate], axis=1).astype(BF16)
    p_hy, *qkv, gates = _proj(x2d, ln_mix_g[None], w_all, b_gate[None], cos_t, sin_t, b, s)

    t_col = jnp.linspace(0.0, 1.0, s, dtype=F32)[:, None]
    grid_col = 2.0 * math.pi * jnp.arange(s, dtype=F32)[:, None] / s
    bands = jnp.linspace(1e-4, HY_BANDS - 1, HY_BANDS, dtype=F32)[None, :]
    feat = _filt(t_col, grid_col, bands, filt_w1, filt_b1, filt_freq1, filt_w2, filt_b2, filt_freq2)
    delta = jnp.linspace(math.log(HY_TARGET) / HY_SLOW_PCT, math.log(HY_TARGET) / HY_FAST_PCT,
                         HY_WIDTH, dtype=F32)[None, :]
    tables = _fft_tables(s)
    khat = _hyfilt(t_col, delta, feat, filt_w3, tables[1], tables[3])
    z = _hyena(p_hy.reshape(b, s, -1), hy_conv_w, hy_conv_b[None], hy_skip, khat, tables)

    outs, lses = [], []
    for g, (window, dil) in enumerate(DIL_PAIRS):
        o_g, l_g = _attn_group(*qkv[3 * g:3 * g + 3], g, window // (2 * dil))
        outs.append(o_g)
        lses.append(l_g)

    kv = _memkv(mem, ln_mem_g[None], w_kv_mem.astype(BF16))
    wr_pad = jnp.pad(w_router, ((0, 0), (0, LANES - N_EXPERTS)))
    wr_hi = wr_pad.astype(BF16)
    wr_lo = (wr_pad - wr_hi.astype(F32)).astype(BF16)
    x2, hm, aff = _merge(x2d, z.reshape(n, HY_WIDTH), outs, lses, gates,
                         w_up_hy.astype(BF16), w_up_attn.astype(BF16), w_out.astype(BF16),
                         ln_x_g[None], w_q_x.astype(BF16), kv, w_o_x.astype(BF16), ln_moe_g[None],
                         wr_hi, wr_lo, s)

    cap = max(1, EC_FACTOR * s // N_EXPERTS)
    aff3 = aff.reshape(b, s, LANES)
    aff_hi = aff3.astype(BF16)
    aff_lo = (aff3 - aff_hi.astype(F32)).astype(BF16)
    rank, bounds = _topk(aff3[:, :, :N_EXPERTS].transpose(0, 2, 1), cap, MOE_TOK_BLOCK)
    xe, gs = _gather(bounds, rank, hm.reshape(b, s, d), aff_hi, aff_lo, cap)
    ye = _ffn(xe, gs, w_e_gate, w_e_up, w_e_down)
    return _scatter(bounds, rank.transpose(0, 2, 1), ye, x2.reshape(b, s, d), ln_f_g[None])
```

```python
import functools
import math

import numpy as np
import jax
import jax.numpy as jnp
from jax import lax
from jax.experimental import pallas as pl
from jax.experimental.pallas import tpu as pltpu

F32 = jnp.float32
BF16 = jnp.bfloat16
HIGHEST = lax.Precision.HIGHEST

EPS = 1e-6
HY_WIDTH = 768
HY_BANDS = 16
HY_FFN = 64
HY_FAST_PCT = 0.3
HY_SLOW_PCT = 1.5
HY_TARGET = 1e-2
HEAD_DIM = 64
HEADS_PER_GROUP = 4
DIL_PAIRS = ((128, 1), (512, 4), (2048, 16))
N_GROUPS = len(DIL_PAIRS)
ATTN_WIDTH = N_GROUPS * HEADS_PER_GROUP * HEAD_DIM
ATTN_OUT = HEADS_PER_GROUP * HEAD_DIM
ROPE_THETA = 10000.0
X_HEADS = 4
X_HEAD_DIM = 128
N_EXPERTS = 16
EC_FACTOR = 2

LANES = 128
SUBLANES = 8
MXU_COLS = 256
VMEM_LIMIT = 56 * 1024 * 1024

FFT_N2 = 128
HY_CB = 128
FFT_UNROLL = 8
MOE_TOK_BLOCK = 512
MOE_WINDOW = 128
BF16_ROWS = 16


def _cparams(sem, vmem=VMEM_LIMIT):
    return pltpu.CompilerParams(dimension_semantics=sem, vmem_limit_bytes=vmem)


def _rms(x, g):
    return x * lax.rsqrt(jnp.mean(x * x, axis=-1, keepdims=True) + EPS) * g


def _dot(a, b):
    return jnp.dot(a, b, preferred_element_type=F32)


def _dot_hi(a, b):
    return jnp.dot(a, b, precision=HIGHEST, preferred_element_type=F32)


def _dot_t(a, b):
    return lax.dot_general(a, b, (((1,), (1,)), ((), ())), preferred_element_type=F32)


def _proj_body(x_ref, g_ref, w_ref, bg_ref, cos_ref, sin_ref, *rest, hyc, aw):
    ng = N_GROUPS
    perm_refs = (None,) + rest[:ng - 1]
    phy_ref = rest[ng - 1]
    qkv_refs = rest[ng:ng + 3 * ng]
    gate_ref = rest[ng + 3 * ng]
    h = _rms(x_ref[...], g_ref[...]).astype(BF16)
    phy_ref[...] = _dot(h, w_ref[:, :hyc])
    cos = cos_ref[...]
    sin = sin_ref[...]
    tm = cos.shape[0]
    lane = lax.broadcasted_iota(jnp.int32, (tm, LANES), 1)
    first = (lane % HEAD_DIM) < (HEAD_DIM // 2)

    def rope(t, scale):
        chunks = []
        for j in range(aw // LANES):
            tj = t[:, j * LANES:(j + 1) * LANES]
            partner = jnp.where(first, pltpu.roll(tj, LANES - HEAD_DIM // 2, 1),
                                pltpu.roll(tj, HEAD_DIM // 2, 1))
            chunks.append(((tj * cos + partner * sin) * scale).astype(BF16))
        return chunks

    gl = ATTN_OUT // LANES
    q = rope(_dot(h, w_ref[:, hyc:hyc + aw]), HEAD_DIM ** -0.5)
    k = rope(_dot(h, w_ref[:, hyc + aw:hyc + 2 * aw]), 1.0)
    vf = _dot(h, w_ref[:, hyc + 2 * aw:hyc + 3 * aw]).astype(BF16)
    v = [vf[:, j * LANES:(j + 1) * LANES] for j in range(aw // LANES)]
    for g in range(ng):
        for i, t in enumerate((q, k, v)):
            out_ref = qkv_refs[3 * g + i]
            tg = jnp.concatenate(t[g * gl:(g + 1) * gl], axis=1)
            if perm_refs[g] is not None:
                tg = _dot(perm_refs[g][...], tg).astype(BF16)
            dil = out_ref.shape[1]
            rows = tm // dil
            for r in range(dil):
                out_ref[0, r] = tg[r * rows:(r + 1) * rows, :]
    gate_ref[...] = jax.nn.sigmoid(_dot(h, w_ref[:, hyc + 3 * aw:]) + bg_ref[...]).astype(BF16)


def _proj(x2d, g, w_all, b_gate, cos_t, sin_t, batch, seq, tm=256):
    n, d = x2d.shape
    hyc = 3 * HY_WIDTH
    aw = ATTN_WIDTH
    gd = w_all.shape[1] - hyc - 3 * aw
    nseq = seq // tm
    row = lambda i: (i, 0)
    const = lambda i: (0, 0)
    perms = []
    for _, dil in DIL_PAIRS[1:]:
        rows = tm // dil
        src = (np.arange(tm) % rows) * dil + np.arange(tm) // rows
        perms.append(jnp.asarray(np.eye(tm)[src], dtype=BF16))
    qkv_specs, qkv_shapes = [], []
    for _, dil in DIL_PAIRS:
        for _ in range(3):
            qkv_specs.append(pl.BlockSpec((1, dil, tm // dil, ATTN_OUT), lambda i: (i // nseq, 0, i % nseq, 0)))
            qkv_shapes.append(jax.ShapeDtypeStruct((batch, dil, seq // dil, ATTN_OUT), BF16))
    return pl.pallas_call(
        functools.partial(_proj_body, hyc=hyc, aw=aw),
        grid=(n // tm,),
        in_specs=[
            pl.BlockSpec((tm, d), row),
            pl.BlockSpec((1, d), const),
            pl.BlockSpec(w_all.shape, const, pipeline_mode=pl.Buffered(1)),
            pl.BlockSpec((1, gd), const),
            pl.BlockSpec((tm, LANES), lambda i: (i % nseq, 0)),
            pl.BlockSpec((tm, LANES), lambda i: (i % nseq, 0)),
        ] + [pl.BlockSpec((tm, tm), const) for _ in perms],
        out_specs=[pl.BlockSpec((tm, hyc), row)] + qkv_specs + [pl.BlockSpec((tm, gd), row)],
        out_shape=[jax.ShapeDtypeStruct((n, hyc), F32)] + qkv_shapes + [jax.ShapeDtypeStruct((n, gd), BF16)],
        compiler_params=_cparams(("arbitrary",)),
        name="proj",
    )(x2d, g, w_all, b_gate, cos_t, sin_t, *perms)


def _filt_body(t_ref, grid_ref, bands_ref, w1t_ref, w1c_ref, w1s_ref, b1_ref, f1_ref,
               w2_ref, b2_ref, f2_ref, feat_ref):
    ang = bands_ref[...] * grid_ref[...]
    pre = (t_ref[...] * w1t_ref[...] + _dot_hi(jnp.cos(ang), w1c_ref[...])
           + _dot_hi(-jnp.sin(ang), w1s_ref[...]))
    h = jnp.sin(f1_ref[...] * (pre + b1_ref[...]))
    feat_ref[...] = jnp.sin(f2_ref[...] * (_dot_hi(h, w2_ref[...]) + b2_ref[...]))


def _filt(t_col, grid_col, bands, w1, b1, f1, w2, b2, f2):
    length = t_col.shape[0]
    nb = bands.shape[1]
    args = (t_col, grid_col, bands, w1[0:1], w1[1:1 + nb], w1[1 + nb:], b1[None], f1[None],
            w2, b2[None], f2[None])
    return pl.pallas_call(
        _filt_body,
        out_shape=jax.ShapeDtypeStruct((length, HY_FFN), F32),
        compiler_params=_cparams(None),
        name="filt",
    )(*args)


def _fft_tables(length):
    n = 2 * length
    n2 = FFT_N2
    n1 = n // n2
    h1 = n1 // 2
    k1 = np.arange(n1)[:, None]
    g_sig = np.zeros((n2, 2 * n1, 2 * h1))
    g_flt = np.zeros((n2, 2 * n1, 2 * h1))
    g_inv = np.zeros((n2, 2 * h1, 2 * n1))
    j = np.arange(h1)[None, :]
    for r in range(n2):
        th = 2 * np.pi * (k1 * (n2 * j + r) % n) / n
        gr, gi = np.cos(th), -np.sin(th)
        g_sig[r, 0::2, :h1] = gr
        g_sig[r, 0::2, h1:] = -gi
        g_sig[r, 1::2, :h1] = gi
        g_sig[r, 1::2, h1:] = gr
        g_flt[r, 0::2, :h1] = gr
        g_flt[r, 1::2, :h1] = gi
        m = n2 * (j + 1) - r
        thb = 2 * np.pi * (k1 * m % n) / n
        live = (m < length).astype(np.float64)
        g_flt[r, 0::2, h1:] = np.cos(thb) * live
        g_flt[r, 1::2, h1:] = np.sin(thb) * live
        wr, wi = (np.cos(th) / n).T, (np.sin(th) / n).T
        g_inv[r, :h1, 0::2] = wr
        g_inv[r, :h1, 1::2] = -wi
        g_inv[r, h1:, 0::2] = wi
        g_inv[r, h1:, 1::2] = wr
    a = np.arange(n2)
    th2 = 2 * np.pi * (np.outer(a, a) % n2) / n2
    c2, s2 = np.cos(th2), np.sin(th2)
    f_fwd = np.block([[c2, s2], [-s2, c2]])
    f_inv = np.block([[c2, -s2], [s2, c2]])
    cast = lambda z: jnp.asarray(z, dtype=F32).astype(BF16)
    return cast(g_sig), cast(g_flt), cast(g_inv), cast(f_fwd), cast(f_inv)


def _fft_stage1(src_a_ref, src_b_ref, tab_ref, work_ref, n1, is_filter):
    n2 = FFT_N2
    grp = SUBLANES

    def body(i, c):
        r0 = pl.multiple_of(i * grp, grp)
        b0 = pl.multiple_of(n2 - grp - r0, grp) if is_filter else r0
        a = pltpu.einshape("jsc->sjc", src_a_ref[:, pl.ds(r0, grp), :])
        b = pltpu.einshape("jsc->sjc", src_b_ref[:, pl.ds(b0, grp), :])
        outs = []
        for s in range(grp):
            rhs = jnp.concatenate([a[s], b[grp - 1 - s] if is_filter else b[s]], axis=0).astype(BF16)
            outs.append(_dot(tab_ref[r0 + s], rhs))
        work_ref[:, pl.ds(r0, grp), :] = pltpu.einshape("sqc->qsc", jnp.stack(outs, axis=0))
        return c

    lax.fori_loop(0, n2 // grp, body, 0, unroll=2)


def _hyfilt_body(t_ref, ts_ref, delta_ref, feat_ref, feats_ref, w3f_ref, w3b_ref, gflt_ref, ffwd_ref, khat_ref,
                 hf_ref, hb_ref, work_ref, *, length, rc):
    n2 = FFT_N2
    n1 = 2 * length // n2
    tiles = rc // n2
    cb = hf_ref.shape[2]

    def gen(i, c):
        r0 = pl.multiple_of(i * rc, rc)
        j0 = pl.multiple_of(i * tiles, tiles)
        delta = jnp.abs(delta_ref[...])
        hf = _dot_hi(feat_ref[pl.ds(r0, rc), :], w3f_ref[...]) * jnp.exp(-t_ref[pl.ds(r0, rc), :] * delta)
        hb = _dot_hi(feats_ref[pl.ds(r0, rc), :], w3b_ref[...]) * jnp.exp(-ts_ref[pl.ds(r0, rc), :] * delta)
        hf_ref[pl.ds(j0, tiles)] = hf.reshape(tiles, n2, cb)
        hb_ref[pl.ds(j0, tiles)] = hb.reshape(tiles, n2, cb)
        return c

    lax.fori_loop(0, length // rc, gen, 0)
    _fft_stage1(hf_ref, hb_ref, gflt_ref, work_ref, n1, True)

    side = MXU_COLS // cb

    def stage2(i, c):
        k1s = [i * side + v for v in range(side)]
        blks = [work_ref[pl.ds(pl.multiple_of(2 * k1, 2), 2)].reshape(2 * n2, cb).astype(BF16) for k1 in k1s]
        x = _dot(ffwd_ref[...], jnp.concatenate(blks, axis=1)).astype(BF16)
        for v, k1 in enumerate(k1s):
            khat_ref[0, pl.ds(pl.multiple_of(k1 * (2 * n2), 2 * n2), 2 * n2), :] = x[:, v * cb:(v + 1) * cb]
        return c

    lax.fori_loop(0, n1 // side, stage2, 0, unroll=FFT_UNROLL // side)


def _hyfilt(t_col, delta, feat, w3, g_flt, f_fwd, rc=512):
    s = t_col.shape[0]
    cb = HY_CB
    ncb = HY_WIDTH // cb
    n2 = FFT_N2
    n1 = 2 * s // n2
    norder = w3.shape[1] // (2 * HY_WIDTH)
    one = pl.Buffered(1)

    def full(a):
        nd = a.ndim
        return pl.BlockSpec(a.shape, lambda o, c: (0,) * nd, pipeline_mode=one)

    shift = lambda a: jnp.concatenate([a[1:], jnp.zeros_like(a[:1])], axis=0)
    return pl.pallas_call(
        functools.partial(_hyfilt_body, length=s, rc=rc),
        grid=(norder, ncb),
        in_specs=[full(t_col), full(t_col),
                  pl.BlockSpec((1, cb), lambda o, c: (0, c)),
                  full(feat), full(feat),
                  pl.BlockSpec((HY_FFN, cb), lambda o, c: (0, 2 * ncb * o + c)),
                  pl.BlockSpec((HY_FFN, cb), lambda o, c: (0, 2 * ncb * o + ncb + c)),
                  full(g_flt), full(f_fwd)],
        out_specs=pl.BlockSpec((1, 2 * n1 * n2, cb), lambda o, c: (o, 0, c)),
        out_shape=jax.ShapeDtypeStruct((norder, 2 * n1 * n2, HY_WIDTH), BF16),
        scratch_shapes=[pltpu.VMEM((n1 // 2, n2, cb), F32), pltpu.VMEM((n1 // 2, n2, cb), F32),
                        pltpu.VMEM((2 * n1, n2, cb), F32)],
        compiler_params=_cparams(("arbitrary", "arbitrary")),
        name="hyfilt",
    )(t_col, shift(t_col), delta, feat, shift(feat), w3, w3, g_flt, f_fwd)


def _hyena_body(pz_ref, pg1_ref, pg2_ref, cwz_ref, cwg1_ref, cwg2_ref, cbz_ref, cbg1_ref, cbg2_ref,
                skip_ref, khat_ref, gsig_ref, ginv_ref, ffwd_ref, finv_ref,
                out_ref,
                za_ref, zb_ref, ya_ref, yb_ref, work_ref, *, length, rc):
    n2 = FFT_N2
    n1 = 2 * length // n2
    h1 = n1 // 2
    tiles = rc // n2
    cb = za_ref.shape[2]
    side = MXU_COLS // cb
    sub = SUBLANES

    def conv3(p_ref, b, r0, w_ref, bias_ref):
        p = p_ref[b, pl.ds(r0, rc), :]
        before = p_ref[b, pl.ds(pl.multiple_of(jnp.maximum(r0 - sub, 0), sub), sub), :][sub - 1:sub, :]
        after = p_ref[b, pl.ds(pl.multiple_of(jnp.minimum(r0 + rc, length - sub), sub), sub), :][0:1, :]
        before = jnp.where(r0 == 0, 0.0, before)
        after = jnp.where(r0 + rc == length, 0.0, after)
        row = lax.broadcasted_iota(jnp.int32, p.shape, 0)
        prev = jnp.where(row == 0, before, pltpu.roll(p, 1, 0))
        nxt = jnp.where(row == rc - 1, after, pltpu.roll(p, rc - 1, 0))
        return prev * w_ref[0:1, :] + p * w_ref[1:2, :] + nxt * w_ref[2:3, :] + bias_ref[...]

    def stage2(o):
        def body(i, c):
            k1s = [i * FFT_UNROLL + u for u in range(FFT_UNROLL)]
            blks = [work_ref[pl.ds(pl.multiple_of(2 * k1, 2), 2)].reshape(2 * n2, cb).astype(BF16) for k1 in k1s]
            outs = []
            for u in range(0, FFT_UNROLL, side):
                x = _dot(ffwd_ref[...], jnp.concatenate(blks[u:u + side], axis=1))
                ys = []
                for v in range(side):
                    k1 = k1s[u + v]
                    kh = khat_ref[o, pl.ds(pl.multiple_of(k1 * (2 * n2), 2 * n2), 2 * n2), :].astype(F32)
                    xr, xi = x[:n2, v * cb:(v + 1) * cb], x[n2:, v * cb:(v + 1) * cb]
                    kr, ki = kh[:n2], kh[n2:]
                    ys.append(jnp.concatenate([xr * kr - xi * ki, xr * ki + xi * kr], axis=0).astype(BF16))
                out = _dot(finv_ref[...], jnp.concatenate(ys, axis=1))
                outs.extend(out[:, v * cb:(v + 1) * cb] for v in range(side))
            for k1, out in zip(k1s, outs):
                work_ref[pl.ds(pl.multiple_of(2 * k1, 2), 2)] = out.reshape(2, n2, cb)
            return c
        lax.fori_loop(0, n1 // FFT_UNROLL, body, 0)

    def stage3():
        grp = SUBLANES

        def body(i, c):
            r0 = pl.multiple_of(i * grp, grp)
            wk = pltpu.einshape("qsc->sqc", work_ref[:, pl.ds(r0, grp), :])
            res = jnp.stack([_dot(ginv_ref[r0 + s], wk[s].astype(BF16)) for s in range(grp)], axis=0)
            ya_ref[:, pl.ds(r0, grp), :] = pltpu.einshape("sjc->jsc", res[:, :h1])
            yb_ref[:, pl.ds(r0, grp), :] = pltpu.einshape("sjc->jsc", res[:, h1:])
            return c
        lax.fori_loop(0, n2 // grp, body, 0, unroll=2)

    def rows3(ref, i):
        return ref.at[pl.ds(pl.multiple_of(i * tiles, tiles), tiles)]

    def load_z(i, c):
        r0 = pl.multiple_of(i * rc, rc)
        rows3(za_ref, i)[...] = conv3(pz_ref, 0, r0, cwz_ref, cbz_ref).reshape(tiles, n2, cb)
        rows3(zb_ref, i)[...] = conv3(pz_ref, 1, r0, cwz_ref, cbz_ref).reshape(tiles, n2, cb)
        return c

    lax.fori_loop(0, length // rc, load_z, 0)
    for o, (pg_ref, cw_ref, cb_ref) in enumerate(((pg1_ref, cwg1_ref, cbg1_ref), (pg2_ref, cwg2_ref, cbg2_ref))):
        _fft_stage1(za_ref, zb_ref, gsig_ref, work_ref, n1, False)
        stage2(o)
        stage3()
        skip = skip_ref[o:o + 1, :]

        def gate(i, c):
            r0 = pl.multiple_of(i * rc, rc)
            rows = pl.ds(r0, rc)
            ya, yb = (rows3(r, i)[...].reshape(rc, cb) for r in (ya_ref, yb_ref))
            za, zb = (rows3(r, i)[...].reshape(rc, cb) for r in (za_ref, zb_ref))
            new_a = conv3(pg_ref, 0, r0, cw_ref, cb_ref) * (ya + skip * za)
            new_b = conv3(pg_ref, 1, r0, cw_ref, cb_ref) * (yb + skip * zb)
            if o == 0:
                rows3(za_ref, i)[...] = new_a.reshape(tiles, n2, cb)
                rows3(zb_ref, i)[...] = new_b.reshape(tiles, n2, cb)
            else:
                out_ref[0, rows, :] = new_a
                out_ref[1, rows, :] = new_b
            return c

        lax.fori_loop(0, length // rc, gate, 0)


def _hyena(p_hy, conv_w, conv_b, skip, khat, tables, rc=512):
    b, s, _ = p_hy.shape
    w = HY_WIDTH
    cb = HY_CB
    ncb = w // cb
    n2 = FFT_N2
    n1 = 2 * s // n2
    g_sig, _, g_inv, f_fwd, f_inv = tables
    one = pl.Buffered(1)

    def pspec(off):
        return pl.BlockSpec((2, s, cb), lambda c, p: (p, 0, off + c), pipeline_mode=one)

    def cwspec(off):
        return pl.BlockSpec((3, cb), lambda c, p: (0, off + c))

    def cbspec(off):
        return pl.BlockSpec((1, cb), lambda c, p: (0, off + c))

    def full(a):
        nd = a.ndim
        return pl.BlockSpec(a.shape, lambda c, p: (0,) * nd, pipeline_mode=one)

    in_specs = [
        pspec(0), pspec(ncb), pspec(2 * ncb),
        cwspec(0), cwspec(ncb), cwspec(2 * ncb),
        cbspec(0), cbspec(ncb), cbspec(2 * ncb),
        pl.BlockSpec((2, cb), lambda c, p: (0, c)),
        pl.BlockSpec((khat.shape[0], 2 * n1 * n2, cb), lambda c, p: (0, 0, c), pipeline_mode=one),
        full(g_sig), full(g_inv), full(f_fwd), full(f_inv),
    ]
    return pl.pallas_call(
        functools.partial(_hyena_body, length=s, rc=rc),
        grid=(ncb, b // 2),
        in_specs=in_specs,
        out_specs=pl.BlockSpec((2, s, cb), lambda c, p: (p, 0, c)),
        out_shape=jax.ShapeDtypeStruct((b, s, w), F32),
        scratch_shapes=[pltpu.VMEM((n1 // 2, n2, cb), F32)] * 4 + [pltpu.VMEM((2 * n1, n2, cb), F32)],
        compiler_params=_cparams(("arbitrary", "arbitrary")),
        name="hyena",
    )(p_hy, p_hy, p_hy, conv_w, conv_w, conv_w, conv_b, conv_b, conv_b,
      skip, khat, g_sig, g_inv, f_fwd, f_inv)


def _attn_body(q_ref, k_ref, v_ref, o_ref, lse_ref, *, n_side, qb):
    lr = q_ref.shape[2]
    kw = qb + 2 * n_side
    nh = HEADS_PER_GROUP
    lane_head = lax.broadcasted_iota(jnp.int32, (qb, ATTN_OUT), 1) // HEAD_DIM
    qi = lax.broadcasted_iota(jnp.int32, (nh * qb, kw), 0) % qb
    ki = lax.broadcasted_iota(jnp.int32, (nh * qb, kw), 1)

    def body(i, c):
        q0 = pl.multiple_of(i * qb, qb)
        w0 = pl.multiple_of(jnp.clip(q0 - n_side, 0, lr - kw), n_side)
        q = q_ref[0, 0, pl.ds(q0, qb), :]
        kwin = k_ref[0, 0, pl.ds(w0, kw), :]
        vwin = v_ref[0, 0, pl.ds(w0, kw), :]
        valid = jnp.abs((ki + w0) - (qi + q0)) <= n_side
        qs = jnp.concatenate([jnp.where(lane_head == h, q, jnp.zeros_like(q)) for h in range(nh)], axis=0)
        s = jnp.where(valid, _dot_t(qs, kwin), -1e30)
        m = jnp.max(s, axis=-1, keepdims=True)
        p = jnp.exp(s - m)
        l = jnp.sum(p, axis=-1, keepdims=True)
        pv = _dot(p.astype(BF16), vwin) / l
        lse_rows = m + jnp.log(l)
        o = jnp.zeros((qb, ATTN_OUT), F32)
        lse = jnp.zeros((qb, ATTN_OUT), F32)
        for h in range(nh):
            mine = lane_head == h
            o = jnp.where(mine, pv[h * qb:(h + 1) * qb], o)
            lse = jnp.where(mine, lse_rows[h * qb:(h + 1) * qb], lse)
        o_ref[0, pl.ds(q0, qb), :] = o
        lse_ref[0, pl.ds(q0, qb), :] = lse
        return c

    lax.fori_loop(0, lr // qb, body, 0, unroll=2)


def _attn_group(q, k, v, g, n_side, qb=128):
    b, dil, lr, _ = q.shape
    in_spec = pl.BlockSpec((1, 1, lr, ATTN_OUT), lambda bi, r: (bi, r, 0, 0))
    out_spec = pl.BlockSpec((1, lr, ATTN_OUT), lambda bi, r: (bi, 0, r))
    o, lse = pl.pallas_call(
        functools.partial(_attn_body, n_side=n_side, qb=qb),
        grid=(b, dil),
        in_specs=[in_spec, in_spec, in_spec],
        out_specs=[out_spec, out_spec],
        out_shape=[jax.ShapeDtypeStruct((b, lr, dil * ATTN_OUT), F32)] * 2,
        compiler_params=_cparams(("arbitrary", "arbitrary")),
        name=f"attn{g}",
    )(q, k, v)
    return o.reshape(b * lr * dil, ATTN_OUT), lse.reshape(b * lr * dil, ATTN_OUT)


def _memkv_body(mem_ref, g_ref, w_ref, kv_ref):
    mn = _rms(mem_ref[0], g_ref[...]).astype(BF16)
    kv_ref[0] = _dot(mn, w_ref[...]).astype(BF16)


def _memkv(mem, g, w_kv):
    b, m, d = mem.shape
    n = w_kv.shape[1]
    return pl.pallas_call(
        _memkv_body,
        grid=(b,),
        in_specs=[pl.BlockSpec((1, m, d), lambda i: (i, 0, 0)),
                  pl.BlockSpec((1, d), lambda i: (0, 0)),
                  pl.BlockSpec((d, n), lambda i: (0, 0))],
        out_specs=pl.BlockSpec((1, m, n), lambda i: (i, 0, 0)),
        out_shape=jax.ShapeDtypeStruct((b, m, n), BF16),
        compiler_params=_cparams(("arbitrary",)),
        name="memkv",
    )(mem, g, w_kv)


def _merge_body(x_ref, z_ref, o0_ref, o1_ref, o2_ref, l0_ref, l1_ref, l2_ref, gate_ref,
                wuh_ref, wua_ref, wout_ref, gx_ref, wq_ref, kv_ref, wo_ref, gm_ref, wrh_ref, wrl_ref,
                x2_ref, hm_ref, aff_ref, *, nsplit):
    rows = x_ref.shape[0] // nsplit
    for part in range(nsplit):
        sl = slice(part * rows, (part + 1) * rows)
        _merge_rows(sl, x_ref, z_ref, o0_ref, o1_ref, o2_ref, l0_ref, l1_ref, l2_ref, gate_ref,
                    wuh_ref, wua_ref, wout_ref, gx_ref, wq_ref, kv_ref, wo_ref, gm_ref, wrh_ref, wrl_ref,
                    x2_ref, hm_ref, aff_ref)


def _merge_rows(sl, x_ref, z_ref, o0_ref, o1_ref, o2_ref, l0_ref, l1_ref, l2_ref, gate_ref,
                wuh_ref, wua_ref, wout_ref, gx_ref, wq_ref, kv_ref, wo_ref, gm_ref, wrh_ref, wrl_ref,
                x2_ref, hm_ref, aff_ref):
    d = x_ref.shape[1]
    l0, l1, l2 = l0_ref[sl, :], l1_ref[sl, :], l2_ref[sl, :]
    mx = jnp.maximum(jnp.maximum(l0, l1), l2)
    e0, e1, e2 = jnp.exp(l0 - mx), jnp.exp(l1 - mx), jnp.exp(l2 - mx)
    attn = (e0 * o0_ref[sl, :] + e1 * o1_ref[sl, :] + e2 * o2_ref[sl, :]) / (e0 + e1 + e2)
    y_hy = _dot(z_ref[sl, :].astype(BF16), wuh_ref[...])
    y_at = _dot(attn.astype(BF16), wua_ref[...])
    gates = gate_ref[sl, :].astype(F32)
    mix = gates[:, :d] * y_hy + gates[:, d:] * y_at
    x1 = x_ref[sl, :] + _dot(mix.astype(BF16), wout_ref[...])

    hx = _rms(x1, gx_ref[...]).astype(BF16)
    qx = _dot(hx, wq_ref[...]).astype(BF16)
    kv = kv_ref[0]
    xw = X_HEADS * X_HEAD_DIM
    heads = []
    for h in range(X_HEADS):
        hc = slice(h * X_HEAD_DIM, (h + 1) * X_HEAD_DIM)
        s = _dot_t(qx[:, hc], kv[:, hc]) * (X_HEAD_DIM ** -0.5)
        m = jnp.max(s, axis=-1, keepdims=True)
        p = jnp.exp(s - m)
        p = p / jnp.sum(p, axis=-1, keepdims=True)
        heads.append(_dot(p.astype(BF16), kv[:, xw + h * X_HEAD_DIM:xw + (h + 1) * X_HEAD_DIM]))
    ox = jnp.concatenate(heads, axis=-1).astype(BF16)
    x2 = x1 + _dot(ox, wo_ref[...])
    x2_ref[sl, :] = x2

    hm = _rms(x2, gm_ref[...])
    hm_hi = hm.astype(BF16)
    hm_lo = (hm - hm_hi.astype(F32)).astype(BF16)
    hm_ref[sl, :] = hm_hi
    logits = _dot(hm_hi, wrh_ref[...]) + _dot(hm_lo, wrh_ref[...]) + _dot(hm_hi, wrl_ref[...])
    lane = lax.broadcasted_iota(jnp.int32, logits.shape, 1)
    logits = jnp.where(lane < N_EXPERTS, logits, -1e30)
    m = jnp.max(logits, axis=-1, keepdims=True)
    p = jnp.exp(logits - m)
    aff_ref[sl, :] = p / jnp.sum(p, axis=-1, keepdims=True)


def _merge(x2d, z2d, outs, lses, gates, wuh, wua, wout, gx, wq, kv, wo, gm, wr_hi, wr_lo, seq, tm=512, nsplit=2):
    n, d = x2d.shape
    nseq = seq // tm
    row = lambda i: (i, 0)
    const = lambda i: (0, 0)

    def rspec(a):
        return pl.BlockSpec((tm, a.shape[1]), row)

    def cspec(a):
        return pl.BlockSpec(a.shape, const)

    in_arrays = [x2d, z2d, *outs, *lses, gates, wuh, wua, wout, gx, wq, kv, wo, gm, wr_hi, wr_lo]
    in_specs = ([rspec(a) for a in in_arrays[:9]] + [cspec(a) for a in in_arrays[9:14]]
                + [pl.BlockSpec((1,) + kv.shape[1:], lambda i: (i // nseq, 0, 0))]
                + [cspec(a) for a in in_arrays[15:]])
    return pl.pallas_call(
        functools.partial(_merge_body, nsplit=nsplit),
        grid=(n // tm,),
        in_specs=in_specs,
        out_specs=[pl.BlockSpec((tm, d), row), pl.BlockSpec((tm, d), row), pl.BlockSpec((tm, LANES), row)],
        out_shape=[jax.ShapeDtypeStruct((n, d), F32), jax.ShapeDtypeStruct((n, d), BF16),
                   jax.ShapeDtypeStruct((n, LANES), F32)],
        compiler_params=_cparams(("arbitrary",)),
        name="merge",
    )(*in_arrays)


def _topk_body(aff_ref, tri_ref, rank_ref, bounds_ref, *, cap, tok_block):
    a = aff_ref[0]
    e, s = a.shape

    def count(mask):
        return jnp.sum(jnp.where(mask, 1.0, 0.0), axis=-1, keepdims=True)

    def as_float(bits):
        return pltpu.bitcast(jnp.broadcast_to(bits, (e, LANES)), F32)[:, 0:1]

    def search(i, thr):
        cand = thr | (jnp.int32(1) << (30 - i))
        return jnp.where(count(a >= as_float(cand)) >= cap, cand, thr)

    thr = as_float(lax.fori_loop(0, 31, search, jnp.zeros((e, 1), jnp.int32)))
    gt = a > thr
    eq = a == thr
    need = cap - count(gt)

    def prefix_excl(mask):
        mf = jnp.where(mask, 1.0, 0.0)
        parts = []
        carry = jnp.zeros((e, 1), F32)
        for c in range(s // LANES):
            blk = mf[:, c * LANES:(c + 1) * LANES]
            inc = _dot(blk.astype(BF16), tri_ref[...])
            parts.append(inc - blk + carry)
            carry = carry + inc[:, LANES - 1:LANES]
        return jnp.concatenate(parts, axis=-1)

    sel = gt | (eq & (prefix_excl(eq) < need))
    excl = prefix_excl(sel)
    rank_ref[0] = jnp.where(sel, excl, -1.0)

    tok = lax.broadcasted_iota(jnp.int32, (e, s), 1)
    lane = lax.broadcasted_iota(jnp.int32, (e, LANES), 1)
    bounds = jnp.zeros((e, LANES), F32)
    for j in range(s // tok_block + 1):
        bounds = jnp.where(lane == j, count(sel & (tok < j * tok_block)), bounds)
    bounds_ref[0] = bounds.astype(jnp.int32)


def _topk(aff_t, cap, tok_block):
    b, e, s = aff_t.shape
    assert s // tok_block + 1 <= LANES
    tri = jnp.asarray(np.triu(np.ones((LANES, LANES))), dtype=BF16)
    return pl.pallas_call(
        functools.partial(_topk_body, cap=cap, tok_block=tok_block),
        grid=(b,),
        in_specs=[pl.BlockSpec((1, e, s), lambda i: (i, 0, 0)),
                  pl.BlockSpec((LANES, LANES), lambda i: (0, 0))],
        out_specs=[pl.BlockSpec((1, e, s), lambda i: (i, 0, 0)),
                   pl.BlockSpec((1, e, LANES), lambda i: (i, 0, 0))],
        out_shape=[jax.ShapeDtypeStruct((b, e, s), F32),
                   jax.ShapeDtypeStruct((b, e, LANES), jnp.int32)],
        compiler_params=_cparams(("arbitrary",)),
        name="topk",
    )(aff_t, tri)


def _moe_windows(bounds_ref, base, ne, step):
    starts = []
    passes = jnp.int32(0)
    for e in range(ne):
        r_lo = bounds_ref[base + e * LANES + step]
        r_hi = bounds_ref[base + e * LANES + step + 1]
        ws = (r_lo // BF16_ROWS) * BF16_ROWS
        starts.append(ws)
        passes = jnp.maximum(passes, (r_hi - ws + MOE_WINDOW - 1) // MOE_WINDOW)
    return starts, passes


def _gather_body(bounds_ref, rank_ref, hm_ref, affh_ref, affl_ref, xe_ref, gs_ref, *, cap, eg):
    w = MOE_WINDOW
    ne, tc = rank_ref.shape[1], rank_ref.shape[2]
    bi = pl.program_id(0)
    ci = pl.program_id(1)

    @pl.when(ci == 0)
    def _():
        xe_ref[...] = jnp.zeros_like(xe_ref)
        gs_ref[...] = jnp.zeros_like(gs_ref)

    starts, passes = _moe_windows(bounds_ref, bi * ne * LANES, ne, ci)
    iota_w = lax.broadcasted_iota(jnp.int32, (w, 1), 0)

    def one_pass(p, carry):
        for g0 in range(0, ne, eg):
            lo = [starts[e] + p * w for e in range(g0, g0 + eg)]
            phys = [pl.multiple_of(jnp.minimum(v, cap - w), BF16_ROWS) for v in lo]
            rk = jnp.concatenate([jnp.broadcast_to(rank_ref[0, e:e + 1, :], (w, tc))
                                  for e in range(g0, g0 + eg)], axis=0)
            slot = jnp.concatenate([iota_w + v for v in phys], axis=0).astype(F32)
            lom = jnp.concatenate([jnp.zeros_like(iota_w) + v for v in lo], axis=0).astype(F32)
            onehot = jnp.where((rk == slot) & (slot >= lom), 1.0, 0.0).astype(BF16)
            res = _dot(onehot, hm_ref[0])
            resg = _dot(onehot, affh_ref[0]) + _dot(onehot, affl_ref[0])
            for k, e in enumerate(range(g0, g0 + eg)):
                rows = pl.ds(phys[k], w)
                xe_ref[0, e, rows, :] = (xe_ref[0, e, rows, :].astype(F32) + res[k * w:(k + 1) * w]).astype(BF16)
                gs_ref[0, e, rows, :] = gs_ref[0, e, rows, :] + resg[k * w:(k + 1) * w]
        return carry

    lax.fori_loop(0, passes, one_pass, 0)


def _gather(bounds, rank, hm, aff_hi, aff_lo, cap, eg=8):
    b, e, s = rank.shape
    d = hm.shape[2]
    tc = MOE_TOK_BLOCK
    one = pl.Buffered(1)
    return pl.pallas_call(
        functools.partial(_gather_body, cap=cap, eg=eg),
        grid_spec=pltpu.PrefetchScalarGridSpec(
            num_scalar_prefetch=1,
            grid=(b, s // tc),
            in_specs=[pl.BlockSpec((1, e, tc), lambda bi, ci, bnd: (bi, 0, ci)),
                      pl.BlockSpec((1, tc, d), lambda bi, ci, bnd: (bi, ci, 0)),
                      pl.BlockSpec((1, tc, LANES), lambda bi, ci, bnd: (bi, ci, 0)),
                      pl.BlockSpec((1, tc, LANES), lambda bi, ci, bnd: (bi, ci, 0))],
            out_specs=[pl.BlockSpec((1, e, cap, d), lambda bi, ci, bnd: (bi, 0, 0, 0), pipeline_mode=one),
                       pl.BlockSpec((1, e, cap, LANES), lambda bi, ci, bnd: (bi, 0, 0, 0), pipeline_mode=one)],
        ),
        out_shape=[jax.ShapeDtypeStruct((b, e, cap, d), BF16),
                   jax.ShapeDtypeStruct((b, e, cap, LANES), F32)],
        compiler_params=_cparams(("arbitrary", "arbitrary")),
        name="gather",
    )(bounds.reshape(-1), rank, hm, aff_hi, aff_lo)


def _ffn_body(xe_ref, gs_ref, wg_ref, wu_ref, wd_ref, ye_ref, acc_ref):
    f = pl.program_id(1)
    nb = xe_ref.shape[0]
    wg = wg_ref[0].astype(BF16)
    wu = wu_ref[0].astype(BF16)
    wd = wd_ref[0].astype(BF16)
    for b in range(nb):
        xe = xe_ref[b, 0]
        a = _dot(xe, wg)
        u = _dot(xe, wu)
        hsw = (a * jax.nn.sigmoid(a) * u).astype(BF16)
        part = _dot(hsw, wd)

        @pl.when(f == 0)
        def _():
            acc_ref[b] = part

        @pl.when(f != 0)
        def _():
            acc_ref[b] += part

    @pl.when(f == pl.num_programs(1) - 1)
    def _():
        lane = lax.broadcasted_iota(jnp.int32, gs_ref.shape[2:], 1)
        mine = lane == pl.program_id(0)
        for b in range(nb):
            gate = jnp.sum(jnp.where(mine, gs_ref[b, 0], 0.0), axis=-1, keepdims=True)
            ye_ref[b, 0] = (acc_ref[b] * gate).astype(BF16)


def _ffn(xe, gs, wg, wu, wd, ft=1024):
    b, e, cap, d = xe.shape
    ff = wg.shape[2]
    one = pl.Buffered(1)
    return pl.pallas_call(
        _ffn_body,
        grid=(e, ff // ft),
        in_specs=[pl.BlockSpec((b, 1, cap, d), lambda ei, fi: (0, ei, 0, 0), pipeline_mode=one),
                  pl.BlockSpec((b, 1, cap, LANES), lambda ei, fi: (0, ei, 0, 0), pipeline_mode=one),
                  pl.BlockSpec((1, d, ft), lambda ei, fi: (ei, 0, fi)),
                  pl.BlockSpec((1, d, ft), lambda ei, fi: (ei, 0, fi)),
                  pl.BlockSpec((1, ft, d), lambda ei, fi: (ei, fi, 0))],
        out_specs=pl.BlockSpec((b, 1, cap, d), lambda ei, fi: (0, ei, 0, 0)),
        out_shape=jax.ShapeDtypeStruct((b, e, cap, d), BF16),
        scratch_shapes=[pltpu.VMEM((b, cap, d), F32)],
        compiler_params=_cparams(("arbitrary", "arbitrary")),
        name="ffn",
    )(xe, gs, wg, wu, wd)


def _scatter_body(bounds_ref, rank_ref, ye_ref, x2_ref, g_ref, out_ref, acc_ref, *, eg):
    w = MOE_WINDOW
    ts = x2_ref.shape[1]
    ne, cap = ye_ref.shape[1], ye_ref.shape[2]
    starts, passes = _moe_windows(bounds_ref, pl.program_id(0) * ne * LANES, ne, pl.program_id(1))
    rank = rank_ref[0]
    acc_ref[...] = x2_ref[0]
    iota_w = lax.broadcasted_iota(jnp.int32, (1, w), 1)

    def one_pass(p, carry):
        for g0 in range(0, ne, eg):
            onehots, rows = [], []
            for e in range(g0, g0 + eg):
                lo = starts[e] + p * w
                phys = pl.multiple_of(jnp.minimum(lo, cap - w), BF16_ROWS)
                slot = (iota_w + phys).astype(F32)
                hit = (rank[:, e:e + 1] == slot) & (slot >= lo.astype(F32))
                onehots.append(jnp.where(hit, 1.0, 0.0).astype(BF16))
                rows.append(ye_ref[0, e, pl.ds(phys, w), :])
            acc_ref[...] += _dot(jnp.concatenate(onehots, axis=1), jnp.concatenate(rows, axis=0))
        return carry

    lax.fori_loop(0, passes, one_pass, 0)
    out_ref[0] = _rms(acc_ref[...], g_ref[...])


def _scatter(bounds, rank_t, ye, x2, g, eg=8):
    b, s, e = rank_t.shape
    cap, d = ye.shape[2], ye.shape[3]
    ts = MOE_TOK_BLOCK
    return pl.pallas_call(
        functools.partial(_scatter_body, eg=eg),
        grid_spec=pltpu.PrefetchScalarGridSpec(
            num_scalar_prefetch=1,
            grid=(b, s // ts),
            in_specs=[pl.BlockSpec((1, ts, e), lambda bi, ti, bnd: (bi, ti, 0)),
                      pl.BlockSpec((1, e, cap, d), lambda bi, ti, bnd: (bi, 0, 0, 0),
                                   pipeline_mode=pl.Buffered(1)),
                      pl.BlockSpec((1, ts, d), lambda bi, ti, bnd: (bi, ti, 0)),
                      pl.BlockSpec((1, d), lambda bi, ti, bnd: (0, 0))],
            out_specs=pl.BlockSpec((1, ts, d), lambda bi, ti, bnd: (bi, ti, 0)),
            scratch_shapes=[pltpu.VMEM((ts, d), F32)],
        ),
        out_shape=jax.ShapeDtypeStruct((b, s, d), F32),
        compiler_params=_cparams(("arbitrary", "arbitrary")),
        name="scatter",
    )(bounds.reshape(-1), rank_t, ye, x2, g)


def kernel(x, mem, ln_mix_g, w_in, hy_conv_w, hy_conv_b, filt_w1, filt_b1, filt_freq1, filt_w2, filt_b2,
           filt_freq2, filt_w3, hy_skip, w_up_hy, w_up_attn, w_gate, b_gate, w_out, ln_x_g, ln_mem_g,
           w_q_x, w_kv_mem, w_o_x, ln_moe_g, w_router, w_e_gate, w_e_up, w_e_down, ln_f_g):
    b, s, d = x.shape
    n = b * s
    x2d = x.reshape(n, d)

    half = HEAD_DIM // 2
    inv = ROPE_THETA ** (-jnp.arange(0, HEAD_DIM, 2, dtype=F32) / HEAD_DIM)
    ang = jnp.arange(s, dtype=F32)[:, None] * inv[None, :]
    lane = np.arange(LANES)
    cos_t = jnp.cos(ang)[:, lane % half]
    sin_t = jnp.sin(ang)[:, lane % half] * jnp.asarray(np.where(lane % HEAD_DIM < half, -1.0, 1.0), F32)[None, :]

    w_all = jnp.concatenate([w_in, w_gate], axis=1).astype(BF16)
    p_hy, *qkv, gates = _proj(x2d, ln_mix_g[None], w_all, b_gate[None], cos_t, sin_t, b, s)

    t_col = jnp.linspace(0.0, 1.0, s, dtype=F32)[:, None]
    grid_col = 2.0 * math.pi * jnp.arange(s, dtype=F32)[:, None] / s
    bands = jnp.linspace(1e-4, HY_BANDS - 1, HY_BANDS, dtype=F32)[None, :]
    feat = _filt(t_col, grid_col, bands, filt_w1, filt_b1, filt_freq1, filt_w2, filt_b2, filt_freq2)
    delta = jnp.linspace(math.log(HY_TARGET) / HY_SLOW_PCT, math.log(HY_TARGET) / HY_FAST_PCT,
                         HY_WIDTH, dtype=F32)[None, :]
    tables = _fft_tables(s)
    khat = _hyfilt(t_col, delta, feat, filt_w3, tables[1], tables[3])
    z = _hyena(p_hy.reshape(b, s, -1), hy_conv_w, hy_conv_b[None], hy_skip, khat, tables)

    outs, lses = [], []
    for g, (window, dil) in enumerate(DIL_PAIRS):
        o_g, l_g = _attn_group(*qkv[3 * g:3 * g + 3], g, window // (2 * dil))
        outs.append(o_g)
        lses.append(l_g)

    kv = _memkv(mem, ln_mem_g[None], w_kv_mem.astype(BF16))
    wr_pad = jnp.pad(w_router, ((0, 0), (0, LANES - N_EXPERTS)))
    wr_hi = wr_pad.astype(BF16)
    wr_lo = (wr_pad - wr_hi.astype(F32)).astype(BF16)
    x2, hm, aff = _merge(x2d, z.reshape(n, HY_WIDTH), outs, lses, gates,
                         w_up_hy.astype(BF16), w_up_attn.astype(BF16), w_out.astype(BF16),
                         ln_x_g[None], w_q_x.astype(BF16), kv, w_o_x.astype(BF16), ln_moe_g[None],
                         wr_hi, wr_lo, s)

    cap = max(1, EC_FACTOR * s // N_EXPERTS)
    aff3 = aff.reshape(b, s, LANES)
    aff_hi = aff3.astype(BF16)
    aff_lo = (aff3 - aff_hi.astype(F32)).astype(BF16)
    rank, bounds = _topk(aff3[:, :, :N_EXPERTS].transpose(0, 2, 1), cap, MOE_TOK_BLOCK)
    xe, gs = _gather(bounds, rank, hm.reshape(b, s, d), aff_hi, aff_lo, cap)
    ye = _ffn(xe, gs, w_e_gate, w_e_up, w_e_down)
    return _scatter(bounds, rank.transpose(0, 2, 1), ye, x2.reshape(b, s, d), ln_f_g[None])
```

```python
import functools
import math

import numpy as np
import jax
import jax.numpy as jnp
from jax import lax
from jax.experimental import pallas as pl
from jax.experimental.pallas import tpu as pltpu

F32 = jnp.float32
BF16 = jnp.bfloat16
HIGHEST = lax.Precision.HIGHEST

EPS = 1e-6
HY_WIDTH = 768
HY_BANDS = 16
HY_FFN = 64
HY_FAST_PCT = 0.3
HY_SLOW_PCT = 1.5
HY_TARGET = 1e-2
HEAD_DIM = 64
HEADS_PER_GROUP = 4
DIL_PAIRS = ((128, 1), (512, 4), (2048, 16))
N_GROUPS = len(DIL_PAIRS)
ATTN_WIDTH = N_GROUPS * HEADS_PER_GROUP * HEAD_DIM
ATTN_OUT = HEADS_PER_GROUP * HEAD_DIM
ROPE_THETA = 10000.0
X_HEADS = 4
X_HEAD_DIM = 128
N_EXPERTS = 16
EC_FACTOR = 2

LANES = 128
SUBLANES = 8
MXU_COLS = 256
VMEM_LIMIT = 56 * 1024 * 1024

FFT_N2 = 128
HY_CB = 128
FFT_UNROLL = 8
MOE_BOUND_STEP = 256
GATHER_TOKENS, GATHER_WINDOW = 256, 64
SCATTER_TOKENS, SCATTER_WINDOW = 512, 128
BF16_ROWS = 16


def _cparams(sem, vmem=VMEM_LIMIT):
    return pltpu.CompilerParams(dimension_semantics=sem, vmem_limit_bytes=vmem)


def _rms(x, g):
    return x * lax.rsqrt(jnp.mean(x * x, axis=-1, keepdims=True) + EPS) * g


def _dot(a, b):
    return jnp.dot(a, b, preferred_element_type=F32)


def _dot_hi(a, b):
    return jnp.dot(a, b, precision=HIGHEST, preferred_element_type=F32)


def _split(a):
    hi = a.astype(BF16)
    return hi, (a - hi.astype(F32)).astype(BF16)


def _dot_split(a, b):
    a_hi, a_lo = _split(a)
    b_hi, b_lo = _split(b)
    return _dot(a_hi, b_hi) + _dot(a_lo, b_hi) + _dot(a_hi, b_lo)


def _dot_t(a, b):
    return lax.dot_general(a, b, (((1,), (1,)), ((), ())), preferred_element_type=F32)


def _proj_body(x_ref, g_ref, w_ref, bg_ref, cos_ref, sin_ref, *rest, hyc, aw):
    ng = N_GROUPS
    perm_refs = (None,) + rest[:ng - 1]
    phy_ref = rest[ng - 1]
    qkv_refs = rest[ng:ng + 3 * ng]
    gate_ref = rest[ng + 3 * ng]
    h = _rms(x_ref[...], g_ref[...]).astype(BF16)
    phy_ref[...] = _dot(h, w_ref[:, :hyc])
    cos = cos_ref[...]
    sin = sin_ref[...]
    tm = cos.shape[0]
    lane = lax.broadcasted_iota(jnp.int32, (tm, LANES), 1)
    first = (lane % HEAD_DIM) < (HEAD_DIM // 2)

    def rope(t, scale):
        chunks = []
        for j in range(aw // LANES):
            tj = t[:, j * LANES:(j + 1) * LANES]
            partner = jnp.where(first, pltpu.roll(tj, LANES - HEAD_DIM // 2, 1),
                                pltpu.roll(tj, HEAD_DIM // 2, 1))
            chunks.append(((tj * cos + partner * sin) * scale).astype(BF16))
        return chunks

    gl = ATTN_OUT // LANES
    q = rope(_dot(h, w_ref[:, hyc:hyc + aw]), HEAD_DIM ** -0.5)
    k = rope(_dot(h, w_ref[:, hyc + aw:hyc + 2 * aw]), 1.0)
    vf = _dot(h, w_ref[:, hyc + 2 * aw:hyc + 3 * aw]).astype(BF16)
    v = [vf[:, j * LANES:(j + 1) * LANES] for j in range(aw // LANES)]
    for g in range(ng):
        for i, t in enumerate((q, k, v)):
            out_ref = qkv_refs[3 * g + i]
            tg = jnp.concatenate(t[g * gl:(g + 1) * gl], axis=1)
            if perm_refs[g] is not None:
                tg = _dot(perm_refs[g][...], tg).astype(BF16)
            dil = out_ref.shape[1]
            rows = tm // dil
            for r in range(dil):
                out_ref[0, r] = tg[r * rows:(r + 1) * rows, :]
    gate_ref[...] = jax.nn.sigmoid(_dot(h, w_ref[:, hyc + 3 * aw:]) + bg_ref[...]).astype(BF16)


def _proj(x2d, g, w_all, b_gate, cos_t, sin_t, batch, seq, tm=256):
    n, d = x2d.shape
    hyc = 3 * HY_WIDTH
    aw = ATTN_WIDTH
    gd = w_all.shape[1] - hyc - 3 * aw
    nseq = seq // tm
    row = lambda i: (i, 0)
    const = lambda i: (0, 0)
    perms = []
    for _, dil in DIL_PAIRS[1:]:
        rows = tm // dil
        src = (np.arange(tm) % rows) * dil + np.arange(tm) // rows
        perms.append(jnp.asarray(np.eye(tm)[src], dtype=BF16))
    qkv_specs, qkv_shapes = [], []
    for _, dil in DIL_PAIRS:
        for _ in range(3):
            qkv_specs.append(pl.BlockSpec((1, dil, tm // dil, ATTN_OUT), lambda i: (i // nseq, 0, i % nseq, 0)))
            qkv_shapes.append(jax.ShapeDtypeStruct((batch, dil, seq // dil, ATTN_OUT), BF16))
    return pl.pallas_call(
        functools.partial(_proj_body, hyc=hyc, aw=aw),
        grid=(n // tm,),
        in_specs=[
            pl.BlockSpec((tm, d), row),
            pl.BlockSpec((1, d), const),
            pl.BlockSpec(w_all.shape, const, pipeline_mode=pl.Buffered(1)),
            pl.BlockSpec((1, gd), const),
            pl.BlockSpec((tm, LANES), lambda i: (i % nseq, 0)),
            pl.BlockSpec((tm, LANES), lambda i: (i % nseq, 0)),
        ] + [pl.BlockSpec((tm, tm), const) for _ in perms],
        out_specs=[pl.BlockSpec((tm, hyc), row)] + qkv_specs + [pl.BlockSpec((tm, gd), row)],
        out_shape=[jax.ShapeDtypeStruct((n, hyc), F32)] + qkv_shapes + [jax.ShapeDtypeStruct((n, gd), BF16)],
        compiler_params=_cparams(("arbitrary",)),
        name="proj",
    )(x2d, g, w_all, b_gate, cos_t, sin_t, *perms)


def _filt_body(t_ref, grid_ref, bands_ref, w1t_ref, w1c_ref, w1s_ref, b1_ref, f1_ref,
               w2_ref, b2_ref, f2_ref, feat_ref):
    ang = bands_ref[...] * grid_ref[...]
    pre = (t_ref[...] * w1t_ref[...] + _dot_hi(jnp.cos(ang), w1c_ref[...])
           + _dot_hi(-jnp.sin(ang), w1s_ref[...]))
    h = jnp.sin(f1_ref[...] * (pre + b1_ref[...]))
    feat_ref[...] = jnp.sin(f2_ref[...] * (_dot_hi(h, w2_ref[...]) + b2_ref[...]))


def _filt(t_col, grid_col, bands, w1, b1, f1, w2, b2, f2):
    length = t_col.shape[0]
    nb = bands.shape[1]
    args = (t_col, grid_col, bands, w1[0:1], w1[1:1 + nb], w1[1 + nb:], b1[None], f1[None],
            w2, b2[None], f2[None])
    return pl.pallas_call(
        _filt_body,
        out_shape=jax.ShapeDtypeStruct((length, HY_FFN), F32),
        compiler_params=_cparams(None),
        name="filt",
    )(*args)


def _fft_tables(length):
    n = 2 * length
    n2 = FFT_N2
    n1 = n // n2
    h1 = n1 // 2
    k1 = np.arange(n1)[:, None]
    g_sig = np.zeros((n2, 2 * n1, 2 * h1))
    g_flt = np.zeros((n2, 2 * n1, 2 * h1))
    g_inv = np.zeros((n2, 2 * h1, 2 * n1))
    j = np.arange(h1)[None, :]
    for r in range(n2):
        th = 2 * np.pi * (k1 * (n2 * j + r) % n) / n
        gr, gi = np.cos(th), -np.sin(th)
        g_sig[r, 0::2, :h1] = gr
        g_sig[r, 0::2, h1:] = -gi
        g_sig[r, 1::2, :h1] = gi
        g_sig[r, 1::2, h1:] = gr
        g_flt[r, 0::2, :h1] = gr
        g_flt[r, 1::2, :h1] = gi
        m = n2 * (j + 1) - r
        thb = 2 * np.pi * (k1 * m % n) / n
        live = (m < length).astype(np.float64)
        g_flt[r, 0::2, h1:] = np.cos(thb) * live
        g_flt[r, 1::2, h1:] = np.sin(thb) * live
        wr, wi = (np.cos(th) / n).T, (np.sin(th) / n).T
        g_inv[r, :h1, 0::2] = wr
        g_inv[r, :h1, 1::2] = -wi
        g_inv[r, h1:, 0::2] = wi
        g_inv[r, h1:, 1::2] = wr
    a = np.arange(n2)
    th2 = 2 * np.pi * (np.outer(a, a) % n2) / n2
    c2, s2 = np.cos(th2), np.sin(th2)
    f_fwd = np.block([[c2, s2], [-s2, c2]])
    f_inv = np.block([[c2, -s2], [s2, c2]])
    cast = lambda z: jnp.asarray(z, dtype=F32).astype(BF16)
    return cast(g_sig), cast(g_flt), cast(g_inv), cast(f_fwd), cast(f_inv)


def _fft_stage1(src_a_ref, src_b_ref, tab_ref, work_ref, n1, is_filter):
    n2 = FFT_N2
    grp = SUBLANES

    def body(i, c):
        r0 = pl.multiple_of(i * grp, grp)
        b0 = pl.multiple_of(n2 - grp - r0, grp) if is_filter else r0
        a = jnp.swapaxes(src_a_ref[:, pl.ds(r0, grp), :], 0, 1)
        b = jnp.swapaxes(src_b_ref[:, pl.ds(b0, grp), :], 0, 1)
        outs = []
        for s in range(grp):
            rhs = jnp.concatenate([a[s], b[grp - 1 - s] if is_filter else b[s]], axis=0).astype(BF16)
            outs.append(_dot(tab_ref[r0 + s], rhs))
        work_ref[:, pl.ds(r0, grp), :] = jnp.swapaxes(jnp.stack(outs, axis=0), 0, 1)
        return c

    lax.fori_loop(0, n2 // grp, body, 0, unroll=2)


def _hyfilt_body(t_ref, ts_ref, delta_ref, feat_ref, feats_ref, w3f_ref, w3b_ref, gflt_ref, ffwd_ref, khat_ref,
                 hf_ref, hb_ref, work_ref, *, length, rc):
    n2 = FFT_N2
    n1 = 2 * length // n2
    tiles = rc // n2
    cb = hf_ref.shape[2]

    def gen(i, c):
        r0 = pl.multiple_of(i * rc, rc)
        j0 = pl.multiple_of(i * tiles, tiles)
        delta = jnp.abs(delta_ref[...])
        hf = _dot_split(feat_ref[pl.ds(r0, rc), :], w3f_ref[...]) * jnp.exp(-t_ref[pl.ds(r0, rc), :] * delta)
        hb = _dot_split(feats_ref[pl.ds(r0, rc), :], w3b_ref[...]) * jnp.exp(-ts_ref[pl.ds(r0, rc), :] * delta)
        hf_ref[pl.ds(j0, tiles)] = hf.reshape(tiles, n2, cb)
        hb_ref[pl.ds(j0, tiles)] = hb.reshape(tiles, n2, cb)
        return c

    lax.fori_loop(0, length // rc, gen, 0)
    _fft_stage1(hf_ref, hb_ref, gflt_ref, work_ref, n1, True)

    side = MXU_COLS // cb

    def stage2(i, c):
        k1s = [i * side + v for v in range(side)]
        blks = [work_ref[pl.ds(pl.multiple_of(2 * k1, 2), 2)].reshape(2 * n2, cb).astype(BF16) for k1 in k1s]
        x = _dot(ffwd_ref[...], jnp.concatenate(blks, axis=1)).astype(BF16)
        for v, k1 in enumerate(k1s):
            khat_ref[0, pl.ds(pl.multiple_of(k1 * (2 * n2), 2 * n2), 2 * n2), :] = x[:, v * cb:(v + 1) * cb]
        return c

    lax.fori_loop(0, n1 // side, stage2, 0, unroll=FFT_UNROLL // side)


def _hyfilt(t_col, delta, feat, w3, g_flt, f_fwd, rc=512):
    s = t_col.shape[0]
    cb = HY_CB
    ncb = HY_WIDTH // cb
    n2 = FFT_N2
    n1 = 2 * s // n2
    norder = w3.shape[1] // (2 * HY_WIDTH)
    one = pl.Buffered(1)

    def full(a):
        nd = a.ndim
        return pl.BlockSpec(a.shape, lambda o, c: (0,) * nd, pipeline_mode=one)

    shift = lambda a: jnp.concatenate([a[1:], jnp.zeros_like(a[:1])], axis=0)
    return pl.pallas_call(
        functools.partial(_hyfilt_body, length=s, rc=rc),
        grid=(norder, ncb),
        in_specs=[full(t_col), full(t_col),
                  pl.BlockSpec((1, cb), lambda o, c: (0, c)),
                  full(feat), full(feat),
                  pl.BlockSpec((HY_FFN, cb), lambda o, c: (0, 2 * ncb * o + c)),
                  pl.BlockSpec((HY_FFN, cb), lambda o, c: (0, 2 * ncb * o + ncb + c)),
                  full(g_flt), full(f_fwd)],
        out_specs=pl.BlockSpec((1, 2 * n1 * n2, cb), lambda o, c: (o, 0, c)),
        out_shape=jax.ShapeDtypeStruct((norder, 2 * n1 * n2, HY_WIDTH), BF16),
        scratch_shapes=[pltpu.VMEM((n1 // 2, n2, cb), F32), pltpu.VMEM((n1 // 2, n2, cb), F32),
                        pltpu.VMEM((2 * n1, n2, cb), F32)],
        compiler_params=_cparams(("arbitrary", "arbitrary")),
        name="hyfilt",
    )(t_col, shift(t_col), delta, feat, shift(feat), w3, w3, g_flt, f_fwd)


def _hyena_body(pz_ref, pg1_ref, pg2_ref, cwz_ref, cwg1_ref, cwg2_ref, cbz_ref, cbg1_ref, cbg2_ref,
                skip_ref, khat_ref, gsig_ref, ginv_ref, ffwd_ref, finv_ref,
                out_ref,
                za_ref, zb_ref, ya_ref, yb_ref, work_ref, *, length, rc):
    n2 = FFT_N2
    n1 = 2 * length // n2
    h1 = n1 // 2
    tiles = rc // n2
    cb = za_ref.shape[2]
    side = MXU_COLS // cb
    sub = SUBLANES

    def conv3(p_ref, b, r0, w_ref, bias_ref):
        p = p_ref[b, pl.ds(r0, rc), :]
        before = p_ref[b, pl.ds(pl.multiple_of(jnp.maximum(r0 - sub, 0), sub), sub), :][sub - 1:sub, :]
        after = p_ref[b, pl.ds(pl.multiple_of(jnp.minimum(r0 + rc, length - sub), sub), sub), :][0:1, :]
        before = jnp.where(r0 == 0, 0.0, before)
        after = jnp.where(r0 + rc == length, 0.0, after)
        row = lax.broadcasted_iota(jnp.int32, p.shape, 0)
        prev = jnp.where(row == 0, before, pltpu.roll(p, 1, 0))
        nxt = jnp.where(row == rc - 1, after, pltpu.roll(p, rc - 1, 0))
        return prev * w_ref[0:1, :] + p * w_ref[1:2, :] + nxt * w_ref[2:3, :] + bias_ref[...]

    def stage2(o):
        def body(i, c):
            k1s = [i * FFT_UNROLL + u for u in range(FFT_UNROLL)]
            blks = [work_ref[pl.ds(pl.multiple_of(2 * k1, 2), 2)].reshape(2 * n2, cb).astype(BF16) for k1 in k1s]
            outs = []
            for u in range(0, FFT_UNROLL, side):
                x = _dot(ffwd_ref[...], jnp.concatenate(blks[u:u + side], axis=1))
                ys = []
                for v in range(side):
                    k1 = k1s[u + v]
                    kh = khat_ref[o, pl.ds(pl.multiple_of(k1 * (2 * n2), 2 * n2), 2 * n2), :].astype(F32)
                    xr, xi = x[:n2, v * cb:(v + 1) * cb], x[n2:, v * cb:(v + 1) * cb]
                    kr, ki = kh[:n2], kh[n2:]
                    ys.append(jnp.concatenate([xr * kr - xi * ki, xr * ki + xi * kr], axis=0).astype(BF16))
                out = _dot(finv_ref[...], jnp.concatenate(ys, axis=1))
                outs.extend(out[:, v * cb:(v + 1) * cb] for v in range(side))
            for k1, out in zip(k1s, outs):
                work_ref[pl.ds(pl.multiple_of(2 * k1, 2), 2)] = out.reshape(2, n2, cb)
            return c
        lax.fori_loop(0, n1 // FFT_UNROLL, body, 0)

    def stage3():
        grp = SUBLANES

        def body(i, c):
            r0 = pl.multiple_of(i * grp, grp)
            wk = jnp.swapaxes(work_ref[:, pl.ds(r0, grp), :], 0, 1)
            res = jnp.stack([_dot(ginv_ref[r0 + s], wk[s].astype(BF16)) for s in range(grp)], axis=0)
            ya_ref[:, pl.ds(r0, grp), :] = jnp.swapaxes(res[:, :h1], 0, 1)
            yb_ref[:, pl.ds(r0, grp), :] = jnp.swapaxes(res[:, h1:], 0, 1)
            return c
        lax.fori_loop(0, n2 // grp, body, 0, unroll=2)

    def rows3(ref, i):
        return ref.at[pl.ds(pl.multiple_of(i * tiles, tiles), tiles)]

    def load_z(i, c):
        r0 = pl.multiple_of(i * rc, rc)
        rows3(za_ref, i)[...] = conv3(pz_ref, 0, r0, cwz_ref, cbz_ref).reshape(tiles, n2, cb)
        rows3(zb_ref, i)[...] = conv3(pz_ref, 1, r0, cwz_ref, cbz_ref).reshape(tiles, n2, cb)
        return c

    lax.fori_loop(0, length // rc, load_z, 0)
    for o, (pg_ref, cw_ref, cb_ref) in enumerate(((pg1_ref, cwg1_ref, cbg1_ref), (pg2_ref, cwg2_ref, cbg2_ref))):
        _fft_stage1(za_ref, zb_ref, gsig_ref, work_ref, n1, False)
        stage2(o)
        stage3()
        skip = skip_ref[o:o + 1, :]

        def gate(i, c):
            r0 = pl.multiple_of(i * rc, rc)
            rows = pl.ds(r0, rc)
            ya, yb = (rows3(r, i)[...].reshape(rc, cb) for r in (ya_ref, yb_ref))
            za, zb = (rows3(r, i)[...].reshape(rc, cb) for r in (za_ref, zb_ref))
            new_a = conv3(pg_ref, 0, r0, cw_ref, cb_ref) * (ya + skip * za)
            new_b = conv3(pg_ref, 1, r0, cw_ref, cb_ref) * (yb + skip * zb)
            if o == 0:
                rows3(za_ref, i)[...] = new_a.reshape(tiles, n2, cb)
                rows3(zb_ref, i)[...] = new_b.reshape(tiles, n2, cb)
            else:
                out_ref[0, rows, :] = new_a
                out_ref[1, rows, :] = new_b
            return c

        lax.fori_loop(0, length // rc, gate, 0)


def _hyena(p_hy, conv_w, conv_b, skip, khat, tables, rc=512):
    b, s, _ = p_hy.shape
    w = HY_WIDTH
    cb = HY_CB
    ncb = w // cb
    n2 = FFT_N2
    n1 = 2 * s // n2
    g_sig, _, g_inv, f_fwd, f_inv = tables
    one = pl.Buffered(1)

    def pspec(off):
        return pl.BlockSpec((2, s, cb), lambda c, p: (p, 0, off + c), pipeline_mode=one)

    def cwspec(off):
        return pl.BlockSpec((3, cb), lambda c, p: (0, off + c))

    def cbspec(off):
        return pl.BlockSpec((1, cb), lambda c, p: (0, off + c))

    def full(a):
        nd = a.ndim
        return pl.BlockSpec(a.shape, lambda c, p: (0,) * nd, pipeline_mode=one)

    in_specs = [
        pspec(0), pspec(ncb), pspec(2 * ncb),
        cwspec(0), cwspec(ncb), cwspec(2 * ncb),
        cbspec(0), cbspec(ncb), cbspec(2 * ncb),
        pl.BlockSpec((2, cb), lambda c, p: (0, c)),
        pl.BlockSpec((khat.shape[0], 2 * n1 * n2, cb), lambda c, p: (0, 0, c), pipeline_mode=one),
        full(g_sig), full(g_inv), full(f_fwd), full(f_inv),
    ]
    return pl.pallas_call(
        functools.partial(_hyena_body, length=s, rc=rc),
        grid=(ncb, b // 2),
        in_specs=in_specs,
        out_specs=pl.BlockSpec((2, s, cb), lambda c, p: (p, 0, c)),
        out_shape=jax.ShapeDtypeStruct((b, s, w), F32),
        scratch_shapes=[pltpu.VMEM((n1 // 2, n2, cb), F32)] * 4 + [pltpu.VMEM((2 * n1, n2, cb), F32)],
        compiler_params=_cparams(("arbitrary", "arbitrary")),
        name="hyena",
    )(p_hy, p_hy, p_hy, conv_w, conv_w, conv_w, conv_b, conv_b, conv_b,
      skip, khat, g_sig, g_inv, f_fwd, f_inv)


def _attn_body(q_ref, k_ref, v_ref, o_ref, lse_ref, *, n_side, qb):
    lr = q_ref.shape[2]
    kw = qb + 2 * n_side
    nh = HEADS_PER_GROUP
    lane_head = lax.broadcasted_iota(jnp.int32, (qb, ATTN_OUT), 1) // HEAD_DIM
    qi = lax.broadcasted_iota(jnp.int32, (nh * qb, kw), 0) % qb
    ki = lax.broadcasted_iota(jnp.int32, (nh * qb, kw), 1)

    def body(i, c):
        q0 = pl.multiple_of(i * qb, qb)
        w0 = pl.multiple_of(jnp.clip(q0 - n_side, 0, lr - kw), n_side)
        q = q_ref[0, 0, pl.ds(q0, qb), :]
        kwin = k_ref[0, 0, pl.ds(w0, kw), :]
        vwin = v_ref[0, 0, pl.ds(w0, kw), :]
        valid = jnp.abs((ki + w0) - (qi + q0)) <= n_side
        qs = jnp.concatenate([jnp.where(lane_head == h, q, jnp.zeros_like(q)) for h in range(nh)], axis=0)
        s = jnp.where(valid, _dot_t(qs, kwin), -1e30)
        m = jnp.max(s, axis=-1, keepdims=True)
        p = jnp.exp(s - m)
        l = jnp.sum(p, axis=-1, keepdims=True)
        pv = _dot(p.astype(BF16), vwin) / l
        lse_rows = m + jnp.log(l)
        o = jnp.zeros((qb, ATTN_OUT), F32)
        lse = jnp.zeros((qb, ATTN_OUT), F32)
        for h in range(nh):
            mine = lane_head == h
            o = jnp.where(mine, pv[h * qb:(h + 1) * qb], o)
            lse = jnp.where(mine, lse_rows[h * qb:(h + 1) * qb], lse)
        o_ref[0, pl.ds(q0, qb), :] = o
        lse_ref[0, pl.ds(q0, qb), :] = lse
        return c

    lax.fori_loop(0, lr // qb, body, 0, unroll=2)


def _attn_group(q, k, v, g, n_side, qb=128):
    b, dil, lr, _ = q.shape
    in_spec = pl.BlockSpec((1, 1, lr, ATTN_OUT), lambda bi, r: (bi, r, 0, 0))
    out_spec = pl.BlockSpec((1, lr, ATTN_OUT), lambda bi, r: (bi, 0, r))
    o, lse = pl.pallas_call(
        functools.partial(_attn_body, n_side=n_side, qb=qb),
        grid=(b, dil),
        in_specs=[in_spec, in_spec, in_spec],
        out_specs=[out_spec, out_spec],
        out_shape=[jax.ShapeDtypeStruct((b, lr, dil * ATTN_OUT), F32)] * 2,
        compiler_params=_cparams(("arbitrary", "arbitrary")),
        name=f"attn{g}",
    )(q, k, v)
    return o, lse


def _memkv_body(mem_ref, g_ref, w_ref, kv_ref):
    mn = _rms(mem_ref[0], g_ref[...]).astype(BF16)
    kv_ref[0] = _dot(mn, w_ref[...]).astype(BF16)


def _memkv(mem, g, w_kv):
    b, m, d = mem.shape
    n = w_kv.shape[1]
    return pl.pallas_call(
        _memkv_body,
        grid=(b,),
        in_specs=[pl.BlockSpec((1, m, d), lambda i: (i, 0, 0)),
                  pl.BlockSpec((1, d), lambda i: (0, 0)),
                  pl.BlockSpec((d, n), lambda i: (0, 0))],
        out_specs=pl.BlockSpec((1, m, n), lambda i: (i, 0, 0)),
        out_shape=jax.ShapeDtypeStruct((b, m, n), BF16),
        compiler_params=_cparams(("arbitrary",)),
        name="memkv",
    )(mem, g, w_kv)


def _merge_body(x_ref, z_ref, o0_ref, o1_ref, o2_ref, l0_ref, l1_ref, l2_ref, gate_ref,
                wuh_ref, wua_ref, wout_ref, gx_ref, wq_ref, kv_ref, wo_ref, gm_ref, wrh_ref, wrl_ref,
                p1_ref, p2_ref, x2_ref, hm_ref, aff_ref, *, nsplit):
    rows = x_ref.shape[0] // nsplit
    for part in range(nsplit):
        sl = slice(part * rows, (part + 1) * rows)

        def natural(ref, perm_ref):
            dil = ref.shape[2] // ATTN_OUT
            rr = rows // dil
            blk = ref[0, part * rr:(part + 1) * rr, :]
            if dil == 1:
                return blk
            stacked = jnp.concatenate([blk[:, r * ATTN_OUT:(r + 1) * ATTN_OUT] for r in range(dil)], axis=0)
            hi = stacked.astype(BF16)
            lo = (stacked - hi.astype(F32)).astype(BF16)
            return _dot(perm_ref[...], hi) + _dot(perm_ref[...], lo)

        outs = [natural(r, p) for r, p in ((o0_ref, None), (o1_ref, p1_ref), (o2_ref, p2_ref))]
        lses = [natural(r, p) for r, p in ((l0_ref, None), (l1_ref, p1_ref), (l2_ref, p2_ref))]
        _merge_rows(sl, outs, lses, x_ref, z_ref, gate_ref,
                    wuh_ref, wua_ref, wout_ref, gx_ref, wq_ref, kv_ref, wo_ref, gm_ref, wrh_ref, wrl_ref,
                    x2_ref, hm_ref, aff_ref)


def _merge_rows(sl, outs, lses, x_ref, z_ref, gate_ref,
                wuh_ref, wua_ref, wout_ref, gx_ref, wq_ref, kv_ref, wo_ref, gm_ref, wrh_ref, wrl_ref,
                x2_ref, hm_ref, aff_ref):
    d = x_ref.shape[1]
    l0, l1, l2 = lses
    mx = jnp.maximum(jnp.maximum(l0, l1), l2)
    e0, e1, e2 = jnp.exp(l0 - mx), jnp.exp(l1 - mx), jnp.exp(l2 - mx)
    attn = (e0 * outs[0] + e1 * outs[1] + e2 * outs[2]) / (e0 + e1 + e2)
    y_hy = _dot(z_ref[sl, :].astype(BF16), wuh_ref[...])
    y_at = _dot(attn.astype(BF16), wua_ref[...])
    gates = gate_ref[sl, :].astype(F32)
    mix = gates[:, :d] * y_hy + gates[:, d:] * y_at
    x1 = x_ref[sl, :] + _dot(mix.astype(BF16), wout_ref[...])

    hx = _rms(x1, gx_ref[...]).astype(BF16)
    qx = _dot(hx, wq_ref[...]).astype(BF16)
    kv = kv_ref[0]
    xw = X_HEADS * X_HEAD_DIM
    heads = []
    for h in range(X_HEADS):
        hc = slice(h * X_HEAD_DIM, (h + 1) * X_HEAD_DIM)
        s = _dot_t(qx[:, hc], kv[:, hc]) * (X_HEAD_DIM ** -0.5)
        m = jnp.max(s, axis=-1, keepdims=True)
        p = jnp.exp(s - m)
        p = p / jnp.sum(p, axis=-1, keepdims=True)
        heads.append(_dot(p.astype(BF16), kv[:, xw + h * X_HEAD_DIM:xw + (h + 1) * X_HEAD_DIM]))
    ox = jnp.concatenate(heads, axis=-1).astype(BF16)
    x2 = x1 + _dot(ox, wo_ref[...])
    x2_ref[sl, :] = x2

    hm = _rms(x2, gm_ref[...])
    hm_hi = hm.astype(BF16)
    hm_lo = (hm - hm_hi.astype(F32)).astype(BF16)
    hm_ref[sl, :] = hm_hi
    logits = _dot(hm_hi, wrh_ref[...]) + _dot(hm_lo, wrh_ref[...]) + _dot(hm_hi, wrl_ref[...])
    lane = lax.broadcasted_iota(jnp.int32, logits.shape, 1)
    logits = jnp.where(lane < N_EXPERTS, logits, -1e30)
    m = jnp.max(logits, axis=-1, keepdims=True)
    p = jnp.exp(logits - m)
    aff_ref[sl, :] = p / jnp.sum(p, axis=-1, keepdims=True)


def _merge(x2d, z2d, outs, lses, gates, wuh, wua, wout, gx, wq, kv, wo, gm, wr_hi, wr_lo, seq, tm=512, nsplit=2):
    n, d = x2d.shape
    nseq = seq // tm
    row = lambda i: (i, 0)
    const = lambda i: (0, 0)

    def rspec(a):
        return pl.BlockSpec((tm, a.shape[1]), row)

    def aspec(a):
        dil = a.shape[2] // ATTN_OUT
        return pl.BlockSpec((1, tm // dil, a.shape[2]), lambda i: (i // nseq, i % nseq, 0))

    def cspec(a):
        return pl.BlockSpec(a.shape, const)

    rows = tm // nsplit
    perms = []
    for a in outs[1:]:
        dil = a.shape[2] // ATTN_OUT
        src = (np.arange(rows) % dil) * (rows // dil) + np.arange(rows) // dil
        perms.append(jnp.asarray(np.eye(rows)[src], dtype=BF16))

    in_arrays = [x2d, z2d, *outs, *lses, gates, wuh, wua, wout, gx, wq, kv, wo, gm, wr_hi, wr_lo, *perms]
    in_specs = ([rspec(a) for a in in_arrays[:2]] + [aspec(a) for a in in_arrays[2:8]] + [rspec(gates)]
                + [cspec(a) for a in in_arrays[9:14]]
                + [pl.BlockSpec((1,) + kv.shape[1:], lambda i: (i // nseq, 0, 0))]
                + [cspec(a) for a in in_arrays[15:]])
    return pl.pallas_call(
        functools.partial(_merge_body, nsplit=nsplit),
        grid=(n // tm,),
        in_specs=in_specs,
        out_specs=[pl.BlockSpec((tm, d), row), pl.BlockSpec((tm, d), row), pl.BlockSpec((tm, LANES), row)],
        out_shape=[jax.ShapeDtypeStruct((n, d), F32), jax.ShapeDtypeStruct((n, d), BF16),
                   jax.ShapeDtypeStruct((n, LANES), F32)],
        compiler_params=_cparams(("arbitrary",)),
        name="merge",
    )(*in_arrays)


def _topk_body(aff_ref, tri_ref, rank_ref, bounds_ref, *, cap, tok_block):
    a = aff_ref[0]
    e, s = a.shape

    def count(mask):
        return jnp.sum(jnp.where(mask, 1.0, 0.0), axis=-1, keepdims=True)

    def as_float(bits):
        return pltpu.bitcast(jnp.broadcast_to(bits, (e, LANES)), F32)[:, 0:1]

    def search(i, thr):
        cand = thr | (jnp.int32(1) << (30 - i))
        return jnp.where(count(a >= as_float(cand)) >= cap, cand, thr)

    thr = as_float(lax.fori_loop(0, 31, search, jnp.zeros((e, 1), jnp.int32)))
    gt = a > thr
    eq = a == thr
    need = cap - count(gt)

    def prefix_excl(mask):
        mf = jnp.where(mask, 1.0, 0.0)
        parts = []
        carry = jnp.zeros((e, 1), F32)
        for c in range(s // LANES):
            blk = mf[:, c * LANES:(c + 1) * LANES]
            inc = _dot(blk.astype(BF16), tri_ref[...])
            parts.append(inc - blk + carry)
            carry = carry + inc[:, LANES - 1:LANES]
        return jnp.concatenate(parts, axis=-1)

    sel = gt | (eq & (prefix_excl(eq) < need))
    excl = prefix_excl(sel)
    rank_ref[0] = jnp.where(sel, excl, -1.0)

    tok = lax.broadcasted_iota(jnp.int32, (e, s), 1)
    lane = lax.broadcasted_iota(jnp.int32, (e, LANES), 1)
    bounds = jnp.zeros((e, LANES), F32)
    for j in range(s // tok_block + 1):
        bounds = jnp.where(lane == j, count(sel & (tok < j * tok_block)), bounds)
    bounds_ref[0] = bounds.astype(jnp.int32)


def _topk(aff_t, cap, tok_block):
    b, e, s = aff_t.shape
    assert s // tok_block + 1 <= LANES
    tri = jnp.asarray(np.triu(np.ones((LANES, LANES))), dtype=BF16)
    return pl.pallas_call(
        functools.partial(_topk_body, cap=cap, tok_block=tok_block),
        grid=(b,),
        in_specs=[pl.BlockSpec((1, e, s), lambda i: (i, 0, 0)),
                  pl.BlockSpec((LANES, LANES), lambda i: (0, 0))],
        out_specs=[pl.BlockSpec((1, e, s), lambda i: (i, 0, 0)),
                   pl.BlockSpec((1, e, LANES), lambda i: (i, 0, 0))],
        out_shape=[jax.ShapeDtypeStruct((b, e, s), F32),
                   jax.ShapeDtypeStruct((b, e, LANES), jnp.int32)],
        compiler_params=_cparams(("arbitrary",)),
        name="topk",
    )(aff_t, tri)


def _moe_windows(bounds_ref, base, ne, step, tokens, window):
    per = tokens // MOE_BOUND_STEP
    starts = []
    passes = jnp.int32(0)
    for e in range(ne):
        r_lo = bounds_ref[base + e * LANES + step * per]
        r_hi = bounds_ref[base + e * LANES + (step + 1) * per]
        ws = (r_lo // BF16_ROWS) * BF16_ROWS
        starts.append(ws)
        passes = jnp.maximum(passes, (r_hi - ws + window - 1) // window)
    return starts, passes


def _gather_body(bounds_ref, rank_ref, hm_ref, affh_ref, affl_ref, xe_ref, gs_ref, *, cap, eg):
    w = GATHER_WINDOW
    ne, tc = rank_ref.shape[1], rank_ref.shape[2]
    bi = pl.program_id(0)
    ci = pl.program_id(1)

    @pl.when(ci == 0)
    def _():
        xe_ref[...] = jnp.zeros_like(xe_ref)
        gs_ref[...] = jnp.zeros_like(gs_ref)

    starts, passes = _moe_windows(bounds_ref, bi * ne * LANES, ne, ci, tc, w)
    iota_w = lax.broadcasted_iota(jnp.int32, (w, 1), 0)

    def one_pass(p, carry):
        for g0 in range(0, ne, eg):
            lo = [starts[e] + p * w for e in range(g0, g0 + eg)]
            phys = [pl.multiple_of(jnp.minimum(v, cap - w), BF16_ROWS) for v in lo]
            rk = jnp.concatenate([jnp.broadcast_to(rank_ref[0, e:e + 1, :], (w, tc))
                                  for e in range(g0, g0 + eg)], axis=0)
            slot = jnp.concatenate([iota_w + v for v in phys], axis=0).astype(F32)
            lom = jnp.concatenate([jnp.zeros_like(iota_w) + v for v in lo], axis=0).astype(F32)
            onehot = jnp.where((rk == slot) & (slot >= lom), 1.0, 0.0).astype(BF16)
            res = _dot(onehot, hm_ref[0])
            resg = _dot(onehot, affh_ref[0]) + _dot(onehot, affl_ref[0])
            for k, e in enumerate(range(g0, g0 + eg)):
                rows = pl.ds(phys[k], w)
                xe_ref[0, e, rows, :] = (xe_ref[0, e, rows, :].astype(F32) + res[k * w:(k + 1) * w]).astype(BF16)
                gs_ref[0, e, rows, :] = gs_ref[0, e, rows, :] + resg[k * w:(k + 1) * w]
        return carry

    lax.fori_loop(0, passes, one_pass, 0)


def _gather(bounds, rank, hm, aff_hi, aff_lo, cap, eg=16):
    b, e, s = rank.shape
    d = hm.shape[2]
    tc = GATHER_TOKENS
    one = pl.Buffered(1)
    return pl.pallas_call(
        functools.partial(_gather_body, cap=cap, eg=eg),
        grid_spec=pltpu.PrefetchScalarGridSpec(
            num_scalar_prefetch=1,
            grid=(b, s // tc),
            in_specs=[pl.BlockSpec((1, e, tc), lambda bi, ci, bnd: (bi, 0, ci)),
                      pl.BlockSpec((1, tc, d), lambda bi, ci, bnd: (bi, ci, 0)),
                      pl.BlockSpec((1, tc, LANES), lambda bi, ci, bnd: (bi, ci, 0)),
                      pl.BlockSpec((1, tc, LANES), lambda bi, ci, bnd: (bi, ci, 0))],
            out_specs=[pl.BlockSpec((1, e, cap, d), lambda bi, ci, bnd: (bi, 0, 0, 0), pipeline_mode=one),
                       pl.BlockSpec((1, e, cap, LANES), lambda bi, ci, bnd: (bi, 0, 0, 0), pipeline_mode=one)],
        ),
        out_shape=[jax.ShapeDtypeStruct((b, e, cap, d), BF16),
                   jax.ShapeDtypeStruct((b, e, cap, LANES), F32)],
        compiler_params=_cparams(("arbitrary", "arbitrary")),
        name="gather",
    )(bounds.reshape(-1), rank, hm, aff_hi, aff_lo)


def _ffn_body(xe_ref, gs_ref, wg_ref, wu_ref, wd_ref, ye_ref, acc_ref):
    f = pl.program_id(1)
    nb = xe_ref.shape[0]
    wg = wg_ref[0].astype(BF16)
    wu = wu_ref[0].astype(BF16)
    wd = wd_ref[0].astype(BF16)
    for b in range(nb):
        xe = xe_ref[b, 0]
        a = _dot(xe, wg)
        u = _dot(xe, wu)
        hsw = (a * jax.nn.sigmoid(a) * u).astype(BF16)
        part = _dot(hsw, wd)

        @pl.when(f == 0)
        def _():
            acc_ref[b] = part

        @pl.when(f != 0)
        def _():
            acc_ref[b] += part

    @pl.when(f == pl.num_programs(1) - 1)
    def _():
        lane = lax.broadcasted_iota(jnp.int32, gs_ref.shape[2:], 1)
        mine = lane == pl.program_id(0)
        for b in range(nb):
            gate = jnp.sum(jnp.where(mine, gs_ref[b, 0], 0.0), axis=-1, keepdims=True)
            ye_ref[b, 0] = (acc_ref[b] * gate).astype(BF16)


def _ffn(xe, gs, wg, wu, wd, ft=1024):
    b, e, cap, d = xe.shape
    ff = wg.shape[2]
    one = pl.Buffered(1)
    return pl.pallas_call(
        _ffn_body,
        grid=(e, ff // ft),
        in_specs=[pl.BlockSpec((b, 1, cap, d), lambda ei, fi: (0, ei, 0, 0), pipeline_mode=one),
                  pl.BlockSpec((b, 1, cap, LANES), lambda ei, fi: (0, ei, 0, 0), pipeline_mode=one),
                  pl.BlockSpec((1, d, ft), lambda ei, fi: (ei, 0, fi)),
                  pl.BlockSpec((1, d, ft), lambda ei, fi: (ei, 0, fi)),
                  pl.BlockSpec((1, ft, d), lambda ei, fi: (ei, fi, 0))],
        out_specs=pl.BlockSpec((b, 1, cap, d), lambda ei, fi: (0, ei, 0, 0)),
        out_shape=jax.ShapeDtypeStruct((b, e, cap, d), BF16),
        scratch_shapes=[pltpu.VMEM((b, cap, d), F32)],
        compiler_params=_cparams(("arbitrary", "arbitrary")),
        name="ffn",
    )(xe, gs, wg, wu, wd)


def _scatter_body(bounds_ref, rank_ref, ye_ref, x2_ref, g_ref, out_ref, acc_ref, *, eg):
    w = SCATTER_WINDOW
    ts = x2_ref.shape[1]
    ne, cap = ye_ref.shape[1], ye_ref.shape[2]
    starts, passes = _moe_windows(bounds_ref, pl.program_id(0) * ne * LANES, ne, pl.program_id(1), ts, w)
    rank = rank_ref[0]
    acc_ref[...] = x2_ref[0]
    iota_w = lax.broadcasted_iota(jnp.int32, (1, w), 1)

    def one_pass(p, carry):
        for g0 in range(0, ne, eg):
            onehots, rows = [], []
            for e in range(g0, g0 + eg):
                lo = starts[e] + p * w
                phys = pl.multiple_of(jnp.minimum(lo, cap - w), BF16_ROWS)
                slot = (iota_w + phys).astype(F32)
                hit = (rank[:, e:e + 1] == slot) & (slot >= lo.astype(F32))
                onehots.append(jnp.where(hit, 1.0, 0.0).astype(BF16))
                rows.append(ye_ref[0, e, pl.ds(phys, w), :])
            acc_ref[...] += _dot(jnp.concatenate(onehots, axis=1), jnp.concatenate(rows, axis=0))
        return carry

    lax.fori_loop(0, passes, one_pass, 0)
    out_ref[0] = _rms(acc_ref[...], g_ref[...])


def _scatter(bounds, rank_t, ye, x2, g, eg=8):
    b, s, e = rank_t.shape
    cap, d = ye.shape[2], ye.shape[3]
    ts = SCATTER_TOKENS
    return pl.pallas_call(
        functools.partial(_scatter_body, eg=eg),
        grid_spec=pltpu.PrefetchScalarGridSpec(
            num_scalar_prefetch=1,
            grid=(b, s // ts),
            in_specs=[pl.BlockSpec((1, ts, e), lambda bi, ti, bnd: (bi, ti, 0)),
                      pl.BlockSpec((1, e, cap, d), lambda bi, ti, bnd: (bi, 0, 0, 0),
                                   pipeline_mode=pl.Buffered(1)),
                      pl.BlockSpec((1, ts, d), lambda bi, ti, bnd: (bi, ti, 0)),
                      pl.BlockSpec((1, d), lambda bi, ti, bnd: (0, 0))],
            out_specs=pl.BlockSpec((1, ts, d), lambda bi, ti, bnd: (bi, ti, 0)),
            scratch_shapes=[pltpu.VMEM((ts, d), F32)],
        ),
        out_shape=jax.ShapeDtypeStruct((b, s, d), F32),
        compiler_params=_cparams(("arbitrary", "arbitrary")),
        name="scatter",
    )(bounds.reshape(-1), rank_t, ye, x2, g)


def kernel(x, mem, ln_mix_g, w_in, hy_conv_w, hy_conv_b, filt_w1, filt_b1, filt_freq1, filt_w2, filt_b2,
           filt_freq2, filt_w3, hy_skip, w_up_hy, w_up_attn, w_gate, b_gate, w_out, ln_x_g, ln_mem_g,
           w_q_x, w_kv_mem, w_o_x, ln_moe_g, w_router, w_e_gate, w_e_up, w_e_down, ln_f_g):
    b, s, d = x.shape
    n = b * s
    x2d = x.reshape(n, d)

    half = HEAD_DIM // 2
    inv = ROPE_THETA ** (-jnp.arange(0, HEAD_DIM, 2, dtype=F32) / HEAD_DIM)
    ang = jnp.arange(s, dtype=F32)[:, None] * inv[None, :]
    lane = np.arange(LANES)
    cos_t = jnp.cos(ang)[:, lane % half]
    sin_t = jnp.sin(ang)[:, lane % half] * jnp.asarray(np.where(lane % HEAD_DIM < half, -1.0, 1.0), F32)[None, :]

    w_all = jnp.concatenate([w_in, w_gate], axis=1).astype(BF16)
    p_hy, *qkv, gates = _proj(x2d, ln_mix_g[None], w_all, b_gate[None], cos_t, sin_t, b, s)

    t_col = jnp.linspace(0.0, 1.0, s, dtype=F32)[:, None]
    grid_col = 2.0 * math.pi * jnp.arange(s, dtype=F32)[:, None] / s
    bands = jnp.linspace(1e-4, HY_BANDS - 1, HY_BANDS, dtype=F32)[None, :]
    feat = _filt(t_col, grid_col, bands, filt_w1, filt_b1, filt_freq1, filt_w2, filt_b2, filt_freq2)
    delta = jnp.linspace(math.log(HY_TARGET) / HY_SLOW_PCT, math.log(HY_TARGET) / HY_FAST_PCT,
                         HY_WIDTH, dtype=F32)[None, :]
    tables = _fft_tables(s)
    khat = _hyfilt(t_col, delta, feat, filt_w3, tables[1], tables[3])
    z = _hyena(p_hy.reshape(b, s, -1), hy_conv_w, hy_conv_b[None], hy_skip, khat, tables)

    outs, lses = [], []
    for g, (window, dil) in enumerate(DIL_PAIRS):
        o_g, l_g = _attn_group(*qkv[3 * g:3 * g + 3], g, window // (2 * dil))
        outs.append(o_g)
        lses.append(l_g)

    kv = _memkv(mem, ln_mem_g[None], w_kv_mem.astype(BF16))
    wr_pad = jnp.pad(w_router, ((0, 0), (0, LANES - N_EXPERTS)))
    wr_hi = wr_pad.astype(BF16)
    wr_lo = (wr_pad - wr_hi.astype(F32)).astype(BF16)
    x2, hm, aff = _merge(x2d, z.reshape(n, HY_WIDTH), outs, lses, gates,
                         w_up_hy.astype(BF16), w_up_attn.astype(BF16), w_out.astype(BF16),
                         ln_x_g[None], w_q_x.astype(BF16), kv, w_o_x.astype(BF16), ln_moe_g[None],
                         wr_hi, wr_lo, s)

    cap = max(1, EC_FACTOR * s // N_EXPERTS)
    aff3 = aff.reshape(b, s, LANES)
    aff_hi = aff3.astype(BF16)
    aff_lo = (aff3 - aff_hi.astype(F32)).astype(BF16)
    rank, bounds = _topk(aff3[:, :, :N_EXPERTS].transpose(0, 2, 1), cap, MOE_BOUND_STEP)
    xe, gs = _gather(bounds, rank, hm.reshape(b, s, d), aff_hi, aff_lo, cap)
    ye = _ffn(xe, gs, w_e_gate, w_e_up, w_e_down)
    return _scatter(bounds, rank.transpose(0, 2, 1), ye, x2.reshape(b, s, d), ln_f_g[None])
```

```python
import functools
import math

import numpy as np
import jax
import jax.numpy as jnp
from jax import lax
from jax.experimental import pallas as pl
from jax.experimental.pallas import tpu as pltpu

F32 = jnp.float32
BF16 = jnp.bfloat16
HIGHEST = lax.Precision.HIGHEST

EPS = 1e-6
HY_WIDTH = 768
HY_BANDS = 16
HY_FFN = 64
HY_FAST_PCT = 0.3
HY_SLOW_PCT = 1.5
HY_TARGET = 1e-2
HEAD_DIM = 64
HEADS_PER_GROUP = 4
DIL_PAIRS = ((128, 1), (512, 4), (2048, 16))
N_GROUPS = len(DIL_PAIRS)
ATTN_WIDTH = N_GROUPS * HEADS_PER_GROUP * HEAD_DIM
ATTN_OUT = HEADS_PER_GROUP * HEAD_DIM
ROPE_THETA = 10000.0
X_HEADS = 4
X_HEAD_DIM = 128
N_EXPERTS = 16
EC_FACTOR = 2

LANES = 128
SUBLANES = 8
MXU_COLS = 256
VMEM_LIMIT = 56 * 1024 * 1024

FFT_N2 = 128
HY_CB = 128
FFT_UNROLL = 8
MOE_BOUND_STEP = 256
GATHER_TOKENS, GATHER_WINDOW = 256, 64
SCATTER_TOKENS, SCATTER_WINDOW = 512, 128
BF16_ROWS = 16


def _cparams(sem, vmem=VMEM_LIMIT):
    return pltpu.CompilerParams(dimension_semantics=sem, vmem_limit_bytes=vmem)


def _rms(x, g):
    return x * lax.rsqrt(jnp.mean(x * x, axis=-1, keepdims=True) + EPS) * g


def _dot(a, b):
    return jnp.dot(a, b, preferred_element_type=F32)


def _dot_hi(a, b):
    return jnp.dot(a, b, precision=HIGHEST, preferred_element_type=F32)


def _split(a):
    hi = a.astype(BF16)
    return hi, (a - hi.astype(F32)).astype(BF16)


def _dot_split(a, b):
    a_hi, a_lo = _split(a)
    b_hi, b_lo = _split(b)
    return _dot(a_hi, b_hi) + _dot(a_lo, b_hi) + _dot(a_hi, b_lo)


def _dot_t(a, b):
    return lax.dot_general(a, b, (((1,), (1,)), ((), ())), preferred_element_type=F32)


def _proj_body(x_ref, g_ref, w_ref, bg_ref, cos_ref, sin_ref, *rest, hyc, aw):
    ng = N_GROUPS
    perm_refs = (None,) + rest[:ng - 1]
    phy_ref = rest[ng - 1]
    qkv_refs = rest[ng:ng + 3 * ng]
    gate_ref = rest[ng + 3 * ng]
    h = _rms(x_ref[...], g_ref[...]).astype(BF16)
    phy = _dot(h, w_ref[:, :hyc])
    for j in range(hyc // HY_CB):
        phy_ref[0, j] = phy[:, j * HY_CB:(j + 1) * HY_CB]
    cos = cos_ref[...]
    sin = sin_ref[...]
    tm = cos.shape[0]
    lane = lax.broadcasted_iota(jnp.int32, (tm, LANES), 1)
    first = (lane % HEAD_DIM) < (HEAD_DIM // 2)

    def rope(t, scale):
        chunks = []
        for j in range(aw // LANES):
            tj = t[:, j * LANES:(j + 1) * LANES]
            partner = jnp.where(first, pltpu.roll(tj, LANES - HEAD_DIM // 2, 1),
                                pltpu.roll(tj, HEAD_DIM // 2, 1))
            chunks.append(((tj * cos + partner * sin) * scale).astype(BF16))
        return chunks

    gl = ATTN_OUT // LANES
    q = rope(_dot(h, w_ref[:, hyc:hyc + aw]), HEAD_DIM ** -0.5)
    k = rope(_dot(h, w_ref[:, hyc + aw:hyc + 2 * aw]), 1.0)
    vf = _dot(h, w_ref[:, hyc + 2 * aw:hyc + 3 * aw]).astype(BF16)
    v = [vf[:, j * LANES:(j + 1) * LANES] for j in range(aw // LANES)]
    for g in range(ng):
        for i, t in enumerate((q, k, v)):
            out_ref = qkv_refs[3 * g + i]
            tg = jnp.concatenate(t[g * gl:(g + 1) * gl], axis=1)
            if perm_refs[g] is not None:
                tg = _dot(perm_refs[g][...], tg).astype(BF16)
            dil = out_ref.shape[1]
            rows = tm // dil
            for r in range(dil):
                out_ref[0, r] = tg[r * rows:(r + 1) * rows, :]
    gate_ref[...] = jax.nn.sigmoid(_dot(h, w_ref[:, hyc + 3 * aw:]) + bg_ref[...]).astype(BF16)


def _proj(x2d, g, w_all, b_gate, cos_t, sin_t, batch, seq, tm=256):
    n, d = x2d.shape
    hyc = 3 * HY_WIDTH
    aw = ATTN_WIDTH
    gd = w_all.shape[1] - hyc - 3 * aw
    nseq = seq // tm
    row = lambda i: (i, 0)
    const = lambda i: (0, 0)
    perms = []
    for _, dil in DIL_PAIRS[1:]:
        rows = tm // dil
        src = (np.arange(tm) % rows) * dil + np.arange(tm) // rows
        perms.append(jnp.asarray(np.eye(tm)[src], dtype=BF16))
    qkv_specs, qkv_shapes = [], []
    for _, dil in DIL_PAIRS:
        for _ in range(3):
            qkv_specs.append(pl.BlockSpec((1, dil, tm // dil, ATTN_OUT), lambda i: (i // nseq, 0, i % nseq, 0)))
            qkv_shapes.append(jax.ShapeDtypeStruct((batch, dil, seq // dil, ATTN_OUT), BF16))
    return pl.pallas_call(
        functools.partial(_proj_body, hyc=hyc, aw=aw),
        grid=(n // tm,),
        in_specs=[
            pl.BlockSpec((tm, d), row),
            pl.BlockSpec((1, d), const),
            pl.BlockSpec(w_all.shape, const, pipeline_mode=pl.Buffered(1)),
            pl.BlockSpec((1, gd), const),
            pl.BlockSpec((tm, LANES), lambda i: (i % nseq, 0)),
            pl.BlockSpec((tm, LANES), lambda i: (i % nseq, 0)),
        ] + [pl.BlockSpec((tm, tm), const) for _ in perms],
        out_specs=([pl.BlockSpec((1, hyc // HY_CB, tm, HY_CB), lambda i: (i // nseq, 0, i % nseq, 0))]
                   + qkv_specs + [pl.BlockSpec((tm, gd), row)]),
        out_shape=([jax.ShapeDtypeStruct((batch, hyc // HY_CB, seq, HY_CB), F32)] + qkv_shapes
                   + [jax.ShapeDtypeStruct((n, gd), BF16)]),
        compiler_params=_cparams(("arbitrary",)),
        name="proj",
    )(x2d, g, w_all, b_gate, cos_t, sin_t, *perms)


def _filt_body(t_ref, grid_ref, bands_ref, w1t_ref, w1c_ref, w1s_ref, b1_ref, f1_ref,
               w2_ref, b2_ref, f2_ref, feat_ref):
    ang = bands_ref[...] * grid_ref[...]
    pre = (t_ref[...] * w1t_ref[...] + _dot_hi(jnp.cos(ang), w1c_ref[...])
           + _dot_hi(-jnp.sin(ang), w1s_ref[...]))
    h = jnp.sin(f1_ref[...] * (pre + b1_ref[...]))
    feat_ref[...] = jnp.sin(f2_ref[...] * (_dot_hi(h, w2_ref[...]) + b2_ref[...]))


def _filt(t_col, grid_col, bands, w1, b1, f1, w2, b2, f2):
    length = t_col.shape[0]
    nb = bands.shape[1]
    args = (t_col, grid_col, bands, w1[0:1], w1[1:1 + nb], w1[1 + nb:], b1[None], f1[None],
            w2, b2[None], f2[None])
    return pl.pallas_call(
        _filt_body,
        out_shape=jax.ShapeDtypeStruct((length, HY_FFN), F32),
        compiler_params=_cparams(None),
        name="filt",
    )(*args)


def _fft_tables(length):
    n = 2 * length
    n2 = FFT_N2
    n1 = n // n2
    h1 = n1 // 2
    k1 = np.arange(n1)[:, None]
    g_sig = np.zeros((n2, 2 * n1, 2 * h1))
    g_flt = np.zeros((n2, 2 * n1, 2 * h1))
    g_inv = np.zeros((n2, 2 * h1, 2 * n1))
    j = np.arange(h1)[None, :]
    for r in range(n2):
        th = 2 * np.pi * (k1 * (n2 * j + r) % n) / n
        gr, gi = np.cos(th), -np.sin(th)
        g_sig[r, 0::2, :h1] = gr
        g_sig[r, 0::2, h1:] = -gi
        g_sig[r, 1::2, :h1] = gi
        g_sig[r, 1::2, h1:] = gr
        g_flt[r, 0::2, :h1] = gr
        g_flt[r, 1::2, :h1] = gi
        m = n2 * (j + 1) - r
        thb = 2 * np.pi * (k1 * m % n) / n
        live = (m < length).astype(np.float64)
        g_flt[r, 0::2, h1:] = np.cos(thb) * live
        g_flt[r, 1::2, h1:] = np.sin(thb) * live
        wr, wi = (np.cos(th) / n).T, (np.sin(th) / n).T
        g_inv[r, :h1, 0::2] = wr
        g_inv[r, :h1, 1::2] = -wi
        g_inv[r, h1:, 0::2] = wi
        g_inv[r, h1:, 1::2] = wr
    a = np.arange(n2)
    th2 = 2 * np.pi * (np.outer(a, a) % n2) / n2
    c2, s2 = np.cos(th2), np.sin(th2)
    f_fwd = np.block([[c2, s2], [-s2, c2]])
    f_inv = np.block([[c2, -s2], [s2, c2]])
    cast = lambda z: jnp.asarray(z, dtype=F32).astype(BF16)
    return cast(g_sig), cast(g_flt), cast(g_inv), cast(f_fwd), cast(f_inv)


def _fft_stage1(src_a_ref, src_b_ref, tab_ref, work_ref, n1, is_filter):
    n2 = FFT_N2
    grp = SUBLANES

    def body(i, c):
        r0 = pl.multiple_of(i * grp, grp)
        b0 = pl.multiple_of(n2 - grp - r0, grp) if is_filter else r0
        a = jnp.swapaxes(src_a_ref[:, pl.ds(r0, grp), :], 0, 1)
        b = jnp.swapaxes(src_b_ref[:, pl.ds(b0, grp), :], 0, 1)
        outs = []
        for s in range(grp):
            rhs = jnp.concatenate([a[s], b[grp - 1 - s] if is_filter else b[s]], axis=0).astype(BF16)
            outs.append(_dot(tab_ref[r0 + s], rhs))
        work_ref[:, pl.ds(r0, grp), :] = jnp.swapaxes(jnp.stack(outs, axis=0), 0, 1)
        return c

    lax.fori_loop(0, n2 // grp, body, 0, unroll=2)


def _hyfilt_body(t_ref, ts_ref, delta_ref, feat_ref, feats_ref, w3f_ref, w3b_ref, gflt_ref, ffwd_ref, khat_ref,
                 hf_ref, hb_ref, work_ref, *, length, rc):
    n2 = FFT_N2
    n1 = 2 * length // n2
    tiles = rc // n2
    cb = hf_ref.shape[2]

    def gen(i, c):
        r0 = pl.multiple_of(i * rc, rc)
        j0 = pl.multiple_of(i * tiles, tiles)
        delta = jnp.abs(delta_ref[...])
        hf = _dot_split(feat_ref[pl.ds(r0, rc), :], w3f_ref[...]) * jnp.exp(-t_ref[pl.ds(r0, rc), :] * delta)
        hb = _dot_split(feats_ref[pl.ds(r0, rc), :], w3b_ref[...]) * jnp.exp(-ts_ref[pl.ds(r0, rc), :] * delta)
        hf_ref[pl.ds(j0, tiles)] = hf.reshape(tiles, n2, cb)
        hb_ref[pl.ds(j0, tiles)] = hb.reshape(tiles, n2, cb)
        return c

    lax.fori_loop(0, length // rc, gen, 0)
    _fft_stage1(hf_ref, hb_ref, gflt_ref, work_ref, n1, True)

    side = MXU_COLS // cb

    def stage2(i, c):
        k1s = [i * side + v for v in range(side)]
        blks = [work_ref[pl.ds(pl.multiple_of(2 * k1, 2), 2)].reshape(2 * n2, cb).astype(BF16) for k1 in k1s]
        x = _dot(ffwd_ref[...], jnp.concatenate(blks, axis=1)).astype(BF16)
        for v, k1 in enumerate(k1s):
            khat_ref[0, 0, pl.ds(pl.multiple_of(k1 * (2 * n2), 2 * n2), 2 * n2), :] = x[:, v * cb:(v + 1) * cb]
        return c

    lax.fori_loop(0, n1 // side, stage2, 0, unroll=FFT_UNROLL // side)


def _hyfilt(t_col, delta, feat, w3, g_flt, f_fwd, rc=512):
    s = t_col.shape[0]
    cb = HY_CB
    ncb = HY_WIDTH // cb
    n2 = FFT_N2
    n1 = 2 * s // n2
    norder = w3.shape[1] // (2 * HY_WIDTH)
    one = pl.Buffered(1)

    def full(a):
        nd = a.ndim
        return pl.BlockSpec(a.shape, lambda o, c: (0,) * nd, pipeline_mode=one)

    shift = lambda a: jnp.concatenate([a[1:], jnp.zeros_like(a[:1])], axis=0)
    return pl.pallas_call(
        functools.partial(_hyfilt_body, length=s, rc=rc),
        grid=(norder, ncb),
        in_specs=[full(t_col), full(t_col),
                  pl.BlockSpec((1, cb), lambda o, c: (0, c)),
                  full(feat), full(feat),
                  pl.BlockSpec((HY_FFN, cb), lambda o, c: (0, 2 * ncb * o + c)),
                  pl.BlockSpec((HY_FFN, cb), lambda o, c: (0, 2 * ncb * o + ncb + c)),
                  full(g_flt), full(f_fwd)],
        out_specs=pl.BlockSpec((1, 1, 2 * n1 * n2, cb), lambda o, c: (o, c, 0, 0)),
        out_shape=jax.ShapeDtypeStruct((norder, ncb, 2 * n1 * n2, cb), BF16),
        scratch_shapes=[pltpu.VMEM((n1 // 2, n2, cb), F32), pltpu.VMEM((n1 // 2, n2, cb), F32),
                        pltpu.VMEM((2 * n1, n2, cb), F32)],
        compiler_params=_cparams(("arbitrary", "arbitrary")),
        name="hyfilt",
    )(t_col, shift(t_col), delta, feat, shift(feat), w3, w3, g_flt, f_fwd)


def _hyena_body(pz_ref, pg1_ref, pg2_ref, cwz_ref, cwg1_ref, cwg2_ref, cbz_ref, cbg1_ref, cbg2_ref,
                skip_ref, khat_ref, gsig_ref, ginv_ref, ffwd_ref, finv_ref,
                out_ref,
                za_ref, zb_ref, ya_ref, yb_ref, work_ref, *, length, rc):
    n2 = FFT_N2
    n1 = 2 * length // n2
    h1 = n1 // 2
    tiles = rc // n2
    cb = za_ref.shape[2]
    side = MXU_COLS // cb
    sub = SUBLANES

    def conv3(p_ref, b, r0, w_ref, bias_ref):
        p = p_ref[b, 0, pl.ds(r0, rc), :]
        before = p_ref[b, 0, pl.ds(pl.multiple_of(jnp.maximum(r0 - sub, 0), sub), sub), :][sub - 1:sub, :]
        after = p_ref[b, 0, pl.ds(pl.multiple_of(jnp.minimum(r0 + rc, length - sub), sub), sub), :][0:1, :]
        before = jnp.where(r0 == 0, 0.0, before)
        after = jnp.where(r0 + rc == length, 0.0, after)
        row = lax.broadcasted_iota(jnp.int32, p.shape, 0)
        prev = jnp.where(row == 0, before, pltpu.roll(p, 1, 0))
        nxt = jnp.where(row == rc - 1, after, pltpu.roll(p, rc - 1, 0))
        return prev * w_ref[0:1, :] + p * w_ref[1:2, :] + nxt * w_ref[2:3, :] + bias_ref[...]

    def stage2(o):
        def body(i, c):
            k1s = [i * FFT_UNROLL + u for u in range(FFT_UNROLL)]
            blks = [work_ref[pl.ds(pl.multiple_of(2 * k1, 2), 2)].reshape(2 * n2, cb).astype(BF16) for k1 in k1s]
            outs = []
            for u in range(0, FFT_UNROLL, side):
                x = _dot(ffwd_ref[...], jnp.concatenate(blks[u:u + side], axis=1))
                ys = []
                for v in range(side):
                    k1 = k1s[u + v]
                    kh = khat_ref[o, 0, pl.ds(pl.multiple_of(k1 * (2 * n2), 2 * n2), 2 * n2), :].astype(F32)
                    xr, xi = x[:n2, v * cb:(v + 1) * cb], x[n2:, v * cb:(v + 1) * cb]
                    kr, ki = kh[:n2], kh[n2:]
                    ys.append(jnp.concatenate([xr * kr - xi * ki, xr * ki + xi * kr], axis=0).astype(BF16))
                out = _dot(finv_ref[...], jnp.concatenate(ys, axis=1))
                outs.extend(out[:, v * cb:(v + 1) * cb] for v in range(side))
            for k1, out in zip(k1s, outs):
                work_ref[pl.ds(pl.multiple_of(2 * k1, 2), 2)] = out.reshape(2, n2, cb)
            return c
        lax.fori_loop(0, n1 // FFT_UNROLL, body, 0)

    def stage3():
        grp = SUBLANES

        def body(i, c):
            r0 = pl.multiple_of(i * grp, grp)
            wk = jnp.swapaxes(work_ref[:, pl.ds(r0, grp), :], 0, 1)
            res = jnp.stack([_dot(ginv_ref[r0 + s], wk[s].astype(BF16)) for s in range(grp)], axis=0)
            ya_ref[:, pl.ds(r0, grp), :] = jnp.swapaxes(res[:, :h1], 0, 1)
            yb_ref[:, pl.ds(r0, grp), :] = jnp.swapaxes(res[:, h1:], 0, 1)
            return c
        lax.fori_loop(0, n2 // grp, body, 0, unroll=2)

    def rows3(ref, i):
        return ref.at[pl.ds(pl.multiple_of(i * tiles, tiles), tiles)]

    def load_z(i, c):
        r0 = pl.multiple_of(i * rc, rc)
        rows3(za_ref, i)[...] = conv3(pz_ref, 0, r0, cwz_ref, cbz_ref).reshape(tiles, n2, cb)
        rows3(zb_ref, i)[...] = conv3(pz_ref, 1, r0, cwz_ref, cbz_ref).reshape(tiles, n2, cb)
        return c

    lax.fori_loop(0, length // rc, load_z, 0)
    for o, (pg_ref, cw_ref, cb_ref) in enumerate(((pg1_ref, cwg1_ref, cbg1_ref), (pg2_ref, cwg2_ref, cbg2_ref))):
        _fft_stage1(za_ref, zb_ref, gsig_ref, work_ref, n1, False)
        stage2(o)
        stage3()
        skip = skip_ref[o:o + 1, :]

        def gate(i, c):
            r0 = pl.multiple_of(i * rc, rc)
            rows = pl.ds(r0, rc)
            ya, yb = (rows3(r, i)[...].reshape(rc, cb) for r in (ya_ref, yb_ref))
            za, zb = (rows3(r, i)[...].reshape(rc, cb) for r in (za_ref, zb_ref))
            new_a = conv3(pg_ref, 0, r0, cw_ref, cb_ref) * (ya + skip * za)
            new_b = conv3(pg_ref, 1, r0, cw_ref, cb_ref) * (yb + skip * zb)
            if o == 0:
                rows3(za_ref, i)[...] = new_a.reshape(tiles, n2, cb)
                rows3(zb_ref, i)[...] = new_b.reshape(tiles, n2, cb)
            else:
                out_ref[0, rows, :] = new_a
                out_ref[1, rows, :] = new_b
            return c

        lax.fori_loop(0, length // rc, gate, 0)


def _hyena(p_hy, conv_w, conv_b, skip, khat, tables, rc=512):
    b, _, s, _ = p_hy.shape
    w = HY_WIDTH
    cb = HY_CB
    ncb = w // cb
    n2 = FFT_N2
    n1 = 2 * s // n2
    g_sig, _, g_inv, f_fwd, f_inv = tables
    one = pl.Buffered(1)

    def pspec(off):
        return pl.BlockSpec((2, 1, s, cb), lambda c, p: (p, off + c, 0, 0), pipeline_mode=one)

    def cwspec(off):
        return pl.BlockSpec((3, cb), lambda c, p: (0, off + c))

    def cbspec(off):
        return pl.BlockSpec((1, cb), lambda c, p: (0, off + c))

    def full(a):
        nd = a.ndim
        return pl.BlockSpec(a.shape, lambda c, p: (0,) * nd, pipeline_mode=one)

    in_specs = [
        pspec(0), pspec(ncb), pspec(2 * ncb),
        cwspec(0), cwspec(ncb), cwspec(2 * ncb),
        cbspec(0), cbspec(ncb), cbspec(2 * ncb),
        pl.BlockSpec((2, cb), lambda c, p: (0, c)),
        pl.BlockSpec((khat.shape[0], 1, 2 * n1 * n2, cb), lambda c, p: (0, c, 0, 0), pipeline_mode=one),
        full(g_sig), full(g_inv), full(f_fwd), full(f_inv),
    ]
    return pl.pallas_call(
        functools.partial(_hyena_body, length=s, rc=rc),
        grid=(ncb, b // 2),
        in_specs=in_specs,
        out_specs=pl.BlockSpec((2, s, cb), lambda c, p: (p, 0, c)),
        out_shape=jax.ShapeDtypeStruct((b, s, w), F32),
        scratch_shapes=[pltpu.VMEM((n1 // 2, n2, cb), F32)] * 4 + [pltpu.VMEM((2 * n1, n2, cb), F32)],
        compiler_params=_cparams(("arbitrary", "arbitrary")),
        name="hyena",
    )(p_hy, p_hy, p_hy, conv_w, conv_w, conv_w, conv_b, conv_b, conv_b,
      skip, khat, g_sig, g_inv, f_fwd, f_inv)


def _attn_body(q_ref, k_ref, v_ref, o_ref, lse_ref, *, n_side, qb):
    lr = q_ref.shape[2]
    kw = qb + 2 * n_side
    nh = HEADS_PER_GROUP
    lane_head = lax.broadcasted_iota(jnp.int32, (qb, ATTN_OUT), 1) // HEAD_DIM
    qi = lax.broadcasted_iota(jnp.int32, (nh * qb, kw), 0) % qb
    ki = lax.broadcasted_iota(jnp.int32, (nh * qb, kw), 1)

    def body(i, c):
        q0 = pl.multiple_of(i * qb, qb)
        w0 = pl.multiple_of(jnp.clip(q0 - n_side, 0, lr - kw), n_side)
        q = q_ref[0, 0, pl.ds(q0, qb), :]
        kwin = k_ref[0, 0, pl.ds(w0, kw), :]
        vwin = v_ref[0, 0, pl.ds(w0, kw), :]
        valid = jnp.abs((ki + w0) - (qi + q0)) <= n_side
        qs = jnp.concatenate([jnp.where(lane_head == h, q, jnp.zeros_like(q)) for h in range(nh)], axis=0)
        s = jnp.where(valid, _dot_t(qs, kwin), -1e30)
        m = jnp.max(s, axis=-1, keepdims=True)
        p = jnp.exp(s - m)
        l = jnp.sum(p, axis=-1, keepdims=True)
        pv = _dot(p.astype(BF16), vwin) / l
        lse_rows = m + jnp.log(l)
        o = jnp.zeros((qb, ATTN_OUT), F32)
        lse = jnp.zeros((qb, ATTN_OUT), F32)
        for h in range(nh):
            mine = lane_head == h
            o = jnp.where(mine, pv[h * qb:(h + 1) * qb], o)
            lse = jnp.where(mine, lse_rows[h * qb:(h + 1) * qb], lse)
        o_ref[0, pl.ds(q0, qb), :] = o
        lse_ref[0, pl.ds(q0, qb), :] = lse
        return c

    lax.fori_loop(0, lr // qb, body, 0, unroll=2)


def _attn_group(q, k, v, g, n_side, qb=128):
    b, dil, lr, _ = q.shape
    in_spec = pl.BlockSpec((1, 1, lr, ATTN_OUT), lambda bi, r: (bi, r, 0, 0))
    out_spec = pl.BlockSpec((1, lr, ATTN_OUT), lambda bi, r: (bi, 0, r))
    o, lse = pl.pallas_call(
        functools.partial(_attn_body, n_side=n_side, qb=qb),
        grid=(b, dil),
        in_specs=[in_spec, in_spec, in_spec],
        out_specs=[out_spec, out_spec],
        out_shape=[jax.ShapeDtypeStruct((b, lr, dil * ATTN_OUT), F32)] * 2,
        compiler_params=_cparams(("arbitrary", "arbitrary")),
        name=f"attn{g}",
    )(q, k, v)
    return o, lse


def _memkv_body(mem_ref, g_ref, w_ref, kv_ref):
    mn = _rms(mem_ref[0], g_ref[...]).astype(BF16)
    kv_ref[0] = _dot(mn, w_ref[...]).astype(BF16)


def _memkv(mem, g, w_kv):
    b, m, d = mem.shape
    n = w_kv.shape[1]
    return pl.pallas_call(
        _memkv_body,
        grid=(b,),
        in_specs=[pl.BlockSpec((1, m, d), lambda i: (i, 0, 0)),
                  pl.BlockSpec((1, d), lambda i: (0, 0)),
                  pl.BlockSpec((d, n), lambda i: (0, 0))],
        out_specs=pl.BlockSpec((1, m, n), lambda i: (i, 0, 0)),
        out_shape=jax.ShapeDtypeStruct((b, m, n), BF16),
        compiler_params=_cparams(("arbitrary",)),
        name="memkv",
    )(mem, g, w_kv)


def _merge_body(x_ref, z_ref, o0_ref, o1_ref, o2_ref, l0_ref, l1_ref, l2_ref, gate_ref,
                wuh_ref, wua_ref, wout_ref, gx_ref, wq_ref, kv_ref, wo_ref, gm_ref, wrh_ref, wrl_ref,
                p1_ref, p2_ref, x2_ref, hm_ref, aff_ref, *, nsplit):
    rows = x_ref.shape[0] // nsplit
    for part in range(nsplit):
        sl = slice(part * rows, (part + 1) * rows)

        def natural(ref, perm_ref):
            dil = ref.shape[2] // ATTN_OUT
            rr = rows // dil
            blk = ref[0, part * rr:(part + 1) * rr, :]
            if dil == 1:
                return blk
            stacked = jnp.concatenate([blk[:, r * ATTN_OUT:(r + 1) * ATTN_OUT] for r in range(dil)], axis=0)
            hi = stacked.astype(BF16)
            lo = (stacked - hi.astype(F32)).astype(BF16)
            return _dot(perm_ref[...], hi) + _dot(perm_ref[...], lo)

        outs = [natural(r, p) for r, p in ((o0_ref, None), (o1_ref, p1_ref), (o2_ref, p2_ref))]
        lses = [natural(r, p) for r, p in ((l0_ref, None), (l1_ref, p1_ref), (l2_ref, p2_ref))]
        _merge_rows(sl, outs, lses, x_ref, z_ref, gate_ref,
                    wuh_ref, wua_ref, wout_ref, gx_ref, wq_ref, kv_ref, wo_ref, gm_ref, wrh_ref, wrl_ref,
                    x2_ref, hm_ref, aff_ref)


def _merge_rows(sl, outs, lses, x_ref, z_ref, gate_ref,
                wuh_ref, wua_ref, wout_ref, gx_ref, wq_ref, kv_ref, wo_ref, gm_ref, wrh_ref, wrl_ref,
                x2_ref, hm_ref, aff_ref):
    d = x_ref.shape[1]
    l0, l1, l2 = lses
    mx = jnp.maximum(jnp.maximum(l0, l1), l2)
    e0, e1, e2 = jnp.exp(l0 - mx), jnp.exp(l1 - mx), jnp.exp(l2 - mx)
    attn = (e0 * outs[0] + e1 * outs[1] + e2 * outs[2]) / (e0 + e1 + e2)
    y_hy = _dot(z_ref[sl, :].astype(BF16), wuh_ref[...])
    y_at = _dot(attn.astype(BF16), wua_ref[...])
    gates = gate_ref[sl, :].astype(F32)
    mix = gates[:, :d] * y_hy + gates[:, d:] * y_at
    x1 = x_ref[sl, :] + _dot(mix.astype(BF16), wout_ref[...])

    hx = _rms(x1, gx_ref[...]).astype(BF16)
    qx = _dot(hx, wq_ref[...]).astype(BF16)
    kv = kv_ref[0]
    xw = X_HEADS * X_HEAD_DIM
    heads = []
    for h in range(X_HEADS):
        hc = slice(h * X_HEAD_DIM, (h + 1) * X_HEAD_DIM)
        s = _dot_t(qx[:, hc], kv[:, hc]) * (X_HEAD_DIM ** -0.5)
        m = jnp.max(s, axis=-1, keepdims=True)
        p = jnp.exp(s - m)
        p = p / jnp.sum(p, axis=-1, keepdims=True)
        heads.append(_dot(p.astype(BF16), kv[:, xw + h * X_HEAD_DIM:xw + (h + 1) * X_HEAD_DIM]))
    ox = jnp.concatenate(heads, axis=-1).astype(BF16)
    x2 = x1 + _dot(ox, wo_ref[...])
    x2_ref[sl, :] = x2

    hm = _rms(x2, gm_ref[...])
    hm_hi = hm.astype(BF16)
    hm_lo = (hm - hm_hi.astype(F32)).astype(BF16)
    hm_ref[sl, :] = hm_hi
    logits = _dot(hm_hi, wrh_ref[...]) + _dot(hm_lo, wrh_ref[...]) + _dot(hm_hi, wrl_ref[...])
    lane = lax.broadcasted_iota(jnp.int32, logits.shape, 1)
    logits = jnp.where(lane < N_EXPERTS, logits, -1e30)
    m = jnp.max(logits, axis=-1, keepdims=True)
    p = jnp.exp(logits - m)
    aff_ref[sl, :] = p / jnp.sum(p, axis=-1, keepdims=True)


def _merge(x2d, z2d, outs, lses, gates, wuh, wua, wout, gx, wq, kv, wo, gm, wr_hi, wr_lo, seq, tm=512, nsplit=2):
    n, d = x2d.shape
    nseq = seq // tm
    row = lambda i: (i, 0)
    const = lambda i: (0, 0)

    def rspec(a):
        return pl.BlockSpec((tm, a.shape[1]), row)

    def aspec(a):
        dil = a.shape[2] // ATTN_OUT
        return pl.BlockSpec((1, tm // dil, a.shape[2]), lambda i: (i // nseq, i % nseq, 0))

    def cspec(a):
        return pl.BlockSpec(a.shape, const)

    rows = tm // nsplit
    perms = []
    for a in outs[1:]:
        dil = a.shape[2] // ATTN_OUT
        src = (np.arange(rows) % dil) * (rows // dil) + np.arange(rows) // dil
        perms.append(jnp.asarray(np.eye(rows)[src], dtype=BF16))

    in_arrays = [x2d, z2d, *outs, *lses, gates, wuh, wua, wout, gx, wq, kv, wo, gm, wr_hi, wr_lo, *perms]
    in_specs = ([rspec(a) for a in in_arrays[:2]] + [aspec(a) for a in in_arrays[2:8]] + [rspec(gates)]
                + [cspec(a) for a in in_arrays[9:14]]
                + [pl.BlockSpec((1,) + kv.shape[1:], lambda i: (i // nseq, 0, 0))]
                + [cspec(a) for a in in_arrays[15:]])
    return pl.pallas_call(
        functools.partial(_merge_body, nsplit=nsplit),
        grid=(n // tm,),
        in_specs=in_specs,
        out_specs=[pl.BlockSpec((tm, d), row), pl.BlockSpec((tm, d), row), pl.BlockSpec((tm, LANES), row)],
        out_shape=[jax.ShapeDtypeStruct((n, d), F32), jax.ShapeDtypeStruct((n, d), BF16),
                   jax.ShapeDtypeStruct((n, LANES), F32)],
        compiler_params=_cparams(("arbitrary",)),
        name="merge",
    )(*in_arrays)


def _topk_body(aff_ref, tri_ref, rank_ref, bounds_ref, *, cap, tok_block):
    a = aff_ref[0]
    e, s = a.shape

    def count(mask):
        return jnp.sum(jnp.where(mask, 1.0, 0.0), axis=-1, keepdims=True)

    def as_float(bits):
        return pltpu.bitcast(jnp.broadcast_to(bits, (e, LANES)), F32)[:, 0:1]

    def search(i, thr):
        cand = thr | (jnp.int32(1) << (30 - i))
        return jnp.where(count(a >= as_float(cand)) >= cap, cand, thr)

    thr = as_float(lax.fori_loop(0, 31, search, jnp.zeros((e, 1), jnp.int32)))
    gt = a > thr
    eq = a == thr
    need = cap - count(gt)

    def prefix_excl(mask):
        mf = jnp.where(mask, 1.0, 0.0)
        parts = []
        carry = jnp.zeros((e, 1), F32)
        for c in range(s // LANES):
            blk = mf[:, c * LANES:(c + 1) * LANES]
            inc = _dot(blk.astype(BF16), tri_ref[...])
            parts.append(inc - blk + carry)
            carry = carry + inc[:, LANES - 1:LANES]
        return jnp.concatenate(parts, axis=-1)

    sel = gt | (eq & (prefix_excl(eq) < need))
    excl = prefix_excl(sel)
    rank_ref[0] = jnp.where(sel, excl, -1.0)

    tok = lax.broadcasted_iota(jnp.int32, (e, s), 1)
    lane = lax.broadcasted_iota(jnp.int32, (e, LANES), 1)
    bounds = jnp.zeros((e, LANES), F32)
    for j in range(s // tok_block + 1):
        bounds = jnp.where(lane == j, count(sel & (tok < j * tok_block)), bounds)
    bounds_ref[0] = bounds.astype(jnp.int32)


def _topk(aff_t, cap, tok_block):
    b, e, s = aff_t.shape
    assert s // tok_block + 1 <= LANES
    tri = jnp.asarray(np.triu(np.ones((LANES, LANES))), dtype=BF16)
    return pl.pallas_call(
        functools.partial(_topk_body, cap=cap, tok_block=tok_block),
        grid=(b,),
        in_specs=[pl.BlockSpec((1, e, s), lambda i: (i, 0, 0)),
                  pl.BlockSpec((LANES, LANES), lambda i: (0, 0))],
        out_specs=[pl.BlockSpec((1, e, s), lambda i: (i, 0, 0)),
                   pl.BlockSpec((1, e, LANES), lambda i: (i, 0, 0))],
        out_shape=[jax.ShapeDtypeStruct((b, e, s), F32),
                   jax.ShapeDtypeStruct((b, e, LANES), jnp.int32)],
        compiler_params=_cparams(("arbitrary",)),
        name="topk",
    )(aff_t, tri)


def _moe_windows(bounds_ref, base, ne, step, tokens, window):
    per = tokens // MOE_BOUND_STEP
    starts = []
    passes = jnp.int32(0)
    for e in range(ne):
        r_lo = bounds_ref[base + e * LANES + step * per]
        r_hi = bounds_ref[base + e * LANES + (step + 1) * per]
        ws = (r_lo // BF16_ROWS) * BF16_ROWS
        starts.append(ws)
        passes = jnp.maximum(passes, (r_hi - ws + window - 1) // window)
    return starts, passes


def _gather_body(bounds_ref, rank_ref, hm_ref, affh_ref, affl_ref, xe_ref, gs_ref, *, cap, eg):
    w = GATHER_WINDOW
    ne, tc = rank_ref.shape[1], rank_ref.shape[2]
    bi = pl.program_id(0)
    ci = pl.program_id(1)

    @pl.when(ci == 0)
    def _():
        xe_ref[...] = jnp.zeros_like(xe_ref)
        gs_ref[...] = jnp.zeros_like(gs_ref)

    starts, passes = _moe_windows(bounds_ref, bi * ne * LANES, ne, ci, tc, w)
    iota_w = lax.broadcasted_iota(jnp.int32, (w, 1), 0)

    def one_pass(p, carry):
        for g0 in range(0, ne, eg):
            lo = [starts[e] + p * w for e in range(g0, g0 + eg)]
            phys = [pl.multiple_of(jnp.minimum(v, cap - w), BF16_ROWS) for v in lo]
            rk = jnp.concatenate([jnp.broadcast_to(rank_ref[0, e:e + 1, :], (w, tc))
                                  for e in range(g0, g0 + eg)], axis=0)
            slot = jnp.concatenate([iota_w + v for v in phys], axis=0).astype(F32)
            lom = jnp.concatenate([jnp.zeros_like(iota_w) + v for v in lo], axis=0).astype(F32)
            onehot = jnp.where((rk == slot) & (slot >= lom), 1.0, 0.0).astype(BF16)
            res = _dot(onehot, hm_ref[0])
            resg = _dot(onehot, affh_ref[0]) + _dot(onehot, affl_ref[0])
            for k, e in enumerate(range(g0, g0 + eg)):
                rows = pl.ds(phys[k], w)
                xe_ref[0, e, rows, :] = (xe_ref[0, e, rows, :].astype(F32) + res[k * w:(k + 1) * w]).astype(BF16)
                gs_ref[0, e, rows, :] = gs_ref[0, e, rows, :] + resg[k * w:(k + 1) * w]
        return carry

    lax.fori_loop(0, passes, one_pass, 0)


def _gather(bounds, rank, hm, aff_hi, aff_lo, cap, eg=16):
    b, e, s = rank.shape
    d = hm.shape[2]
    tc = GATHER_TOKENS
    one = pl.Buffered(1)
    return pl.pallas_call(
        functools.partial(_gather_body, cap=cap, eg=eg),
        grid_spec=pltpu.PrefetchScalarGridSpec(
            num_scalar_prefetch=1,
            grid=(b, s // tc),
            in_specs=[pl.BlockSpec((1, e, tc), lambda bi, ci, bnd: (bi, 0, ci)),
                      pl.BlockSpec((1, tc, d), lambda bi, ci, bnd: (bi, ci, 0)),
                      pl.BlockSpec((1, tc, LANES), lambda bi, ci, bnd: (bi, ci, 0)),
                      pl.BlockSpec((1, tc, LANES), lambda bi, ci, bnd: (bi, ci, 0))],
            out_specs=[pl.BlockSpec((1, e, cap, d), lambda bi, ci, bnd: (bi, 0, 0, 0), pipeline_mode=one),
                       pl.BlockSpec((1, e, cap, LANES), lambda bi, ci, bnd: (bi, 0, 0, 0), pipeline_mode=one)],
        ),
        out_shape=[jax.ShapeDtypeStruct((b, e, cap, d), BF16),
                   jax.ShapeDtypeStruct((b, e, cap, LANES), F32)],
        compiler_params=_cparams(("arbitrary", "arbitrary")),
        name="gather",
    )(bounds.reshape(-1), rank, hm, aff_hi, aff_lo)


def _ffn_body(xe_ref, gs_ref, wg_ref, wu_ref, wd_ref, ye_ref, acc_ref):
    f = pl.program_id(1)
    nb = xe_ref.shape[0]
    wg = wg_ref[0].astype(BF16)
    wu = wu_ref[0].astype(BF16)
    wd = wd_ref[0].astype(BF16)
    for b in range(nb):
        xe = xe_ref[b, 0]
        a = _dot(xe, wg)
        u = _dot(xe, wu)
        hsw = (a * jax.nn.sigmoid(a) * u).astype(BF16)
        part = _dot(hsw, wd)

        @pl.when(f == 0)
        def _():
            acc_ref[b] = part

        @pl.when(f != 0)
        def _():
            acc_ref[b] += part

    @pl.when(f == pl.num_programs(1) - 1)
    def _():
        lane = lax.broadcasted_iota(jnp.int32, gs_ref.shape[2:], 1)
        mine = lane == pl.program_id(0)
        for b in range(nb):
            gate = jnp.sum(jnp.where(mine, gs_ref[b, 0], 0.0), axis=-1, keepdims=True)
            ye_ref[b, 0] = (acc_ref[b] * gate).astype(BF16)


def _ffn(xe, gs, wg, wu, wd, ft=1024):
    b, e, cap, d = xe.shape
    ff = wg.shape[2]
    one = pl.Buffered(1)
    return pl.pallas_call(
        _ffn_body,
        grid=(e, ff // ft),
        in_specs=[pl.BlockSpec((b, 1, cap, d), lambda ei, fi: (0, ei, 0, 0), pipeline_mode=one),
                  pl.BlockSpec((b, 1, cap, LANES), lambda ei, fi: (0, ei, 0, 0), pipeline_mode=one),
                  pl.BlockSpec((1, d, ft), lambda ei, fi: (ei, 0, fi)),
                  pl.BlockSpec((1, d, ft), lambda ei, fi: (ei, 0, fi)),
                  pl.BlockSpec((1, ft, d), lambda ei, fi: (ei, fi, 0))],
        out_specs=pl.BlockSpec((b, 1, cap, d), lambda ei, fi: (0, ei, 0, 0)),
        out_shape=jax.ShapeDtypeStruct((b, e, cap, d), BF16),
        scratch_shapes=[pltpu.VMEM((b, cap, d), F32)],
        compiler_params=_cparams(("arbitrary", "arbitrary")),
        name="ffn",
    )(xe, gs, wg, wu, wd)


def _scatter_body(bounds_ref, rank_ref, ye_ref, x2_ref, g_ref, out_ref, acc_ref, *, eg):
    w = SCATTER_WINDOW
    ts = x2_ref.shape[1]
    ne, cap = ye_ref.shape[1], ye_ref.shape[2]
    starts, passes = _moe_windows(bounds_ref, pl.program_id(0) * ne * LANES, ne, pl.program_id(1), ts, w)
    rank = rank_ref[0]
    acc_ref[...] = x2_ref[0]
    iota_w = lax.broadcasted_iota(jnp.int32, (1, w), 1)

    def one_pass(p, carry):
        for g0 in range(0, ne, eg):
            onehots, rows = [], []
            for e in range(g0, g0 + eg):
                lo = starts[e] + p * w
                phys = pl.multiple_of(jnp.minimum(lo, cap - w), BF16_ROWS)
                slot = (iota_w + phys).astype(F32)
                hit = (rank[:, e:e + 1] == slot) & (slot >= lo.astype(F32))
                onehots.append(jnp.where(hit, 1.0, 0.0).astype(BF16))
                rows.append(ye_ref[0, e, pl.ds(phys, w), :])
            acc_ref[...] += _dot(jnp.concatenate(onehots, axis=1), jnp.concatenate(rows, axis=0))
        return carry

    lax.fori_loop(0, passes, one_pass, 0)
    out_ref[0] = _rms(acc_ref[...], g_ref[...])


def _scatter(bounds, rank_t, ye, x2, g, eg=8):
    b, s, e = rank_t.shape
    cap, d = ye.shape[2], ye.shape[3]
    ts = SCATTER_TOKENS
    return pl.pallas_call(
        functools.partial(_scatter_body, eg=eg),
        grid_spec=pltpu.PrefetchScalarGridSpec(
            num_scalar_prefetch=1,
            grid=(b, s // ts),
            in_specs=[pl.BlockSpec((1, ts, e), lambda bi, ti, bnd: (bi, ti, 0)),
                      pl.BlockSpec((1, e, cap, d), lambda bi, ti, bnd: (bi, 0, 0, 0),
                                   pipeline_mode=pl.Buffered(1)),
                      pl.BlockSpec((1, ts, d), lambda bi, ti, bnd: (bi, ti, 0)),
                      pl.BlockSpec((1, d), lambda bi, ti, bnd: (0, 0))],
            out_specs=pl.BlockSpec((1, ts, d), lambda bi, ti, bnd: (bi, ti, 0)),
            scratch_shapes=[pltpu.VMEM((ts, d), F32)],
        ),
        out_shape=jax.ShapeDtypeStruct((b, s, d), F32),
        compiler_params=_cparams(("arbitrary", "arbitrary")),
        name="scatter",
    )(bounds.reshape(-1), rank_t, ye, x2, g)


def kernel(x, mem, ln_mix_g, w_in, hy_conv_w, hy_conv_b, filt_w1, filt_b1, filt_freq1, filt_w2, filt_b2,
           filt_freq2, filt_w3, hy_skip, w_up_hy, w_up_attn, w_gate, b_gate, w_out, ln_x_g, ln_mem_g,
           w_q_x, w_kv_mem, w_o_x, ln_moe_g, w_router, w_e_gate, w_e_up, w_e_down, ln_f_g):
    b, s, d = x.shape
    n = b * s
    x2d = x.reshape(n, d)

    half = HEAD_DIM // 2
    inv = ROPE_THETA ** (-jnp.arange(0, HEAD_DIM, 2, dtype=F32) / HEAD_DIM)
    ang = jnp.arange(s, dtype=F32)[:, None] * inv[None, :]
    lane = np.arange(LANES)
    cos_t = jnp.cos(ang)[:, lane % half]
    sin_t = jnp.sin(ang)[:, lane % half] * jnp.asarray(np.where(lane % HEAD_DIM < half, -1.0, 1.0), F32)[None, :]

    w_all = jnp.concatenate([w_in, w_gate], axis=1).astype(BF16)
    p_hy, *qkv, gates = _proj(x2d, ln_mix_g[None], w_all, b_gate[None], cos_t, sin_t, b, s)

    t_col = jnp.linspace(0.0, 1.0, s, dtype=F32)[:, None]
    grid_col = 2.0 * math.pi * jnp.arange(s, dtype=F32)[:, None] / s
    bands = jnp.linspace(1e-4, HY_BANDS - 1, HY_BANDS, dtype=F32)[None, :]
    feat = _filt(t_col, grid_col, bands, filt_w1, filt_b1, filt_freq1, filt_w2, filt_b2, filt_freq2)
    delta = jnp.linspace(math.log(HY_TARGET) / HY_SLOW_PCT, math.log(HY_TARGET) / HY_FAST_PCT,
                         HY_WIDTH, dtype=F32)[None, :]
    tables = _fft_tables(s)
    khat = _hyfilt(t_col, delta, feat, filt_w3, tables[1], tables[3])
    z = _hyena(p_hy, hy_conv_w, hy_conv_b[None], hy_skip, khat, tables)

    outs, lses = [], []
    for g, (window, dil) in enumerate(DIL_PAIRS):
        o_g, l_g = _attn_group(*qkv[3 * g:3 * g + 3], g, window // (2 * dil))
        outs.append(o_g)
        lses.append(l_g)

    kv = _memkv(mem, ln_mem_g[None], w_kv_mem.astype(BF16))
    wr_pad = jnp.pad(w_router, ((0, 0), (0, LANES - N_EXPERTS)))
    wr_hi = wr_pad.astype(BF16)
    wr_lo = (wr_pad - wr_hi.astype(F32)).astype(BF16)
    x2, hm, aff = _merge(x2d, z.reshape(n, HY_WIDTH), outs, lses, gates,
                         w_up_hy.astype(BF16), w_up_attn.astype(BF16), w_out.astype(BF16),
                         ln_x_g[None], w_q_x.astype(BF16), kv, w_o_x.astype(BF16), ln_moe_g[None],
                         wr_hi, wr_lo, s)

    cap = max(1, EC_FACTOR * s // N_EXPERTS)
    aff3 = aff.reshape(b, s, LANES)
    aff_hi = aff3.astype(BF16)
    aff_lo = (aff3 - aff_hi.astype(F32)).astype(BF16)
    rank, bounds = _topk(aff3[:, :, :N_EXPERTS].transpose(0, 2, 1), cap, MOE_BOUND_STEP)
    xe, gs = _gather(bounds, rank, hm.reshape(b, s, d), aff_hi, aff_lo, cap)
    ye = _ffn(xe, gs, w_e_gate, w_e_up, w_e_down)
    return _scatter(bounds, rank.transpose(0, 2, 1), ye, x2.reshape(b, s, d), ln_f_g[None])
```

```python
import functools
import math

import numpy as np
import jax
import jax.numpy as jnp
from jax import lax
from jax.experimental import pallas as pl
from jax.experimental.pallas import tpu as pltpu

F32 = jnp.float32
BF16 = jnp.bfloat16
HIGHEST = lax.Precision.HIGHEST

EPS = 1e-6
HY_WIDTH = 768
HY_BANDS = 16
HY_FFN = 64
HY_FAST_PCT = 0.3
HY_SLOW_PCT = 1.5
HY_TARGET = 1e-2
HEAD_DIM = 64
HEADS_PER_GROUP = 4
DIL_PAIRS = ((128, 1), (512, 4), (2048, 16))
N_GROUPS = len(DIL_PAIRS)
ATTN_WIDTH = N_GROUPS * HEADS_PER_GROUP * HEAD_DIM
ATTN_OUT = HEADS_PER_GROUP * HEAD_DIM
ROPE_THETA = 10000.0
X_HEADS = 4
X_HEAD_DIM = 128
N_EXPERTS = 16
EC_FACTOR = 2

LANES = 128
SUBLANES = 8
MXU_COLS = 256
VMEM_LIMIT = 56 * 1024 * 1024

FFT_N2 = 128
HY_CB = 128
FFT_UNROLL = 8
FFT_PAD = 8
MOE_BOUND_STEP = 256
GATHER_TOKENS, GATHER_WINDOW = 256, 64
SCATTER_TOKENS, SCATTER_WINDOW = 512, 128
BF16_ROWS = 16


def _cparams(sem, vmem=VMEM_LIMIT):
    return pltpu.CompilerParams(dimension_semantics=sem, vmem_limit_bytes=vmem)


def _rms(x, g):
    return x * lax.rsqrt(jnp.mean(x * x, axis=-1, keepdims=True) + EPS) * g


def _dot(a, b):
    return jnp.dot(a, b, preferred_element_type=F32)


def _dot_hi(a, b):
    return jnp.dot(a, b, precision=HIGHEST, preferred_element_type=F32)


def _split(a):
    hi = a.astype(BF16)
    return hi, (a - hi.astype(F32)).astype(BF16)


def _dot_split(a, b):
    a_hi, a_lo = _split(a)
    b_hi, b_lo = _split(b)
    return _dot(a_hi, b_hi) + _dot(a_lo, b_hi) + _dot(a_hi, b_lo)


def _dot_t(a, b):
    return lax.dot_general(a, b, (((1,), (1,)), ((), ())), preferred_element_type=F32)


def _proj_body(x_ref, g_ref, w_ref, bg_ref, cos_ref, sin_ref, *rest, hyc, aw):
    ng = N_GROUPS
    perm_refs = (None,) + rest[:ng - 1]
    phy_ref = rest[ng - 1]
    qkv_refs = rest[ng:ng + 3 * ng]
    gate_ref = rest[ng + 3 * ng]
    h = _rms(x_ref[...], g_ref[...]).astype(BF16)
    phy = _dot(h, w_ref[:, :hyc])
    for j in range(hyc // HY_CB):
        phy_ref[0, j] = phy[:, j * HY_CB:(j + 1) * HY_CB]
    cos = cos_ref[...]
    sin = sin_ref[...]
    tm = cos.shape[0]
    lane = lax.broadcasted_iota(jnp.int32, (tm, LANES), 1)
    first = (lane % HEAD_DIM) < (HEAD_DIM // 2)

    def rope(t, scale):
        chunks = []
        for j in range(aw // LANES):
            tj = t[:, j * LANES:(j + 1) * LANES]
            partner = jnp.where(first, pltpu.roll(tj, LANES - HEAD_DIM // 2, 1),
                                pltpu.roll(tj, HEAD_DIM // 2, 1))
            chunks.append(((tj * cos + partner * sin) * scale).astype(BF16))
        return chunks

    gl = ATTN_OUT // LANES
    q = rope(_dot(h, w_ref[:, hyc:hyc + aw]), HEAD_DIM ** -0.5)
    k = rope(_dot(h, w_ref[:, hyc + aw:hyc + 2 * aw]), 1.0)
    vf = _dot(h, w_ref[:, hyc + 2 * aw:hyc + 3 * aw]).astype(BF16)
    v = [vf[:, j * LANES:(j + 1) * LANES] for j in range(aw // LANES)]
    for g in range(ng):
        for i, t in enumerate((q, k, v)):
            out_ref = qkv_refs[3 * g + i]
            tg = jnp.concatenate(t[g * gl:(g + 1) * gl], axis=1)
            if perm_refs[g] is not None:
                tg = _dot(perm_refs[g][...], tg).astype(BF16)
            dil = out_ref.shape[1]
            rows = tm // dil
            for r in range(dil):
                out_ref[0, r] = tg[r * rows:(r + 1) * rows, :]
    gate_ref[...] = jax.nn.sigmoid(_dot(h, w_ref[:, hyc + 3 * aw:]) + bg_ref[...]).astype(BF16)


def _proj(x2d, g, w_all, b_gate, cos_t, sin_t, batch, seq, tm=256):
    n, d = x2d.shape
    hyc = 3 * HY_WIDTH
    aw = ATTN_WIDTH
    gd = w_all.shape[1] - hyc - 3 * aw
    nseq = seq // tm
    row = lambda i: (i, 0)
    const = lambda i: (0, 0)
    perms = []
    for _, dil in DIL_PAIRS[1:]:
        rows = tm // dil
        src = (np.arange(tm) % rows) * dil + np.arange(tm) // rows
        perms.append(jnp.asarray(np.eye(tm)[src], dtype=BF16))
    qkv_specs, qkv_shapes = [], []
    for _, dil in DIL_PAIRS:
        for _ in range(3):
            qkv_specs.append(pl.BlockSpec((1, dil, tm // dil, ATTN_OUT), lambda i: (i // nseq, 0, i % nseq, 0)))
            qkv_shapes.append(jax.ShapeDtypeStruct((batch, dil, seq // dil, ATTN_OUT), BF16))
    return pl.pallas_call(
        functools.partial(_proj_body, hyc=hyc, aw=aw),
        grid=(n // tm,),
        in_specs=[
            pl.BlockSpec((tm, d), row),
            pl.BlockSpec((1, d), const),
            pl.BlockSpec(w_all.shape, const, pipeline_mode=pl.Buffered(1)),
            pl.BlockSpec((1, gd), const),
            pl.BlockSpec((tm, LANES), lambda i: (i % nseq, 0)),
            pl.BlockSpec((tm, LANES), lambda i: (i % nseq, 0)),
        ] + [pl.BlockSpec((tm, tm), const) for _ in perms],
        out_specs=([pl.BlockSpec((1, hyc // HY_CB, tm, HY_CB), lambda i: (i // nseq, 0, i % nseq, 0))]
                   + qkv_specs + [pl.BlockSpec((tm, gd), row)]),
        out_shape=([jax.ShapeDtypeStruct((batch, hyc // HY_CB, seq, HY_CB), F32)] + qkv_shapes
                   + [jax.ShapeDtypeStruct((n, gd), BF16)]),
        compiler_params=_cparams(("arbitrary",)),
        name="proj",
    )(x2d, g, w_all, b_gate, cos_t, sin_t, *perms)


def _filt_body(t_ref, grid_ref, bands_ref, w1t_ref, w1c_ref, w1s_ref, b1_ref, f1_ref,
               w2_ref, b2_ref, f2_ref, feat_ref):
    ang = bands_ref[...] * grid_ref[...]
    pre = (t_ref[...] * w1t_ref[...] + _dot_hi(jnp.cos(ang), w1c_ref[...])
           + _dot_hi(-jnp.sin(ang), w1s_ref[...]))
    h = jnp.sin(f1_ref[...] * (pre + b1_ref[...]))
    feat_ref[...] = jnp.sin(f2_ref[...] * (_dot_hi(h, w2_ref[...]) + b2_ref[...]))


def _filt(t_col, grid_col, bands, w1, b1, f1, w2, b2, f2):
    length = t_col.shape[0]
    nb = bands.shape[1]
    args = (t_col, grid_col, bands, w1[0:1], w1[1:1 + nb], w1[1 + nb:], b1[None], f1[None],
            w2, b2[None], f2[None])
    return pl.pallas_call(
        _filt_body,
        out_shape=jax.ShapeDtypeStruct((length, HY_FFN), F32),
        compiler_params=_cparams(None),
        name="filt",
    )(*args)


def _fft_tables(length):
    n = 2 * length
    n2 = FFT_N2
    n1 = n // n2
    h1 = n1 // 2
    k1 = np.arange(n1)[:, None]
    g_sig = np.zeros((n2, 2 * n1, 2 * h1))
    g_flt = np.zeros((n2, 2 * n1, 2 * h1))
    g_inv = np.zeros((n2, 2 * h1, 2 * n1))
    j = np.arange(h1)[None, :]
    for r in range(n2):
        th = 2 * np.pi * (k1 * (n2 * j + r) % n) / n
        gr, gi = np.cos(th), -np.sin(th)
        g_sig[r, 0::2, :h1] = gr
        g_sig[r, 0::2, h1:] = -gi
        g_sig[r, 1::2, :h1] = gi
        g_sig[r, 1::2, h1:] = gr
        g_flt[r, 0::2, :h1] = gr
        g_flt[r, 1::2, :h1] = gi
        m = n2 * (j + 1) - r
        thb = 2 * np.pi * (k1 * m % n) / n
        live = (m < length).astype(np.float64)
        g_flt[r, 0::2, h1:] = np.cos(thb) * live
        g_flt[r, 1::2, h1:] = np.sin(thb) * live
        wr, wi = (np.cos(th) / n).T, (np.sin(th) / n).T
        g_inv[r, :h1, 0::2] = wr
        g_inv[r, :h1, 1::2] = -wi
        g_inv[r, h1:, 0::2] = wi
        g_inv[r, h1:, 1::2] = wr
    a = np.arange(n2)
    th2 = 2 * np.pi * (np.outer(a, a) % n2) / n2
    c2, s2 = np.cos(th2), np.sin(th2)
    f_fwd = np.block([[c2, s2], [-s2, c2]])
    f_inv = np.block([[c2, -s2], [s2, c2]])
    cast = lambda z: jnp.asarray(z, dtype=F32).astype(BF16)
    return cast(g_sig), cast(g_flt), cast(g_inv), cast(f_fwd), cast(f_inv)


def _fft_stage1(src_a_ref, src_b_ref, tab_ref, work_ref, n1, is_filter):
    n2 = FFT_N2
    grp = SUBLANES

    def body(i, c):
        r0 = pl.multiple_of(i * grp, grp)
        b0 = pl.multiple_of(n2 - grp - r0, grp) if is_filter else r0
        a = jnp.swapaxes(src_a_ref[:, pl.ds(r0, grp), :], 0, 1)
        b = jnp.swapaxes(src_b_ref[:, pl.ds(b0, grp), :], 0, 1)
        outs = []
        for s in range(grp):
            rhs = jnp.concatenate([a[s], b[grp - 1 - s] if is_filter else b[s]], axis=0).astype(BF16)
            outs.append(_dot(tab_ref[r0 + s], rhs))
        work_ref[:, pl.ds(r0, grp), :] = jnp.swapaxes(jnp.stack(outs, axis=0), 0, 1)
        return c

    lax.fori_loop(0, n2 // grp, body, 0, unroll=2)


def _hyfilt_body(t_ref, ts_ref, delta_ref, feat_ref, feats_ref, w3f_ref, w3b_ref, gflt_ref, ffwd_ref, khat_ref,
                 hf_ref, hb_ref, work_ref, *, length, rc):
    n2 = FFT_N2
    n1 = 2 * length // n2
    tiles = rc // n2
    cb = hf_ref.shape[2]

    def gen(i, c):
        r0 = pl.multiple_of(i * rc, rc)
        j0 = pl.multiple_of(i * tiles, tiles)
        delta = jnp.abs(delta_ref[...])
        hf = _dot_split(feat_ref[pl.ds(r0, rc), :], w3f_ref[...]) * jnp.exp(-t_ref[pl.ds(r0, rc), :] * delta)
        hb = _dot_split(feats_ref[pl.ds(r0, rc), :], w3b_ref[...]) * jnp.exp(-ts_ref[pl.ds(r0, rc), :] * delta)
        hf_ref[pl.ds(j0, tiles), :n2, :] = hf.reshape(tiles, n2, cb)
        hb_ref[pl.ds(j0, tiles), :n2, :] = hb.reshape(tiles, n2, cb)
        return c

    lax.fori_loop(0, length // rc, gen, 0)
    _fft_stage1(hf_ref, hb_ref, gflt_ref, work_ref, n1, True)

    side = MXU_COLS // cb

    def stage2(i, c):
        k1s = [i * side + v for v in range(side)]
        blks = [work_ref[pl.ds(pl.multiple_of(2 * k1, 2), 2), :n2, :].reshape(2 * n2, cb).astype(BF16)
                for k1 in k1s]
        x = _dot(ffwd_ref[...], jnp.concatenate(blks, axis=1)).astype(BF16)
        for v, k1 in enumerate(k1s):
            khat_ref[0, 0, pl.ds(pl.multiple_of(k1 * (2 * n2), 2 * n2), 2 * n2), :] = x[:, v * cb:(v + 1) * cb]
        return c

    lax.fori_loop(0, n1 // side, stage2, 0, unroll=FFT_UNROLL // side)


def _hyfilt(t_col, delta, feat, w3, g_flt, f_fwd, rc=512):
    s = t_col.shape[0]
    cb = HY_CB
    ncb = HY_WIDTH // cb
    n2 = FFT_N2
    n1 = 2 * s // n2
    norder = w3.shape[1] // (2 * HY_WIDTH)
    one = pl.Buffered(1)

    def full(a):
        nd = a.ndim
        return pl.BlockSpec(a.shape, lambda o, c: (0,) * nd, pipeline_mode=one)

    shift = lambda a: jnp.concatenate([a[1:], jnp.zeros_like(a[:1])], axis=0)
    return pl.pallas_call(
        functools.partial(_hyfilt_body, length=s, rc=rc),
        grid=(norder, ncb),
        in_specs=[full(t_col), full(t_col),
                  pl.BlockSpec((1, cb), lambda o, c: (0, c)),
                  full(feat), full(feat),
                  pl.BlockSpec((HY_FFN, cb), lambda o, c: (0, 2 * ncb * o + c)),
                  pl.BlockSpec((HY_FFN, cb), lambda o, c: (0, 2 * ncb * o + ncb + c)),
                  full(g_flt), full(f_fwd)],
        out_specs=pl.BlockSpec((1, 1, 2 * n1 * n2, cb), lambda o, c: (o, c, 0, 0)),
        out_shape=jax.ShapeDtypeStruct((norder, ncb, 2 * n1 * n2, cb), BF16),
        scratch_shapes=[pltpu.VMEM((n1 // 2, n2 + FFT_PAD, cb), F32)] * 2
        + [pltpu.VMEM((2 * n1, n2 + FFT_PAD, cb), F32)],
        compiler_params=_cparams(("arbitrary", "arbitrary")),
        name="hyfilt",
    )(t_col, shift(t_col), delta, feat, shift(feat), w3, w3, g_flt, f_fwd)


def _hyena_body(pz_ref, pg1_ref, pg2_ref, cwz_ref, cwg1_ref, cwg2_ref, cbz_ref, cbg1_ref, cbg2_ref,
                skip_ref, khat_ref, gsig_ref, ginv_ref, ffwd_ref, finv_ref,
                out_ref,
                za_ref, zb_ref, ya_ref, yb_ref, work_ref, *, length, rc):
    n2 = FFT_N2
    n1 = 2 * length // n2
    h1 = n1 // 2
    tiles = rc // n2
    cb = za_ref.shape[2]
    side = MXU_COLS // cb
    sub = SUBLANES

    def conv3(p_ref, b, r0, w_ref, bias_ref):
        p = p_ref[b, 0, pl.ds(r0, rc), :]
        before = p_ref[b, 0, pl.ds(pl.multiple_of(jnp.maximum(r0 - sub, 0), sub), sub), :][sub - 1:sub, :]
        after = p_ref[b, 0, pl.ds(pl.multiple_of(jnp.minimum(r0 + rc, length - sub), sub), sub), :][0:1, :]
        before = jnp.where(r0 == 0, 0.0, before)
        after = jnp.where(r0 + rc == length, 0.0, after)
        row = lax.broadcasted_iota(jnp.int32, p.shape, 0)
        prev = jnp.where(row == 0, before, pltpu.roll(p, 1, 0))
        nxt = jnp.where(row == rc - 1, after, pltpu.roll(p, rc - 1, 0))
        return prev * w_ref[0:1, :] + p * w_ref[1:2, :] + nxt * w_ref[2:3, :] + bias_ref[...]

    def stage2(o):
        def body(i, c):
            k1s = [i * FFT_UNROLL + u for u in range(FFT_UNROLL)]
            blks = [work_ref[pl.ds(pl.multiple_of(2 * k1, 2), 2), :n2, :].reshape(2 * n2, cb).astype(BF16)
                    for k1 in k1s]
            outs = []
            for u in range(0, FFT_UNROLL, side):
                x = _dot(ffwd_ref[...], jnp.concatenate(blks[u:u + side], axis=1))
                ys = []
                for v in range(side):
                    k1 = k1s[u + v]
                    kh = khat_ref[o, 0, pl.ds(pl.multiple_of(k1 * (2 * n2), 2 * n2), 2 * n2), :].astype(F32)
                    xr, xi = x[:n2, v * cb:(v + 1) * cb], x[n2:, v * cb:(v + 1) * cb]
                    kr, ki = kh[:n2], kh[n2:]
                    ys.append(jnp.concatenate([xr * kr - xi * ki, xr * ki + xi * kr], axis=0).astype(BF16))
                out = _dot(finv_ref[...], jnp.concatenate(ys, axis=1))
                outs.extend(out[:, v * cb:(v + 1) * cb] for v in range(side))
            for k1, out in zip(k1s, outs):
                work_ref[pl.ds(pl.multiple_of(2 * k1, 2), 2), :n2, :] = out.reshape(2, n2, cb)
            return c
        lax.fori_loop(0, n1 // FFT_UNROLL, body, 0)

    def stage3():
        grp = SUBLANES

        def body(i, c):
            r0 = pl.multiple_of(i * grp, grp)
            wk = jnp.swapaxes(work_ref[:, pl.ds(r0, grp), :], 0, 1)
            res = jnp.stack([_dot(ginv_ref[r0 + s], wk[s].astype(BF16)) for s in range(grp)], axis=0)
            ya_ref[:, pl.ds(r0, grp), :] = jnp.swapaxes(res[:, :h1], 0, 1)
            yb_ref[:, pl.ds(r0, grp), :] = jnp.swapaxes(res[:, h1:], 0, 1)
            return c
        lax.fori_loop(0, n2 // grp, body, 0, unroll=2)

    def rows3(ref, i):
        return ref.at[pl.ds(pl.multiple_of(i * tiles, tiles), tiles), :n2, :]

    def load_z(i, c):
        r0 = pl.multiple_of(i * rc, rc)
        rows3(za_ref, i)[...] = conv3(pz_ref, 0, r0, cwz_ref, cbz_ref).reshape(tiles, n2, cb)
        rows3(zb_ref, i)[...] = conv3(pz_ref, 1, r0, cwz_ref, cbz_ref).reshape(tiles, n2, cb)
        return c

    lax.fori_loop(0, length // rc, load_z, 0)
    for o, (pg_ref, cw_ref, cb_ref) in enumerate(((pg1_ref, cwg1_ref, cbg1_ref), (pg2_ref, cwg2_ref, cbg2_ref))):
        _fft_stage1(za_ref, zb_ref, gsig_ref, work_ref, n1, False)
        stage2(o)
        stage3()
        skip = skip_ref[o:o + 1, :]

        def gate(i, c):
            r0 = pl.multiple_of(i * rc, rc)
            rows = pl.ds(r0, rc)
            ya, yb = (rows3(r, i)[...].reshape(rc, cb) for r in (ya_ref, yb_ref))
            za, zb = (rows3(r, i)[...].reshape(rc, cb) for r in (za_ref, zb_ref))
            new_a = conv3(pg_ref, 0, r0, cw_ref, cb_ref) * (ya + skip * za)
            new_b = conv3(pg_ref, 1, r0, cw_ref, cb_ref) * (yb + skip * zb)
            if o == 0:
                rows3(za_ref, i)[...] = new_a.reshape(tiles, n2, cb)
                rows3(zb_ref, i)[...] = new_b.reshape(tiles, n2, cb)
            else:
                out_ref[0, rows, :] = new_a
                out_ref[1, rows, :] = new_b
            return c

        lax.fori_loop(0, length // rc, gate, 0)


def _hyena(p_hy, conv_w, conv_b, skip, khat, tables, rc=512):
    b, _, s, _ = p_hy.shape
    w = HY_WIDTH
    cb = HY_CB
    ncb = w // cb
    n2 = FFT_N2
    n1 = 2 * s // n2
    g_sig, _, g_inv, f_fwd, f_inv = tables
    one = pl.Buffered(1)

    def pspec(off):
        return pl.BlockSpec((2, 1, s, cb), lambda c, p: (p, off + c, 0, 0), pipeline_mode=one)

    def cwspec(off):
        return pl.BlockSpec((3, cb), lambda c, p: (0, off + c))

    def cbspec(off):
        return pl.BlockSpec((1, cb), lambda c, p: (0, off + c))

    def full(a):
        nd = a.ndim
        return pl.BlockSpec(a.shape, lambda c, p: (0,) * nd, pipeline_mode=one)

    in_specs = [
        pspec(0), pspec(ncb), pspec(2 * ncb),
        cwspec(0), cwspec(ncb), cwspec(2 * ncb),
        cbspec(0), cbspec(ncb), cbspec(2 * ncb),
        pl.BlockSpec((2, cb), lambda c, p: (0, c)),
        pl.BlockSpec((khat.shape[0], 1, 2 * n1 * n2, cb), lambda c, p: (0, c, 0, 0), pipeline_mode=one),
        full(g_sig), full(g_inv), full(f_fwd), full(f_inv),
    ]
    return pl.pallas_call(
        functools.partial(_hyena_body, length=s, rc=rc),
        grid=(ncb, b // 2),
        in_specs=in_specs,
        out_specs=pl.BlockSpec((2, s, cb), lambda c, p: (p, 0, c)),
        out_shape=jax.ShapeDtypeStruct((b, s, w), F32),
        scratch_shapes=[pltpu.VMEM((n1 // 2, n2 + FFT_PAD, cb), F32)] * 4
        + [pltpu.VMEM((2 * n1, n2 + FFT_PAD, cb), F32)],
        compiler_params=_cparams(("arbitrary", "arbitrary")),
        name="hyena",
    )(p_hy, p_hy, p_hy, conv_w, conv_w, conv_w, conv_b, conv_b, conv_b,
      skip, khat, g_sig, g_inv, f_fwd, f_inv)


def _attn_body(q_ref, k_ref, v_ref, o_ref, lse_ref, *, n_side, qb):
    lr = q_ref.shape[2]
    kw = qb + 2 * n_side
    nh = HEADS_PER_GROUP
    lane_head = lax.broadcasted_iota(jnp.int32, (qb, ATTN_OUT), 1) // HEAD_DIM
    qi = lax.broadcasted_iota(jnp.int32, (nh * qb, kw), 0) % qb
    ki = lax.broadcasted_iota(jnp.int32, (nh * qb, kw), 1)

    def body(i, c):
        q0 = pl.multiple_of(i * qb, qb)
        w0 = pl.multiple_of(jnp.clip(q0 - n_side, 0, lr - kw), n_side)
        q = q_ref[0, 0, pl.ds(q0, qb), :]
        kwin = k_ref[0, 0, pl.ds(w0, kw), :]
        vwin = v_ref[0, 0, pl.ds(w0, kw), :]
        valid = jnp.abs((ki + w0) - (qi + q0)) <= n_side
        qs = jnp.concatenate([jnp.where(lane_head == h, q, jnp.zeros_like(q)) for h in range(nh)], axis=0)
        s = jnp.where(valid, _dot_t(qs, kwin), -1e30)
        m = jnp.max(s, axis=-1, keepdims=True)
        p = jnp.exp(s - m)
        l = jnp.sum(p, axis=-1, keepdims=True)
        pv = _dot(p.astype(BF16), vwin) / l
        lse_rows = m + jnp.log(l)
        o = jnp.zeros((qb, ATTN_OUT), F32)
        lse = jnp.zeros((qb, ATTN_OUT), F32)
        for h in range(nh):
            mine = lane_head == h
            o = jnp.where(mine, pv[h * qb:(h + 1) * qb], o)
            lse = jnp.where(mine, lse_rows[h * qb:(h + 1) * qb], lse)
        o_ref[0, pl.ds(q0, qb), :] = o
        lse_ref[0, pl.ds(q0, qb), :] = lse
        return c

    lax.fori_loop(0, lr // qb, body, 0, unroll=2)


def _attn_group(q, k, v, g, n_side, qb=128):
    b, dil, lr, _ = q.shape
    in_spec = pl.BlockSpec((1, 1, lr, ATTN_OUT), lambda bi, r: (bi, r, 0, 0))
    out_spec = pl.BlockSpec((1, lr, ATTN_OUT), lambda bi, r: (bi, 0, r))
    o, lse = pl.pallas_call(
        functools.partial(_attn_body, n_side=n_side, qb=qb),
        grid=(b, dil),
        in_specs=[in_spec, in_spec, in_spec],
        out_specs=[out_spec, out_spec],
        out_shape=[jax.ShapeDtypeStruct((b, lr, dil * ATTN_OUT), F32)] * 2,
        compiler_params=_cparams(("arbitrary", "arbitrary")),
        name=f"attn{g}",
    )(q, k, v)
    return o, lse


def _memkv_body(mem_ref, g_ref, w_ref, kv_ref):
    mn = _rms(mem_ref[0], g_ref[...]).astype(BF16)
    kv_ref[0] = _dot(mn, w_ref[...]).astype(BF16)


def _memkv(mem, g, w_kv):
    b, m, d = mem.shape
    n = w_kv.shape[1]
    return pl.pallas_call(
        _memkv_body,
        grid=(b,),
        in_specs=[pl.BlockSpec((1, m, d), lambda i: (i, 0, 0)),
                  pl.BlockSpec((1, d), lambda i: (0, 0)),
                  pl.BlockSpec((d, n), lambda i: (0, 0))],
        out_specs=pl.BlockSpec((1, m, n), lambda i: (i, 0, 0)),
        out_shape=jax.ShapeDtypeStruct((b, m, n), BF16),
        compiler_params=_cparams(("arbitrary",)),
        name="memkv",
    )(mem, g, w_kv)


def _merge_body(x_ref, z_ref, o0_ref, o1_ref, o2_ref, l0_ref, l1_ref, l2_ref, gate_ref,
                wuh_ref, wua_ref, wout_ref, gx_ref, wq_ref, kv_ref, wo_ref, gm_ref, wrh_ref, wrl_ref,
                p1_ref, p2_ref, x2_ref, hm_ref, aff_ref, *, nsplit):
    rows = x_ref.shape[0] // nsplit
    for part in range(nsplit):
        sl = slice(part * rows, (part + 1) * rows)

        def natural(ref, perm_ref):
            dil = ref.shape[2] // ATTN_OUT
            rr = rows // dil
            blk = ref[0, part * rr:(part + 1) * rr, :]
            if dil == 1:
                return blk
            stacked = jnp.concatenate([blk[:, r * ATTN_OUT:(r + 1) * ATTN_OUT] for r in range(dil)], axis=0)
            hi = stacked.astype(BF16)
            lo = (stacked - hi.astype(F32)).astype(BF16)
            return _dot(perm_ref[...], hi) + _dot(perm_ref[...], lo)

        outs = [natural(r, p) for r, p in ((o0_ref, None), (o1_ref, p1_ref), (o2_ref, p2_ref))]
        lses = [natural(r, p) for r, p in ((l0_ref, None), (l1_ref, p1_ref), (l2_ref, p2_ref))]
        _merge_rows(sl, outs, lses, x_ref, z_ref, gate_ref,
                    wuh_ref, wua_ref, wout_ref, gx_ref, wq_ref, kv_ref, wo_ref, gm_ref, wrh_ref, wrl_ref,
                    x2_ref, hm_ref, aff_ref)


def _merge_rows(sl, outs, lses, x_ref, z_ref, gate_ref,
                wuh_ref, wua_ref, wout_ref, gx_ref, wq_ref, kv_ref, wo_ref, gm_ref, wrh_ref, wrl_ref,
                x2_ref, hm_ref, aff_ref):
    d = x_ref.shape[1]
    l0, l1, l2 = lses
    mx = jnp.maximum(jnp.maximum(l0, l1), l2)
    e0, e1, e2 = jnp.exp(l0 - mx), jnp.exp(l1 - mx), jnp.exp(l2 - mx)
    attn = (e0 * outs[0] + e1 * outs[1] + e2 * outs[2]) / (e0 + e1 + e2)
    y_hy = _dot(z_ref[sl, :].astype(BF16), wuh_ref[...])
    y_at = _dot(attn.astype(BF16), wua_ref[...])
    gates = gate_ref[sl, :].astype(F32)
    mix = gates[:, :d] * y_hy + gates[:, d:] * y_at
    x1 = x_ref[sl, :] + _dot(mix.astype(BF16), wout_ref[...])

    hx = _rms(x1, gx_ref[...]).astype(BF16)
    qx = _dot(hx, wq_ref[...]).astype(BF16)
    kv = kv_ref[0]
    xw = X_HEADS * X_HEAD_DIM
    heads = []
    for h in range(X_HEADS):
        hc = slice(h * X_HEAD_DIM, (h + 1) * X_HEAD_DIM)
        s = _dot_t(qx[:, hc], kv[:, hc]) * (X_HEAD_DIM ** -0.5)
        m = jnp.max(s, axis=-1, keepdims=True)
        p = jnp.exp(s - m)
        p = p / jnp.sum(p, axis=-1, keepdims=True)
        heads.append(_dot(p.astype(BF16), kv[:, xw + h * X_HEAD_DIM:xw + (h + 1) * X_HEAD_DIM]))
    ox = jnp.concatenate(heads, axis=-1).astype(BF16)
    x2 = x1 + _dot(ox, wo_ref[...])
    x2_ref[sl, :] = x2

    hm = _rms(x2, gm_ref[...])
    hm_hi = hm.astype(BF16)
    hm_lo = (hm - hm_hi.astype(F32)).astype(BF16)
    hm_ref[sl, :] = hm_hi
    logits = _dot(hm_hi, wrh_ref[...]) + _dot(hm_lo, wrh_ref[...]) + _dot(hm_hi, wrl_ref[...])
    lane = lax.broadcasted_iota(jnp.int32, logits.shape, 1)
    logits = jnp.where(lane < N_EXPERTS, logits, -1e30)
    m = jnp.max(logits, axis=-1, keepdims=True)
    p = jnp.exp(logits - m)
    aff_ref[sl, :] = p / jnp.sum(p, axis=-1, keepdims=True)


def _merge(x2d, z2d, outs, lses, gates, wuh, wua, wout, gx, wq, kv, wo, gm, wr_hi, wr_lo, seq, tm=512, nsplit=2):
    n, d = x2d.shape
    nseq = seq // tm
    row = lambda i: (i, 0)
    const = lambda i: (0, 0)

    def rspec(a):
        return pl.BlockSpec((tm, a.shape[1]), row)

    def aspec(a):
        dil = a.shape[2] // ATTN_OUT
        return pl.BlockSpec((1, tm // dil, a.shape[2]), lambda i: (i // nseq, i % nseq, 0))

    def cspec(a):
        return pl.BlockSpec(a.shape, const)

    rows = tm // nsplit
    perms = []
    for a in outs[1:]:
        dil = a.shape[2] // ATTN_OUT
        src = (np.arange(rows) % dil) * (rows // dil) + np.arange(rows) // dil
        perms.append(jnp.asarray(np.eye(rows)[src], dtype=BF16))

    in_arrays = [x2d, z2d, *outs, *lses, gates, wuh, wua, wout, gx, wq, kv, wo, gm, wr_hi, wr_lo, *perms]
    in_specs = ([rspec(a) for a in in_arrays[:2]] + [aspec(a) for a in in_arrays[2:8]] + [rspec(gates)]
                + [cspec(a) for a in in_arrays[9:14]]
                + [pl.BlockSpec((1,) + kv.shape[1:], lambda i: (i // nseq, 0, 0))]
                + [cspec(a) for a in in_arrays[15:]])
    return pl.pallas_call(
        functools.partial(_merge_body, nsplit=nsplit),
        grid=(n // tm,),
        in_specs=in_specs,
        out_specs=[pl.BlockSpec((tm, d), row), pl.BlockSpec((tm, d), row), pl.BlockSpec((tm, LANES), row)],
        out_shape=[jax.ShapeDtypeStruct((n, d), F32), jax.ShapeDtypeStruct((n, d), BF16),
                   jax.ShapeDtypeStruct((n, LANES), F32)],
        compiler_params=_cparams(("arbitrary",)),
        name="merge",
    )(*in_arrays)


def _topk_body(aff_ref, tri_ref, rank_ref, bounds_ref, *, cap, tok_block):
    a = aff_ref[0]
    e, s = a.shape

    def count(mask):
        return jnp.sum(jnp.where(mask, 1.0, 0.0), axis=-1, keepdims=True)

    def as_float(bits):
        return pltpu.bitcast(jnp.broadcast_to(bits, (e, LANES)), F32)[:, 0:1]

    def search(i, thr):
        cand = thr | (jnp.int32(1) << (30 - i))
        return jnp.where(count(a >= as_float(cand)) >= cap, cand, thr)

    thr = as_float(lax.fori_loop(0, 31, search, jnp.zeros((e, 1), jnp.int32)))
    gt = a > thr
    eq = a == thr
    need = cap - count(gt)

    def prefix_excl(mask):
        mf = jnp.where(mask, 1.0, 0.0)
        parts = []
        carry = jnp.zeros((e, 1), F32)
        for c in range(s // LANES):
            blk = mf[:, c * LANES:(c + 1) * LANES]
            inc = _dot(blk.astype(BF16), tri_ref[...])
            parts.append(inc - blk + carry)
            carry = carry + inc[:, LANES - 1:LANES]
        return jnp.concatenate(parts, axis=-1)

    sel = gt | (eq & (prefix_excl(eq) < need))
    excl = prefix_excl(sel)
    rank_ref[0] = jnp.where(sel, excl, -1.0)

    tok = lax.broadcasted_iota(jnp.int32, (e, s), 1)
    lane = lax.broadcasted_iota(jnp.int32, (e, LANES), 1)
    bounds = jnp.zeros((e, LANES), F32)
    for j in range(s // tok_block + 1):
        bounds = jnp.where(lane == j, count(sel & (tok < j * tok_block)), bounds)
    bounds_ref[0] = bounds.astype(jnp.int32)


def _topk(aff_t, cap, tok_block):
    b, e, s = aff_t.shape
    assert s // tok_block + 1 <= LANES
    tri = jnp.asarray(np.triu(np.ones((LANES, LANES))), dtype=BF16)
    return pl.pallas_call(
        functools.partial(_topk_body, cap=cap, tok_block=tok_block),
        grid=(b,),
        in_specs=[pl.BlockSpec((1, e, s), lambda i: (i, 0, 0)),
                  pl.BlockSpec((LANES, LANES), lambda i: (0, 0))],
        out_specs=[pl.BlockSpec((1, e, s), lambda i: (i, 0, 0)),
                   pl.BlockSpec((1, e, LANES), lambda i: (i, 0, 0))],
        out_shape=[jax.ShapeDtypeStruct((b, e, s), F32),
                   jax.ShapeDtypeStruct((b, e, LANES), jnp.int32)],
        compiler_params=_cparams(("arbitrary",)),
        name="topk",
    )(aff_t, tri)


def _moe_windows(bounds_ref, base, ne, step, tokens, window):
    per = tokens // MOE_BOUND_STEP
    starts = []
    passes = jnp.int32(0)
    for e in range(ne):
        r_lo = bounds_ref[base + e * LANES + step * per]
        r_hi = bounds_ref[base + e * LANES + (step + 1) * per]
        ws = (r_lo // BF16_ROWS) * BF16_ROWS
        starts.append(ws)
        passes = jnp.maximum(passes, (r_hi - ws + window - 1) // window)
    return starts, passes


def _gather_body(bounds_ref, rank_ref, hm_ref, affh_ref, affl_ref, xe_ref, gs_ref, *, cap, eg):
    w = GATHER_WINDOW
    ne, tc = rank_ref.shape[1], rank_ref.shape[2]
    bi = pl.program_id(0)
    ci = pl.program_id(1)

    @pl.when(ci == 0)
    def _():
        xe_ref[...] = jnp.zeros_like(xe_ref)
        gs_ref[...] = jnp.zeros_like(gs_ref)

    starts, passes = _moe_windows(bounds_ref, bi * ne * LANES, ne, ci, tc, w)
    iota_w = lax.broadcasted_iota(jnp.int32, (w, 1), 0)

    def one_pass(p, carry):
        for g0 in range(0, ne, eg):
            lo = [starts[e] + p * w for e in range(g0, g0 + eg)]
            phys = [pl.multiple_of(jnp.minimum(v, cap - w), BF16_ROWS) for v in lo]
            rk = jnp.concatenate([jnp.broadcast_to(rank_ref[0, e:e + 1, :], (w, tc))
                                  for e in range(g0, g0 + eg)], axis=0)
            slot = jnp.concatenate([iota_w + v for v in phys], axis=0).astype(F32)
            lom = jnp.concatenate([jnp.zeros_like(iota_w) + v for v in lo], axis=0).astype(F32)
            onehot = jnp.where((rk == slot) & (slot >= lom), 1.0, 0.0).astype(BF16)
            res = _dot(onehot, hm_ref[0])
            resg = _dot(onehot, affh_ref[0]) + _dot(onehot, affl_ref[0])
            for k, e in enumerate(range(g0, g0 + eg)):
                rows = pl.ds(phys[k], w)
                xe_ref[0, e, rows, :] = (xe_ref[0, e, rows, :].astype(F32) + res[k * w:(k + 1) * w]).astype(BF16)
                gs_ref[0, e, rows, :] = gs_ref[0, e, rows, :] + resg[k * w:(k + 1) * w]
        return carry

    lax.fori_loop(0, passes, one_pass, 0)


def _gather(bounds, rank, hm, aff_hi, aff_lo, cap, eg=16):
    b, e, s = rank.shape
    d = hm.shape[2]
    tc = GATHER_TOKENS
    one = pl.Buffered(1)
    return pl.pallas_call(
        functools.partial(_gather_body, cap=cap, eg=eg),
        grid_spec=pltpu.PrefetchScalarGridSpec(
            num_scalar_prefetch=1,
            grid=(b, s // tc),
            in_specs=[pl.BlockSpec((1, e, tc), lambda bi, ci, bnd: (bi, 0, ci)),
                      pl.BlockSpec((1, tc, d), lambda bi, ci, bnd: (bi, ci, 0)),
                      pl.BlockSpec((1, tc, LANES), lambda bi, ci, bnd: (bi, ci, 0)),
                      pl.BlockSpec((1, tc, LANES), lambda bi, ci, bnd: (bi, ci, 0))],
            out_specs=[pl.BlockSpec((1, e, cap, d), lambda bi, ci, bnd: (bi, 0, 0, 0), pipeline_mode=one),
                       pl.BlockSpec((1, e, cap, LANES), lambda bi, ci, bnd: (bi, 0, 0, 0), pipeline_mode=one)],
        ),
        out_shape=[jax.ShapeDtypeStruct((b, e, cap, d), BF16),
                   jax.ShapeDtypeStruct((b, e, cap, LANES), F32)],
        compiler_params=_cparams(("arbitrary", "arbitrary")),
        name="gather",
    )(bounds.reshape(-1), rank, hm, aff_hi, aff_lo)


def _ffn_body(xe_ref, gs_ref, wg_ref, wu_ref, wd_ref, ye_ref, acc_ref):
    f = pl.program_id(1)
    nb = xe_ref.shape[0]
    wg = wg_ref[0].astype(BF16)
    wu = wu_ref[0].astype(BF16)
    wd = wd_ref[0].astype(BF16)
    for b in range(nb):
        xe = xe_ref[b, 0]
        a = _dot(xe, wg)
        u = _dot(xe, wu)
        hsw = (a * jax.nn.sigmoid(a) * u).astype(BF16)
        part = _dot(hsw, wd)

        @pl.when(f == 0)
        def _():
            acc_ref[b] = part

        @pl.when(f != 0)
        def _():
            acc_ref[b] += part

    @pl.when(f == pl.num_programs(1) - 1)
    def _():
        lane = lax.broadcasted_iota(jnp.int32, gs_ref.shape[2:], 1)
        mine = lane == pl.program_id(0)
        for b in range(nb):
            gate = jnp.sum(jnp.where(mine, gs_ref[b, 0], 0.0), axis=-1, keepdims=True)
            ye_ref[b, 0] = (acc_ref[b] * gate).astype(BF16)


def _ffn(xe, gs, wg, wu, wd, ft=1024):
    b, e, cap, d = xe.shape
    ff = wg.shape[2]
    one = pl.Buffered(1)
    return pl.pallas_call(
        _ffn_body,
        grid=(e, ff // ft),
        in_specs=[pl.BlockSpec((b, 1, cap, d), lambda ei, fi: (0, ei, 0, 0), pipeline_mode=one),
                  pl.BlockSpec((b, 1, cap, LANES), lambda ei, fi: (0, ei, 0, 0), pipeline_mode=one),
                  pl.BlockSpec((1, d, ft), lambda ei, fi: (ei, 0, fi)),
                  pl.BlockSpec((1, d, ft), lambda ei, fi: (ei, 0, fi)),
                  pl.BlockSpec((1, ft, d), lambda ei, fi: (ei, fi, 0))],
        out_specs=pl.BlockSpec((b, 1, cap, d), lambda ei, fi: (0, ei, 0, 0)),
        out_shape=jax.ShapeDtypeStruct((b, e, cap, d), BF16),
        scratch_shapes=[pltpu.VMEM((b, cap, d), F32)],
        compiler_params=_cparams(("arbitrary", "arbitrary")),
        name="ffn",
    )(xe, gs, wg, wu, wd)


def _scatter_body(bounds_ref, rank_ref, ye_ref, x2_ref, g_ref, out_ref, acc_ref, *, eg):
    w = SCATTER_WINDOW
    ts = x2_ref.shape[1]
    ne, cap = ye_ref.shape[1], ye_ref.shape[2]
    starts, passes = _moe_windows(bounds_ref, pl.program_id(0) * ne * LANES, ne, pl.program_id(1), ts, w)
    rank = rank_ref[0]
    acc_ref[...] = x2_ref[0]
    iota_w = lax.broadcasted_iota(jnp.int32, (1, w), 1)

    def one_pass(p, carry):
        for g0 in range(0, ne, eg):
            onehots, rows = [], []
            for e in range(g0, g0 + eg):
                lo = starts[e] + p * w
                phys = pl.multiple_of(jnp.minimum(lo, cap - w), BF16_ROWS)
                slot = (iota_w + phys).astype(F32)
                hit = (rank[:, e:e + 1] == slot) & (slot >= lo.astype(F32))
                onehots.append(jnp.where(hit, 1.0, 0.0).astype(BF16))
                rows.append(ye_ref[0, e, pl.ds(phys, w), :])
            acc_ref[...] += _dot(jnp.concatenate(onehots, axis=1), jnp.concatenate(rows, axis=0))
        return carry

    lax.fori_loop(0, passes, one_pass, 0)
    out_ref[0] = _rms(acc_ref[...], g_ref[...])


def _scatter(bounds, rank_t, ye, x2, g, eg=8):
    b, s, e = rank_t.shape
    cap, d = ye.shape[2], ye.shape[3]
    ts = SCATTER_TOKENS
    return pl.pallas_call(
        functools.partial(_scatter_body, eg=eg),
        grid_spec=pltpu.PrefetchScalarGridSpec(
            num_scalar_prefetch=1,
            grid=(b, s // ts),
            in_specs=[pl.BlockSpec((1, ts, e), lambda bi, ti, bnd: (bi, ti, 0)),
                      pl.BlockSpec((1, e, cap, d), lambda bi, ti, bnd: (bi, 0, 0, 0),
                                   pipeline_mode=pl.Buffered(1)),
                      pl.BlockSpec((1, ts, d), lambda bi, ti, bnd: (bi, ti, 0)),
                      pl.BlockSpec((1, d), lambda bi, ti, bnd: (0, 0))],
            out_specs=pl.BlockSpec((1, ts, d), lambda bi, ti, bnd: (bi, ti, 0)),
            scratch_shapes=[pltpu.VMEM((ts, d), F32)],
        ),
        out_shape=jax.ShapeDtypeStruct((b, s, d), F32),
        compiler_params=_cparams(("arbitrary", "arbitrary")),
        name="scatter",
    )(bounds.reshape(-1), rank_t, ye, x2, g)


def kernel(x, mem, ln_mix_g, w_in, hy_conv_w, hy_conv_b, filt_w1, filt_b1, filt_freq1, filt_w2, filt_b2,
           filt_freq2, filt_w3, hy_skip, w_up_hy, w_up_attn, w_gate, b_gate, w_out, ln_x_g, ln_mem_g,
           w_q_x, w_kv_mem, w_o_x, ln_moe_g, w_router, w_e_gate, w_e_up, w_e_down, ln_f_g):
    b, s, d = x.shape
    n = b * s
    x2d = x.reshape(n, d)

    half = HEAD_DIM // 2
    inv = ROPE_THETA ** (-jnp.arange(0, HEAD_DIM, 2, dtype=F32) / HEAD_DIM)
    ang = jnp.arange(s, dtype=F32)[:, None] * inv[None, :]
    lane = np.arange(LANES)
    cos_t = jnp.cos(ang)[:, lane % half]
    sin_t = jnp.sin(ang)[:, lane % half] * jnp.asarray(np.where(lane % HEAD_DIM < half, -1.0, 1.0), F32)[None, :]

    w_all = jnp.concatenate([w_in, w_gate], axis=1).astype(BF16)
    p_hy, *qkv, gates = _proj(x2d, ln_mix_g[None], w_all, b_gate[None], cos_t, sin_t, b, s)

    t_col = jnp.linspace(0.0, 1.0, s, dtype=F32)[:, None]
    grid_col = 2.0 * math.pi * jnp.arange(s, dtype=F32)[:, None] / s
    bands = jnp.linspace(1e-4, HY_BANDS - 1, HY_BANDS, dtype=F32)[None, :]
    feat = _filt(t_col, grid_col, bands, filt_w1, filt_b1, filt_freq1, filt_w2, filt_b2, filt_freq2)
    delta = jnp.linspace(math.log(HY_TARGET) / HY_SLOW_PCT, math.log(HY_TARGET) / HY_FAST_PCT,
                         HY_WIDTH, dtype=F32)[None, :]
    tables = _fft_tables(s)
    khat = _hyfilt(t_col, delta, feat, filt_w3, tables[1], tables[3])
    z = _hyena(p_hy, hy_conv_w, hy_conv_b[None], hy_skip, khat, tables)

    outs, lses = [], []
    for g, (window, dil) in enumerate(DIL_PAIRS):
        o_g, l_g = _attn_group(*qkv[3 * g:3 * g + 3], g, window // (2 * dil))
        outs.append(o_g)
        lses.append(l_g)

    kv = _memkv(mem, ln_mem_g[None], w_kv_mem.astype(BF16))
    wr_pad = jnp.pad(w_router, ((0, 0), (0, LANES - N_EXPERTS)))
    wr_hi = wr_pad.astype(BF16)
    wr_lo = (wr_pad - wr_hi.astype(F32)).astype(BF16)
    x2, hm, aff = _merge(x2d, z.reshape(n, HY_WIDTH), outs, lses, gates,
                         w_up_hy.astype(BF16), w_up_attn.astype(BF16), w_out.astype(BF16),
                         ln_x_g[None], w_q_x.astype(BF16), kv, w_o_x.astype(BF16), ln_moe_g[None],
                         wr_hi, wr_lo, s)

    cap = max(1, EC_FACTOR * s // N_EXPERTS)
    aff3 = aff.reshape(b, s, LANES)
    aff_hi = aff3.astype(BF16)
    aff_lo = (aff3 - aff_hi.astype(F32)).astype(BF16)
    rank, bounds = _topk(aff3[:, :, :N_EXPERTS].transpose(0, 2, 1), cap, MOE_BOUND_STEP)
    xe, gs = _gather(bounds, rank, hm.reshape(b, s, d), aff_hi, aff_lo, cap)
    ye = _ffn(xe, gs, w_e_gate, w_e_up, w_e_down)
    return _scatter(bounds, rank.transpose(0, 2, 1), ye, x2.reshape(b, s, d), ln_f_g[None])
```

```python
import functools
import math

import numpy as np
import jax
import jax.numpy as jnp
from jax import lax
from jax.experimental import pallas as pl
from jax.experimental.pallas import tpu as pltpu

F32 = jnp.float32
BF16 = jnp.bfloat16
HIGHEST = lax.Precision.HIGHEST

EPS = 1e-6
HY_WIDTH = 768
HY_BANDS = 16
HY_FFN = 64
HY_FAST_PCT = 0.3
HY_SLOW_PCT = 1.5
HY_TARGET = 1e-2
HEAD_DIM = 64
HEADS_PER_GROUP = 4
DIL_PAIRS = ((128, 1), (512, 4), (2048, 16))
N_GROUPS = len(DIL_PAIRS)
ATTN_WIDTH = N_GROUPS * HEADS_PER_GROUP * HEAD_DIM
ATTN_OUT = HEADS_PER_GROUP * HEAD_DIM
ROPE_THETA = 10000.0
X_HEADS = 4
X_HEAD_DIM = 128
N_EXPERTS = 16
EC_FACTOR = 2

LANES = 128
SUBLANES = 8
MXU_COLS = 256
VMEM_LIMIT = 56 * 1024 * 1024

FFT_N2 = 128
HY_CB = 128
FFT_UNROLL = 8
FFT_PAD = 8
MOE_BOUND_STEP = 256
GATHER_TOKENS, GATHER_WINDOW = 256, 64
SCATTER_TOKENS, SCATTER_WINDOW = 512, 128
BF16_ROWS = 16


def _cparams(sem, vmem=VMEM_LIMIT):
    return pltpu.CompilerParams(dimension_semantics=sem, vmem_limit_bytes=vmem)


def _rms(x, g):
    return x * lax.rsqrt(jnp.mean(x * x, axis=-1, keepdims=True) + EPS) * g


def _dot(a, b):
    return jnp.dot(a, b, preferred_element_type=F32)


def _dot_hi(a, b):
    return jnp.dot(a, b, precision=HIGHEST, preferred_element_type=F32)


def _split(a):
    hi = a.astype(BF16)
    return hi, (a - hi.astype(F32)).astype(BF16)


def _dot_split(a, b):
    a_hi, a_lo = _split(a)
    b_hi, b_lo = _split(b)
    return _dot(a_hi, b_hi) + _dot(a_lo, b_hi) + _dot(a_hi, b_lo)


def _dot_t(a, b):
    return lax.dot_general(a, b, (((1,), (1,)), ((), ())), preferred_element_type=F32)


def _proj_body(x_ref, g_ref, w_ref, bg_ref, cos_ref, sin_ref, *rest, hyc, aw):
    ng = N_GROUPS
    perm_refs = (None,) + rest[:ng - 1]
    phy_ref = rest[ng - 1]
    qkv_refs = rest[ng:ng + 3 * ng]
    gate_ref = rest[ng + 3 * ng]
    h = _rms(x_ref[...], g_ref[...]).astype(BF16)
    phy = _dot(h, w_ref[:, :hyc])
    for j in range(hyc // HY_CB):
        phy_ref[0, j] = phy[:, j * HY_CB:(j + 1) * HY_CB]
    cos = cos_ref[...]
    sin = sin_ref[...]
    tm = cos.shape[0]
    lane = lax.broadcasted_iota(jnp.int32, (tm, LANES), 1)
    first = (lane % HEAD_DIM) < (HEAD_DIM // 2)

    def rope(t, scale):
        chunks = []
        for j in range(aw // LANES):
            tj = t[:, j * LANES:(j + 1) * LANES]
            partner = jnp.where(first, pltpu.roll(tj, LANES - HEAD_DIM // 2, 1),
                                pltpu.roll(tj, HEAD_DIM // 2, 1))
            chunks.append(((tj * cos + partner * sin) * scale).astype(BF16))
        return chunks

    gl = ATTN_OUT // LANES
    q = rope(_dot(h, w_ref[:, hyc:hyc + aw]), HEAD_DIM ** -0.5)
    k = rope(_dot(h, w_ref[:, hyc + aw:hyc + 2 * aw]), 1.0)
    vf = _dot(h, w_ref[:, hyc + 2 * aw:hyc + 3 * aw]).astype(BF16)
    v = [vf[:, j * LANES:(j + 1) * LANES] for j in range(aw // LANES)]
    for g in range(ng):
        for i, t in enumerate((q, k, v)):
            out_ref = qkv_refs[3 * g + i]
            tg = jnp.concatenate(t[g * gl:(g + 1) * gl], axis=1)
            if perm_refs[g] is not None:
                tg = _dot(perm_refs[g][...], tg).astype(BF16)
            dil = out_ref.shape[1]
            rows = tm // dil
            for r in range(dil):
                out_ref[0, r] = tg[r * rows:(r + 1) * rows, :]
    gate_ref[...] = jax.nn.sigmoid(_dot(h, w_ref[:, hyc + 3 * aw:]) + bg_ref[...]).astype(BF16)


def _proj(x2d, g, w_all, b_gate, cos_t, sin_t, batch, seq, tm=256):
    n, d = x2d.shape
    hyc = 3 * HY_WIDTH
    aw = ATTN_WIDTH
    gd = w_all.shape[1] - hyc - 3 * aw
    nseq = seq // tm
    row = lambda i: (i, 0)
    const = lambda i: (0, 0)
    perms = []
    for _, dil in DIL_PAIRS[1:]:
        rows = tm // dil
        src = (np.arange(tm) % rows) * dil + np.arange(tm) // rows
        perms.append(jnp.asarray(np.eye(tm)[src], dtype=BF16))
    qkv_specs, qkv_shapes = [], []
    for _, dil in DIL_PAIRS:
        for _ in range(3):
            qkv_specs.append(pl.BlockSpec((1, dil, tm // dil, ATTN_OUT), lambda i: (i // nseq, 0, i % nseq, 0)))
            qkv_shapes.append(jax.ShapeDtypeStruct((batch, dil, seq // dil, ATTN_OUT), BF16))
    return pl.pallas_call(
        functools.partial(_proj_body, hyc=hyc, aw=aw),
        grid=(n // tm,),
        in_specs=[
            pl.BlockSpec((tm, d), row),
            pl.BlockSpec((1, d), const),
            pl.BlockSpec(w_all.shape, const, pipeline_mode=pl.Buffered(1)),
            pl.BlockSpec((1, gd), const),
            pl.BlockSpec((tm, LANES), lambda i: (i % nseq, 0)),
            pl.BlockSpec((tm, LANES), lambda i: (i % nseq, 0)),
        ] + [pl.BlockSpec((tm, tm), const) for _ in perms],
        out_specs=([pl.BlockSpec((1, hyc // HY_CB, tm, HY_CB), lambda i: (i // nseq, 0, i % nseq, 0))]
                   + qkv_specs + [pl.BlockSpec((tm, gd), row)]),
        out_shape=([jax.ShapeDtypeStruct((batch, hyc // HY_CB, seq, HY_CB), F32)] + qkv_shapes
                   + [jax.ShapeDtypeStruct((n, gd), BF16)]),
        compiler_params=_cparams(("arbitrary",)),
        name="proj",
    )(x2d, g, w_all, b_gate, cos_t, sin_t, *perms)


def _filt_body(t_ref, grid_ref, bands_ref, w1t_ref, w1c_ref, w1s_ref, b1_ref, f1_ref,
               w2_ref, b2_ref, f2_ref, feat_ref):
    ang = bands_ref[...] * grid_ref[...]
    pre = (t_ref[...] * w1t_ref[...] + _dot_hi(jnp.cos(ang), w1c_ref[...])
           + _dot_hi(-jnp.sin(ang), w1s_ref[...]))
    h = jnp.sin(f1_ref[...] * (pre + b1_ref[...]))
    feat_ref[...] = jnp.sin(f2_ref[...] * (_dot_hi(h, w2_ref[...]) + b2_ref[...]))


def _filt(t_col, grid_col, bands, w1, b1, f1, w2, b2, f2):
    length = t_col.shape[0]
    nb = bands.shape[1]
    args = (t_col, grid_col, bands, w1[0:1], w1[1:1 + nb], w1[1 + nb:], b1[None], f1[None],
            w2, b2[None], f2[None])
    return pl.pallas_call(
        _filt_body,
        out_shape=jax.ShapeDtypeStruct((length, HY_FFN), F32),
        compiler_params=_cparams(None),
        name="filt",
    )(*args)


def _fft_tables(length):
    n = 2 * length
    n2 = FFT_N2
    n1 = n // n2
    h1 = n1 // 2
    k1 = np.arange(n1)[:, None]
    g_sig = np.zeros((n2, 2 * n1, 2 * h1))
    g_flt = np.zeros((n2, 2 * n1, 2 * h1))
    g_inv = np.zeros((n2, 2 * h1, 2 * n1))
    j = np.arange(h1)[None, :]
    for r in range(n2):
        th = 2 * np.pi * (k1 * (n2 * j + r) % n) / n
        gr, gi = np.cos(th), -np.sin(th)
        g_sig[r, 0::2, :h1] = gr
        g_sig[r, 0::2, h1:] = -gi
        g_sig[r, 1::2, :h1] = gi
        g_sig[r, 1::2, h1:] = gr
        g_flt[r, 0::2, :h1] = gr
        g_flt[r, 1::2, :h1] = gi
        m = n2 * (j + 1) - r
        thb = 2 * np.pi * (k1 * m % n) / n
        live = (m < length).astype(np.float64)
        g_flt[r, 0::2, h1:] = np.cos(thb) * live
        g_flt[r, 1::2, h1:] = np.sin(thb) * live
        wr, wi = (np.cos(th) / n).T, (np.sin(th) / n).T
        g_inv[r, :h1, 0::2] = wr
        g_inv[r, :h1, 1::2] = -wi
        g_inv[r, h1:, 0::2] = wi
        g_inv[r, h1:, 1::2] = wr
    a = np.arange(n2)
    th2 = 2 * np.pi * (np.outer(a, a) % n2) / n2
    c2, s2 = np.cos(th2), np.sin(th2)
    f_fwd = np.block([[c2, s2], [-s2, c2]])
    f_inv = np.block([[c2, -s2], [s2, c2]])
    cast = lambda z: jnp.asarray(z, dtype=F32).astype(BF16)
    return cast(g_sig), cast(g_flt), cast(g_inv), cast(f_fwd), cast(f_inv)


def _fft_stage1(src_a_ref, src_b_ref, tab_ref, work_ref, n1, is_filter):
    n2 = FFT_N2
    grp = SUBLANES

    def body(i, c):
        r0 = pl.multiple_of(i * grp, grp)
        b0 = pl.multiple_of(n2 - grp - r0, grp) if is_filter else r0
        a = jnp.swapaxes(src_a_ref[:, pl.ds(r0, grp), :], 0, 1)
        b = jnp.swapaxes(src_b_ref[:, pl.ds(b0, grp), :], 0, 1)
        outs = []
        for s in range(grp):
            rhs = jnp.concatenate([a[s], b[grp - 1 - s] if is_filter else b[s]], axis=0).astype(BF16)
            outs.append(_dot(tab_ref[r0 + s], rhs))
        work_ref[:, pl.ds(r0, grp), :] = jnp.swapaxes(jnp.stack(outs, axis=0), 0, 1)
        return c

    lax.fori_loop(0, n2 // grp, body, 0, unroll=2)


def _hyfilt_body(t_ref, ts_ref, delta_ref, feat_ref, feats_ref, w3f_ref, w3b_ref, gflt_ref, ffwd_ref, khat_ref,
                 hf_ref, hb_ref, work_ref, *, length, rc):
    n2 = FFT_N2
    n1 = 2 * length // n2
    tiles = rc // n2
    cb = hf_ref.shape[2]

    def gen(i, c):
        r0 = pl.multiple_of(i * rc, rc)
        j0 = pl.multiple_of(i * tiles, tiles)
        delta = jnp.abs(delta_ref[...])
        hf = _dot_split(feat_ref[pl.ds(r0, rc), :], w3f_ref[...]) * jnp.exp(-t_ref[pl.ds(r0, rc), :] * delta)
        hb = _dot_split(feats_ref[pl.ds(r0, rc), :], w3b_ref[...]) * jnp.exp(-ts_ref[pl.ds(r0, rc), :] * delta)
        hf_ref[pl.ds(j0, tiles), :n2, :] = hf.reshape(tiles, n2, cb)
        hb_ref[pl.ds(j0, tiles), :n2, :] = hb.reshape(tiles, n2, cb)
        return c

    lax.fori_loop(0, length // rc, gen, 0)
    _fft_stage1(hf_ref, hb_ref, gflt_ref, work_ref, n1, True)

    side = MXU_COLS // cb

    def stage2(i, c):
        k1s = [i * side + v for v in range(side)]
        blks = [work_ref[pl.ds(pl.multiple_of(2 * k1, 2), 2), :n2, :].reshape(2 * n2, cb).astype(BF16)
                for k1 in k1s]
        x = _dot(ffwd_ref[...], jnp.concatenate(blks, axis=1)).astype(BF16)
        for v, k1 in enumerate(k1s):
            khat_ref[0, 0, pl.ds(pl.multiple_of(k1 * (2 * n2), 2 * n2), 2 * n2), :] = x[:, v * cb:(v + 1) * cb]
        return c

    lax.fori_loop(0, n1 // side, stage2, 0, unroll=FFT_UNROLL // side)


def _hyfilt(t_col, delta, feat, w3, g_flt, f_fwd, rc=512):
    s = t_col.shape[0]
    cb = HY_CB
    ncb = HY_WIDTH // cb
    n2 = FFT_N2
    n1 = 2 * s // n2
    norder = w3.shape[1] // (2 * HY_WIDTH)
    one = pl.Buffered(1)

    def full(a):
        nd = a.ndim
        return pl.BlockSpec(a.shape, lambda o, c: (0,) * nd, pipeline_mode=one)

    shift = lambda a: jnp.concatenate([a[1:], jnp.zeros_like(a[:1])], axis=0)
    return pl.pallas_call(
        functools.partial(_hyfilt_body, length=s, rc=rc),
        grid=(norder, ncb),
        in_specs=[full(t_col), full(t_col),
                  pl.BlockSpec((1, cb), lambda o, c: (0, c)),
                  full(feat), full(feat),
                  pl.BlockSpec((HY_FFN, cb), lambda o, c: (0, 2 * ncb * o + c)),
                  pl.BlockSpec((HY_FFN, cb), lambda o, c: (0, 2 * ncb * o + ncb + c)),
                  full(g_flt), full(f_fwd)],
        out_specs=pl.BlockSpec((1, 1, 2 * n1 * n2, cb), lambda o, c: (o, c, 0, 0)),
        out_shape=jax.ShapeDtypeStruct((norder, ncb, 2 * n1 * n2, cb), BF16),
        scratch_shapes=[pltpu.VMEM((n1 // 2, n2 + FFT_PAD, cb), F32)] * 2
        + [pltpu.VMEM((2 * n1, n2 + FFT_PAD, cb), F32)],
        compiler_params=_cparams(("arbitrary", "arbitrary")),
        name="hyfilt",
    )(t_col, shift(t_col), delta, feat, shift(feat), w3, w3, g_flt, f_fwd)


def _hyena_body(pz_ref, pg1_ref, pg2_ref, cwz_ref, cwg1_ref, cwg2_ref, cbz_ref, cbg1_ref, cbg2_ref,
                skip_ref, khat_ref, gsig_ref, ginv_ref, ffwd_ref, finv_ref,
                out_ref,
                za_ref, zb_ref, ya_ref, yb_ref, work_ref, *, length, rc):
    n2 = FFT_N2
    n1 = 2 * length // n2
    h1 = n1 // 2
    tiles = rc // n2
    cb = za_ref.shape[2]
    side = MXU_COLS // cb
    sub = SUBLANES

    def conv3(p_ref, b, r0, w_ref, bias_ref):
        p = p_ref[b, 0, pl.ds(r0, rc), :]
        before = p_ref[b, 0, pl.ds(pl.multiple_of(jnp.maximum(r0 - sub, 0), sub), sub), :][sub - 1:sub, :]
        after = p_ref[b, 0, pl.ds(pl.multiple_of(jnp.minimum(r0 + rc, length - sub), sub), sub), :][0:1, :]
        before = jnp.where(r0 == 0, 0.0, before)
        after = jnp.where(r0 + rc == length, 0.0, after)
        row = lax.broadcasted_iota(jnp.int32, p.shape, 0)
        prev = jnp.where(row == 0, before, pltpu.roll(p, 1, 0))
        nxt = jnp.where(row == rc - 1, after, pltpu.roll(p, rc - 1, 0))
        return prev * w_ref[0:1, :] + p * w_ref[1:2, :] + nxt * w_ref[2:3, :] + bias_ref[...]

    def stage2(o):
        def body(i, c):
            k1s = [i * FFT_UNROLL + u for u in range(FFT_UNROLL)]
            blks = [work_ref[pl.ds(pl.multiple_of(2 * k1, 2), 2), :n2, :].reshape(2 * n2, cb).astype(BF16)
                    for k1 in k1s]
            outs = []
            for u in range(0, FFT_UNROLL, side):
                x = _dot(ffwd_ref[...], jnp.concatenate(blks[u:u + side], axis=1))
                ys = []
                for v in range(side):
                    k1 = k1s[u + v]
                    kh = khat_ref[o, 0, pl.ds(pl.multiple_of(k1 * (2 * n2), 2 * n2), 2 * n2), :].astype(F32)
                    xr, xi = x[:n2, v * cb:(v + 1) * cb], x[n2:, v * cb:(v + 1) * cb]
                    kr, ki = kh[:n2], kh[n2:]
                    ys.append(jnp.concatenate([xr * kr - xi * ki, xr * ki + xi * kr], axis=0).astype(BF16))
                out = _dot(finv_ref[...], jnp.concatenate(ys, axis=1))
                outs.extend(out[:, v * cb:(v + 1) * cb] for v in range(side))
            for k1, out in zip(k1s, outs):
                work_ref[pl.ds(pl.multiple_of(2 * k1, 2), 2), :n2, :] = out.reshape(2, n2, cb)
            return c
        lax.fori_loop(0, n1 // FFT_UNROLL, body, 0)

    def stage3():
        grp = SUBLANES

        def body(i, c):
            r0 = pl.multiple_of(i * grp, grp)
            wk = jnp.swapaxes(work_ref[:, pl.ds(r0, grp), :], 0, 1)
            res = jnp.stack([_dot(ginv_ref[r0 + s], wk[s].astype(BF16)) for s in range(grp)], axis=0)
            ya_ref[:, pl.ds(r0, grp), :] = jnp.swapaxes(res[:, :h1], 0, 1)
            yb_ref[:, pl.ds(r0, grp), :] = jnp.swapaxes(res[:, h1:], 0, 1)
            return c
        lax.fori_loop(0, n2 // grp, body, 0, unroll=2)

    def rows3(ref, i):
        return ref.at[pl.ds(pl.multiple_of(i * tiles, tiles), tiles), :n2, :]

    def load_z(i, c):
        r0 = pl.multiple_of(i * rc, rc)
        rows3(za_ref, i)[...] = conv3(pz_ref, 0, r0, cwz_ref, cbz_ref).reshape(tiles, n2, cb)
        rows3(zb_ref, i)[...] = conv3(pz_ref, 1, r0, cwz_ref, cbz_ref).reshape(tiles, n2, cb)
        return c

    lax.fori_loop(0, length // rc, load_z, 0)
    for o, (pg_ref, cw_ref, cb_ref) in enumerate(((pg1_ref, cwg1_ref, cbg1_ref), (pg2_ref, cwg2_ref, cbg2_ref))):
        _fft_stage1(za_ref, zb_ref, gsig_ref, work_ref, n1, False)
        stage2(o)
        stage3()
        skip = skip_ref[o:o + 1, :]

        def gate(i, c):
            r0 = pl.multiple_of(i * rc, rc)
            rows = pl.ds(r0, rc)
            ya, yb = (rows3(r, i)[...].reshape(rc, cb) for r in (ya_ref, yb_ref))
            za, zb = (rows3(r, i)[...].reshape(rc, cb) for r in (za_ref, zb_ref))
            new_a = conv3(pg_ref, 0, r0, cw_ref, cb_ref) * (ya + skip * za)
            new_b = conv3(pg_ref, 1, r0, cw_ref, cb_ref) * (yb + skip * zb)
            if o == 0:
                rows3(za_ref, i)[...] = new_a.reshape(tiles, n2, cb)
                rows3(zb_ref, i)[...] = new_b.reshape(tiles, n2, cb)
            else:
                out_ref[0, rows, :] = new_a
                out_ref[1, rows, :] = new_b
            return c

        lax.fori_loop(0, length // rc, gate, 0)


def _hyena(p_hy, conv_w, conv_b, skip, khat, tables, rc=512):
    b, _, s, _ = p_hy.shape
    w = HY_WIDTH
    cb = HY_CB
    ncb = w // cb
    n2 = FFT_N2
    n1 = 2 * s // n2
    g_sig, _, g_inv, f_fwd, f_inv = tables
    one = pl.Buffered(1)

    def pspec(off):
        return pl.BlockSpec((2, 1, s, cb), lambda c, p: (p, off + c, 0, 0), pipeline_mode=one)

    def cwspec(off):
        return pl.BlockSpec((3, cb), lambda c, p: (0, off + c))

    def cbspec(off):
        return pl.BlockSpec((1, cb), lambda c, p: (0, off + c))

    def full(a):
        nd = a.ndim
        return pl.BlockSpec(a.shape, lambda c, p: (0,) * nd, pipeline_mode=one)

    in_specs = [
        pspec(0), pspec(ncb), pspec(2 * ncb),
        cwspec(0), cwspec(ncb), cwspec(2 * ncb),
        cbspec(0), cbspec(ncb), cbspec(2 * ncb),
        pl.BlockSpec((2, cb), lambda c, p: (0, c)),
        pl.BlockSpec((khat.shape[0], 1, 2 * n1 * n2, cb), lambda c, p: (0, c, 0, 0), pipeline_mode=one),
        full(g_sig), full(g_inv), full(f_fwd), full(f_inv),
    ]
    return pl.pallas_call(
        functools.partial(_hyena_body, length=s, rc=rc),
        grid=(ncb, b // 2),
        in_specs=in_specs,
        out_specs=pl.BlockSpec((2, s, cb), lambda c, p: (p, 0, c)),
        out_shape=jax.ShapeDtypeStruct((b, s, w), F32),
        scratch_shapes=[pltpu.VMEM((n1 // 2, n2 + FFT_PAD, cb), F32)] * 4
        + [pltpu.VMEM((2 * n1, n2 + FFT_PAD, cb), F32)],
        compiler_params=_cparams(("arbitrary", "arbitrary")),
        name="hyena",
    )(p_hy, p_hy, p_hy, conv_w, conv_w, conv_w, conv_b, conv_b, conv_b,
      skip, khat, g_sig, g_inv, f_fwd, f_inv)


def _attn_body(q_ref, k_ref, v_ref, o_ref, lse_ref, *, n_side, qb):
    lr = q_ref.shape[2]
    kw = qb + 2 * n_side
    nh = HEADS_PER_GROUP
    lane_head = lax.broadcasted_iota(jnp.int32, (qb, ATTN_OUT), 1) // HEAD_DIM
    qi = lax.broadcasted_iota(jnp.int32, (nh * qb, kw), 0) % qb
    ki = lax.broadcasted_iota(jnp.int32, (nh * qb, kw), 1)
    band = [jnp.where(jnp.abs(ki - qi - shift) <= n_side, 0.0, -1e30) for shift in (0, n_side, 2 * n_side)]

    def body(i, c):
        q0 = pl.multiple_of(i * qb, qb)
        w0 = pl.multiple_of(jnp.clip(q0 - n_side, 0, lr - kw), n_side)
        q = q_ref[0, 0, pl.ds(q0, qb), :]
        kwin = k_ref[0, 0, pl.ds(w0, kw), :]
        vwin = v_ref[0, 0, pl.ds(w0, kw), :]
        shift = q0 - w0
        bias = jnp.where(shift == n_side, band[1], jnp.where(shift == 0, band[0], band[2]))
        qs = jnp.concatenate([jnp.where(lane_head == h, q, jnp.zeros_like(q)) for h in range(nh)], axis=0)
        s = _dot_t(qs, kwin) + bias
        m = jnp.max(s, axis=-1, keepdims=True)
        p = jnp.exp(s - m)
        l = jnp.sum(p, axis=-1, keepdims=True)
        pv = _dot(p.astype(BF16), vwin) / l
        lse_rows = m + jnp.log(l)
        o = jnp.zeros((qb, ATTN_OUT), F32)
        lse = jnp.zeros((qb, ATTN_OUT), F32)
        for h in range(nh):
            mine = lane_head == h
            o = jnp.where(mine, pv[h * qb:(h + 1) * qb], o)
            lse = jnp.where(mine, lse_rows[h * qb:(h + 1) * qb], lse)
        o_ref[0, pl.ds(q0, qb), :] = o
        lse_ref[0, pl.ds(q0, qb), :] = lse
        return c

    lax.fori_loop(0, lr // qb, body, 0, unroll=2)


def _attn_group(q, k, v, g, n_side, qb=128):
    b, dil, lr, _ = q.shape
    in_spec = pl.BlockSpec((1, 1, lr, ATTN_OUT), lambda bi, r: (bi, r, 0, 0))
    out_spec = pl.BlockSpec((1, lr, ATTN_OUT), lambda bi, r: (bi, 0, r))
    o, lse = pl.pallas_call(
        functools.partial(_attn_body, n_side=n_side, qb=qb),
        grid=(b, dil),
        in_specs=[in_spec, in_spec, in_spec],
        out_specs=[out_spec, out_spec],
        out_shape=[jax.ShapeDtypeStruct((b, lr, dil * ATTN_OUT), F32)] * 2,
        compiler_params=_cparams(("arbitrary", "arbitrary")),
        name=f"attn{g}",
    )(q, k, v)
    return o, lse


def _memkv_body(mem_ref, g_ref, w_ref, kv_ref):
    mn = _rms(mem_ref[0], g_ref[...]).astype(BF16)
    kv_ref[0] = _dot(mn, w_ref[...]).astype(BF16)


def _memkv(mem, g, w_kv):
    b, m, d = mem.shape
    n = w_kv.shape[1]
    return pl.pallas_call(
        _memkv_body,
        grid=(b,),
        in_specs=[pl.BlockSpec((1, m, d), lambda i: (i, 0, 0)),
                  pl.BlockSpec((1, d), lambda i: (0, 0)),
                  pl.BlockSpec((d, n), lambda i: (0, 0))],
        out_specs=pl.BlockSpec((1, m, n), lambda i: (i, 0, 0)),
        out_shape=jax.ShapeDtypeStruct((b, m, n), BF16),
        compiler_params=_cparams(("arbitrary",)),
        name="memkv",
    )(mem, g, w_kv)


def _merge_body(x_ref, z_ref, o0_ref, o1_ref, o2_ref, l0_ref, l1_ref, l2_ref, gate_ref,
                wuh_ref, wua_ref, wout_ref, gx_ref, wq_ref, kv_ref, wo_ref, gm_ref, wr_ref,
                p1_ref, p2_ref, x2_ref, hm_ref, aff_ref, *, nsplit):
    rows = x_ref.shape[0] // nsplit
    for part in range(nsplit):
        sl = slice(part * rows, (part + 1) * rows)

        def natural(ref, perm_ref):
            dil = ref.shape[2] // ATTN_OUT
            rr = rows // dil
            blk = ref[0, part * rr:(part + 1) * rr, :]
            if dil == 1:
                return blk
            pieces = [blk[:, r * ATTN_OUT:(r + 1) * ATTN_OUT] for r in range(dil)]
            if rr % SUBLANES == 0 and dil % SUBLANES == 0:
                return jnp.swapaxes(jnp.stack(pieces, axis=0), 0, 1).reshape(rows, ATTN_OUT)
            hi, lo = _split(jnp.concatenate(pieces, axis=0))
            return _dot(perm_ref[...], hi) + _dot(perm_ref[...], lo)

        outs = [natural(r, p) for r, p in ((o0_ref, None), (o1_ref, p1_ref), (o2_ref, p2_ref))]
        lses = [natural(r, p) for r, p in ((l0_ref, None), (l1_ref, p1_ref), (l2_ref, p2_ref))]
        _merge_rows(sl, outs, lses, x_ref, z_ref, gate_ref,
                    wuh_ref, wua_ref, wout_ref, gx_ref, wq_ref, kv_ref, wo_ref, gm_ref, wr_ref,
                    x2_ref, hm_ref, aff_ref)


def _merge_rows(sl, outs, lses, x_ref, z_ref, gate_ref,
                wuh_ref, wua_ref, wout_ref, gx_ref, wq_ref, kv_ref, wo_ref, gm_ref, wr_ref,
                x2_ref, hm_ref, aff_ref):
    d = x_ref.shape[1]
    l0, l1, l2 = lses
    mx = jnp.maximum(jnp.maximum(l0, l1), l2)
    e0, e1, e2 = jnp.exp(l0 - mx), jnp.exp(l1 - mx), jnp.exp(l2 - mx)
    attn = (e0 * outs[0] + e1 * outs[1] + e2 * outs[2]) / (e0 + e1 + e2)
    y_hy = _dot(z_ref[sl, :].astype(BF16), wuh_ref[...])
    y_at = _dot(attn.astype(BF16), wua_ref[...])
    gates = gate_ref[sl, :].astype(F32)
    mix = gates[:, :d] * y_hy + gates[:, d:] * y_at
    x1 = x_ref[sl, :] + _dot(mix.astype(BF16), wout_ref[...])

    hx = _rms(x1, gx_ref[...]).astype(BF16)
    qx = _dot(hx, wq_ref[...]).astype(BF16)
    kv = kv_ref[0]
    xw = X_HEADS * X_HEAD_DIM
    heads = []
    for h in range(X_HEADS):
        hc = slice(h * X_HEAD_DIM, (h + 1) * X_HEAD_DIM)
        s = _dot_t(qx[:, hc], kv[:, hc]) * (X_HEAD_DIM ** -0.5)
        m = jnp.max(s, axis=-1, keepdims=True)
        p = jnp.exp(s - m)
        p = p / jnp.sum(p, axis=-1, keepdims=True)
        heads.append(_dot(p.astype(BF16), kv[:, xw + h * X_HEAD_DIM:xw + (h + 1) * X_HEAD_DIM]))
    ox = jnp.concatenate(heads, axis=-1).astype(BF16)
    x2 = x1 + _dot(ox, wo_ref[...])
    x2_ref[sl, :] = x2

    hm = _rms(x2, gm_ref[...])
    hm_hi, hm_lo = _split(hm)
    hm_ref[sl, :] = hm_hi
    nr = hm.shape[0]
    cross = _dot(jnp.concatenate([hm_hi, hm_lo], axis=0), wr_ref[...])
    logits = (cross[:nr, :LANES] + cross[:nr, LANES:]) + (cross[nr:, :LANES] + cross[nr:, LANES:])
    lane = lax.broadcasted_iota(jnp.int32, logits.shape, 1)
    logits = jnp.where(lane < N_EXPERTS, logits, -1e30)
    m = jnp.max(logits, axis=-1, keepdims=True)
    p = jnp.exp(logits - m)
    aff_ref[sl, :] = p / jnp.sum(p, axis=-1, keepdims=True)


def _merge(x2d, z2d, outs, lses, gates, wuh, wua, wout, gx, wq, kv, wo, gm, wr_cat, seq, tm=512, nsplit=2):
    n, d = x2d.shape
    nseq = seq // tm
    row = lambda i: (i, 0)
    const = lambda i: (0, 0)

    def rspec(a):
        return pl.BlockSpec((tm, a.shape[1]), row)

    def aspec(a):
        dil = a.shape[2] // ATTN_OUT
        return pl.BlockSpec((1, tm // dil, a.shape[2]), lambda i: (i // nseq, i % nseq, 0))

    def cspec(a):
        return pl.BlockSpec(a.shape, const)

    rows = tm // nsplit
    perms = []
    for a in outs[1:]:
        dil = a.shape[2] // ATTN_OUT
        src = (np.arange(rows) % dil) * (rows // dil) + np.arange(rows) // dil
        perms.append(jnp.asarray(np.eye(rows)[src], dtype=BF16))

    in_arrays = [x2d, z2d, *outs, *lses, gates, wuh, wua, wout, gx, wq, kv, wo, gm, wr_cat, *perms]
    in_specs = ([rspec(a) for a in in_arrays[:2]] + [aspec(a) for a in in_arrays[2:8]] + [rspec(gates)]
                + [cspec(a) for a in in_arrays[9:14]]
                + [pl.BlockSpec((1,) + kv.shape[1:], lambda i: (i // nseq, 0, 0))]
                + [cspec(a) for a in in_arrays[15:]])
    return pl.pallas_call(
        functools.partial(_merge_body, nsplit=nsplit),
        grid=(n // tm,),
        in_specs=in_specs,
        out_specs=[pl.BlockSpec((tm, d), row), pl.BlockSpec((tm, d), row), pl.BlockSpec((tm, LANES), row)],
        out_shape=[jax.ShapeDtypeStruct((n, d), F32), jax.ShapeDtypeStruct((n, d), BF16),
                   jax.ShapeDtypeStruct((n, LANES), F32)],
        compiler_params=_cparams(("arbitrary",)),
        name="merge",
    )(*in_arrays)


def _topk_body(aff_ref, tri_ref, rank_ref, bounds_ref, *, cap, tok_block):
    a = aff_ref[0]
    e, s = a.shape

    def count(mask):
        return jnp.sum(jnp.where(mask, 1.0, 0.0), axis=-1, keepdims=True)

    def as_float(bits):
        return pltpu.bitcast(jnp.broadcast_to(bits, (e, LANES)), F32)[:, 0:1]

    def search(i, thr):
        cand = thr | (jnp.int32(1) << (30 - i))
        return jnp.where(count(a >= as_float(cand)) >= cap, cand, thr)

    thr = as_float(lax.fori_loop(0, 31, search, jnp.zeros((e, 1), jnp.int32)))
    gt = a > thr
    eq = a == thr
    need = cap - count(gt)

    def prefix_excl(mask):
        mf = jnp.where(mask, 1.0, 0.0)
        parts = []
        carry = jnp.zeros((e, 1), F32)
        for c in range(s // LANES):
            blk = mf[:, c * LANES:(c + 1) * LANES]
            inc = _dot(blk.astype(BF16), tri_ref[...])
            parts.append(inc - blk + carry)
            carry = carry + inc[:, LANES - 1:LANES]
        return jnp.concatenate(parts, axis=-1)

    sel = gt | (eq & (prefix_excl(eq) < need))
    excl = prefix_excl(sel)
    rank_ref[0] = jnp.where(sel, excl, -1.0)

    tok = lax.broadcasted_iota(jnp.int32, (e, s), 1)
    lane = lax.broadcasted_iota(jnp.int32, (e, LANES), 1)
    bounds = jnp.zeros((e, LANES), F32)
    for j in range(s // tok_block + 1):
        bounds = jnp.where(lane == j, count(sel & (tok < j * tok_block)), bounds)
    bounds_ref[0] = bounds.astype(jnp.int32)


def _topk(aff_t, cap, tok_block):
    b, e, s = aff_t.shape
    assert s // tok_block + 1 <= LANES
    tri = jnp.asarray(np.triu(np.ones((LANES, LANES))), dtype=BF16)
    return pl.pallas_call(
        functools.partial(_topk_body, cap=cap, tok_block=tok_block),
        grid=(b,),
        in_specs=[pl.BlockSpec((1, e, s), lambda i: (i, 0, 0)),
                  pl.BlockSpec((LANES, LANES), lambda i: (0, 0))],
        out_specs=[pl.BlockSpec((1, e, s), lambda i: (i, 0, 0)),
                   pl.BlockSpec((1, e, LANES), lambda i: (i, 0, 0))],
        out_shape=[jax.ShapeDtypeStruct((b, e, s), F32),
                   jax.ShapeDtypeStruct((b, e, LANES), jnp.int32)],
        compiler_params=_cparams(("arbitrary",)),
        name="topk",
    )(aff_t, tri)


def _moe_windows(bounds_ref, base, ne, step, tokens, window):
    per = tokens // MOE_BOUND_STEP
    starts = []
    passes = jnp.int32(0)
    for e in range(ne):
        r_lo = bounds_ref[base + e * LANES + step * per]
        r_hi = bounds_ref[base + e * LANES + (step + 1) * per]
        ws = (r_lo // BF16_ROWS) * BF16_ROWS
        starts.append(ws)
        passes = jnp.maximum(passes, (r_hi - ws + window - 1) // window)
    return starts, passes


def _gather_body(bounds_ref, rank_ref, hm_ref, affh_ref, affl_ref, xe_ref, gs_ref, *, cap, eg):
    w = GATHER_WINDOW
    ne, tc = rank_ref.shape[1], rank_ref.shape[2]
    bi = pl.program_id(0)
    ci = pl.program_id(1)

    @pl.when(ci == 0)
    def _():
        xe_ref[...] = jnp.zeros_like(xe_ref)
        gs_ref[...] = jnp.zeros_like(gs_ref)

    starts, passes = _moe_windows(bounds_ref, bi * ne * LANES, ne, ci, tc, w)
    iota_w = lax.broadcasted_iota(jnp.int32, (w, 1), 0)

    def one_pass(p, carry):
        for g0 in range(0, ne, eg):
            lo = [starts[e] + p * w for e in range(g0, g0 + eg)]
            phys = [pl.multiple_of(jnp.minimum(v, cap - w), BF16_ROWS) for v in lo]
            rk = jnp.concatenate([jnp.broadcast_to(rank_ref[0, e:e + 1, :], (w, tc))
                                  for e in range(g0, g0 + eg)], axis=0)
            slot = jnp.concatenate([iota_w + v for v in phys], axis=0).astype(F32)
            lom = jnp.concatenate([jnp.zeros_like(iota_w) + v for v in lo], axis=0).astype(F32)
            onehot = jnp.where((rk == slot) & (slot >= lom), 1.0, 0.0).astype(BF16)
            res = _dot(onehot, hm_ref[0])
            resg = _dot(onehot, affh_ref[0]) + _dot(onehot, affl_ref[0])
            for k, e in enumerate(range(g0, g0 + eg)):
                rows = pl.ds(phys[k], w)
                xe_ref[0, e, rows, :] = (xe_ref[0, e, rows, :].astype(F32) + res[k * w:(k + 1) * w]).astype(BF16)
                gs_ref[0, e, rows, :] = gs_ref[0, e, rows, :] + resg[k * w:(k + 1) * w]
        return carry

    lax.fori_loop(0, passes, one_pass, 0)


def _gather(bounds, rank, hm, aff_hi, aff_lo, cap, eg=16):
    b, e, s = rank.shape
    d = hm.shape[2]
    tc = GATHER_TOKENS
    return pl.pallas_call(
        functools.partial(_gather_body, cap=cap, eg=eg),
        grid_spec=pltpu.PrefetchScalarGridSpec(
            num_scalar_prefetch=1,
            grid=(b, s // tc),
            in_specs=[pl.BlockSpec((1, e, tc), lambda bi, ci, bnd: (bi, 0, ci)),
                      pl.BlockSpec((1, tc, d), lambda bi, ci, bnd: (bi, ci, 0)),
                      pl.BlockSpec((1, tc, LANES), lambda bi, ci, bnd: (bi, ci, 0)),
                      pl.BlockSpec((1, tc, LANES), lambda bi, ci, bnd: (bi, ci, 0))],
            out_specs=[pl.BlockSpec((1, e, cap, d), lambda bi, ci, bnd: (bi, 0, 0, 0)),
                       pl.BlockSpec((1, e, cap, LANES), lambda bi, ci, bnd: (bi, 0, 0, 0))],
        ),
        out_shape=[jax.ShapeDtypeStruct((b, e, cap, d), BF16),
                   jax.ShapeDtypeStruct((b, e, cap, LANES), F32)],
        compiler_params=_cparams(("arbitrary", "arbitrary")),
        name="gather",
    )(bounds.reshape(-1), rank, hm, aff_hi, aff_lo)


def _ffn_body(xe_ref, gs_ref, wg_ref, wu_ref, wd_ref, ye_ref, acc_ref):
    f = pl.program_id(1)
    nb = xe_ref.shape[0]
    wg = wg_ref[0].astype(BF16)
    wu = wu_ref[0].astype(BF16)
    wd = wd_ref[0].astype(BF16)
    for b in range(nb):
        xe = xe_ref[b, 0]
        a = _dot(xe, wg)
        u = _dot(xe, wu)
        hsw = (a * jax.nn.sigmoid(a) * u).astype(BF16)
        part = _dot(hsw, wd)

        @pl.when(f == 0)
        def _():
            acc_ref[b] = part

        @pl.when(f != 0)
        def _():
            acc_ref[b] += part

    @pl.when(f == pl.num_programs(1) - 1)
    def _():
        lane = lax.broadcasted_iota(jnp.int32, gs_ref.shape[2:], 1)
        mine = lane == pl.program_id(0)
        for b in range(nb):
            gate = jnp.sum(jnp.where(mine, gs_ref[b, 0], 0.0), axis=-1, keepdims=True)
            ye_ref[b, 0] = (acc_ref[b] * gate).astype(BF16)


def _ffn(xe, gs, wg, wu, wd, ft=1024):
    b, e, cap, d = xe.shape
    ff = wg.shape[2]
    one = pl.Buffered(1)
    return pl.pallas_call(
        _ffn_body,
        grid=(e, ff // ft),
        in_specs=[pl.BlockSpec((b, 1, cap, d), lambda ei, fi: (0, ei, 0, 0), pipeline_mode=one),
                  pl.BlockSpec((b, 1, cap, LANES), lambda ei, fi: (0, ei, 0, 0), pipeline_mode=one),
                  pl.BlockSpec((1, d, ft), lambda ei, fi: (ei, 0, fi)),
                  pl.BlockSpec((1, d, ft), lambda ei, fi: (ei, 0, fi)),
                  pl.BlockSpec((1, ft, d), lambda ei, fi: (ei, fi, 0))],
        out_specs=pl.BlockSpec((b, 1, cap, d), lambda ei, fi: (0, ei, 0, 0)),
        out_shape=jax.ShapeDtypeStruct((b, e, cap, d), BF16),
        scratch_shapes=[pltpu.VMEM((b, cap, d), F32)],
        compiler_params=_cparams(("arbitrary", "arbitrary")),
        name="ffn",
    )(xe, gs, wg, wu, wd)


def _scatter_body(bounds_ref, rank_ref, ye_ref, x2_ref, g_ref, out_ref, acc_ref, *, eg):
    w = SCATTER_WINDOW
    ts = x2_ref.shape[1]
    ne, cap = ye_ref.shape[1], ye_ref.shape[2]
    starts, passes = _moe_windows(bounds_ref, pl.program_id(0) * ne * LANES, ne, pl.program_id(1), ts, w)
    rank = rank_ref[0]
    acc_ref[...] = x2_ref[0]
    iota_w = lax.broadcasted_iota(jnp.int32, (1, w), 1)

    def one_pass(p, carry):
        for g0 in range(0, ne, eg):
            onehots, rows = [], []
            for e in range(g0, g0 + eg):
                lo = starts[e] + p * w
                phys = pl.multiple_of(jnp.minimum(lo, cap - w), BF16_ROWS)
                slot = (iota_w + phys).astype(F32)
                hit = (rank[:, e:e + 1] == slot) & (slot >= lo.astype(F32))
                onehots.append(jnp.where(hit, 1.0, 0.0).astype(BF16))
                rows.append(ye_ref[0, e, pl.ds(phys, w), :])
            acc_ref[...] += _dot(jnp.concatenate(onehots, axis=1), jnp.concatenate(rows, axis=0))
        return carry

    lax.fori_loop(0, passes, one_pass, 0)
    out_ref[0] = _rms(acc_ref[...], g_ref[...])


def _scatter(bounds, rank_t, ye, x2, g, eg=8):
    b, s, e = rank_t.shape
    cap, d = ye.shape[2], ye.shape[3]
    ts = SCATTER_TOKENS
    return pl.pallas_call(
        functools.partial(_scatter_body, eg=eg),
        grid_spec=pltpu.PrefetchScalarGridSpec(
            num_scalar_prefetch=1,
            grid=(b, s // ts),
            in_specs=[pl.BlockSpec((1, ts, e), lambda bi, ti, bnd: (bi, ti, 0)),
                      pl.BlockSpec((1, e, cap, d), lambda bi, ti, bnd: (bi, 0, 0, 0)),
                      pl.BlockSpec((1, ts, d), lambda bi, ti, bnd: (bi, ti, 0)),
                      pl.BlockSpec((1, d), lambda bi, ti, bnd: (0, 0))],
            out_specs=pl.BlockSpec((1, ts, d), lambda bi, ti, bnd: (bi, ti, 0)),
            scratch_shapes=[pltpu.VMEM((ts, d), F32)],
        ),
        out_shape=jax.ShapeDtypeStruct((b, s, d), F32),
        compiler_params=_cparams(("arbitrary", "arbitrary")),
        name="scatter",
    )(bounds.reshape(-1), rank_t, ye, x2, g)


def kernel(x, mem, ln_mix_g, w_in, hy_conv_w, hy_conv_b, filt_w1, filt_b1, filt_freq1, filt_w2, filt_b2,
           filt_freq2, filt_w3, hy_skip, w_up_hy, w_up_attn, w_gate, b_gate, w_out, ln_x_g, ln_mem_g,
           w_q_x, w_kv_mem, w_o_x, ln_moe_g, w_router, w_e_gate, w_e_up, w_e_down, ln_f_g):
    b, s, d = x.shape
    n = b * s
    x2d = x.reshape(n, d)

    half = HEAD_DIM // 2
    inv = ROPE_THETA ** (-jnp.arange(0, HEAD_DIM, 2, dtype=F32) / HEAD_DIM)
    ang = jnp.arange(s, dtype=F32)[:, None] * inv[None, :]
    lane = np.arange(LANES)
    cos_t = jnp.cos(ang)[:, lane % half]
    sin_t = jnp.sin(ang)[:, lane % half] * jnp.asarray(np.where(lane % HEAD_DIM < half, -1.0, 1.0), F32)[None, :]

    w_all = jnp.concatenate([w_in, w_gate], axis=1).astype(BF16)
    p_hy, *qkv, gates = _proj(x2d, ln_mix_g[None], w_all, b_gate[None], cos_t, sin_t, b, s)

    t_col = jnp.linspace(0.0, 1.0, s, dtype=F32)[:, None]
    grid_col = 2.0 * math.pi * jnp.arange(s, dtype=F32)[:, None] / s
    bands = jnp.linspace(1e-4, HY_BANDS - 1, HY_BANDS, dtype=F32)[None, :]
    feat = _filt(t_col, grid_col, bands, filt_w1, filt_b1, filt_freq1, filt_w2, filt_b2, filt_freq2)
    delta = jnp.linspace(math.log(HY_TARGET) / HY_SLOW_PCT, math.log(HY_TARGET) / HY_FAST_PCT,
                         HY_WIDTH, dtype=F32)[None, :]
    tables = _fft_tables(s)
    khat = _hyfilt(t_col, delta, feat, filt_w3, tables[1], tables[3])
    z = _hyena(p_hy, hy_conv_w, hy_conv_b[None], hy_skip, khat, tables)

    outs, lses = [], []
    for g, (window, dil) in enumerate(DIL_PAIRS):
        o_g, l_g = _attn_group(*qkv[3 * g:3 * g + 3], g, window // (2 * dil))
        outs.append(o_g)
        lses.append(l_g)

    kv = _memkv(mem, ln_mem_g[None], w_kv_mem.astype(BF16))
    wr_pad = jnp.pad(w_router, ((0, 0), (0, LANES - N_EXPERTS)))
    wr_cat = jnp.concatenate(_split(wr_pad), axis=1)
    x2, hm, aff = _merge(x2d, z.reshape(n, HY_WIDTH), outs, lses, gates,
                         w_up_hy.astype(BF16), w_up_attn.astype(BF16), w_out.astype(BF16),
                         ln_x_g[None], w_q_x.astype(BF16), kv, w_o_x.astype(BF16), ln_moe_g[None],
                         wr_cat, s)

    cap = max(1, EC_FACTOR * s // N_EXPERTS)
    aff3 = aff.reshape(b, s, LANES)
    aff_hi = aff3.astype(BF16)
    aff_lo = (aff3 - aff_hi.astype(F32)).astype(BF16)
    rank, bounds = _topk(aff3[:, :, :N_EXPERTS].transpose(0, 2, 1), cap, MOE_BOUND_STEP)
    xe, gs = _gather(bounds, rank, hm.reshape(b, s, d), aff_hi, aff_lo, cap)
    ye = _ffn(xe, gs, w_e_gate, w_e_up, w_e_down)
    return _scatter(bounds, rank.transpose(0, 2, 1), ye, x2.reshape(b, s, d), ln_f_g[None])
```

```python
import functools
import math

import numpy as np
import jax
import jax.numpy as jnp
from jax import lax
from jax.experimental import pallas as pl
from jax.experimental.pallas import tpu as pltpu

F32 = jnp.float32
BF16 = jnp.bfloat16
HIGHEST = lax.Precision.HIGHEST

EPS = 1e-6
HY_WIDTH = 768
HY_BANDS = 16
HY_FFN = 64
HY_FAST_PCT = 0.3
HY_SLOW_PCT = 1.5
HY_TARGET = 1e-2
HEAD_DIM = 64
HEADS_PER_GROUP = 4
DIL_PAIRS = ((128, 1), (512, 4), (2048, 16))
N_GROUPS = len(DIL_PAIRS)
ATTN_WIDTH = N_GROUPS * HEADS_PER_GROUP * HEAD_DIM
ATTN_OUT = HEADS_PER_GROUP * HEAD_DIM
ROPE_THETA = 10000.0
X_HEADS = 4
X_HEAD_DIM = 128
N_EXPERTS = 16
EC_FACTOR = 2

LANES = 128
SUBLANES = 8
MXU_COLS = 256
VMEM_LIMIT = 56 * 1024 * 1024

FFT_N2 = 128
HY_CB = 128
FFT_UNROLL = 8
FFT_PAD = 8
MOE_BOUND_STEP = 256
GATHER_TOKENS, GATHER_WINDOW = 256, 64
SCATTER_TOKENS, SCATTER_WINDOW = 512, 128
BF16_ROWS = 16


def _cparams(sem, vmem=VMEM_LIMIT):
    return pltpu.CompilerParams(dimension_semantics=sem, vmem_limit_bytes=vmem)


def _rms(x, g):
    return x * lax.rsqrt(jnp.mean(x * x, axis=-1, keepdims=True) + EPS) * g


def _dot(a, b):
    return jnp.dot(a, b, preferred_element_type=F32)


def _dot_hi(a, b):
    return jnp.dot(a, b, precision=HIGHEST, preferred_element_type=F32)


def _split(a):
    hi = a.astype(BF16)
    return hi, (a - hi.astype(F32)).astype(BF16)


def _dot_split(a, b):
    a_hi, a_lo = _split(a)
    b_hi, b_lo = _split(b)
    return _dot(a_hi, b_hi) + _dot(a_lo, b_hi) + _dot(a_hi, b_lo)


def _dot_t(a, b):
    return lax.dot_general(a, b, (((1,), (1,)), ((), ())), preferred_element_type=F32)


def _proj_body(x_ref, g_ref, w_ref, bg_ref, cos_ref, sin_ref, *rest, hyc, aw):
    ng = N_GROUPS
    perm_refs = (None,) + rest[:ng - 1]
    phy_ref = rest[ng - 1]
    qkv_refs = rest[ng:ng + 3 * ng]
    gate_ref = rest[ng + 3 * ng]
    h = _rms(x_ref[...], g_ref[...]).astype(BF16)
    phy = _dot(h, w_ref[:, :hyc])
    for j in range(hyc // HY_CB):
        phy_ref[0, j] = phy[:, j * HY_CB:(j + 1) * HY_CB]
    cos = cos_ref[...]
    sin = sin_ref[...]
    tm = cos.shape[0]
    lane = lax.broadcasted_iota(jnp.int32, (tm, LANES), 1)
    first = (lane % HEAD_DIM) < (HEAD_DIM // 2)

    def rope(t, scale):
        chunks = []
        for j in range(aw // LANES):
            tj = t[:, j * LANES:(j + 1) * LANES]
            partner = jnp.where(first, pltpu.roll(tj, LANES - HEAD_DIM // 2, 1),
                                pltpu.roll(tj, HEAD_DIM // 2, 1))
            chunks.append(((tj * cos + partner * sin) * scale).astype(BF16))
        return chunks

    gl = ATTN_OUT // LANES
    q = rope(_dot(h, w_ref[:, hyc:hyc + aw]), HEAD_DIM ** -0.5)
    k = rope(_dot(h, w_ref[:, hyc + aw:hyc + 2 * aw]), 1.0)
    vf = _dot(h, w_ref[:, hyc + 2 * aw:hyc + 3 * aw]).astype(BF16)
    v = [vf[:, j * LANES:(j + 1) * LANES] for j in range(aw // LANES)]
    for g in range(ng):
        for i, t in enumerate((q, k, v)):
            out_ref = qkv_refs[3 * g + i]
            tg = jnp.concatenate(t[g * gl:(g + 1) * gl], axis=1)
            if perm_refs[g] is not None:
                tg = _dot(perm_refs[g][...], tg).astype(BF16)
            dil = out_ref.shape[1]
            rows = tm // dil
            for r in range(dil):
                out_ref[0, r] = tg[r * rows:(r + 1) * rows, :]
    gate_ref[...] = jax.nn.sigmoid(_dot(h, w_ref[:, hyc + 3 * aw:]) + bg_ref[...]).astype(BF16)


def _proj(x2d, g, w_all, b_gate, cos_t, sin_t, batch, seq, tm=256):
    n, d = x2d.shape
    hyc = 3 * HY_WIDTH
    aw = ATTN_WIDTH
    gd = w_all.shape[1] - hyc - 3 * aw
    nseq = seq // tm
    row = lambda i: (i, 0)
    const = lambda i: (0, 0)
    perms = []
    for _, dil in DIL_PAIRS[1:]:
        rows = tm // dil
        src = (np.arange(tm) % rows) * dil + np.arange(tm) // rows
        perms.append(jnp.asarray(np.eye(tm)[src], dtype=BF16))
    qkv_specs, qkv_shapes = [], []
    for _, dil in DIL_PAIRS:
        for _ in range(3):
            qkv_specs.append(pl.BlockSpec((1, dil, tm // dil, ATTN_OUT), lambda i: (i // nseq, 0, i % nseq, 0)))
            qkv_shapes.append(jax.ShapeDtypeStruct((batch, dil, seq // dil, ATTN_OUT), BF16))
    return pl.pallas_call(
        functools.partial(_proj_body, hyc=hyc, aw=aw),
        grid=(n // tm,),
        in_specs=[
            pl.BlockSpec((tm, d), row),
            pl.BlockSpec((1, d), const),
            pl.BlockSpec(w_all.shape, const, pipeline_mode=pl.Buffered(1)),
            pl.BlockSpec((1, gd), const),
            pl.BlockSpec((tm, LANES), lambda i: (i % nseq, 0)),
            pl.BlockSpec((tm, LANES), lambda i: (i % nseq, 0)),
        ] + [pl.BlockSpec((tm, tm), const) for _ in perms],
        out_specs=([pl.BlockSpec((1, hyc // HY_CB, tm, HY_CB), lambda i: (i // nseq, 0, i % nseq, 0))]
                   + qkv_specs + [pl.BlockSpec((tm, gd), row)]),
        out_shape=([jax.ShapeDtypeStruct((batch, hyc // HY_CB, seq, HY_CB), F32)] + qkv_shapes
                   + [jax.ShapeDtypeStruct((n, gd), BF16)]),
        compiler_params=_cparams(("arbitrary",)),
        name="proj",
    )(x2d, g, w_all, b_gate, cos_t, sin_t, *perms)


def _filt_body(t_ref, grid_ref, bands_ref, w1t_ref, w1c_ref, w1s_ref, b1_ref, f1_ref,
               w2_ref, b2_ref, f2_ref, feat_ref):
    ang = bands_ref[...] * grid_ref[...]
    pre = (t_ref[...] * w1t_ref[...] + _dot_hi(jnp.cos(ang), w1c_ref[...])
           + _dot_hi(-jnp.sin(ang), w1s_ref[...]))
    h = jnp.sin(f1_ref[...] * (pre + b1_ref[...]))
    feat_ref[...] = jnp.sin(f2_ref[...] * (_dot_hi(h, w2_ref[...]) + b2_ref[...]))


def _filt(t_col, grid_col, bands, w1, b1, f1, w2, b2, f2):
    length = t_col.shape[0]
    nb = bands.shape[1]
    args = (t_col, grid_col, bands, w1[0:1], w1[1:1 + nb], w1[1 + nb:], b1[None], f1[None],
            w2, b2[None], f2[None])
    return pl.pallas_call(
        _filt_body,
        out_shape=jax.ShapeDtypeStruct((length, HY_FFN), F32),
        compiler_params=_cparams(None),
        name="filt",
    )(*args)


def _fft_tables(length):
    n = 2 * length
    n2 = FFT_N2
    n1 = n // n2
    h1 = n1 // 2
    k1 = np.arange(n1)[:, None]
    g_sig = np.zeros((n2, 2 * n1, 2 * h1))
    g_flt = np.zeros((n2, 2 * n1, 2 * h1))
    g_inv = np.zeros((n2, 2 * h1, 2 * n1))
    j = np.arange(h1)[None, :]
    for r in range(n2):
        th = 2 * np.pi * (k1 * (n2 * j + r) % n) / n
        gr, gi = np.cos(th), -np.sin(th)
        g_sig[r, 0::2, :h1] = gr
        g_sig[r, 0::2, h1:] = -gi
        g_sig[r, 1::2, :h1] = gi
        g_sig[r, 1::2, h1:] = gr
        g_flt[r, 0::2, :h1] = gr
        g_flt[r, 1::2, :h1] = gi
        m = n2 * (j + 1) - r
        thb = 2 * np.pi * (k1 * m % n) / n
        live = (m < length).astype(np.float64)
        g_flt[r, 0::2, h1:] = np.cos(thb) * live
        g_flt[r, 1::2, h1:] = np.sin(thb) * live
        wr, wi = (np.cos(th) / n).T, (np.sin(th) / n).T
        g_inv[r, :h1, 0::2] = wr
        g_inv[r, :h1, 1::2] = -wi
        g_inv[r, h1:, 0::2] = wi
        g_inv[r, h1:, 1::2] = wr
    a = np.arange(n2)
    th2 = 2 * np.pi * (np.outer(a, a) % n2) / n2
    c2, s2 = np.cos(th2), np.sin(th2)
    f_fwd = np.block([[c2, s2], [-s2, c2]])
    f_inv = np.block([[c2, -s2], [s2, c2]])
    cast = lambda z: jnp.asarray(z, dtype=F32).astype(BF16)
    return cast(g_sig), cast(g_flt), cast(g_inv), cast(f_fwd), cast(f_inv)


def _fft_stage1(src_a_ref, src_b_ref, tab_ref, work_ref, n1, is_filter):
    n2 = FFT_N2
    grp = SUBLANES

    def body(i, c):
        r0 = pl.multiple_of(i * grp, grp)
        b0 = pl.multiple_of(n2 - grp - r0, grp) if is_filter else r0
        a = jnp.swapaxes(src_a_ref[:, pl.ds(r0, grp), :], 0, 1)
        b = jnp.swapaxes(src_b_ref[:, pl.ds(b0, grp), :], 0, 1)
        outs = []
        for s in range(grp):
            rhs = jnp.concatenate([a[s], b[grp - 1 - s] if is_filter else b[s]], axis=0).astype(BF16)
            outs.append(_dot(tab_ref[r0 + s], rhs))
        work_ref[:, pl.ds(r0, grp), :] = jnp.swapaxes(jnp.stack(outs, axis=0), 0, 1)
        return c

    lax.fori_loop(0, n2 // grp, body, 0, unroll=2)


def _hyfilt_body(t_ref, ts_ref, delta_ref, feat_ref, feats_ref, w3f_ref, w3b_ref, gflt_ref, ffwd_ref, khat_ref,
                 hf_ref, hb_ref, work_ref, *, length, rc):
    n2 = FFT_N2
    n1 = 2 * length // n2
    tiles = rc // n2
    cb = hf_ref.shape[2]

    def gen(i, c):
        r0 = pl.multiple_of(i * rc, rc)
        j0 = pl.multiple_of(i * tiles, tiles)
        delta = jnp.abs(delta_ref[...])
        hf = _dot_split(feat_ref[pl.ds(r0, rc), :], w3f_ref[...]) * jnp.exp(-t_ref[pl.ds(r0, rc), :] * delta)
        hb = _dot_split(feats_ref[pl.ds(r0, rc), :], w3b_ref[...]) * jnp.exp(-ts_ref[pl.ds(r0, rc), :] * delta)
        hf_ref[pl.ds(j0, tiles), :n2, :] = hf.reshape(tiles, n2, cb)
        hb_ref[pl.ds(j0, tiles), :n2, :] = hb.reshape(tiles, n2, cb)
        return c

    lax.fori_loop(0, length // rc, gen, 0)
    _fft_stage1(hf_ref, hb_ref, gflt_ref, work_ref, n1, True)

    side = MXU_COLS // cb

    def stage2(i, c):
        k1s = [i * side + v for v in range(side)]
        blks = [work_ref[pl.ds(pl.multiple_of(2 * k1, 2), 2), :n2, :].reshape(2 * n2, cb).astype(BF16)
                for k1 in k1s]
        x = _dot(ffwd_ref[...], jnp.concatenate(blks, axis=1)).astype(BF16)
        for v, k1 in enumerate(k1s):
            khat_ref[0, 0, pl.ds(pl.multiple_of(k1 * (2 * n2), 2 * n2), 2 * n2), :] = x[:, v * cb:(v + 1) * cb]
        return c

    lax.fori_loop(0, n1 // side, stage2, 0, unroll=FFT_UNROLL // side)


def _hyfilt(t_col, delta, feat, w3, g_flt, f_fwd, rc=512):
    s = t_col.shape[0]
    cb = HY_CB
    ncb = HY_WIDTH // cb
    n2 = FFT_N2
    n1 = 2 * s // n2
    norder = w3.shape[1] // (2 * HY_WIDTH)
    one = pl.Buffered(1)

    def full(a):
        nd = a.ndim
        return pl.BlockSpec(a.shape, lambda o, c: (0,) * nd, pipeline_mode=one)

    shift = lambda a: jnp.concatenate([a[1:], jnp.zeros_like(a[:1])], axis=0)
    return pl.pallas_call(
        functools.partial(_hyfilt_body, length=s, rc=rc),
        grid=(norder, ncb),
        in_specs=[full(t_col), full(t_col),
                  pl.BlockSpec((1, cb), lambda o, c: (0, c)),
                  full(feat), full(feat),
                  pl.BlockSpec((HY_FFN, cb), lambda o, c: (0, 2 * ncb * o + c)),
                  pl.BlockSpec((HY_FFN, cb), lambda o, c: (0, 2 * ncb * o + ncb + c)),
                  full(g_flt), full(f_fwd)],
        out_specs=pl.BlockSpec((1, 1, 2 * n1 * n2, cb), lambda o, c: (o, c, 0, 0)),
        out_shape=jax.ShapeDtypeStruct((norder, ncb, 2 * n1 * n2, cb), BF16),
        scratch_shapes=[pltpu.VMEM((n1 // 2, n2 + FFT_PAD, cb), F32)] * 2
        + [pltpu.VMEM((2 * n1, n2 + FFT_PAD, cb), F32)],
        compiler_params=_cparams(("arbitrary", "arbitrary")),
        name="hyfilt",
    )(t_col, shift(t_col), delta, feat, shift(feat), w3, w3, g_flt, f_fwd)


def _hyena_body(pz_ref, pg1_ref, pg2_ref, cwz_ref, cwg1_ref, cwg2_ref, cbz_ref, cbg1_ref, cbg2_ref,
                skip_ref, khat_ref, gsig_ref, ginv_ref, ffwd_ref, finv_ref,
                out_ref,
                za_ref, zb_ref, ya_ref, yb_ref, work_ref, *, length, rc):
    n2 = FFT_N2
    n1 = 2 * length // n2
    h1 = n1 // 2
    tiles = rc // n2
    cb = za_ref.shape[2]
    side = MXU_COLS // cb
    sub = SUBLANES

    def conv3(p_ref, b, r0, w_ref, bias_ref):
        p = p_ref[b, 0, pl.ds(r0, rc), :]
        before = p_ref[b, 0, pl.ds(pl.multiple_of(jnp.maximum(r0 - sub, 0), sub), sub), :][sub - 1:sub, :]
        after = p_ref[b, 0, pl.ds(pl.multiple_of(jnp.minimum(r0 + rc, length - sub), sub), sub), :][0:1, :]
        before = jnp.where(r0 == 0, 0.0, before)
        after = jnp.where(r0 + rc == length, 0.0, after)
        row = lax.broadcasted_iota(jnp.int32, p.shape, 0)
        prev = jnp.where(row == 0, before, pltpu.roll(p, 1, 0))
        nxt = jnp.where(row == rc - 1, after, pltpu.roll(p, rc - 1, 0))
        return prev * w_ref[0:1, :] + p * w_ref[1:2, :] + nxt * w_ref[2:3, :] + bias_ref[...]

    def stage2(o):
        def body(i, c):
            k1s = [i * FFT_UNROLL + u for u in range(FFT_UNROLL)]
            blks = [work_ref[pl.ds(pl.multiple_of(2 * k1, 2), 2), :n2, :].reshape(2 * n2, cb).astype(BF16)
                    for k1 in k1s]
            outs = []
            for u in range(0, FFT_UNROLL, side):
                x = _dot(ffwd_ref[...], jnp.concatenate(blks[u:u + side], axis=1))
                ys = []
                for v in range(side):
                    k1 = k1s[u + v]
                    kh = khat_ref[o, 0, pl.ds(pl.multiple_of(k1 * (2 * n2), 2 * n2), 2 * n2), :].astype(F32)
                    xr, xi = x[:n2, v * cb:(v + 1) * cb], x[n2:, v * cb:(v + 1) * cb]
                    kr, ki = kh[:n2], kh[n2:]
                    ys.append(jnp.concatenate([xr * kr - xi * ki, xr * ki + xi * kr], axis=0).astype(BF16))
                out = _dot(finv_ref[...], jnp.concatenate(ys, axis=1))
                outs.extend(out[:, v * cb:(v + 1) * cb] for v in range(side))
            for k1, out in zip(k1s, outs):
                work_ref[pl.ds(pl.multiple_of(2 * k1, 2), 2), :n2, :] = out.reshape(2, n2, cb)
            return c
        lax.fori_loop(0, n1 // FFT_UNROLL, body, 0)

    def stage3():
        grp = SUBLANES

        def body(i, c):
            r0 = pl.multiple_of(i * grp, grp)
            wk = jnp.swapaxes(work_ref[:, pl.ds(r0, grp), :], 0, 1)
            res = jnp.stack([_dot(ginv_ref[r0 + s], wk[s].astype(BF16)) for s in range(grp)], axis=0)
            ya_ref[:, pl.ds(r0, grp), :] = jnp.swapaxes(res[:, :h1], 0, 1)
            yb_ref[:, pl.ds(r0, grp), :] = jnp.swapaxes(res[:, h1:], 0, 1)
            return c
        lax.fori_loop(0, n2 // grp, body, 0, unroll=2)

    def rows3(ref, i):
        return ref.at[pl.ds(pl.multiple_of(i * tiles, tiles), tiles), :n2, :]

    def load_z(i, c):
        r0 = pl.multiple_of(i * rc, rc)
        rows3(za_ref, i)[...] = conv3(pz_ref, 0, r0, cwz_ref, cbz_ref).reshape(tiles, n2, cb)
        rows3(zb_ref, i)[...] = conv3(pz_ref, 1, r0, cwz_ref, cbz_ref).reshape(tiles, n2, cb)
        return c

    lax.fori_loop(0, length // rc, load_z, 0)
    for o, (pg_ref, cw_ref, cb_ref) in enumerate(((pg1_ref, cwg1_ref, cbg1_ref), (pg2_ref, cwg2_ref, cbg2_ref))):
        _fft_stage1(za_ref, zb_ref, gsig_ref, work_ref, n1, False)
        stage2(o)
        stage3()
        skip = skip_ref[o:o + 1, :]

        def gate(i, c):
            r0 = pl.multiple_of(i * rc, rc)
            rows = pl.ds(r0, rc)
            ya, yb = (rows3(r, i)[...].reshape(rc, cb) for r in (ya_ref, yb_ref))
            za, zb = (rows3(r, i)[...].reshape(rc, cb) for r in (za_ref, zb_ref))
            new_a = conv3(pg_ref, 0, r0, cw_ref, cb_ref) * (ya + skip * za)
            new_b = conv3(pg_ref, 1, r0, cw_ref, cb_ref) * (yb + skip * zb)
            if o == 0:
                rows3(za_ref, i)[...] = new_a.reshape(tiles, n2, cb)
                rows3(zb_ref, i)[...] = new_b.reshape(tiles, n2, cb)
            else:
                out_ref[0, rows, :] = new_a
                out_ref[1, rows, :] = new_b
            return c

        lax.fori_loop(0, length // rc, gate, 0)


def _hyena(p_hy, conv_w, conv_b, skip, khat, tables, rc=512):
    b, _, s, _ = p_hy.shape
    w = HY_WIDTH
    cb = HY_CB
    ncb = w // cb
    n2 = FFT_N2
    n1 = 2 * s // n2
    g_sig, _, g_inv, f_fwd, f_inv = tables
    one = pl.Buffered(1)

    def pspec(off):
        return pl.BlockSpec((2, 1, s, cb), lambda c, p: (p, off + c, 0, 0), pipeline_mode=one)

    def cwspec(off):
        return pl.BlockSpec((3, cb), lambda c, p: (0, off + c))

    def cbspec(off):
        return pl.BlockSpec((1, cb), lambda c, p: (0, off + c))

    def full(a):
        nd = a.ndim
        return pl.BlockSpec(a.shape, lambda c, p: (0,) * nd, pipeline_mode=one)

    in_specs = [
        pspec(0), pspec(ncb), pspec(2 * ncb),
        cwspec(0), cwspec(ncb), cwspec(2 * ncb),
        cbspec(0), cbspec(ncb), cbspec(2 * ncb),
        pl.BlockSpec((2, cb), lambda c, p: (0, c)),
        pl.BlockSpec((khat.shape[0], 1, 2 * n1 * n2, cb), lambda c, p: (0, c, 0, 0), pipeline_mode=one),
        full(g_sig), full(g_inv), full(f_fwd), full(f_inv),
    ]
    return pl.pallas_call(
        functools.partial(_hyena_body, length=s, rc=rc),
        grid=(ncb, b // 2),
        in_specs=in_specs,
        out_specs=pl.BlockSpec((2, s, cb), lambda c, p: (p, 0, c)),
        out_shape=jax.ShapeDtypeStruct((b, s, w), F32),
        scratch_shapes=[pltpu.VMEM((n1 // 2, n2 + FFT_PAD, cb), F32)] * 4
        + [pltpu.VMEM((2 * n1, n2 + FFT_PAD, cb), F32)],
        compiler_params=_cparams(("arbitrary", "arbitrary")),
        name="hyena",
    )(p_hy, p_hy, p_hy, conv_w, conv_w, conv_w, conv_b, conv_b, conv_b,
      skip, khat, g_sig, g_inv, f_fwd, f_inv)


def _attn_body(q_ref, k_ref, v_ref, o_ref, lse_ref, *, n_side, qb):
    lr = q_ref.shape[2]
    kw = qb + 2 * n_side
    nh = HEADS_PER_GROUP
    lane_head = lax.broadcasted_iota(jnp.int32, (qb, ATTN_OUT), 1) // HEAD_DIM
    qi = lax.broadcasted_iota(jnp.int32, (nh * qb, kw), 0) % qb
    ki = lax.broadcasted_iota(jnp.int32, (nh * qb, kw), 1)
    band = [jnp.where(jnp.abs(ki - qi - shift) <= n_side, 0.0, -1e30) for shift in (0, n_side, 2 * n_side)]

    def body(i, c):
        q0 = pl.multiple_of(i * qb, qb)
        w0 = pl.multiple_of(jnp.clip(q0 - n_side, 0, lr - kw), n_side)
        q = q_ref[0, 0, pl.ds(q0, qb), :]
        kwin = k_ref[0, 0, pl.ds(w0, kw), :]
        vwin = v_ref[0, 0, pl.ds(w0, kw), :]
        shift = q0 - w0
        bias = jnp.where(shift == n_side, band[1], jnp.where(shift == 0, band[0], band[2]))
        qs = jnp.concatenate([jnp.where(lane_head == h, q, jnp.zeros_like(q)) for h in range(nh)], axis=0)
        s = _dot_t(qs, kwin) + bias
        m = jnp.max(s, axis=-1, keepdims=True)
        p = jnp.exp(s - m)
        l = jnp.sum(p, axis=-1, keepdims=True)
        pv = _dot(p.astype(BF16), vwin) / l
        lse_rows = m + jnp.log(l)
        o = jnp.zeros((qb, ATTN_OUT), F32)
        lse = jnp.zeros((qb, ATTN_OUT), F32)
        for h in range(nh):
            mine = lane_head == h
            o = jnp.where(mine, pv[h * qb:(h + 1) * qb], o)
            lse = jnp.where(mine, lse_rows[h * qb:(h + 1) * qb], lse)
        o_ref[0, pl.ds(q0, qb), :] = o
        lse_ref[0, pl.ds(q0, qb), :] = lse
        return c

    lax.fori_loop(0, lr // qb, body, 0, unroll=2)


def _attn_group(q, k, v, g, n_side, qb=128):
    b, dil, lr, _ = q.shape
    in_spec = pl.BlockSpec((1, 1, lr, ATTN_OUT), lambda bi, r: (bi, r, 0, 0))
    out_spec = pl.BlockSpec((1, lr, ATTN_OUT), lambda bi, r: (bi, 0, r))
    o, lse = pl.pallas_call(
        functools.partial(_attn_body, n_side=n_side, qb=qb),
        grid=(b, dil),
        in_specs=[in_spec, in_spec, in_spec],
        out_specs=[out_spec, out_spec],
        out_shape=[jax.ShapeDtypeStruct((b, lr, dil * ATTN_OUT), F32)] * 2,
        compiler_params=_cparams(("arbitrary", "arbitrary")),
        name=f"attn{g}",
    )(q, k, v)
    return o, lse


def _memkv_body(mem_ref, g_ref, w_ref, kv_ref):
    mn = _rms(mem_ref[0], g_ref[...]).astype(BF16)
    kv_ref[0] = _dot(mn, w_ref[...]).astype(BF16)


def _memkv(mem, g, w_kv):
    b, m, d = mem.shape
    n = w_kv.shape[1]
    return pl.pallas_call(
        _memkv_body,
        grid=(b,),
        in_specs=[pl.BlockSpec((1, m, d), lambda i: (i, 0, 0)),
                  pl.BlockSpec((1, d), lambda i: (0, 0)),
                  pl.BlockSpec((d, n), lambda i: (0, 0))],
        out_specs=pl.BlockSpec((1, m, n), lambda i: (i, 0, 0)),
        out_shape=jax.ShapeDtypeStruct((b, m, n), BF16),
        compiler_params=_cparams(("arbitrary",)),
        name="memkv",
    )(mem, g, w_kv)


def _merge_body(x_ref, z_ref, o0_ref, o1_ref, o2_ref, l0_ref, l1_ref, l2_ref, gate_ref,
                wuh_ref, wua_ref, wout_ref, gx_ref, wq_ref, kv_ref, wo_ref, gm_ref, wr_ref,
                p1_ref, p2_ref, x2_ref, hm_ref, aff_ref, *, nsplit):
    rows = x_ref.shape[0] // nsplit
    for part in range(nsplit):
        sl = slice(part * rows, (part + 1) * rows)

        def natural(ref, perm_ref):
            dil = ref.shape[2] // ATTN_OUT
            rr = rows // dil
            blk = ref[0, part * rr:(part + 1) * rr, :]
            if dil == 1:
                return blk
            pieces = [blk[:, r * ATTN_OUT:(r + 1) * ATTN_OUT] for r in range(dil)]
            if rr % SUBLANES == 0 and dil % SUBLANES == 0:
                return jnp.swapaxes(jnp.stack(pieces, axis=0), 0, 1).reshape(rows, ATTN_OUT)
            hi, lo = _split(jnp.concatenate(pieces, axis=0))
            return _dot(perm_ref[...], hi) + _dot(perm_ref[...], lo)

        outs = [natural(r, p) for r, p in ((o0_ref, None), (o1_ref, p1_ref), (o2_ref, p2_ref))]
        lses = [natural(r, p) for r, p in ((l0_ref, None), (l1_ref, p1_ref), (l2_ref, p2_ref))]
        _merge_rows(sl, outs, lses, x_ref, z_ref, gate_ref,
                    wuh_ref, wua_ref, wout_ref, gx_ref, wq_ref, kv_ref, wo_ref, gm_ref, wr_ref,
                    x2_ref, hm_ref, aff_ref)


def _merge_rows(sl, outs, lses, x_ref, z_ref, gate_ref,
                wuh_ref, wua_ref, wout_ref, gx_ref, wq_ref, kv_ref, wo_ref, gm_ref, wr_ref,
                x2_ref, hm_ref, aff_ref):
    d = x_ref.shape[1]
    l0, l1, l2 = lses
    mx = jnp.maximum(jnp.maximum(l0, l1), l2)
    e0, e1, e2 = jnp.exp(l0 - mx), jnp.exp(l1 - mx), jnp.exp(l2 - mx)
    attn = (e0 * outs[0] + e1 * outs[1] + e2 * outs[2]) / (e0 + e1 + e2)
    y_hy = _dot(z_ref[sl, :].astype(BF16), wuh_ref[...])
    y_at = _dot(attn.astype(BF16), wua_ref[...])
    gates = gate_ref[sl, :].astype(F32)
    mix = gates[:, :d] * y_hy + gates[:, d:] * y_at
    x1 = x_ref[sl, :] + _dot(mix.astype(BF16), wout_ref[...])

    hx = _rms(x1, gx_ref[...]).astype(BF16)
    qx = _dot(hx, wq_ref[...]).astype(BF16)
    kv = kv_ref[0]
    xw = X_HEADS * X_HEAD_DIM
    heads = []
    for h in range(X_HEADS):
        hc = slice(h * X_HEAD_DIM, (h + 1) * X_HEAD_DIM)
        s = _dot_t(qx[:, hc], kv[:, hc]) * (X_HEAD_DIM ** -0.5)
        m = jnp.max(s, axis=-1, keepdims=True)
        p = jnp.exp(s - m)
        p = p / jnp.sum(p, axis=-1, keepdims=True)
        heads.append(_dot(p.astype(BF16), kv[:, xw + h * X_HEAD_DIM:xw + (h + 1) * X_HEAD_DIM]))
    ox = jnp.concatenate(heads, axis=-1).astype(BF16)
    x2 = x1 + _dot(ox, wo_ref[...])
    x2_ref[sl, :] = x2

    hm = _rms(x2, gm_ref[...])
    hm_hi, hm_lo = _split(hm)
    hm_ref[sl, :] = hm_hi
    nr = hm.shape[0]
    cross = _dot(jnp.concatenate([hm_hi, hm_lo], axis=0), wr_ref[...])
    logits = (cross[:nr, :LANES] + cross[:nr, LANES:]) + (cross[nr:, :LANES] + cross[nr:, LANES:])
    lane = lax.broadcasted_iota(jnp.int32, logits.shape, 1)
    logits = jnp.where(lane < N_EXPERTS, logits, -1e30)
    m = jnp.max(logits, axis=-1, keepdims=True)
    p = jnp.exp(logits - m)
    aff_ref[sl, :] = p / jnp.sum(p, axis=-1, keepdims=True)


def _merge(x2d, z2d, outs, lses, gates, wuh, wua, wout, gx, wq, kv, wo, gm, wr_cat, seq, tm=512, nsplit=2):
    n, d = x2d.shape
    nseq = seq // tm
    row = lambda i: (i, 0)
    const = lambda i: (0, 0)

    def rspec(a):
        return pl.BlockSpec((tm, a.shape[1]), row)

    def aspec(a):
        dil = a.shape[2] // ATTN_OUT
        return pl.BlockSpec((1, tm // dil, a.shape[2]), lambda i: (i // nseq, i % nseq, 0))

    def cspec(a):
        return pl.BlockSpec(a.shape, const)

    rows = tm // nsplit
    perms = []
    for a in outs[1:]:
        dil = a.shape[2] // ATTN_OUT
        src = (np.arange(rows) % dil) * (rows // dil) + np.arange(rows) // dil
        perms.append(jnp.asarray(np.eye(rows)[src], dtype=BF16))

    in_arrays = [x2d, z2d, *outs, *lses, gates, wuh, wua, wout, gx, wq, kv, wo, gm, wr_cat, *perms]
    in_specs = ([rspec(a) for a in in_arrays[:2]] + [aspec(a) for a in in_arrays[2:8]] + [rspec(gates)]
                + [cspec(a) for a in in_arrays[9:14]]
                + [pl.BlockSpec((1,) + kv.shape[1:], lambda i: (i // nseq, 0, 0))]
                + [cspec(a) for a in in_arrays[15:]])
    return pl.pallas_call(
        functools.partial(_merge_body, nsplit=nsplit),
        grid=(n // tm,),
        in_specs=in_specs,
        out_specs=[pl.BlockSpec((tm, d), row), pl.BlockSpec((tm, d), row), pl.BlockSpec((tm, LANES), row)],
        out_shape=[jax.ShapeDtypeStruct((n, d), F32), jax.ShapeDtypeStruct((n, d), BF16),
                   jax.ShapeDtypeStruct((n, LANES), F32)],
        compiler_params=_cparams(("arbitrary",)),
        name="merge",
    )(*in_arrays)


def _topk_body(aff_ref, tri_ref, rank_ref, bounds_ref, *, cap, tok_block):
    a = aff_ref[0]
    e, s = a.shape

    def count(mask):
        return jnp.sum(jnp.where(mask, 1.0, 0.0), axis=-1, keepdims=True)

    def as_float(bits):
        return pltpu.bitcast(jnp.broadcast_to(bits, (e, LANES)), F32)[:, 0:1]

    def search(i, thr):
        cand = thr | (jnp.int32(1) << (30 - i))
        return jnp.where(count(a >= as_float(cand)) >= cap, cand, thr)

    thr = as_float(lax.fori_loop(0, 31, search, jnp.zeros((e, 1), jnp.int32)))
    gt = a > thr
    eq = a == thr
    need = cap - count(gt)

    def prefix_excl(mask):
        mf = jnp.where(mask, 1.0, 0.0)
        parts = []
        carry = jnp.zeros((e, 1), F32)
        for c in range(s // LANES):
            blk = mf[:, c * LANES:(c + 1) * LANES]
            inc = _dot(blk.astype(BF16), tri_ref[...])
            parts.append(inc - blk + carry)
            carry = carry + inc[:, LANES - 1:LANES]
        return jnp.concatenate(parts, axis=-1)

    sel = gt | (eq & (prefix_excl(eq) < need))
    excl = prefix_excl(sel)
    rank_ref[0] = jnp.where(sel, excl, -1.0)

    tok = lax.broadcasted_iota(jnp.int32, (e, s), 1)
    lane = lax.broadcasted_iota(jnp.int32, (e, LANES), 1)
    bounds = jnp.zeros((e, LANES), F32)
    for j in range(s // tok_block + 1):
        bounds = jnp.where(lane == j, count(sel & (tok < j * tok_block)), bounds)
    bounds_ref[0] = bounds.astype(jnp.int32)


def _topk(aff_t, cap, tok_block):
    b, e, s = aff_t.shape
    assert s // tok_block + 1 <= LANES
    tri = jnp.asarray(np.triu(np.ones((LANES, LANES))), dtype=BF16)
    return pl.pallas_call(
        functools.partial(_topk_body, cap=cap, tok_block=tok_block),
        grid=(b,),
        in_specs=[pl.BlockSpec((1, e, s), lambda i: (i, 0, 0)),
                  pl.BlockSpec((LANES, LANES), lambda i: (0, 0))],
        out_specs=[pl.BlockSpec((1, e, s), lambda i: (i, 0, 0)),
                   pl.BlockSpec((1, e, LANES), lambda i: (i, 0, 0))],
        out_shape=[jax.ShapeDtypeStruct((b, e, s), F32),
                   jax.ShapeDtypeStruct((b, e, LANES), jnp.int32)],
        compiler_params=_cparams(("arbitrary",)),
        name="topk",
    )(aff_t, tri)


def _moe_windows(bounds_ref, base, ne, step, tokens, window):
    per = tokens // MOE_BOUND_STEP
    starts = []
    passes = jnp.int32(0)
    for e in range(ne):
        r_lo = bounds_ref[base + e * LANES + step * per]
        r_hi = bounds_ref[base + e * LANES + (step + 1) * per]
        ws = (r_lo // BF16_ROWS) * BF16_ROWS
        starts.append(ws)
        passes = jnp.maximum(passes, (r_hi - ws + window - 1) // window)
    return starts, passes


def _gather_body(bounds_ref, rank_ref, hm_ref, affh_ref, affl_ref, xe_ref, gs_ref, *, cap, eg):
    w = GATHER_WINDOW
    ne, tc = rank_ref.shape[1], rank_ref.shape[2]
    bi = pl.program_id(0)
    ci = pl.program_id(1)

    @pl.when(ci == 0)
    def _():
        xe_ref[...] = jnp.zeros_like(xe_ref)
        gs_ref[...] = jnp.zeros_like(gs_ref)

    starts, passes = _moe_windows(bounds_ref, bi * ne * LANES, ne, ci, tc, w)
    iota_w = lax.broadcasted_iota(jnp.int32, (w, 1), 0)

    def one_pass(p, carry):
        for g0 in range(0, ne, eg):
            lo = [starts[e] + p * w for e in range(g0, g0 + eg)]
            phys = [pl.multiple_of(jnp.minimum(v, cap - w), BF16_ROWS) for v in lo]
            rk = jnp.concatenate([jnp.broadcast_to(rank_ref[0, e:e + 1, :], (w, tc))
                                  for e in range(g0, g0 + eg)], axis=0)
            slot = jnp.concatenate([iota_w + v for v in phys], axis=0).astype(F32)
            lom = jnp.concatenate([jnp.zeros_like(iota_w) + v for v in lo], axis=0).astype(F32)
            onehot = jnp.where((rk == slot) & (slot >= lom), 1.0, 0.0).astype(BF16)
            res = _dot(onehot, hm_ref[0])
            resg = _dot(onehot, affh_ref[0]) + _dot(onehot, affl_ref[0])
            for k, e in enumerate(range(g0, g0 + eg)):
                rows = pl.ds(phys[k], w)
                xe_ref[0, e, rows, :] = (xe_ref[0, e, rows, :].astype(F32) + res[k * w:(k + 1) * w]).astype(BF16)
                gs_ref[0, e, rows, :] = gs_ref[0, e, rows, :] + resg[k * w:(k + 1) * w]
        return carry

    lax.fori_loop(0, passes, one_pass, 0)


def _gather(bounds, rank, hm, aff_hi, aff_lo, cap, eg=16):
    b, e, s = rank.shape
    d = hm.shape[2]
    tc = GATHER_TOKENS
    return pl.pallas_call(
        functools.partial(_gather_body, cap=cap, eg=eg),
        grid_spec=pltpu.PrefetchScalarGridSpec(
            num_scalar_prefetch=1,
            grid=(b, s // tc),
            in_specs=[pl.BlockSpec((1, e, tc), lambda bi, ci, bnd: (bi, 0, ci)),
                      pl.BlockSpec((1, tc, d), lambda bi, ci, bnd: (bi, ci, 0)),
                      pl.BlockSpec((1, tc, LANES), lambda bi, ci, bnd: (bi, ci, 0)),
                      pl.BlockSpec((1, tc, LANES), lambda bi, ci, bnd: (bi, ci, 0))],
            out_specs=[pl.BlockSpec((1, e, cap, d), lambda bi, ci, bnd: (bi, 0, 0, 0)),
                       pl.BlockSpec((1, e, cap, LANES), lambda bi, ci, bnd: (bi, 0, 0, 0))],
        ),
        out_shape=[jax.ShapeDtypeStruct((b, e, cap, d), BF16),
                   jax.ShapeDtypeStruct((b, e, cap, LANES), F32)],
        compiler_params=_cparams(("arbitrary", "arbitrary")),
        name="gather",
    )(bounds.reshape(-1), rank, hm, aff_hi, aff_lo)


def _ffn_body(xe_ref, gs_ref, wg_ref, wu_ref, wd_ref, ye_ref, acc_ref):
    f = pl.program_id(1)
    nb = xe_ref.shape[0]
    wg = wg_ref[0].astype(BF16)
    wu = wu_ref[0].astype(BF16)
    wd = wd_ref[0].astype(BF16)

    @pl.when(f == 0)
    def _():
        acc_ref[...] = jnp.zeros_like(acc_ref)

    for b in range(nb):
        xe = xe_ref[b, 0]
        a = _dot(xe, wg)
        u = _dot(xe, wu)
        hsw = (a * jax.nn.sigmoid(a) * u).astype(BF16)
        part = _dot(hsw, wd)
        acc_ref[b] += part

    @pl.when(f == pl.num_programs(1) - 1)
    def _():
        lane = lax.broadcasted_iota(jnp.int32, gs_ref.shape[2:], 1)
        mine = lane == pl.program_id(0)
        for b in range(nb):
            gate = jnp.sum(jnp.where(mine, gs_ref[b, 0], 0.0), axis=-1, keepdims=True)
            ye_ref[b, 0] = (acc_ref[b] * gate).astype(BF16)


def _ffn(xe, gs, wg, wu, wd, ft=1024):
    b, e, cap, d = xe.shape
    ff = wg.shape[2]
    one = pl.Buffered(1)
    return pl.pallas_call(
        _ffn_body,
        grid=(e, ff // ft),
        in_specs=[pl.BlockSpec((b, 1, cap, d), lambda ei, fi: (0, ei, 0, 0), pipeline_mode=one),
                  pl.BlockSpec((b, 1, cap, LANES), lambda ei, fi: (0, ei, 0, 0), pipeline_mode=one),
                  pl.BlockSpec((1, d, ft), lambda ei, fi: (ei, 0, fi)),
                  pl.BlockSpec((1, d, ft), lambda ei, fi: (ei, 0, fi)),
                  pl.BlockSpec((1, ft, d), lambda ei, fi: (ei, fi, 0))],
        out_specs=pl.BlockSpec((b, 1, cap, d), lambda ei, fi: (0, ei, 0, 0)),
        out_shape=jax.ShapeDtypeStruct((b, e, cap, d), BF16),
        scratch_shapes=[pltpu.VMEM((b, cap, d), F32)],
        compiler_params=_cparams(("arbitrary", "arbitrary")),
        name="ffn",
    )(xe, gs, wg, wu, wd)


def _scatter_body(bounds_ref, rank_ref, ye_ref, x2_ref, g_ref, out_ref, acc_ref, *, eg):
    w = SCATTER_WINDOW
    ts = x2_ref.shape[1]
    ne, cap = ye_ref.shape[1], ye_ref.shape[2]
    starts, passes = _moe_windows(bounds_ref, pl.program_id(0) * ne * LANES, ne, pl.program_id(1), ts, w)
    rank = rank_ref[0]
    acc_ref[...] = x2_ref[0]
    iota_w = lax.broadcasted_iota(jnp.int32, (1, w), 1)

    def one_pass(p, carry):
        for g0 in range(0, ne, eg):
            onehots, rows = [], []
            for e in range(g0, g0 + eg):
                lo = starts[e] + p * w
                phys = pl.multiple_of(jnp.minimum(lo, cap - w), BF16_ROWS)
                slot = (iota_w + phys).astype(F32)
                hit = (rank[:, e:e + 1] == slot) & (slot >= lo.astype(F32))
                onehots.append(jnp.where(hit, 1.0, 0.0).astype(BF16))
                rows.append(ye_ref[0, e, pl.ds(phys, w), :])
            acc_ref[...] += _dot(jnp.concatenate(onehots, axis=1), jnp.concatenate(rows, axis=0))
        return carry

    lax.fori_loop(0, passes, one_pass, 0)
    out_ref[0] = _rms(acc_ref[...], g_ref[...])


def _scatter(bounds, rank_t, ye, x2, g, eg=8):
    b, s, e = rank_t.shape
    cap, d = ye.shape[2], ye.shape[3]
    ts = SCATTER_TOKENS
    return pl.pallas_call(
        functools.partial(_scatter_body, eg=eg),
        grid_spec=pltpu.PrefetchScalarGridSpec(
            num_scalar_prefetch=1,
            grid=(b, s // ts),
            in_specs=[pl.BlockSpec((1, ts, e), lambda bi, ti, bnd: (bi, ti, 0)),
                      pl.BlockSpec((1, e, cap, d), lambda bi, ti, bnd: (bi, 0, 0, 0)),
                      pl.BlockSpec((1, ts, d), lambda bi, ti, bnd: (bi, ti, 0)),
                      pl.BlockSpec((1, d), lambda bi, ti, bnd: (0, 0))],
            out_specs=pl.BlockSpec((1, ts, d), lambda bi, ti, bnd: (bi, ti, 0)),
            scratch_shapes=[pltpu.VMEM((ts, d), F32)],
        ),
        out_shape=jax.ShapeDtypeStruct((b, s, d), F32),
        compiler_params=_cparams(("arbitrary", "arbitrary")),
        name="scatter",
    )(bounds.reshape(-1), rank_t, ye, x2, g)


def kernel(x, mem, ln_mix_g, w_in, hy_conv_w, hy_conv_b, filt_w1, filt_b1, filt_freq1, filt_w2, filt_b2,
           filt_freq2, filt_w3, hy_skip, w_up_hy, w_up_attn, w_gate, b_gate, w_out, ln_x_g, ln_mem_g,
           w_q_x, w_kv_mem, w_o_x, ln_moe_g, w_router, w_e_gate, w_e_up, w_e_down, ln_f_g):
    b, s, d = x.shape
    n = b * s
    x2d = x.reshape(n, d)

    half = HEAD_DIM // 2
    inv = ROPE_THETA ** (-jnp.arange(0, HEAD_DIM, 2, dtype=F32) / HEAD_DIM)
    ang = jnp.arange(s, dtype=F32)[:, None] * inv[None, :]
    lane = np.arange(LANES)
    cos_t = jnp.cos(ang)[:, lane % half]
    sin_t = jnp.sin(ang)[:, lane % half] * jnp.asarray(np.where(lane % HEAD_DIM < half, -1.0, 1.0), F32)[None, :]

    w_all = jnp.concatenate([w_in, w_gate], axis=1).astype(BF16)
    p_hy, *qkv, gates = _proj(x2d, ln_mix_g[None], w_all, b_gate[None], cos_t, sin_t, b, s)

    t_col = jnp.linspace(0.0, 1.0, s, dtype=F32)[:, None]
    grid_col = 2.0 * math.pi * jnp.arange(s, dtype=F32)[:, None] / s
    bands = jnp.linspace(1e-4, HY_BANDS - 1, HY_BANDS, dtype=F32)[None, :]
    feat = _filt(t_col, grid_col, bands, filt_w1, filt_b1, filt_freq1, filt_w2, filt_b2, filt_freq2)
    delta = jnp.linspace(math.log(HY_TARGET) / HY_SLOW_PCT, math.log(HY_TARGET) / HY_FAST_PCT,
                         HY_WIDTH, dtype=F32)[None, :]
    tables = _fft_tables(s)
    khat = _hyfilt(t_col, delta, feat, filt_w3, tables[1], tables[3])
    z = _hyena(p_hy, hy_conv_w, hy_conv_b[None], hy_skip, khat, tables)

    outs, lses = [], []
    for g, (window, dil) in enumerate(DIL_PAIRS):
        o_g, l_g = _attn_group(*qkv[3 * g:3 * g + 3], g, window // (2 * dil))
        outs.append(o_g)
        lses.append(l_g)

    kv = _memkv(mem, ln_mem_g[None], w_kv_mem.astype(BF16))
    wr_pad = jnp.pad(w_router, ((0, 0), (0, LANES - N_EXPERTS)))
    wr_cat = jnp.concatenate(_split(wr_pad), axis=1)
    x2, hm, aff = _merge(x2d, z.reshape(n, HY_WIDTH), outs, lses, gates,
                         w_up_hy.astype(BF16), w_up_attn.astype(BF16), w_out.astype(BF16),
                         ln_x_g[None], w_q_x.astype(BF16), kv, w_o_x.astype(BF16), ln_moe_g[None],
                         wr_cat, s)

    cap = max(1, EC_FACTOR * s // N_EXPERTS)
    aff3 = aff.reshape(b, s, LANES)
    aff_hi = aff3.astype(BF16)
    aff_lo = (aff3 - aff_hi.astype(F32)).astype(BF16)
    rank, bounds = _topk(aff3[:, :, :N_EXPERTS].transpose(0, 2, 1), cap, MOE_BOUND_STEP)
    xe, gs = _gather(bounds, rank, hm.reshape(b, s, d), aff_hi, aff_lo, cap)
    ye = _ffn(xe, gs, w_e_gate, w_e_up, w_e_down)
    return _scatter(bounds, rank.transpose(0, 2, 1), ye, x2.reshape(b, s, d), ln_f_g[None])
```

```python
import functools
import math

import numpy as np
import jax
import jax.numpy as jnp
from jax import lax
from jax.experimental import pallas as pl
from jax.experimental.pallas import tpu as pltpu

F32 = jnp.float32
BF16 = jnp.bfloat16
HIGHEST = lax.Precision.HIGHEST

EPS = 1e-6
HY_WIDTH = 768
HY_BANDS = 16
HY_FFN = 64
HY_FAST_PCT = 0.3
HY_SLOW_PCT = 1.5
HY_TARGET = 1e-2
HEAD_DIM = 64
HEADS_PER_GROUP = 4
DIL_PAIRS = ((128, 1), (512, 4), (2048, 16))
N_GROUPS = len(DIL_PAIRS)
ATTN_WIDTH = N_GROUPS * HEADS_PER_GROUP * HEAD_DIM
ATTN_OUT = HEADS_PER_GROUP * HEAD_DIM
ROPE_THETA = 10000.0
X_HEADS = 4
X_HEAD_DIM = 128
N_EXPERTS = 16
EC_FACTOR = 2

LANES = 128
SUBLANES = 8
MXU_COLS = 256
VMEM_LIMIT = 56 * 1024 * 1024

FFT_N2 = 128
HY_CB = 128
FFT_UNROLL = 8
FFT_PAD = 8
MOE_BOUND_STEP = 256
GATHER_TOKENS, GATHER_WINDOW = 256, 64
SCATTER_TOKENS, SCATTER_WINDOW = 512, 128
BF16_ROWS = 16


def _cparams(sem, vmem=VMEM_LIMIT):
    return pltpu.CompilerParams(dimension_semantics=sem, vmem_limit_bytes=vmem)


def _rms(x, g):
    return x * lax.rsqrt(jnp.mean(x * x, axis=-1, keepdims=True) + EPS) * g


def _dot(a, b):
    return jnp.dot(a, b, preferred_element_type=F32)


def _dot_hi(a, b):
    return jnp.dot(a, b, precision=HIGHEST, preferred_element_type=F32)


def _split(a):
    hi = a.astype(BF16)
    return hi, (a - hi.astype(F32)).astype(BF16)


def _dot_split(a, b):
    a_hi, a_lo = _split(a)
    b_hi, b_lo = _split(b)
    return _dot(a_hi, b_hi) + _dot(a_lo, b_hi) + _dot(a_hi, b_lo)


def _dot_t(a, b):
    return lax.dot_general(a, b, (((1,), (1,)), ((), ())), preferred_element_type=F32)


def _proj_body(x_ref, g_ref, w_ref, bg_ref, cos_ref, sin_ref, *rest, hyc, aw):
    ng = N_GROUPS
    perm_refs = (None,) + rest[:ng - 1]
    phy_ref = rest[ng - 1]
    qkv_refs = rest[ng:ng + 3 * ng]
    gate_ref = rest[ng + 3 * ng]
    h = _rms(x_ref[...], g_ref[...]).astype(BF16)
    phy = _dot(h, w_ref[:, :hyc])
    for j in range(hyc // HY_CB):
        phy_ref[0, j] = phy[:, j * HY_CB:(j + 1) * HY_CB]
    cos = cos_ref[...]
    sin = sin_ref[...]
    tm = cos.shape[0]
    lane = lax.broadcasted_iota(jnp.int32, (tm, LANES), 1)
    first = (lane % HEAD_DIM) < (HEAD_DIM // 2)

    def rope(t, scale):
        chunks = []
        for j in range(aw // LANES):
            tj = t[:, j * LANES:(j + 1) * LANES]
            partner = jnp.where(first, pltpu.roll(tj, LANES - HEAD_DIM // 2, 1),
                                pltpu.roll(tj, HEAD_DIM // 2, 1))
            chunks.append(((tj * cos + partner * sin) * scale).astype(BF16))
        return chunks

    gl = ATTN_OUT // LANES
    q = rope(_dot(h, w_ref[:, hyc:hyc + aw]), HEAD_DIM ** -0.5)
    k = rope(_dot(h, w_ref[:, hyc + aw:hyc + 2 * aw]), 1.0)
    vf = _dot(h, w_ref[:, hyc + 2 * aw:hyc + 3 * aw]).astype(BF16)
    v = [vf[:, j * LANES:(j + 1) * LANES] for j in range(aw // LANES)]
    for g in range(ng):
        for i, t in enumerate((q, k, v)):
            out_ref = qkv_refs[3 * g + i]
            tg = jnp.concatenate(t[g * gl:(g + 1) * gl], axis=1)
            if perm_refs[g] is not None:
                tg = _dot(perm_refs[g][...], tg).astype(BF16)
            dil = out_ref.shape[1]
            rows = tm // dil
            for r in range(dil):
                out_ref[0, r] = tg[r * rows:(r + 1) * rows, :]
    gate_ref[...] = jax.nn.sigmoid(_dot(h, w_ref[:, hyc + 3 * aw:]) + bg_ref[...]).astype(BF16)


def _proj(x2d, g, w_all, b_gate, cos_t, sin_t, batch, seq, tm=256):
    n, d = x2d.shape
    hyc = 3 * HY_WIDTH
    aw = ATTN_WIDTH
    gd = w_all.shape[1] - hyc - 3 * aw
    nseq = seq // tm
    row = lambda i: (i, 0)
    const = lambda i: (0, 0)
    perms = []
    for _, dil in DIL_PAIRS[1:]:
        rows = tm // dil
        src = (np.arange(tm) % rows) * dil + np.arange(tm) // rows
        perms.append(jnp.asarray(np.eye(tm)[src], dtype=BF16))
    qkv_specs, qkv_shapes = [], []
    for _, dil in DIL_PAIRS:
        for _ in range(3):
            qkv_specs.append(pl.BlockSpec((1, dil, tm // dil, ATTN_OUT), lambda i: (i // nseq, 0, i % nseq, 0)))
            qkv_shapes.append(jax.ShapeDtypeStruct((batch, dil, seq // dil, ATTN_OUT), BF16))
    return pl.pallas_call(
        functools.partial(_proj_body, hyc=hyc, aw=aw),
        grid=(n // tm,),
        in_specs=[
            pl.BlockSpec((tm, d), row),
            pl.BlockSpec((1, d), const),
            pl.BlockSpec(w_all.shape, const, pipeline_mode=pl.Buffered(1)),
            pl.BlockSpec((1, gd), const),
            pl.BlockSpec((tm, LANES), lambda i: (i % nseq, 0)),
            pl.BlockSpec((tm, LANES), lambda i: (i % nseq, 0)),
        ] + [pl.BlockSpec((tm, tm), const) for _ in perms],
        out_specs=([pl.BlockSpec((1, hyc // HY_CB, tm, HY_CB), lambda i: (i // nseq, 0, i % nseq, 0))]
                   + qkv_specs + [pl.BlockSpec((tm, gd), row)]),
        out_shape=([jax.ShapeDtypeStruct((batch, hyc // HY_CB, seq, HY_CB), F32)] + qkv_shapes
                   + [jax.ShapeDtypeStruct((n, gd), BF16)]),
        compiler_params=_cparams(("arbitrary",)),
        name="proj",
    )(x2d, g, w_all, b_gate, cos_t, sin_t, *perms)


def _filt_body(t_ref, grid_ref, bands_ref, w1t_ref, w1c_ref, w1s_ref, b1_ref, f1_ref,
               w2_ref, b2_ref, f2_ref, feat_ref):
    ang = bands_ref[...] * grid_ref[...]
    pre = (t_ref[...] * w1t_ref[...] + _dot_hi(jnp.cos(ang), w1c_ref[...])
           + _dot_hi(-jnp.sin(ang), w1s_ref[...]))
    h = jnp.sin(f1_ref[...] * (pre + b1_ref[...]))
    feat_ref[...] = jnp.sin(f2_ref[...] * (_dot_hi(h, w2_ref[...]) + b2_ref[...]))


def _filt(t_col, grid_col, bands, w1, b1, f1, w2, b2, f2):
    length = t_col.shape[0]
    nb = bands.shape[1]
    args = (t_col, grid_col, bands, w1[0:1], w1[1:1 + nb], w1[1 + nb:], b1[None], f1[None],
            w2, b2[None], f2[None])
    return pl.pallas_call(
        _filt_body,
        out_shape=jax.ShapeDtypeStruct((length, HY_FFN), F32),
        compiler_params=_cparams(None),
        name="filt",
    )(*args)


def _fft_tables(length):
    n = 2 * length
    n2 = FFT_N2
    n1 = n // n2
    h1 = n1 // 2
    k1 = np.arange(n1)[:, None]
    g_sig = np.zeros((n2, 2 * n1, 2 * h1))
    g_flt = np.zeros((n2, 2 * n1, 2 * h1))
    g_inv = np.zeros((n2, 2 * h1, 2 * n1))
    j = np.arange(h1)[None, :]
    for r in range(n2):
        th = 2 * np.pi * (k1 * (n2 * j + r) % n) / n
        gr, gi = np.cos(th), -np.sin(th)
        g_sig[r, 0::2, :h1] = gr
        g_sig[r, 0::2, h1:] = -gi
        g_sig[r, 1::2, :h1] = gi
        g_sig[r, 1::2, h1:] = gr
        g_flt[r, 0::2, :h1] = gr
        g_flt[r, 1::2, :h1] = gi
        m = n2 * (j + 1) - r
        thb = 2 * np.pi * (k1 * m % n) / n
        live = (m < length).astype(np.float64)
        g_flt[r, 0::2, h1:] = np.cos(thb) * live
        g_flt[r, 1::2, h1:] = np.sin(thb) * live
        wr, wi = (np.cos(th) / n).T, (np.sin(th) / n).T
        g_inv[r, :h1, 0::2] = wr
        g_inv[r, :h1, 1::2] = -wi
        g_inv[r, h1:, 0::2] = wi
        g_inv[r, h1:, 1::2] = wr
    a = np.arange(n2)
    th2 = 2 * np.pi * (np.outer(a, a) % n2) / n2
    c2, s2 = np.cos(th2), np.sin(th2)
    f_fwd = np.block([[c2, s2], [-s2, c2]])
    f_inv = np.block([[c2, -s2], [s2, c2]])
    cast = lambda z: jnp.asarray(z, dtype=F32).astype(BF16)
    return cast(g_sig), cast(g_flt), cast(g_inv), cast(f_fwd), cast(f_inv)


def _fft_stage1(src_a_ref, src_b_ref, tab_ref, work_ref, n1, is_filter):
    n2 = FFT_N2
    grp = SUBLANES

    def body(i, c):
        r0 = pl.multiple_of(i * grp, grp)
        b0 = pl.multiple_of(n2 - grp - r0, grp) if is_filter else r0
        a = jnp.swapaxes(src_a_ref[:, pl.ds(r0, grp), :], 0, 1)
        b = jnp.swapaxes(src_b_ref[:, pl.ds(b0, grp), :], 0, 1)
        outs = []
        for s in range(grp):
            rhs = jnp.concatenate([a[s], b[grp - 1 - s] if is_filter else b[s]], axis=0).astype(BF16)
            outs.append(_dot(tab_ref[r0 + s], rhs))
        work_ref[:, pl.ds(r0, grp), :] = jnp.swapaxes(jnp.stack(outs, axis=0), 0, 1)
        return c

    lax.fori_loop(0, n2 // grp, body, 0, unroll=2)


def _hyfilt_body(t_ref, ts_ref, delta_ref, feat_ref, feats_ref, w3f_ref, w3b_ref, gflt_ref, ffwd_ref, khat_ref,
                 hf_ref, hb_ref, work_ref, *, length, rc):
    n2 = FFT_N2
    n1 = 2 * length // n2
    tiles = rc // n2
    cb = hf_ref.shape[2]

    def gen(i, c):
        r0 = pl.multiple_of(i * rc, rc)
        j0 = pl.multiple_of(i * tiles, tiles)
        delta = jnp.abs(delta_ref[...])
        hf = _dot_split(feat_ref[pl.ds(r0, rc), :], w3f_ref[...]) * jnp.exp(-t_ref[pl.ds(r0, rc), :] * delta)
        hb = _dot_split(feats_ref[pl.ds(r0, rc), :], w3b_ref[...]) * jnp.exp(-ts_ref[pl.ds(r0, rc), :] * delta)
        hf_ref[pl.ds(j0, tiles), :n2, :] = hf.reshape(tiles, n2, cb)
        hb_ref[pl.ds(j0, tiles), :n2, :] = hb.reshape(tiles, n2, cb)
        return c

    lax.fori_loop(0, length // rc, gen, 0)
    _fft_stage1(hf_ref, hb_ref, gflt_ref, work_ref, n1, True)

    side = MXU_COLS // cb

    def stage2(i, c):
        k1s = [i * side + v for v in range(side)]
        blks = [work_ref[pl.ds(pl.multiple_of(2 * k1, 2), 2), :n2, :].reshape(2 * n2, cb).astype(BF16)
                for k1 in k1s]
        x = _dot(ffwd_ref[...], jnp.concatenate(blks, axis=1)).astype(BF16)
        for v, k1 in enumerate(k1s):
            khat_ref[0, 0, pl.ds(pl.multiple_of(k1 * (2 * n2), 2 * n2), 2 * n2), :] = x[:, v * cb:(v + 1) * cb]
        return c

    lax.fori_loop(0, n1 // side, stage2, 0, unroll=FFT_UNROLL // side)


def _hyfilt(t_col, delta, feat, w3, g_flt, f_fwd, rc=512):
    s = t_col.shape[0]
    cb = HY_CB
    ncb = HY_WIDTH // cb
    n2 = FFT_N2
    n1 = 2 * s // n2
    norder = w3.shape[1] // (2 * HY_WIDTH)
    one = pl.Buffered(1)

    def full(a):
        nd = a.ndim
        return pl.BlockSpec(a.shape, lambda o, c: (0,) * nd, pipeline_mode=one)

    shift = lambda a: jnp.concatenate([a[1:], jnp.zeros_like(a[:1])], axis=0)
    return pl.pallas_call(
        functools.partial(_hyfilt_body, length=s, rc=rc),
        grid=(norder, ncb),
        in_specs=[full(t_col), full(t_col),
                  pl.BlockSpec((1, cb), lambda o, c: (0, c)),
                  full(feat), full(feat),
                  pl.BlockSpec((HY_FFN, cb), lambda o, c: (0, 2 * ncb * o + c)),
                  pl.BlockSpec((HY_FFN, cb), lambda o, c: (0, 2 * ncb * o + ncb + c)),
                  full(g_flt), full(f_fwd)],
        out_specs=pl.BlockSpec((1, 1, 2 * n1 * n2, cb), lambda o, c: (o, c, 0, 0)),
        out_shape=jax.ShapeDtypeStruct((norder, ncb, 2 * n1 * n2, cb), BF16),
        scratch_shapes=[pltpu.VMEM((n1 // 2, n2 + FFT_PAD, cb), F32)] * 2
        + [pltpu.VMEM((2 * n1, n2 + FFT_PAD, cb), F32)],
        compiler_params=_cparams(("arbitrary", "arbitrary")),
        name="hyfilt",
    )(t_col, shift(t_col), delta, feat, shift(feat), w3, w3, g_flt, f_fwd)


def _hyena_body(pz_ref, phy_hbm, cwz_ref, cwg1_ref, cwg2_ref, cbz_ref, cbg1_ref, cbg2_ref,
                skip_ref, khat_ref, gsig_ref, ginv_ref, ffwd_ref, finv_ref,
                out_ref,
                za_ref, zb_ref, ya_ref, yb_ref, work_ref, pg1_ref, pg2_ref, sem, *, length, rc):
    ncb = pl.num_programs(0)
    pair = pl.ds(2 * pl.program_id(1), 2)

    def gate_copy(order, dst_ref):
        src = phy_hbm.at[pair, pl.ds((order + 1) * ncb + pl.program_id(0), 1)]
        return pltpu.make_async_copy(src, dst_ref, sem.at[order])

    gate_copies = [gate_copy(0, pg1_ref), gate_copy(1, pg2_ref)]
    for cp in gate_copies:
        cp.start()
    n2 = FFT_N2
    n1 = 2 * length // n2
    h1 = n1 // 2
    tiles = rc // n2
    cb = za_ref.shape[2]
    side = MXU_COLS // cb
    sub = SUBLANES

    def conv3(p_ref, b, r0, w_ref, bias_ref):
        p = p_ref[b, 0, pl.ds(r0, rc), :]
        before = p_ref[b, 0, pl.ds(pl.multiple_of(jnp.maximum(r0 - sub, 0), sub), sub), :][sub - 1:sub, :]
        after = p_ref[b, 0, pl.ds(pl.multiple_of(jnp.minimum(r0 + rc, length - sub), sub), sub), :][0:1, :]
        before = jnp.where(r0 == 0, 0.0, before)
        after = jnp.where(r0 + rc == length, 0.0, after)
        row = lax.broadcasted_iota(jnp.int32, p.shape, 0)
        prev = jnp.where(row == 0, before, pltpu.roll(p, 1, 0))
        nxt = jnp.where(row == rc - 1, after, pltpu.roll(p, rc - 1, 0))
        return prev * w_ref[0:1, :] + p * w_ref[1:2, :] + nxt * w_ref[2:3, :] + bias_ref[...]

    def stage2(o):
        def body(i, c):
            k1s = [i * FFT_UNROLL + u for u in range(FFT_UNROLL)]
            blks = [work_ref[pl.ds(pl.multiple_of(2 * k1, 2), 2), :n2, :].reshape(2 * n2, cb).astype(BF16)
                    for k1 in k1s]
            outs = []
            for u in range(0, FFT_UNROLL, side):
                x = _dot(ffwd_ref[...], jnp.concatenate(blks[u:u + side], axis=1))
                ys = []
                for v in range(side):
                    k1 = k1s[u + v]
                    kh = khat_ref[o, 0, pl.ds(pl.multiple_of(k1 * (2 * n2), 2 * n2), 2 * n2), :].astype(F32)
                    xr, xi = x[:n2, v * cb:(v + 1) * cb], x[n2:, v * cb:(v + 1) * cb]
                    kr, ki = kh[:n2], kh[n2:]
                    ys.append(jnp.concatenate([xr * kr - xi * ki, xr * ki + xi * kr], axis=0).astype(BF16))
                out = _dot(finv_ref[...], jnp.concatenate(ys, axis=1))
                outs.extend(out[:, v * cb:(v + 1) * cb] for v in range(side))
            for k1, out in zip(k1s, outs):
                work_ref[pl.ds(pl.multiple_of(2 * k1, 2), 2), :n2, :] = out.reshape(2, n2, cb)
            return c
        lax.fori_loop(0, n1 // FFT_UNROLL, body, 0)

    def stage3():
        grp = SUBLANES

        def body(i, c):
            r0 = pl.multiple_of(i * grp, grp)
            wk = jnp.swapaxes(work_ref[:, pl.ds(r0, grp), :], 0, 1)
            res = jnp.stack([_dot(ginv_ref[r0 + s], wk[s].astype(BF16)) for s in range(grp)], axis=0)
            ya_ref[:, pl.ds(r0, grp), :] = jnp.swapaxes(res[:, :h1], 0, 1)
            yb_ref[:, pl.ds(r0, grp), :] = jnp.swapaxes(res[:, h1:], 0, 1)
            return c
        lax.fori_loop(0, n2 // grp, body, 0, unroll=2)

    def rows3(ref, i):
        return ref.at[pl.ds(pl.multiple_of(i * tiles, tiles), tiles), :n2, :]

    def load_z(i, c):
        r0 = pl.multiple_of(i * rc, rc)
        rows3(za_ref, i)[...] = conv3(pz_ref, 0, r0, cwz_ref, cbz_ref).reshape(tiles, n2, cb)
        rows3(zb_ref, i)[...] = conv3(pz_ref, 1, r0, cwz_ref, cbz_ref).reshape(tiles, n2, cb)
        return c

    lax.fori_loop(0, length // rc, load_z, 0)
    for o, (pg_ref, cw_ref, cb_ref) in enumerate(((pg1_ref, cwg1_ref, cbg1_ref), (pg2_ref, cwg2_ref, cbg2_ref))):
        _fft_stage1(za_ref, zb_ref, gsig_ref, work_ref, n1, False)
        stage2(o)
        stage3()
        skip = skip_ref[o:o + 1, :]
        gate_copies[o].wait()

        def gate(i, c):
            r0 = pl.multiple_of(i * rc, rc)
            rows = pl.ds(r0, rc)
            ya, yb = (rows3(r, i)[...].reshape(rc, cb) for r in (ya_ref, yb_ref))
            za, zb = (rows3(r, i)[...].reshape(rc, cb) for r in (za_ref, zb_ref))
            new_a = conv3(pg_ref, 0, r0, cw_ref, cb_ref) * (ya + skip * za)
            new_b = conv3(pg_ref, 1, r0, cw_ref, cb_ref) * (yb + skip * zb)
            if o == 0:
                rows3(za_ref, i)[...] = new_a.reshape(tiles, n2, cb)
                rows3(zb_ref, i)[...] = new_b.reshape(tiles, n2, cb)
            else:
                out_ref[0, rows, :] = new_a.astype(out_ref.dtype)
                out_ref[1, rows, :] = new_b.astype(out_ref.dtype)
            return c

        lax.fori_loop(0, length // rc, gate, 0)


def _hyena(p_hy, conv_w, conv_b, skip, khat, tables, rc=512):
    b, _, s, _ = p_hy.shape
    w = HY_WIDTH
    cb = HY_CB
    ncb = w // cb
    n2 = FFT_N2
    n1 = 2 * s // n2
    g_sig, _, g_inv, f_fwd, f_inv = tables
    one = pl.Buffered(1)

    def cwspec(off):
        return pl.BlockSpec((3, cb), lambda c, p: (0, off + c))

    def cbspec(off):
        return pl.BlockSpec((1, cb), lambda c, p: (0, off + c))

    def full(a):
        nd = a.ndim
        return pl.BlockSpec(a.shape, lambda c, p: (0,) * nd, pipeline_mode=one)

    in_specs = [
        pl.BlockSpec((2, 1, s, cb), lambda c, p: (p, c, 0, 0)),
        pl.BlockSpec(memory_space=pl.ANY),
        cwspec(0), cwspec(ncb), cwspec(2 * ncb),
        cbspec(0), cbspec(ncb), cbspec(2 * ncb),
        pl.BlockSpec((2, cb), lambda c, p: (0, c)),
        pl.BlockSpec((khat.shape[0], 1, 2 * n1 * n2, cb), lambda c, p: (0, c, 0, 0), pipeline_mode=one),
        full(g_sig), full(g_inv), full(f_fwd), full(f_inv),
    ]
    return pl.pallas_call(
        functools.partial(_hyena_body, length=s, rc=rc),
        grid=(ncb, b // 2),
        in_specs=in_specs,
        out_specs=pl.BlockSpec((2, s, cb), lambda c, p: (p, 0, c)),
        out_shape=jax.ShapeDtypeStruct((b, s, w), BF16),
        scratch_shapes=[pltpu.VMEM((n1 // 2, n2 + FFT_PAD, cb), F32)] * 4
        + [pltpu.VMEM((2 * n1, n2 + FFT_PAD, cb), F32)]
        + [pltpu.VMEM((2, 1, s, cb), F32)] * 2 + [pltpu.SemaphoreType.DMA((2,))],
        compiler_params=_cparams(("arbitrary", "arbitrary")),
        name="hyena",
    )(p_hy, p_hy, conv_w, conv_w, conv_w, conv_b, conv_b, conv_b,
      skip, khat, g_sig, g_inv, f_fwd, f_inv)


def _attn_body(q_ref, k_ref, v_ref, o_ref, lse_ref, *, n_side, qb):
    lr = q_ref.shape[2]
    kw = qb + 2 * n_side
    nh = HEADS_PER_GROUP
    lane_head = lax.broadcasted_iota(jnp.int32, (qb, ATTN_OUT), 1) // HEAD_DIM
    qi = lax.broadcasted_iota(jnp.int32, (nh * qb, kw), 0) % qb
    ki = lax.broadcasted_iota(jnp.int32, (nh * qb, kw), 1)
    band = [jnp.where(jnp.abs(ki - qi - shift) <= n_side, 0.0, -1e30) for shift in (0, n_side, 2 * n_side)]

    def body(i, c):
        q0 = pl.multiple_of(i * qb, qb)
        w0 = pl.multiple_of(jnp.clip(q0 - n_side, 0, lr - kw), n_side)
        q = q_ref[0, 0, pl.ds(q0, qb), :]
        kwin = k_ref[0, 0, pl.ds(w0, kw), :]
        vwin = v_ref[0, 0, pl.ds(w0, kw), :]
        shift = q0 - w0
        bias = jnp.where(shift == n_side, band[1], jnp.where(shift == 0, band[0], band[2]))
        qs = jnp.concatenate([jnp.where(lane_head == h, q, jnp.zeros_like(q)) for h in range(nh)], axis=0)
        s = _dot_t(qs, kwin) + bias
        m = jnp.max(s, axis=-1, keepdims=True)
        p = jnp.exp(s - m)
        l = jnp.sum(p, axis=-1, keepdims=True)
        pv = _dot(p.astype(BF16), vwin) / l
        lse_rows = m + jnp.log(l)
        o = jnp.zeros((qb, ATTN_OUT), F32)
        lse = jnp.zeros((qb, ATTN_OUT), F32)
        for h in range(nh):
            mine = lane_head == h
            o = jnp.where(mine, pv[h * qb:(h + 1) * qb], o)
            lse = jnp.where(mine, lse_rows[h * qb:(h + 1) * qb], lse)
        o_ref[0, pl.ds(q0, qb), :] = o
        lse_ref[0, pl.ds(q0, qb), :] = lse
        return c

    lax.fori_loop(0, lr // qb, body, 0, unroll=2)


def _attn_group(q, k, v, g, n_side, qb=128):
    b, dil, lr, _ = q.shape
    in_spec = pl.BlockSpec((1, 1, lr, ATTN_OUT), lambda bi, r: (bi, r, 0, 0))
    out_spec = pl.BlockSpec((1, lr, ATTN_OUT), lambda bi, r: (bi, 0, r))
    o, lse = pl.pallas_call(
        functools.partial(_attn_body, n_side=n_side, qb=qb),
        grid=(b, dil),
        in_specs=[in_spec, in_spec, in_spec],
        out_specs=[out_spec, out_spec],
        out_shape=[jax.ShapeDtypeStruct((b, lr, dil * ATTN_OUT), F32)] * 2,
        compiler_params=_cparams(("arbitrary", "arbitrary")),
        name=f"attn{g}",
    )(q, k, v)
    return o, lse


def _memkv_body(mem_ref, g_ref, w_ref, kv_ref):
    mn = _rms(mem_ref[0], g_ref[...]).astype(BF16)
    kv_ref[0] = _dot(mn, w_ref[...]).astype(BF16)


def _memkv(mem, g, w_kv):
    b, m, d = mem.shape
    n = w_kv.shape[1]
    return pl.pallas_call(
        _memkv_body,
        grid=(b,),
        in_specs=[pl.BlockSpec((1, m, d), lambda i: (i, 0, 0)),
                  pl.BlockSpec((1, d), lambda i: (0, 0)),
                  pl.BlockSpec((d, n), lambda i: (0, 0))],
        out_specs=pl.BlockSpec((1, m, n), lambda i: (i, 0, 0)),
        out_shape=jax.ShapeDtypeStruct((b, m, n), BF16),
        compiler_params=_cparams(("arbitrary",)),
        name="memkv",
    )(mem, g, w_kv)


def _merge_body(x_ref, z_ref, o0_ref, o1_ref, o2_ref, l0_ref, l1_ref, l2_ref, gate_ref,
                wuh_ref, wua_ref, wout_ref, gx_ref, wq_ref, kv_ref, wo_ref, gm_ref, wr_ref,
                p1_ref, p2_ref, x2_ref, hm_ref, aff_ref, *, nsplit):
    rows = x_ref.shape[0] // nsplit
    for part in range(nsplit):
        sl = slice(part * rows, (part + 1) * rows)

        def natural(ref, perm_ref):
            dil = ref.shape[2] // ATTN_OUT
            rr = rows // dil
            blk = ref[0, part * rr:(part + 1) * rr, :]
            if dil == 1:
                return blk
            pieces = [blk[:, r * ATTN_OUT:(r + 1) * ATTN_OUT] for r in range(dil)]
            if rr % SUBLANES == 0 and dil % SUBLANES == 0:
                return jnp.swapaxes(jnp.stack(pieces, axis=0), 0, 1).reshape(rows, ATTN_OUT)
            hi, lo = _split(jnp.concatenate(pieces, axis=0))
            return _dot(perm_ref[...], hi) + _dot(perm_ref[...], lo)

        outs = [natural(r, p) for r, p in ((o0_ref, None), (o1_ref, p1_ref), (o2_ref, p2_ref))]
        lses = [natural(r, p) for r, p in ((l0_ref, None), (l1_ref, p1_ref), (l2_ref, p2_ref))]
        _merge_rows(sl, outs, lses, x_ref, z_ref, gate_ref,
                    wuh_ref, wua_ref, wout_ref, gx_ref, wq_ref, kv_ref, wo_ref, gm_ref, wr_ref,
                    x2_ref, hm_ref, aff_ref)


def _merge_rows(sl, outs, lses, x_ref, z_ref, gate_ref,
                wuh_ref, wua_ref, wout_ref, gx_ref, wq_ref, kv_ref, wo_ref, gm_ref, wr_ref,
                x2_ref, hm_ref, aff_ref):
    d = x_ref.shape[1]
    l0, l1, l2 = lses
    mx = jnp.maximum(jnp.maximum(l0, l1), l2)
    e0, e1, e2 = jnp.exp(l0 - mx), jnp.exp(l1 - mx), jnp.exp(l2 - mx)
    attn = (e0 * outs[0] + e1 * outs[1] + e2 * outs[2]) / (e0 + e1 + e2)
    y_hy = _dot(z_ref[sl, :].astype(BF16), wuh_ref[...])
    y_at = _dot(attn.astype(BF16), wua_ref[...])
    gates = gate_ref[sl, :].astype(F32)
    mix = gates[:, :d] * y_hy + gates[:, d:] * y_at
    x1 = x_ref[sl, :] + _dot(mix.astype(BF16), wout_ref[...])

    hx = _rms(x1, gx_ref[...]).astype(BF16)
    qx = _dot(hx, wq_ref[...]).astype(BF16)
    kv = kv_ref[0]
    xw = X_HEADS * X_HEAD_DIM
    heads = []
    for h in range(X_HEADS):
        hc = slice(h * X_HEAD_DIM, (h + 1) * X_HEAD_DIM)
        s = _dot_t(qx[:, hc], kv[:, hc]) * (X_HEAD_DIM ** -0.5)
        m = jnp.max(s, axis=-1, keepdims=True)
        p = jnp.exp(s - m)
        p = p / jnp.sum(p, axis=-1, keepdims=True)
        heads.append(_dot(p.astype(BF16), kv[:, xw + h * X_HEAD_DIM:xw + (h + 1) * X_HEAD_DIM]))
    ox = jnp.concatenate(heads, axis=-1).astype(BF16)
    x2 = x1 + _dot(ox, wo_ref[...])
    x2_ref[sl, :] = x2

    hm = _rms(x2, gm_ref[...])
    hm_hi, hm_lo = _split(hm)
    hm_ref[sl, :] = hm_hi
    nr = hm.shape[0]
    cross = _dot(jnp.concatenate([hm_hi, hm_lo], axis=0), wr_ref[...])
    logits = (cross[:nr, :LANES] + cross[:nr, LANES:]) + (cross[nr:, :LANES] + cross[nr:, LANES:])
    lane = lax.broadcasted_iota(jnp.int32, logits.shape, 1)
    logits = jnp.where(lane < N_EXPERTS, logits, -1e30)
    m = jnp.max(logits, axis=-1, keepdims=True)
    p = jnp.exp(logits - m)
    aff_ref[sl, :] = p / jnp.sum(p, axis=-1, keepdims=True)


def _merge(x2d, z2d, outs, lses, gates, wuh, wua, wout, gx, wq, kv, wo, gm, wr_cat, seq, tm=512, nsplit=2):
    n, d = x2d.shape
    nseq = seq // tm
    row = lambda i: (i, 0)
    const = lambda i: (0, 0)

    def rspec(a):
        return pl.BlockSpec((tm, a.shape[1]), row)

    def aspec(a):
        dil = a.shape[2] // ATTN_OUT
        return pl.BlockSpec((1, tm // dil, a.shape[2]), lambda i: (i // nseq, i % nseq, 0))

    def cspec(a):
        return pl.BlockSpec(a.shape, const)

    rows = tm // nsplit
    perms = []
    for a in outs[1:]:
        dil = a.shape[2] // ATTN_OUT
        src = (np.arange(rows) % dil) * (rows // dil) + np.arange(rows) // dil
        perms.append(jnp.asarray(np.eye(rows)[src], dtype=BF16))

    in_arrays = [x2d, z2d, *outs, *lses, gates, wuh, wua, wout, gx, wq, kv, wo, gm, wr_cat, *perms]
    in_specs = ([rspec(a) for a in in_arrays[:2]] + [aspec(a) for a in in_arrays[2:8]] + [rspec(gates)]
                + [cspec(a) for a in in_arrays[9:14]]
                + [pl.BlockSpec((1,) + kv.shape[1:], lambda i: (i // nseq, 0, 0))]
                + [cspec(a) for a in in_arrays[15:]])
    return pl.pallas_call(
        functools.partial(_merge_body, nsplit=nsplit),
        grid=(n // tm,),
        in_specs=in_specs,
        out_specs=[pl.BlockSpec((tm, d), row), pl.BlockSpec((tm, d), row), pl.BlockSpec((tm, LANES), row)],
        out_shape=[jax.ShapeDtypeStruct((n, d), F32), jax.ShapeDtypeStruct((n, d), BF16),
                   jax.ShapeDtypeStruct((n, LANES), F32)],
        compiler_params=_cparams(("arbitrary",)),
        name="merge",
    )(*in_arrays)


def _topk_body(aff_ref, tri_ref, rank_ref, bounds_ref, *, cap, tok_block):
    a = aff_ref[0]
    e, s = a.shape

    def count(mask):
        return jnp.sum(jnp.where(mask, 1.0, 0.0), axis=-1, keepdims=True)

    def as_float(bits):
        return pltpu.bitcast(jnp.broadcast_to(bits, (e, LANES)), F32)[:, 0:1]

    def search(i, thr):
        cand = thr | (jnp.int32(1) << (30 - i))
        return jnp.where(count(a >= as_float(cand)) >= cap, cand, thr)

    thr = as_float(lax.fori_loop(0, 31, search, jnp.zeros((e, 1), jnp.int32)))
    gt = a > thr
    eq = a == thr
    need = cap - count(gt)

    def prefix_excl(mask):
        mf = jnp.where(mask, 1.0, 0.0)
        parts = []
        carry = jnp.zeros((e, 1), F32)
        for c in range(s // LANES):
            blk = mf[:, c * LANES:(c + 1) * LANES]
            inc = _dot(blk.astype(BF16), tri_ref[...])
            parts.append(inc - blk + carry)
            carry = carry + inc[:, LANES - 1:LANES]
        return jnp.concatenate(parts, axis=-1)

    sel = gt | (eq & (prefix_excl(eq) < need))
    excl = prefix_excl(sel)
    rank_ref[0] = jnp.where(sel, excl, -1.0)

    tok = lax.broadcasted_iota(jnp.int32, (e, s), 1)
    lane = lax.broadcasted_iota(jnp.int32, (e, LANES), 1)
    bounds = jnp.zeros((e, LANES), F32)
    for j in range(s // tok_block + 1):
        bounds = jnp.where(lane == j, count(sel & (tok < j * tok_block)), bounds)
    bounds_ref[0] = bounds.astype(jnp.int32)


def _topk(aff_t, cap, tok_block):
    b, e, s = aff_t.shape
    assert s // tok_block + 1 <= LANES
    tri = jnp.asarray(np.triu(np.ones((LANES, LANES))), dtype=BF16)
    return pl.pallas_call(
        functools.partial(_topk_body, cap=cap, tok_block=tok_block),
        grid=(b,),
        in_specs=[pl.BlockSpec((1, e, s), lambda i: (i, 0, 0)),
                  pl.BlockSpec((LANES, LANES), lambda i: (0, 0))],
        out_specs=[pl.BlockSpec((1, e, s), lambda i: (i, 0, 0)),
                   pl.BlockSpec((1, e, LANES), lambda i: (i, 0, 0))],
        out_shape=[jax.ShapeDtypeStruct((b, e, s), F32),
                   jax.ShapeDtypeStruct((b, e, LANES), jnp.int32)],
        compiler_params=_cparams(("arbitrary",)),
        name="topk",
    )(aff_t, tri)


def _moe_windows(bounds_ref, base, ne, step, tokens, window):
    per = tokens // MOE_BOUND_STEP
    starts = []
    passes = jnp.int32(0)
    for e in range(ne):
        r_lo = bounds_ref[base + e * LANES + step * per]
        r_hi = bounds_ref[base + e * LANES + (step + 1) * per]
        ws = (r_lo // BF16_ROWS) * BF16_ROWS
        starts.append(ws)
        passes = jnp.maximum(passes, (r_hi - ws + window - 1) // window)
    return starts, passes


def _gather_body(bounds_ref, rank_ref, hm_ref, affh_ref, affl_ref, xe_ref, gs_ref, *, cap, eg):
    w = GATHER_WINDOW
    ne, tc = rank_ref.shape[1], rank_ref.shape[2]
    bi = pl.program_id(0)
    ci = pl.program_id(1)

    @pl.when(ci == 0)
    def _():
        xe_ref[...] = jnp.zeros_like(xe_ref)
        gs_ref[...] = jnp.zeros_like(gs_ref)

    starts, passes = _moe_windows(bounds_ref, bi * ne * LANES, ne, ci, tc, w)
    iota_w = lax.broadcasted_iota(jnp.int32, (w, 1), 0)

    def one_pass(p, carry):
        for g0 in range(0, ne, eg):
            lo = [starts[e] + p * w for e in range(g0, g0 + eg)]
            phys = [pl.multiple_of(jnp.minimum(v, cap - w), BF16_ROWS) for v in lo]
            rk = jnp.concatenate([jnp.broadcast_to(rank_ref[0, e:e + 1, :], (w, tc))
                                  for e in range(g0, g0 + eg)], axis=0)
            slot = jnp.concatenate([iota_w + v for v in phys], axis=0).astype(F32)
            lom = jnp.concatenate([jnp.zeros_like(iota_w) + v for v in lo], axis=0).astype(F32)
            onehot = jnp.where((rk == slot) & (slot >= lom), 1.0, 0.0).astype(BF16)
            res = _dot(onehot, hm_ref[0])
            resg = _dot(onehot, affh_ref[0]) + _dot(onehot, affl_ref[0])
            for k, e in enumerate(range(g0, g0 + eg)):
                rows = pl.ds(phys[k], w)
                xe_ref[0, e, rows, :] = (xe_ref[0, e, rows, :].astype(F32) + res[k * w:(k + 1) * w]).astype(BF16)
                gs_ref[0, e, rows, :] = gs_ref[0, e, rows, :] + resg[k * w:(k + 1) * w]
        return carry

    lax.fori_loop(0, passes, one_pass, 0)


def _gather(bounds, rank, hm, aff_hi, aff_lo, cap, eg=16):
    b, e, s = rank.shape
    d = hm.shape[2]
    tc = GATHER_TOKENS
    return pl.pallas_call(
        functools.partial(_gather_body, cap=cap, eg=eg),
        grid_spec=pltpu.PrefetchScalarGridSpec(
            num_scalar_prefetch=1,
            grid=(b, s // tc),
            in_specs=[pl.BlockSpec((1, e, tc), lambda bi, ci, bnd: (bi, 0, ci)),
                      pl.BlockSpec((1, tc, d), lambda bi, ci, bnd: (bi, ci, 0)),
                      pl.BlockSpec((1, tc, LANES), lambda bi, ci, bnd: (bi, ci, 0)),
                      pl.BlockSpec((1, tc, LANES), lambda bi, ci, bnd: (bi, ci, 0))],
            out_specs=[pl.BlockSpec((1, e, cap, d), lambda bi, ci, bnd: (bi, 0, 0, 0)),
                       pl.BlockSpec((1, e, cap, LANES), lambda bi, ci, bnd: (bi, 0, 0, 0))],
        ),
        out_shape=[jax.ShapeDtypeStruct((b, e, cap, d), BF16),
                   jax.ShapeDtypeStruct((b, e, cap, LANES), F32)],
        compiler_params=_cparams(("arbitrary", "arbitrary")),
        name="gather",
    )(bounds.reshape(-1), rank, hm, aff_hi, aff_lo)


def _ffn_body(xe_ref, gs_ref, wg_ref, wu_ref, wd_ref, ye_ref, acc_ref, *, mb):
    f = pl.program_id(1)
    nb, _, cap, d = xe_ref.shape
    wg = wg_ref[0].astype(BF16)
    wu = wu_ref[0].astype(BF16)
    wd = wd_ref[0].astype(BF16)

    @pl.when(f == 0)
    def _():
        acc_ref[...] = jnp.zeros_like(acc_ref)

    for b0 in range(0, nb, mb):
        xe = xe_ref[b0:b0 + mb, 0].reshape(mb * cap, d)
        a = _dot(xe, wg)
        u = _dot(xe, wu)
        hsw = (a * jax.nn.sigmoid(a) * u).astype(BF16)
        acc_ref[b0:b0 + mb] += _dot(hsw, wd).reshape(mb, cap, d)

    @pl.when(f == pl.num_programs(1) - 1)
    def _():
        lane = lax.broadcasted_iota(jnp.int32, gs_ref.shape[2:], 1)
        mine = lane == pl.program_id(0)
        for b in range(nb):
            gate = jnp.sum(jnp.where(mine, gs_ref[b, 0], 0.0), axis=-1, keepdims=True)
            ye_ref[b, 0] = (acc_ref[b] * gate).astype(BF16)


def _ffn(xe, gs, wg, wu, wd, ft=1024, mb=1):
    b, e, cap, d = xe.shape
    ff = wg.shape[2]
    one = pl.Buffered(1)
    return pl.pallas_call(
        functools.partial(_ffn_body, mb=mb),
        grid=(e, ff // ft),
        in_specs=[pl.BlockSpec((b, 1, cap, d), lambda ei, fi: (0, ei, 0, 0), pipeline_mode=one),
                  pl.BlockSpec((b, 1, cap, LANES), lambda ei, fi: (0, ei, 0, 0), pipeline_mode=one),
                  pl.BlockSpec((1, d, ft), lambda ei, fi: (ei, 0, fi)),
                  pl.BlockSpec((1, d, ft), lambda ei, fi: (ei, 0, fi)),
                  pl.BlockSpec((1, ft, d), lambda ei, fi: (ei, fi, 0))],
        out_specs=pl.BlockSpec((b, 1, cap, d), lambda ei, fi: (0, ei, 0, 0)),
        out_shape=jax.ShapeDtypeStruct((b, e, cap, d), BF16),
        scratch_shapes=[pltpu.VMEM((b, cap, d), F32)],
        compiler_params=_cparams(("arbitrary", "arbitrary")),
        name="ffn",
    )(xe, gs, wg, wu, wd)


def _scatter_body(bounds_ref, rank_ref, ye_ref, x2_ref, g_ref, out_ref, acc_ref, *, eg):
    w = SCATTER_WINDOW
    ts = x2_ref.shape[1]
    ne, cap = ye_ref.shape[1], ye_ref.shape[2]
    starts, passes = _moe_windows(bounds_ref, pl.program_id(0) * ne * LANES, ne, pl.program_id(1), ts, w)
    rank = rank_ref[0]
    acc_ref[...] = x2_ref[0]
    iota_w = lax.broadcasted_iota(jnp.int32, (1, w), 1)

    def one_pass(p, carry):
        for g0 in range(0, ne, eg):
            onehots, rows = [], []
            for e in range(g0, g0 + eg):
                lo = starts[e] + p * w
                phys = pl.multiple_of(jnp.minimum(lo, cap - w), BF16_ROWS)
                slot = (iota_w + phys).astype(F32)
                hit = (rank[:, e:e + 1] == slot) & (slot >= lo.astype(F32))
                onehots.append(jnp.where(hit, 1.0, 0.0).astype(BF16))
                rows.append(ye_ref[0, e, pl.ds(phys, w), :])
            acc_ref[...] += _dot(jnp.concatenate(onehots, axis=1), jnp.concatenate(rows, axis=0))
        return carry

    lax.fori_loop(0, passes, one_pass, 0)
    out_ref[0] = _rms(acc_ref[...], g_ref[...])


def _scatter(bounds, rank_t, ye, x2, g, eg=8):
    b, s, e = rank_t.shape
    cap, d = ye.shape[2], ye.shape[3]
    ts = SCATTER_TOKENS
    return pl.pallas_call(
        functools.partial(_scatter_body, eg=eg),
        grid_spec=pltpu.PrefetchScalarGridSpec(
            num_scalar_prefetch=1,
            grid=(b, s // ts),
            in_specs=[pl.BlockSpec((1, ts, e), lambda bi, ti, bnd: (bi, ti, 0)),
                      pl.BlockSpec((1, e, cap, d), lambda bi, ti, bnd: (bi, 0, 0, 0)),
                      pl.BlockSpec((1, ts, d), lambda bi, ti, bnd: (bi, ti, 0)),
                      pl.BlockSpec((1, d), lambda bi, ti, bnd: (0, 0))],
            out_specs=pl.BlockSpec((1, ts, d), lambda bi, ti, bnd: (bi, ti, 0)),
            scratch_shapes=[pltpu.VMEM((ts, d), F32)],
        ),
        out_shape=jax.ShapeDtypeStruct((b, s, d), F32),
        compiler_params=_cparams(("arbitrary", "arbitrary")),
        name="scatter",
    )(bounds.reshape(-1), rank_t, ye, x2, g)


def kernel(x, mem, ln_mix_g, w_in, hy_conv_w, hy_conv_b, filt_w1, filt_b1, filt_freq1, filt_w2, filt_b2,
           filt_freq2, filt_w3, hy_skip, w_up_hy, w_up_attn, w_gate, b_gate, w_out, ln_x_g, ln_mem_g,
           w_q_x, w_kv_mem, w_o_x, ln_moe_g, w_router, w_e_gate, w_e_up, w_e_down, ln_f_g):
    b, s, d = x.shape
    n = b * s
    x2d = x.reshape(n, d)

    half = HEAD_DIM // 2
    inv = ROPE_THETA ** (-jnp.arange(0, HEAD_DIM, 2, dtype=F32) / HEAD_DIM)
    ang = jnp.arange(s, dtype=F32)[:, None] * inv[None, :]
    lane = np.arange(LANES)
    cos_t = jnp.cos(ang)[:, lane % half]
    sin_t = jnp.sin(ang)[:, lane % half] * jnp.asarray(np.where(lane % HEAD_DIM < half, -1.0, 1.0), F32)[None, :]

    w_all = jnp.concatenate([w_in, w_gate], axis=1).astype(BF16)
    p_hy, *qkv, gates = _proj(x2d, ln_mix_g[None], w_all, b_gate[None], cos_t, sin_t, b, s)

    t_col = jnp.linspace(0.0, 1.0, s, dtype=F32)[:, None]
    grid_col = 2.0 * math.pi * jnp.arange(s, dtype=F32)[:, None] / s
    bands = jnp.linspace(1e-4, HY_BANDS - 1, HY_BANDS, dtype=F32)[None, :]
    feat = _filt(t_col, grid_col, bands, filt_w1, filt_b1, filt_freq1, filt_w2, filt_b2, filt_freq2)
    delta = jnp.linspace(math.log(HY_TARGET) / HY_SLOW_PCT, math.log(HY_TARGET) / HY_FAST_PCT,
                         HY_WIDTH, dtype=F32)[None, :]
    tables = _fft_tables(s)
    khat = _hyfilt(t_col, delta, feat, filt_w3, tables[1], tables[3])
    z = _hyena(p_hy, hy_conv_w, hy_conv_b[None], hy_skip, khat, tables)

    outs, lses = [], []
    for g, (window, dil) in enumerate(DIL_PAIRS):
        o_g, l_g = _attn_group(*qkv[3 * g:3 * g + 3], g, window // (2 * dil))
        outs.append(o_g)
        lses.append(l_g)

    kv = _memkv(mem, ln_mem_g[None], w_kv_mem.astype(BF16))
    wr_pad = jnp.pad(w_router, ((0, 0), (0, LANES - N_EXPERTS)))
    wr_cat = jnp.concatenate(_split(wr_pad), axis=1)
    x2, hm, aff = _merge(x2d, z.reshape(n, HY_WIDTH), outs, lses, gates,
                         w_up_hy.astype(BF16), w_up_attn.astype(BF16), w_out.astype(BF16),
                         ln_x_g[None], w_q_x.astype(BF16), kv, w_o_x.astype(BF16), ln_moe_g[None],
                         wr_cat, s)

    cap = max(1, EC_FACTOR * s // N_EXPERTS)
    aff3 = aff.reshape(b, s, LANES)
    aff_hi = aff3.astype(BF16)
    aff_lo = (aff3 - aff_hi.astype(F32)).astype(BF16)
    rank, bounds = _topk(aff3[:, :, :N_EXPERTS].transpose(0, 2, 1), cap, MOE_BOUND_STEP)
    xe, gs = _gather(bounds, rank, hm.reshape(b, s, d), aff_hi, aff_lo, cap)
    ye = _ffn(xe, gs, w_e_gate, w_e_up, w_e_down)
    return _scatter(bounds, rank.transpose(0, 2, 1), ye, x2.reshape(b, s, d), ln_f_g[None])
```

```python
import functools
import math

import numpy as np
import jax
import jax.numpy as jnp
from jax import lax
from jax.experimental import pallas as pl
from jax.experimental.pallas import tpu as pltpu

F32 = jnp.float32
BF16 = jnp.bfloat16
HIGHEST = lax.Precision.HIGHEST

EPS = 1e-6
HY_WIDTH = 768
HY_BANDS = 16
HY_FFN = 64
HY_FAST_PCT = 0.3
HY_SLOW_PCT = 1.5
HY_TARGET = 1e-2
HEAD_DIM = 64
HEADS_PER_GROUP = 4
DIL_PAIRS = ((128, 1), (512, 4), (2048, 16))
N_GROUPS = len(DIL_PAIRS)
ATTN_WIDTH = N_GROUPS * HEADS_PER_GROUP * HEAD_DIM
ATTN_OUT = HEADS_PER_GROUP * HEAD_DIM
ROPE_THETA = 10000.0
X_HEADS = 4
X_HEAD_DIM = 128
N_EXPERTS = 16
EC_FACTOR = 2

LANES = 128
SUBLANES = 8
MXU_COLS = 256
VMEM_LIMIT = 56 * 1024 * 1024
VMEM_LIMIT_LARGE = 60 * 1024 * 1024

FFT_N2 = 128
HY_CB = 128
FFT_UNROLL = 8
FFT_PAD = 8
MOE_BOUND_STEP = 256
GATHER_TOKENS, GATHER_WINDOW = 256, 64
SCATTER_TOKENS, SCATTER_WINDOW = 512, 128
BF16_ROWS = 16


def _cparams(sem, vmem=VMEM_LIMIT):
    return pltpu.CompilerParams(dimension_semantics=sem, vmem_limit_bytes=vmem)


def _rms(x, g):
    return x * lax.rsqrt(jnp.mean(x * x, axis=-1, keepdims=True) + EPS) * g


def _dot(a, b):
    return jnp.dot(a, b, preferred_element_type=F32)


def _dot_hi(a, b):
    return jnp.dot(a, b, precision=HIGHEST, preferred_element_type=F32)


def _split(a):
    hi = a.astype(BF16)
    return hi, (a - hi.astype(F32)).astype(BF16)


def _dot_split(a, b):
    a_hi, a_lo = _split(a)
    b_hi, b_lo = _split(b)
    return _dot(a_hi, b_hi) + _dot(a_lo, b_hi) + _dot(a_hi, b_lo)


def _dot_t(a, b):
    return lax.dot_general(a, b, (((1,), (1,)), ((), ())), preferred_element_type=F32)


def _proj_body(x_ref, g_ref, w_ref, bg_ref, cos_ref, sin_ref, *rest, hyc, aw):
    ng = N_GROUPS
    perm_refs = (None,) + rest[:ng - 1]
    phy_ref = rest[ng - 1]
    qkv_refs = rest[ng:ng + 3 * ng]
    gate_ref = rest[ng + 3 * ng]
    h = _rms(x_ref[...], g_ref[...]).astype(BF16)
    phy = _dot(h, w_ref[:, :hyc])
    for j in range(hyc // HY_CB):
        phy_ref[0, j] = phy[:, j * HY_CB:(j + 1) * HY_CB]
    cos = cos_ref[...]
    sin = sin_ref[...]
    tm = cos.shape[0]
    lane = lax.broadcasted_iota(jnp.int32, (tm, LANES), 1)
    first = (lane % HEAD_DIM) < (HEAD_DIM // 2)

    def rope(t, scale):
        chunks = []
        for j in range(aw // LANES):
            tj = t[:, j * LANES:(j + 1) * LANES]
            partner = jnp.where(first, pltpu.roll(tj, LANES - HEAD_DIM // 2, 1),
                                pltpu.roll(tj, HEAD_DIM // 2, 1))
            chunks.append(((tj * cos + partner * sin) * scale).astype(BF16))
        return chunks

    gl = ATTN_OUT // LANES
    q = rope(_dot(h, w_ref[:, hyc:hyc + aw]), HEAD_DIM ** -0.5)
    k = rope(_dot(h, w_ref[:, hyc + aw:hyc + 2 * aw]), 1.0)
    vf = _dot(h, w_ref[:, hyc + 2 * aw:hyc + 3 * aw]).astype(BF16)
    v = [vf[:, j * LANES:(j + 1) * LANES] for j in range(aw // LANES)]
    for g in range(ng):
        for i, t in enumerate((q, k, v)):
            out_ref = qkv_refs[3 * g + i]
            tg = jnp.concatenate(t[g * gl:(g + 1) * gl], axis=1)
            if perm_refs[g] is not None:
                tg = _dot(perm_refs[g][...], tg).astype(BF16)
            dil = out_ref.shape[1]
            rows = tm // dil
            for r in range(dil):
                out_ref[0, r] = tg[r * rows:(r + 1) * rows, :]
    gate_ref[...] = jax.nn.sigmoid(_dot(h, w_ref[:, hyc + 3 * aw:]) + bg_ref[...]).astype(BF16)


def _proj(x2d, g, w_all, b_gate, cos_t, sin_t, batch, seq, tm=256):
    n, d = x2d.shape
    hyc = 3 * HY_WIDTH
    aw = ATTN_WIDTH
    gd = w_all.shape[1] - hyc - 3 * aw
    nseq = seq // tm
    row = lambda i: (i, 0)
    const = lambda i: (0, 0)
    perms = []
    for _, dil in DIL_PAIRS[1:]:
        rows = tm // dil
        src = (np.arange(tm) % rows) * dil + np.arange(tm) // rows
        perms.append(jnp.asarray(np.eye(tm)[src], dtype=BF16))
    qkv_specs, qkv_shapes = [], []
    for _, dil in DIL_PAIRS:
        for _ in range(3):
            qkv_specs.append(pl.BlockSpec((1, dil, tm // dil, ATTN_OUT), lambda i: (i // nseq, 0, i % nseq, 0)))
            qkv_shapes.append(jax.ShapeDtypeStruct((batch, dil, seq // dil, ATTN_OUT), BF16))
    return pl.pallas_call(
        functools.partial(_proj_body, hyc=hyc, aw=aw),
        grid=(n // tm,),
        in_specs=[
            pl.BlockSpec((tm, d), row),
            pl.BlockSpec((1, d), const),
            pl.BlockSpec(w_all.shape, const, pipeline_mode=pl.Buffered(1)),
            pl.BlockSpec((1, gd), const),
            pl.BlockSpec((tm, LANES), lambda i: (i % nseq, 0)),
            pl.BlockSpec((tm, LANES), lambda i: (i % nseq, 0)),
        ] + [pl.BlockSpec((tm, tm), const) for _ in perms],
        out_specs=([pl.BlockSpec((1, hyc // HY_CB, tm, HY_CB), lambda i: (i // nseq, 0, i % nseq, 0))]
                   + qkv_specs + [pl.BlockSpec((tm, gd), row)]),
        out_shape=([jax.ShapeDtypeStruct((batch, hyc // HY_CB, seq, HY_CB), F32)] + qkv_shapes
                   + [jax.ShapeDtypeStruct((n, gd), BF16)]),
        compiler_params=_cparams(("arbitrary",)),
        name="proj",
    )(x2d, g, w_all, b_gate, cos_t, sin_t, *perms)


def _filt_body(t_ref, grid_ref, bands_ref, w1t_ref, w1c_ref, w1s_ref, b1_ref, f1_ref,
               w2_ref, b2_ref, f2_ref, feat_ref):
    ang = bands_ref[...] * grid_ref[...]
    pre = (t_ref[...] * w1t_ref[...] + _dot_hi(jnp.cos(ang), w1c_ref[...])
           + _dot_hi(-jnp.sin(ang), w1s_ref[...]))
    h = jnp.sin(f1_ref[...] * (pre + b1_ref[...]))
    feat_ref[...] = jnp.sin(f2_ref[...] * (_dot_hi(h, w2_ref[...]) + b2_ref[...]))


def _filt(t_col, grid_col, bands, w1, b1, f1, w2, b2, f2):
    length = t_col.shape[0]
    nb = bands.shape[1]
    args = (t_col, grid_col, bands, w1[0:1], w1[1:1 + nb], w1[1 + nb:], b1[None], f1[None],
            w2, b2[None], f2[None])
    return pl.pallas_call(
        _filt_body,
        out_shape=jax.ShapeDtypeStruct((length, HY_FFN), F32),
        compiler_params=_cparams(None),
        name="filt",
    )(*args)


def _fft_tables(length):
    n = 2 * length
    n2 = FFT_N2
    n1 = n // n2
    h1 = n1 // 2
    k1 = np.arange(n1)[:, None]
    g_sig = np.zeros((n2, 2 * n1, 2 * h1))
    g_flt = np.zeros((n2, 2 * n1, 2 * h1))
    g_inv = np.zeros((n2, 2 * h1, 2 * n1))
    j = np.arange(h1)[None, :]
    for r in range(n2):
        th = 2 * np.pi * (k1 * (n2 * j + r) % n) / n
        gr, gi = np.cos(th), -np.sin(th)
        g_sig[r, 0::2, :h1] = gr
        g_sig[r, 0::2, h1:] = -gi
        g_sig[r, 1::2, :h1] = gi
        g_sig[r, 1::2, h1:] = gr
        g_flt[r, 0::2, :h1] = gr
        g_flt[r, 1::2, :h1] = gi
        m = n2 * (j + 1) - r
        thb = 2 * np.pi * (k1 * m % n) / n
        live = (m < length).astype(np.float64)
        g_flt[r, 0::2, h1:] = np.cos(thb) * live
        g_flt[r, 1::2, h1:] = np.sin(thb) * live
        wr, wi = (np.cos(th) / n).T, (np.sin(th) / n).T
        g_inv[r, :h1, 0::2] = wr
        g_inv[r, :h1, 1::2] = -wi
        g_inv[r, h1:, 0::2] = wi
        g_inv[r, h1:, 1::2] = wr
    a = np.arange(n2)
    th2 = 2 * np.pi * (np.outer(a, a) % n2) / n2
    c2, s2 = np.cos(th2), np.sin(th2)
    f_fwd = np.block([[c2, s2], [-s2, c2]])
    f_inv = np.block([[c2, -s2], [s2, c2]])
    cast = lambda z: jnp.asarray(z, dtype=F32).astype(BF16)
    return cast(g_sig), cast(g_flt), cast(g_inv), cast(f_fwd), cast(f_inv)


def _fft_stage1(src_a_ref, src_b_ref, tab_ref, work_ref, n1, is_filter):
    n2 = FFT_N2
    grp = SUBLANES

    def body(i, c):
        r0 = pl.multiple_of(i * grp, grp)
        b0 = pl.multiple_of(n2 - grp - r0, grp) if is_filter else r0
        a = jnp.swapaxes(src_a_ref[:, pl.ds(r0, grp), :], 0, 1)
        b = jnp.swapaxes(src_b_ref[:, pl.ds(b0, grp), :], 0, 1)
        outs = []
        for s in range(grp):
            rhs = jnp.concatenate([a[s], b[grp - 1 - s] if is_filter else b[s]], axis=0).astype(BF16)
            outs.append(_dot(tab_ref[r0 + s], rhs))
        work_ref[:, pl.ds(r0, grp), :] = jnp.swapaxes(jnp.stack(outs, axis=0), 0, 1)
        return c

    lax.fori_loop(0, n2 // grp, body, 0, unroll=2)


def _hyfilt_body(t_ref, ts_ref, delta_ref, feat_ref, feats_ref, w3f_ref, w3b_ref, gflt_ref, ffwd_ref, khat_ref,
                 hf_ref, hb_ref, work_ref, *, length, rc):
    n2 = FFT_N2
    n1 = 2 * length // n2
    tiles = rc // n2
    cb = hf_ref.shape[2]

    def gen(i, c):
        r0 = pl.multiple_of(i * rc, rc)
        j0 = pl.multiple_of(i * tiles, tiles)
        delta = jnp.abs(delta_ref[...])
        hf = _dot_split(feat_ref[pl.ds(r0, rc), :], w3f_ref[...]) * jnp.exp(-t_ref[pl.ds(r0, rc), :] * delta)
        hb = _dot_split(feats_ref[pl.ds(r0, rc), :], w3b_ref[...]) * jnp.exp(-ts_ref[pl.ds(r0, rc), :] * delta)
        hf_ref[pl.ds(j0, tiles), :n2, :] = hf.reshape(tiles, n2, cb)
        hb_ref[pl.ds(j0, tiles), :n2, :] = hb.reshape(tiles, n2, cb)
        return c

    lax.fori_loop(0, length // rc, gen, 0)
    _fft_stage1(hf_ref, hb_ref, gflt_ref, work_ref, n1, True)

    side = MXU_COLS // cb

    def stage2(i, c):
        k1s = [i * side + v for v in range(side)]
        blks = [work_ref[pl.ds(pl.multiple_of(2 * k1, 2), 2), :n2, :].reshape(2 * n2, cb).astype(BF16)
                for k1 in k1s]
        x = _dot(ffwd_ref[...], jnp.concatenate(blks, axis=1)).astype(BF16)
        for v, k1 in enumerate(k1s):
            khat_ref[0, 0, pl.ds(pl.multiple_of(k1 * (2 * n2), 2 * n2), 2 * n2), :] = x[:, v * cb:(v + 1) * cb]
        return c

    lax.fori_loop(0, n1 // side, stage2, 0, unroll=FFT_UNROLL // side)


def _hyfilt(t_col, delta, feat, w3, g_flt, f_fwd, rc=512):
    s = t_col.shape[0]
    cb = HY_CB
    ncb = HY_WIDTH // cb
    n2 = FFT_N2
    n1 = 2 * s // n2
    norder = w3.shape[1] // (2 * HY_WIDTH)
    one = pl.Buffered(1)

    def full(a):
        nd = a.ndim
        return pl.BlockSpec(a.shape, lambda o, c: (0,) * nd, pipeline_mode=one)

    shift = lambda a: jnp.concatenate([a[1:], jnp.zeros_like(a[:1])], axis=0)
    return pl.pallas_call(
        functools.partial(_hyfilt_body, length=s, rc=rc),
        grid=(norder, ncb),
        in_specs=[full(t_col), full(t_col),
                  pl.BlockSpec((1, cb), lambda o, c: (0, c)),
                  full(feat), full(feat),
                  pl.BlockSpec((HY_FFN, cb), lambda o, c: (0, 2 * ncb * o + c)),
                  pl.BlockSpec((HY_FFN, cb), lambda o, c: (0, 2 * ncb * o + ncb + c)),
                  full(g_flt), full(f_fwd)],
        out_specs=pl.BlockSpec((1, 1, 2 * n1 * n2, cb), lambda o, c: (o, c, 0, 0)),
        out_shape=jax.ShapeDtypeStruct((norder, ncb, 2 * n1 * n2, cb), BF16),
        scratch_shapes=[pltpu.VMEM((n1 // 2, n2 + FFT_PAD, cb), F32)] * 2
        + [pltpu.VMEM((2 * n1, n2 + FFT_PAD, cb), F32)],
        compiler_params=_cparams(("arbitrary", "arbitrary")),
        name="hyfilt",
    )(t_col, shift(t_col), delta, feat, shift(feat), w3, w3, g_flt, f_fwd)


def _hyena_body(pz_ref, phy_hbm, cwz_ref, cwg1_ref, cwg2_ref, cbz_ref, cbg1_ref, cbg2_ref,
                skip_ref, khat_hbm, gsig_ref, ginv_ref, ffwd_ref, finv_ref,
                out_ref,
                za_ref, zb_ref, ya_ref, yb_ref, work_ref, pg1_ref, pg2_ref, khat_ref, sem, ksem, *, length, rc):
    ncb = pl.num_programs(0)
    pair = pl.ds(2 * pl.program_id(1), 2)

    def gate_copy(order, dst_ref):
        src = phy_hbm.at[pair, pl.ds((order + 1) * ncb + pl.program_id(0), 1)]
        return pltpu.make_async_copy(src, dst_ref, sem.at[order])

    gate_copies = [gate_copy(0, pg1_ref), gate_copy(1, pg2_ref)]
    for cp in gate_copies:
        cp.start()

    new_block = pl.program_id(1) == 0
    spec_copies = [pltpu.make_async_copy(khat_hbm.at[o, pl.program_id(0)], khat_ref.at[o], ksem.at[o])
                   for o in range(khat_ref.shape[0])]

    @pl.when(new_block)
    def _():
        for cp in spec_copies:
            cp.start()
    n2 = FFT_N2
    n1 = 2 * length // n2
    h1 = n1 // 2
    tiles = rc // n2
    cb = za_ref.shape[2]
    side = MXU_COLS // cb
    sub = SUBLANES

    def conv3(p_ref, b, r0, w_ref, bias_ref):
        p = p_ref[b, 0, pl.ds(r0, rc), :]
        before = p_ref[b, 0, pl.ds(pl.multiple_of(jnp.maximum(r0 - sub, 0), sub), sub), :][sub - 1:sub, :]
        after = p_ref[b, 0, pl.ds(pl.multiple_of(jnp.minimum(r0 + rc, length - sub), sub), sub), :][0:1, :]
        before = jnp.where(r0 == 0, 0.0, before)
        after = jnp.where(r0 + rc == length, 0.0, after)
        row = lax.broadcasted_iota(jnp.int32, p.shape, 0)
        prev = jnp.where(row == 0, before, pltpu.roll(p, 1, 0))
        nxt = jnp.where(row == rc - 1, after, pltpu.roll(p, rc - 1, 0))
        return prev * w_ref[0:1, :] + p * w_ref[1:2, :] + nxt * w_ref[2:3, :] + bias_ref[...]

    def stage2(o):
        def body(i, c):
            k1s = [i * FFT_UNROLL + u for u in range(FFT_UNROLL)]
            blks = [work_ref[pl.ds(pl.multiple_of(2 * k1, 2), 2), :n2, :].reshape(2 * n2, cb).astype(BF16)
                    for k1 in k1s]
            outs = []
            for u in range(0, FFT_UNROLL, side):
                x = _dot(ffwd_ref[...], jnp.concatenate(blks[u:u + side], axis=1))
                ys = []
                for v in range(side):
                    k1 = k1s[u + v]
                    kh = khat_ref[o, pl.ds(pl.multiple_of(k1 * (2 * n2), 2 * n2), 2 * n2), :].astype(F32)
                    xr, xi = x[:n2, v * cb:(v + 1) * cb], x[n2:, v * cb:(v + 1) * cb]
                    kr, ki = kh[:n2], kh[n2:]
                    ys.append(jnp.concatenate([xr * kr - xi * ki, xr * ki + xi * kr], axis=0).astype(BF16))
                out = _dot(finv_ref[...], jnp.concatenate(ys, axis=1))
                outs.extend(out[:, v * cb:(v + 1) * cb] for v in range(side))
            for k1, out in zip(k1s, outs):
                work_ref[pl.ds(pl.multiple_of(2 * k1, 2), 2), :n2, :] = out.reshape(2, n2, cb)
            return c
        lax.fori_loop(0, n1 // FFT_UNROLL, body, 0)

    def stage3():
        grp = SUBLANES

        def body(i, c):
            r0 = pl.multiple_of(i * grp, grp)
            wk = jnp.swapaxes(work_ref[:, pl.ds(r0, grp), :], 0, 1)
            res = jnp.stack([_dot(ginv_ref[r0 + s], wk[s].astype(BF16)) for s in range(grp)], axis=0)
            ya_ref[:, pl.ds(r0, grp), :] = jnp.swapaxes(res[:, :h1], 0, 1)
            yb_ref[:, pl.ds(r0, grp), :] = jnp.swapaxes(res[:, h1:], 0, 1)
            return c
        lax.fori_loop(0, n2 // grp, body, 0, unroll=2)

    def rows3(ref, i):
        return ref.at[pl.ds(pl.multiple_of(i * tiles, tiles), tiles), :n2, :]

    def load_z(i, c):
        r0 = pl.multiple_of(i * rc, rc)
        rows3(za_ref, i)[...] = conv3(pz_ref, 0, r0, cwz_ref, cbz_ref).reshape(tiles, n2, cb)
        rows3(zb_ref, i)[...] = conv3(pz_ref, 1, r0, cwz_ref, cbz_ref).reshape(tiles, n2, cb)
        return c

    lax.fori_loop(0, length // rc, load_z, 0)
    for o, (pg_ref, cw_ref, cb_ref) in enumerate(((pg1_ref, cwg1_ref, cbg1_ref), (pg2_ref, cwg2_ref, cbg2_ref))):
        _fft_stage1(za_ref, zb_ref, gsig_ref, work_ref, n1, False)

        @pl.when(new_block)
        def _():
            spec_copies[o].wait()

        stage2(o)
        stage3()
        skip = skip_ref[o:o + 1, :]
        gate_copies[o].wait()

        def gate(i, c):
            r0 = pl.multiple_of(i * rc, rc)
            rows = pl.ds(r0, rc)
            ya, yb = (rows3(r, i)[...].reshape(rc, cb) for r in (ya_ref, yb_ref))
            za, zb = (rows3(r, i)[...].reshape(rc, cb) for r in (za_ref, zb_ref))
            new_a = conv3(pg_ref, 0, r0, cw_ref, cb_ref) * (ya + skip * za)
            new_b = conv3(pg_ref, 1, r0, cw_ref, cb_ref) * (yb + skip * zb)
            if o == 0:
                rows3(za_ref, i)[...] = new_a.reshape(tiles, n2, cb)
                rows3(zb_ref, i)[...] = new_b.reshape(tiles, n2, cb)
            else:
                out_ref[0, rows, :] = new_a.astype(out_ref.dtype)
                out_ref[1, rows, :] = new_b.astype(out_ref.dtype)
            return c

        lax.fori_loop(0, length // rc, gate, 0)


def _hyena(p_hy, conv_w, conv_b, skip, khat, tables, rc=512):
    b, _, s, _ = p_hy.shape
    w = HY_WIDTH
    cb = HY_CB
    ncb = w // cb
    n2 = FFT_N2
    n1 = 2 * s // n2
    g_sig, _, g_inv, f_fwd, f_inv = tables
    one = pl.Buffered(1)

    def cwspec(off):
        return pl.BlockSpec((3, cb), lambda c, p: (0, off + c))

    def cbspec(off):
        return pl.BlockSpec((1, cb), lambda c, p: (0, off + c))

    def full(a):
        nd = a.ndim
        return pl.BlockSpec(a.shape, lambda c, p: (0,) * nd, pipeline_mode=one)

    in_specs = [
        pl.BlockSpec((2, 1, s, cb), lambda c, p: (p, c, 0, 0)),
        pl.BlockSpec(memory_space=pl.ANY),
        cwspec(0), cwspec(ncb), cwspec(2 * ncb),
        cbspec(0), cbspec(ncb), cbspec(2 * ncb),
        pl.BlockSpec((2, cb), lambda c, p: (0, c)),
        pl.BlockSpec(memory_space=pl.ANY),
        full(g_sig), full(g_inv), full(f_fwd), full(f_inv),
    ]
    return pl.pallas_call(
        functools.partial(_hyena_body, length=s, rc=rc),
        grid=(ncb, b // 2),
        in_specs=in_specs,
        out_specs=pl.BlockSpec((2, s, cb), lambda c, p: (p, 0, c)),
        out_shape=jax.ShapeDtypeStruct((b, s, w), BF16),
        scratch_shapes=[pltpu.VMEM((n1 // 2, n2 + FFT_PAD, cb), F32)] * 4
        + [pltpu.VMEM((2 * n1, n2 + FFT_PAD, cb), F32)]
        + [pltpu.VMEM((2, 1, s, cb), F32)] * 2 + [pltpu.VMEM((khat.shape[0], 2 * n1 * n2, cb), BF16)]
        + [pltpu.SemaphoreType.DMA((2,)), pltpu.SemaphoreType.DMA((khat.shape[0],))],
        compiler_params=_cparams(("arbitrary", "arbitrary")),
        name="hyena",
    )(p_hy, p_hy, conv_w, conv_w, conv_w, conv_b, conv_b, conv_b,
      skip, khat, g_sig, g_inv, f_fwd, f_inv)


def _attn_body(q_ref, k_ref, v_ref, o_ref, lse_ref, *, n_side, qb):
    lr = q_ref.shape[2]
    kw = qb + 2 * n_side
    nh = HEADS_PER_GROUP
    lane_head = lax.broadcasted_iota(jnp.int32, (qb, ATTN_OUT), 1) // HEAD_DIM
    qi = lax.broadcasted_iota(jnp.int32, (nh * qb, kw), 0) % qb
    ki = lax.broadcasted_iota(jnp.int32, (nh * qb, kw), 1)
    band = [jnp.where(jnp.abs(ki - qi - shift) <= n_side, 0.0, -1e30) for shift in (0, n_side, 2 * n_side)]

    def body(i, c):
        q0 = pl.multiple_of(i * qb, qb)
        w0 = pl.multiple_of(jnp.clip(q0 - n_side, 0, lr - kw), n_side)
        q = q_ref[0, 0, pl.ds(q0, qb), :]
        kwin = k_ref[0, 0, pl.ds(w0, kw), :]
        vwin = v_ref[0, 0, pl.ds(w0, kw), :]
        shift = q0 - w0
        bias = jnp.where(shift == n_side, band[1], jnp.where(shift == 0, band[0], band[2]))
        qs = jnp.concatenate([jnp.where(lane_head == h, q, jnp.zeros_like(q)) for h in range(nh)], axis=0)
        s = _dot_t(qs, kwin) + bias
        m = jnp.max(s, axis=-1, keepdims=True)
        p = jnp.exp(s - m)
        l = jnp.sum(p, axis=-1, keepdims=True)
        pv = _dot(p.astype(BF16), vwin) / l
        lse_rows = m + jnp.log(l)
        o = jnp.zeros((qb, ATTN_OUT), F32)
        lse = jnp.zeros((qb, ATTN_OUT), F32)
        for h in range(nh):
            mine = lane_head == h
            o = jnp.where(mine, pv[h * qb:(h + 1) * qb], o)
            lse = jnp.where(mine, lse_rows[h * qb:(h + 1) * qb], lse)
        o_ref[0, pl.ds(q0, qb), :] = o
        lse_ref[0, pl.ds(q0, qb), :] = lse
        return c

    lax.fori_loop(0, lr // qb, body, 0, unroll=2)


def _attn_group(q, k, v, g, n_side, qb=128):
    b, dil, lr, _ = q.shape
    in_spec = pl.BlockSpec((1, 1, lr, ATTN_OUT), lambda bi, r: (bi, r, 0, 0))
    out_spec = pl.BlockSpec((1, lr, ATTN_OUT), lambda bi, r: (bi, 0, r))
    o, lse = pl.pallas_call(
        functools.partial(_attn_body, n_side=n_side, qb=qb),
        grid=(b, dil),
        in_specs=[in_spec, in_spec, in_spec],
        out_specs=[out_spec, out_spec],
        out_shape=[jax.ShapeDtypeStruct((b, lr, dil * ATTN_OUT), F32)] * 2,
        compiler_params=_cparams(("arbitrary", "arbitrary")),
        name=f"attn{g}",
    )(q, k, v)
    return o, lse


def _memkv_body(mem_ref, g_ref, w_ref, kv_ref):
    mn = _rms(mem_ref[0], g_ref[...]).astype(BF16)
    kv_ref[0] = _dot(mn, w_ref[...]).astype(BF16)


def _memkv(mem, g, w_kv):
    b, m, d = mem.shape
    n = w_kv.shape[1]
    return pl.pallas_call(
        _memkv_body,
        grid=(b,),
        in_specs=[pl.BlockSpec((1, m, d), lambda i: (i, 0, 0)),
                  pl.BlockSpec((1, d), lambda i: (0, 0)),
                  pl.BlockSpec((d, n), lambda i: (0, 0))],
        out_specs=pl.BlockSpec((1, m, n), lambda i: (i, 0, 0)),
        out_shape=jax.ShapeDtypeStruct((b, m, n), BF16),
        compiler_params=_cparams(("arbitrary",)),
        name="memkv",
    )(mem, g, w_kv)


def _merge_body(x_ref, z_ref, o0_ref, o1_ref, o2_ref, l0_ref, l1_ref, l2_ref, gate_ref,
                wuh_ref, wua_ref, wout_ref, gx_ref, wq_ref, kv_ref, wo_ref, gm_ref, wr_ref,
                p1_ref, p2_ref, x2_ref, hm_ref, aff_ref, affh_ref, affl_ref, *, nsplit):
    rows = x_ref.shape[0] // nsplit
    for part in range(nsplit):
        sl = slice(part * rows, (part + 1) * rows)

        def natural(ref, perm_ref):
            dil = ref.shape[2] // ATTN_OUT
            rr = rows // dil
            blk = ref[0, part * rr:(part + 1) * rr, :]
            if dil == 1:
                return blk
            pieces = [blk[:, r * ATTN_OUT:(r + 1) * ATTN_OUT] for r in range(dil)]
            if rr % SUBLANES == 0 and dil % SUBLANES == 0:
                return jnp.swapaxes(jnp.stack(pieces, axis=0), 0, 1).reshape(rows, ATTN_OUT)
            hi, lo = _split(jnp.concatenate(pieces, axis=0))
            return _dot(perm_ref[...], hi) + _dot(perm_ref[...], lo)

        outs = [natural(r, p) for r, p in ((o0_ref, None), (o1_ref, p1_ref), (o2_ref, p2_ref))]
        lses = [natural(r, p) for r, p in ((l0_ref, None), (l1_ref, p1_ref), (l2_ref, p2_ref))]
        _merge_rows(sl, outs, lses, x_ref, z_ref, gate_ref,
                    wuh_ref, wua_ref, wout_ref, gx_ref, wq_ref, kv_ref, wo_ref, gm_ref, wr_ref,
                    x2_ref, hm_ref, aff_ref, affh_ref, affl_ref)


def _merge_rows(sl, outs, lses, x_ref, z_ref, gate_ref,
                wuh_ref, wua_ref, wout_ref, gx_ref, wq_ref, kv_ref, wo_ref, gm_ref, wr_ref,
                x2_ref, hm_ref, aff_ref, affh_ref, affl_ref):
    d = x_ref.shape[1]
    l0, l1, l2 = lses
    mx = jnp.maximum(jnp.maximum(l0, l1), l2)
    e0, e1, e2 = jnp.exp(l0 - mx), jnp.exp(l1 - mx), jnp.exp(l2 - mx)
    attn = (e0 * outs[0] + e1 * outs[1] + e2 * outs[2]) / (e0 + e1 + e2)
    y_hy = _dot(z_ref[sl, :].astype(BF16), wuh_ref[...])
    y_at = _dot(attn.astype(BF16), wua_ref[...])
    gates = gate_ref[sl, :].astype(F32)
    mix = gates[:, :d] * y_hy + gates[:, d:] * y_at
    x1 = x_ref[sl, :] + _dot(mix.astype(BF16), wout_ref[...])

    hx = _rms(x1, gx_ref[...]).astype(BF16)
    qx = _dot(hx, wq_ref[...]).astype(BF16)
    kv = kv_ref[0]
    xw = X_HEADS * X_HEAD_DIM
    heads = []
    for h in range(X_HEADS):
        hc = slice(h * X_HEAD_DIM, (h + 1) * X_HEAD_DIM)
        s = _dot_t(qx[:, hc], kv[:, hc]) * (X_HEAD_DIM ** -0.5)
        m = jnp.max(s, axis=-1, keepdims=True)
        p = jnp.exp(s - m)
        p = p / jnp.sum(p, axis=-1, keepdims=True)
        heads.append(_dot(p.astype(BF16), kv[:, xw + h * X_HEAD_DIM:xw + (h + 1) * X_HEAD_DIM]))
    ox = jnp.concatenate(heads, axis=-1).astype(BF16)
    x2 = x1 + _dot(ox, wo_ref[...])
    x2_ref[sl, :] = x2

    hm = _rms(x2, gm_ref[...])
    hm_hi, hm_lo = _split(hm)
    hm_ref[sl, :] = hm_hi
    nr = hm.shape[0]
    cross = _dot(jnp.concatenate([hm_hi, hm_lo], axis=0), wr_ref[...])
    logits = (cross[:nr, :LANES] + cross[:nr, LANES:]) + (cross[nr:, :LANES] + cross[nr:, LANES:])
    lane = lax.broadcasted_iota(jnp.int32, logits.shape, 1)
    logits = jnp.where(lane < N_EXPERTS, logits, -1e30)
    m = jnp.max(logits, axis=-1, keepdims=True)
    p = jnp.exp(logits - m)
    aff = p / jnp.sum(p, axis=-1, keepdims=True)
    aff_ref[sl, :] = aff
    affh_ref[sl, :], affl_ref[sl, :] = _split(aff)


def _merge(x2d, z2d, outs, lses, gates, wuh, wua, wout, gx, wq, kv, wo, gm, wr_cat, seq, tm=512, nsplit=2):
    n, d = x2d.shape
    nseq = seq // tm
    row = lambda i: (i, 0)
    const = lambda i: (0, 0)

    def rspec(a):
        return pl.BlockSpec((tm, a.shape[1]), row)

    def aspec(a):
        dil = a.shape[2] // ATTN_OUT
        return pl.BlockSpec((1, tm // dil, a.shape[2]), lambda i: (i // nseq, i % nseq, 0))

    def cspec(a):
        return pl.BlockSpec(a.shape, const)

    rows = tm // nsplit
    perms = []
    for a in outs[1:]:
        dil = a.shape[2] // ATTN_OUT
        src = (np.arange(rows) % dil) * (rows // dil) + np.arange(rows) // dil
        perms.append(jnp.asarray(np.eye(rows)[src], dtype=BF16))

    in_arrays = [x2d, z2d, *outs, *lses, gates, wuh, wua, wout, gx, wq, kv, wo, gm, wr_cat, *perms]
    in_specs = ([rspec(a) for a in in_arrays[:2]] + [aspec(a) for a in in_arrays[2:8]] + [rspec(gates)]
                + [cspec(a) for a in in_arrays[9:14]]
                + [pl.BlockSpec((1,) + kv.shape[1:], lambda i: (i // nseq, 0, 0))]
                + [cspec(a) for a in in_arrays[15:]])
    return pl.pallas_call(
        functools.partial(_merge_body, nsplit=nsplit),
        grid=(n // tm,),
        in_specs=in_specs,
        out_specs=[pl.BlockSpec((tm, d), row), pl.BlockSpec((tm, d), row)] + [pl.BlockSpec((tm, LANES), row)] * 3,
        out_shape=[jax.ShapeDtypeStruct((n, d), F32), jax.ShapeDtypeStruct((n, d), BF16),
                   jax.ShapeDtypeStruct((n, LANES), F32), jax.ShapeDtypeStruct((n, LANES), BF16),
                   jax.ShapeDtypeStruct((n, LANES), BF16)],
        compiler_params=_cparams(("arbitrary",)),
        name="merge",
    )(*in_arrays)


def _topk_body(aff_ref, tri_ref, rank_ref, bounds_ref, *, cap, tok_block):
    a = aff_ref[0]
    e, s = a.shape

    def count(mask):
        return jnp.sum(jnp.where(mask, 1.0, 0.0), axis=-1, keepdims=True)

    def as_float(bits):
        return pltpu.bitcast(jnp.broadcast_to(bits, (e, LANES)), F32)[:, 0:1]

    def search(i, thr):
        cand = thr | (jnp.int32(1) << (30 - i))
        return jnp.where(count(a >= as_float(cand)) >= cap, cand, thr)

    thr = as_float(lax.fori_loop(0, 31, search, jnp.zeros((e, 1), jnp.int32)))
    gt = a > thr
    eq = a == thr
    need = cap - count(gt)

    def prefix_excl(mask):
        mf = jnp.where(mask, 1.0, 0.0)
        parts = []
        carry = jnp.zeros((e, 1), F32)
        for c in range(s // LANES):
            blk = mf[:, c * LANES:(c + 1) * LANES]
            inc = _dot(blk.astype(BF16), tri_ref[...])
            parts.append(inc - blk + carry)
            carry = carry + inc[:, LANES - 1:LANES]
        return jnp.concatenate(parts, axis=-1)

    sel = gt | (eq & (prefix_excl(eq) < need))
    excl = prefix_excl(sel)
    rank_ref[0] = jnp.where(sel, excl, -1.0)

    tok = lax.broadcasted_iota(jnp.int32, (e, s), 1)
    lane = lax.broadcasted_iota(jnp.int32, (e, LANES), 1)
    bounds = jnp.zeros((e, LANES), F32)
    for j in range(s // tok_block + 1):
        bounds = jnp.where(lane == j, count(sel & (tok < j * tok_block)), bounds)
    bounds_ref[0] = bounds.astype(jnp.int32)


def _topk(aff_t, cap, tok_block):
    b, e, s = aff_t.shape
    assert s // tok_block + 1 <= LANES
    tri = jnp.asarray(np.triu(np.ones((LANES, LANES))), dtype=BF16)
    return pl.pallas_call(
        functools.partial(_topk_body, cap=cap, tok_block=tok_block),
        grid=(b,),
        in_specs=[pl.BlockSpec((1, e, s), lambda i: (i, 0, 0)),
                  pl.BlockSpec((LANES, LANES), lambda i: (0, 0))],
        out_specs=[pl.BlockSpec((1, e, s), lambda i: (i, 0, 0)),
                   pl.BlockSpec((1, e, LANES), lambda i: (i, 0, 0))],
        out_shape=[jax.ShapeDtypeStruct((b, e, s), F32),
                   jax.ShapeDtypeStruct((b, e, LANES), jnp.int32)],
        compiler_params=_cparams(("arbitrary",)),
        name="topk",
    )(aff_t, tri)


def _moe_windows(bounds_ref, base, ne, step, tokens, window):
    per = tokens // MOE_BOUND_STEP
    starts = []
    passes = jnp.int32(0)
    for e in range(ne):
        r_lo = bounds_ref[base + e * LANES + step * per]
        r_hi = bounds_ref[base + e * LANES + (step + 1) * per]
        ws = (r_lo // BF16_ROWS) * BF16_ROWS
        starts.append(ws)
        passes = jnp.maximum(passes, (r_hi - ws + window - 1) // window)
    return starts, passes


def _gather_body(bounds_ref, rank_ref, hm_ref, affh_ref, affl_ref, xe_ref, gs_ref, *, cap, eg):
    w = GATHER_WINDOW
    ne, tc = rank_ref.shape[1], rank_ref.shape[2]
    bi = pl.program_id(0)
    ci = pl.program_id(1)

    @pl.when(ci == 0)
    def _():
        xe_ref[...] = jnp.zeros_like(xe_ref)
        gs_ref[...] = jnp.zeros_like(gs_ref)

    starts, passes = _moe_windows(bounds_ref, bi * ne * LANES, ne, ci, tc, w)
    iota_w = lax.broadcasted_iota(jnp.int32, (w, 1), 0)

    def one_pass(p, carry):
        for g0 in range(0, ne, eg):
            lo = [starts[e] + p * w for e in range(g0, g0 + eg)]
            phys = [pl.multiple_of(jnp.minimum(v, cap - w), BF16_ROWS) for v in lo]
            rk = jnp.concatenate([jnp.broadcast_to(rank_ref[0, e:e + 1, :], (w, tc))
                                  for e in range(g0, g0 + eg)], axis=0)
            slot = jnp.concatenate([iota_w + v for v in phys], axis=0).astype(F32)
            lom = jnp.concatenate([jnp.zeros_like(iota_w) + v for v in lo], axis=0).astype(F32)
            onehot = jnp.where((rk == slot) & (slot >= lom), 1.0, 0.0).astype(BF16)
            res = _dot(onehot, hm_ref[0])
            resg = _dot(onehot, affh_ref[0]) + _dot(onehot, affl_ref[0])
            for k, e in enumerate(range(g0, g0 + eg)):
                rows = pl.ds(phys[k], w)
                xe_ref[0, e, rows, :] = (xe_ref[0, e, rows, :].astype(F32) + res[k * w:(k + 1) * w]).astype(BF16)
                gs_ref[0, e, rows, :] = gs_ref[0, e, rows, :] + resg[k * w:(k + 1) * w]
        return carry

    lax.fori_loop(0, passes, one_pass, 0)


def _gather(bounds, rank, hm, aff_hi, aff_lo, cap, eg=16):
    b, e, s = rank.shape
    d = hm.shape[2]
    tc = GATHER_TOKENS
    return pl.pallas_call(
        functools.partial(_gather_body, cap=cap, eg=eg),
        grid_spec=pltpu.PrefetchScalarGridSpec(
            num_scalar_prefetch=1,
            grid=(b, s // tc),
            in_specs=[pl.BlockSpec((1, e, tc), lambda bi, ci, bnd: (bi, 0, ci)),
                      pl.BlockSpec((1, tc, d), lambda bi, ci, bnd: (bi, ci, 0)),
                      pl.BlockSpec((1, tc, LANES), lambda bi, ci, bnd: (bi, ci, 0)),
                      pl.BlockSpec((1, tc, LANES), lambda bi, ci, bnd: (bi, ci, 0))],
            out_specs=[pl.BlockSpec((1, e, cap, d), lambda bi, ci, bnd: (bi, 0, 0, 0)),
                       pl.BlockSpec((1, e, cap, LANES), lambda bi, ci, bnd: (bi, 0, 0, 0))],
        ),
        out_shape=[jax.ShapeDtypeStruct((b, e, cap, d), BF16),
                   jax.ShapeDtypeStruct((b, e, cap, LANES), F32)],
        compiler_params=_cparams(("arbitrary", "arbitrary")),
        name="gather",
    )(bounds.reshape(-1), rank, hm, aff_hi, aff_lo)


def _ffn_body(xe_ref, gs_ref, wg_ref, wu_ref, wd_ref, ye_ref, acc_ref, *, mb):
    f = pl.program_id(1)
    nb, _, cap, d = xe_ref.shape
    wg = wg_ref[0].astype(BF16)
    wu = wu_ref[0].astype(BF16)
    wd = wd_ref[0].astype(BF16)

    @pl.when(f == 0)
    def _():
        acc_ref[...] = jnp.zeros_like(acc_ref)

    for b0 in range(0, nb, mb):
        xe = xe_ref[b0:b0 + mb, 0].reshape(mb * cap, d)
        a = _dot(xe, wg)
        u = _dot(xe, wu)
        hsw = (a * jax.nn.sigmoid(a) * u).astype(BF16)
        acc_ref[b0:b0 + mb] += _dot(hsw, wd).reshape(mb, cap, d)

    @pl.when(f == pl.num_programs(1) - 1)
    def _():
        lane = lax.broadcasted_iota(jnp.int32, gs_ref.shape[2:], 1)
        mine = lane == pl.program_id(0)
        for b in range(nb):
            gate = jnp.sum(jnp.where(mine, gs_ref[b, 0], 0.0), axis=-1, keepdims=True)
            ye_ref[b, 0] = (acc_ref[b] * gate).astype(BF16)


def _ffn(xe, gs, wg, wu, wd, ft=1024, mb=1):
    b, e, cap, d = xe.shape
    ff = wg.shape[2]
    return pl.pallas_call(
        functools.partial(_ffn_body, mb=mb),
        grid=(e, ff // ft),
        in_specs=[pl.BlockSpec((b, 1, cap, d), lambda ei, fi: (0, ei, 0, 0)),
                  pl.BlockSpec((b, 1, cap, LANES), lambda ei, fi: (0, ei, 0, 0)),
                  pl.BlockSpec((1, d, ft), lambda ei, fi: (ei, 0, fi)),
                  pl.BlockSpec((1, d, ft), lambda ei, fi: (ei, 0, fi)),
                  pl.BlockSpec((1, ft, d), lambda ei, fi: (ei, fi, 0))],
        out_specs=pl.BlockSpec((b, 1, cap, d), lambda ei, fi: (0, ei, 0, 0)),
        out_shape=jax.ShapeDtypeStruct((b, e, cap, d), BF16),
        scratch_shapes=[pltpu.VMEM((b, cap, d), F32)],
        compiler_params=_cparams(("arbitrary", "arbitrary"), vmem=VMEM_LIMIT_LARGE),
        name="ffn",
    )(xe, gs, wg, wu, wd)


def _scatter_body(bounds_ref, rank_ref, ye_ref, x2_ref, g_ref, out_ref, acc_ref, *, eg):
    w = SCATTER_WINDOW
    ts = x2_ref.shape[1]
    ne, cap = ye_ref.shape[1], ye_ref.shape[2]
    starts, passes = _moe_windows(bounds_ref, pl.program_id(0) * ne * LANES, ne, pl.program_id(1), ts, w)
    rank = rank_ref[0]
    acc_ref[...] = x2_ref[0]
    iota_w = lax.broadcasted_iota(jnp.int32, (1, w), 1)

    def one_pass(p, carry):
        for g0 in range(0, ne, eg):
            onehots, rows = [], []
            for e in range(g0, g0 + eg):
                lo = starts[e] + p * w
                phys = pl.multiple_of(jnp.minimum(lo, cap - w), BF16_ROWS)
                slot = (iota_w + phys).astype(F32)
                hit = (rank[:, e:e + 1] == slot) & (slot >= lo.astype(F32))
                onehots.append(jnp.where(hit, 1.0, 0.0).astype(BF16))
                rows.append(ye_ref[0, e, pl.ds(phys, w), :])
            acc_ref[...] += _dot(jnp.concatenate(onehots, axis=1), jnp.concatenate(rows, axis=0))
        return carry

    lax.fori_loop(0, passes, one_pass, 0)
    out_ref[0] = _rms(acc_ref[...], g_ref[...])


def _scatter(bounds, rank_t, ye, x2, g, eg=8):
    b, s, e = rank_t.shape
    cap, d = ye.shape[2], ye.shape[3]
    ts = SCATTER_TOKENS
    return pl.pallas_call(
        functools.partial(_scatter_body, eg=eg),
        grid_spec=pltpu.PrefetchScalarGridSpec(
            num_scalar_prefetch=1,
            grid=(b, s // ts),
            in_specs=[pl.BlockSpec((1, ts, e), lambda bi, ti, bnd: (bi, ti, 0)),
                      pl.BlockSpec((1, e, cap, d), lambda bi, ti, bnd: (bi, 0, 0, 0)),
                      pl.BlockSpec((1, ts, d), lambda bi, ti, bnd: (bi, ti, 0)),
                      pl.BlockSpec((1, d), lambda bi, ti, bnd: (0, 0))],
            out_specs=pl.BlockSpec((1, ts, d), lambda bi, ti, bnd: (bi, ti, 0)),
            scratch_shapes=[pltpu.VMEM((ts, d), F32)],
        ),
        out_shape=jax.ShapeDtypeStruct((b, s, d), F32),
        compiler_params=_cparams(("arbitrary", "arbitrary")),
        name="scatter",
    )(bounds.reshape(-1), rank_t, ye, x2, g)


def kernel(x, mem, ln_mix_g, w_in, hy_conv_w, hy_conv_b, filt_w1, filt_b1, filt_freq1, filt_w2, filt_b2,
           filt_freq2, filt_w3, hy_skip, w_up_hy, w_up_attn, w_gate, b_gate, w_out, ln_x_g, ln_mem_g,
           w_q_x, w_kv_mem, w_o_x, ln_moe_g, w_router, w_e_gate, w_e_up, w_e_down, ln_f_g):
    b, s, d = x.shape
    n = b * s
    x2d = x.reshape(n, d)

    half = HEAD_DIM // 2
    inv = ROPE_THETA ** (-jnp.arange(0, HEAD_DIM, 2, dtype=F32) / HEAD_DIM)
    ang = jnp.arange(s, dtype=F32)[:, None] * inv[None, :]
    lane = np.arange(LANES)
    cos_t = jnp.cos(ang)[:, lane % half]
    sin_t = jnp.sin(ang)[:, lane % half] * jnp.asarray(np.where(lane % HEAD_DIM < half, -1.0, 1.0), F32)[None, :]

    w_all = jnp.concatenate([w_in, w_gate], axis=1).astype(BF16)
    p_hy, *qkv, gates = _proj(x2d, ln_mix_g[None], w_all, b_gate[None], cos_t, sin_t, b, s)

    t_col = jnp.linspace(0.0, 1.0, s, dtype=F32)[:, None]
    grid_col = 2.0 * math.pi * jnp.arange(s, dtype=F32)[:, None] / s
    bands = jnp.linspace(1e-4, HY_BANDS - 1, HY_BANDS, dtype=F32)[None, :]
    feat = _filt(t_col, grid_col, bands, filt_w1, filt_b1, filt_freq1, filt_w2, filt_b2, filt_freq2)
    delta = jnp.linspace(math.log(HY_TARGET) / HY_SLOW_PCT, math.log(HY_TARGET) / HY_FAST_PCT,
                         HY_WIDTH, dtype=F32)[None, :]
    tables = _fft_tables(s)
    khat = _hyfilt(t_col, delta, feat, filt_w3, tables[1], tables[3])
    z = _hyena(p_hy, hy_conv_w, hy_conv_b[None], hy_skip, khat, tables)

    outs, lses = [], []
    for g, (window, dil) in enumerate(DIL_PAIRS):
        o_g, l_g = _attn_group(*qkv[3 * g:3 * g + 3], g, window // (2 * dil))
        outs.append(o_g)
        lses.append(l_g)

    kv = _memkv(mem, ln_mem_g[None], w_kv_mem.astype(BF16))
    wr_pad = jnp.pad(w_router, ((0, 0), (0, LANES - N_EXPERTS)))
    wr_cat = jnp.concatenate(_split(wr_pad), axis=1)
    x2, hm, aff, aff_hi, aff_lo = _merge(x2d, z.reshape(n, HY_WIDTH), outs, lses, gates,
                         w_up_hy.astype(BF16), w_up_attn.astype(BF16), w_out.astype(BF16),
                         ln_x_g[None], w_q_x.astype(BF16), kv, w_o_x.astype(BF16), ln_moe_g[None],
                         wr_cat, s)

    cap = max(1, EC_FACTOR * s // N_EXPERTS)
    aff3 = aff.reshape(b, s, LANES)
    rank, bounds = _topk(aff3[:, :, :N_EXPERTS].transpose(0, 2, 1), cap, MOE_BOUND_STEP)
    xe, gs = _gather(bounds, rank, hm.reshape(b, s, d), aff_hi.reshape(b, s, LANES), aff_lo.reshape(b, s, LANES), cap)
    ye = _ffn(xe, gs, w_e_gate, w_e_up, w_e_down)
    return _scatter(bounds, rank.transpose(0, 2, 1), ye, x2.reshape(b, s, d), ln_f_g[None])
```

```python
import functools
import math

import numpy as np
import jax
import jax.numpy as jnp
from jax import lax
from jax.experimental import pallas as pl
from jax.experimental.pallas import tpu as pltpu

F32 = jnp.float32
BF16 = jnp.bfloat16
HIGHEST = lax.Precision.HIGHEST

EPS = 1e-6
HY_WIDTH = 768
HY_BANDS = 16
HY_FFN = 64
HY_FAST_PCT = 0.3
HY_SLOW_PCT = 1.5
HY_TARGET = 1e-2
HEAD_DIM = 64
HEADS_PER_GROUP = 4
DIL_PAIRS = ((128, 1), (512, 4), (2048, 16))
N_GROUPS = len(DIL_PAIRS)
ATTN_WIDTH = N_GROUPS * HEADS_PER_GROUP * HEAD_DIM
ATTN_OUT = HEADS_PER_GROUP * HEAD_DIM
ROPE_THETA = 10000.0
X_HEADS = 4
X_HEAD_DIM = 128
N_EXPERTS = 16
EC_FACTOR = 2

LANES = 128
SUBLANES = 8
MXU_COLS = 256
VMEM_LIMIT = 56 * 1024 * 1024
VMEM_LIMIT_LARGE = 60 * 1024 * 1024

FFT_N2 = 128
HY_CB = 128
FFT_UNROLL = 8
FFT_PAD = 8
MOE_BOUND_STEP = 256
GATHER_TOKENS, GATHER_WINDOW = 256, 64
SCATTER_TOKENS, SCATTER_WINDOW = 512, 128
BF16_ROWS = 16


def _cparams(sem, vmem=VMEM_LIMIT):
    return pltpu.CompilerParams(dimension_semantics=sem, vmem_limit_bytes=vmem)


def _rms(x, g):
    return x * lax.rsqrt(jnp.mean(x * x, axis=-1, keepdims=True) + EPS) * g


def _dot(a, b):
    return jnp.dot(a, b, preferred_element_type=F32)


def _dot_hi(a, b):
    return jnp.dot(a, b, precision=HIGHEST, preferred_element_type=F32)


def _split(a):
    hi = a.astype(BF16)
    return hi, (a - hi.astype(F32)).astype(BF16)


def _dot_split(a, b):
    a_hi, a_lo = _split(a)
    b_hi, b_lo = _split(b)
    return _dot(a_hi, b_hi) + _dot(a_lo, b_hi) + _dot(a_hi, b_lo)


def _dot_t(a, b):
    return lax.dot_general(a, b, (((1,), (1,)), ((), ())), preferred_element_type=F32)


def _proj_body(x_ref, g_ref, w_ref, bg_ref, cos_ref, sin_ref, *rest, hyc, aw):
    ng = N_GROUPS
    perm_refs = (None,) + rest[:ng - 1]
    phy_ref = rest[ng - 1]
    qkv_refs = rest[ng:ng + 3 * ng]
    gate_ref = rest[ng + 3 * ng]
    h = _rms(x_ref[...], g_ref[...]).astype(BF16)
    phy = _dot(h, w_ref[:, :hyc])
    for j in range(hyc // HY_CB):
        phy_ref[0, j] = phy[:, j * HY_CB:(j + 1) * HY_CB]
    cos = cos_ref[...]
    sin = sin_ref[...]
    tm = cos.shape[0]
    lane = lax.broadcasted_iota(jnp.int32, (tm, LANES), 1)
    first = (lane % HEAD_DIM) < (HEAD_DIM // 2)

    def rope(t, scale):
        chunks = []
        for j in range(aw // LANES):
            tj = t[:, j * LANES:(j + 1) * LANES]
            partner = jnp.where(first, pltpu.roll(tj, LANES - HEAD_DIM // 2, 1),
                                pltpu.roll(tj, HEAD_DIM // 2, 1))
            chunks.append(((tj * cos + partner * sin) * scale).astype(BF16))
        return chunks

    gl = ATTN_OUT // LANES
    q = rope(_dot(h, w_ref[:, hyc:hyc + aw]), HEAD_DIM ** -0.5)
    k = rope(_dot(h, w_ref[:, hyc + aw:hyc + 2 * aw]), 1.0)
    vf = _dot(h, w_ref[:, hyc + 2 * aw:hyc + 3 * aw]).astype(BF16)
    v = [vf[:, j * LANES:(j + 1) * LANES] for j in range(aw // LANES)]
    for g in range(ng):
        for i, t in enumerate((q, k, v)):
            out_ref = qkv_refs[3 * g + i]
            tg = jnp.concatenate(t[g * gl:(g + 1) * gl], axis=1)
            if perm_refs[g] is not None:
                tg = _dot(perm_refs[g][...], tg).astype(BF16)
            dil = out_ref.shape[1]
            rows = tm // dil
            for r in range(dil):
                out_ref[0, r] = tg[r * rows:(r + 1) * rows, :]
    gate_ref[...] = jax.nn.sigmoid(_dot(h, w_ref[:, hyc + 3 * aw:]) + bg_ref[...]).astype(BF16)


def _proj(x2d, g, w_all, b_gate, cos_t, sin_t, batch, seq, tm=512):
    n, d = x2d.shape
    hyc = 3 * HY_WIDTH
    aw = ATTN_WIDTH
    gd = w_all.shape[1] - hyc - 3 * aw
    nseq = seq // tm
    row = lambda i: (i, 0)
    const = lambda i: (0, 0)
    perms = []
    for _, dil in DIL_PAIRS[1:]:
        rows = tm // dil
        src = (np.arange(tm) % rows) * dil + np.arange(tm) // rows
        perms.append(jnp.asarray(np.eye(tm)[src], dtype=BF16))
    qkv_specs, qkv_shapes = [], []
    for _, dil in DIL_PAIRS:
        for _ in range(3):
            qkv_specs.append(pl.BlockSpec((1, dil, tm // dil, ATTN_OUT), lambda i: (i // nseq, 0, i % nseq, 0)))
            qkv_shapes.append(jax.ShapeDtypeStruct((batch, dil, seq // dil, ATTN_OUT), BF16))
    return pl.pallas_call(
        functools.partial(_proj_body, hyc=hyc, aw=aw),
        grid=(n // tm,),
        in_specs=[
            pl.BlockSpec((tm, d), row),
            pl.BlockSpec((1, d), const),
            pl.BlockSpec(w_all.shape, const, pipeline_mode=pl.Buffered(1)),
            pl.BlockSpec((1, gd), const),
            pl.BlockSpec((tm, LANES), lambda i: (i % nseq, 0)),
            pl.BlockSpec((tm, LANES), lambda i: (i % nseq, 0)),
        ] + [pl.BlockSpec((tm, tm), const) for _ in perms],
        out_specs=([pl.BlockSpec((1, hyc // HY_CB, tm, HY_CB), lambda i: (i // nseq, 0, i % nseq, 0))]
                   + qkv_specs + [pl.BlockSpec((tm, gd), row)]),
        out_shape=([jax.ShapeDtypeStruct((batch, hyc // HY_CB, seq, HY_CB), F32)] + qkv_shapes
                   + [jax.ShapeDtypeStruct((n, gd), BF16)]),
        compiler_params=_cparams(("arbitrary",)),
        name="proj",
    )(x2d, g, w_all, b_gate, cos_t, sin_t, *perms)


def _filt_body(t_ref, grid_ref, bands_ref, w1t_ref, w1c_ref, w1s_ref, b1_ref, f1_ref,
               w2_ref, b2_ref, f2_ref, feat_ref):
    ang = bands_ref[...] * grid_ref[...]
    pre = (t_ref[...] * w1t_ref[...] + _dot_hi(jnp.cos(ang), w1c_ref[...])
           + _dot_hi(-jnp.sin(ang), w1s_ref[...]))
    h = jnp.sin(f1_ref[...] * (pre + b1_ref[...]))
    feat_ref[...] = jnp.sin(f2_ref[...] * (_dot_hi(h, w2_ref[...]) + b2_ref[...]))


def _filt(t_col, grid_col, bands, w1, b1, f1, w2, b2, f2):
    length = t_col.shape[0]
    nb = bands.shape[1]
    args = (t_col, grid_col, bands, w1[0:1], w1[1:1 + nb], w1[1 + nb:], b1[None], f1[None],
            w2, b2[None], f2[None])
    return pl.pallas_call(
        _filt_body,
        out_shape=jax.ShapeDtypeStruct((length, HY_FFN), F32),
        compiler_params=_cparams(None),
        name="filt",
    )(*args)


def _fft_tables(length):
    n = 2 * length
    n2 = FFT_N2
    n1 = n // n2
    h1 = n1 // 2
    k1 = np.arange(n1)[:, None]
    g_sig = np.zeros((n2, 2 * n1, 2 * h1))
    g_flt = np.zeros((n2, 2 * n1, 2 * h1))
    g_inv = np.zeros((n2, 2 * h1, 2 * n1))
    j = np.arange(h1)[None, :]
    for r in range(n2):
        th = 2 * np.pi * (k1 * (n2 * j + r) % n) / n
        gr, gi = np.cos(th), -np.sin(th)
        g_sig[r, 0::2, :h1] = gr
        g_sig[r, 0::2, h1:] = -gi
        g_sig[r, 1::2, :h1] = gi
        g_sig[r, 1::2, h1:] = gr
        g_flt[r, 0::2, :h1] = gr
        g_flt[r, 1::2, :h1] = gi
        m = n2 * (j + 1) - r
        thb = 2 * np.pi * (k1 * m % n) / n
        live = (m < length).astype(np.float64)
        g_flt[r, 0::2, h1:] = np.cos(thb) * live
        g_flt[r, 1::2, h1:] = np.sin(thb) * live
        wr, wi = (np.cos(th) / n).T, (np.sin(th) / n).T
        g_inv[r, :h1, 0::2] = wr
        g_inv[r, :h1, 1::2] = -wi
        g_inv[r, h1:, 0::2] = wi
        g_inv[r, h1:, 1::2] = wr
    a = np.arange(n2)
    th2 = 2 * np.pi * (np.outer(a, a) % n2) / n2
    c2, s2 = np.cos(th2), np.sin(th2)
    f_fwd = np.block([[c2, s2], [-s2, c2]])
    f_inv = np.block([[c2, -s2], [s2, c2]])
    cast = lambda z: jnp.asarray(z, dtype=F32).astype(BF16)
    return cast(g_sig), cast(g_flt), cast(g_inv), cast(f_fwd), cast(f_inv)


def _fft_stage1(src_a_ref, src_b_ref, tab_ref, work_ref, n1, is_filter):
    n2 = FFT_N2
    grp = SUBLANES

    def body(i, c):
        r0 = pl.multiple_of(i * grp, grp)
        b0 = pl.multiple_of(n2 - grp - r0, grp) if is_filter else r0
        a = jnp.swapaxes(src_a_ref[:, pl.ds(r0, grp), :], 0, 1)
        b = jnp.swapaxes(src_b_ref[:, pl.ds(b0, grp), :], 0, 1)
        outs = []
        for s in range(grp):
            rhs = jnp.concatenate([a[s], b[grp - 1 - s] if is_filter else b[s]], axis=0).astype(BF16)
            outs.append(_dot(tab_ref[r0 + s], rhs))
        work_ref[:, pl.ds(r0, grp), :] = jnp.swapaxes(jnp.stack(outs, axis=0), 0, 1)
        return c

    lax.fori_loop(0, n2 // grp, body, 0, unroll=2)


def _hyfilt_body(t_ref, ts_ref, delta_ref, feat_ref, feats_ref, w3f_ref, w3b_ref, gflt_ref, ffwd_ref, khat_ref,
                 hf_ref, hb_ref, work_ref, *, length, rc):
    n2 = FFT_N2
    n1 = 2 * length // n2
    tiles = rc // n2
    cb = hf_ref.shape[2]

    def gen(i, c):
        r0 = pl.multiple_of(i * rc, rc)
        j0 = pl.multiple_of(i * tiles, tiles)
        delta = jnp.abs(delta_ref[...])
        hf = _dot_split(feat_ref[pl.ds(r0, rc), :], w3f_ref[...]) * jnp.exp(-t_ref[pl.ds(r0, rc), :] * delta)
        hb = _dot_split(feats_ref[pl.ds(r0, rc), :], w3b_ref[...]) * jnp.exp(-ts_ref[pl.ds(r0, rc), :] * delta)
        hf_ref[pl.ds(j0, tiles), :n2, :] = hf.reshape(tiles, n2, cb)
        hb_ref[pl.ds(j0, tiles), :n2, :] = hb.reshape(tiles, n2, cb)
        return c

    lax.fori_loop(0, length // rc, gen, 0)
    _fft_stage1(hf_ref, hb_ref, gflt_ref, work_ref, n1, True)

    side = MXU_COLS // cb

    def stage2(i, c):
        k1s = [i * side + v for v in range(side)]
        blks = [work_ref[pl.ds(pl.multiple_of(2 * k1, 2), 2), :n2, :].reshape(2 * n2, cb).astype(BF16)
                for k1 in k1s]
        x = _dot(ffwd_ref[...], jnp.concatenate(blks, axis=1)).astype(BF16)
        for v, k1 in enumerate(k1s):
            khat_ref[0, 0, pl.ds(pl.multiple_of(k1 * (2 * n2), 2 * n2), 2 * n2), :] = x[:, v * cb:(v + 1) * cb]
        return c

    lax.fori_loop(0, n1 // side, stage2, 0, unroll=FFT_UNROLL // side)


def _hyfilt(t_col, delta, feat, w3, g_flt, f_fwd, rc=512):
    s = t_col.shape[0]
    cb = HY_CB
    ncb = HY_WIDTH // cb
    n2 = FFT_N2
    n1 = 2 * s // n2
    norder = w3.shape[1] // (2 * HY_WIDTH)
    one = pl.Buffered(1)

    def full(a):
        nd = a.ndim
        return pl.BlockSpec(a.shape, lambda o, c: (0,) * nd, pipeline_mode=one)

    shift = lambda a: jnp.concatenate([a[1:], jnp.zeros_like(a[:1])], axis=0)
    return pl.pallas_call(
        functools.partial(_hyfilt_body, length=s, rc=rc),
        grid=(norder, ncb),
        in_specs=[full(t_col), full(t_col),
                  pl.BlockSpec((1, cb), lambda o, c: (0, c)),
                  full(feat), full(feat),
                  pl.BlockSpec((HY_FFN, cb), lambda o, c: (0, 2 * ncb * o + c)),
                  pl.BlockSpec((HY_FFN, cb), lambda o, c: (0, 2 * ncb * o + ncb + c)),
                  full(g_flt), full(f_fwd)],
        out_specs=pl.BlockSpec((1, 1, 2 * n1 * n2, cb), lambda o, c: (o, c, 0, 0)),
        out_shape=jax.ShapeDtypeStruct((norder, ncb, 2 * n1 * n2, cb), BF16),
        scratch_shapes=[pltpu.VMEM((n1 // 2, n2 + FFT_PAD, cb), F32)] * 2
        + [pltpu.VMEM((2 * n1, n2 + FFT_PAD, cb), F32)],
        compiler_params=_cparams(("arbitrary", "arbitrary")),
        name="hyfilt",
    )(t_col, shift(t_col), delta, feat, shift(feat), w3, w3, g_flt, f_fwd)


def _hyena_body(pz_ref, phy_hbm, cwz_ref, cwg1_ref, cwg2_ref, cbz_ref, cbg1_ref, cbg2_ref,
                skip_ref, khat_hbm, gsig_ref, ginv_ref, ffwd_ref, finv_ref,
                out_ref,
                za_ref, zb_ref, ya_ref, yb_ref, work_ref, pg1_ref, pg2_ref, khat_ref, sem, ksem, *, length, rc):
    ncb = pl.num_programs(0)
    pair = pl.ds(2 * pl.program_id(1), 2)

    def gate_copy(order, dst_ref):
        src = phy_hbm.at[pair, pl.ds((order + 1) * ncb + pl.program_id(0), 1)]
        return pltpu.make_async_copy(src, dst_ref, sem.at[order])

    gate_copies = [gate_copy(0, pg1_ref), gate_copy(1, pg2_ref)]
    for cp in gate_copies:
        cp.start()

    new_block = pl.program_id(1) == 0
    spec_copies = [pltpu.make_async_copy(khat_hbm.at[o, pl.program_id(0)], khat_ref.at[o], ksem.at[o])
                   for o in range(khat_ref.shape[0])]

    @pl.when(new_block)
    def _():
        for cp in spec_copies:
            cp.start()
    n2 = FFT_N2
    n1 = 2 * length // n2
    h1 = n1 // 2
    tiles = rc // n2
    cb = za_ref.shape[2]
    side = MXU_COLS // cb
    sub = SUBLANES

    def conv3(p_ref, b, r0, w_ref, bias_ref):
        p = p_ref[b, 0, pl.ds(r0, rc), :]
        before = p_ref[b, 0, pl.ds(pl.multiple_of(jnp.maximum(r0 - sub, 0), sub), sub), :][sub - 1:sub, :]
        after = p_ref[b, 0, pl.ds(pl.multiple_of(jnp.minimum(r0 + rc, length - sub), sub), sub), :][0:1, :]
        before = jnp.where(r0 == 0, 0.0, before)
        after = jnp.where(r0 + rc == length, 0.0, after)
        row = lax.broadcasted_iota(jnp.int32, p.shape, 0)
        prev = jnp.where(row == 0, before, pltpu.roll(p, 1, 0))
        nxt = jnp.where(row == rc - 1, after, pltpu.roll(p, rc - 1, 0))
        return prev * w_ref[0:1, :] + p * w_ref[1:2, :] + nxt * w_ref[2:3, :] + bias_ref[...]

    def stage2(o):
        def body(i, c):
            k1s = [i * FFT_UNROLL + u for u in range(FFT_UNROLL)]
            blks = [work_ref[pl.ds(pl.multiple_of(2 * k1, 2), 2), :n2, :].reshape(2 * n2, cb).astype(BF16)
                    for k1 in k1s]
            outs = []
            for u in range(0, FFT_UNROLL, side):
                x = _dot(ffwd_ref[...], jnp.concatenate(blks[u:u + side], axis=1))
                ys = []
                for v in range(side):
                    k1 = k1s[u + v]
                    kh = khat_ref[o, pl.ds(pl.multiple_of(k1 * (2 * n2), 2 * n2), 2 * n2), :].astype(F32)
                    xr, xi = x[:n2, v * cb:(v + 1) * cb], x[n2:, v * cb:(v + 1) * cb]
                    kr, ki = kh[:n2], kh[n2:]
                    ys.append(jnp.concatenate([xr * kr - xi * ki, xr * ki + xi * kr], axis=0).astype(BF16))
                out = _dot(finv_ref[...], jnp.concatenate(ys, axis=1))
                outs.extend(out[:, v * cb:(v + 1) * cb] for v in range(side))
            for k1, out in zip(k1s, outs):
                work_ref[pl.ds(pl.multiple_of(2 * k1, 2), 2), :n2, :] = out.reshape(2, n2, cb)
            return c
        lax.fori_loop(0, n1 // FFT_UNROLL, body, 0)

    def stage3():
        grp = SUBLANES

        def body(i, c):
            r0 = pl.multiple_of(i * grp, grp)
            wk = jnp.swapaxes(work_ref[:, pl.ds(r0, grp), :], 0, 1)
            res = jnp.stack([_dot(ginv_ref[r0 + s], wk[s].astype(BF16)) for s in range(grp)], axis=0)
            ya_ref[:, pl.ds(r0, grp), :] = jnp.swapaxes(res[:, :h1], 0, 1)
            yb_ref[:, pl.ds(r0, grp), :] = jnp.swapaxes(res[:, h1:], 0, 1)
            return c
        lax.fori_loop(0, n2 // grp, body, 0, unroll=2)

    def rows3(ref, i):
        return ref.at[pl.ds(pl.multiple_of(i * tiles, tiles), tiles), :n2, :]

    def load_z(i, c):
        r0 = pl.multiple_of(i * rc, rc)
        rows3(za_ref, i)[...] = conv3(pz_ref, 0, r0, cwz_ref, cbz_ref).reshape(tiles, n2, cb)
        rows3(zb_ref, i)[...] = conv3(pz_ref, 1, r0, cwz_ref, cbz_ref).reshape(tiles, n2, cb)
        return c

    lax.fori_loop(0, length // rc, load_z, 0)
    for o, (pg_ref, cw_ref, cb_ref) in enumerate(((pg1_ref, cwg1_ref, cbg1_ref), (pg2_ref, cwg2_ref, cbg2_ref))):
        _fft_stage1(za_ref, zb_ref, gsig_ref, work_ref, n1, False)

        @pl.when(new_block)
        def _():
            spec_copies[o].wait()

        stage2(o)
        stage3()
        skip = skip_ref[o:o + 1, :]
        gate_copies[o].wait()

        def gate(i, c):
            r0 = pl.multiple_of(i * rc, rc)
            rows = pl.ds(r0, rc)
            ya, yb = (rows3(r, i)[...].reshape(rc, cb) for r in (ya_ref, yb_ref))
            za, zb = (rows3(r, i)[...].reshape(rc, cb) for r in (za_ref, zb_ref))
            new_a = conv3(pg_ref, 0, r0, cw_ref, cb_ref) * (ya + skip * za)
            new_b = conv3(pg_ref, 1, r0, cw_ref, cb_ref) * (yb + skip * zb)
            if o == 0:
                rows3(za_ref, i)[...] = new_a.reshape(tiles, n2, cb)
                rows3(zb_ref, i)[...] = new_b.reshape(tiles, n2, cb)
            else:
                out_ref[0, rows, :] = new_a.astype(out_ref.dtype)
                out_ref[1, rows, :] = new_b.astype(out_ref.dtype)
            return c

        lax.fori_loop(0, length // rc, gate, 0)


def _hyena(p_hy, conv_w, conv_b, skip, khat, tables, rc=512):
    b, _, s, _ = p_hy.shape
    w = HY_WIDTH
    cb = HY_CB
    ncb = w // cb
    n2 = FFT_N2
    n1 = 2 * s // n2
    g_sig, _, g_inv, f_fwd, f_inv = tables
    one = pl.Buffered(1)

    def cwspec(off):
        return pl.BlockSpec((3, cb), lambda c, p: (0, off + c))

    def cbspec(off):
        return pl.BlockSpec((1, cb), lambda c, p: (0, off + c))

    def full(a):
        nd = a.ndim
        return pl.BlockSpec(a.shape, lambda c, p: (0,) * nd, pipeline_mode=one)

    in_specs = [
        pl.BlockSpec((2, 1, s, cb), lambda c, p: (p, c, 0, 0)),
        pl.BlockSpec(memory_space=pl.ANY),
        cwspec(0), cwspec(ncb), cwspec(2 * ncb),
        cbspec(0), cbspec(ncb), cbspec(2 * ncb),
        pl.BlockSpec((2, cb), lambda c, p: (0, c)),
        pl.BlockSpec(memory_space=pl.ANY),
        full(g_sig), full(g_inv), full(f_fwd), full(f_inv),
    ]
    return pl.pallas_call(
        functools.partial(_hyena_body, length=s, rc=rc),
        grid=(ncb, b // 2),
        in_specs=in_specs,
        out_specs=pl.BlockSpec((2, s, cb), lambda c, p: (p, 0, c)),
        out_shape=jax.ShapeDtypeStruct((b, s, w), BF16),
        scratch_shapes=[pltpu.VMEM((n1 // 2, n2 + FFT_PAD, cb), F32)] * 4
        + [pltpu.VMEM((2 * n1, n2 + FFT_PAD, cb), F32)]
        + [pltpu.VMEM((2, 1, s, cb), F32)] * 2 + [pltpu.VMEM((khat.shape[0], 2 * n1 * n2, cb), BF16)]
        + [pltpu.SemaphoreType.DMA((2,)), pltpu.SemaphoreType.DMA((khat.shape[0],))],
        compiler_params=_cparams(("arbitrary", "arbitrary")),
        name="hyena",
    )(p_hy, p_hy, conv_w, conv_w, conv_w, conv_b, conv_b, conv_b,
      skip, khat, g_sig, g_inv, f_fwd, f_inv)


def _attn_body(q_ref, k_ref, v_ref, o_ref, lse_ref, *, n_side, qb):
    lr = q_ref.shape[2]
    kw = qb + 2 * n_side
    nh = HEADS_PER_GROUP
    lane_head = lax.broadcasted_iota(jnp.int32, (qb, ATTN_OUT), 1) // HEAD_DIM
    qi = lax.broadcasted_iota(jnp.int32, (nh * qb, kw), 0) % qb
    ki = lax.broadcasted_iota(jnp.int32, (nh * qb, kw), 1)
    band = [jnp.where(jnp.abs(ki - qi - shift) <= n_side, 0.0, -1e30) for shift in (0, n_side, 2 * n_side)]

    def body(cls, i, c):
        q0 = pl.multiple_of(i * qb, qb)
        w0 = pl.multiple_of(jnp.clip(q0 - n_side, 0, lr - kw), n_side)
        q = q_ref[0, cls, pl.ds(q0, qb), :]
        kwin = k_ref[0, cls, pl.ds(w0, kw), :]
        vwin = v_ref[0, cls, pl.ds(w0, kw), :]
        shift = q0 - w0
        bias = jnp.where(shift == n_side, band[1], jnp.where(shift == 0, band[0], band[2]))
        qs = jnp.concatenate([jnp.where(lane_head == h, q, jnp.zeros_like(q)) for h in range(nh)], axis=0)
        s = _dot_t(qs, kwin) + bias
        m = jnp.max(s, axis=-1, keepdims=True)
        p = jnp.exp(s - m)
        l = jnp.sum(p, axis=-1, keepdims=True)
        pv = _dot(p.astype(BF16), vwin) / l
        lse_rows = m + jnp.log(l)
        o = jnp.zeros((qb, ATTN_OUT), F32)
        lse = jnp.zeros((qb, ATTN_OUT), F32)
        for h in range(nh):
            mine = lane_head == h
            o = jnp.where(mine, pv[h * qb:(h + 1) * qb], o)
            lse = jnp.where(mine, lse_rows[h * qb:(h + 1) * qb], lse)
        o_ref[0, pl.ds(q0, qb), cls * ATTN_OUT:(cls + 1) * ATTN_OUT] = o
        lse_ref[0, pl.ds(q0, qb), cls * ATTN_OUT:(cls + 1) * ATTN_OUT] = lse
        return c

    for cls in range(q_ref.shape[1]):
        lax.fori_loop(0, lr // qb, functools.partial(body, cls), 0, unroll=2)


def _attn_group(q, k, v, g, n_side, qb=128, max_classes=8):
    b, dil, lr, _ = q.shape
    cg = min(dil, max_classes)
    in_spec = pl.BlockSpec((1, cg, lr, ATTN_OUT), lambda bi, r: (bi, r, 0, 0))
    out_spec = pl.BlockSpec((1, lr, cg * ATTN_OUT), lambda bi, r: (bi, 0, r))
    o, lse = pl.pallas_call(
        functools.partial(_attn_body, n_side=n_side, qb=qb),
        grid=(b, dil // cg),
        in_specs=[in_spec, in_spec, in_spec],
        out_specs=[out_spec, out_spec],
        out_shape=[jax.ShapeDtypeStruct((b, lr, dil * ATTN_OUT), F32)] * 2,
        compiler_params=_cparams(("arbitrary", "arbitrary")),
        name=f"attn{g}",
    )(q, k, v)
    return o, lse


def _memkv_body(mem_ref, g_ref, w_ref, kv_ref):
    mn = _rms(mem_ref[0], g_ref[...]).astype(BF16)
    kv_ref[0] = _dot(mn, w_ref[...]).astype(BF16)


def _memkv(mem, g, w_kv):
    b, m, d = mem.shape
    n = w_kv.shape[1]
    return pl.pallas_call(
        _memkv_body,
        grid=(b,),
        in_specs=[pl.BlockSpec((1, m, d), lambda i: (i, 0, 0)),
                  pl.BlockSpec((1, d), lambda i: (0, 0)),
                  pl.BlockSpec((d, n), lambda i: (0, 0))],
        out_specs=pl.BlockSpec((1, m, n), lambda i: (i, 0, 0)),
        out_shape=jax.ShapeDtypeStruct((b, m, n), BF16),
        compiler_params=_cparams(("arbitrary",)),
        name="memkv",
    )(mem, g, w_kv)


def _merge_body(x_ref, z_ref, o0_ref, o1_ref, o2_ref, l0_ref, l1_ref, l2_ref, gate_ref,
                wuh_ref, wua_ref, wout_ref, gx_ref, wq_ref, kv_ref, wo_ref, gm_ref, wr_ref,
                p1_ref, p2_ref, x2_ref, hm_ref, aff_ref, affh_ref, affl_ref, *, nsplit):
    rows = x_ref.shape[0] // nsplit
    for part in range(nsplit):
        sl = slice(part * rows, (part + 1) * rows)

        def natural(ref, perm_ref):
            dil = ref.shape[2] // ATTN_OUT
            rr = rows // dil
            blk = ref[0, part * rr:(part + 1) * rr, :]
            if dil == 1:
                return blk
            pieces = [blk[:, r * ATTN_OUT:(r + 1) * ATTN_OUT] for r in range(dil)]
            if rr % SUBLANES == 0 and dil % SUBLANES == 0:
                return jnp.swapaxes(jnp.stack(pieces, axis=0), 0, 1).reshape(rows, ATTN_OUT)
            hi, lo = _split(jnp.concatenate(pieces, axis=0))
            return _dot(perm_ref[...], hi) + _dot(perm_ref[...], lo)

        outs = [natural(r, p) for r, p in ((o0_ref, None), (o1_ref, p1_ref), (o2_ref, p2_ref))]
        lses = [natural(r, p) for r, p in ((l0_ref, None), (l1_ref, p1_ref), (l2_ref, p2_ref))]
        _merge_rows(sl, outs, lses, x_ref, z_ref, gate_ref,
                    wuh_ref, wua_ref, wout_ref, gx_ref, wq_ref, kv_ref, wo_ref, gm_ref, wr_ref,
                    x2_ref, hm_ref, aff_ref, affh_ref, affl_ref)


def _merge_rows(sl, outs, lses, x_ref, z_ref, gate_ref,
                wuh_ref, wua_ref, wout_ref, gx_ref, wq_ref, kv_ref, wo_ref, gm_ref, wr_ref,
                x2_ref, hm_ref, aff_ref, affh_ref, affl_ref):
    d = x_ref.shape[1]
    l0, l1, l2 = lses
    mx = jnp.maximum(jnp.maximum(l0, l1), l2)
    e0, e1, e2 = jnp.exp(l0 - mx), jnp.exp(l1 - mx), jnp.exp(l2 - mx)
    attn = (e0 * outs[0] + e1 * outs[1] + e2 * outs[2]) / (e0 + e1 + e2)
    y_hy = _dot(z_ref[sl, :].astype(BF16), wuh_ref[...])
    y_at = _dot(attn.astype(BF16), wua_ref[...])
    gates = gate_ref[sl, :].astype(F32)
    mix = gates[:, :d] * y_hy + gates[:, d:] * y_at
    x1 = x_ref[sl, :] + _dot(mix.astype(BF16), wout_ref[...])

    hx = _rms(x1, gx_ref[...]).astype(BF16)
    qx = _dot(hx, wq_ref[...]).astype(BF16)
    kv = kv_ref[0]
    xw = X_HEADS * X_HEAD_DIM
    heads = []
    for h in range(X_HEADS):
        hc = slice(h * X_HEAD_DIM, (h + 1) * X_HEAD_DIM)
        s = _dot_t(qx[:, hc], kv[:, hc]) * (X_HEAD_DIM ** -0.5)
        m = jnp.max(s, axis=-1, keepdims=True)
        p = jnp.exp(s - m)
        p = p / jnp.sum(p, axis=-1, keepdims=True)
        heads.append(_dot(p.astype(BF16), kv[:, xw + h * X_HEAD_DIM:xw + (h + 1) * X_HEAD_DIM]))
    ox = jnp.concatenate(heads, axis=-1).astype(BF16)
    x2 = x1 + _dot(ox, wo_ref[...])
    x2_ref[sl, :] = x2

    hm = _rms(x2, gm_ref[...])
    hm_hi, hm_lo = _split(hm)
    hm_ref[sl, :] = hm_hi
    nr = hm.shape[0]
    cross = _dot(jnp.concatenate([hm_hi, hm_lo], axis=0), wr_ref[...])
    logits = (cross[:nr, :LANES] + cross[:nr, LANES:]) + (cross[nr:, :LANES] + cross[nr:, LANES:])
    lane = lax.broadcasted_iota(jnp.int32, logits.shape, 1)
    logits = jnp.where(lane < N_EXPERTS, logits, -1e30)
    m = jnp.max(logits, axis=-1, keepdims=True)
    p = jnp.exp(logits - m)
    aff = p / jnp.sum(p, axis=-1, keepdims=True)
    aff_ref[sl, :] = aff
    affh_ref[sl, :], affl_ref[sl, :] = _split(aff)


def _merge(x2d, z2d, outs, lses, gates, wuh, wua, wout, gx, wq, kv, wo, gm, wr_cat, seq, tm=512, nsplit=2):
    n, d = x2d.shape
    nseq = seq // tm
    row = lambda i: (i, 0)
    const = lambda i: (0, 0)

    def rspec(a):
        return pl.BlockSpec((tm, a.shape[1]), row)

    def aspec(a):
        dil = a.shape[2] // ATTN_OUT
        return pl.BlockSpec((1, tm // dil, a.shape[2]), lambda i: (i // nseq, i % nseq, 0))

    def cspec(a):
        return pl.BlockSpec(a.shape, const)

    rows = tm // nsplit
    perms = []
    for a in outs[1:]:
        dil = a.shape[2] // ATTN_OUT
        src = (np.arange(rows) % dil) * (rows // dil) + np.arange(rows) // dil
        perms.append(jnp.asarray(np.eye(rows)[src], dtype=BF16))

    in_arrays = [x2d, z2d, *outs, *lses, gates, wuh, wua, wout, gx, wq, kv, wo, gm, wr_cat, *perms]
    in_specs = ([rspec(a) for a in in_arrays[:2]] + [aspec(a) for a in in_arrays[2:8]] + [rspec(gates)]
                + [cspec(a) for a in in_arrays[9:14]]
                + [pl.BlockSpec((1,) + kv.shape[1:], lambda i: (i // nseq, 0, 0))]
                + [cspec(a) for a in in_arrays[15:]])
    return pl.pallas_call(
        functools.partial(_merge_body, nsplit=nsplit),
        grid=(n // tm,),
        in_specs=in_specs,
        out_specs=[pl.BlockSpec((tm, d), row), pl.BlockSpec((tm, d), row)] + [pl.BlockSpec((tm, LANES), row)] * 3,
        out_shape=[jax.ShapeDtypeStruct((n, d), F32), jax.ShapeDtypeStruct((n, d), BF16),
                   jax.ShapeDtypeStruct((n, LANES), F32), jax.ShapeDtypeStruct((n, LANES), BF16),
                   jax.ShapeDtypeStruct((n, LANES), BF16)],
        compiler_params=_cparams(("arbitrary",)),
        name="merge",
    )(*in_arrays)


def _topk_body(aff_ref, tri_ref, rank_ref, bounds_ref, *, cap, tok_block):
    a = aff_ref[0]
    e, s = a.shape

    def count(mask):
        return jnp.sum(jnp.where(mask, 1.0, 0.0), axis=-1, keepdims=True)

    def as_float(bits):
        return pltpu.bitcast(jnp.broadcast_to(bits, (e, LANES)), F32)[:, 0:1]

    def search(i, thr):
        cand = thr | (jnp.int32(1) << (30 - i))
        return jnp.where(count(a >= as_float(cand)) >= cap, cand, thr)

    thr = as_float(lax.fori_loop(0, 31, search, jnp.zeros((e, 1), jnp.int32)))
    gt = a > thr
    eq = a == thr
    need = cap - count(gt)

    def prefix_excl(mask):
        mf = jnp.where(mask, 1.0, 0.0)
        parts = []
        carry = jnp.zeros((e, 1), F32)
        for c in range(s // LANES):
            blk = mf[:, c * LANES:(c + 1) * LANES]
            inc = _dot(blk.astype(BF16), tri_ref[...])
            parts.append(inc - blk + carry)
            carry = carry + inc[:, LANES - 1:LANES]
        return jnp.concatenate(parts, axis=-1)

    sel = gt | (eq & (prefix_excl(eq) < need))
    excl = prefix_excl(sel)
    rank_ref[0] = jnp.where(sel, excl, -1.0)

    tok = lax.broadcasted_iota(jnp.int32, (e, s), 1)
    lane = lax.broadcasted_iota(jnp.int32, (e, LANES), 1)
    bounds = jnp.zeros((e, LANES), F32)
    for j in range(s // tok_block + 1):
        bounds = jnp.where(lane == j, count(sel & (tok < j * tok_block)), bounds)
    bounds_ref[0] = bounds.astype(jnp.int32)


def _topk(aff_t, cap, tok_block):
    b, e, s = aff_t.shape
    assert s // tok_block + 1 <= LANES
    tri = jnp.asarray(np.triu(np.ones((LANES, LANES))), dtype=BF16)
    return pl.pallas_call(
        functools.partial(_topk_body, cap=cap, tok_block=tok_block),
        grid=(b,),
        in_specs=[pl.BlockSpec((1, e, s), lambda i: (i, 0, 0)),
                  pl.BlockSpec((LANES, LANES), lambda i: (0, 0))],
        out_specs=[pl.BlockSpec((1, e, s), lambda i: (i, 0, 0)),
                   pl.BlockSpec((1, e, LANES), lambda i: (i, 0, 0))],
        out_shape=[jax.ShapeDtypeStruct((b, e, s), F32),
                   jax.ShapeDtypeStruct((b, e, LANES), jnp.int32)],
        compiler_params=_cparams(("arbitrary",)),
        name="topk",
    )(aff_t, tri)


def _moe_windows(bounds_ref, base, ne, step, tokens, window):
    per = tokens // MOE_BOUND_STEP
    starts = []
    passes = jnp.int32(0)
    for e in range(ne):
        r_lo = bounds_ref[base + e * LANES + step * per]
        r_hi = bounds_ref[base + e * LANES + (step + 1) * per]
        ws = (r_lo // BF16_ROWS) * BF16_ROWS
        starts.append(ws)
        passes = jnp.maximum(passes, (r_hi - ws + window - 1) // window)
    return starts, passes


def _gather_body(bounds_ref, rank_ref, hm_ref, affh_ref, affl_ref, xe_ref, gs_ref, *, cap, eg):
    w = GATHER_WINDOW
    ne, tc = rank_ref.shape[1], rank_ref.shape[2]
    bi = pl.program_id(0)
    ci = pl.program_id(1)

    @pl.when(ci == 0)
    def _():
        xe_ref[...] = jnp.zeros_like(xe_ref)
        gs_ref[...] = jnp.zeros_like(gs_ref)

    starts, passes = _moe_windows(bounds_ref, bi * ne * LANES, ne, ci, tc, w)
    iota_w = lax.broadcasted_iota(jnp.int32, (w, 1), 0)

    def one_pass(p, carry):
        for g0 in range(0, ne, eg):
            lo = [starts[e] + p * w for e in range(g0, g0 + eg)]
            phys = [pl.multiple_of(jnp.minimum(v, cap - w), BF16_ROWS) for v in lo]
            rk = jnp.concatenate([jnp.broadcast_to(rank_ref[0, e:e + 1, :], (w, tc))
                                  for e in range(g0, g0 + eg)], axis=0)
            slot = jnp.concatenate([iota_w + v for v in phys], axis=0).astype(F32)
            lom = jnp.concatenate([jnp.zeros_like(iota_w) + v for v in lo], axis=0).astype(F32)
            onehot = jnp.where((rk == slot) & (slot >= lom), 1.0, 0.0).astype(BF16)
            res = _dot(onehot, hm_ref[0])
            resg = _dot(onehot, affh_ref[0]) + _dot(onehot, affl_ref[0])
            for k, e in enumerate(range(g0, g0 + eg)):
                rows = pl.ds(phys[k], w)
                xe_ref[0, e, rows, :] = (xe_ref[0, e, rows, :].astype(F32) + res[k * w:(k + 1) * w]).astype(BF16)
                gs_ref[0, e, rows, :] = gs_ref[0, e, rows, :] + resg[k * w:(k + 1) * w]
        return carry

    lax.fori_loop(0, passes, one_pass, 0)


def _gather(bounds, rank, hm, aff_hi, aff_lo, cap, eg=16):
    b, e, s = rank.shape
    d = hm.shape[2]
    tc = GATHER_TOKENS
    return pl.pallas_call(
        functools.partial(_gather_body, cap=cap, eg=eg),
        grid_spec=pltpu.PrefetchScalarGridSpec(
            num_scalar_prefetch=1,
            grid=(b, s // tc),
            in_specs=[pl.BlockSpec((1, e, tc), lambda bi, ci, bnd: (bi, 0, ci)),
                      pl.BlockSpec((1, tc, d), lambda bi, ci, bnd: (bi, ci, 0)),
                      pl.BlockSpec((1, tc, LANES), lambda bi, ci, bnd: (bi, ci, 0)),
                      pl.BlockSpec((1, tc, LANES), lambda bi, ci, bnd: (bi, ci, 0))],
            out_specs=[pl.BlockSpec((1, e, cap, d), lambda bi, ci, bnd: (bi, 0, 0, 0)),
                       pl.BlockSpec((1, e, cap, LANES), lambda bi, ci, bnd: (bi, 0, 0, 0))],
        ),
        out_shape=[jax.ShapeDtypeStruct((b, e, cap, d), BF16),
                   jax.ShapeDtypeStruct((b, e, cap, LANES), F32)],
        compiler_params=_cparams(("arbitrary", "arbitrary")),
        name="gather",
    )(bounds.reshape(-1), rank, hm, aff_hi, aff_lo)


def _ffn_body(xe_ref, gs_ref, wg_ref, wu_ref, wd_ref, ye_ref, acc_ref, *, mb):
    f = pl.program_id(1)
    nb, _, cap, d = xe_ref.shape
    wg = wg_ref[0].astype(BF16)
    wu = wu_ref[0].astype(BF16)
    wd = wd_ref[0].astype(BF16)

    @pl.when(f == 0)
    def _():
        acc_ref[...] = jnp.zeros_like(acc_ref)

    for b0 in range(0, nb, mb):
        xe = xe_ref[b0:b0 + mb, 0].reshape(mb * cap, d)
        a = _dot(xe, wg)
        u = _dot(xe, wu)
        hsw = (a * jax.nn.sigmoid(a) * u).astype(BF16)
        acc_ref[b0:b0 + mb] += _dot(hsw, wd).reshape(mb, cap, d)

    @pl.when(f == pl.num_programs(1) - 1)
    def _():
        lane = lax.broadcasted_iota(jnp.int32, gs_ref.shape[2:], 1)
        mine = lane == pl.program_id(0)
        for b in range(nb):
            gate = jnp.sum(jnp.where(mine, gs_ref[b, 0], 0.0), axis=-1, keepdims=True)
            ye_ref[b, 0] = (acc_ref[b] * gate).astype(BF16)


def _ffn(xe, gs, wg, wu, wd, ft=1024, mb=1):
    b, e, cap, d = xe.shape
    ff = wg.shape[2]
    return pl.pallas_call(
        functools.partial(_ffn_body, mb=mb),
        grid=(e, ff // ft),
        in_specs=[pl.BlockSpec((b, 1, cap, d), lambda ei, fi: (0, ei, 0, 0)),
                  pl.BlockSpec((b, 1, cap, LANES), lambda ei, fi: (0, ei, 0, 0)),
                  pl.BlockSpec((1, d, ft), lambda ei, fi: (ei, 0, fi)),
                  pl.BlockSpec((1, d, ft), lambda ei, fi: (ei, 0, fi)),
                  pl.BlockSpec((1, ft, d), lambda ei, fi: (ei, fi, 0))],
        out_specs=pl.BlockSpec((b, 1, cap, d), lambda ei, fi: (0, ei, 0, 0)),
        out_shape=jax.ShapeDtypeStruct((b, e, cap, d), BF16),
        scratch_shapes=[pltpu.VMEM((b, cap, d), F32)],
        compiler_params=_cparams(("arbitrary", "arbitrary"), vmem=VMEM_LIMIT_LARGE),
        name="ffn",
    )(xe, gs, wg, wu, wd)


def _scatter_body(bounds_ref, rank_ref, ye_ref, x2_ref, g_ref, out_ref, acc_ref, *, eg):
    w = SCATTER_WINDOW
    ts = x2_ref.shape[1]
    ne, cap = ye_ref.shape[1], ye_ref.shape[2]
    starts, passes = _moe_windows(bounds_ref, pl.program_id(0) * ne * LANES, ne, pl.program_id(1), ts, w)
    rank = rank_ref[0]
    acc_ref[...] = x2_ref[0]
    iota_w = lax.broadcasted_iota(jnp.int32, (1, w), 1)

    def one_pass(p, carry):
        for g0 in range(0, ne, eg):
            onehots, rows = [], []
            for e in range(g0, g0 + eg):
                lo = starts[e] + p * w
                phys = pl.multiple_of(jnp.minimum(lo, cap - w), BF16_ROWS)
                slot = (iota_w + phys).astype(F32)
                hit = (rank[:, e:e + 1] == slot) & (slot >= lo.astype(F32))
                onehots.append(jnp.where(hit, 1.0, 0.0).astype(BF16))
                rows.append(ye_ref[0, e, pl.ds(phys, w), :])
            acc_ref[...] += _dot(jnp.concatenate(onehots, axis=1), jnp.concatenate(rows, axis=0))
        return carry

    lax.fori_loop(0, passes, one_pass, 0)
    out_ref[0] = _rms(acc_ref[...], g_ref[...])


def _scatter(bounds, rank_t, ye, x2, g, eg=8):
    b, s, e = rank_t.shape
    cap, d = ye.shape[2], ye.shape[3]
    ts = SCATTER_TOKENS
    return pl.pallas_call(
        functools.partial(_scatter_body, eg=eg),
        grid_spec=pltpu.PrefetchScalarGridSpec(
            num_scalar_prefetch=1,
            grid=(b, s // ts),
            in_specs=[pl.BlockSpec((1, ts, e), lambda bi, ti, bnd: (bi, ti, 0)),
                      pl.BlockSpec((1, e, cap, d), lambda bi, ti, bnd: (bi, 0, 0, 0)),
                      pl.BlockSpec((1, ts, d), lambda bi, ti, bnd: (bi, ti, 0)),
                      pl.BlockSpec((1, d), lambda bi, ti, bnd: (0, 0))],
            out_specs=pl.BlockSpec((1, ts, d), lambda bi, ti, bnd: (bi, ti, 0)),
            scratch_shapes=[pltpu.VMEM((ts, d), F32)],
        ),
        out_shape=jax.ShapeDtypeStruct((b, s, d), F32),
        compiler_params=_cparams(("arbitrary", "arbitrary")),
        name="scatter",
    )(bounds.reshape(-1), rank_t, ye, x2, g)


def kernel(x, mem, ln_mix_g, w_in, hy_conv_w, hy_conv_b, filt_w1, filt_b1, filt_freq1, filt_w2, filt_b2,
           filt_freq2, filt_w3, hy_skip, w_up_hy, w_up_attn, w_gate, b_gate, w_out, ln_x_g, ln_mem_g,
           w_q_x, w_kv_mem, w_o_x, ln_moe_g, w_router, w_e_gate, w_e_up, w_e_down, ln_f_g):
    b, s, d = x.shape
    n = b * s
    x2d = x.reshape(n, d)

    half = HEAD_DIM // 2
    inv = ROPE_THETA ** (-jnp.arange(0, HEAD_DIM, 2, dtype=F32) / HEAD_DIM)
    ang = jnp.arange(s, dtype=F32)[:, None] * inv[None, :]
    lane = np.arange(LANES)
    cos_t = jnp.cos(ang)[:, lane % half]
    sin_t = jnp.sin(ang)[:, lane % half] * jnp.asarray(np.where(lane % HEAD_DIM < half, -1.0, 1.0), F32)[None, :]

    w_all = jnp.concatenate([w_in, w_gate], axis=1).astype(BF16)
    p_hy, *qkv, gates = _proj(x2d, ln_mix_g[None], w_all, b_gate[None], cos_t, sin_t, b, s)

    t_col = jnp.linspace(0.0, 1.0, s, dtype=F32)[:, None]
    grid_col = 2.0 * math.pi * jnp.arange(s, dtype=F32)[:, None] / s
    bands = jnp.linspace(1e-4, HY_BANDS - 1, HY_BANDS, dtype=F32)[None, :]
    feat = _filt(t_col, grid_col, bands, filt_w1, filt_b1, filt_freq1, filt_w2, filt_b2, filt_freq2)
    delta = jnp.linspace(math.log(HY_TARGET) / HY_SLOW_PCT, math.log(HY_TARGET) / HY_FAST_PCT,
                         HY_WIDTH, dtype=F32)[None, :]
    tables = _fft_tables(s)
    khat = _hyfilt(t_col, delta, feat, filt_w3, tables[1], tables[3])
    z = _hyena(p_hy, hy_conv_w, hy_conv_b[None], hy_skip, khat, tables)

    outs, lses = [], []
    for g, (window, dil) in enumerate(DIL_PAIRS):
        o_g, l_g = _attn_group(*qkv[3 * g:3 * g + 3], g, window // (2 * dil))
        outs.append(o_g)
        lses.append(l_g)

    kv = _memkv(mem, ln_mem_g[None], w_kv_mem.astype(BF16))
    wr_pad = jnp.pad(w_router, ((0, 0), (0, LANES - N_EXPERTS)))
    wr_cat = jnp.concatenate(_split(wr_pad), axis=1)
    x2, hm, aff, aff_hi, aff_lo = _merge(x2d, z.reshape(n, HY_WIDTH), outs, lses, gates,
                         w_up_hy.astype(BF16), w_up_attn.astype(BF16), w_out.astype(BF16),
                         ln_x_g[None], w_q_x.astype(BF16), kv, w_o_x.astype(BF16), ln_moe_g[None],
                         wr_cat, s)

    cap = max(1, EC_FACTOR * s // N_EXPERTS)
    aff3 = aff.reshape(b, s, LANES)
    rank, bounds = _topk(aff3[:, :, :N_EXPERTS].transpose(0, 2, 1), cap, MOE_BOUND_STEP)
    xe, gs = _gather(bounds, rank, hm.reshape(b, s, d), aff_hi.reshape(b, s, LANES), aff_lo.reshape(b, s, LANES), cap)
    ye = _ffn(xe, gs, w_e_gate, w_e_up, w_e_down)
    return _scatter(bounds, rank.transpose(0, 2, 1), ye, x2.reshape(b, s, d), ln_f_g[None])
```

```python
import functools
import math

import numpy as np
import jax
import jax.numpy as jnp
from jax import lax
from jax.experimental import pallas as pl
from jax.experimental.pallas import tpu as pltpu

F32 = jnp.float32
BF16 = jnp.bfloat16
HIGHEST = lax.Precision.HIGHEST

EPS = 1e-6
HY_WIDTH = 768
HY_BANDS = 16
HY_FFN = 64
HY_FAST_PCT = 0.3
HY_SLOW_PCT = 1.5
HY_TARGET = 1e-2
HEAD_DIM = 64
HEADS_PER_GROUP = 4
DIL_PAIRS = ((128, 1), (512, 4), (2048, 16))
N_GROUPS = len(DIL_PAIRS)
ATTN_WIDTH = N_GROUPS * HEADS_PER_GROUP * HEAD_DIM
ATTN_OUT = HEADS_PER_GROUP * HEAD_DIM
ROPE_THETA = 10000.0
X_HEADS = 4
X_HEAD_DIM = 128
N_EXPERTS = 16
EC_FACTOR = 2

LANES = 128
SUBLANES = 8
MXU_COLS = 256
VMEM_LIMIT = 56 * 1024 * 1024
VMEM_LIMIT_LARGE = 60 * 1024 * 1024

FFT_N2 = 128
HY_CB = 128
FFT_UNROLL = 8
FFT_PAD = 8
MOE_BOUND_STEP = 256
GATHER_TOKENS, GATHER_WINDOW = 256, 64
SCATTER_TOKENS, SCATTER_WINDOW = 512, 128
BF16_ROWS = 16


def _cparams(sem, vmem=VMEM_LIMIT):
    return pltpu.CompilerParams(dimension_semantics=sem, vmem_limit_bytes=vmem)


def _rms(x, g):
    return x * lax.rsqrt(jnp.mean(x * x, axis=-1, keepdims=True) + EPS) * g


def _dot(a, b):
    return jnp.dot(a, b, preferred_element_type=F32)


def _dot_hi(a, b):
    return jnp.dot(a, b, precision=HIGHEST, preferred_element_type=F32)


def _split(a):
    hi = a.astype(BF16)
    return hi, (a - hi.astype(F32)).astype(BF16)


def _dot_split(a, b):
    a_hi, a_lo = _split(a)
    b_hi, b_lo = _split(b)
    return _dot(a_hi, b_hi) + _dot(a_lo, b_hi) + _dot(a_hi, b_lo)


def _dot_t(a, b):
    return lax.dot_general(a, b, (((1,), (1,)), ((), ())), preferred_element_type=F32)


def _proj_body(x_ref, g_ref, w_ref, bg_ref, cos_ref, sin_ref, *rest, hyc, aw):
    ng = N_GROUPS
    perm_refs = (None,) + rest[:ng - 1]
    phy_ref = rest[ng - 1]
    qkv_refs = rest[ng:ng + 3 * ng]
    gate_ref = rest[ng + 3 * ng]
    h = _rms(x_ref[...], g_ref[...]).astype(BF16)
    phy = _dot(h, w_ref[:, :hyc])
    for j in range(hyc // HY_CB):
        phy_ref[0, j] = phy[:, j * HY_CB:(j + 1) * HY_CB]
    cos = cos_ref[...]
    sin = sin_ref[...]
    tm = cos.shape[0]
    lane = lax.broadcasted_iota(jnp.int32, (tm, LANES), 1)
    first = (lane % HEAD_DIM) < (HEAD_DIM // 2)

    def rope(t, scale):
        chunks = []
        for j in range(aw // LANES):
            tj = t[:, j * LANES:(j + 1) * LANES]
            partner = jnp.where(first, pltpu.roll(tj, LANES - HEAD_DIM // 2, 1),
                                pltpu.roll(tj, HEAD_DIM // 2, 1))
            chunks.append(((tj * cos + partner * sin) * scale).astype(BF16))
        return chunks

    gl = ATTN_OUT // LANES
    q = rope(_dot(h, w_ref[:, hyc:hyc + aw]), HEAD_DIM ** -0.5)
    k = rope(_dot(h, w_ref[:, hyc + aw:hyc + 2 * aw]), 1.0)
    vf = _dot(h, w_ref[:, hyc + 2 * aw:hyc + 3 * aw]).astype(BF16)
    v = [vf[:, j * LANES:(j + 1) * LANES] for j in range(aw // LANES)]
    for g in range(ng):
        for i, t in enumerate((q, k, v)):
            out_ref = qkv_refs[3 * g + i]
            tg = jnp.concatenate(t[g * gl:(g + 1) * gl], axis=1)
            if perm_refs[g] is not None:
                tg = _dot(perm_refs[g][...], tg).astype(BF16)
            dil = out_ref.shape[1]
            rows = tm // dil
            for r in range(dil):
                out_ref[0, r] = tg[r * rows:(r + 1) * rows, :]
    gate_ref[...] = jax.nn.sigmoid(_dot(h, w_ref[:, hyc + 3 * aw:]) + bg_ref[...]).astype(BF16)


def _proj(x2d, g, w_all, b_gate, cos_t, sin_t, batch, seq, tm=256):
    n, d = x2d.shape
    hyc = 3 * HY_WIDTH
    aw = ATTN_WIDTH
    gd = w_all.shape[1] - hyc - 3 * aw
    nseq = seq // tm
    row = lambda i: (i, 0)
    const = lambda i: (0, 0)
    perms = []
    for _, dil in DIL_PAIRS[1:]:
        rows = tm // dil
        src = (np.arange(tm) % rows) * dil + np.arange(tm) // rows
        perms.append(jnp.asarray(np.eye(tm)[src], dtype=BF16))
    qkv_specs, qkv_shapes = [], []
    for _, dil in DIL_PAIRS:
        for _ in range(3):
            qkv_specs.append(pl.BlockSpec((1, dil, tm // dil, ATTN_OUT), lambda i: (i // nseq, 0, i % nseq, 0)))
            qkv_shapes.append(jax.ShapeDtypeStruct((batch, dil, seq // dil, ATTN_OUT), BF16))
    return pl.pallas_call(
        functools.partial(_proj_body, hyc=hyc, aw=aw),
        grid=(n // tm,),
        in_specs=[
            pl.BlockSpec((tm, d), row),
            pl.BlockSpec((1, d), const),
            pl.BlockSpec(w_all.shape, const, pipeline_mode=pl.Buffered(1)),
            pl.BlockSpec((1, gd), const),
            pl.BlockSpec((tm, LANES), lambda i: (i % nseq, 0)),
            pl.BlockSpec((tm, LANES), lambda i: (i % nseq, 0)),
        ] + [pl.BlockSpec((tm, tm), const) for _ in perms],
        out_specs=([pl.BlockSpec((1, hyc // HY_CB, tm, HY_CB), lambda i: (i // nseq, 0, i % nseq, 0))]
                   + qkv_specs + [pl.BlockSpec((tm, gd), row)]),
        out_shape=([jax.ShapeDtypeStruct((batch, hyc // HY_CB, seq, HY_CB), F32)] + qkv_shapes
                   + [jax.ShapeDtypeStruct((n, gd), BF16)]),
        compiler_params=_cparams(("arbitrary",)),
        name="proj",
    )(x2d, g, w_all, b_gate, cos_t, sin_t, *perms)


def _filt_body(t_ref, grid_ref, bands_ref, w1t_ref, w1c_ref, w1s_ref, b1_ref, f1_ref,
               w2_ref, b2_ref, f2_ref, feat_ref):
    ang = bands_ref[...] * grid_ref[...]
    pre = (t_ref[...] * w1t_ref[...] + _dot_hi(jnp.cos(ang), w1c_ref[...])
           + _dot_hi(-jnp.sin(ang), w1s_ref[...]))
    h = jnp.sin(f1_ref[...] * (pre + b1_ref[...]))
    feat_ref[...] = jnp.sin(f2_ref[...] * (_dot_hi(h, w2_ref[...]) + b2_ref[...]))


def _filt(t_col, grid_col, bands, w1, b1, f1, w2, b2, f2):
    length = t_col.shape[0]
    nb = bands.shape[1]
    args = (t_col, grid_col, bands, w1[0:1], w1[1:1 + nb], w1[1 + nb:], b1[None], f1[None],
            w2, b2[None], f2[None])
    return pl.pallas_call(
        _filt_body,
        out_shape=jax.ShapeDtypeStruct((length, HY_FFN), F32),
        compiler_params=_cparams(None),
        name="filt",
    )(*args)


def _fft_tables(length):
    n = 2 * length
    n2 = FFT_N2
    n1 = n // n2
    h1 = n1 // 2
    k1 = np.arange(n1)[:, None]
    g_sig = np.zeros((n2, 2 * n1, 2 * h1))
    g_flt = np.zeros((n2, 2 * n1, 2 * h1))
    g_inv = np.zeros((n2, 2 * h1, 2 * n1))
    j = np.arange(h1)[None, :]
    for r in range(n2):
        th = 2 * np.pi * (k1 * (n2 * j + r) % n) / n
        gr, gi = np.cos(th), -np.sin(th)
        g_sig[r, 0::2, :h1] = gr
        g_sig[r, 0::2, h1:] = -gi
        g_sig[r, 1::2, :h1] = gi
        g_sig[r, 1::2, h1:] = gr
        g_flt[r, 0::2, :h1] = gr
        g_flt[r, 1::2, :h1] = gi
        m = n2 * (j + 1) - r
        thb = 2 * np.pi * (k1 * m % n) / n
        live = (m < length).astype(np.float64)
        g_flt[r, 0::2, h1:] = np.cos(thb) * live
        g_flt[r, 1::2, h1:] = np.sin(thb) * live
        wr, wi = (np.cos(th) / n).T, (np.sin(th) / n).T
        g_inv[r, :h1, 0::2] = wr
        g_inv[r, :h1, 1::2] = -wi
        g_inv[r, h1:, 0::2] = wi
        g_inv[r, h1:, 1::2] = wr
    a = np.arange(n2)
    th2 = 2 * np.pi * (np.outer(a, a) % n2) / n2
    c2, s2 = np.cos(th2), np.sin(th2)
    f_fwd = np.block([[c2, s2], [-s2, c2]])
    f_inv = np.block([[c2, -s2], [s2, c2]])
    cast = lambda z: jnp.asarray(z, dtype=F32).astype(BF16)
    return cast(g_sig), cast(g_flt), cast(g_inv), cast(f_fwd), cast(f_inv)


def _fft_stage1(src_a_ref, src_b_ref, tab_ref, work_ref, n1, is_filter):
    n2 = FFT_N2
    grp = SUBLANES

    def body(i, c):
        r0 = pl.multiple_of(i * grp, grp)
        b0 = pl.multiple_of(n2 - grp - r0, grp) if is_filter else r0
        a = jnp.swapaxes(src_a_ref[:, pl.ds(r0, grp), :], 0, 1)
        b = jnp.swapaxes(src_b_ref[:, pl.ds(b0, grp), :], 0, 1)
        outs = []
        for s in range(grp):
            rhs = jnp.concatenate([a[s], b[grp - 1 - s] if is_filter else b[s]], axis=0).astype(BF16)
            outs.append(_dot(tab_ref[r0 + s], rhs))
        work_ref[:, pl.ds(r0, grp), :] = jnp.swapaxes(jnp.stack(outs, axis=0), 0, 1)
        return c

    lax.fori_loop(0, n2 // grp, body, 0, unroll=2)


def _hyfilt_body(t_ref, ts_ref, delta_ref, feat_ref, feats_ref, w3f_ref, w3b_ref, gflt_ref, ffwd_ref, khat_ref,
                 hf_ref, hb_ref, work_ref, *, length, rc):
    n2 = FFT_N2
    n1 = 2 * length // n2
    tiles = rc // n2
    cb = hf_ref.shape[2]

    def gen(i, c):
        r0 = pl.multiple_of(i * rc, rc)
        j0 = pl.multiple_of(i * tiles, tiles)
        delta = jnp.abs(delta_ref[...])
        hf = _dot_split(feat_ref[pl.ds(r0, rc), :], w3f_ref[...]) * jnp.exp(-t_ref[pl.ds(r0, rc), :] * delta)
        hb = _dot_split(feats_ref[pl.ds(r0, rc), :], w3b_ref[...]) * jnp.exp(-ts_ref[pl.ds(r0, rc), :] * delta)
        hf_ref[pl.ds(j0, tiles), :n2, :] = hf.reshape(tiles, n2, cb)
        hb_ref[pl.ds(j0, tiles), :n2, :] = hb.reshape(tiles, n2, cb)
        return c

    lax.fori_loop(0, length // rc, gen, 0)
    _fft_stage1(hf_ref, hb_ref, gflt_ref, work_ref, n1, True)

    side = MXU_COLS // cb

    def stage2(i, c):
        k1s = [i * side + v for v in range(side)]
        blks = [work_ref[pl.ds(pl.multiple_of(2 * k1, 2), 2), :n2, :].reshape(2 * n2, cb).astype(BF16)
                for k1 in k1s]
        x = _dot(ffwd_ref[...], jnp.concatenate(blks, axis=1)).astype(BF16)
        for v, k1 in enumerate(k1s):
            khat_ref[0, 0, pl.ds(pl.multiple_of(k1 * (2 * n2), 2 * n2), 2 * n2), :] = x[:, v * cb:(v + 1) * cb]
        return c

    lax.fori_loop(0, n1 // side, stage2, 0, unroll=FFT_UNROLL // side)


def _hyfilt(t_col, delta, feat, w3, g_flt, f_fwd, rc=512):
    s = t_col.shape[0]
    cb = HY_CB
    ncb = HY_WIDTH // cb
    n2 = FFT_N2
    n1 = 2 * s // n2
    norder = w3.shape[1] // (2 * HY_WIDTH)
    one = pl.Buffered(1)

    def full(a):
        nd = a.ndim
        return pl.BlockSpec(a.shape, lambda o, c: (0,) * nd, pipeline_mode=one)

    shift = lambda a: jnp.concatenate([a[1:], jnp.zeros_like(a[:1])], axis=0)
    return pl.pallas_call(
        functools.partial(_hyfilt_body, length=s, rc=rc),
        grid=(norder, ncb),
        in_specs=[full(t_col), full(t_col),
                  pl.BlockSpec((1, cb), lambda o, c: (0, c)),
                  full(feat), full(feat),
                  pl.BlockSpec((HY_FFN, cb), lambda o, c: (0, 2 * ncb * o + c)),
                  pl.BlockSpec((HY_FFN, cb), lambda o, c: (0, 2 * ncb * o + ncb + c)),
                  full(g_flt), full(f_fwd)],
        out_specs=pl.BlockSpec((1, 1, 2 * n1 * n2, cb), lambda o, c: (o, c, 0, 0)),
        out_shape=jax.ShapeDtypeStruct((norder, ncb, 2 * n1 * n2, cb), BF16),
        scratch_shapes=[pltpu.VMEM((n1 // 2, n2 + FFT_PAD, cb), F32)] * 2
        + [pltpu.VMEM((2 * n1, n2 + FFT_PAD, cb), F32)],
        compiler_params=_cparams(("arbitrary", "arbitrary")),
        name="hyfilt",
    )(t_col, shift(t_col), delta, feat, shift(feat), w3, w3, g_flt, f_fwd)


def _hyena_body(pz_ref, phy_hbm, cwz_ref, cwg1_ref, cwg2_ref, cbz_ref, cbg1_ref, cbg2_ref,
                skip_ref, khat_hbm, gsig_ref, ginv_ref, ffwd_ref, finv_ref,
                out_ref,
                za_ref, zb_ref, ya_ref, yb_ref, work_ref, pg1_ref, pg2_ref, khat_ref, sem, ksem, *, length, rc):
    ncb = pl.num_programs(0)
    pair = pl.ds(2 * pl.program_id(1), 2)

    def gate_copy(order, dst_ref):
        src = phy_hbm.at[pair, pl.ds((order + 1) * ncb + pl.program_id(0), 1)]
        return pltpu.make_async_copy(src, dst_ref, sem.at[order])

    gate_copies = [gate_copy(0, pg1_ref), gate_copy(1, pg2_ref)]
    for cp in gate_copies:
        cp.start()

    new_block = pl.program_id(1) == 0
    spec_copies = [pltpu.make_async_copy(khat_hbm.at[o, pl.program_id(0)], khat_ref.at[o], ksem.at[o])
                   for o in range(khat_ref.shape[0])]

    @pl.when(new_block)
    def _():
        for cp in spec_copies:
            cp.start()
    n2 = FFT_N2
    n1 = 2 * length // n2
    h1 = n1 // 2
    tiles = rc // n2
    cb = za_ref.shape[2]
    side = MXU_COLS // cb
    sub = SUBLANES

    def conv3(p_ref, b, r0, w_ref, bias_ref):
        p = p_ref[b, 0, pl.ds(r0, rc), :]
        before = p_ref[b, 0, pl.ds(pl.multiple_of(jnp.maximum(r0 - sub, 0), sub), sub), :][sub - 1:sub, :]
        after = p_ref[b, 0, pl.ds(pl.multiple_of(jnp.minimum(r0 + rc, length - sub), sub), sub), :][0:1, :]
        before = jnp.where(r0 == 0, 0.0, before)
        after = jnp.where(r0 + rc == length, 0.0, after)
        row = lax.broadcasted_iota(jnp.int32, p.shape, 0)
        prev = jnp.where(row == 0, before, pltpu.roll(p, 1, 0))
        nxt = jnp.where(row == rc - 1, after, pltpu.roll(p, rc - 1, 0))
        return prev * w_ref[0:1, :] + p * w_ref[1:2, :] + nxt * w_ref[2:3, :] + bias_ref[...]

    def stage2(o):
        def body(i, c):
            k1s = [i * FFT_UNROLL + u for u in range(FFT_UNROLL)]
            blks = [work_ref[pl.ds(pl.multiple_of(2 * k1, 2), 2), :n2, :].reshape(2 * n2, cb).astype(BF16)
                    for k1 in k1s]
            outs = []
            for u in range(0, FFT_UNROLL, side):
                x = _dot(ffwd_ref[...], jnp.concatenate(blks[u:u + side], axis=1))
                ys = []
                for v in range(side):
                    k1 = k1s[u + v]
                    kh = khat_ref[o, pl.ds(pl.multiple_of(k1 * (2 * n2), 2 * n2), 2 * n2), :].astype(F32)
                    xr, xi = x[:n2, v * cb:(v + 1) * cb], x[n2:, v * cb:(v + 1) * cb]
                    kr, ki = kh[:n2], kh[n2:]
                    ys.append(jnp.concatenate([xr * kr - xi * ki, xr * ki + xi * kr], axis=0).astype(BF16))
                out = _dot(finv_ref[...], jnp.concatenate(ys, axis=1))
                outs.extend(out[:, v * cb:(v + 1) * cb] for v in range(side))
            for k1, out in zip(k1s, outs):
                work_ref[pl.ds(pl.multiple_of(2 * k1, 2), 2), :n2, :] = out.reshape(2, n2, cb)
            return c
        lax.fori_loop(0, n1 // FFT_UNROLL, body, 0)

    def stage3():
        grp = SUBLANES

        def body(i, c):
            r0 = pl.multiple_of(i * grp, grp)
            wk = jnp.swapaxes(work_ref[:, pl.ds(r0, grp), :], 0, 1)
            res = jnp.stack([_dot(ginv_ref[r0 + s], wk[s].astype(BF16)) for s in range(grp)], axis=0)
            ya_ref[:, pl.ds(r0, grp), :] = jnp.swapaxes(res[:, :h1], 0, 1)
            yb_ref[:, pl.ds(r0, grp), :] = jnp.swapaxes(res[:, h1:], 0, 1)
            return c
        lax.fori_loop(0, n2 // grp, body, 0, unroll=2)

    def rows3(ref, i):
        return ref.at[pl.ds(pl.multiple_of(i * tiles, tiles), tiles), :n2, :]

    def load_z(i, c):
        r0 = pl.multiple_of(i * rc, rc)
        rows3(za_ref, i)[...] = conv3(pz_ref, 0, r0, cwz_ref, cbz_ref).reshape(tiles, n2, cb)
        rows3(zb_ref, i)[...] = conv3(pz_ref, 1, r0, cwz_ref, cbz_ref).reshape(tiles, n2, cb)
        return c

    lax.fori_loop(0, length // rc, load_z, 0)
    for o, (pg_ref, cw_ref, cb_ref) in enumerate(((pg1_ref, cwg1_ref, cbg1_ref), (pg2_ref, cwg2_ref, cbg2_ref))):
        _fft_stage1(za_ref, zb_ref, gsig_ref, work_ref, n1, False)

        @pl.when(new_block)
        def _():
            spec_copies[o].wait()

        stage2(o)
        stage3()
        skip = skip_ref[o:o + 1, :]
        gate_copies[o].wait()

        def gate(i, c):
            r0 = pl.multiple_of(i * rc, rc)
            rows = pl.ds(r0, rc)
            ya, yb = (rows3(r, i)[...].reshape(rc, cb) for r in (ya_ref, yb_ref))
            za, zb = (rows3(r, i)[...].reshape(rc, cb) for r in (za_ref, zb_ref))
            new_a = conv3(pg_ref, 0, r0, cw_ref, cb_ref) * (ya + skip * za)
            new_b = conv3(pg_ref, 1, r0, cw_ref, cb_ref) * (yb + skip * zb)
            if o == 0:
                rows3(za_ref, i)[...] = new_a.reshape(tiles, n2, cb)
                rows3(zb_ref, i)[...] = new_b.reshape(tiles, n2, cb)
            else:
                out_ref[0, rows, :] = new_a.astype(out_ref.dtype)
                out_ref[1, rows, :] = new_b.astype(out_ref.dtype)
            return c

        lax.fori_loop(0, length // rc, gate, 0)


def _hyena(p_hy, conv_w, conv_b, skip, khat, tables, rc=512):
    b, _, s, _ = p_hy.shape
    w = HY_WIDTH
    cb = HY_CB
    ncb = w // cb
    n2 = FFT_N2
    n1 = 2 * s // n2
    g_sig, _, g_inv, f_fwd, f_inv = tables
    one = pl.Buffered(1)

    def cwspec(off):
        return pl.BlockSpec((3, cb), lambda c, p: (0, off + c))

    def cbspec(off):
        return pl.BlockSpec((1, cb), lambda c, p: (0, off + c))

    def full(a):
        nd = a.ndim
        return pl.BlockSpec(a.shape, lambda c, p: (0,) * nd, pipeline_mode=one)

    in_specs = [
        pl.BlockSpec((2, 1, s, cb), lambda c, p: (p, c, 0, 0)),
        pl.BlockSpec(memory_space=pl.ANY),
        cwspec(0), cwspec(ncb), cwspec(2 * ncb),
        cbspec(0), cbspec(ncb), cbspec(2 * ncb),
        pl.BlockSpec((2, cb), lambda c, p: (0, c)),
        pl.BlockSpec(memory_space=pl.ANY),
        full(g_sig), full(g_inv), full(f_fwd), full(f_inv),
    ]
    return pl.pallas_call(
        functools.partial(_hyena_body, length=s, rc=rc),
        grid=(ncb, b // 2),
        in_specs=in_specs,
        out_specs=pl.BlockSpec((2, s, cb), lambda c, p: (p, 0, c)),
        out_shape=jax.ShapeDtypeStruct((b, s, w), BF16),
        scratch_shapes=[pltpu.VMEM((n1 // 2, n2 + FFT_PAD, cb), F32)] * 4
        + [pltpu.VMEM((2 * n1, n2 + FFT_PAD, cb), F32)]
        + [pltpu.VMEM((2, 1, s, cb), F32)] * 2 + [pltpu.VMEM((khat.shape[0], 2 * n1 * n2, cb), BF16)]
        + [pltpu.SemaphoreType.DMA((2,)), pltpu.SemaphoreType.DMA((khat.shape[0],))],
        compiler_params=_cparams(("arbitrary", "arbitrary")),
        name="hyena",
    )(p_hy, p_hy, conv_w, conv_w, conv_w, conv_b, conv_b, conv_b,
      skip, khat, g_sig, g_inv, f_fwd, f_inv)


def _attn_body(q_ref, k_ref, v_ref, o_ref, lse_ref, *, n_side, qb):
    lr = q_ref.shape[2]
    kw = qb + 2 * n_side
    nh = HEADS_PER_GROUP
    lane_head = lax.broadcasted_iota(jnp.int32, (qb, ATTN_OUT), 1) // HEAD_DIM
    qi = lax.broadcasted_iota(jnp.int32, (nh * qb, kw), 0) % qb
    ki = lax.broadcasted_iota(jnp.int32, (nh * qb, kw), 1)
    band = [jnp.where(jnp.abs(ki - qi - shift) <= n_side, 0.0, -1e30) for shift in (0, n_side, 2 * n_side)]

    def body(cls, i, c):
        q0 = pl.multiple_of(i * qb, qb)
        w0 = pl.multiple_of(jnp.clip(q0 - n_side, 0, lr - kw), n_side)
        q = q_ref[0, cls, pl.ds(q0, qb), :]
        kwin = k_ref[0, cls, pl.ds(w0, kw), :]
        vwin = v_ref[0, cls, pl.ds(w0, kw), :]
        shift = q0 - w0
        bias = jnp.where(shift == n_side, band[1], jnp.where(shift == 0, band[0], band[2]))
        qs = jnp.concatenate([jnp.where(lane_head == h, q, jnp.zeros_like(q)) for h in range(nh)], axis=0)
        s = _dot_t(qs, kwin) + bias
        m = jnp.max(s, axis=-1, keepdims=True)
        p = jnp.exp(s - m)
        l = jnp.sum(p, axis=-1, keepdims=True)
        pv = _dot(p.astype(BF16), vwin) / l
        lse_rows = m + jnp.log(l)
        o = jnp.zeros((qb, ATTN_OUT), F32)
        lse = jnp.zeros((qb, ATTN_OUT), F32)
        for h in range(nh):
            mine = lane_head == h
            o = jnp.where(mine, pv[h * qb:(h + 1) * qb], o)
            lse = jnp.where(mine, lse_rows[h * qb:(h + 1) * qb], lse)
        o_ref[0, pl.ds(q0, qb), cls * ATTN_OUT:(cls + 1) * ATTN_OUT] = o
        lse_ref[0, pl.ds(q0, qb), cls * ATTN_OUT:(cls + 1) * ATTN_OUT] = lse
        return c

    for cls in range(q_ref.shape[1]):
        lax.fori_loop(0, lr // qb, functools.partial(body, cls), 0, unroll=2)


def _attn_group(q, k, v, g, n_side, qb=128, max_classes=8):
    b, dil, lr, _ = q.shape
    cg = min(dil, max_classes)
    in_spec = pl.BlockSpec((1, cg, lr, ATTN_OUT), lambda bi, r: (bi, r, 0, 0))
    out_spec = pl.BlockSpec((1, lr, cg * ATTN_OUT), lambda bi, r: (bi, 0, r))
    o, lse = pl.pallas_call(
        functools.partial(_attn_body, n_side=n_side, qb=qb),
        grid=(b, dil // cg),
        in_specs=[in_spec, in_spec, in_spec],
        out_specs=[out_spec, out_spec],
        out_shape=[jax.ShapeDtypeStruct((b, lr, dil * ATTN_OUT), F32)] * 2,
        compiler_params=_cparams(("arbitrary", "arbitrary")),
        name=f"attn{g}",
    )(q, k, v)
    return o, lse


def _memkv_body(mem_ref, g_ref, w_ref, kv_ref):
    mn = _rms(mem_ref[0], g_ref[...]).astype(BF16)
    kv_ref[0] = _dot(mn, w_ref[...]).astype(BF16)


def _memkv(mem, g, w_kv):
    b, m, d = mem.shape
    n = w_kv.shape[1]
    return pl.pallas_call(
        _memkv_body,
        grid=(b,),
        in_specs=[pl.BlockSpec((1, m, d), lambda i: (i, 0, 0)),
                  pl.BlockSpec((1, d), lambda i: (0, 0)),
                  pl.BlockSpec((d, n), lambda i: (0, 0))],
        out_specs=pl.BlockSpec((1, m, n), lambda i: (i, 0, 0)),
        out_shape=jax.ShapeDtypeStruct((b, m, n), BF16),
        compiler_params=_cparams(("arbitrary",)),
        name="memkv",
    )(mem, g, w_kv)


def _merge_body(x_ref, z_ref, o0_ref, o1_ref, o2_ref, l0_ref, l1_ref, l2_ref, gate_ref,
                wuh_ref, wua_ref, wout_ref, gx_ref, wq_ref, kv_ref, wo_ref, gm_ref, wr_ref,
                p1_ref, p2_ref, x2_ref, hm_ref, aff_ref, affh_ref, affl_ref, *, nsplit):
    rows = x_ref.shape[0] // nsplit
    for part in range(nsplit):
        sl = slice(part * rows, (part + 1) * rows)

        def natural(ref, perm_ref):
            dil = ref.shape[2] // ATTN_OUT
            rr = rows // dil
            blk = ref[0, part * rr:(part + 1) * rr, :]
            if dil == 1:
                return blk
            pieces = [blk[:, r * ATTN_OUT:(r + 1) * ATTN_OUT] for r in range(dil)]
            if rr % SUBLANES == 0 and dil % SUBLANES == 0:
                return jnp.swapaxes(jnp.stack(pieces, axis=0), 0, 1).reshape(rows, ATTN_OUT)
            hi, lo = _split(jnp.concatenate(pieces, axis=0))
            return _dot(perm_ref[...], hi) + _dot(perm_ref[...], lo)

        outs = [natural(r, p) for r, p in ((o0_ref, None), (o1_ref, p1_ref), (o2_ref, p2_ref))]
        lses = [natural(r, p) for r, p in ((l0_ref, None), (l1_ref, p1_ref), (l2_ref, p2_ref))]
        _merge_rows(sl, outs, lses, x_ref, z_ref, gate_ref,
                    wuh_ref, wua_ref, wout_ref, gx_ref, wq_ref, kv_ref, wo_ref, gm_ref, wr_ref,
                    x2_ref, hm_ref, aff_ref, affh_ref, affl_ref)


def _merge_rows(sl, outs, lses, x_ref, z_ref, gate_ref,
                wuh_ref, wua_ref, wout_ref, gx_ref, wq_ref, kv_ref, wo_ref, gm_ref, wr_ref,
                x2_ref, hm_ref, aff_ref, affh_ref, affl_ref):
    d = x_ref.shape[1]
    l0, l1, l2 = lses
    mx = jnp.maximum(jnp.maximum(l0, l1), l2)
    e0, e1, e2 = jnp.exp(l0 - mx), jnp.exp(l1 - mx), jnp.exp(l2 - mx)
    attn = (e0 * outs[0] + e1 * outs[1] + e2 * outs[2]) / (e0 + e1 + e2)
    y_hy = _dot(z_ref[sl, :].astype(BF16), wuh_ref[...])
    y_at = _dot(attn.astype(BF16), wua_ref[...])
    gates = gate_ref[sl, :].astype(F32)
    mix = gates[:, :d] * y_hy + gates[:, d:] * y_at
    x1 = x_ref[sl, :] + _dot(mix.astype(BF16), wout_ref[...])

    hx = _rms(x1, gx_ref[...]).astype(BF16)
    qx = _dot(hx, wq_ref[...]).astype(BF16)
    kv = kv_ref[0]
    xw = X_HEADS * X_HEAD_DIM
    heads = []
    for h in range(X_HEADS):
        hc = slice(h * X_HEAD_DIM, (h + 1) * X_HEAD_DIM)
        s = _dot_t(qx[:, hc], kv[:, hc]) * (X_HEAD_DIM ** -0.5)
        m = jnp.max(s, axis=-1, keepdims=True)
        p = jnp.exp(s - m)
        p = p / jnp.sum(p, axis=-1, keepdims=True)
        heads.append(_dot(p.astype(BF16), kv[:, xw + h * X_HEAD_DIM:xw + (h + 1) * X_HEAD_DIM]))
    ox = jnp.concatenate(heads, axis=-1).astype(BF16)
    x2 = x1 + _dot(ox, wo_ref[...])
    x2_ref[sl, :] = x2

    hm = _rms(x2, gm_ref[...])
    hm_hi, hm_lo = _split(hm)
    hm_ref[sl, :] = hm_hi
    nr = hm.shape[0]
    cross = _dot(jnp.concatenate([hm_hi, hm_lo], axis=0), wr_ref[...])
    logits = (cross[:nr, :LANES] + cross[:nr, LANES:]) + (cross[nr:, :LANES] + cross[nr:, LANES:])
    lane = lax.broadcasted_iota(jnp.int32, logits.shape, 1)
    logits = jnp.where(lane < N_EXPERTS, logits, -1e30)
    m = jnp.max(logits, axis=-1, keepdims=True)
    p = jnp.exp(logits - m)
    aff = p / jnp.sum(p, axis=-1, keepdims=True)
    aff_ref[sl, :] = aff
    affh_ref[sl, :], affl_ref[sl, :] = _split(aff)


def _merge_staged_body(x_ref, z_ref, o0_ref, o1_ref, o2_ref, l0_ref, l1_ref, l2_ref, gate_ref,
                       wuh_ref, wua_ref, wout_ref, gx_ref, wq_ref, kv_ref, wo_ref, gm_ref, wr_ref,
                       p1_ref, p2_ref, x2_ref, hm_ref, aff_ref, affh_ref, affl_ref, *, nsplit):
    d = x_ref.shape[1]
    rows = x_ref.shape[0] // nsplit
    parts = range(nsplit)
    sls = [slice(p * rows, (p + 1) * rows) for p in parts]

    def natural(ref, perm_ref, part):
        dil = ref.shape[2] // ATTN_OUT
        rr = rows // dil
        blk = ref[0, part * rr:(part + 1) * rr, :]
        if dil == 1:
            return blk
        pieces = [blk[:, r * ATTN_OUT:(r + 1) * ATTN_OUT] for r in range(dil)]
        if rr % SUBLANES == 0 and dil % SUBLANES == 0:
            return jnp.swapaxes(jnp.stack(pieces, axis=0), 0, 1).reshape(rows, ATTN_OUT)
        hi, lo = _split(jnp.concatenate(pieces, axis=0))
        return _dot(perm_ref[...], hi) + _dot(perm_ref[...], lo)

    def combine(part):
        o0, o1, o2 = (natural(r, p, part) for r, p in ((o0_ref, None), (o1_ref, p1_ref), (o2_ref, p2_ref)))
        l0, l1, l2 = (natural(r, p, part) for r, p in ((l0_ref, None), (l1_ref, p1_ref), (l2_ref, p2_ref)))
        mx = jnp.maximum(jnp.maximum(l0, l1), l2)
        e0, e1, e2 = jnp.exp(l0 - mx), jnp.exp(l1 - mx), jnp.exp(l2 - mx)
        return ((e0 * o0 + e1 * o1 + e2 * o2) / (e0 + e1 + e2)).astype(BF16)

    attn = [combine(p) for p in parts]
    y_hy = [_dot(z_ref[sl, :].astype(BF16), wuh_ref[...]) for sl in sls]
    y_at = [_dot(a, wua_ref[...]) for a in attn]

    def mixed(sl, yh, ya):
        gates = gate_ref[sl, :].astype(F32)
        return (gates[:, :d] * yh + gates[:, d:] * ya).astype(BF16)

    mix = [mixed(sl, yh, ya) for sl, yh, ya in zip(sls, y_hy, y_at)]
    x1 = [x_ref[sl, :] + _dot(m, wout_ref[...]) for sl, m in zip(sls, mix)]

    hx = [_rms(v, gx_ref[...]).astype(BF16) for v in x1]
    qx = [_dot(h, wq_ref[...]).astype(BF16) for h in hx]
    kv = kv_ref[0]
    xw = X_HEADS * X_HEAD_DIM

    def cross_attention(q):
        heads = []
        for h in range(X_HEADS):
            hc = slice(h * X_HEAD_DIM, (h + 1) * X_HEAD_DIM)
            s = _dot_t(q[:, hc], kv[:, hc]) * (X_HEAD_DIM ** -0.5)
            m = jnp.max(s, axis=-1, keepdims=True)
            p = jnp.exp(s - m)
            p = p / jnp.sum(p, axis=-1, keepdims=True)
            heads.append(_dot(p.astype(BF16), kv[:, xw + h * X_HEAD_DIM:xw + (h + 1) * X_HEAD_DIM]))
        return jnp.concatenate(heads, axis=-1).astype(BF16)

    ox = [cross_attention(q) for q in qx]
    x2 = [v + _dot(o, wo_ref[...]) for v, o in zip(x1, ox)]
    for sl, v in zip(sls, x2):
        x2_ref[sl, :] = v

    hm = [_split(_rms(v, gm_ref[...])) for v in x2]
    cross = [_dot(jnp.concatenate([hi, lo], axis=0), wr_ref[...]) for hi, lo in hm]
    for sl, (hi, _), c in zip(sls, hm, cross):
        hm_ref[sl, :] = hi
        logits = (c[:rows, :LANES] + c[:rows, LANES:]) + (c[rows:, :LANES] + c[rows:, LANES:])
        lane = lax.broadcasted_iota(jnp.int32, logits.shape, 1)
        logits = jnp.where(lane < N_EXPERTS, logits, -1e30)
        m = jnp.max(logits, axis=-1, keepdims=True)
        p = jnp.exp(logits - m)
        aff = p / jnp.sum(p, axis=-1, keepdims=True)
        aff_ref[sl, :] = aff
        affh_ref[sl, :], affl_ref[sl, :] = _split(aff)


def _merge(x2d, z2d, outs, lses, gates, wuh, wua, wout, gx, wq, kv, wo, gm, wr_cat, seq, tm=512, nsplit=2):
    n, d = x2d.shape
    nseq = seq // tm
    row = lambda i: (i, 0)
    const = lambda i: (0, 0)

    def rspec(a):
        return pl.BlockSpec((tm, a.shape[1]), row)

    def aspec(a):
        dil = a.shape[2] // ATTN_OUT
        return pl.BlockSpec((1, tm // dil, a.shape[2]), lambda i: (i // nseq, i % nseq, 0))

    def cspec(a):
        return pl.BlockSpec(a.shape, const)

    rows = tm // nsplit
    perms = []
    for a in outs[1:]:
        dil = a.shape[2] // ATTN_OUT
        src = (np.arange(rows) % dil) * (rows // dil) + np.arange(rows) // dil
        perms.append(jnp.asarray(np.eye(rows)[src], dtype=BF16))

    in_arrays = [x2d, z2d, *outs, *lses, gates, wuh, wua, wout, gx, wq, kv, wo, gm, wr_cat, *perms]
    in_specs = ([rspec(a) for a in in_arrays[:2]] + [aspec(a) for a in in_arrays[2:8]] + [rspec(gates)]
                + [cspec(a) for a in in_arrays[9:14]]
                + [pl.BlockSpec((1,) + kv.shape[1:], lambda i: (i // nseq, 0, 0))]
                + [cspec(a) for a in in_arrays[15:]])
    return pl.pallas_call(
        functools.partial(_merge_staged_body, nsplit=nsplit),
        grid=(n // tm,),
        in_specs=in_specs,
        out_specs=[pl.BlockSpec((tm, d), row), pl.BlockSpec((tm, d), row)] + [pl.BlockSpec((tm, LANES), row)] * 3,
        out_shape=[jax.ShapeDtypeStruct((n, d), F32), jax.ShapeDtypeStruct((n, d), BF16),
                   jax.ShapeDtypeStruct((n, LANES), F32), jax.ShapeDtypeStruct((n, LANES), BF16),
                   jax.ShapeDtypeStruct((n, LANES), BF16)],
        compiler_params=_cparams(("arbitrary",)),
        name="merge",
    )(*in_arrays)


def _topk_body(aff_ref, tri_ref, rank_ref, bounds_ref, *, cap, tok_block):
    a = aff_ref[0]
    e, s = a.shape

    def count(mask):
        return jnp.sum(jnp.where(mask, 1.0, 0.0), axis=-1, keepdims=True)

    def as_float(bits):
        return pltpu.bitcast(jnp.broadcast_to(bits, (e, LANES)), F32)[:, 0:1]

    def search(i, thr):
        cand = thr | (jnp.int32(1) << (30 - i))
        return jnp.where(count(a >= as_float(cand)) >= cap, cand, thr)

    thr = as_float(lax.fori_loop(0, 31, search, jnp.zeros((e, 1), jnp.int32)))
    gt = a > thr
    eq = a == thr
    need = cap - count(gt)

    def prefix_excl(mask):
        mf = jnp.where(mask, 1.0, 0.0)
        parts = []
        carry = jnp.zeros((e, 1), F32)
        for c in range(s // LANES):
            blk = mf[:, c * LANES:(c + 1) * LANES]
            inc = _dot(blk.astype(BF16), tri_ref[...])
            parts.append(inc - blk + carry)
            carry = carry + inc[:, LANES - 1:LANES]
        return jnp.concatenate(parts, axis=-1)

    sel = gt | (eq & (prefix_excl(eq) < need))
    excl = prefix_excl(sel)
    rank_ref[0] = jnp.where(sel, excl, -1.0)

    tok = lax.broadcasted_iota(jnp.int32, (e, s), 1)
    lane = lax.broadcasted_iota(jnp.int32, (e, LANES), 1)
    bounds = jnp.zeros((e, LANES), F32)
    for j in range(s // tok_block + 1):
        bounds = jnp.where(lane == j, count(sel & (tok < j * tok_block)), bounds)
    bounds_ref[0] = bounds.astype(jnp.int32)


def _topk(aff_t, cap, tok_block):
    b, e, s = aff_t.shape
    assert s // tok_block + 1 <= LANES
    tri = jnp.asarray(np.triu(np.ones((LANES, LANES))), dtype=BF16)
    return pl.pallas_call(
        functools.partial(_topk_body, cap=cap, tok_block=tok_block),
        grid=(b,),
        in_specs=[pl.BlockSpec((1, e, s), lambda i: (i, 0, 0)),
                  pl.BlockSpec((LANES, LANES), lambda i: (0, 0))],
        out_specs=[pl.BlockSpec((1, e, s), lambda i: (i, 0, 0)),
                   pl.BlockSpec((1, e, LANES), lambda i: (i, 0, 0))],
        out_shape=[jax.ShapeDtypeStruct((b, e, s), F32),
                   jax.ShapeDtypeStruct((b, e, LANES), jnp.int32)],
        compiler_params=_cparams(("arbitrary",)),
        name="topk",
    )(aff_t, tri)


def _moe_windows(bounds_ref, base, ne, step, tokens, window):
    per = tokens // MOE_BOUND_STEP
    starts = []
    passes = jnp.int32(0)
    for e in range(ne):
        r_lo = bounds_ref[base + e * LANES + step * per]
        r_hi = bounds_ref[base + e * LANES + (step + 1) * per]
        ws = (r_lo // BF16_ROWS) * BF16_ROWS
        starts.append(ws)
        passes = jnp.maximum(passes, (r_hi - ws + window - 1) // window)
    return starts, passes


def _gather_body(bounds_ref, rank_ref, hm_ref, affh_ref, affl_ref, xe_ref, gs_ref, *, cap, eg):
    w = GATHER_WINDOW
    ne, tc = rank_ref.shape[1], rank_ref.shape[2]
    bi = pl.program_id(0)
    ci = pl.program_id(1)

    @pl.when(ci == 0)
    def _():
        xe_ref[...] = jnp.zeros_like(xe_ref)
        gs_ref[...] = jnp.zeros_like(gs_ref)

    starts, passes = _moe_windows(bounds_ref, bi * ne * LANES, ne, ci, tc, w)
    iota_w = lax.broadcasted_iota(jnp.int32, (w, 1), 0)

    def one_pass(p, carry):
        for g0 in range(0, ne, eg):
            lo = [starts[e] + p * w for e in range(g0, g0 + eg)]
            phys = [pl.multiple_of(jnp.minimum(v, cap - w), BF16_ROWS) for v in lo]
            rk = jnp.concatenate([jnp.broadcast_to(rank_ref[0, e:e + 1, :], (w, tc))
                                  for e in range(g0, g0 + eg)], axis=0)
            slot = jnp.concatenate([iota_w + v for v in phys], axis=0).astype(F32)
            lom = jnp.concatenate([jnp.zeros_like(iota_w) + v for v in lo], axis=0).astype(F32)
            onehot = jnp.where((rk == slot) & (slot >= lom), 1.0, 0.0).astype(BF16)
            res = _dot(onehot, hm_ref[0])
            resg = _dot(onehot, affh_ref[0]) + _dot(onehot, affl_ref[0])
            for k, e in enumerate(range(g0, g0 + eg)):
                rows = pl.ds(phys[k], w)
                xe_ref[0, e, rows, :] = (xe_ref[0, e, rows, :].astype(F32) + res[k * w:(k + 1) * w]).astype(BF16)
                gs_ref[0, e, rows, :] = gs_ref[0, e, rows, :] + resg[k * w:(k + 1) * w]
        return carry

    lax.fori_loop(0, passes, one_pass, 0)


def _gather(bounds, rank, hm, aff_hi, aff_lo, cap, eg=16):
    b, e, s = rank.shape
    d = hm.shape[2]
    tc = GATHER_TOKENS
    return pl.pallas_call(
        functools.partial(_gather_body, cap=cap, eg=eg),
        grid_spec=pltpu.PrefetchScalarGridSpec(
            num_scalar_prefetch=1,
            grid=(b, s // tc),
            in_specs=[pl.BlockSpec((1, e, tc), lambda bi, ci, bnd: (bi, 0, ci)),
                      pl.BlockSpec((1, tc, d), lambda bi, ci, bnd: (bi, ci, 0)),
                      pl.BlockSpec((1, tc, LANES), lambda bi, ci, bnd: (bi, ci, 0)),
                      pl.BlockSpec((1, tc, LANES), lambda bi, ci, bnd: (bi, ci, 0))],
            out_specs=[pl.BlockSpec((1, e, cap, d), lambda bi, ci, bnd: (bi, 0, 0, 0)),
                       pl.BlockSpec((1, e, cap, LANES), lambda bi, ci, bnd: (bi, 0, 0, 0))],
        ),
        out_shape=[jax.ShapeDtypeStruct((b, e, cap, d), BF16),
                   jax.ShapeDtypeStruct((b, e, cap, LANES), F32)],
        compiler_params=_cparams(("arbitrary", "arbitrary")),
        name="gather",
    )(bounds.reshape(-1), rank, hm, aff_hi, aff_lo)


def _ffn_body(xe_ref, gs_ref, wg_ref, wu_ref, wd_ref, ye_ref, acc_ref, *, mb):
    f = pl.program_id(1)
    nb, _, cap, d = xe_ref.shape
    wg = wg_ref[0].astype(BF16)
    wu = wu_ref[0].astype(BF16)
    wd = wd_ref[0].astype(BF16)

    @pl.when(f == 0)
    def _():
        acc_ref[...] = jnp.zeros_like(acc_ref)

    for b0 in range(0, nb, mb):
        xe = xe_ref[b0:b0 + mb, 0].reshape(mb * cap, d)
        a = _dot(xe, wg)
        u = _dot(xe, wu)
        hsw = (a * jax.nn.sigmoid(a) * u).astype(BF16)
        acc_ref[b0:b0 + mb] += _dot(hsw, wd).reshape(mb, cap, d)

    @pl.when(f == pl.num_programs(1) - 1)
    def _():
        lane = lax.broadcasted_iota(jnp.int32, gs_ref.shape[2:], 1)
        mine = lane == pl.program_id(0)
        for b in range(nb):
            gate = jnp.sum(jnp.where(mine, gs_ref[b, 0], 0.0), axis=-1, keepdims=True)
            ye_ref[b, 0] = (acc_ref[b] * gate).astype(BF16)


def _ffn(xe, gs, wg, wu, wd, ft=1024, mb=1):
    b, e, cap, d = xe.shape
    ff = wg.shape[2]
    return pl.pallas_call(
        functools.partial(_ffn_body, mb=mb),
        grid=(e, ff // ft),
        in_specs=[pl.BlockSpec((b, 1, cap, d), lambda ei, fi: (0, ei, 0, 0)),
                  pl.BlockSpec((b, 1, cap, LANES), lambda ei, fi: (0, ei, 0, 0)),
                  pl.BlockSpec((1, d, ft), lambda ei, fi: (ei, 0, fi)),
                  pl.BlockSpec((1, d, ft), lambda ei, fi: (ei, 0, fi)),
                  pl.BlockSpec((1, ft, d), lambda ei, fi: (ei, fi, 0))],
        out_specs=pl.BlockSpec((b, 1, cap, d), lambda ei, fi: (0, ei, 0, 0)),
        out_shape=jax.ShapeDtypeStruct((b, e, cap, d), BF16),
        scratch_shapes=[pltpu.VMEM((b, cap, d), F32)],
        compiler_params=_cparams(("arbitrary", "arbitrary"), vmem=VMEM_LIMIT_LARGE),
        name="ffn",
    )(xe, gs, wg, wu, wd)


def _scatter_body(bounds_ref, rank_ref, ye_ref, x2_ref, g_ref, out_ref, acc_ref, *, eg):
    w = SCATTER_WINDOW
    ts = x2_ref.shape[1]
    ne, cap = ye_ref.shape[1], ye_ref.shape[2]
    starts, passes = _moe_windows(bounds_ref, pl.program_id(0) * ne * LANES, ne, pl.program_id(1), ts, w)
    rank = rank_ref[0]
    acc_ref[...] = x2_ref[0]
    iota_w = lax.broadcasted_iota(jnp.int32, (1, w), 1)

    def one_pass(p, carry):
        for g0 in range(0, ne, eg):
            onehots, rows = [], []
            for e in range(g0, g0 + eg):
                lo = starts[e] + p * w
                phys = pl.multiple_of(jnp.minimum(lo, cap - w), BF16_ROWS)
                slot = (iota_w + phys).astype(F32)
                hit = (rank[:, e:e + 1] == slot) & (slot >= lo.astype(F32))
                onehots.append(jnp.where(hit, 1.0, 0.0).astype(BF16))
                rows.append(ye_ref[0, e, pl.ds(phys, w), :])
            acc_ref[...] += _dot(jnp.concatenate(onehots, axis=1), jnp.concatenate(rows, axis=0))
        return carry

    lax.fori_loop(0, passes, one_pass, 0)
    out_ref[0] = _rms(acc_ref[...], g_ref[...])


def _scatter(bounds, rank_t, ye, x2, g, eg=8):
    b, s, e = rank_t.shape
    cap, d = ye.shape[2], ye.shape[3]
    ts = SCATTER_TOKENS
    return pl.pallas_call(
        functools.partial(_scatter_body, eg=eg),
        grid_spec=pltpu.PrefetchScalarGridSpec(
            num_scalar_prefetch=1,
            grid=(b, s // ts),
            in_specs=[pl.BlockSpec((1, ts, e), lambda bi, ti, bnd: (bi, ti, 0)),
                      pl.BlockSpec((1, e, cap, d), lambda bi, ti, bnd: (bi, 0, 0, 0)),
                      pl.BlockSpec((1, ts, d), lambda bi, ti, bnd: (bi, ti, 0)),
                      pl.BlockSpec((1, d), lambda bi, ti, bnd: (0, 0))],
            out_specs=pl.BlockSpec((1, ts, d), lambda bi, ti, bnd: (bi, ti, 0)),
            scratch_shapes=[pltpu.VMEM((ts, d), F32)],
        ),
        out_shape=jax.ShapeDtypeStruct((b, s, d), F32),
        compiler_params=_cparams(("arbitrary", "arbitrary")),
        name="scatter",
    )(bounds.reshape(-1), rank_t, ye, x2, g)


def kernel(x, mem, ln_mix_g, w_in, hy_conv_w, hy_conv_b, filt_w1, filt_b1, filt_freq1, filt_w2, filt_b2,
           filt_freq2, filt_w3, hy_skip, w_up_hy, w_up_attn, w_gate, b_gate, w_out, ln_x_g, ln_mem_g,
           w_q_x, w_kv_mem, w_o_x, ln_moe_g, w_router, w_e_gate, w_e_up, w_e_down, ln_f_g):
    b, s, d = x.shape
    n = b * s
    x2d = x.reshape(n, d)

    half = HEAD_DIM // 2
    inv = ROPE_THETA ** (-jnp.arange(0, HEAD_DIM, 2, dtype=F32) / HEAD_DIM)
    ang = jnp.arange(s, dtype=F32)[:, None] * inv[None, :]
    lane = np.arange(LANES)
    cos_t = jnp.cos(ang)[:, lane % half]
    sin_t = jnp.sin(ang)[:, lane % half] * jnp.asarray(np.where(lane % HEAD_DIM < half, -1.0, 1.0), F32)[None, :]

    w_all = jnp.concatenate([w_in, w_gate], axis=1).astype(BF16)
    p_hy, *qkv, gates = _proj(x2d, ln_mix_g[None], w_all, b_gate[None], cos_t, sin_t, b, s)

    t_col = jnp.linspace(0.0, 1.0, s, dtype=F32)[:, None]
    grid_col = 2.0 * math.pi * jnp.arange(s, dtype=F32)[:, None] / s
    bands = jnp.linspace(1e-4, HY_BANDS - 1, HY_BANDS, dtype=F32)[None, :]
    feat = _filt(t_col, grid_col, bands, filt_w1, filt_b1, filt_freq1, filt_w2, filt_b2, filt_freq2)
    delta = jnp.linspace(math.log(HY_TARGET) / HY_SLOW_PCT, math.log(HY_TARGET) / HY_FAST_PCT,
                         HY_WIDTH, dtype=F32)[None, :]
    tables = _fft_tables(s)
    khat = _hyfilt(t_col, delta, feat, filt_w3, tables[1], tables[3])
    z = _hyena(p_hy, hy_conv_w, hy_conv_b[None], hy_skip, khat, tables)

    outs, lses = [], []
    for g, (window, dil) in enumerate(DIL_PAIRS):
        o_g, l_g = _attn_group(*qkv[3 * g:3 * g + 3], g, window // (2 * dil))
        outs.append(o_g)
        lses.append(l_g)

    kv = _memkv(mem, ln_mem_g[None], w_kv_mem.astype(BF16))
    wr_pad = jnp.pad(w_router, ((0, 0), (0, LANES - N_EXPERTS)))
    wr_cat = jnp.concatenate(_split(wr_pad), axis=1)
    x2, hm, aff, aff_hi, aff_lo = _merge(x2d, z.reshape(n, HY_WIDTH), outs, lses, gates,
                         w_up_hy.astype(BF16), w_up_attn.astype(BF16), w_out.astype(BF16),
                         ln_x_g[None], w_q_x.astype(BF16), kv, w_o_x.astype(BF16), ln_moe_g[None],
                         wr_cat, s)

    cap = max(1, EC_FACTOR * s // N_EXPERTS)
    aff3 = aff.reshape(b, s, LANES)
    rank, bounds = _topk(aff3[:, :, :N_EXPERTS].transpose(0, 2, 1), cap, MOE_BOUND_STEP)
    xe, gs = _gather(bounds, rank, hm.reshape(b, s, d), aff_hi.reshape(b, s, LANES), aff_lo.reshape(b, s, LANES), cap)
    ye = _ffn(xe, gs, w_e_gate, w_e_up, w_e_down)
    return _scatter(bounds, rank.transpose(0, 2, 1), ye, x2.reshape(b, s, d), ln_f_g[None])
```

```python
import functools
import math

import numpy as np
import jax
import jax.numpy as jnp
from jax import lax
from jax.experimental import pallas as pl
from jax.experimental.pallas import tpu as pltpu

F32 = jnp.float32
BF16 = jnp.bfloat16
HIGHEST = lax.Precision.HIGHEST

EPS = 1e-6
HY_WIDTH = 768
HY_BANDS = 16
HY_FFN = 64
HY_FAST_PCT = 0.3
HY_SLOW_PCT = 1.5
HY_TARGET = 1e-2
HEAD_DIM = 64
HEADS_PER_GROUP = 4
DIL_PAIRS = ((128, 1), (512, 4), (2048, 16))
N_GROUPS = len(DIL_PAIRS)
ATTN_WIDTH = N_GROUPS * HEADS_PER_GROUP * HEAD_DIM
ATTN_OUT = HEADS_PER_GROUP * HEAD_DIM
ROPE_THETA = 10000.0
X_HEADS = 4
X_HEAD_DIM = 128
N_EXPERTS = 16
EC_FACTOR = 2

LANES = 128
SUBLANES = 8
MXU_COLS = 256
VMEM_LIMIT = 56 * 1024 * 1024
VMEM_LIMIT_LARGE = 60 * 1024 * 1024

FFT_N2 = 128
HY_CB = 128
ATTN_BLOCKS = 2
FFT_UNROLL = 8
FFT_PAD = 8
MOE_BOUND_STEP = 256
GATHER_TOKENS, GATHER_WINDOW = 256, 64
SCATTER_TOKENS, SCATTER_WINDOW = 512, 128
BF16_ROWS = 16


def _cparams(sem, vmem=VMEM_LIMIT):
    return pltpu.CompilerParams(dimension_semantics=sem, vmem_limit_bytes=vmem)


def _rms(x, g):
    return x * lax.rsqrt(jnp.mean(x * x, axis=-1, keepdims=True) + EPS) * g


def _dot(a, b):
    return jnp.dot(a, b, preferred_element_type=F32)


def _dot_hi(a, b):
    return jnp.dot(a, b, precision=HIGHEST, preferred_element_type=F32)


def _split(a):
    hi = a.astype(BF16)
    return hi, (a - hi.astype(F32)).astype(BF16)


def _dot_split(a, b):
    a_hi, a_lo = _split(a)
    b_hi, b_lo = _split(b)
    return _dot(a_hi, b_hi) + _dot(a_lo, b_hi) + _dot(a_hi, b_lo)


def _dot_t(a, b):
    return lax.dot_general(a, b, (((1,), (1,)), ((), ())), preferred_element_type=F32)


def _proj_body(x_ref, g_ref, w_ref, bg_ref, cos_ref, sin_ref, *rest, hyc, aw):
    ng = N_GROUPS
    perm_refs = (None,) + rest[:ng - 1]
    phy_ref = rest[ng - 1]
    qkv_refs = rest[ng:ng + 3 * ng]
    gate_ref = rest[ng + 3 * ng]
    h = _rms(x_ref[...], g_ref[...]).astype(BF16)
    phy = _dot(h, w_ref[:, :hyc])
    for j in range(hyc // HY_CB):
        phy_ref[0, j] = phy[:, j * HY_CB:(j + 1) * HY_CB]
    cos = cos_ref[...]
    sin = sin_ref[...]
    tm = cos.shape[0]
    lane = lax.broadcasted_iota(jnp.int32, (tm, LANES), 1)
    first = (lane % HEAD_DIM) < (HEAD_DIM // 2)

    def rope(t, scale):
        chunks = []
        for j in range(aw // LANES):
            tj = t[:, j * LANES:(j + 1) * LANES]
            partner = jnp.where(first, pltpu.roll(tj, LANES - HEAD_DIM // 2, 1),
                                pltpu.roll(tj, HEAD_DIM // 2, 1))
            chunks.append(((tj * cos + partner * sin) * scale).astype(BF16))
        return chunks

    gl = ATTN_OUT // LANES
    q = rope(_dot(h, w_ref[:, hyc:hyc + aw]), HEAD_DIM ** -0.5)
    k = rope(_dot(h, w_ref[:, hyc + aw:hyc + 2 * aw]), 1.0)
    vf = _dot(h, w_ref[:, hyc + 2 * aw:hyc + 3 * aw]).astype(BF16)
    v = [vf[:, j * LANES:(j + 1) * LANES] for j in range(aw // LANES)]
    for g in range(ng):
        for i, t in enumerate((q, k, v)):
            out_ref = qkv_refs[3 * g + i]
            tg = jnp.concatenate(t[g * gl:(g + 1) * gl], axis=1)
            if perm_refs[g] is not None:
                tg = _dot(perm_refs[g][...], tg).astype(BF16)
            dil = out_ref.shape[1]
            rows = tm // dil
            for r in range(dil):
                out_ref[0, r] = tg[r * rows:(r + 1) * rows, :]
    gate_ref[...] = jax.nn.sigmoid(_dot(h, w_ref[:, hyc + 3 * aw:]) + bg_ref[...]).astype(BF16)


def _proj(x2d, g, w_all, b_gate, cos_t, sin_t, batch, seq, tm=256):
    n, d = x2d.shape
    hyc = 3 * HY_WIDTH
    aw = ATTN_WIDTH
    gd = w_all.shape[1] - hyc - 3 * aw
    nseq = seq // tm
    row = lambda i: (i, 0)
    const = lambda i: (0, 0)
    perms = []
    for _, dil in DIL_PAIRS[1:]:
        rows = tm // dil
        src = (np.arange(tm) % rows) * dil + np.arange(tm) // rows
        perms.append(jnp.asarray(np.eye(tm)[src], dtype=BF16))
    qkv_specs, qkv_shapes = [], []
    for _, dil in DIL_PAIRS:
        for _ in range(3):
            qkv_specs.append(pl.BlockSpec((1, dil, tm // dil, ATTN_OUT), lambda i: (i // nseq, 0, i % nseq, 0)))
            qkv_shapes.append(jax.ShapeDtypeStruct((batch, dil, seq // dil, ATTN_OUT), BF16))
    return pl.pallas_call(
        functools.partial(_proj_body, hyc=hyc, aw=aw),
        grid=(n // tm,),
        in_specs=[
            pl.BlockSpec((tm, d), row),
            pl.BlockSpec((1, d), const),
            pl.BlockSpec(w_all.shape, const, pipeline_mode=pl.Buffered(1)),
            pl.BlockSpec((1, gd), const),
            pl.BlockSpec((tm, LANES), lambda i: (i % nseq, 0)),
            pl.BlockSpec((tm, LANES), lambda i: (i % nseq, 0)),
        ] + [pl.BlockSpec((tm, tm), const) for _ in perms],
        out_specs=([pl.BlockSpec((1, hyc // HY_CB, tm, HY_CB), lambda i: (i // nseq, 0, i % nseq, 0))]
                   + qkv_specs + [pl.BlockSpec((tm, gd), row)]),
        out_shape=([jax.ShapeDtypeStruct((batch, hyc // HY_CB, seq, HY_CB), F32)] + qkv_shapes
                   + [jax.ShapeDtypeStruct((n, gd), BF16)]),
        compiler_params=_cparams(("arbitrary",)),
        name="proj",
    )(x2d, g, w_all, b_gate, cos_t, sin_t, *perms)


def _filt_body(t_ref, grid_ref, bands_ref, w1t_ref, w1c_ref, w1s_ref, b1_ref, f1_ref,
               w2_ref, b2_ref, f2_ref, feat_ref):
    ang = bands_ref[...] * grid_ref[...]
    pre = (t_ref[...] * w1t_ref[...] + _dot_hi(jnp.cos(ang), w1c_ref[...])
           + _dot_hi(-jnp.sin(ang), w1s_ref[...]))
    h = jnp.sin(f1_ref[...] * (pre + b1_ref[...]))
    feat_ref[...] = jnp.sin(f2_ref[...] * (_dot_hi(h, w2_ref[...]) + b2_ref[...]))


def _filt(t_col, grid_col, bands, w1, b1, f1, w2, b2, f2):
    length = t_col.shape[0]
    nb = bands.shape[1]
    args = (t_col, grid_col, bands, w1[0:1], w1[1:1 + nb], w1[1 + nb:], b1[None], f1[None],
            w2, b2[None], f2[None])
    return pl.pallas_call(
        _filt_body,
        out_shape=jax.ShapeDtypeStruct((length, HY_FFN), F32),
        compiler_params=_cparams(None),
        name="filt",
    )(*args)


def _fft_tables(length):
    n = 2 * length
    n2 = FFT_N2
    n1 = n // n2
    h1 = n1 // 2
    k1 = np.arange(n1)[:, None]
    g_sig = np.zeros((n2, 2 * n1, 2 * h1))
    g_flt = np.zeros((n2, 2 * n1, 2 * h1))
    g_inv = np.zeros((n2, 2 * h1, 2 * n1))
    j = np.arange(h1)[None, :]
    for r in range(n2):
        th = 2 * np.pi * (k1 * (n2 * j + r) % n) / n
        gr, gi = np.cos(th), -np.sin(th)
        g_sig[r, 0::2, :h1] = gr
        g_sig[r, 0::2, h1:] = -gi
        g_sig[r, 1::2, :h1] = gi
        g_sig[r, 1::2, h1:] = gr
        g_flt[r, 0::2, :h1] = gr
        g_flt[r, 1::2, :h1] = gi
        m = n2 * (j + 1) - r
        thb = 2 * np.pi * (k1 * m % n) / n
        live = (m < length).astype(np.float64)
        g_flt[r, 0::2, h1:] = np.cos(thb) * live
        g_flt[r, 1::2, h1:] = np.sin(thb) * live
        wr, wi = (np.cos(th) / n).T, (np.sin(th) / n).T
        g_inv[r, :h1, 0::2] = wr
        g_inv[r, :h1, 1::2] = -wi
        g_inv[r, h1:, 0::2] = wi
        g_inv[r, h1:, 1::2] = wr
    a = np.arange(n2)
    th2 = 2 * np.pi * (np.outer(a, a) % n2) / n2
    c2, s2 = np.cos(th2), np.sin(th2)
    f_fwd = np.block([[c2, s2], [-s2, c2]])
    f_inv = np.block([[c2, -s2], [s2, c2]])
    cast = lambda z: jnp.asarray(z, dtype=F32).astype(BF16)
    return cast(g_sig), cast(g_flt), cast(g_inv), cast(f_fwd), cast(f_inv)


def _fft_stage1(src_a_ref, src_b_ref, tab_ref, work_ref, n1, is_filter):
    n2 = FFT_N2
    grp = SUBLANES

    def body(i, c):
        r0 = pl.multiple_of(i * grp, grp)
        b0 = pl.multiple_of(n2 - grp - r0, grp) if is_filter else r0
        a = jnp.swapaxes(src_a_ref[:, pl.ds(r0, grp), :], 0, 1)
        b = jnp.swapaxes(src_b_ref[:, pl.ds(b0, grp), :], 0, 1)
        outs = []
        for s in range(grp):
            rhs = jnp.concatenate([a[s], b[grp - 1 - s] if is_filter else b[s]], axis=0).astype(BF16)
            outs.append(_dot(tab_ref[r0 + s], rhs))
        work_ref[:, pl.ds(r0, grp), :] = jnp.swapaxes(jnp.stack(outs, axis=0), 0, 1)
        return c

    lax.fori_loop(0, n2 // grp, body, 0, unroll=2)


def _hyfilt_body(t_ref, ts_ref, delta_ref, feat_ref, feats_ref, w3f_ref, w3b_ref, gflt_ref, ffwd_ref, khat_ref,
                 hf_ref, hb_ref, work_ref, *, length, rc):
    n2 = FFT_N2
    n1 = 2 * length // n2
    tiles = rc // n2
    cb = hf_ref.shape[2]

    def gen(i, c):
        r0 = pl.multiple_of(i * rc, rc)
        j0 = pl.multiple_of(i * tiles, tiles)
        delta = jnp.abs(delta_ref[...])
        hf = _dot_split(feat_ref[pl.ds(r0, rc), :], w3f_ref[...]) * jnp.exp(-t_ref[pl.ds(r0, rc), :] * delta)
        hb = _dot_split(feats_ref[pl.ds(r0, rc), :], w3b_ref[...]) * jnp.exp(-ts_ref[pl.ds(r0, rc), :] * delta)
        hf_ref[pl.ds(j0, tiles), :n2, :] = hf.reshape(tiles, n2, cb)
        hb_ref[pl.ds(j0, tiles), :n2, :] = hb.reshape(tiles, n2, cb)
        return c

    lax.fori_loop(0, length // rc, gen, 0)
    _fft_stage1(hf_ref, hb_ref, gflt_ref, work_ref, n1, True)

    side = MXU_COLS // cb

    def stage2(i, c):
        k1s = [i * side + v for v in range(side)]
        blks = [work_ref[pl.ds(pl.multiple_of(2 * k1, 2), 2), :n2, :].reshape(2 * n2, cb).astype(BF16)
                for k1 in k1s]
        x = _dot(ffwd_ref[...], jnp.concatenate(blks, axis=1)).astype(BF16)
        for v, k1 in enumerate(k1s):
            khat_ref[0, 0, pl.ds(pl.multiple_of(k1 * (2 * n2), 2 * n2), 2 * n2), :] = x[:, v * cb:(v + 1) * cb]
        return c

    lax.fori_loop(0, n1 // side, stage2, 0, unroll=FFT_UNROLL // side)


def _hyfilt(t_col, delta, feat, w3, g_flt, f_fwd, rc=512):
    s = t_col.shape[0]
    cb = HY_CB
    ncb = HY_WIDTH // cb
    n2 = FFT_N2
    n1 = 2 * s // n2
    norder = w3.shape[1] // (2 * HY_WIDTH)
    one = pl.Buffered(1)

    def full(a):
        nd = a.ndim
        return pl.BlockSpec(a.shape, lambda o, c: (0,) * nd, pipeline_mode=one)

    shift = lambda a: jnp.concatenate([a[1:], jnp.zeros_like(a[:1])], axis=0)
    return pl.pallas_call(
        functools.partial(_hyfilt_body, length=s, rc=rc),
        grid=(norder, ncb),
        in_specs=[full(t_col), full(t_col),
                  pl.BlockSpec((1, cb), lambda o, c: (0, c)),
                  full(feat), full(feat),
                  pl.BlockSpec((HY_FFN, cb), lambda o, c: (0, 2 * ncb * o + c)),
                  pl.BlockSpec((HY_FFN, cb), lambda o, c: (0, 2 * ncb * o + ncb + c)),
                  full(g_flt), full(f_fwd)],
        out_specs=pl.BlockSpec((1, 1, 2 * n1 * n2, cb), lambda o, c: (o, c, 0, 0)),
        out_shape=jax.ShapeDtypeStruct((norder, ncb, 2 * n1 * n2, cb), BF16),
        scratch_shapes=[pltpu.VMEM((n1 // 2, n2 + FFT_PAD, cb), F32)] * 2
        + [pltpu.VMEM((2 * n1, n2 + FFT_PAD, cb), F32)],
        compiler_params=_cparams(("arbitrary", "arbitrary")),
        name="hyfilt",
    )(t_col, shift(t_col), delta, feat, shift(feat), w3, w3, g_flt, f_fwd)


def _hyena_body(pz_ref, phy_hbm, cwz_ref, cwg1_ref, cwg2_ref, cbz_ref, cbg1_ref, cbg2_ref,
                skip_ref, khat_hbm, gsig_ref, ginv_ref, ffwd_ref, finv_ref,
                out_ref,
                za_ref, zb_ref, ya_ref, yb_ref, work_ref, pg1_ref, pg2_ref, khat_ref, sem, ksem, *, length, rc):
    ncb = pl.num_programs(0)
    pair = pl.ds(2 * pl.program_id(1), 2)

    def gate_copy(order, dst_ref):
        src = phy_hbm.at[pair, pl.ds((order + 1) * ncb + pl.program_id(0), 1)]
        return pltpu.make_async_copy(src, dst_ref, sem.at[order])

    gate_copies = [gate_copy(0, pg1_ref), gate_copy(1, pg2_ref)]
    for cp in gate_copies:
        cp.start()

    new_block = pl.program_id(1) == 0
    spec_copies = [pltpu.make_async_copy(khat_hbm.at[o, pl.program_id(0)], khat_ref.at[o], ksem.at[o])
                   for o in range(khat_ref.shape[0])]

    @pl.when(new_block)
    def _():
        for cp in spec_copies:
            cp.start()
    n2 = FFT_N2
    n1 = 2 * length // n2
    h1 = n1 // 2
    tiles = rc // n2
    cb = za_ref.shape[2]
    side = MXU_COLS // cb
    sub = SUBLANES

    def conv3(p_ref, b, r0, w_ref, bias_ref):
        p = p_ref[b, 0, pl.ds(r0, rc), :]
        before = p_ref[b, 0, pl.ds(pl.multiple_of(jnp.maximum(r0 - sub, 0), sub), sub), :][sub - 1:sub, :]
        after = p_ref[b, 0, pl.ds(pl.multiple_of(jnp.minimum(r0 + rc, length - sub), sub), sub), :][0:1, :]
        before = jnp.where(r0 == 0, 0.0, before)
        after = jnp.where(r0 + rc == length, 0.0, after)
        row = lax.broadcasted_iota(jnp.int32, p.shape, 0)
        prev = jnp.where(row == 0, before, pltpu.roll(p, 1, 0))
        nxt = jnp.where(row == rc - 1, after, pltpu.roll(p, rc - 1, 0))
        return prev * w_ref[0:1, :] + p * w_ref[1:2, :] + nxt * w_ref[2:3, :] + bias_ref[...]

    def stage2(o):
        def body(i, c):
            k1s = [i * FFT_UNROLL + u for u in range(FFT_UNROLL)]
            blks = [work_ref[pl.ds(pl.multiple_of(2 * k1, 2), 2), :n2, :].reshape(2 * n2, cb).astype(BF16)
                    for k1 in k1s]
            groups = range(0, FFT_UNROLL, side)
            xs = [_dot(ffwd_ref[...], jnp.concatenate(blks[u:u + side], axis=1)) for u in groups]

            def filtered(u, x):
                ys = []
                for v in range(side):
                    k1 = k1s[u + v]
                    kh = khat_ref[o, pl.ds(pl.multiple_of(k1 * (2 * n2), 2 * n2), 2 * n2), :].astype(F32)
                    xr, xi = x[:n2, v * cb:(v + 1) * cb], x[n2:, v * cb:(v + 1) * cb]
                    kr, ki = kh[:n2], kh[n2:]
                    ys.append(jnp.concatenate([xr * kr - xi * ki, xr * ki + xi * kr], axis=0).astype(BF16))
                return jnp.concatenate(ys, axis=1)

            ys = [filtered(u, x) for u, x in zip(groups, xs)]
            outs = []
            for y in ys:
                out = _dot(finv_ref[...], y)
                outs.extend(out[:, v * cb:(v + 1) * cb] for v in range(side))
            for k1, out in zip(k1s, outs):
                work_ref[pl.ds(pl.multiple_of(2 * k1, 2), 2), :n2, :] = out.reshape(2, n2, cb)
            return c
        lax.fori_loop(0, n1 // FFT_UNROLL, body, 0)

    def stage3():
        grp = SUBLANES

        def body(i, c):
            r0 = pl.multiple_of(i * grp, grp)
            wk = jnp.swapaxes(work_ref[:, pl.ds(r0, grp), :], 0, 1)
            res = jnp.stack([_dot(ginv_ref[r0 + s], wk[s].astype(BF16)) for s in range(grp)], axis=0)
            ya_ref[:, pl.ds(r0, grp), :] = jnp.swapaxes(res[:, :h1], 0, 1)
            yb_ref[:, pl.ds(r0, grp), :] = jnp.swapaxes(res[:, h1:], 0, 1)
            return c
        lax.fori_loop(0, n2 // grp, body, 0, unroll=2)

    def rows3(ref, i):
        return ref.at[pl.ds(pl.multiple_of(i * tiles, tiles), tiles), :n2, :]

    def load_z(i, c):
        r0 = pl.multiple_of(i * rc, rc)
        rows3(za_ref, i)[...] = conv3(pz_ref, 0, r0, cwz_ref, cbz_ref).reshape(tiles, n2, cb)
        rows3(zb_ref, i)[...] = conv3(pz_ref, 1, r0, cwz_ref, cbz_ref).reshape(tiles, n2, cb)
        return c

    lax.fori_loop(0, length // rc, load_z, 0)
    for o, (pg_ref, cw_ref, cb_ref) in enumerate(((pg1_ref, cwg1_ref, cbg1_ref), (pg2_ref, cwg2_ref, cbg2_ref))):
        _fft_stage1(za_ref, zb_ref, gsig_ref, work_ref, n1, False)

        @pl.when(new_block)
        def _():
            spec_copies[o].wait()

        stage2(o)
        stage3()
        skip = skip_ref[o:o + 1, :]
        gate_copies[o].wait()

        def gate(i, c):
            r0 = pl.multiple_of(i * rc, rc)
            rows = pl.ds(r0, rc)
            ya, yb = (rows3(r, i)[...].reshape(rc, cb) for r in (ya_ref, yb_ref))
            za, zb = (rows3(r, i)[...].reshape(rc, cb) for r in (za_ref, zb_ref))
            new_a = conv3(pg_ref, 0, r0, cw_ref, cb_ref) * (ya + skip * za)
            new_b = conv3(pg_ref, 1, r0, cw_ref, cb_ref) * (yb + skip * zb)
            if o == 0:
                rows3(za_ref, i)[...] = new_a.reshape(tiles, n2, cb)
                rows3(zb_ref, i)[...] = new_b.reshape(tiles, n2, cb)
            else:
                out_ref[0, rows, :] = new_a.astype(out_ref.dtype)
                out_ref[1, rows, :] = new_b.astype(out_ref.dtype)
            return c

        lax.fori_loop(0, length // rc, gate, 0)


def _hyena(p_hy, conv_w, conv_b, skip, khat, tables, rc=512):
    b, _, s, _ = p_hy.shape
    w = HY_WIDTH
    cb = HY_CB
    ncb = w // cb
    n2 = FFT_N2
    n1 = 2 * s // n2
    g_sig, _, g_inv, f_fwd, f_inv = tables
    one = pl.Buffered(1)

    def cwspec(off):
        return pl.BlockSpec((3, cb), lambda c, p: (0, off + c))

    def cbspec(off):
        return pl.BlockSpec((1, cb), lambda c, p: (0, off + c))

    def full(a):
        nd = a.ndim
        return pl.BlockSpec(a.shape, lambda c, p: (0,) * nd, pipeline_mode=one)

    in_specs = [
        pl.BlockSpec((2, 1, s, cb), lambda c, p: (p, c, 0, 0)),
        pl.BlockSpec(memory_space=pl.ANY),
        cwspec(0), cwspec(ncb), cwspec(2 * ncb),
        cbspec(0), cbspec(ncb), cbspec(2 * ncb),
        pl.BlockSpec((2, cb), lambda c, p: (0, c)),
        pl.BlockSpec(memory_space=pl.ANY),
        full(g_sig), full(g_inv), full(f_fwd), full(f_inv),
    ]
    return pl.pallas_call(
        functools.partial(_hyena_body, length=s, rc=rc),
        grid=(ncb, b // 2),
        in_specs=in_specs,
        out_specs=pl.BlockSpec((2, s, cb), lambda c, p: (p, 0, c)),
        out_shape=jax.ShapeDtypeStruct((b, s, w), BF16),
        scratch_shapes=[pltpu.VMEM((n1 // 2, n2 + FFT_PAD, cb), F32)] * 4
        + [pltpu.VMEM((2 * n1, n2 + FFT_PAD, cb), F32)]
        + [pltpu.VMEM((2, 1, s, cb), F32)] * 2 + [pltpu.VMEM((khat.shape[0], 2 * n1 * n2, cb), BF16)]
        + [pltpu.SemaphoreType.DMA((2,)), pltpu.SemaphoreType.DMA((khat.shape[0],))],
        compiler_params=_cparams(("arbitrary", "arbitrary")),
        name="hyena",
    )(p_hy, p_hy, conv_w, conv_w, conv_w, conv_b, conv_b, conv_b,
      skip, khat, g_sig, g_inv, f_fwd, f_inv)


def _attn_body(q_ref, k_ref, v_ref, o_ref, lse_ref, *, n_side, qb):
    lr = q_ref.shape[2]
    kw = qb + 2 * n_side
    nh = HEADS_PER_GROUP
    lane_head = lax.broadcasted_iota(jnp.int32, (qb, ATTN_OUT), 1) // HEAD_DIM
    qi = lax.broadcasted_iota(jnp.int32, (nh * qb, kw), 0) % qb
    ki = lax.broadcasted_iota(jnp.int32, (nh * qb, kw), 1)
    band = [jnp.where(jnp.abs(ki - qi - shift) <= n_side, 0.0, -1e30) for shift in (0, n_side, 2 * n_side)]

    def scores(cls, i):
        q0 = pl.multiple_of(i * qb, qb)
        w0 = pl.multiple_of(jnp.clip(q0 - n_side, 0, lr - kw), n_side)
        q = q_ref[0, cls, pl.ds(q0, qb), :]
        shift = q0 - w0
        bias = jnp.where(shift == n_side, band[1], jnp.where(shift == 0, band[0], band[2]))
        qs = jnp.concatenate([jnp.where(lane_head == h, q, jnp.zeros_like(q)) for h in range(nh)], axis=0)
        return q0, w0, _dot_t(qs, k_ref[0, cls, pl.ds(w0, kw), :]) + bias

    def softmax(s):
        m = jnp.max(s, axis=-1, keepdims=True)
        p = jnp.exp(s - m)
        l = jnp.sum(p, axis=-1, keepdims=True)
        return p.astype(BF16), l, m + jnp.log(l)

    def store(cls, q0, pv, lse_rows):
        o = jnp.zeros((qb, ATTN_OUT), F32)
        lse = jnp.zeros((qb, ATTN_OUT), F32)
        for h in range(nh):
            mine = lane_head == h
            o = jnp.where(mine, pv[h * qb:(h + 1) * qb], o)
            lse = jnp.where(mine, lse_rows[h * qb:(h + 1) * qb], lse)
        o_ref[0, pl.ds(q0, qb), cls * ATTN_OUT:(cls + 1) * ATTN_OUT] = o
        lse_ref[0, pl.ds(q0, qb), cls * ATTN_OUT:(cls + 1) * ATTN_OUT] = lse

    def body(cls, j, c):
        sc = [scores(cls, ATTN_BLOCKS * j + t) for t in range(ATTN_BLOCKS)]
        sm = [softmax(s) for _, _, s in sc]
        pvs = [_dot(p, v_ref[0, cls, pl.ds(w0, kw), :]) / l for (_, w0, _), (p, l, _) in zip(sc, sm)]
        for (q0, _, _), (_, _, lse_rows), pv in zip(sc, sm, pvs):
            store(cls, q0, pv, lse_rows)
        return c

    for cls in range(q_ref.shape[1]):
        lax.fori_loop(0, lr // (qb * ATTN_BLOCKS), functools.partial(body, cls), 0)


def _attn_group(q, k, v, g, n_side, qb=128, max_classes=8):
    b, dil, lr, _ = q.shape
    cg = min(dil, max_classes)
    in_spec = pl.BlockSpec((1, cg, lr, ATTN_OUT), lambda bi, r: (bi, r, 0, 0))
    out_spec = pl.BlockSpec((1, lr, cg * ATTN_OUT), lambda bi, r: (bi, 0, r))
    o, lse = pl.pallas_call(
        functools.partial(_attn_body, n_side=n_side, qb=qb),
        grid=(b, dil // cg),
        in_specs=[in_spec, in_spec, in_spec],
        out_specs=[out_spec, out_spec],
        out_shape=[jax.ShapeDtypeStruct((b, lr, dil * ATTN_OUT), F32)] * 2,
        compiler_params=_cparams(("arbitrary", "arbitrary")),
        name=f"attn{g}",
    )(q, k, v)
    return o, lse


def _memkv_body(mem_ref, g_ref, w_ref, kv_ref):
    mn = _rms(mem_ref[0], g_ref[...]).astype(BF16)
    kv_ref[0] = _dot(mn, w_ref[...]).astype(BF16)


def _memkv(mem, g, w_kv):
    b, m, d = mem.shape
    n = w_kv.shape[1]
    return pl.pallas_call(
        _memkv_body,
        grid=(b,),
        in_specs=[pl.BlockSpec((1, m, d), lambda i: (i, 0, 0)),
                  pl.BlockSpec((1, d), lambda i: (0, 0)),
                  pl.BlockSpec((d, n), lambda i: (0, 0))],
        out_specs=pl.BlockSpec((1, m, n), lambda i: (i, 0, 0)),
        out_shape=jax.ShapeDtypeStruct((b, m, n), BF16),
        compiler_params=_cparams(("arbitrary",)),
        name="memkv",
    )(mem, g, w_kv)


def _merge_staged_body(x_ref, z_ref, o0_ref, o1_ref, o2_ref, l0_ref, l1_ref, l2_ref, gate_ref,
                       wuh_ref, wua_ref, wout_ref, gx_ref, wq_ref, kv_ref, wo_ref, gm_ref, wr_ref,
                       p1_ref, p2_ref, x2_ref, hm_ref, aff_ref, affh_ref, affl_ref, *, nsplit):
    d = x_ref.shape[1]
    rows = x_ref.shape[0] // nsplit
    parts = range(nsplit)
    sls = [slice(p * rows, (p + 1) * rows) for p in parts]

    def natural(ref, perm_ref, part):
        dil = ref.shape[2] // ATTN_OUT
        rr = rows // dil
        blk = ref[0, part * rr:(part + 1) * rr, :]
        if dil == 1:
            return blk
        pieces = [blk[:, r * ATTN_OUT:(r + 1) * ATTN_OUT] for r in range(dil)]
        if rr % SUBLANES == 0 and dil % SUBLANES == 0:
            return jnp.swapaxes(jnp.stack(pieces, axis=0), 0, 1).reshape(rows, ATTN_OUT)
        hi, lo = _split(jnp.concatenate(pieces, axis=0))
        return _dot(perm_ref[...], hi) + _dot(perm_ref[...], lo)

    def combine(part):
        o0, o1, o2 = (natural(r, p, part) for r, p in ((o0_ref, None), (o1_ref, p1_ref), (o2_ref, p2_ref)))
        l0, l1, l2 = (natural(r, p, part) for r, p in ((l0_ref, None), (l1_ref, p1_ref), (l2_ref, p2_ref)))
        mx = jnp.maximum(jnp.maximum(l0, l1), l2)
        e0, e1, e2 = jnp.exp(l0 - mx), jnp.exp(l1 - mx), jnp.exp(l2 - mx)
        return ((e0 * o0 + e1 * o1 + e2 * o2) / (e0 + e1 + e2)).astype(BF16)

    attn = [combine(p) for p in parts]
    y_hy = [_dot(z_ref[sl, :].astype(BF16), wuh_ref[...]) for sl in sls]
    y_at = [_dot(a, wua_ref[...]) for a in attn]

    def mixed(sl, yh, ya):
        gates = gate_ref[sl, :].astype(F32)
        return (gates[:, :d] * yh + gates[:, d:] * ya).astype(BF16)

    mix = [mixed(sl, yh, ya) for sl, yh, ya in zip(sls, y_hy, y_at)]
    x1 = [x_ref[sl, :] + _dot(m, wout_ref[...]) for sl, m in zip(sls, mix)]

    hx = [_rms(v, gx_ref[...]).astype(BF16) for v in x1]
    qx = [_dot(h, wq_ref[...]).astype(BF16) for h in hx]
    kv = kv_ref[0]
    xw = X_HEADS * X_HEAD_DIM

    def cross_attention(q):
        heads = []
        for h in range(X_HEADS):
            hc = slice(h * X_HEAD_DIM, (h + 1) * X_HEAD_DIM)
            s = _dot_t(q[:, hc], kv[:, hc]) * (X_HEAD_DIM ** -0.5)
            m = jnp.max(s, axis=-1, keepdims=True)
            p = jnp.exp(s - m)
            p = p / jnp.sum(p, axis=-1, keepdims=True)
            heads.append(_dot(p.astype(BF16), kv[:, xw + h * X_HEAD_DIM:xw + (h + 1) * X_HEAD_DIM]))
        return jnp.concatenate(heads, axis=-1).astype(BF16)

    ox = [cross_attention(q) for q in qx]
    x2 = [v + _dot(o, wo_ref[...]) for v, o in zip(x1, ox)]
    for sl, v in zip(sls, x2):
        x2_ref[sl, :] = v

    hm = [_split(_rms(v, gm_ref[...])) for v in x2]
    cross = [_dot(jnp.concatenate([hi, lo], axis=0), wr_ref[...]) for hi, lo in hm]
    for sl, (hi, _), c in zip(sls, hm, cross):
        hm_ref[sl, :] = hi
        logits = (c[:rows, :LANES] + c[:rows, LANES:]) + (c[rows:, :LANES] + c[rows:, LANES:])
        lane = lax.broadcasted_iota(jnp.int32, logits.shape, 1)
        logits = jnp.where(lane < N_EXPERTS, logits, -1e30)
        m = jnp.max(logits, axis=-1, keepdims=True)
        p = jnp.exp(logits - m)
        aff = p / jnp.sum(p, axis=-1, keepdims=True)
        aff_ref[sl, :] = aff
        affh_ref[sl, :], affl_ref[sl, :] = _split(aff)


def _merge(x2d, z2d, outs, lses, gates, wuh, wua, wout, gx, wq, kv, wo, gm, wr_cat, seq, tm=512, nsplit=2):
    n, d = x2d.shape
    nseq = seq // tm
    row = lambda i: (i, 0)
    const = lambda i: (0, 0)

    def rspec(a):
        return pl.BlockSpec((tm, a.shape[1]), row)

    def aspec(a):
        dil = a.shape[2] // ATTN_OUT
        return pl.BlockSpec((1, tm // dil, a.shape[2]), lambda i: (i // nseq, i % nseq, 0))

    def cspec(a):
        return pl.BlockSpec(a.shape, const)

    rows = tm // nsplit
    perms = []
    for a in outs[1:]:
        dil = a.shape[2] // ATTN_OUT
        src = (np.arange(rows) % dil) * (rows // dil) + np.arange(rows) // dil
        perms.append(jnp.asarray(np.eye(rows)[src], dtype=BF16))

    in_arrays = [x2d, z2d, *outs, *lses, gates, wuh, wua, wout, gx, wq, kv, wo, gm, wr_cat, *perms]
    in_specs = ([rspec(a) for a in in_arrays[:2]] + [aspec(a) for a in in_arrays[2:8]] + [rspec(gates)]
                + [cspec(a) for a in in_arrays[9:14]]
                + [pl.BlockSpec((1,) + kv.shape[1:], lambda i: (i // nseq, 0, 0))]
                + [cspec(a) for a in in_arrays[15:]])
    return pl.pallas_call(
        functools.partial(_merge_staged_body, nsplit=nsplit),
        grid=(n // tm,),
        in_specs=in_specs,
        out_specs=[pl.BlockSpec((tm, d), row), pl.BlockSpec((tm, d), row)] + [pl.BlockSpec((tm, LANES), row)] * 3,
        out_shape=[jax.ShapeDtypeStruct((n, d), F32), jax.ShapeDtypeStruct((n, d), BF16),
                   jax.ShapeDtypeStruct((n, LANES), F32), jax.ShapeDtypeStruct((n, LANES), BF16),
                   jax.ShapeDtypeStruct((n, LANES), BF16)],
        compiler_params=_cparams(("arbitrary",)),
        name="merge",
    )(*in_arrays)


def _topk_body(aff_ref, tri_ref, rank_ref, bounds_ref, *, cap, tok_block):
    a = aff_ref[0]
    e, s = a.shape

    def count(mask):
        return jnp.sum(jnp.where(mask, 1.0, 0.0), axis=-1, keepdims=True)

    def as_float(bits):
        return pltpu.bitcast(jnp.broadcast_to(bits, (e, LANES)), F32)[:, 0:1]

    def search(i, thr):
        cand = thr | (jnp.int32(1) << (30 - i))
        return jnp.where(count(a >= as_float(cand)) >= cap, cand, thr)

    thr = as_float(lax.fori_loop(0, 31, search, jnp.zeros((e, 1), jnp.int32)))
    gt = a > thr
    eq = a == thr
    need = cap - count(gt)

    def prefix_excl(mask):
        mf = jnp.where(mask, 1.0, 0.0)
        parts = []
        carry = jnp.zeros((e, 1), F32)
        for c in range(s // LANES):
            blk = mf[:, c * LANES:(c + 1) * LANES]
            inc = _dot(blk.astype(BF16), tri_ref[...])
            parts.append(inc - blk + carry)
            carry = carry + inc[:, LANES - 1:LANES]
        return jnp.concatenate(parts, axis=-1)

    sel = gt | (eq & (prefix_excl(eq) < need))
    excl = prefix_excl(sel)
    rank_ref[0] = jnp.where(sel, excl, -1.0)

    tok = lax.broadcasted_iota(jnp.int32, (e, s), 1)
    lane = lax.broadcasted_iota(jnp.int32, (e, LANES), 1)
    bounds = jnp.zeros((e, LANES), F32)
    for j in range(s // tok_block + 1):
        bounds = jnp.where(lane == j, count(sel & (tok < j * tok_block)), bounds)
    bounds_ref[0] = bounds.astype(jnp.int32)


def _topk(aff_t, cap, tok_block):
    b, e, s = aff_t.shape
    assert s // tok_block + 1 <= LANES
    tri = jnp.asarray(np.triu(np.ones((LANES, LANES))), dtype=BF16)
    return pl.pallas_call(
        functools.partial(_topk_body, cap=cap, tok_block=tok_block),
        grid=(b,),
        in_specs=[pl.BlockSpec((1, e, s), lambda i: (i, 0, 0)),
                  pl.BlockSpec((LANES, LANES), lambda i: (0, 0))],
        out_specs=[pl.BlockSpec((1, e, s), lambda i: (i, 0, 0)),
                   pl.BlockSpec((1, e, LANES), lambda i: (i, 0, 0))],
        out_shape=[jax.ShapeDtypeStruct((b, e, s), F32),
                   jax.ShapeDtypeStruct((b, e, LANES), jnp.int32)],
        compiler_params=_cparams(("arbitrary",)),
        name="topk",
    )(aff_t, tri)


def _moe_windows(bounds_ref, base, ne, step, tokens, window):
    per = tokens // MOE_BOUND_STEP
    starts = []
    passes = jnp.int32(0)
    for e in range(ne):
        r_lo = bounds_ref[base + e * LANES + step * per]
        r_hi = bounds_ref[base + e * LANES + (step + 1) * per]
        ws = (r_lo // BF16_ROWS) * BF16_ROWS
        starts.append(ws)
        passes = jnp.maximum(passes, (r_hi - ws + window - 1) // window)
    return starts, passes


def _gather_body(bounds_ref, rank_ref, hm_ref, affh_ref, affl_ref, xe_ref, gs_ref, *, cap, eg):
    w = GATHER_WINDOW
    ne, tc = rank_ref.shape[1], rank_ref.shape[2]
    bi = pl.program_id(0)
    ci = pl.program_id(1)

    @pl.when(ci == 0)
    def _():
        xe_ref[...] = jnp.zeros_like(xe_ref)
        gs_ref[...] = jnp.zeros_like(gs_ref)

    starts, passes = _moe_windows(bounds_ref, bi * ne * LANES, ne, ci, tc, w)
    iota_w = lax.broadcasted_iota(jnp.int32, (w, 1), 0)

    def one_pass(p, carry):
        for g0 in range(0, ne, eg):
            lo = [starts[e] + p * w for e in range(g0, g0 + eg)]
            phys = [pl.multiple_of(jnp.minimum(v, cap - w), BF16_ROWS) for v in lo]
            rk = jnp.concatenate([jnp.broadcast_to(rank_ref[0, e:e + 1, :], (w, tc))
                                  for e in range(g0, g0 + eg)], axis=0)
            slot = jnp.concatenate([iota_w + v for v in phys], axis=0).astype(F32)
            lom = jnp.concatenate([jnp.zeros_like(iota_w) + v for v in lo], axis=0).astype(F32)
            onehot = jnp.where((rk == slot) & (slot >= lom), 1.0, 0.0).astype(BF16)
            res = _dot(onehot, hm_ref[0])
            resg = _dot(onehot, affh_ref[0]) + _dot(onehot, affl_ref[0])
            for k, e in enumerate(range(g0, g0 + eg)):
                rows = pl.ds(phys[k], w)
                xe_ref[0, e, rows, :] = (xe_ref[0, e, rows, :].astype(F32) + res[k * w:(k + 1) * w]).astype(BF16)
                gs_ref[0, e, rows, :] = gs_ref[0, e, rows, :] + resg[k * w:(k + 1) * w]
        return carry

    lax.fori_loop(0, passes, one_pass, 0)


def _gather(bounds, rank, hm, aff_hi, aff_lo, cap, eg=16):
    b, e, s = rank.shape
    d = hm.shape[2]
    tc = GATHER_TOKENS
    return pl.pallas_call(
        functools.partial(_gather_body, cap=cap, eg=eg),
        grid_spec=pltpu.PrefetchScalarGridSpec(
            num_scalar_prefetch=1,
            grid=(b, s // tc),
            in_specs=[pl.BlockSpec((1, e, tc), lambda bi, ci, bnd: (bi, 0, ci)),
                      pl.BlockSpec((1, tc, d), lambda bi, ci, bnd: (bi, ci, 0)),
                      pl.BlockSpec((1, tc, LANES), lambda bi, ci, bnd: (bi, ci, 0)),
                      pl.BlockSpec((1, tc, LANES), lambda bi, ci, bnd: (bi, ci, 0))],
            out_specs=[pl.BlockSpec((1, e, cap, d), lambda bi, ci, bnd: (bi, 0, 0, 0)),
                       pl.BlockSpec((1, e, cap, LANES), lambda bi, ci, bnd: (bi, 0, 0, 0))],
        ),
        out_shape=[jax.ShapeDtypeStruct((b, e, cap, d), BF16),
                   jax.ShapeDtypeStruct((b, e, cap, LANES), F32)],
        compiler_params=_cparams(("arbitrary", "arbitrary")),
        name="gather",
    )(bounds.reshape(-1), rank, hm, aff_hi, aff_lo)


def _ffn_body(xe_ref, gs_ref, wg_ref, wu_ref, wd_ref, ye_ref, acc_ref, *, mb):
    f = pl.program_id(1)
    nb, _, cap, d = xe_ref.shape
    wg = wg_ref[0].astype(BF16)
    wu = wu_ref[0].astype(BF16)
    wd = wd_ref[0].astype(BF16)

    @pl.when(f == 0)
    def _():
        acc_ref[...] = jnp.zeros_like(acc_ref)

    for b0 in range(0, nb, mb):
        xe = xe_ref[b0:b0 + mb, 0].reshape(mb * cap, d)
        a = _dot(xe, wg)
        u = _dot(xe, wu)
        hsw = (a * jax.nn.sigmoid(a) * u).astype(BF16)
        acc_ref[b0:b0 + mb] += _dot(hsw, wd).reshape(mb, cap, d)

    @pl.when(f == pl.num_programs(1) - 1)
    def _():
        lane = lax.broadcasted_iota(jnp.int32, gs_ref.shape[2:], 1)
        mine = lane == pl.program_id(0)
        for b in range(nb):
            gate = jnp.sum(jnp.where(mine, gs_ref[b, 0], 0.0), axis=-1, keepdims=True)
            ye_ref[b, 0] = (acc_ref[b] * gate).astype(BF16)


def _ffn(xe, gs, wg, wu, wd, ft=1024, mb=1):
    b, e, cap, d = xe.shape
    ff = wg.shape[2]
    return pl.pallas_call(
        functools.partial(_ffn_body, mb=mb),
        grid=(e, ff // ft),
        in_specs=[pl.BlockSpec((b, 1, cap, d), lambda ei, fi: (0, ei, 0, 0)),
                  pl.BlockSpec((b, 1, cap, LANES), lambda ei, fi: (0, ei, 0, 0)),
                  pl.BlockSpec((1, d, ft), lambda ei, fi: (ei, 0, fi)),
                  pl.BlockSpec((1, d, ft), lambda ei, fi: (ei, 0, fi)),
                  pl.BlockSpec((1, ft, d), lambda ei, fi: (ei, fi, 0))],
        out_specs=pl.BlockSpec((b, 1, cap, d), lambda ei, fi: (0, ei, 0, 0)),
        out_shape=jax.ShapeDtypeStruct((b, e, cap, d), BF16),
        scratch_shapes=[pltpu.VMEM((b, cap, d), F32)],
        compiler_params=_cparams(("arbitrary", "arbitrary"), vmem=VMEM_LIMIT_LARGE),
        name="ffn",
    )(xe, gs, wg, wu, wd)


def _scatter_body(bounds_ref, rank_ref, ye_ref, x2_ref, g_ref, out_ref, acc_ref, *, eg):
    w = SCATTER_WINDOW
    ts = x2_ref.shape[1]
    ne, cap = ye_ref.shape[1], ye_ref.shape[2]
    starts, passes = _moe_windows(bounds_ref, pl.program_id(0) * ne * LANES, ne, pl.program_id(1), ts, w)
    rank = rank_ref[0]
    acc_ref[...] = x2_ref[0]
    iota_w = lax.broadcasted_iota(jnp.int32, (1, w), 1)

    def one_pass(p, carry):
        for g0 in range(0, ne, eg):
            onehots, rows = [], []
            for e in range(g0, g0 + eg):
                lo = starts[e] + p * w
                phys = pl.multiple_of(jnp.minimum(lo, cap - w), BF16_ROWS)
                slot = (iota_w + phys).astype(F32)
                hit = (rank[:, e:e + 1] == slot) & (slot >= lo.astype(F32))
                onehots.append(jnp.where(hit, 1.0, 0.0).astype(BF16))
                rows.append(ye_ref[0, e, pl.ds(phys, w), :])
            acc_ref[...] += _dot(jnp.concatenate(onehots, axis=1), jnp.concatenate(rows, axis=0))
        return carry

    lax.fori_loop(0, passes, one_pass, 0)
    out_ref[0] = _rms(acc_ref[...], g_ref[...])


def _scatter(bounds, rank_t, ye, x2, g, eg=8):
    b, s, e = rank_t.shape
    cap, d = ye.shape[2], ye.shape[3]
    ts = SCATTER_TOKENS
    return pl.pallas_call(
        functools.partial(_scatter_body, eg=eg),
        grid_spec=pltpu.PrefetchScalarGridSpec(
            num_scalar_prefetch=1,
            grid=(b, s // ts),
            in_specs=[pl.BlockSpec((1, ts, e), lambda bi, ti, bnd: (bi, ti, 0)),
                      pl.BlockSpec((1, e, cap, d), lambda bi, ti, bnd: (bi, 0, 0, 0)),
                      pl.BlockSpec((1, ts, d), lambda bi, ti, bnd: (bi, ti, 0)),
                      pl.BlockSpec((1, d), lambda bi, ti, bnd: (0, 0))],
            out_specs=pl.BlockSpec((1, ts, d), lambda bi, ti, bnd: (bi, ti, 0)),
            scratch_shapes=[pltpu.VMEM((ts, d), F32)],
        ),
        out_shape=jax.ShapeDtypeStruct((b, s, d), F32),
        compiler_params=_cparams(("arbitrary", "arbitrary")),
        name="scatter",
    )(bounds.reshape(-1), rank_t, ye, x2, g)


def kernel(x, mem, ln_mix_g, w_in, hy_conv_w, hy_conv_b, filt_w1, filt_b1, filt_freq1, filt_w2, filt_b2,
           filt_freq2, filt_w3, hy_skip, w_up_hy, w_up_attn, w_gate, b_gate, w_out, ln_x_g, ln_mem_g,
           w_q_x, w_kv_mem, w_o_x, ln_moe_g, w_router, w_e_gate, w_e_up, w_e_down, ln_f_g):
    b, s, d = x.shape
    n = b * s
    x2d = x.reshape(n, d)

    half = HEAD_DIM // 2
    inv = ROPE_THETA ** (-jnp.arange(0, HEAD_DIM, 2, dtype=F32) / HEAD_DIM)
    ang = jnp.arange(s, dtype=F32)[:, None] * inv[None, :]
    lane = np.arange(LANES)
    cos_t = jnp.cos(ang)[:, lane % half]
    sin_t = jnp.sin(ang)[:, lane % half] * jnp.asarray(np.where(lane % HEAD_DIM < half, -1.0, 1.0), F32)[None, :]

    w_all = jnp.concatenate([w_in, w_gate], axis=1).astype(BF16)
    p_hy, *qkv, gates = _proj(x2d, ln_mix_g[None], w_all, b_gate[None], cos_t, sin_t, b, s)

    t_col = jnp.linspace(0.0, 1.0, s, dtype=F32)[:, None]
    grid_col = 2.0 * math.pi * jnp.arange(s, dtype=F32)[:, None] / s
    bands = jnp.linspace(1e-4, HY_BANDS - 1, HY_BANDS, dtype=F32)[None, :]
    feat = _filt(t_col, grid_col, bands, filt_w1, filt_b1, filt_freq1, filt_w2, filt_b2, filt_freq2)
    delta = jnp.linspace(math.log(HY_TARGET) / HY_SLOW_PCT, math.log(HY_TARGET) / HY_FAST_PCT,
                         HY_WIDTH, dtype=F32)[None, :]
    tables = _fft_tables(s)
    khat = _hyfilt(t_col, delta, feat, filt_w3, tables[1], tables[3])
    z = _hyena(p_hy, hy_conv_w, hy_conv_b[None], hy_skip, khat, tables)

    outs, lses = [], []
    for g, (window, dil) in enumerate(DIL_PAIRS):
        o_g, l_g = _attn_group(*qkv[3 * g:3 * g + 3], g, window // (2 * dil))
        outs.append(o_g)
        lses.append(l_g)

    kv = _memkv(mem, ln_mem_g[None], w_kv_mem.astype(BF16))
    wr_pad = jnp.pad(w_router, ((0, 0), (0, LANES - N_EXPERTS)))
    wr_cat = jnp.concatenate(_split(wr_pad), axis=1)
    x2, hm, aff, aff_hi, aff_lo = _merge(x2d, z.reshape(n, HY_WIDTH), outs, lses, gates,
                         w_up_hy.astype(BF16), w_up_attn.astype(BF16), w_out.astype(BF16),
                         ln_x_g[None], w_q_x.astype(BF16), kv, w_o_x.astype(BF16), ln_moe_g[None],
                         wr_cat, s)

    cap = max(1, EC_FACTOR * s // N_EXPERTS)
    aff3 = aff.reshape(b, s, LANES)
    rank, bounds = _topk(aff3[:, :, :N_EXPERTS].transpose(0, 2, 1), cap, MOE_BOUND_STEP)
    xe, gs = _gather(bounds, rank, hm.reshape(b, s, d), aff_hi.reshape(b, s, LANES), aff_lo.reshape(b, s, LANES), cap)
    ye = _ffn(xe, gs, w_e_gate, w_e_up, w_e_down)
    return _scatter(bounds, rank.transpose(0, 2, 1), ye, x2.reshape(b, s, d), ln_f_g[None])
```

```python
import functools
import math

import numpy as np
import jax
import jax.numpy as jnp
from jax import lax
from jax.experimental import pallas as pl
from jax.experimental.pallas import tpu as pltpu

F32 = jnp.float32
BF16 = jnp.bfloat16
HIGHEST = lax.Precision.HIGHEST

EPS = 1e-6
HY_WIDTH = 768
HY_BANDS = 16
HY_FFN = 64
HY_FAST_PCT = 0.3
HY_SLOW_PCT = 1.5
HY_TARGET = 1e-2
HEAD_DIM = 64
HEADS_PER_GROUP = 4
DIL_PAIRS = ((128, 1), (512, 4), (2048, 16))
N_GROUPS = len(DIL_PAIRS)
ATTN_WIDTH = N_GROUPS * HEADS_PER_GROUP * HEAD_DIM
ATTN_OUT = HEADS_PER_GROUP * HEAD_DIM
ROPE_THETA = 10000.0
X_HEADS = 4
X_HEAD_DIM = 128
N_EXPERTS = 16
EC_FACTOR = 2

LANES = 128
SUBLANES = 8
MXU_COLS = 256
VMEM_LIMIT = 56 * 1024 * 1024
VMEM_LIMIT_LARGE = 60 * 1024 * 1024

FFT_N2 = 128
HY_CB = 128
ATTN_BLOCKS = 4
FFT_UNROLL = 8
FFT_PAD = 8
MOE_BOUND_STEP = 256
GATHER_TOKENS, GATHER_WINDOW = 256, 64
SCATTER_TOKENS, SCATTER_WINDOW = 512, 128
BF16_ROWS = 16


def _cparams(sem, vmem=VMEM_LIMIT):
    return pltpu.CompilerParams(dimension_semantics=sem, vmem_limit_bytes=vmem)


def _rms(x, g):
    return x * lax.rsqrt(jnp.mean(x * x, axis=-1, keepdims=True) + EPS) * g


def _dot(a, b):
    return jnp.dot(a, b, preferred_element_type=F32)


def _dot_hi(a, b):
    return jnp.dot(a, b, precision=HIGHEST, preferred_element_type=F32)


def _split(a):
    hi = a.astype(BF16)
    return hi, (a - hi.astype(F32)).astype(BF16)


def _dot_split(a, b):
    a_hi, a_lo = _split(a)
    b_hi, b_lo = _split(b)
    return _dot(a_hi, b_hi) + _dot(a_lo, b_hi) + _dot(a_hi, b_lo)


def _dot_t(a, b):
    return lax.dot_general(a, b, (((1,), (1,)), ((), ())), preferred_element_type=F32)


def _proj_body(x_ref, g_ref, w_ref, bg_ref, cos_ref, sin_ref, *rest, hyc, aw):
    ng = N_GROUPS
    perm_refs = (None,) + rest[:ng - 1]
    phy_ref = rest[ng - 1]
    qkv_refs = rest[ng:ng + 3 * ng]
    gate_ref = rest[ng + 3 * ng]
    h = _rms(x_ref[...], g_ref[...]).astype(BF16)
    phy = _dot(h, w_ref[:, :hyc])
    for j in range(hyc // HY_CB):
        phy_ref[0, j] = phy[:, j * HY_CB:(j + 1) * HY_CB]
    cos = cos_ref[...]
    sin = sin_ref[...]
    tm = cos.shape[0]
    lane = lax.broadcasted_iota(jnp.int32, (tm, LANES), 1)
    first = (lane % HEAD_DIM) < (HEAD_DIM // 2)

    def rope(t, scale):
        chunks = []
        for j in range(aw // LANES):
            tj = t[:, j * LANES:(j + 1) * LANES]
            partner = jnp.where(first, pltpu.roll(tj, LANES - HEAD_DIM // 2, 1),
                                pltpu.roll(tj, HEAD_DIM // 2, 1))
            chunks.append(((tj * cos + partner * sin) * scale).astype(BF16))
        return chunks

    gl = ATTN_OUT // LANES
    q = rope(_dot(h, w_ref[:, hyc:hyc + aw]), HEAD_DIM ** -0.5)
    k = rope(_dot(h, w_ref[:, hyc + aw:hyc + 2 * aw]), 1.0)
    vf = _dot(h, w_ref[:, hyc + 2 * aw:hyc + 3 * aw]).astype(BF16)
    v = [vf[:, j * LANES:(j + 1) * LANES] for j in range(aw // LANES)]
    for g in range(ng):
        for i, t in enumerate((q, k, v)):
            out_ref = qkv_refs[3 * g + i]
            tg = jnp.concatenate(t[g * gl:(g + 1) * gl], axis=1)
            if perm_refs[g] is not None:
                tg = _dot(perm_refs[g][...], tg).astype(BF16)
            dil = out_ref.shape[1]
            rows = tm // dil
            for r in range(dil):
                out_ref[0, r] = tg[r * rows:(r + 1) * rows, :]
    gate_ref[...] = jax.nn.sigmoid(_dot(h, w_ref[:, hyc + 3 * aw:]) + bg_ref[...]).astype(BF16)


def _proj(x2d, g, w_all, b_gate, cos_t, sin_t, batch, seq, tm=256):
    n, d = x2d.shape
    hyc = 3 * HY_WIDTH
    aw = ATTN_WIDTH
    gd = w_all.shape[1] - hyc - 3 * aw
    nseq = seq // tm
    row = lambda i: (i, 0)
    const = lambda i: (0, 0)
    perms = []
    for _, dil in DIL_PAIRS[1:]:
        rows = tm // dil
        src = (np.arange(tm) % rows) * dil + np.arange(tm) // rows
        perms.append(jnp.asarray(np.eye(tm)[src], dtype=BF16))
    qkv_specs, qkv_shapes = [], []
    for _, dil in DIL_PAIRS:
        for _ in range(3):
            qkv_specs.append(pl.BlockSpec((1, dil, tm // dil, ATTN_OUT), lambda i: (i // nseq, 0, i % nseq, 0)))
            qkv_shapes.append(jax.ShapeDtypeStruct((batch, dil, seq // dil, ATTN_OUT), BF16))
    return pl.pallas_call(
        functools.partial(_proj_body, hyc=hyc, aw=aw),
        grid=(n // tm,),
        in_specs=[
            pl.BlockSpec((tm, d), row),
            pl.BlockSpec((1, d), const),
            pl.BlockSpec(w_all.shape, const, pipeline_mode=pl.Buffered(1)),
            pl.BlockSpec((1, gd), const),
            pl.BlockSpec((tm, LANES), lambda i: (i % nseq, 0)),
            pl.BlockSpec((tm, LANES), lambda i: (i % nseq, 0)),
        ] + [pl.BlockSpec((tm, tm), const) for _ in perms],
        out_specs=([pl.BlockSpec((1, hyc // HY_CB, tm, HY_CB), lambda i: (i // nseq, 0, i % nseq, 0))]
                   + qkv_specs + [pl.BlockSpec((tm, gd), row)]),
        out_shape=([jax.ShapeDtypeStruct((batch, hyc // HY_CB, seq, HY_CB), F32)] + qkv_shapes
                   + [jax.ShapeDtypeStruct((n, gd), BF16)]),
        compiler_params=_cparams(("arbitrary",)),
        name="proj",
    )(x2d, g, w_all, b_gate, cos_t, sin_t, *perms)


def _filt_body(t_ref, grid_ref, bands_ref, w1t_ref, w1c_ref, w1s_ref, b1_ref, f1_ref,
               w2_ref, b2_ref, f2_ref, feat_ref):
    ang = bands_ref[...] * grid_ref[...]
    pre = (t_ref[...] * w1t_ref[...] + _dot_hi(jnp.cos(ang), w1c_ref[...])
           + _dot_hi(-jnp.sin(ang), w1s_ref[...]))
    h = jnp.sin(f1_ref[...] * (pre + b1_ref[...]))
    feat_ref[...] = jnp.sin(f2_ref[...] * (_dot_hi(h, w2_ref[...]) + b2_ref[...]))


def _filt(t_col, grid_col, bands, w1, b1, f1, w2, b2, f2):
    length = t_col.shape[0]
    nb = bands.shape[1]
    args = (t_col, grid_col, bands, w1[0:1], w1[1:1 + nb], w1[1 + nb:], b1[None], f1[None],
            w2, b2[None], f2[None])
    return pl.pallas_call(
        _filt_body,
        out_shape=jax.ShapeDtypeStruct((length, HY_FFN), F32),
        compiler_params=_cparams(None),
        name="filt",
    )(*args)


def _fft_tables(length):
    n = 2 * length
    n2 = FFT_N2
    n1 = n // n2
    h1 = n1 // 2
    k1 = np.arange(n1)[:, None]
    g_sig = np.zeros((n2, 2 * n1, 2 * h1))
    g_flt = np.zeros((n2, 2 * n1, 2 * h1))
    g_inv = np.zeros((n2, 2 * h1, 2 * n1))
    j = np.arange(h1)[None, :]
    for r in range(n2):
        th = 2 * np.pi * (k1 * (n2 * j + r) % n) / n
        gr, gi = np.cos(th), -np.sin(th)
        g_sig[r, 0::2, :h1] = gr
        g_sig[r, 0::2, h1:] = -gi
        g_sig[r, 1::2, :h1] = gi
        g_sig[r, 1::2, h1:] = gr
        g_flt[r, 0::2, :h1] = gr
        g_flt[r, 1::2, :h1] = gi
        m = n2 * (j + 1) - r
        thb = 2 * np.pi * (k1 * m % n) / n
        live = (m < length).astype(np.float64)
        g_flt[r, 0::2, h1:] = np.cos(thb) * live
        g_flt[r, 1::2, h1:] = np.sin(thb) * live
        wr, wi = (np.cos(th) / n).T, (np.sin(th) / n).T
        g_inv[r, :h1, 0::2] = wr
        g_inv[r, :h1, 1::2] = -wi
        g_inv[r, h1:, 0::2] = wi
        g_inv[r, h1:, 1::2] = wr
    a = np.arange(n2)
    th2 = 2 * np.pi * (np.outer(a, a) % n2) / n2
    c2, s2 = np.cos(th2), np.sin(th2)
    f_fwd = np.block([[c2, s2], [-s2, c2]])
    f_inv = np.block([[c2, -s2], [s2, c2]])
    cast = lambda z: jnp.asarray(z, dtype=F32).astype(BF16)
    return cast(g_sig), cast(g_flt), cast(g_inv), cast(f_fwd), cast(f_inv)


def _fft_stage1(src_a_ref, src_b_ref, tab_ref, work_ref, n1, is_filter):
    n2 = FFT_N2
    grp = SUBLANES

    def body(i, c):
        r0 = pl.multiple_of(i * grp, grp)
        b0 = pl.multiple_of(n2 - grp - r0, grp) if is_filter else r0
        a = jnp.swapaxes(src_a_ref[:, pl.ds(r0, grp), :], 0, 1)
        b = jnp.swapaxes(src_b_ref[:, pl.ds(b0, grp), :], 0, 1)
        outs = []
        for s in range(grp):
            rhs = jnp.concatenate([a[s], b[grp - 1 - s] if is_filter else b[s]], axis=0).astype(BF16)
            outs.append(_dot(tab_ref[r0 + s], rhs))
        work_ref[:, pl.ds(r0, grp), :] = jnp.swapaxes(jnp.stack(outs, axis=0), 0, 1)
        return c

    lax.fori_loop(0, n2 // grp, body, 0, unroll=2)


def _hyfilt_body(t_ref, ts_ref, delta_ref, feat_ref, feats_ref, w3f_ref, w3b_ref, gflt_ref, ffwd_ref, khat_ref,
                 hf_ref, hb_ref, work_ref, *, length, rc):
    n2 = FFT_N2
    n1 = 2 * length // n2
    tiles = rc // n2
    cb = hf_ref.shape[2]

    def gen(i, c):
        r0 = pl.multiple_of(i * rc, rc)
        j0 = pl.multiple_of(i * tiles, tiles)
        delta = jnp.abs(delta_ref[...])
        hf = _dot_split(feat_ref[pl.ds(r0, rc), :], w3f_ref[...]) * jnp.exp(-t_ref[pl.ds(r0, rc), :] * delta)
        hb = _dot_split(feats_ref[pl.ds(r0, rc), :], w3b_ref[...]) * jnp.exp(-ts_ref[pl.ds(r0, rc), :] * delta)
        hf_ref[pl.ds(j0, tiles), :n2, :] = hf.reshape(tiles, n2, cb)
        hb_ref[pl.ds(j0, tiles), :n2, :] = hb.reshape(tiles, n2, cb)
        return c

    lax.fori_loop(0, length // rc, gen, 0)
    _fft_stage1(hf_ref, hb_ref, gflt_ref, work_ref, n1, True)

    side = MXU_COLS // cb

    def stage2(i, c):
        k1s = [i * side + v for v in range(side)]
        blks = [work_ref[pl.ds(pl.multiple_of(2 * k1, 2), 2), :n2, :].reshape(2 * n2, cb).astype(BF16)
                for k1 in k1s]
        x = _dot(ffwd_ref[...], jnp.concatenate(blks, axis=1)).astype(BF16)
        for v, k1 in enumerate(k1s):
            khat_ref[0, 0, pl.ds(pl.multiple_of(k1 * (2 * n2), 2 * n2), 2 * n2), :] = x[:, v * cb:(v + 1) * cb]
        return c

    lax.fori_loop(0, n1 // side, stage2, 0, unroll=FFT_UNROLL // side)


def _hyfilt(t_col, delta, feat, w3, g_flt, f_fwd, rc=512):
    s = t_col.shape[0]
    cb = HY_CB
    ncb = HY_WIDTH // cb
    n2 = FFT_N2
    n1 = 2 * s // n2
    norder = w3.shape[1] // (2 * HY_WIDTH)
    one = pl.Buffered(1)

    def full(a):
        nd = a.ndim
        return pl.BlockSpec(a.shape, lambda o, c: (0,) * nd, pipeline_mode=one)

    shift = lambda a: jnp.concatenate([a[1:], jnp.zeros_like(a[:1])], axis=0)
    return pl.pallas_call(
        functools.partial(_hyfilt_body, length=s, rc=rc),
        grid=(norder, ncb),
        in_specs=[full(t_col), full(t_col),
                  pl.BlockSpec((1, cb), lambda o, c: (0, c)),
                  full(feat), full(feat),
                  pl.BlockSpec((HY_FFN, cb), lambda o, c: (0, 2 * ncb * o + c)),
                  pl.BlockSpec((HY_FFN, cb), lambda o, c: (0, 2 * ncb * o + ncb + c)),
                  full(g_flt), full(f_fwd)],
        out_specs=pl.BlockSpec((1, 1, 2 * n1 * n2, cb), lambda o, c: (o, c, 0, 0)),
        out_shape=jax.ShapeDtypeStruct((norder, ncb, 2 * n1 * n2, cb), BF16),
        scratch_shapes=[pltpu.VMEM((n1 // 2, n2 + FFT_PAD, cb), F32)] * 2
        + [pltpu.VMEM((2 * n1, n2 + FFT_PAD, cb), F32)],
        compiler_params=_cparams(("arbitrary", "arbitrary")),
        name="hyfilt",
    )(t_col, shift(t_col), delta, feat, shift(feat), w3, w3, g_flt, f_fwd)


def _hyena_body(pz_ref, phy_hbm, cwz_ref, cwg1_ref, cwg2_ref, cbz_ref, cbg1_ref, cbg2_ref,
                skip_ref, khat_hbm, gsig_ref, ginv_ref, ffwd_ref, finv_ref,
                out_ref,
                za_ref, zb_ref, ya_ref, yb_ref, work_ref, pg1_ref, pg2_ref, khat_ref, sem, ksem, *, length, rc):
    ncb = pl.num_programs(0)
    pair = pl.ds(2 * pl.program_id(1), 2)

    def gate_copy(order, dst_ref):
        src = phy_hbm.at[pair, pl.ds((order + 1) * ncb + pl.program_id(0), 1)]
        return pltpu.make_async_copy(src, dst_ref, sem.at[order])

    gate_copies = [gate_copy(0, pg1_ref), gate_copy(1, pg2_ref)]
    for cp in gate_copies:
        cp.start()

    new_block = pl.program_id(1) == 0
    spec_copies = [pltpu.make_async_copy(khat_hbm.at[o, pl.program_id(0)], khat_ref.at[o], ksem.at[o])
                   for o in range(khat_ref.shape[0])]

    @pl.when(new_block)
    def _():
        for cp in spec_copies:
            cp.start()
    n2 = FFT_N2
    n1 = 2 * length // n2
    h1 = n1 // 2
    tiles = rc // n2
    cb = za_ref.shape[2]
    side = MXU_COLS // cb
    sub = SUBLANES

    def conv3(p_ref, b, r0, w_ref, bias_ref):
        p = p_ref[b, 0, pl.ds(r0, rc), :]
        before = p_ref[b, 0, pl.ds(pl.multiple_of(jnp.maximum(r0 - sub, 0), sub), sub), :][sub - 1:sub, :]
        after = p_ref[b, 0, pl.ds(pl.multiple_of(jnp.minimum(r0 + rc, length - sub), sub), sub), :][0:1, :]
        before = jnp.where(r0 == 0, 0.0, before)
        after = jnp.where(r0 + rc == length, 0.0, after)
        row = lax.broadcasted_iota(jnp.int32, p.shape, 0)
        prev = jnp.where(row == 0, before, pltpu.roll(p, 1, 0))
        nxt = jnp.where(row == rc - 1, after, pltpu.roll(p, rc - 1, 0))
        return prev * w_ref[0:1, :] + p * w_ref[1:2, :] + nxt * w_ref[2:3, :] + bias_ref[...]

    def stage2(o):
        def body(i, c):
            k1s = [i * FFT_UNROLL + u for u in range(FFT_UNROLL)]
            blks = [work_ref[pl.ds(pl.multiple_of(2 * k1, 2), 2), :n2, :].reshape(2 * n2, cb).astype(BF16)
                    for k1 in k1s]
            groups = range(0, FFT_UNROLL, side)
            xs = [_dot(ffwd_ref[...], jnp.concatenate(blks[u:u + side], axis=1)) for u in groups]

            def filtered(u, x):
                ys = []
                for v in range(side):
                    k1 = k1s[u + v]
                    kh = khat_ref[o, pl.ds(pl.multiple_of(k1 * (2 * n2), 2 * n2), 2 * n2), :].astype(F32)
                    xr, xi = x[:n2, v * cb:(v + 1) * cb], x[n2:, v * cb:(v + 1) * cb]
                    kr, ki = kh[:n2], kh[n2:]
                    ys.append(jnp.concatenate([xr * kr - xi * ki, xr * ki + xi * kr], axis=0).astype(BF16))
                return jnp.concatenate(ys, axis=1)

            ys = [filtered(u, x) for u, x in zip(groups, xs)]
            outs = []
            for y in ys:
                out = _dot(finv_ref[...], y)
                outs.extend(out[:, v * cb:(v + 1) * cb] for v in range(side))
            for k1, out in zip(k1s, outs):
                work_ref[pl.ds(pl.multiple_of(2 * k1, 2), 2), :n2, :] = out.reshape(2, n2, cb)
            return c
        lax.fori_loop(0, n1 // FFT_UNROLL, body, 0)

    def stage3():
        grp = SUBLANES

        def body(i, c):
            r0 = pl.multiple_of(i * grp, grp)
            wk = jnp.swapaxes(work_ref[:, pl.ds(r0, grp), :], 0, 1)
            res = jnp.stack([_dot(ginv_ref[r0 + s], wk[s].astype(BF16)) for s in range(grp)], axis=0)
            ya_ref[:, pl.ds(r0, grp), :] = jnp.swapaxes(res[:, :h1], 0, 1)
            yb_ref[:, pl.ds(r0, grp), :] = jnp.swapaxes(res[:, h1:], 0, 1)
            return c
        lax.fori_loop(0, n2 // grp, body, 0, unroll=2)

    def rows3(ref, i):
        return ref.at[pl.ds(pl.multiple_of(i * tiles, tiles), tiles), :n2, :]

    def load_z(i, c):
        r0 = pl.multiple_of(i * rc, rc)
        rows3(za_ref, i)[...] = conv3(pz_ref, 0, r0, cwz_ref, cbz_ref).reshape(tiles, n2, cb)
        rows3(zb_ref, i)[...] = conv3(pz_ref, 1, r0, cwz_ref, cbz_ref).reshape(tiles, n2, cb)
        return c

    lax.fori_loop(0, length // rc, load_z, 0)
    for o, (pg_ref, cw_ref, cb_ref) in enumerate(((pg1_ref, cwg1_ref, cbg1_ref), (pg2_ref, cwg2_ref, cbg2_ref))):
        _fft_stage1(za_ref, zb_ref, gsig_ref, work_ref, n1, False)

        @pl.when(new_block)
        def _():
            spec_copies[o].wait()

        stage2(o)
        stage3()
        skip = skip_ref[o:o + 1, :]
        gate_copies[o].wait()

        def gate(i, c):
            r0 = pl.multiple_of(i * rc, rc)
            rows = pl.ds(r0, rc)
            ya, yb = (rows3(r, i)[...].reshape(rc, cb) for r in (ya_ref, yb_ref))
            za, zb = (rows3(r, i)[...].reshape(rc, cb) for r in (za_ref, zb_ref))
            new_a = conv3(pg_ref, 0, r0, cw_ref, cb_ref) * (ya + skip * za)
            new_b = conv3(pg_ref, 1, r0, cw_ref, cb_ref) * (yb + skip * zb)
            if o == 0:
                rows3(za_ref, i)[...] = new_a.reshape(tiles, n2, cb)
                rows3(zb_ref, i)[...] = new_b.reshape(tiles, n2, cb)
            else:
                out_ref[0, rows, :] = new_a.astype(out_ref.dtype)
                out_ref[1, rows, :] = new_b.astype(out_ref.dtype)
            return c

        lax.fori_loop(0, length // rc, gate, 0)


def _hyena(p_hy, conv_w, conv_b, skip, khat, tables, rc=512):
    b, _, s, _ = p_hy.shape
    w = HY_WIDTH
    cb = HY_CB
    ncb = w // cb
    n2 = FFT_N2
    n1 = 2 * s // n2
    g_sig, _, g_inv, f_fwd, f_inv = tables
    one = pl.Buffered(1)

    def cwspec(off):
        return pl.BlockSpec((3, cb), lambda c, p: (0, off + c))

    def cbspec(off):
        return pl.BlockSpec((1, cb), lambda c, p: (0, off + c))

    def full(a):
        nd = a.ndim
        return pl.BlockSpec(a.shape, lambda c, p: (0,) * nd, pipeline_mode=one)

    in_specs = [
        pl.BlockSpec((2, 1, s, cb), lambda c, p: (p, c, 0, 0)),
        pl.BlockSpec(memory_space=pl.ANY),
        cwspec(0), cwspec(ncb), cwspec(2 * ncb),
        cbspec(0), cbspec(ncb), cbspec(2 * ncb),
        pl.BlockSpec((2, cb), lambda c, p: (0, c)),
        pl.BlockSpec(memory_space=pl.ANY),
        full(g_sig), full(g_inv), full(f_fwd), full(f_inv),
    ]
    return pl.pallas_call(
        functools.partial(_hyena_body, length=s, rc=rc),
        grid=(ncb, b // 2),
        in_specs=in_specs,
        out_specs=pl.BlockSpec((2, s, cb), lambda c, p: (p, 0, c)),
        out_shape=jax.ShapeDtypeStruct((b, s, w), BF16),
        scratch_shapes=[pltpu.VMEM((n1 // 2, n2 + FFT_PAD, cb), F32)] * 4
        + [pltpu.VMEM((2 * n1, n2 + FFT_PAD, cb), F32)]
        + [pltpu.VMEM((2, 1, s, cb), F32)] * 2 + [pltpu.VMEM((khat.shape[0], 2 * n1 * n2, cb), BF16)]
        + [pltpu.SemaphoreType.DMA((2,)), pltpu.SemaphoreType.DMA((khat.shape[0],))],
        compiler_params=_cparams(("arbitrary", "arbitrary")),
        name="hyena",
    )(p_hy, p_hy, conv_w, conv_w, conv_w, conv_b, conv_b, conv_b,
      skip, khat, g_sig, g_inv, f_fwd, f_inv)


def _attn_body(q_ref, k_ref, v_ref, o_ref, lse_ref, *, n_side, qb):
    lr = q_ref.shape[2]
    kw = qb + 2 * n_side
    nh = HEADS_PER_GROUP
    lane_head = lax.broadcasted_iota(jnp.int32, (qb, ATTN_OUT), 1) // HEAD_DIM
    qi = lax.broadcasted_iota(jnp.int32, (nh * qb, kw), 0) % qb
    ki = lax.broadcasted_iota(jnp.int32, (nh * qb, kw), 1)
    band = [jnp.where(jnp.abs(ki - qi - shift) <= n_side, 0.0, -1e30) for shift in (0, n_side, 2 * n_side)]

    def scores(cls, i):
        q0 = pl.multiple_of(i * qb, qb)
        w0 = pl.multiple_of(jnp.clip(q0 - n_side, 0, lr - kw), n_side)
        q = q_ref[0, cls, pl.ds(q0, qb), :]
        shift = q0 - w0
        bias = jnp.where(shift == n_side, band[1], jnp.where(shift == 0, band[0], band[2]))
        qs = jnp.concatenate([jnp.where(lane_head == h, q, jnp.zeros_like(q)) for h in range(nh)], axis=0)
        return q0, w0, _dot_t(qs, k_ref[0, cls, pl.ds(w0, kw), :]) + bias

    def softmax(s):
        m = jnp.max(s, axis=-1, keepdims=True)
        p = jnp.exp(s - m)
        l = jnp.sum(p, axis=-1, keepdims=True)
        return p.astype(BF16), l, m + jnp.log(l)

    def store(cls, q0, pv, lse_rows):
        o = jnp.zeros((qb, ATTN_OUT), F32)
        lse = jnp.zeros((qb, ATTN_OUT), F32)
        for h in range(nh):
            mine = lane_head == h
            o = jnp.where(mine, pv[h * qb:(h + 1) * qb], o)
            lse = jnp.where(mine, lse_rows[h * qb:(h + 1) * qb], lse)
        o_ref[0, pl.ds(q0, qb), cls * ATTN_OUT:(cls + 1) * ATTN_OUT] = o
        lse_ref[0, pl.ds(q0, qb), cls * ATTN_OUT:(cls + 1) * ATTN_OUT] = lse

    per_trip = min(ATTN_BLOCKS, lr // qb)

    def body(cls, j, c):
        sc = [scores(cls, per_trip * j + t) for t in range(per_trip)]
        sm = [softmax(s) for _, _, s in sc]
        pvs = [_dot(p, v_ref[0, cls, pl.ds(w0, kw), :]) / l for (_, w0, _), (p, l, _) in zip(sc, sm)]
        for (q0, _, _), (_, _, lse_rows), pv in zip(sc, sm, pvs):
            store(cls, q0, pv, lse_rows)
        return c

    for cls in range(q_ref.shape[1]):
        lax.fori_loop(0, lr // (qb * per_trip), functools.partial(body, cls), 0)


def _attn_group(q, k, v, g, n_side, qb=128, max_classes=8):
    b, dil, lr, _ = q.shape
    cg = min(dil, max_classes)
    in_spec = pl.BlockSpec((1, cg, lr, ATTN_OUT), lambda bi, r: (bi, r, 0, 0))
    out_spec = pl.BlockSpec((1, lr, cg * ATTN_OUT), lambda bi, r: (bi, 0, r))
    o, lse = pl.pallas_call(
        functools.partial(_attn_body, n_side=n_side, qb=qb),
        grid=(b, dil // cg),
        in_specs=[in_spec, in_spec, in_spec],
        out_specs=[out_spec, out_spec],
        out_shape=[jax.ShapeDtypeStruct((b, lr, dil * ATTN_OUT), F32)] * 2,
        compiler_params=_cparams(("arbitrary", "arbitrary")),
        name=f"attn{g}",
    )(q, k, v)
    return o, lse


def _memkv_body(mem_ref, g_ref, w_ref, kv_ref):
    mn = _rms(mem_ref[0], g_ref[...]).astype(BF16)
    kv_ref[0] = _dot(mn, w_ref[...]).astype(BF16)


def _memkv(mem, g, w_kv):
    b, m, d = mem.shape
    n = w_kv.shape[1]
    return pl.pallas_call(
        _memkv_body,
        grid=(b,),
        in_specs=[pl.BlockSpec((1, m, d), lambda i: (i, 0, 0)),
                  pl.BlockSpec((1, d), lambda i: (0, 0)),
                  pl.BlockSpec((d, n), lambda i: (0, 0))],
        out_specs=pl.BlockSpec((1, m, n), lambda i: (i, 0, 0)),
        out_shape=jax.ShapeDtypeStruct((b, m, n), BF16),
        compiler_params=_cparams(("arbitrary",)),
        name="memkv",
    )(mem, g, w_kv)


def _merge_staged_body(x_ref, z_ref, o0_ref, o1_ref, o2_ref, l0_ref, l1_ref, l2_ref, gate_ref,
                       wuh_ref, wua_ref, wout_ref, gx_ref, wq_ref, kv_ref, wo_ref, gm_ref, wr_ref,
                       p1_ref, p2_ref, x2_ref, hm_ref, aff_ref, affh_ref, affl_ref, *, nsplit):
    d = x_ref.shape[1]
    rows = x_ref.shape[0] // nsplit
    parts = range(nsplit)
    sls = [slice(p * rows, (p + 1) * rows) for p in parts]

    def natural(ref, perm_ref, part):
        dil = ref.shape[2] // ATTN_OUT
        rr = rows // dil
        blk = ref[0, part * rr:(part + 1) * rr, :]
        if dil == 1:
            return blk
        pieces = [blk[:, r * ATTN_OUT:(r + 1) * ATTN_OUT] for r in range(dil)]
        if rr % SUBLANES == 0 and dil % SUBLANES == 0:
            return jnp.swapaxes(jnp.stack(pieces, axis=0), 0, 1).reshape(rows, ATTN_OUT)
        hi, lo = _split(jnp.concatenate(pieces, axis=0))
        return _dot(perm_ref[...], hi) + _dot(perm_ref[...], lo)

    def combine(part):
        o0, o1, o2 = (natural(r, p, part) for r, p in ((o0_ref, None), (o1_ref, p1_ref), (o2_ref, p2_ref)))
        l0, l1, l2 = (natural(r, p, part) for r, p in ((l0_ref, None), (l1_ref, p1_ref), (l2_ref, p2_ref)))
        mx = jnp.maximum(jnp.maximum(l0, l1), l2)
        e0, e1, e2 = jnp.exp(l0 - mx), jnp.exp(l1 - mx), jnp.exp(l2 - mx)
        return ((e0 * o0 + e1 * o1 + e2 * o2) / (e0 + e1 + e2)).astype(BF16)

    attn = [combine(p) for p in parts]
    y_hy = [_dot(z_ref[sl, :].astype(BF16), wuh_ref[...]) for sl in sls]
    y_at = [_dot(a, wua_ref[...]) for a in attn]

    def mixed(sl, yh, ya):
        gates = gate_ref[sl, :].astype(F32)
        return (gates[:, :d] * yh + gates[:, d:] * ya).astype(BF16)

    mix = [mixed(sl, yh, ya) for sl, yh, ya in zip(sls, y_hy, y_at)]
    x1 = [x_ref[sl, :] + _dot(m, wout_ref[...]) for sl, m in zip(sls, mix)]

    hx = [_rms(v, gx_ref[...]).astype(BF16) for v in x1]
    qx = [_dot(h, wq_ref[...]).astype(BF16) for h in hx]
    kv = kv_ref[0]
    xw = X_HEADS * X_HEAD_DIM

    def cross_attention(q):
        heads = []
        for h in range(X_HEADS):
            hc = slice(h * X_HEAD_DIM, (h + 1) * X_HEAD_DIM)
            s = _dot_t(q[:, hc], kv[:, hc]) * (X_HEAD_DIM ** -0.5)
            m = jnp.max(s, axis=-1, keepdims=True)
            p = jnp.exp(s - m)
            p = p / jnp.sum(p, axis=-1, keepdims=True)
            heads.append(_dot(p.astype(BF16), kv[:, xw + h * X_HEAD_DIM:xw + (h + 1) * X_HEAD_DIM]))
        return jnp.concatenate(heads, axis=-1).astype(BF16)

    ox = [cross_attention(q) for q in qx]
    x2 = [v + _dot(o, wo_ref[...]) for v, o in zip(x1, ox)]
    for sl, v in zip(sls, x2):
        x2_ref[sl, :] = v

    hm = [_split(_rms(v, gm_ref[...])) for v in x2]
    cross = [_dot(jnp.concatenate([hi, lo], axis=0), wr_ref[...]) for hi, lo in hm]
    for sl, (hi, _), c in zip(sls, hm, cross):
        hm_ref[sl, :] = hi
        logits = (c[:rows, :LANES] + c[:rows, LANES:]) + (c[rows:, :LANES] + c[rows:, LANES:])
        lane = lax.broadcasted_iota(jnp.int32, logits.shape, 1)
        logits = jnp.where(lane < N_EXPERTS, logits, -1e30)
        m = jnp.max(logits, axis=-1, keepdims=True)
        p = jnp.exp(logits - m)
        aff = p / jnp.sum(p, axis=-1, keepdims=True)
        aff_ref[sl, :] = aff
        affh_ref[sl, :], affl_ref[sl, :] = _split(aff)


def _merge(x2d, z2d, outs, lses, gates, wuh, wua, wout, gx, wq, kv, wo, gm, wr_cat, seq, tm=1024, nsplit=4):
    n, d = x2d.shape
    nseq = seq // tm
    row = lambda i: (i, 0)
    const = lambda i: (0, 0)

    def rspec(a):
        return pl.BlockSpec((tm, a.shape[1]), row)

    def aspec(a):
        dil = a.shape[2] // ATTN_OUT
        return pl.BlockSpec((1, tm // dil, a.shape[2]), lambda i: (i // nseq, i % nseq, 0))

    def cspec(a):
        return pl.BlockSpec(a.shape, const)

    rows = tm // nsplit
    perms = []
    for a in outs[1:]:
        dil = a.shape[2] // ATTN_OUT
        src = (np.arange(rows) % dil) * (rows // dil) + np.arange(rows) // dil
        perms.append(jnp.asarray(np.eye(rows)[src], dtype=BF16))

    in_arrays = [x2d, z2d, *outs, *lses, gates, wuh, wua, wout, gx, wq, kv, wo, gm, wr_cat, *perms]
    in_specs = ([rspec(a) for a in in_arrays[:2]] + [aspec(a) for a in in_arrays[2:8]] + [rspec(gates)]
                + [cspec(a) for a in in_arrays[9:14]]
                + [pl.BlockSpec((1,) + kv.shape[1:], lambda i: (i // nseq, 0, 0))]
                + [cspec(a) for a in in_arrays[15:]])
    return pl.pallas_call(
        functools.partial(_merge_staged_body, nsplit=nsplit),
        grid=(n // tm,),
        in_specs=in_specs,
        out_specs=[pl.BlockSpec((tm, d), row), pl.BlockSpec((tm, d), row)] + [pl.BlockSpec((tm, LANES), row)] * 3,
        out_shape=[jax.ShapeDtypeStruct((n, d), F32), jax.ShapeDtypeStruct((n, d), BF16),
                   jax.ShapeDtypeStruct((n, LANES), F32), jax.ShapeDtypeStruct((n, LANES), BF16),
                   jax.ShapeDtypeStruct((n, LANES), BF16)],
        compiler_params=_cparams(("arbitrary",), vmem=VMEM_LIMIT_LARGE),
        name="merge",
    )(*in_arrays)


def _topk_body(aff_ref, tri_ref, rank_ref, bounds_ref, *, cap, tok_block):
    a = aff_ref[0]
    e, s = a.shape

    def count(mask):
        return jnp.sum(jnp.where(mask, 1.0, 0.0), axis=-1, keepdims=True)

    def as_float(bits):
        return pltpu.bitcast(jnp.broadcast_to(bits, (e, LANES)), F32)[:, 0:1]

    def search(i, thr):
        cand = thr | (jnp.int32(1) << (30 - i))
        return jnp.where(count(a >= as_float(cand)) >= cap, cand, thr)

    thr = as_float(lax.fori_loop(0, 31, search, jnp.zeros((e, 1), jnp.int32)))
    gt = a > thr
    eq = a == thr
    need = cap - count(gt)

    def prefix_excl(mask):
        mf = jnp.where(mask, 1.0, 0.0)
        parts = []
        carry = jnp.zeros((e, 1), F32)
        for c in range(s // LANES):
            blk = mf[:, c * LANES:(c + 1) * LANES]
            inc = _dot(blk.astype(BF16), tri_ref[...])
            parts.append(inc - blk + carry)
            carry = carry + inc[:, LANES - 1:LANES]
        return jnp.concatenate(parts, axis=-1)

    sel = gt | (eq & (prefix_excl(eq) < need))
    excl = prefix_excl(sel)
    rank_ref[0] = jnp.where(sel, excl, -1.0)

    tok = lax.broadcasted_iota(jnp.int32, (e, s), 1)
    lane = lax.broadcasted_iota(jnp.int32, (e, LANES), 1)
    bounds = jnp.zeros((e, LANES), F32)
    for j in range(s // tok_block + 1):
        bounds = jnp.where(lane == j, count(sel & (tok < j * tok_block)), bounds)
    bounds_ref[0] = bounds.astype(jnp.int32)


def _topk(aff_t, cap, tok_block):
    b, e, s = aff_t.shape
    assert s // tok_block + 1 <= LANES
    tri = jnp.asarray(np.triu(np.ones((LANES, LANES))), dtype=BF16)
    return pl.pallas_call(
        functools.partial(_topk_body, cap=cap, tok_block=tok_block),
        grid=(b,),
        in_specs=[pl.BlockSpec((1, e, s), lambda i: (i, 0, 0)),
                  pl.BlockSpec((LANES, LANES), lambda i: (0, 0))],
        out_specs=[pl.BlockSpec((1, e, s), lambda i: (i, 0, 0)),
                   pl.BlockSpec((1, e, LANES), lambda i: (i, 0, 0))],
        out_shape=[jax.ShapeDtypeStruct((b, e, s), F32),
                   jax.ShapeDtypeStruct((b, e, LANES), jnp.int32)],
        compiler_params=_cparams(("arbitrary",)),
        name="topk",
    )(aff_t, tri)


def _moe_windows(bounds_ref, base, ne, step, tokens, window):
    per = tokens // MOE_BOUND_STEP
    starts = []
    passes = jnp.int32(0)
    for e in range(ne):
        r_lo = bounds_ref[base + e * LANES + step * per]
        r_hi = bounds_ref[base + e * LANES + (step + 1) * per]
        ws = (r_lo // BF16_ROWS) * BF16_ROWS
        starts.append(ws)
        passes = jnp.maximum(passes, (r_hi - ws + window - 1) // window)
    return starts, passes


def _gather_body(bounds_ref, rank_ref, hm_ref, affh_ref, affl_ref, xe_ref, gs_ref, *, cap, eg):
    w = GATHER_WINDOW
    ne, tc = rank_ref.shape[1], rank_ref.shape[2]
    bi = pl.program_id(0)
    ci = pl.program_id(1)

    @pl.when(ci == 0)
    def _():
        xe_ref[...] = jnp.zeros_like(xe_ref)
        gs_ref[...] = jnp.zeros_like(gs_ref)

    starts, passes = _moe_windows(bounds_ref, bi * ne * LANES, ne, ci, tc, w)
    iota_w = lax.broadcasted_iota(jnp.int32, (w, 1), 0)

    def one_pass(p, carry):
        for g0 in range(0, ne, eg):
            lo = [starts[e] + p * w for e in range(g0, g0 + eg)]
            phys = [pl.multiple_of(jnp.minimum(v, cap - w), BF16_ROWS) for v in lo]
            rk = jnp.concatenate([jnp.broadcast_to(rank_ref[0, e:e + 1, :], (w, tc))
                                  for e in range(g0, g0 + eg)], axis=0)
            slot = jnp.concatenate([iota_w + v for v in phys], axis=0).astype(F32)
            lom = jnp.concatenate([jnp.zeros_like(iota_w) + v for v in lo], axis=0).astype(F32)
            onehot = jnp.where((rk == slot) & (slot >= lom), 1.0, 0.0).astype(BF16)
            res = _dot(onehot, hm_ref[0])
            resg = _dot(onehot, affh_ref[0]) + _dot(onehot, affl_ref[0])
            for k, e in enumerate(range(g0, g0 + eg)):
                rows = pl.ds(phys[k], w)
                xe_ref[0, e, rows, :] = (xe_ref[0, e, rows, :].astype(F32) + res[k * w:(k + 1) * w]).astype(BF16)
                gs_ref[0, e, rows, :] = gs_ref[0, e, rows, :] + resg[k * w:(k + 1) * w]
        return carry

    lax.fori_loop(0, passes, one_pass, 0)


def _gather(bounds, rank, hm, aff_hi, aff_lo, cap, eg=16):
    b, e, s = rank.shape
    d = hm.shape[2]
    tc = GATHER_TOKENS
    return pl.pallas_call(
        functools.partial(_gather_body, cap=cap, eg=eg),
        grid_spec=pltpu.PrefetchScalarGridSpec(
            num_scalar_prefetch=1,
            grid=(b, s // tc),
            in_specs=[pl.BlockSpec((1, e, tc), lambda bi, ci, bnd: (bi, 0, ci)),
                      pl.BlockSpec((1, tc, d), lambda bi, ci, bnd: (bi, ci, 0)),
                      pl.BlockSpec((1, tc, LANES), lambda bi, ci, bnd: (bi, ci, 0)),
                      pl.BlockSpec((1, tc, LANES), lambda bi, ci, bnd: (bi, ci, 0))],
            out_specs=[pl.BlockSpec((1, e, cap, d), lambda bi, ci, bnd: (bi, 0, 0, 0)),
                       pl.BlockSpec((1, e, cap, LANES), lambda bi, ci, bnd: (bi, 0, 0, 0))],
        ),
        out_shape=[jax.ShapeDtypeStruct((b, e, cap, d), BF16),
                   jax.ShapeDtypeStruct((b, e, cap, LANES), F32)],
        compiler_params=_cparams(("arbitrary", "arbitrary")),
        name="gather",
    )(bounds.reshape(-1), rank, hm, aff_hi, aff_lo)


def _ffn_body(xe_ref, gs_ref, wg_ref, wu_ref, wd_ref, ye_ref, acc_ref, *, mb):
    f = pl.program_id(1)
    nb, _, cap, d = xe_ref.shape
    wg = wg_ref[0].astype(BF16)
    wu = wu_ref[0].astype(BF16)
    wd = wd_ref[0].astype(BF16)

    @pl.when(f == 0)
    def _():
        acc_ref[...] = jnp.zeros_like(acc_ref)

    for b0 in range(0, nb, mb):
        xe = xe_ref[b0:b0 + mb, 0].reshape(mb * cap, d)
        a = _dot(xe, wg)
        u = _dot(xe, wu)
        hsw = (a * jax.nn.sigmoid(a) * u).astype(BF16)
        acc_ref[b0:b0 + mb] += _dot(hsw, wd).reshape(mb, cap, d)

    @pl.when(f == pl.num_programs(1) - 1)
    def _():
        lane = lax.broadcasted_iota(jnp.int32, gs_ref.shape[2:], 1)
        mine = lane == pl.program_id(0)
        for b in range(nb):
            gate = jnp.sum(jnp.where(mine, gs_ref[b, 0], 0.0), axis=-1, keepdims=True)
            ye_ref[b, 0] = (acc_ref[b] * gate).astype(BF16)


def _ffn(xe, gs, wg, wu, wd, ft=1024, mb=1):
    b, e, cap, d = xe.shape
    ff = wg.shape[2]
    return pl.pallas_call(
        functools.partial(_ffn_body, mb=mb),
        grid=(e, ff // ft),
        in_specs=[pl.BlockSpec((b, 1, cap, d), lambda ei, fi: (0, ei, 0, 0)),
                  pl.BlockSpec((b, 1, cap, LANES), lambda ei, fi: (0, ei, 0, 0)),
                  pl.BlockSpec((1, d, ft), lambda ei, fi: (ei, 0, fi)),
                  pl.BlockSpec((1, d, ft), lambda ei, fi: (ei, 0, fi)),
                  pl.BlockSpec((1, ft, d), lambda ei, fi: (ei, fi, 0))],
        out_specs=pl.BlockSpec((b, 1, cap, d), lambda ei, fi: (0, ei, 0, 0)),
        out_shape=jax.ShapeDtypeStruct((b, e, cap, d), BF16),
        scratch_shapes=[pltpu.VMEM((b, cap, d), F32)],
        compiler_params=_cparams(("arbitrary", "arbitrary"), vmem=VMEM_LIMIT_LARGE),
        name="ffn",
    )(xe, gs, wg, wu, wd)


def _scatter_body(bounds_ref, rank_ref, ye_ref, x2_ref, g_ref, out_ref, acc_ref, *, eg):
    w = SCATTER_WINDOW
    ts = x2_ref.shape[1]
    ne, cap = ye_ref.shape[1], ye_ref.shape[2]
    starts, passes = _moe_windows(bounds_ref, pl.program_id(0) * ne * LANES, ne, pl.program_id(1), ts, w)
    rank = rank_ref[0]
    acc_ref[...] = x2_ref[0]
    iota_w = lax.broadcasted_iota(jnp.int32, (1, w), 1)

    def operands(p, g0):
        onehots, rows = [], []
        for e in range(g0, g0 + eg):
            lo = starts[e] + p * w
            phys = pl.multiple_of(jnp.minimum(lo, cap - w), BF16_ROWS)
            slot = (iota_w + phys).astype(F32)
            hit = (rank[:, e:e + 1] == slot) & (slot >= lo.astype(F32))
            onehots.append(jnp.where(hit, 1.0, 0.0).astype(BF16))
            rows.append(ye_ref[0, e, pl.ds(phys, w), :])
        return jnp.concatenate(onehots, axis=1), jnp.concatenate(rows, axis=0)

    def one_pass(p, carry):
        ops = [operands(p, g0) for g0 in range(0, ne, eg)]
        total = _dot(*ops[0])
        for lhs, rhs in ops[1:]:
            total = total + _dot(lhs, rhs)
        acc_ref[...] += total
        return carry

    lax.fori_loop(0, passes, one_pass, 0)
    out_ref[0] = _rms(acc_ref[...], g_ref[...])


def _scatter(bounds, rank_t, ye, x2, g, eg=8):
    b, s, e = rank_t.shape
    cap, d = ye.shape[2], ye.shape[3]
    ts = SCATTER_TOKENS
    return pl.pallas_call(
        functools.partial(_scatter_body, eg=eg),
        grid_spec=pltpu.PrefetchScalarGridSpec(
            num_scalar_prefetch=1,
            grid=(b, s // ts),
            in_specs=[pl.BlockSpec((1, ts, e), lambda bi, ti, bnd: (bi, ti, 0)),
                      pl.BlockSpec((1, e, cap, d), lambda bi, ti, bnd: (bi, 0, 0, 0)),
                      pl.BlockSpec((1, ts, d), lambda bi, ti, bnd: (bi, ti, 0)),
                      pl.BlockSpec((1, d), lambda bi, ti, bnd: (0, 0))],
            out_specs=pl.BlockSpec((1, ts, d), lambda bi, ti, bnd: (bi, ti, 0)),
            scratch_shapes=[pltpu.VMEM((ts, d), F32)],
        ),
        out_shape=jax.ShapeDtypeStruct((b, s, d), F32),
        compiler_params=_cparams(("arbitrary", "arbitrary")),
        name="scatter",
    )(bounds.reshape(-1), rank_t, ye, x2, g)


def kernel(x, mem, ln_mix_g, w_in, hy_conv_w, hy_conv_b, filt_w1, filt_b1, filt_freq1, filt_w2, filt_b2,
           filt_freq2, filt_w3, hy_skip, w_up_hy, w_up_attn, w_gate, b_gate, w_out, ln_x_g, ln_mem_g,
           w_q_x, w_kv_mem, w_o_x, ln_moe_g, w_router, w_e_gate, w_e_up, w_e_down, ln_f_g):
    b, s, d = x.shape
    n = b * s
    x2d = x.reshape(n, d)

    half = HEAD_DIM // 2
    inv = ROPE_THETA ** (-jnp.arange(0, HEAD_DIM, 2, dtype=F32) / HEAD_DIM)
    ang = jnp.arange(s, dtype=F32)[:, None] * inv[None, :]
    lane = np.arange(LANES)
    cos_t = jnp.cos(ang)[:, lane % half]
    sin_t = jnp.sin(ang)[:, lane % half] * jnp.asarray(np.where(lane % HEAD_DIM < half, -1.0, 1.0), F32)[None, :]

    w_all = jnp.concatenate([w_in, w_gate], axis=1).astype(BF16)
    p_hy, *qkv, gates = _proj(x2d, ln_mix_g[None], w_all, b_gate[None], cos_t, sin_t, b, s)

    t_col = jnp.linspace(0.0, 1.0, s, dtype=F32)[:, None]
    grid_col = 2.0 * math.pi * jnp.arange(s, dtype=F32)[:, None] / s
    bands = jnp.linspace(1e-4, HY_BANDS - 1, HY_BANDS, dtype=F32)[None, :]
    feat = _filt(t_col, grid_col, bands, filt_w1, filt_b1, filt_freq1, filt_w2, filt_b2, filt_freq2)
    delta = jnp.linspace(math.log(HY_TARGET) / HY_SLOW_PCT, math.log(HY_TARGET) / HY_FAST_PCT,
                         HY_WIDTH, dtype=F32)[None, :]
    tables = _fft_tables(s)
    khat = _hyfilt(t_col, delta, feat, filt_w3, tables[1], tables[3])
    z = _hyena(p_hy, hy_conv_w, hy_conv_b[None], hy_skip, khat, tables)

    outs, lses = [], []
    for g, (window, dil) in enumerate(DIL_PAIRS):
        o_g, l_g = _attn_group(*qkv[3 * g:3 * g + 3], g, window // (2 * dil))
        outs.append(o_g)
        lses.append(l_g)

    kv = _memkv(mem, ln_mem_g[None], w_kv_mem.astype(BF16))
    wr_pad = jnp.pad(w_router, ((0, 0), (0, LANES - N_EXPERTS)))
    wr_cat = jnp.concatenate(_split(wr_pad), axis=1)
    x2, hm, aff, aff_hi, aff_lo = _merge(x2d, z.reshape(n, HY_WIDTH), outs, lses, gates,
                         w_up_hy.astype(BF16), w_up_attn.astype(BF16), w_out.astype(BF16),
                         ln_x_g[None], w_q_x.astype(BF16), kv, w_o_x.astype(BF16), ln_moe_g[None],
                         wr_cat, s)

    cap = max(1, EC_FACTOR * s // N_EXPERTS)
    aff3 = aff.reshape(b, s, LANES)
    rank, bounds = _topk(aff3[:, :, :N_EXPERTS].transpose(0, 2, 1), cap, MOE_BOUND_STEP)
    xe, gs = _gather(bounds, rank, hm.reshape(b, s, d), aff_hi.reshape(b, s, LANES), aff_lo.reshape(b, s, LANES), cap)
    ye = _ffn(xe, gs, w_e_gate, w_e_up, w_e_down)
    return _scatter(bounds, rank.transpose(0, 2, 1), ye, x2.reshape(b, s, d), ln_f_g[None])
```

```python
import functools
import math

import numpy as np
import jax
import jax.numpy as jnp
from jax import lax
from jax.experimental import pallas as pl
from jax.experimental.pallas import tpu as pltpu

F32 = jnp.float32
BF16 = jnp.bfloat16
HIGHEST = lax.Precision.HIGHEST

EPS = 1e-6
HY_WIDTH = 768
HY_BANDS = 16
HY_FFN = 64
HY_FAST_PCT = 0.3
HY_SLOW_PCT = 1.5
HY_TARGET = 1e-2
HEAD_DIM = 64
HEADS_PER_GROUP = 4
DIL_PAIRS = ((128, 1), (512, 4), (2048, 16))
N_GROUPS = len(DIL_PAIRS)
ATTN_WIDTH = N_GROUPS * HEADS_PER_GROUP * HEAD_DIM
ATTN_OUT = HEADS_PER_GROUP * HEAD_DIM
ROPE_THETA = 10000.0
X_HEADS = 4
X_HEAD_DIM = 128
N_EXPERTS = 16
EC_FACTOR = 2

LANES = 128
SUBLANES = 8
MXU_COLS = 256
VMEM_LIMIT = 56 * 1024 * 1024
VMEM_LIMIT_LARGE = 60 * 1024 * 1024

FFT_N2 = 128
HY_CB = 128
ATTN_BLOCKS = 4
FFT_UNROLL = 8
FFT_PAD = 8
MOE_BOUND_STEP = 256
GATHER_TOKENS, GATHER_WINDOW = 256, 64
SCATTER_TOKENS, SCATTER_WINDOW = 512, 128
BF16_ROWS = 16


def _cparams(sem, vmem=VMEM_LIMIT):
    return pltpu.CompilerParams(dimension_semantics=sem, vmem_limit_bytes=vmem)


def _rms(x, g):
    return x * lax.rsqrt(jnp.mean(x * x, axis=-1, keepdims=True) + EPS) * g


def _dot(a, b):
    return jnp.dot(a, b, preferred_element_type=F32)


def _dot_hi(a, b):
    return jnp.dot(a, b, precision=HIGHEST, preferred_element_type=F32)


def _split(a):
    hi = a.astype(BF16)
    return hi, (a - hi.astype(F32)).astype(BF16)


def _dot_split(a, b):
    a_hi, a_lo = _split(a)
    b_hi, b_lo = _split(b)
    return _dot(a_hi, b_hi) + _dot(a_lo, b_hi) + _dot(a_hi, b_lo)


def _dot_t(a, b):
    return lax.dot_general(a, b, (((1,), (1,)), ((), ())), preferred_element_type=F32)


def _proj_body(x_ref, g_ref, w_ref, bg_ref, cos_ref, sin_ref, *rest, hyc, aw, gr):
    ng = N_GROUPS
    perm_refs = (None,) + rest[:ng - 1]
    phy_ref = rest[ng - 1]
    qkv_refs = rest[ng:ng + 3 * ng]
    gate_ref = rest[ng + 3 * ng]
    tm = x_ref.shape[0]
    sls = [slice(r0, r0 + gr) for r0 in range(0, tm, gr)]
    lane = lax.broadcasted_iota(jnp.int32, (gr, LANES), 1)
    first = (lane % HEAD_DIM) < (HEAD_DIM // 2)
    gl = ATTN_OUT // LANES

    def rope(t, sl, scale):
        cos, sin = cos_ref[sl, :], sin_ref[sl, :]
        chunks = []
        for j in range(aw // LANES):
            tj = t[:, j * LANES:(j + 1) * LANES]
            partner = jnp.where(first, pltpu.roll(tj, LANES - HEAD_DIM // 2, 1),
                                pltpu.roll(tj, HEAD_DIM // 2, 1))
            chunks.append(((tj * cos + partner * sin) * scale).astype(BF16))
        return chunks

    hs = [_rms(x_ref[sl, :], g_ref[...]).astype(BF16) for sl in sls]
    phys = [_dot(h, w_ref[:, :hyc]) for h in hs]
    qf = [_dot(h, w_ref[:, hyc:hyc + aw]) for h in hs]
    kf = [_dot(h, w_ref[:, hyc + aw:hyc + 2 * aw]) for h in hs]
    vf = [_dot(h, w_ref[:, hyc + 2 * aw:hyc + 3 * aw]).astype(BF16) for h in hs]
    gf = [_dot(h, w_ref[:, hyc + 3 * aw:]) for h in hs]
    for sl, phy in zip(sls, phys):
        for j in range(hyc // HY_CB):
            phy_ref[0, j, sl, :] = phy[:, j * HY_CB:(j + 1) * HY_CB]
    qs = [rope(t, sl, HEAD_DIM ** -0.5) for t, sl in zip(qf, sls)]
    ks = [rope(t, sl, 1.0) for t, sl in zip(kf, sls)]
    vs = [[t[:, j * LANES:(j + 1) * LANES] for j in range(aw // LANES)] for t in vf]
    for p, (q, k, v) in enumerate(zip(qs, ks, vs)):
        for g in range(ng):
            for i, t in enumerate((q, k, v)):
                out_ref = qkv_refs[3 * g + i]
                tg = jnp.concatenate(t[g * gl:(g + 1) * gl], axis=1)
                if perm_refs[g] is not None:
                    tg = _dot(perm_refs[g][...], tg).astype(BF16)
                dil = out_ref.shape[1]
                rows = gr // dil
                for r in range(dil):
                    out_ref[0, r, p * rows:(p + 1) * rows, :] = tg[r * rows:(r + 1) * rows, :]
    for sl, t in zip(sls, gf):
        gate_ref[sl, :] = jax.nn.sigmoid(t + bg_ref[...]).astype(BF16)


def _proj(x2d, g, w_all, b_gate, cos_t, sin_t, batch, seq, tm=512, gr=256):
    n, d = x2d.shape
    hyc = 3 * HY_WIDTH
    aw = ATTN_WIDTH
    gd = w_all.shape[1] - hyc - 3 * aw
    nseq = seq // tm
    row = lambda i: (i, 0)
    const = lambda i: (0, 0)
    perms = []
    for _, dil in DIL_PAIRS[1:]:
        rows = gr // dil
        src = (np.arange(gr) % rows) * dil + np.arange(gr) // rows
        perms.append(jnp.asarray(np.eye(gr)[src], dtype=BF16))
    qkv_specs, qkv_shapes = [], []
    for _, dil in DIL_PAIRS:
        for _ in range(3):
            qkv_specs.append(pl.BlockSpec((1, dil, tm // dil, ATTN_OUT), lambda i: (i // nseq, 0, i % nseq, 0)))
            qkv_shapes.append(jax.ShapeDtypeStruct((batch, dil, seq // dil, ATTN_OUT), BF16))
    return pl.pallas_call(
        functools.partial(_proj_body, hyc=hyc, aw=aw, gr=gr),
        grid=(n // tm,),
        in_specs=[
            pl.BlockSpec((tm, d), row),
            pl.BlockSpec((1, d), const),
            pl.BlockSpec(w_all.shape, const, pipeline_mode=pl.Buffered(1)),
            pl.BlockSpec((1, gd), const),
            pl.BlockSpec((tm, LANES), lambda i: (i % nseq, 0)),
            pl.BlockSpec((tm, LANES), lambda i: (i % nseq, 0)),
        ] + [pl.BlockSpec((gr, gr), const) for _ in perms],
        out_specs=([pl.BlockSpec((1, hyc // HY_CB, tm, HY_CB), lambda i: (i // nseq, 0, i % nseq, 0))]
                   + qkv_specs + [pl.BlockSpec((tm, gd), row)]),
        out_shape=([jax.ShapeDtypeStruct((batch, hyc // HY_CB, seq, HY_CB), F32)] + qkv_shapes
                   + [jax.ShapeDtypeStruct((n, gd), BF16)]),
        compiler_params=_cparams(("arbitrary",)),
        name="proj",
    )(x2d, g, w_all, b_gate, cos_t, sin_t, *perms)


def _filt_body(t_ref, grid_ref, bands_ref, w1t_ref, w1c_ref, w1s_ref, b1_ref, f1_ref,
               w2_ref, b2_ref, f2_ref, feat_ref):
    ang = bands_ref[...] * grid_ref[...]
    pre = (t_ref[...] * w1t_ref[...] + _dot_hi(jnp.cos(ang), w1c_ref[...])
           + _dot_hi(-jnp.sin(ang), w1s_ref[...]))
    h = jnp.sin(f1_ref[...] * (pre + b1_ref[...]))
    feat_ref[...] = jnp.sin(f2_ref[...] * (_dot_hi(h, w2_ref[...]) + b2_ref[...]))


def _filt(t_col, grid_col, bands, w1, b1, f1, w2, b2, f2):
    length = t_col.shape[0]
    nb = bands.shape[1]
    args = (t_col, grid_col, bands, w1[0:1], w1[1:1 + nb], w1[1 + nb:], b1[None], f1[None],
            w2, b2[None], f2[None])
    return pl.pallas_call(
        _filt_body,
        out_shape=jax.ShapeDtypeStruct((length, HY_FFN), F32),
        compiler_params=_cparams(None),
        name="filt",
    )(*args)


def _fft_tables(length):
    n = 2 * length
    n2 = FFT_N2
    n1 = n // n2
    h1 = n1 // 2
    k1 = np.arange(n1)[:, None]
    g_sig = np.zeros((n2, 2 * n1, 2 * h1))
    g_flt = np.zeros((n2, 2 * n1, 2 * h1))
    g_inv = np.zeros((n2, 2 * h1, 2 * n1))
    j = np.arange(h1)[None, :]
    for r in range(n2):
        th = 2 * np.pi * (k1 * (n2 * j + r) % n) / n
        gr, gi = np.cos(th), -np.sin(th)
        g_sig[r, 0::2, :h1] = gr
        g_sig[r, 0::2, h1:] = -gi
        g_sig[r, 1::2, :h1] = gi
        g_sig[r, 1::2, h1:] = gr
        g_flt[r, 0::2, :h1] = gr
        g_flt[r, 1::2, :h1] = gi
        m = n2 * (j + 1) - r
        thb = 2 * np.pi * (k1 * m % n) / n
        live = (m < length).astype(np.float64)
        g_flt[r, 0::2, h1:] = np.cos(thb) * live
        g_flt[r, 1::2, h1:] = np.sin(thb) * live
        wr, wi = (np.cos(th) / n).T, (np.sin(th) / n).T
        g_inv[r, :h1, 0::2] = wr
        g_inv[r, :h1, 1::2] = -wi
        g_inv[r, h1:, 0::2] = wi
        g_inv[r, h1:, 1::2] = wr
    a = np.arange(n2)
    th2 = 2 * np.pi * (np.outer(a, a) % n2) / n2
    c2, s2 = np.cos(th2), np.sin(th2)
    f_fwd = np.block([[c2, s2], [-s2, c2]])
    f_inv = np.block([[c2, -s2], [s2, c2]])
    cast = lambda z: jnp.asarray(z, dtype=F32).astype(BF16)
    return cast(g_sig), cast(g_flt), cast(g_inv), cast(f_fwd), cast(f_inv)


def _fft_stage1(src_a_ref, src_b_ref, tab_ref, work_ref, n1, is_filter):
    n2 = FFT_N2
    grp = SUBLANES

    def body(i, c):
        r0 = pl.multiple_of(i * grp, grp)
        b0 = pl.multiple_of(n2 - grp - r0, grp) if is_filter else r0
        a = jnp.swapaxes(src_a_ref[:, pl.ds(r0, grp), :], 0, 1)
        b = jnp.swapaxes(src_b_ref[:, pl.ds(b0, grp), :], 0, 1)
        outs = []
        for s in range(grp):
            rhs = jnp.concatenate([a[s], b[grp - 1 - s] if is_filter else b[s]], axis=0).astype(BF16)
            outs.append(_dot(tab_ref[r0 + s], rhs))
        work_ref[:, pl.ds(r0, grp), :] = jnp.swapaxes(jnp.stack(outs, axis=0), 0, 1)
        return c

    lax.fori_loop(0, n2 // grp, body, 0, unroll=2)


def _hyfilt_body(t_ref, ts_ref, delta_ref, feat_ref, feats_ref, w3f_ref, w3b_ref, gflt_ref, ffwd_ref, khat_ref,
                 hf_ref, hb_ref, work_ref, *, length, rc):
    n2 = FFT_N2
    n1 = 2 * length // n2
    tiles = rc // n2
    cb = hf_ref.shape[2]

    def gen(i, c):
        r0 = pl.multiple_of(i * rc, rc)
        j0 = pl.multiple_of(i * tiles, tiles)
        delta = jnp.abs(delta_ref[...])
        hf = _dot_split(feat_ref[pl.ds(r0, rc), :], w3f_ref[...]) * jnp.exp(-t_ref[pl.ds(r0, rc), :] * delta)
        hb = _dot_split(feats_ref[pl.ds(r0, rc), :], w3b_ref[...]) * jnp.exp(-ts_ref[pl.ds(r0, rc), :] * delta)
        hf_ref[pl.ds(j0, tiles), :n2, :] = hf.reshape(tiles, n2, cb)
        hb_ref[pl.ds(j0, tiles), :n2, :] = hb.reshape(tiles, n2, cb)
        return c

    lax.fori_loop(0, length // rc, gen, 0)
    _fft_stage1(hf_ref, hb_ref, gflt_ref, work_ref, n1, True)

    side = MXU_COLS // cb

    def stage2(i, c):
        k1s = [i * side + v for v in range(side)]
        blks = [work_ref[pl.ds(pl.multiple_of(2 * k1, 2), 2), :n2, :].reshape(2 * n2, cb).astype(BF16)
                for k1 in k1s]
        x = _dot(ffwd_ref[...], jnp.concatenate(blks, axis=1)).astype(BF16)
        for v, k1 in enumerate(k1s):
            khat_ref[0, 0, pl.ds(pl.multiple_of(k1 * (2 * n2), 2 * n2), 2 * n2), :] = x[:, v * cb:(v + 1) * cb]
        return c

    lax.fori_loop(0, n1 // side, stage2, 0, unroll=FFT_UNROLL // side)


def _hyfilt(t_col, delta, feat, w3, g_flt, f_fwd, rc=512):
    s = t_col.shape[0]
    cb = HY_CB
    ncb = HY_WIDTH // cb
    n2 = FFT_N2
    n1 = 2 * s // n2
    norder = w3.shape[1] // (2 * HY_WIDTH)
    one = pl.Buffered(1)

    def full(a):
        nd = a.ndim
        return pl.BlockSpec(a.shape, lambda o, c: (0,) * nd, pipeline_mode=one)

    shift = lambda a: jnp.concatenate([a[1:], jnp.zeros_like(a[:1])], axis=0)
    return pl.pallas_call(
        functools.partial(_hyfilt_body, length=s, rc=rc),
        grid=(norder, ncb),
        in_specs=[full(t_col), full(t_col),
                  pl.BlockSpec((1, cb), lambda o, c: (0, c)),
                  full(feat), full(feat),
                  pl.BlockSpec((HY_FFN, cb), lambda o, c: (0, 2 * ncb * o + c)),
                  pl.BlockSpec((HY_FFN, cb), lambda o, c: (0, 2 * ncb * o + ncb + c)),
                  full(g_flt), full(f_fwd)],
        out_specs=pl.BlockSpec((1, 1, 2 * n1 * n2, cb), lambda o, c: (o, c, 0, 0)),
        out_shape=jax.ShapeDtypeStruct((norder, ncb, 2 * n1 * n2, cb), BF16),
        scratch_shapes=[pltpu.VMEM((n1 // 2, n2 + FFT_PAD, cb), F32)] * 2
        + [pltpu.VMEM((2 * n1, n2 + FFT_PAD, cb), F32)],
        compiler_params=_cparams(("arbitrary", "arbitrary")),
        name="hyfilt",
    )(t_col, shift(t_col), delta, feat, shift(feat), w3, w3, g_flt, f_fwd)


def _hyena_body(pz_ref, phy_hbm, cwz_ref, cwg1_ref, cwg2_ref, cbz_ref, cbg1_ref, cbg2_ref,
                skip_ref, khat_hbm, gsig_ref, ginv_ref, ffwd_ref, finv_ref,
                out_ref,
                za_ref, zb_ref, ya_ref, yb_ref, work_ref, pg1_ref, pg2_ref, khat_ref, sem, ksem, *, length, rc):
    ncb = pl.num_programs(0)
    pair = pl.ds(2 * pl.program_id(1), 2)

    def gate_copy(order, dst_ref):
        src = phy_hbm.at[pair, pl.ds((order + 1) * ncb + pl.program_id(0), 1)]
        return pltpu.make_async_copy(src, dst_ref, sem.at[order])

    gate_copies = [gate_copy(0, pg1_ref), gate_copy(1, pg2_ref)]
    for cp in gate_copies:
        cp.start()

    new_block = pl.program_id(1) == 0
    spec_copies = [pltpu.make_async_copy(khat_hbm.at[o, pl.program_id(0)], khat_ref.at[o], ksem.at[o])
                   for o in range(khat_ref.shape[0])]

    @pl.when(new_block)
    def _():
        for cp in spec_copies:
            cp.start()
    n2 = FFT_N2
    n1 = 2 * length // n2
    h1 = n1 // 2
    tiles = rc // n2
    cb = za_ref.shape[2]
    side = MXU_COLS // cb
    sub = SUBLANES

    def conv3(p_ref, b, r0, w_ref, bias_ref):
        p = p_ref[b, 0, pl.ds(r0, rc), :]
        before = p_ref[b, 0, pl.ds(pl.multiple_of(jnp.maximum(r0 - sub, 0), sub), sub), :][sub - 1:sub, :]
        after = p_ref[b, 0, pl.ds(pl.multiple_of(jnp.minimum(r0 + rc, length - sub), sub), sub), :][0:1, :]
        before = jnp.where(r0 == 0, 0.0, before)
        after = jnp.where(r0 + rc == length, 0.0, after)
        row = lax.broadcasted_iota(jnp.int32, p.shape, 0)
        prev = jnp.where(row == 0, before, pltpu.roll(p, 1, 0))
        nxt = jnp.where(row == rc - 1, after, pltpu.roll(p, rc - 1, 0))
        return prev * w_ref[0:1, :] + p * w_ref[1:2, :] + nxt * w_ref[2:3, :] + bias_ref[...]

    def stage2(o):
        def body(i, c):
            k1s = [i * FFT_UNROLL + u for u in range(FFT_UNROLL)]
            blks = [work_ref[pl.ds(pl.multiple_of(2 * k1, 2), 2), :n2, :].reshape(2 * n2, cb).astype(BF16)
                    for k1 in k1s]
            groups = range(0, FFT_UNROLL, side)
            xs = [_dot(ffwd_ref[...], jnp.concatenate(blks[u:u + side], axis=1)) for u in groups]

            def filtered(u, x):
                ys = []
                for v in range(side):
                    k1 = k1s[u + v]
                    kh = khat_ref[o, pl.ds(pl.multiple_of(k1 * (2 * n2), 2 * n2), 2 * n2), :].astype(F32)
                    xr, xi = x[:n2, v * cb:(v + 1) * cb], x[n2:, v * cb:(v + 1) * cb]
                    kr, ki = kh[:n2], kh[n2:]
                    ys.append(jnp.concatenate([xr * kr - xi * ki, xr * ki + xi * kr], axis=0).astype(BF16))
                return jnp.concatenate(ys, axis=1)

            ys = [filtered(u, x) for u, x in zip(groups, xs)]
            outs = []
            for y in ys:
                out = _dot(finv_ref[...], y)
                outs.extend(out[:, v * cb:(v + 1) * cb] for v in range(side))
            for k1, out in zip(k1s, outs):
                work_ref[pl.ds(pl.multiple_of(2 * k1, 2), 2), :n2, :] = out.reshape(2, n2, cb)
            return c
        lax.fori_loop(0, n1 // FFT_UNROLL, body, 0)

    def stage3():
        grp = SUBLANES

        def body(i, c):
            r0 = pl.multiple_of(i * grp, grp)
            wk = jnp.swapaxes(work_ref[:, pl.ds(r0, grp), :], 0, 1)
            res = jnp.stack([_dot(ginv_ref[r0 + s], wk[s].astype(BF16)) for s in range(grp)], axis=0)
            ya_ref[:, pl.ds(r0, grp), :] = jnp.swapaxes(res[:, :h1], 0, 1)
            yb_ref[:, pl.ds(r0, grp), :] = jnp.swapaxes(res[:, h1:], 0, 1)
            return c
        lax.fori_loop(0, n2 // grp, body, 0, unroll=2)

    def rows3(ref, i):
        return ref.at[pl.ds(pl.multiple_of(i * tiles, tiles), tiles), :n2, :]

    def load_z(i, c):
        r0 = pl.multiple_of(i * rc, rc)
        rows3(za_ref, i)[...] = conv3(pz_ref, 0, r0, cwz_ref, cbz_ref).reshape(tiles, n2, cb)
        rows3(zb_ref, i)[...] = conv3(pz_ref, 1, r0, cwz_ref, cbz_ref).reshape(tiles, n2, cb)
        return c

    lax.fori_loop(0, length // rc, load_z, 0)
    for o, (pg_ref, cw_ref, cb_ref) in enumerate(((pg1_ref, cwg1_ref, cbg1_ref), (pg2_ref, cwg2_ref, cbg2_ref))):
        _fft_stage1(za_ref, zb_ref, gsig_ref, work_ref, n1, False)

        @pl.when(new_block)
        def _():
            spec_copies[o].wait()

        stage2(o)
        stage3()
        skip = skip_ref[o:o + 1, :]
        gate_copies[o].wait()

        def gate(i, c):
            r0 = pl.multiple_of(i * rc, rc)
            rows = pl.ds(r0, rc)
            ya, yb = (rows3(r, i)[...].reshape(rc, cb) for r in (ya_ref, yb_ref))
            za, zb = (rows3(r, i)[...].reshape(rc, cb) for r in (za_ref, zb_ref))
            new_a = conv3(pg_ref, 0, r0, cw_ref, cb_ref) * (ya + skip * za)
            new_b = conv3(pg_ref, 1, r0, cw_ref, cb_ref) * (yb + skip * zb)
            if o == 0:
                rows3(za_ref, i)[...] = new_a.reshape(tiles, n2, cb)
                rows3(zb_ref, i)[...] = new_b.reshape(tiles, n2, cb)
            else:
                out_ref[0, rows, :] = new_a.astype(out_ref.dtype)
                out_ref[1, rows, :] = new_b.astype(out_ref.dtype)
            return c

        lax.fori_loop(0, length // rc, gate, 0)


def _hyena(p_hy, conv_w, conv_b, skip, khat, tables, rc=512):
    b, _, s, _ = p_hy.shape
    w = HY_WIDTH
    cb = HY_CB
    ncb = w // cb
    n2 = FFT_N2
    n1 = 2 * s // n2
    g_sig, _, g_inv, f_fwd, f_inv = tables
    one = pl.Buffered(1)

    def cwspec(off):
        return pl.BlockSpec((3, cb), lambda c, p: (0, off + c))

    def cbspec(off):
        return pl.BlockSpec((1, cb), lambda c, p: (0, off + c))

    def full(a):
        nd = a.ndim
        return pl.BlockSpec(a.shape, lambda c, p: (0,) * nd, pipeline_mode=one)

    in_specs = [
        pl.BlockSpec((2, 1, s, cb), lambda c, p: (p, c, 0, 0)),
        pl.BlockSpec(memory_space=pl.ANY),
        cwspec(0), cwspec(ncb), cwspec(2 * ncb),
        cbspec(0), cbspec(ncb), cbspec(2 * ncb),
        pl.BlockSpec((2, cb), lambda c, p: (0, c)),
        pl.BlockSpec(memory_space=pl.ANY),
        full(g_sig), full(g_inv), full(f_fwd), full(f_inv),
    ]
    return pl.pallas_call(
        functools.partial(_hyena_body, length=s, rc=rc),
        grid=(ncb, b // 2),
        in_specs=in_specs,
        out_specs=pl.BlockSpec((2, s, cb), lambda c, p: (p, 0, c)),
        out_shape=jax.ShapeDtypeStruct((b, s, w), BF16),
        scratch_shapes=[pltpu.VMEM((n1 // 2, n2 + FFT_PAD, cb), F32)] * 4
        + [pltpu.VMEM((2 * n1, n2 + FFT_PAD, cb), F32)]
        + [pltpu.VMEM((2, 1, s, cb), F32)] * 2 + [pltpu.VMEM((khat.shape[0], 2 * n1 * n2, cb), BF16)]
        + [pltpu.SemaphoreType.DMA((2,)), pltpu.SemaphoreType.DMA((khat.shape[0],))],
        compiler_params=_cparams(("arbitrary", "arbitrary")),
        name="hyena",
    )(p_hy, p_hy, conv_w, conv_w, conv_w, conv_b, conv_b, conv_b,
      skip, khat, g_sig, g_inv, f_fwd, f_inv)


def _attn_body(q_ref, k_ref, v_ref, o_ref, lse_ref, *, n_side, qb):
    lr = q_ref.shape[2]
    kw = qb + 2 * n_side
    nh = HEADS_PER_GROUP
    lane_head = lax.broadcasted_iota(jnp.int32, (qb, ATTN_OUT), 1) // HEAD_DIM
    qi = lax.broadcasted_iota(jnp.int32, (nh * qb, kw), 0) % qb
    ki = lax.broadcasted_iota(jnp.int32, (nh * qb, kw), 1)
    band = [jnp.where(jnp.abs(ki - qi - shift) <= n_side, 0.0, -1e30) for shift in (0, n_side, 2 * n_side)]

    def scores(cls, i):
        q0 = pl.multiple_of(i * qb, qb)
        w0 = pl.multiple_of(jnp.clip(q0 - n_side, 0, lr - kw), n_side)
        q = q_ref[0, cls, pl.ds(q0, qb), :]
        shift = q0 - w0
        bias = jnp.where(shift == n_side, band[1], jnp.where(shift == 0, band[0], band[2]))
        qs = jnp.concatenate([jnp.where(lane_head == h, q, jnp.zeros_like(q)) for h in range(nh)], axis=0)
        return q0, w0, _dot_t(qs, k_ref[0, cls, pl.ds(w0, kw), :]) + bias

    def softmax(s):
        m = jnp.max(s, axis=-1, keepdims=True)
        p = jnp.exp(s - m)
        l = jnp.sum(p, axis=-1, keepdims=True)
        return p.astype(BF16), l, m + jnp.log(l)

    def store(cls, q0, pv, lse_rows):
        o = jnp.zeros((qb, ATTN_OUT), F32)
        lse = jnp.zeros((qb, ATTN_OUT), F32)
        for h in range(nh):
            mine = lane_head == h
            o = jnp.where(mine, pv[h * qb:(h + 1) * qb], o)
            lse = jnp.where(mine, lse_rows[h * qb:(h + 1) * qb], lse)
        o_ref[0, pl.ds(q0, qb), cls * ATTN_OUT:(cls + 1) * ATTN_OUT] = o
        lse_ref[0, pl.ds(q0, qb), cls * ATTN_OUT:(cls + 1) * ATTN_OUT] = lse

    per_trip = min(ATTN_BLOCKS, lr // qb)

    def body(cls, j, c):
        sc = [scores(cls, per_trip * j + t) for t in range(per_trip)]
        sm = [softmax(s) for _, _, s in sc]
        pvs = [_dot(p, v_ref[0, cls, pl.ds(w0, kw), :]) / l for (_, w0, _), (p, l, _) in zip(sc, sm)]
        for (q0, _, _), (_, _, lse_rows), pv in zip(sc, sm, pvs):
            store(cls, q0, pv, lse_rows)
        return c

    for cls in range(q_ref.shape[1]):
        lax.fori_loop(0, lr // (qb * per_trip), functools.partial(body, cls), 0)


def _attn_group(q, k, v, g, n_side, qb=128, max_classes=8):
    b, dil, lr, _ = q.shape
    cg = min(dil, max_classes)
    in_spec = pl.BlockSpec((1, cg, lr, ATTN_OUT), lambda bi, r: (bi, r, 0, 0))
    out_spec = pl.BlockSpec((1, lr, cg * ATTN_OUT), lambda bi, r: (bi, 0, r))
    o, lse = pl.pallas_call(
        functools.partial(_attn_body, n_side=n_side, qb=qb),
        grid=(b, dil // cg),
        in_specs=[in_spec, in_spec, in_spec],
        out_specs=[out_spec, out_spec],
        out_shape=[jax.ShapeDtypeStruct((b, lr, dil * ATTN_OUT), F32)] * 2,
        compiler_params=_cparams(("arbitrary", "arbitrary")),
        name=f"attn{g}",
    )(q, k, v)
    return o, lse


def _memkv_body(mem_ref, g_ref, w_ref, kv_ref):
    mn = _rms(mem_ref[0], g_ref[...]).astype(BF16)
    kv_ref[0] = _dot(mn, w_ref[...]).astype(BF16)


def _memkv(mem, g, w_kv):
    b, m, d = mem.shape
    n = w_kv.shape[1]
    return pl.pallas_call(
        _memkv_body,
        grid=(b,),
        in_specs=[pl.BlockSpec((1, m, d), lambda i: (i, 0, 0)),
                  pl.BlockSpec((1, d), lambda i: (0, 0)),
                  pl.BlockSpec((d, n), lambda i: (0, 0))],
        out_specs=pl.BlockSpec((1, m, n), lambda i: (i, 0, 0)),
        out_shape=jax.ShapeDtypeStruct((b, m, n), BF16),
        compiler_params=_cparams(("arbitrary",)),
        name="memkv",
    )(mem, g, w_kv)


def _merge_staged_body(x_ref, z_ref, o0_ref, o1_ref, o2_ref, l0_ref, l1_ref, l2_ref, gate_ref,
                       wuh_ref, wua_ref, wout_ref, gx_ref, wq_ref, kv_ref, wo_ref, gm_ref, wr_ref,
                       p1_ref, p2_ref, x2_ref, hm_ref, aff_ref, affh_ref, affl_ref, *, nsplit):
    d = x_ref.shape[1]
    rows = x_ref.shape[0] // nsplit
    parts = range(nsplit)
    sls = [slice(p * rows, (p + 1) * rows) for p in parts]

    def natural(ref, perm_ref, part):
        dil = ref.shape[2] // ATTN_OUT
        rr = rows // dil
        blk = ref[0, part * rr:(part + 1) * rr, :]
        if dil == 1:
            return blk
        pieces = [blk[:, r * ATTN_OUT:(r + 1) * ATTN_OUT] for r in range(dil)]
        if rr % SUBLANES == 0 and dil % SUBLANES == 0:
            return jnp.swapaxes(jnp.stack(pieces, axis=0), 0, 1).reshape(rows, ATTN_OUT)
        hi, lo = _split(jnp.concatenate(pieces, axis=0))
        return _dot(perm_ref[...], hi) + _dot(perm_ref[...], lo)

    def combine(part):
        o0, o1, o2 = (natural(r, p, part) for r, p in ((o0_ref, None), (o1_ref, p1_ref), (o2_ref, p2_ref)))
        l0, l1, l2 = (natural(r, p, part) for r, p in ((l0_ref, None), (l1_ref, p1_ref), (l2_ref, p2_ref)))
        mx = jnp.maximum(jnp.maximum(l0, l1), l2)
        e0, e1, e2 = jnp.exp(l0 - mx), jnp.exp(l1 - mx), jnp.exp(l2 - mx)
        return ((e0 * o0 + e1 * o1 + e2 * o2) / (e0 + e1 + e2)).astype(BF16)

    attn = [combine(p) for p in parts]
    y_hy = [_dot(z_ref[sl, :].astype(BF16), wuh_ref[...]) for sl in sls]
    y_at = [_dot(a, wua_ref[...]) for a in attn]

    def mixed(sl, yh, ya):
        gates = gate_ref[sl, :].astype(F32)
        return (gates[:, :d] * yh + gates[:, d:] * ya).astype(BF16)

    mix = [mixed(sl, yh, ya) for sl, yh, ya in zip(sls, y_hy, y_at)]
    x1 = [x_ref[sl, :] + _dot(m, wout_ref[...]) for sl, m in zip(sls, mix)]

    hx = [_rms(v, gx_ref[...]).astype(BF16) for v in x1]
    qx = [_dot(h, wq_ref[...]).astype(BF16) for h in hx]
    kv = kv_ref[0]
    xw = X_HEADS * X_HEAD_DIM

    def cross_attention(q):
        heads = []
        for h in range(X_HEADS):
            hc = slice(h * X_HEAD_DIM, (h + 1) * X_HEAD_DIM)
            s = _dot_t(q[:, hc], kv[:, hc]) * (X_HEAD_DIM ** -0.5)
            m = jnp.max(s, axis=-1, keepdims=True)
            p = jnp.exp(s - m)
            p = p / jnp.sum(p, axis=-1, keepdims=True)
            heads.append(_dot(p.astype(BF16), kv[:, xw + h * X_HEAD_DIM:xw + (h + 1) * X_HEAD_DIM]))
        return jnp.concatenate(heads, axis=-1).astype(BF16)

    ox = [cross_attention(q) for q in qx]
    x2 = [v + _dot(o, wo_ref[...]) for v, o in zip(x1, ox)]
    for sl, v in zip(sls, x2):
        x2_ref[sl, :] = v

    hm = [_split(_rms(v, gm_ref[...])) for v in x2]
    cross = [_dot(jnp.concatenate([hi, lo], axis=0), wr_ref[...]) for hi, lo in hm]
    for sl, (hi, _), c in zip(sls, hm, cross):
        hm_ref[sl, :] = hi
        logits = (c[:rows, :LANES] + c[:rows, LANES:]) + (c[rows:, :LANES] + c[rows:, LANES:])
        lane = lax.broadcasted_iota(jnp.int32, logits.shape, 1)
        logits = jnp.where(lane < N_EXPERTS, logits, -1e30)
        m = jnp.max(logits, axis=-1, keepdims=True)
        p = jnp.exp(logits - m)
        aff = p / jnp.sum(p, axis=-1, keepdims=True)
        aff_ref[sl, :] = aff
        affh_ref[sl, :], affl_ref[sl, :] = _split(aff)


def _merge(x2d, z2d, outs, lses, gates, wuh, wua, wout, gx, wq, kv, wo, gm, wr_cat, seq, tm=1024, nsplit=4):
    n, d = x2d.shape
    nseq = seq // tm
    row = lambda i: (i, 0)
    const = lambda i: (0, 0)

    def rspec(a):
        return pl.BlockSpec((tm, a.shape[1]), row)

    def aspec(a):
        dil = a.shape[2] // ATTN_OUT
        return pl.BlockSpec((1, tm // dil, a.shape[2]), lambda i: (i // nseq, i % nseq, 0))

    def cspec(a):
        return pl.BlockSpec(a.shape, const)

    rows = tm // nsplit
    perms = []
    for a in outs[1:]:
        dil = a.shape[2] // ATTN_OUT
        src = (np.arange(rows) % dil) * (rows // dil) + np.arange(rows) // dil
        perms.append(jnp.asarray(np.eye(rows)[src], dtype=BF16))

    in_arrays = [x2d, z2d, *outs, *lses, gates, wuh, wua, wout, gx, wq, kv, wo, gm, wr_cat, *perms]
    in_specs = ([rspec(a) for a in in_arrays[:2]] + [aspec(a) for a in in_arrays[2:8]] + [rspec(gates)]
                + [cspec(a) for a in in_arrays[9:14]]
                + [pl.BlockSpec((1,) + kv.shape[1:], lambda i: (i // nseq, 0, 0))]
                + [cspec(a) for a in in_arrays[15:]])
    return pl.pallas_call(
        functools.partial(_merge_staged_body, nsplit=nsplit),
        grid=(n // tm,),
        in_specs=in_specs,
        out_specs=[pl.BlockSpec((tm, d), row), pl.BlockSpec((tm, d), row)] + [pl.BlockSpec((tm, LANES), row)] * 3,
        out_shape=[jax.ShapeDtypeStruct((n, d), F32), jax.ShapeDtypeStruct((n, d), BF16),
                   jax.ShapeDtypeStruct((n, LANES), F32), jax.ShapeDtypeStruct((n, LANES), BF16),
                   jax.ShapeDtypeStruct((n, LANES), BF16)],
        compiler_params=_cparams(("arbitrary",), vmem=VMEM_LIMIT_LARGE),
        name="merge",
    )(*in_arrays)


def _topk_body(aff_ref, tri_ref, rank_ref, bounds_ref, *, cap, tok_block):
    a = aff_ref[0]
    e, s = a.shape

    def count(mask):
        return jnp.sum(jnp.where(mask, 1.0, 0.0), axis=-1, keepdims=True)

    def as_float(bits):
        return pltpu.bitcast(jnp.broadcast_to(bits, (e, LANES)), F32)[:, 0:1]

    def search(i, thr):
        cand = thr | (jnp.int32(1) << (30 - i))
        return jnp.where(count(a >= as_float(cand)) >= cap, cand, thr)

    thr = as_float(lax.fori_loop(0, 31, search, jnp.zeros((e, 1), jnp.int32)))
    gt = a > thr
    eq = a == thr
    need = cap - count(gt)

    def prefix_excl(mask):
        mf = jnp.where(mask, 1.0, 0.0)
        parts = []
        carry = jnp.zeros((e, 1), F32)
        for c in range(s // LANES):
            blk = mf[:, c * LANES:(c + 1) * LANES]
            inc = _dot(blk.astype(BF16), tri_ref[...])
            parts.append(inc - blk + carry)
            carry = carry + inc[:, LANES - 1:LANES]
        return jnp.concatenate(parts, axis=-1)

    sel = gt | (eq & (prefix_excl(eq) < need))
    excl = prefix_excl(sel)
    rank_ref[0] = jnp.where(sel, excl, -1.0)

    tok = lax.broadcasted_iota(jnp.int32, (e, s), 1)
    lane = lax.broadcasted_iota(jnp.int32, (e, LANES), 1)
    bounds = jnp.zeros((e, LANES), F32)
    for j in range(s // tok_block + 1):
        bounds = jnp.where(lane == j, count(sel & (tok < j * tok_block)), bounds)
    bounds_ref[0] = bounds.astype(jnp.int32)


def _topk(aff_t, cap, tok_block):
    b, e, s = aff_t.shape
    assert s // tok_block + 1 <= LANES
    tri = jnp.asarray(np.triu(np.ones((LANES, LANES))), dtype=BF16)
    return pl.pallas_call(
        functools.partial(_topk_body, cap=cap, tok_block=tok_block),
        grid=(b,),
        in_specs=[pl.BlockSpec((1, e, s), lambda i: (i, 0, 0)),
                  pl.BlockSpec((LANES, LANES), lambda i: (0, 0))],
        out_specs=[pl.BlockSpec((1, e, s), lambda i: (i, 0, 0)),
                   pl.BlockSpec((1, e, LANES), lambda i: (i, 0, 0))],
        out_shape=[jax.ShapeDtypeStruct((b, e, s), F32),
                   jax.ShapeDtypeStruct((b, e, LANES), jnp.int32)],
        compiler_params=_cparams(("arbitrary",)),
        name="topk",
    )(aff_t, tri)


def _moe_windows(bounds_ref, base, ne, step, tokens, window):
    per = tokens // MOE_BOUND_STEP
    starts = []
    passes = jnp.int32(0)
    for e in range(ne):
        r_lo = bounds_ref[base + e * LANES + step * per]
        r_hi = bounds_ref[base + e * LANES + (step + 1) * per]
        ws = (r_lo // BF16_ROWS) * BF16_ROWS
        starts.append(ws)
        passes = jnp.maximum(passes, (r_hi - ws + window - 1) // window)
    return starts, passes


def _gather_body(bounds_ref, rank_ref, hm_ref, affh_ref, affl_ref, xe_ref, gs_ref, *, cap, eg):
    w = GATHER_WINDOW
    ne, tc = rank_ref.shape[1], rank_ref.shape[2]
    bi = pl.program_id(0)
    ci = pl.program_id(1)

    @pl.when(ci == 0)
    def _():
        xe_ref[...] = jnp.zeros_like(xe_ref)
        gs_ref[...] = jnp.zeros_like(gs_ref)

    starts, passes = _moe_windows(bounds_ref, bi * ne * LANES, ne, ci, tc, w)
    iota_w = lax.broadcasted_iota(jnp.int32, (w, 1), 0)

    def selector(p, g0):
        lo = [starts[e] + p * w for e in range(g0, g0 + eg)]
        phys = [pl.multiple_of(jnp.minimum(v, cap - w), BF16_ROWS) for v in lo]
        rk = jnp.concatenate([jnp.broadcast_to(rank_ref[0, e:e + 1, :], (w, tc))
                              for e in range(g0, g0 + eg)], axis=0)
        slot = jnp.concatenate([iota_w + v for v in phys], axis=0).astype(F32)
        lom = jnp.concatenate([jnp.zeros_like(iota_w) + v for v in lo], axis=0).astype(F32)
        return phys, jnp.where((rk == slot) & (slot >= lom), 1.0, 0.0).astype(BF16)

    def one_pass(p, carry):
        groups = range(0, ne, eg)
        sels = [selector(p, g0) for g0 in groups]
        res = [_dot(onehot, hm_ref[0]) for _, onehot in sels]
        resg = [_dot(onehot, affh_ref[0]) + _dot(onehot, affl_ref[0]) for _, onehot in sels]
        for g0, (phys, _), r, rg in zip(groups, sels, res, resg):
            for k, e in enumerate(range(g0, g0 + eg)):
                rows = pl.ds(phys[k], w)
                xe_ref[0, e, rows, :] = (xe_ref[0, e, rows, :].astype(F32) + r[k * w:(k + 1) * w]).astype(BF16)
                gs_ref[0, e, rows, :] = gs_ref[0, e, rows, :] + rg[k * w:(k + 1) * w]
        return carry

    lax.fori_loop(0, passes, one_pass, 0)


def _gather(bounds, rank, hm, aff_hi, aff_lo, cap, eg=8):
    b, e, s = rank.shape
    d = hm.shape[2]
    tc = GATHER_TOKENS
    return pl.pallas_call(
        functools.partial(_gather_body, cap=cap, eg=eg),
        grid_spec=pltpu.PrefetchScalarGridSpec(
            num_scalar_prefetch=1,
            grid=(b, s // tc),
            in_specs=[pl.BlockSpec((1, e, tc), lambda bi, ci, bnd: (bi, 0, ci)),
                      pl.BlockSpec((1, tc, d), lambda bi, ci, bnd: (bi, ci, 0)),
                      pl.BlockSpec((1, tc, LANES), lambda bi, ci, bnd: (bi, ci, 0)),
                      pl.BlockSpec((1, tc, LANES), lambda bi, ci, bnd: (bi, ci, 0))],
            out_specs=[pl.BlockSpec((1, e, cap, d), lambda bi, ci, bnd: (bi, 0, 0, 0)),
                       pl.BlockSpec((1, e, cap, LANES), lambda bi, ci, bnd: (bi, 0, 0, 0))],
        ),
        out_shape=[jax.ShapeDtypeStruct((b, e, cap, d), BF16),
                   jax.ShapeDtypeStruct((b, e, cap, LANES), F32)],
        compiler_params=_cparams(("arbitrary", "arbitrary")),
        name="gather",
    )(bounds.reshape(-1), rank, hm, aff_hi, aff_lo)


def _ffn_body(xe_ref, gs_ref, wg_ref, wu_ref, wd_ref, ye_ref, acc_ref, *, mb):
    f = pl.program_id(1)
    nb, _, cap, d = xe_ref.shape
    wg = wg_ref[0].astype(BF16)
    wu = wu_ref[0].astype(BF16)
    wd = wd_ref[0].astype(BF16)

    @pl.when(f == 0)
    def _():
        acc_ref[...] = jnp.zeros_like(acc_ref)

    for b0 in range(0, nb, mb):
        xe = xe_ref[b0:b0 + mb, 0].reshape(mb * cap, d)
        a = _dot(xe, wg)
        u = _dot(xe, wu)
        hsw = (a * jax.nn.sigmoid(a) * u).astype(BF16)
        acc_ref[b0:b0 + mb] += _dot(hsw, wd).reshape(mb, cap, d)

    @pl.when(f == pl.num_programs(1) - 1)
    def _():
        lane = lax.broadcasted_iota(jnp.int32, gs_ref.shape[2:], 1)
        mine = lane == pl.program_id(0)
        for b in range(nb):
            gate = jnp.sum(jnp.where(mine, gs_ref[b, 0], 0.0), axis=-1, keepdims=True)
            ye_ref[b, 0] = (acc_ref[b] * gate).astype(BF16)


def _ffn(xe, gs, wg, wu, wd, ft=1024, mb=1):
    b, e, cap, d = xe.shape
    ff = wg.shape[2]
    return pl.pallas_call(
        functools.partial(_ffn_body, mb=mb),
        grid=(e, ff // ft),
        in_specs=[pl.BlockSpec((b, 1, cap, d), lambda ei, fi: (0, ei, 0, 0)),
                  pl.BlockSpec((b, 1, cap, LANES), lambda ei, fi: (0, ei, 0, 0)),
                  pl.BlockSpec((1, d, ft), lambda ei, fi: (ei, 0, fi)),
                  pl.BlockSpec((1, d, ft), lambda ei, fi: (ei, 0, fi)),
                  pl.BlockSpec((1, ft, d), lambda ei, fi: (ei, fi, 0))],
        out_specs=pl.BlockSpec((b, 1, cap, d), lambda ei, fi: (0, ei, 0, 0)),
        out_shape=jax.ShapeDtypeStruct((b, e, cap, d), BF16),
        scratch_shapes=[pltpu.VMEM((b, cap, d), F32)],
        compiler_params=_cparams(("arbitrary", "arbitrary"), vmem=VMEM_LIMIT_LARGE),
        name="ffn",
    )(xe, gs, wg, wu, wd)


def _scatter_body(bounds_ref, rank_ref, ye_ref, x2_ref, g_ref, out_ref, acc_ref, *, eg):
    w = SCATTER_WINDOW
    ts = x2_ref.shape[1]
    ne, cap = ye_ref.shape[1], ye_ref.shape[2]
    starts, passes = _moe_windows(bounds_ref, pl.program_id(0) * ne * LANES, ne, pl.program_id(1), ts, w)
    rank = rank_ref[0]
    acc_ref[...] = x2_ref[0]
    iota_w = lax.broadcasted_iota(jnp.int32, (1, w), 1)

    def operands(p, g0):
        onehots, rows = [], []
        for e in range(g0, g0 + eg):
            lo = starts[e] + p * w
            phys = pl.multiple_of(jnp.minimum(lo, cap - w), BF16_ROWS)
            slot = (iota_w + phys).astype(F32)
            hit = (rank[:, e:e + 1] == slot) & (slot >= lo.astype(F32))
            onehots.append(jnp.where(hit, 1.0, 0.0).astype(BF16))
            rows.append(ye_ref[0, e, pl.ds(phys, w), :])
        return jnp.concatenate(onehots, axis=1), jnp.concatenate(rows, axis=0)

    def one_pass(p, carry):
        ops = [operands(p, g0) for g0 in range(0, ne, eg)]
        total = _dot(*ops[0])
        for lhs, rhs in ops[1:]:
            total = total + _dot(lhs, rhs)
        acc_ref[...] += total
        return carry

    lax.fori_loop(0, passes, one_pass, 0)
    out_ref[0] = _rms(acc_ref[...], g_ref[...])


def _scatter(bounds, rank_t, ye, x2, g, eg=8):
    b, s, e = rank_t.shape
    cap, d = ye.shape[2], ye.shape[3]
    ts = SCATTER_TOKENS
    return pl.pallas_call(
        functools.partial(_scatter_body, eg=eg),
        grid_spec=pltpu.PrefetchScalarGridSpec(
            num_scalar_prefetch=1,
            grid=(b, s // ts),
            in_specs=[pl.BlockSpec((1, ts, e), lambda bi, ti, bnd: (bi, ti, 0)),
                      pl.BlockSpec((1, e, cap, d), lambda bi, ti, bnd: (bi, 0, 0, 0)),
                      pl.BlockSpec((1, ts, d), lambda bi, ti, bnd: (bi, ti, 0)),
                      pl.BlockSpec((1, d), lambda bi, ti, bnd: (0, 0))],
            out_specs=pl.BlockSpec((1, ts, d), lambda bi, ti, bnd: (bi, ti, 0)),
            scratch_shapes=[pltpu.VMEM((ts, d), F32)],
        ),
        out_shape=jax.ShapeDtypeStruct((b, s, d), F32),
        compiler_params=_cparams(("arbitrary", "arbitrary")),
        name="scatter",
    )(bounds.reshape(-1), rank_t, ye, x2, g)


def kernel(x, mem, ln_mix_g, w_in, hy_conv_w, hy_conv_b, filt_w1, filt_b1, filt_freq1, filt_w2, filt_b2,
           filt_freq2, filt_w3, hy_skip, w_up_hy, w_up_attn, w_gate, b_gate, w_out, ln_x_g, ln_mem_g,
           w_q_x, w_kv_mem, w_o_x, ln_moe_g, w_router, w_e_gate, w_e_up, w_e_down, ln_f_g):
    b, s, d = x.shape
    n = b * s
    x2d = x.reshape(n, d)

    half = HEAD_DIM // 2
    inv = ROPE_THETA ** (-jnp.arange(0, HEAD_DIM, 2, dtype=F32) / HEAD_DIM)
    ang = jnp.arange(s, dtype=F32)[:, None] * inv[None, :]
    lane = np.arange(LANES)
    cos_t = jnp.cos(ang)[:, lane % half]
    sin_t = jnp.sin(ang)[:, lane % half] * jnp.asarray(np.where(lane % HEAD_DIM < half, -1.0, 1.0), F32)[None, :]

    w_all = jnp.concatenate([w_in, w_gate], axis=1).astype(BF16)
    p_hy, *qkv, gates = _proj(x2d, ln_mix_g[None], w_all, b_gate[None], cos_t, sin_t, b, s)

    t_col = jnp.linspace(0.0, 1.0, s, dtype=F32)[:, None]
    grid_col = 2.0 * math.pi * jnp.arange(s, dtype=F32)[:, None] / s
    bands = jnp.linspace(1e-4, HY_BANDS - 1, HY_BANDS, dtype=F32)[None, :]
    feat = _filt(t_col, grid_col, bands, filt_w1, filt_b1, filt_freq1, filt_w2, filt_b2, filt_freq2)
    delta = jnp.linspace(math.log(HY_TARGET) / HY_SLOW_PCT, math.log(HY_TARGET) / HY_FAST_PCT,
                         HY_WIDTH, dtype=F32)[None, :]
    tables = _fft_tables(s)
    khat = _hyfilt(t_col, delta, feat, filt_w3, tables[1], tables[3])
    z = _hyena(p_hy, hy_conv_w, hy_conv_b[None], hy_skip, khat, tables)

    outs, lses = [], []
    for g, (window, dil) in enumerate(DIL_PAIRS):
        o_g, l_g = _attn_group(*qkv[3 * g:3 * g + 3], g, window // (2 * dil))
        outs.append(o_g)
        lses.append(l_g)

    kv = _memkv(mem, ln_mem_g[None], w_kv_mem.astype(BF16))
    wr_pad = jnp.pad(w_router, ((0, 0), (0, LANES - N_EXPERTS)))
    wr_cat = jnp.concatenate(_split(wr_pad), axis=1)
    x2, hm, aff, aff_hi, aff_lo = _merge(x2d, z.reshape(n, HY_WIDTH), outs, lses, gates,
                         w_up_hy.astype(BF16), w_up_attn.astype(BF16), w_out.astype(BF16),
                         ln_x_g[None], w_q_x.astype(BF16), kv, w_o_x.astype(BF16), ln_moe_g[None],
                         wr_cat, s)

    cap = max(1, EC_FACTOR * s // N_EXPERTS)
    aff3 = aff.reshape(b, s, LANES)
    rank, bounds = _topk(aff3[:, :, :N_EXPERTS].transpose(0, 2, 1), cap, MOE_BOUND_STEP)
    xe, gs = _gather(bounds, rank, hm.reshape(b, s, d), aff_hi.reshape(b, s, LANES), aff_lo.reshape(b, s, LANES), cap)
    ye = _ffn(xe, gs, w_e_gate, w_e_up, w_e_down)
    return _scatter(bounds, rank.transpose(0, 2, 1), ye, x2.reshape(b, s, d), ln_f_g[None])
```

```python
import functools
import math

import numpy as np
import jax
import jax.numpy as jnp
from jax import lax
from jax.experimental import pallas as pl
from jax.experimental.pallas import tpu as pltpu

F32 = jnp.float32
BF16 = jnp.bfloat16
HIGHEST = lax.Precision.HIGHEST

EPS = 1e-6
HY_WIDTH = 768
HY_BANDS = 16
HY_FFN = 64
HY_FAST_PCT = 0.3
HY_SLOW_PCT = 1.5
HY_TARGET = 1e-2
HEAD_DIM = 64
HEADS_PER_GROUP = 4
DIL_PAIRS = ((128, 1), (512, 4), (2048, 16))
N_GROUPS = len(DIL_PAIRS)
ATTN_WIDTH = N_GROUPS * HEADS_PER_GROUP * HEAD_DIM
ATTN_OUT = HEADS_PER_GROUP * HEAD_DIM
ROPE_THETA = 10000.0
X_HEADS = 4
X_HEAD_DIM = 128
N_EXPERTS = 16
EC_FACTOR = 2

LANES = 128
SUBLANES = 8
MXU_COLS = 256
VMEM_LIMIT = 56 * 1024 * 1024
VMEM_LIMIT_LARGE = 60 * 1024 * 1024

FFT_N2 = 128
HY_CB = 128
ATTN_BLOCKS = 4
FFT_UNROLL = 8
FFT_PAD = 8
MOE_BOUND_STEP = 256
GATHER_TOKENS, GATHER_WINDOW = 256, 64
SCATTER_TOKENS, SCATTER_WINDOW = 512, 128
BF16_ROWS = 16


def _cparams(sem, vmem=VMEM_LIMIT):
    return pltpu.CompilerParams(dimension_semantics=sem, vmem_limit_bytes=vmem)


def _rms(x, g):
    return x * lax.rsqrt(jnp.mean(x * x, axis=-1, keepdims=True) + EPS) * g


def _dot(a, b):
    return jnp.dot(a, b, preferred_element_type=F32)


def _dot_hi(a, b):
    return jnp.dot(a, b, precision=HIGHEST, preferred_element_type=F32)


def _split(a):
    hi = a.astype(BF16)
    return hi, (a - hi.astype(F32)).astype(BF16)


def _dot_split(a, b):
    a_hi, a_lo = _split(a)
    b_hi, b_lo = _split(b)
    return _dot(a_hi, b_hi) + _dot(a_lo, b_hi) + _dot(a_hi, b_lo)


def _dot_t(a, b):
    return lax.dot_general(a, b, (((1,), (1,)), ((), ())), preferred_element_type=F32)


def _proj_body(x_ref, g_ref, w_ref, wg_ref, bg_ref, cos_ref, sin_ref, *rest, hyc, aw, gr):
    ng = N_GROUPS
    perm_refs = (None,) + rest[:ng - 1]
    phy_ref = rest[ng - 1]
    qkv_refs = rest[ng:ng + 3 * ng]
    gate_ref = rest[ng + 3 * ng]
    tm = x_ref.shape[0]
    sls = [slice(r0, r0 + gr) for r0 in range(0, tm, gr)]
    lane = lax.broadcasted_iota(jnp.int32, (gr, LANES), 1)
    first = (lane % HEAD_DIM) < (HEAD_DIM // 2)
    gl = ATTN_OUT // LANES

    def rope(t, sl, scale):
        cos, sin = cos_ref[sl, :], sin_ref[sl, :]
        chunks = []
        for j in range(aw // LANES):
            tj = t[:, j * LANES:(j + 1) * LANES]
            partner = jnp.where(first, pltpu.roll(tj, LANES - HEAD_DIM // 2, 1),
                                pltpu.roll(tj, HEAD_DIM // 2, 1))
            chunks.append(((tj * cos + partner * sin) * scale).astype(BF16))
        return chunks

    hs = [_rms(x_ref[sl, :], g_ref[...]).astype(BF16) for sl in sls]
    phys = [_dot(h, w_ref[:, :hyc]) for h in hs]
    qf = [_dot(h, w_ref[:, hyc:hyc + aw]) for h in hs]
    kf = [_dot(h, w_ref[:, hyc + aw:hyc + 2 * aw]) for h in hs]
    vf = [_dot(h, w_ref[:, hyc + 2 * aw:hyc + 3 * aw]).astype(BF16) for h in hs]
    gf = [_dot(h, wg_ref[...]) for h in hs]
    for sl, phy in zip(sls, phys):
        for j in range(hyc // HY_CB):
            phy_ref[0, j, sl, :] = phy[:, j * HY_CB:(j + 1) * HY_CB]
    qs = [rope(t, sl, HEAD_DIM ** -0.5) for t, sl in zip(qf, sls)]
    ks = [rope(t, sl, 1.0) for t, sl in zip(kf, sls)]
    vs = [[t[:, j * LANES:(j + 1) * LANES] for j in range(aw // LANES)] for t in vf]
    for p, (q, k, v) in enumerate(zip(qs, ks, vs)):
        for g in range(ng):
            for i, t in enumerate((q, k, v)):
                out_ref = qkv_refs[3 * g + i]
                tg = jnp.concatenate(t[g * gl:(g + 1) * gl], axis=1)
                if perm_refs[g] is not None:
                    tg = _dot(perm_refs[g][...], tg).astype(BF16)
                dil = out_ref.shape[1]
                rows = gr // dil
                for r in range(dil):
                    out_ref[0, r, p * rows:(p + 1) * rows, :] = tg[r * rows:(r + 1) * rows, :]
    for sl, t in zip(sls, gf):
        gate_ref[sl, :] = jax.nn.sigmoid(t + bg_ref[...]).astype(BF16)


def _proj(x2d, g, w_in, w_gate, b_gate, cos_t, sin_t, batch, seq, tm=512, gr=256):
    n, d = x2d.shape
    hyc = 3 * HY_WIDTH
    aw = ATTN_WIDTH
    gd = w_gate.shape[1]
    assert w_in.shape[1] == hyc + 3 * aw
    nseq = seq // tm
    row = lambda i: (i, 0)
    const = lambda i: (0, 0)
    perms = []
    for _, dil in DIL_PAIRS[1:]:
        rows = gr // dil
        src = (np.arange(gr) % rows) * dil + np.arange(gr) // rows
        perms.append(jnp.asarray(np.eye(gr)[src], dtype=BF16))
    qkv_specs, qkv_shapes = [], []
    for _, dil in DIL_PAIRS:
        for _ in range(3):
            qkv_specs.append(pl.BlockSpec((1, dil, tm // dil, ATTN_OUT), lambda i: (i // nseq, 0, i % nseq, 0)))
            qkv_shapes.append(jax.ShapeDtypeStruct((batch, dil, seq // dil, ATTN_OUT), BF16))
    return pl.pallas_call(
        functools.partial(_proj_body, hyc=hyc, aw=aw, gr=gr),
        grid=(n // tm,),
        in_specs=[
            pl.BlockSpec((tm, d), row),
            pl.BlockSpec((1, d), const),
            pl.BlockSpec(w_in.shape, const, pipeline_mode=pl.Buffered(1)),
            pl.BlockSpec(w_gate.shape, const, pipeline_mode=pl.Buffered(1)),
            pl.BlockSpec((1, gd), const),
            pl.BlockSpec((tm, LANES), lambda i: (i % nseq, 0)),
            pl.BlockSpec((tm, LANES), lambda i: (i % nseq, 0)),
        ] + [pl.BlockSpec((gr, gr), const) for _ in perms],
        out_specs=([pl.BlockSpec((1, hyc // HY_CB, tm, HY_CB), lambda i: (i // nseq, 0, i % nseq, 0))]
                   + qkv_specs + [pl.BlockSpec((tm, gd), row)]),
        out_shape=([jax.ShapeDtypeStruct((batch, hyc // HY_CB, seq, HY_CB), F32)] + qkv_shapes
                   + [jax.ShapeDtypeStruct((n, gd), BF16)]),
        compiler_params=_cparams(("arbitrary",)),
        name="proj",
    )(x2d, g, w_in, w_gate, b_gate, cos_t, sin_t, *perms)


def _filt_body(t_ref, grid_ref, bands_ref, w1t_ref, w1c_ref, w1s_ref, b1_ref, f1_ref,
               w2_ref, b2_ref, f2_ref, feat_ref):
    ang = bands_ref[...] * grid_ref[...]
    pre = (t_ref[...] * w1t_ref[...] + _dot_hi(jnp.cos(ang), w1c_ref[...])
           + _dot_hi(-jnp.sin(ang), w1s_ref[...]))
    h = jnp.sin(f1_ref[...] * (pre + b1_ref[...]))
    feat_ref[...] = jnp.sin(f2_ref[...] * (_dot_hi(h, w2_ref[...]) + b2_ref[...]))


def _filt(t_col, grid_col, bands, w1, b1, f1, w2, b2, f2):
    length = t_col.shape[0]
    nb = bands.shape[1]
    args = (t_col, grid_col, bands, w1[0:1], w1[1:1 + nb], w1[1 + nb:], b1[None], f1[None],
            w2, b2[None], f2[None])
    return pl.pallas_call(
        _filt_body,
        out_shape=jax.ShapeDtypeStruct((length, HY_FFN), F32),
        compiler_params=_cparams(None),
        name="filt",
    )(*args)


def _fft_tables(length):
    n = 2 * length
    n2 = FFT_N2
    n1 = n // n2
    h1 = n1 // 2
    k1 = np.arange(n1)[:, None]
    g_sig = np.zeros((n2, 2 * n1, 2 * h1))
    g_flt = np.zeros((n2, 2 * n1, 2 * h1))
    g_inv = np.zeros((n2, 2 * h1, 2 * n1))
    j = np.arange(h1)[None, :]
    for r in range(n2):
        th = 2 * np.pi * (k1 * (n2 * j + r) % n) / n
        gr, gi = np.cos(th), -np.sin(th)
        g_sig[r, 0::2, :h1] = gr
        g_sig[r, 0::2, h1:] = -gi
        g_sig[r, 1::2, :h1] = gi
        g_sig[r, 1::2, h1:] = gr
        g_flt[r, 0::2, :h1] = gr
        g_flt[r, 1::2, :h1] = gi
        m = n2 * (j + 1) - r
        thb = 2 * np.pi * (k1 * m % n) / n
        live = (m < length).astype(np.float64)
        g_flt[r, 0::2, h1:] = np.cos(thb) * live
        g_flt[r, 1::2, h1:] = np.sin(thb) * live
        wr, wi = (np.cos(th) / n).T, (np.sin(th) / n).T
        g_inv[r, :h1, 0::2] = wr
        g_inv[r, :h1, 1::2] = -wi
        g_inv[r, h1:, 0::2] = wi
        g_inv[r, h1:, 1::2] = wr
    a = np.arange(n2)
    th2 = 2 * np.pi * (np.outer(a, a) % n2) / n2
    c2, s2 = np.cos(th2), np.sin(th2)
    f_fwd = np.block([[c2, s2], [-s2, c2]])
    f_inv = np.block([[c2, -s2], [s2, c2]])
    cast = lambda z: jnp.asarray(z, dtype=F32).astype(BF16)
    return cast(g_sig), cast(g_flt), cast(g_inv), cast(f_fwd), cast(f_inv)


def _fft_stage1(src_a_ref, src_b_ref, tab_ref, work_ref, n1, is_filter):
    n2 = FFT_N2
    grp = SUBLANES

    def body(i, c):
        r0 = pl.multiple_of(i * grp, grp)
        b0 = pl.multiple_of(n2 - grp - r0, grp) if is_filter else r0
        a = jnp.swapaxes(src_a_ref[:, pl.ds(r0, grp), :], 0, 1)
        b = jnp.swapaxes(src_b_ref[:, pl.ds(b0, grp), :], 0, 1)
        outs = []
        for s in range(grp):
            rhs = jnp.concatenate([a[s], b[grp - 1 - s] if is_filter else b[s]], axis=0).astype(BF16)
            outs.append(_dot(tab_ref[r0 + s], rhs))
        work_ref[:, pl.ds(r0, grp), :] = jnp.swapaxes(jnp.stack(outs, axis=0), 0, 1)
        return c

    lax.fori_loop(0, n2 // grp, body, 0, unroll=2)


def _hyfilt_body(t_ref, ts_ref, delta_ref, feat_ref, feats_ref, w3f_ref, w3b_ref, gflt_ref, ffwd_ref, khat_ref,
                 hf_ref, hb_ref, work_ref, *, length, rc):
    n2 = FFT_N2
    n1 = 2 * length // n2
    tiles = rc // n2
    cb = hf_ref.shape[2]

    def gen(i, c):
        r0 = pl.multiple_of(i * rc, rc)
        j0 = pl.multiple_of(i * tiles, tiles)
        delta = jnp.abs(delta_ref[...])
        hf = _dot_split(feat_ref[pl.ds(r0, rc), :], w3f_ref[...]) * jnp.exp(-t_ref[pl.ds(r0, rc), :] * delta)
        hb = _dot_split(feats_ref[pl.ds(r0, rc), :], w3b_ref[...]) * jnp.exp(-ts_ref[pl.ds(r0, rc), :] * delta)
        hf_ref[pl.ds(j0, tiles), :n2, :] = hf.reshape(tiles, n2, cb)
        hb_ref[pl.ds(j0, tiles), :n2, :] = hb.reshape(tiles, n2, cb)
        return c

    lax.fori_loop(0, length // rc, gen, 0)
    _fft_stage1(hf_ref, hb_ref, gflt_ref, work_ref, n1, True)

    side = MXU_COLS // cb

    def stage2(i, c):
        k1s = [i * side + v for v in range(side)]
        blks = [work_ref[pl.ds(pl.multiple_of(2 * k1, 2), 2), :n2, :].reshape(2 * n2, cb).astype(BF16)
                for k1 in k1s]
        x = _dot(ffwd_ref[...], jnp.concatenate(blks, axis=1)).astype(BF16)
        for v, k1 in enumerate(k1s):
            khat_ref[0, 0, pl.ds(pl.multiple_of(k1 * (2 * n2), 2 * n2), 2 * n2), :] = x[:, v * cb:(v + 1) * cb]
        return c

    lax.fori_loop(0, n1 // side, stage2, 0, unroll=FFT_UNROLL // side)


def _hyfilt(t_col, delta, feat, w3, g_flt, f_fwd, rc=512):
    s = t_col.shape[0]
    cb = HY_CB
    ncb = HY_WIDTH // cb
    n2 = FFT_N2
    n1 = 2 * s // n2
    norder = w3.shape[1] // (2 * HY_WIDTH)
    one = pl.Buffered(1)

    def full(a):
        nd = a.ndim
        return pl.BlockSpec(a.shape, lambda o, c: (0,) * nd, pipeline_mode=one)

    shift = lambda a: jnp.concatenate([a[1:], jnp.zeros_like(a[:1])], axis=0)
    return pl.pallas_call(
        functools.partial(_hyfilt_body, length=s, rc=rc),
        grid=(norder, ncb),
        in_specs=[full(t_col), full(t_col),
                  pl.BlockSpec((1, cb), lambda o, c: (0, c)),
                  full(feat), full(feat),
                  pl.BlockSpec((HY_FFN, cb), lambda o, c: (0, 2 * ncb * o + c)),
                  pl.BlockSpec((HY_FFN, cb), lambda o, c: (0, 2 * ncb * o + ncb + c)),
                  full(g_flt), full(f_fwd)],
        out_specs=pl.BlockSpec((1, 1, 2 * n1 * n2, cb), lambda o, c: (o, c, 0, 0)),
        out_shape=jax.ShapeDtypeStruct((norder, ncb, 2 * n1 * n2, cb), BF16),
        scratch_shapes=[pltpu.VMEM((n1 // 2, n2 + FFT_PAD, cb), F32)] * 2
        + [pltpu.VMEM((2 * n1, n2 + FFT_PAD, cb), F32)],
        compiler_params=_cparams(("arbitrary", "arbitrary")),
        name="hyfilt",
    )(t_col, shift(t_col), delta, feat, shift(feat), w3, w3, g_flt, f_fwd)


def _hyena_body(pz_ref, phy_hbm, cwz_ref, cwg1_ref, cwg2_ref, cbz_ref, cbg1_ref, cbg2_ref,
                skip_ref, khat_hbm, gsig_ref, ginv_ref, ffwd_ref, finv_ref,
                out_ref,
                za_ref, zb_ref, ya_ref, yb_ref, work_ref, pg1_ref, pg2_ref, khat_ref, sem, ksem, *, length, rc):
    ncb = pl.num_programs(0)
    pair = pl.ds(2 * pl.program_id(1), 2)

    def gate_copy(order, dst_ref):
        src = phy_hbm.at[pair, pl.ds((order + 1) * ncb + pl.program_id(0), 1)]
        return pltpu.make_async_copy(src, dst_ref, sem.at[order])

    gate_copies = [gate_copy(0, pg1_ref), gate_copy(1, pg2_ref)]
    for cp in gate_copies:
        cp.start()

    new_block = pl.program_id(1) == 0
    spec_copies = [pltpu.make_async_copy(khat_hbm.at[o, pl.program_id(0)], khat_ref.at[o], ksem.at[o])
                   for o in range(khat_ref.shape[0])]

    @pl.when(new_block)
    def _():
        for cp in spec_copies:
            cp.start()
    n2 = FFT_N2
    n1 = 2 * length // n2
    h1 = n1 // 2
    tiles = rc // n2
    cb = za_ref.shape[2]
    side = MXU_COLS // cb
    sub = SUBLANES

    def conv3(p_ref, b, r0, w_ref, bias_ref):
        p = p_ref[b, 0, pl.ds(r0, rc), :]
        before = p_ref[b, 0, pl.ds(pl.multiple_of(jnp.maximum(r0 - sub, 0), sub), sub), :][sub - 1:sub, :]
        after = p_ref[b, 0, pl.ds(pl.multiple_of(jnp.minimum(r0 + rc, length - sub), sub), sub), :][0:1, :]
        before = jnp.where(r0 == 0, 0.0, before)
        after = jnp.where(r0 + rc == length, 0.0, after)
        row = lax.broadcasted_iota(jnp.int32, p.shape, 0)
        prev = jnp.where(row == 0, before, pltpu.roll(p, 1, 0))
        nxt = jnp.where(row == rc - 1, after, pltpu.roll(p, rc - 1, 0))
        return prev * w_ref[0:1, :] + p * w_ref[1:2, :] + nxt * w_ref[2:3, :] + bias_ref[...]

    def stage2(o):
        def body(i, c):
            k1s = [i * FFT_UNROLL + u for u in range(FFT_UNROLL)]
            blks = [work_ref[pl.ds(pl.multiple_of(2 * k1, 2), 2), :n2, :].reshape(2 * n2, cb).astype(BF16)
                    for k1 in k1s]
            groups = range(0, FFT_UNROLL, side)
            xs = [_dot(ffwd_ref[...], jnp.concatenate(blks[u:u + side], axis=1)) for u in groups]

            def filtered(u, x):
                ys = []
                for v in range(side):
                    k1 = k1s[u + v]
                    kh = khat_ref[o, pl.ds(pl.multiple_of(k1 * (2 * n2), 2 * n2), 2 * n2), :].astype(F32)
                    xr, xi = x[:n2, v * cb:(v + 1) * cb], x[n2:, v * cb:(v + 1) * cb]
                    kr, ki = kh[:n2], kh[n2:]
                    ys.append(jnp.concatenate([xr * kr - xi * ki, xr * ki + xi * kr], axis=0).astype(BF16))
                return jnp.concatenate(ys, axis=1)

            ys = [filtered(u, x) for u, x in zip(groups, xs)]
            outs = []
            for y in ys:
                out = _dot(finv_ref[...], y)
                outs.extend(out[:, v * cb:(v + 1) * cb] for v in range(side))
            for k1, out in zip(k1s, outs):
                work_ref[pl.ds(pl.multiple_of(2 * k1, 2), 2), :n2, :] = out.reshape(2, n2, cb)
            return c
        lax.fori_loop(0, n1 // FFT_UNROLL, body, 0)

    def stage3():
        grp = SUBLANES

        def body(i, c):
            r0 = pl.multiple_of(i * grp, grp)
            wk = jnp.swapaxes(work_ref[:, pl.ds(r0, grp), :], 0, 1)
            res = jnp.stack([_dot(ginv_ref[r0 + s], wk[s].astype(BF16)) for s in range(grp)], axis=0)
            ya_ref[:, pl.ds(r0, grp), :] = jnp.swapaxes(res[:, :h1], 0, 1)
            yb_ref[:, pl.ds(r0, grp), :] = jnp.swapaxes(res[:, h1:], 0, 1)
            return c
        lax.fori_loop(0, n2 // grp, body, 0, unroll=2)

    def rows3(ref, i):
        return ref.at[pl.ds(pl.multiple_of(i * tiles, tiles), tiles), :n2, :]

    def load_z(i, c):
        r0 = pl.multiple_of(i * rc, rc)
        rows3(za_ref, i)[...] = conv3(pz_ref, 0, r0, cwz_ref, cbz_ref).reshape(tiles, n2, cb)
        rows3(zb_ref, i)[...] = conv3(pz_ref, 1, r0, cwz_ref, cbz_ref).reshape(tiles, n2, cb)
        return c

    lax.fori_loop(0, length // rc, load_z, 0)
    for o, (pg_ref, cw_ref, cb_ref) in enumerate(((pg1_ref, cwg1_ref, cbg1_ref), (pg2_ref, cwg2_ref, cbg2_ref))):
        _fft_stage1(za_ref, zb_ref, gsig_ref, work_ref, n1, False)

        @pl.when(new_block)
        def _():
            spec_copies[o].wait()

        stage2(o)
        stage3()
        skip = skip_ref[o:o + 1, :]
        gate_copies[o].wait()

        def gate(i, c):
            r0 = pl.multiple_of(i * rc, rc)
            rows = pl.ds(r0, rc)
            ya, yb = (rows3(r, i)[...].reshape(rc, cb) for r in (ya_ref, yb_ref))
            za, zb = (rows3(r, i)[...].reshape(rc, cb) for r in (za_ref, zb_ref))
            new_a = conv3(pg_ref, 0, r0, cw_ref, cb_ref) * (ya + skip * za)
            new_b = conv3(pg_ref, 1, r0, cw_ref, cb_ref) * (yb + skip * zb)
            if o == 0:
                rows3(za_ref, i)[...] = new_a.reshape(tiles, n2, cb)
                rows3(zb_ref, i)[...] = new_b.reshape(tiles, n2, cb)
            else:
                out_ref[0, rows, :] = new_a.astype(out_ref.dtype)
                out_ref[1, rows, :] = new_b.astype(out_ref.dtype)
            return c

        lax.fori_loop(0, length // rc, gate, 0)


def _hyena(p_hy, conv_w, conv_b, skip, khat, tables, rc=512):
    b, _, s, _ = p_hy.shape
    w = HY_WIDTH
    cb = HY_CB
    ncb = w // cb
    n2 = FFT_N2
    n1 = 2 * s // n2
    g_sig, _, g_inv, f_fwd, f_inv = tables
    one = pl.Buffered(1)

    def cwspec(off):
        return pl.BlockSpec((3, cb), lambda c, p: (0, off + c))

    def cbspec(off):
        return pl.BlockSpec((1, cb), lambda c, p: (0, off + c))

    def full(a):
        nd = a.ndim
        return pl.BlockSpec(a.shape, lambda c, p: (0,) * nd, pipeline_mode=one)

    in_specs = [
        pl.BlockSpec((2, 1, s, cb), lambda c, p: (p, c, 0, 0)),
        pl.BlockSpec(memory_space=pl.ANY),
        cwspec(0), cwspec(ncb), cwspec(2 * ncb),
        cbspec(0), cbspec(ncb), cbspec(2 * ncb),
        pl.BlockSpec((2, cb), lambda c, p: (0, c)),
        pl.BlockSpec(memory_space=pl.ANY),
        full(g_sig), full(g_inv), full(f_fwd), full(f_inv),
    ]
    return pl.pallas_call(
        functools.partial(_hyena_body, length=s, rc=rc),
        grid=(ncb, b // 2),
        in_specs=in_specs,
        out_specs=pl.BlockSpec((2, s, cb), lambda c, p: (p, 0, c)),
        out_shape=jax.ShapeDtypeStruct((b, s, w), BF16),
        scratch_shapes=[pltpu.VMEM((n1 // 2, n2 + FFT_PAD, cb), F32)] * 4
        + [pltpu.VMEM((2 * n1, n2 + FFT_PAD, cb), F32)]
        + [pltpu.VMEM((2, 1, s, cb), F32)] * 2 + [pltpu.VMEM((khat.shape[0], 2 * n1 * n2, cb), BF16)]
        + [pltpu.SemaphoreType.DMA((2,)), pltpu.SemaphoreType.DMA((khat.shape[0],))],
        compiler_params=_cparams(("arbitrary", "arbitrary")),
        name="hyena",
    )(p_hy, p_hy, conv_w, conv_w, conv_w, conv_b, conv_b, conv_b,
      skip, khat, g_sig, g_inv, f_fwd, f_inv)


def _attn_body(q_ref, k_ref, v_ref, o_ref, lse_ref, *, n_side, qb):
    lr = q_ref.shape[2]
    kw = qb + 2 * n_side
    nh = HEADS_PER_GROUP
    lane_head = lax.broadcasted_iota(jnp.int32, (qb, ATTN_OUT), 1) // HEAD_DIM
    qi = lax.broadcasted_iota(jnp.int32, (nh * qb, kw), 0) % qb
    ki = lax.broadcasted_iota(jnp.int32, (nh * qb, kw), 1)
    band = [jnp.where(jnp.abs(ki - qi - shift) <= n_side, 0.0, -1e30) for shift in (0, n_side, 2 * n_side)]

    def scores(cls, i):
        q0 = pl.multiple_of(i * qb, qb)
        w0 = pl.multiple_of(jnp.clip(q0 - n_side, 0, lr - kw), n_side)
        q = q_ref[0, cls, pl.ds(q0, qb), :]
        shift = q0 - w0
        bias = jnp.where(shift == n_side, band[1], jnp.where(shift == 0, band[0], band[2]))
        qs = jnp.concatenate([jnp.where(lane_head == h, q, jnp.zeros_like(q)) for h in range(nh)], axis=0)
        return q0, w0, _dot_t(qs, k_ref[0, cls, pl.ds(w0, kw), :]) + bias

    def softmax(s):
        m = jnp.max(s, axis=-1, keepdims=True)
        p = jnp.exp(s - m)
        l = jnp.sum(p, axis=-1, keepdims=True)
        return p.astype(BF16), l, m + jnp.log(l)

    def store(cls, q0, pv, lse_rows):
        o = jnp.zeros((qb, ATTN_OUT), F32)
        lse = jnp.zeros((qb, ATTN_OUT), F32)
        for h in range(nh):
            mine = lane_head == h
            o = jnp.where(mine, pv[h * qb:(h + 1) * qb], o)
            lse = jnp.where(mine, lse_rows[h * qb:(h + 1) * qb], lse)
        o_ref[0, pl.ds(q0, qb), cls * ATTN_OUT:(cls + 1) * ATTN_OUT] = o
        lse_ref[0, pl.ds(q0, qb), cls * ATTN_OUT:(cls + 1) * ATTN_OUT] = lse

    per_trip = min(ATTN_BLOCKS, lr // qb)

    def body(cls, j, c):
        sc = [scores(cls, per_trip * j + t) for t in range(per_trip)]
        sm = [softmax(s) for _, _, s in sc]
        pvs = [_dot(p, v_ref[0, cls, pl.ds(w0, kw), :]) / l for (_, w0, _), (p, l, _) in zip(sc, sm)]
        for (q0, _, _), (_, _, lse_rows), pv in zip(sc, sm, pvs):
            store(cls, q0, pv, lse_rows)
        return c

    for cls in range(q_ref.shape[1]):
        lax.fori_loop(0, lr // (qb * per_trip), functools.partial(body, cls), 0)


def _attn_group(q, k, v, g, n_side, qb=128, max_classes=8):
    b, dil, lr, _ = q.shape
    cg = min(dil, max_classes)
    in_spec = pl.BlockSpec((1, cg, lr, ATTN_OUT), lambda bi, r: (bi, r, 0, 0))
    out_spec = pl.BlockSpec((1, lr, cg * ATTN_OUT), lambda bi, r: (bi, 0, r))
    o, lse = pl.pallas_call(
        functools.partial(_attn_body, n_side=n_side, qb=qb),
        grid=(b, dil // cg),
        in_specs=[in_spec, in_spec, in_spec],
        out_specs=[out_spec, out_spec],
        out_shape=[jax.ShapeDtypeStruct((b, lr, dil * ATTN_OUT), F32)] * 2,
        compiler_params=_cparams(("arbitrary", "arbitrary")),
        name=f"attn{g}",
    )(q, k, v)
    return o, lse


def _memkv_body(mem_ref, g_ref, w_ref, kv_ref):
    mn = _rms(mem_ref[0], g_ref[...]).astype(BF16)
    kv_ref[0] = _dot(mn, w_ref[...]).astype(BF16)


def _memkv(mem, g, w_kv):
    b, m, d = mem.shape
    n = w_kv.shape[1]
    return pl.pallas_call(
        _memkv_body,
        grid=(b,),
        in_specs=[pl.BlockSpec((1, m, d), lambda i: (i, 0, 0)),
                  pl.BlockSpec((1, d), lambda i: (0, 0)),
                  pl.BlockSpec((d, n), lambda i: (0, 0))],
        out_specs=pl.BlockSpec((1, m, n), lambda i: (i, 0, 0)),
        out_shape=jax.ShapeDtypeStruct((b, m, n), BF16),
        compiler_params=_cparams(("arbitrary",)),
        name="memkv",
    )(mem, g, w_kv)


def _merge_staged_body(x_ref, z_ref, o0_ref, o1_ref, o2_ref, l0_ref, l1_ref, l2_ref, gate_ref,
                       wuh_ref, wua_ref, wout_ref, gx_ref, wq_ref, kv_ref, wo_ref, gm_ref, wr_ref,
                       p1_ref, p2_ref, x2_ref, hm_ref, aff_ref, affh_ref, affl_ref, *, nsplit):
    d = x_ref.shape[1]
    rows = x_ref.shape[0] // nsplit
    parts = range(nsplit)
    sls = [slice(p * rows, (p + 1) * rows) for p in parts]

    def natural(ref, perm_ref, part):
        dil = ref.shape[2] // ATTN_OUT
        rr = rows // dil
        blk = ref[0, part * rr:(part + 1) * rr, :]
        if dil == 1:
            return blk
        pieces = [blk[:, r * ATTN_OUT:(r + 1) * ATTN_OUT] for r in range(dil)]
        if rr % SUBLANES == 0 and dil % SUBLANES == 0:
            return jnp.swapaxes(jnp.stack(pieces, axis=0), 0, 1).reshape(rows, ATTN_OUT)
        hi, lo = _split(jnp.concatenate(pieces, axis=0))
        return _dot(perm_ref[...], hi) + _dot(perm_ref[...], lo)

    def combine(part):
        o0, o1, o2 = (natural(r, p, part) for r, p in ((o0_ref, None), (o1_ref, p1_ref), (o2_ref, p2_ref)))
        l0, l1, l2 = (natural(r, p, part) for r, p in ((l0_ref, None), (l1_ref, p1_ref), (l2_ref, p2_ref)))
        mx = jnp.maximum(jnp.maximum(l0, l1), l2)
        e0, e1, e2 = jnp.exp(l0 - mx), jnp.exp(l1 - mx), jnp.exp(l2 - mx)
        return ((e0 * o0 + e1 * o1 + e2 * o2) / (e0 + e1 + e2)).astype(BF16)

    attn = [combine(p) for p in parts]
    y_hy = [_dot(z_ref[sl, :].astype(BF16), wuh_ref[...]) for sl in sls]
    y_at = [_dot(a, wua_ref[...]) for a in attn]

    def mixed(sl, yh, ya):
        gates = gate_ref[sl, :].astype(F32)
        return (gates[:, :d] * yh + gates[:, d:] * ya).astype(BF16)

    mix = [mixed(sl, yh, ya) for sl, yh, ya in zip(sls, y_hy, y_at)]
    x1 = [x_ref[sl, :] + _dot(m, wout_ref[...]) for sl, m in zip(sls, mix)]

    hx = [_rms(v, gx_ref[...]).astype(BF16) for v in x1]
    qx = [_dot(h, wq_ref[...]).astype(BF16) for h in hx]
    kv = kv_ref[0]
    xw = X_HEADS * X_HEAD_DIM

    def cross_attention(q):
        heads = []
        for h in range(X_HEADS):
            hc = slice(h * X_HEAD_DIM, (h + 1) * X_HEAD_DIM)
            s = _dot_t(q[:, hc], kv[:, hc]) * (X_HEAD_DIM ** -0.5)
            m = jnp.max(s, axis=-1, keepdims=True)
            p = jnp.exp(s - m)
            p = p / jnp.sum(p, axis=-1, keepdims=True)
            heads.append(_dot(p.astype(BF16), kv[:, xw + h * X_HEAD_DIM:xw + (h + 1) * X_HEAD_DIM]))
        return jnp.concatenate(heads, axis=-1).astype(BF16)

    ox = [cross_attention(q) for q in qx]
    x2 = [v + _dot(o, wo_ref[...]) for v, o in zip(x1, ox)]
    for sl, v in zip(sls, x2):
        x2_ref[sl, :] = v

    hm = [_split(_rms(v, gm_ref[...])) for v in x2]
    cross = [_dot(jnp.concatenate([hi, lo], axis=0), wr_ref[...]) for hi, lo in hm]
    for sl, (hi, _), c in zip(sls, hm, cross):
        hm_ref[sl, :] = hi
        logits = (c[:rows, :LANES] + c[:rows, LANES:]) + (c[rows:, :LANES] + c[rows:, LANES:])
        lane = lax.broadcasted_iota(jnp.int32, logits.shape, 1)
        logits = jnp.where(lane < N_EXPERTS, logits, -1e30)
        m = jnp.max(logits, axis=-1, keepdims=True)
        p = jnp.exp(logits - m)
        aff = p / jnp.sum(p, axis=-1, keepdims=True)
        aff_ref[sl, :] = aff
        affh_ref[sl, :], affl_ref[sl, :] = _split(aff)


def _merge(x2d, z2d, outs, lses, gates, wuh, wua, wout, gx, wq, kv, wo, gm, wr_cat, seq, tm=1024, nsplit=4):
    n, d = x2d.shape
    nseq = seq // tm
    row = lambda i: (i, 0)
    const = lambda i: (0, 0)

    def rspec(a):
        return pl.BlockSpec((tm, a.shape[1]), row)

    def aspec(a):
        dil = a.shape[2] // ATTN_OUT
        return pl.BlockSpec((1, tm // dil, a.shape[2]), lambda i: (i // nseq, i % nseq, 0))

    def cspec(a):
        return pl.BlockSpec(a.shape, const)

    rows = tm // nsplit
    perms = []
    for a in outs[1:]:
        dil = a.shape[2] // ATTN_OUT
        src = (np.arange(rows) % dil) * (rows // dil) + np.arange(rows) // dil
        perms.append(jnp.asarray(np.eye(rows)[src], dtype=BF16))

    in_arrays = [x2d, z2d, *outs, *lses, gates, wuh, wua, wout, gx, wq, kv, wo, gm, wr_cat, *perms]
    in_specs = ([rspec(a) for a in in_arrays[:2]] + [aspec(a) for a in in_arrays[2:8]] + [rspec(gates)]
                + [cspec(a) for a in in_arrays[9:14]]
                + [pl.BlockSpec((1,) + kv.shape[1:], lambda i: (i // nseq, 0, 0))]
                + [cspec(a) for a in in_arrays[15:]])
    return pl.pallas_call(
        functools.partial(_merge_staged_body, nsplit=nsplit),
        grid=(n // tm,),
        in_specs=in_specs,
        out_specs=[pl.BlockSpec((tm, d), row), pl.BlockSpec((tm, d), row)] + [pl.BlockSpec((tm, LANES), row)] * 3,
        out_shape=[jax.ShapeDtypeStruct((n, d), F32), jax.ShapeDtypeStruct((n, d), BF16),
                   jax.ShapeDtypeStruct((n, LANES), F32), jax.ShapeDtypeStruct((n, LANES), BF16),
                   jax.ShapeDtypeStruct((n, LANES), BF16)],
        compiler_params=_cparams(("arbitrary",), vmem=VMEM_LIMIT_LARGE),
        name="merge",
    )(*in_arrays)


def _topk_body(aff_ref, tri_ref, rank_ref, bounds_ref, *, cap, tok_block):
    a = aff_ref[0]
    e, s = a.shape

    def count(mask):
        return jnp.sum(jnp.where(mask, 1.0, 0.0), axis=-1, keepdims=True)

    def as_float(bits):
        return pltpu.bitcast(jnp.broadcast_to(bits, (e, LANES)), F32)[:, 0:1]

    def search(i, thr):
        cand = thr | (jnp.int32(1) << (30 - i))
        return jnp.where(count(a >= as_float(cand)) >= cap, cand, thr)

    thr = as_float(lax.fori_loop(0, 31, search, jnp.zeros((e, 1), jnp.int32)))
    gt = a > thr
    eq = a == thr
    need = cap - count(gt)

    def prefix_excl(mask):
        mf = jnp.where(mask, 1.0, 0.0)
        parts = []
        carry = jnp.zeros((e, 1), F32)
        for c in range(s // LANES):
            blk = mf[:, c * LANES:(c + 1) * LANES]
            inc = _dot(blk.astype(BF16), tri_ref[...])
            parts.append(inc - blk + carry)
            carry = carry + inc[:, LANES - 1:LANES]
        return jnp.concatenate(parts, axis=-1)

    sel = gt | (eq & (prefix_excl(eq) < need))
    excl = prefix_excl(sel)
    rank_ref[0] = jnp.where(sel, excl, -1.0)

    tok = lax.broadcasted_iota(jnp.int32, (e, s), 1)
    lane = lax.broadcasted_iota(jnp.int32, (e, LANES), 1)
    bounds = jnp.zeros((e, LANES), F32)
    for j in range(s // tok_block + 1):
        bounds = jnp.where(lane == j, count(sel & (tok < j * tok_block)), bounds)
    bounds_ref[0] = bounds.astype(jnp.int32)


def _topk(aff_t, cap, tok_block):
    b, e, s = aff_t.shape
    assert s // tok_block + 1 <= LANES
    tri = jnp.asarray(np.triu(np.ones((LANES, LANES))), dtype=BF16)
    return pl.pallas_call(
        functools.partial(_topk_body, cap=cap, tok_block=tok_block),
        grid=(b,),
        in_specs=[pl.BlockSpec((1, e, s), lambda i: (i, 0, 0)),
                  pl.BlockSpec((LANES, LANES), lambda i: (0, 0))],
        out_specs=[pl.BlockSpec((1, e, s), lambda i: (i, 0, 0)),
                   pl.BlockSpec((1, e, LANES), lambda i: (i, 0, 0))],
        out_shape=[jax.ShapeDtypeStruct((b, e, s), F32),
                   jax.ShapeDtypeStruct((b, e, LANES), jnp.int32)],
        compiler_params=_cparams(("arbitrary",)),
        name="topk",
    )(aff_t, tri)


def _moe_windows(bounds_ref, base, ne, step, tokens, window):
    per = tokens // MOE_BOUND_STEP
    starts = []
    passes = jnp.int32(0)
    for e in range(ne):
        r_lo = bounds_ref[base + e * LANES + step * per]
        r_hi = bounds_ref[base + e * LANES + (step + 1) * per]
        ws = (r_lo // BF16_ROWS) * BF16_ROWS
        starts.append(ws)
        passes = jnp.maximum(passes, (r_hi - ws + window - 1) // window)
    return starts, passes


def _gather_body(bounds_ref, rank_ref, hm_ref, affh_ref, affl_ref, xe_ref, gs_ref, *, cap, eg):
    w = GATHER_WINDOW
    ne, tc = rank_ref.shape[1], rank_ref.shape[2]
    bi = pl.program_id(0)
    ci = pl.program_id(1)

    @pl.when(ci == 0)
    def _():
        xe_ref[...] = jnp.zeros_like(xe_ref)
        gs_ref[...] = jnp.zeros_like(gs_ref)

    starts, passes = _moe_windows(bounds_ref, bi * ne * LANES, ne, ci, tc, w)
    iota_w = lax.broadcasted_iota(jnp.int32, (w, 1), 0)

    def selector(p, g0):
        lo = [starts[e] + p * w for e in range(g0, g0 + eg)]
        phys = [pl.multiple_of(jnp.minimum(v, cap - w), BF16_ROWS) for v in lo]
        rk = jnp.concatenate([jnp.broadcast_to(rank_ref[0, e:e + 1, :], (w, tc))
                              for e in range(g0, g0 + eg)], axis=0)
        slot = jnp.concatenate([iota_w + v for v in phys], axis=0).astype(F32)
        lom = jnp.concatenate([jnp.zeros_like(iota_w) + v for v in lo], axis=0).astype(F32)
        return phys, jnp.where((rk == slot) & (slot >= lom), 1.0, 0.0).astype(BF16)

    def one_pass(p, carry):
        groups = range(0, ne, eg)
        sels = [selector(p, g0) for g0 in groups]
        res = [_dot(onehot, hm_ref[0]) for _, onehot in sels]
        resg = [_dot(onehot, affh_ref[0]) + _dot(onehot, affl_ref[0]) for _, onehot in sels]
        for g0, (phys, _), r, rg in zip(groups, sels, res, resg):
            for k, e in enumerate(range(g0, g0 + eg)):
                rows = pl.ds(phys[k], w)
                xe_ref[0, e, rows, :] = (xe_ref[0, e, rows, :].astype(F32) + r[k * w:(k + 1) * w]).astype(BF16)
                gs_ref[0, e, rows, :] = gs_ref[0, e, rows, :] + rg[k * w:(k + 1) * w]
        return carry

    lax.fori_loop(0, passes, one_pass, 0)


def _gather(bounds, rank, hm, aff_hi, aff_lo, cap, eg=16):
    b, e, s = rank.shape
    d = hm.shape[2]
    tc = GATHER_TOKENS
    return pl.pallas_call(
        functools.partial(_gather_body, cap=cap, eg=eg),
        grid_spec=pltpu.PrefetchScalarGridSpec(
            num_scalar_prefetch=1,
            grid=(b, s // tc),
            in_specs=[pl.BlockSpec((1, e, tc), lambda bi, ci, bnd: (bi, 0, ci)),
                      pl.BlockSpec((1, tc, d), lambda bi, ci, bnd: (bi, ci, 0)),
                      pl.BlockSpec((1, tc, LANES), lambda bi, ci, bnd: (bi, ci, 0)),
                      pl.BlockSpec((1, tc, LANES), lambda bi, ci, bnd: (bi, ci, 0))],
            out_specs=[pl.BlockSpec((1, e, cap, d), lambda bi, ci, bnd: (bi, 0, 0, 0)),
                       pl.BlockSpec((1, e, cap, LANES), lambda bi, ci, bnd: (bi, 0, 0, 0))],
        ),
        out_shape=[jax.ShapeDtypeStruct((b, e, cap, d), BF16),
                   jax.ShapeDtypeStruct((b, e, cap, LANES), F32)],
        compiler_params=_cparams(("arbitrary", "arbitrary")),
        name="gather",
    )(bounds.reshape(-1), rank, hm, aff_hi, aff_lo)


def _ffn_body(xe_ref, gs_ref, wg_ref, wu_ref, wd_ref, ye_ref, acc_ref, *, mb):
    f = pl.program_id(1)
    nb, _, cap, d = xe_ref.shape
    wg = wg_ref[0].astype(BF16)
    wu = wu_ref[0].astype(BF16)
    wd = wd_ref[0].astype(BF16)

    @pl.when(f == 0)
    def _():
        acc_ref[...] = jnp.zeros_like(acc_ref)

    for b0 in range(0, nb, mb):
        xe = xe_ref[b0:b0 + mb, 0].reshape(mb * cap, d)
        a = _dot(xe, wg)
        u = _dot(xe, wu)
        hsw = (a * jax.nn.sigmoid(a) * u).astype(BF16)
        acc_ref[b0:b0 + mb] += _dot(hsw, wd).reshape(mb, cap, d)

    @pl.when(f == pl.num_programs(1) - 1)
    def _():
        lane = lax.broadcasted_iota(jnp.int32, gs_ref.shape[2:], 1)
        mine = lane == pl.program_id(0)
        for b in range(nb):
            gate = jnp.sum(jnp.where(mine, gs_ref[b, 0], 0.0), axis=-1, keepdims=True)
            ye_ref[b, 0] = (acc_ref[b] * gate).astype(BF16)


def _ffn(xe, gs, wg, wu, wd, ft=1024, mb=1):
    b, e, cap, d = xe.shape
    ff = wg.shape[2]
    return pl.pallas_call(
        functools.partial(_ffn_body, mb=mb),
        grid=(e, ff // ft),
        in_specs=[pl.BlockSpec((b, 1, cap, d), lambda ei, fi: (0, ei, 0, 0)),
                  pl.BlockSpec((b, 1, cap, LANES), lambda ei, fi: (0, ei, 0, 0)),
                  pl.BlockSpec((1, d, ft), lambda ei, fi: (ei, 0, fi)),
                  pl.BlockSpec((1, d, ft), lambda ei, fi: (ei, 0, fi)),
                  pl.BlockSpec((1, ft, d), lambda ei, fi: (ei, fi, 0))],
        out_specs=pl.BlockSpec((b, 1, cap, d), lambda ei, fi: (0, ei, 0, 0)),
        out_shape=jax.ShapeDtypeStruct((b, e, cap, d), BF16),
        scratch_shapes=[pltpu.VMEM((b, cap, d), F32)],
        compiler_params=_cparams(("arbitrary", "arbitrary"), vmem=VMEM_LIMIT_LARGE),
        name="ffn",
    )(xe, gs, wg, wu, wd)


def _scatter_body(bounds_ref, rank_ref, ye_ref, x2_ref, g_ref, out_ref, acc_ref, *, eg):
    w = SCATTER_WINDOW
    ts = x2_ref.shape[1]
    ne, cap = ye_ref.shape[1], ye_ref.shape[2]
    starts, passes = _moe_windows(bounds_ref, pl.program_id(0) * ne * LANES, ne, pl.program_id(1), ts, w)
    rank = rank_ref[0]
    acc_ref[...] = x2_ref[0]
    iota_w = lax.broadcasted_iota(jnp.int32, (1, w), 1)

    def operands(p, g0):
        onehots, rows = [], []
        for e in range(g0, g0 + eg):
            lo = starts[e] + p * w
            phys = pl.multiple_of(jnp.minimum(lo, cap - w), BF16_ROWS)
            slot = (iota_w + phys).astype(F32)
            hit = (rank[:, e:e + 1] == slot) & (slot >= lo.astype(F32))
            onehots.append(jnp.where(hit, 1.0, 0.0).astype(BF16))
            rows.append(ye_ref[0, e, pl.ds(phys, w), :])
        return jnp.concatenate(onehots, axis=1), jnp.concatenate(rows, axis=0)

    def one_pass(p, carry):
        ops = [operands(p, g0) for g0 in range(0, ne, eg)]
        total = _dot(*ops[0])
        for lhs, rhs in ops[1:]:
            total = total + _dot(lhs, rhs)
        acc_ref[...] += total
        return carry

    lax.fori_loop(0, passes, one_pass, 0)
    out_ref[0] = _rms(acc_ref[...], g_ref[...])


def _scatter(bounds, rank_t, ye, x2, g, eg=8):
    b, s, e = rank_t.shape
    cap, d = ye.shape[2], ye.shape[3]
    ts = SCATTER_TOKENS
    return pl.pallas_call(
        functools.partial(_scatter_body, eg=eg),
        grid_spec=pltpu.PrefetchScalarGridSpec(
            num_scalar_prefetch=1,
            grid=(b, s // ts),
            in_specs=[pl.BlockSpec((1, ts, e), lambda bi, ti, bnd: (bi, ti, 0)),
                      pl.BlockSpec((1, e, cap, d), lambda bi, ti, bnd: (bi, 0, 0, 0)),
                      pl.BlockSpec((1, ts, d), lambda bi, ti, bnd: (bi, ti, 0)),
                      pl.BlockSpec((1, d), lambda bi, ti, bnd: (0, 0))],
            out_specs=pl.BlockSpec((1, ts, d), lambda bi, ti, bnd: (bi, ti, 0)),
            scratch_shapes=[pltpu.VMEM((ts, d), F32)],
        ),
        out_shape=jax.ShapeDtypeStruct((b, s, d), F32),
        compiler_params=_cparams(("arbitrary", "arbitrary")),
        name="scatter",
    )(bounds.reshape(-1), rank_t, ye, x2, g)


def kernel(x, mem, ln_mix_g, w_in, hy_conv_w, hy_conv_b, filt_w1, filt_b1, filt_freq1, filt_w2, filt_b2,
           filt_freq2, filt_w3, hy_skip, w_up_hy, w_up_attn, w_gate, b_gate, w_out, ln_x_g, ln_mem_g,
           w_q_x, w_kv_mem, w_o_x, ln_moe_g, w_router, w_e_gate, w_e_up, w_e_down, ln_f_g):
    b, s, d = x.shape
    n = b * s
    x2d = x.reshape(n, d)

    half = HEAD_DIM // 2
    inv = ROPE_THETA ** (-jnp.arange(0, HEAD_DIM, 2, dtype=F32) / HEAD_DIM)
    ang = jnp.arange(s, dtype=F32)[:, None] * inv[None, :]
    lane = np.arange(LANES)
    cos_t = jnp.cos(ang)[:, lane % half]
    sin_t = jnp.sin(ang)[:, lane % half] * jnp.asarray(np.where(lane % HEAD_DIM < half, -1.0, 1.0), F32)[None, :]

    p_hy, *qkv, gates = _proj(x2d, ln_mix_g[None], w_in.astype(BF16), w_gate.astype(BF16), b_gate[None],
                              cos_t, sin_t, b, s)

    t_col = jnp.linspace(0.0, 1.0, s, dtype=F32)[:, None]
    grid_col = 2.0 * math.pi * jnp.arange(s, dtype=F32)[:, None] / s
    bands = jnp.linspace(1e-4, HY_BANDS - 1, HY_BANDS, dtype=F32)[None, :]
    feat = _filt(t_col, grid_col, bands, filt_w1, filt_b1, filt_freq1, filt_w2, filt_b2, filt_freq2)
    delta = jnp.linspace(math.log(HY_TARGET) / HY_SLOW_PCT, math.log(HY_TARGET) / HY_FAST_PCT,
                         HY_WIDTH, dtype=F32)[None, :]
    tables = _fft_tables(s)
    khat = _hyfilt(t_col, delta, feat, filt_w3, tables[1], tables[3])
    z = _hyena(p_hy, hy_conv_w, hy_conv_b[None], hy_skip, khat, tables)

    outs, lses = [], []
    for g, (window, dil) in enumerate(DIL_PAIRS):
        o_g, l_g = _attn_group(*qkv[3 * g:3 * g + 3], g, window // (2 * dil))
        outs.append(o_g)
        lses.append(l_g)

    kv = _memkv(mem, ln_mem_g[None], w_kv_mem.astype(BF16))
    wr_pad = jnp.pad(w_router, ((0, 0), (0, LANES - N_EXPERTS)))
    wr_cat = jnp.concatenate(_split(wr_pad), axis=1)
    x2, hm, aff, aff_hi, aff_lo = _merge(x2d, z.reshape(n, HY_WIDTH), outs, lses, gates,
                         w_up_hy.astype(BF16), w_up_attn.astype(BF16), w_out.astype(BF16),
                         ln_x_g[None], w_q_x.astype(BF16), kv, w_o_x.astype(BF16), ln_moe_g[None],
                         wr_cat, s)

    cap = max(1, EC_FACTOR * s // N_EXPERTS)
    aff3 = aff.reshape(b, s, LANES)
    rank, bounds = _topk(aff3[:, :, :N_EXPERTS].transpose(0, 2, 1), cap, MOE_BOUND_STEP)
    xe, gs = _gather(bounds, rank, hm.reshape(b, s, d), aff_hi.reshape(b, s, LANES), aff_lo.reshape(b, s, LANES), cap)
    ye = _ffn(xe, gs, w_e_gate, w_e_up, w_e_down)
    return _scatter(bounds, rank.transpose(0, 2, 1), ye, x2.reshape(b, s, d), ln_f_g[None])
```

```python
import functools
import math

import numpy as np
import jax
import jax.numpy as jnp
from jax import lax
from jax.experimental import pallas as pl
from jax.experimental.pallas import tpu as pltpu

F32 = jnp.float32
BF16 = jnp.bfloat16
HIGHEST = lax.Precision.HIGHEST

EPS = 1e-6
HY_WIDTH = 768
HY_BANDS = 16
HY_FFN = 64
HY_FAST_PCT = 0.3
HY_SLOW_PCT = 1.5
HY_TARGET = 1e-2
HEAD_DIM = 64
HEADS_PER_GROUP = 4
DIL_PAIRS = ((128, 1), (512, 4), (2048, 16))
N_GROUPS = len(DIL_PAIRS)
ATTN_WIDTH = N_GROUPS * HEADS_PER_GROUP * HEAD_DIM
ATTN_OUT = HEADS_PER_GROUP * HEAD_DIM
ROPE_THETA = 10000.0
X_HEADS = 4
X_HEAD_DIM = 128
N_EXPERTS = 16
EC_FACTOR = 2

LANES = 128
SUBLANES = 8
MXU_COLS = 256
VMEM_LIMIT = 56 * 1024 * 1024
VMEM_LIMIT_LARGE = 60 * 1024 * 1024

FFT_N2 = 128
HY_CB = 128
ATTN_BLOCKS = 8
FFT_UNROLL = 8
FFT_PAD = 8
MOE_BOUND_STEP = 256
GATHER_TOKENS, GATHER_WINDOW = 256, 64
SCATTER_TOKENS, SCATTER_WINDOW = 512, 128
BF16_ROWS = 16


def _cparams(sem, vmem=VMEM_LIMIT):
    return pltpu.CompilerParams(dimension_semantics=sem, vmem_limit_bytes=vmem)


def _rms(x, g):
    return x * lax.rsqrt(jnp.mean(x * x, axis=-1, keepdims=True) + EPS) * g


def _dot(a, b):
    return jnp.dot(a, b, preferred_element_type=F32)


def _dot_hi(a, b):
    return jnp.dot(a, b, precision=HIGHEST, preferred_element_type=F32)


def _split(a):
    hi = a.astype(BF16)
    return hi, (a - hi.astype(F32)).astype(BF16)


def _dot_split(a, b):
    a_hi, a_lo = _split(a)
    b_hi, b_lo = _split(b)
    return _dot(a_hi, b_hi) + _dot(a_lo, b_hi) + _dot(a_hi, b_lo)


def _dot_t(a, b):
    return lax.dot_general(a, b, (((1,), (1,)), ((), ())), preferred_element_type=F32)


def _proj_body(x_ref, g_ref, w_ref, wg_ref, bg_ref, cos_ref, sin_ref, *rest, hyc, aw, gr):
    ng = N_GROUPS
    perm_refs = (None,) + rest[:ng - 1]
    phy_ref = rest[ng - 1]
    qkv_refs = rest[ng:ng + 3 * ng]
    gate_ref = rest[ng + 3 * ng]
    tm = x_ref.shape[0]
    sls = [slice(r0, r0 + gr) for r0 in range(0, tm, gr)]
    lane = lax.broadcasted_iota(jnp.int32, (gr, LANES), 1)
    first = (lane % HEAD_DIM) < (HEAD_DIM // 2)
    gl = ATTN_OUT // LANES

    def rope(t, sl, scale):
        cos, sin = cos_ref[sl, :], sin_ref[sl, :]
        chunks = []
        for j in range(aw // LANES):
            tj = t[:, j * LANES:(j + 1) * LANES]
            partner = jnp.where(first, pltpu.roll(tj, LANES - HEAD_DIM // 2, 1),
                                pltpu.roll(tj, HEAD_DIM // 2, 1))
            chunks.append(((tj * cos + partner * sin) * scale).astype(BF16))
        return chunks

    hs = [_rms(x_ref[sl, :], g_ref[...]).astype(BF16) for sl in sls]
    phys = [_dot(h, w_ref[:, :hyc]) for h in hs]
    qf = [_dot(h, w_ref[:, hyc:hyc + aw]) for h in hs]
    kf = [_dot(h, w_ref[:, hyc + aw:hyc + 2 * aw]) for h in hs]
    vf = [_dot(h, w_ref[:, hyc + 2 * aw:hyc + 3 * aw]).astype(BF16) for h in hs]
    gf = [_dot(h, wg_ref[...]) for h in hs]
    for sl, phy in zip(sls, phys):
        for j in range(hyc // HY_CB):
            phy_ref[0, j, sl, :] = phy[:, j * HY_CB:(j + 1) * HY_CB]
    qs = [rope(t, sl, HEAD_DIM ** -0.5) for t, sl in zip(qf, sls)]
    ks = [rope(t, sl, 1.0) for t, sl in zip(kf, sls)]
    vs = [[t[:, j * LANES:(j + 1) * LANES] for j in range(aw // LANES)] for t in vf]
    for p, (q, k, v) in enumerate(zip(qs, ks, vs)):
        for g in range(ng):
            for i, t in enumerate((q, k, v)):
                out_ref = qkv_refs[3 * g + i]
                tg = jnp.concatenate(t[g * gl:(g + 1) * gl], axis=1)
                if perm_refs[g] is not None:
                    tg = _dot(perm_refs[g][...], tg).astype(BF16)
                dil = out_ref.shape[1]
                rows = gr // dil
                for r in range(dil):
                    out_ref[0, r, p * rows:(p + 1) * rows, :] = tg[r * rows:(r + 1) * rows, :]
    for sl, t in zip(sls, gf):
        gate_ref[sl, :] = jax.nn.sigmoid(t + bg_ref[...]).astype(BF16)


def _proj(x2d, g, w_in, w_gate, b_gate, cos_t, sin_t, batch, seq, tm=512, gr=256):
    n, d = x2d.shape
    hyc = 3 * HY_WIDTH
    aw = ATTN_WIDTH
    gd = w_gate.shape[1]
    assert w_in.shape[1] == hyc + 3 * aw
    nseq = seq // tm
    row = lambda i: (i, 0)
    const = lambda i: (0, 0)
    perms = []
    for _, dil in DIL_PAIRS[1:]:
        rows = gr // dil
        src = (np.arange(gr) % rows) * dil + np.arange(gr) // rows
        perms.append(jnp.asarray(np.eye(gr)[src], dtype=BF16))
    qkv_specs, qkv_shapes = [], []
    for _, dil in DIL_PAIRS:
        for _ in range(3):
            qkv_specs.append(pl.BlockSpec((1, dil, tm // dil, ATTN_OUT), lambda i: (i // nseq, 0, i % nseq, 0)))
            qkv_shapes.append(jax.ShapeDtypeStruct((batch, dil, seq // dil, ATTN_OUT), BF16))
    return pl.pallas_call(
        functools.partial(_proj_body, hyc=hyc, aw=aw, gr=gr),
        grid=(n // tm,),
        in_specs=[
            pl.BlockSpec((tm, d), row),
            pl.BlockSpec((1, d), const),
            pl.BlockSpec(w_in.shape, const, pipeline_mode=pl.Buffered(1)),
            pl.BlockSpec(w_gate.shape, const, pipeline_mode=pl.Buffered(1)),
            pl.BlockSpec((1, gd), const),
            pl.BlockSpec((tm, LANES), lambda i: (i % nseq, 0)),
            pl.BlockSpec((tm, LANES), lambda i: (i % nseq, 0)),
        ] + [pl.BlockSpec((gr, gr), const) for _ in perms],
        out_specs=([pl.BlockSpec((1, hyc // HY_CB, tm, HY_CB), lambda i: (i // nseq, 0, i % nseq, 0))]
                   + qkv_specs + [pl.BlockSpec((tm, gd), row)]),
        out_shape=([jax.ShapeDtypeStruct((batch, hyc // HY_CB, seq, HY_CB), F32)] + qkv_shapes
                   + [jax.ShapeDtypeStruct((n, gd), BF16)]),
        compiler_params=_cparams(("arbitrary",)),
        name="proj",
    )(x2d, g, w_in, w_gate, b_gate, cos_t, sin_t, *perms)


def _filt_body(t_ref, grid_ref, bands_ref, w1t_ref, w1c_ref, w1s_ref, b1_ref, f1_ref,
               w2_ref, b2_ref, f2_ref, feat_ref):
    ang = bands_ref[...] * grid_ref[...]
    pre = (t_ref[...] * w1t_ref[...] + _dot_hi(jnp.cos(ang), w1c_ref[...])
           + _dot_hi(-jnp.sin(ang), w1s_ref[...]))
    h = jnp.sin(f1_ref[...] * (pre + b1_ref[...]))
    feat_ref[...] = jnp.sin(f2_ref[...] * (_dot_hi(h, w2_ref[...]) + b2_ref[...]))


def _filt(t_col, grid_col, bands, w1, b1, f1, w2, b2, f2):
    length = t_col.shape[0]
    nb = bands.shape[1]
    args = (t_col, grid_col, bands, w1[0:1], w1[1:1 + nb], w1[1 + nb:], b1[None], f1[None],
            w2, b2[None], f2[None])
    return pl.pallas_call(
        _filt_body,
        out_shape=jax.ShapeDtypeStruct((length, HY_FFN), F32),
        compiler_params=_cparams(None),
        name="filt",
    )(*args)


def _fft_tables(length):
    n = 2 * length
    n2 = FFT_N2
    n1 = n // n2
    h1 = n1 // 2
    k1 = np.arange(n1)[:, None]
    g_sig = np.zeros((n2, 2 * n1, 2 * h1))
    g_flt = np.zeros((n2, 2 * n1, 2 * h1))
    g_inv = np.zeros((n2, 2 * h1, 2 * n1))
    j = np.arange(h1)[None, :]
    for r in range(n2):
        th = 2 * np.pi * (k1 * (n2 * j + r) % n) / n
        gr, gi = np.cos(th), -np.sin(th)
        g_sig[r, 0::2, :h1] = gr
        g_sig[r, 0::2, h1:] = -gi
        g_sig[r, 1::2, :h1] = gi
        g_sig[r, 1::2, h1:] = gr
        g_flt[r, 0::2, :h1] = gr
        g_flt[r, 1::2, :h1] = gi
        m = n2 * (j + 1) - r
        thb = 2 * np.pi * (k1 * m % n) / n
        live = (m < length).astype(np.float64)
        g_flt[r, 0::2, h1:] = np.cos(thb) * live
        g_flt[r, 1::2, h1:] = np.sin(thb) * live
        wr, wi = (np.cos(th) / n).T, (np.sin(th) / n).T
        g_inv[r, :h1, 0::2] = wr
        g_inv[r, :h1, 1::2] = -wi
        g_inv[r, h1:, 0::2] = wi
        g_inv[r, h1:, 1::2] = wr
    a = np.arange(n2)
    th2 = 2 * np.pi * (np.outer(a, a) % n2) / n2
    c2, s2 = np.cos(th2), np.sin(th2)
    f_fwd = np.block([[c2, s2], [-s2, c2]])
    f_inv = np.block([[c2, -s2], [s2, c2]])
    cast = lambda z: jnp.asarray(z, dtype=F32).astype(BF16)
    return cast(g_sig), cast(g_flt), cast(g_inv), cast(f_fwd), cast(f_inv)


def _fft_stage1(src_a_ref, src_b_ref, tab_ref, work_ref, n1, is_filter):
    n2 = FFT_N2
    grp = SUBLANES

    def body(i, c):
        r0 = pl.multiple_of(i * grp, grp)
        b0 = pl.multiple_of(n2 - grp - r0, grp) if is_filter else r0
        a = jnp.swapaxes(src_a_ref[:, pl.ds(r0, grp), :], 0, 1)
        b = jnp.swapaxes(src_b_ref[:, pl.ds(b0, grp), :], 0, 1)
        outs = []
        for s in range(grp):
            rhs = jnp.concatenate([a[s], b[grp - 1 - s] if is_filter else b[s]], axis=0).astype(BF16)
            outs.append(_dot(tab_ref[r0 + s], rhs))
        work_ref[:, pl.ds(r0, grp), :] = jnp.swapaxes(jnp.stack(outs, axis=0), 0, 1)
        return c

    lax.fori_loop(0, n2 // grp, body, 0, unroll=2)


def _hyfilt_body(t_ref, ts_ref, delta_ref, feat_ref, feats_ref, w3f_ref, w3b_ref, gflt_ref, ffwd_ref, khat_ref,
                 hf_ref, hb_ref, work_ref, *, length, rc):
    n2 = FFT_N2
    n1 = 2 * length // n2
    tiles = rc // n2
    cb = hf_ref.shape[2]

    def gen(i, c):
        r0 = pl.multiple_of(i * rc, rc)
        j0 = pl.multiple_of(i * tiles, tiles)
        delta = jnp.abs(delta_ref[...])
        hf = _dot_split(feat_ref[pl.ds(r0, rc), :], w3f_ref[...]) * jnp.exp(-t_ref[pl.ds(r0, rc), :] * delta)
        hb = _dot_split(feats_ref[pl.ds(r0, rc), :], w3b_ref[...]) * jnp.exp(-ts_ref[pl.ds(r0, rc), :] * delta)
        hf_ref[pl.ds(j0, tiles), :n2, :] = hf.reshape(tiles, n2, cb)
        hb_ref[pl.ds(j0, tiles), :n2, :] = hb.reshape(tiles, n2, cb)
        return c

    lax.fori_loop(0, length // rc, gen, 0)
    _fft_stage1(hf_ref, hb_ref, gflt_ref, work_ref, n1, True)

    side = MXU_COLS // cb

    def stage2(i, c):
        k1s = [i * side + v for v in range(side)]
        blks = [work_ref[pl.ds(pl.multiple_of(2 * k1, 2), 2), :n2, :].reshape(2 * n2, cb).astype(BF16)
                for k1 in k1s]
        x = _dot(ffwd_ref[...], jnp.concatenate(blks, axis=1)).astype(BF16)
        for v, k1 in enumerate(k1s):
            khat_ref[0, 0, pl.ds(pl.multiple_of(k1 * (2 * n2), 2 * n2), 2 * n2), :] = x[:, v * cb:(v + 1) * cb]
        return c

    lax.fori_loop(0, n1 // side, stage2, 0, unroll=FFT_UNROLL // side)


def _hyfilt(t_col, delta, feat, w3, g_flt, f_fwd, rc=512):
    s = t_col.shape[0]
    cb = HY_CB
    ncb = HY_WIDTH // cb
    n2 = FFT_N2
    n1 = 2 * s // n2
    norder = w3.shape[1] // (2 * HY_WIDTH)
    one = pl.Buffered(1)

    def full(a):
        nd = a.ndim
        return pl.BlockSpec(a.shape, lambda o, c: (0,) * nd, pipeline_mode=one)

    shift = lambda a: jnp.concatenate([a[1:], jnp.zeros_like(a[:1])], axis=0)
    return pl.pallas_call(
        functools.partial(_hyfilt_body, length=s, rc=rc),
        grid=(norder, ncb),
        in_specs=[full(t_col), full(t_col),
                  pl.BlockSpec((1, cb), lambda o, c: (0, c)),
                  full(feat), full(feat),
                  pl.BlockSpec((HY_FFN, cb), lambda o, c: (0, 2 * ncb * o + c)),
                  pl.BlockSpec((HY_FFN, cb), lambda o, c: (0, 2 * ncb * o + ncb + c)),
                  full(g_flt), full(f_fwd)],
        out_specs=pl.BlockSpec((1, 1, 2 * n1 * n2, cb), lambda o, c: (o, c, 0, 0)),
        out_shape=jax.ShapeDtypeStruct((norder, ncb, 2 * n1 * n2, cb), BF16),
        scratch_shapes=[pltpu.VMEM((n1 // 2, n2 + FFT_PAD, cb), F32)] * 2
        + [pltpu.VMEM((2 * n1, n2 + FFT_PAD, cb), F32)],
        compiler_params=_cparams(("arbitrary", "arbitrary")),
        name="hyfilt",
    )(t_col, shift(t_col), delta, feat, shift(feat), w3, w3, g_flt, f_fwd)


def _hyena_body(pz_ref, phy_hbm, cwz_ref, cwg1_ref, cwg2_ref, cbz_ref, cbg1_ref, cbg2_ref,
                skip_ref, khat_hbm, gsig_ref, ginv_ref, ffwd_ref, finv_ref,
                out_ref,
                za_ref, zb_ref, ya_ref, yb_ref, work_ref, pg1_ref, pg2_ref, khat_ref, sem, ksem, *, length, rc):
    ncb = pl.num_programs(0)
    pair = pl.ds(2 * pl.program_id(1), 2)

    def gate_copy(order, dst_ref):
        src = phy_hbm.at[pair, pl.ds((order + 1) * ncb + pl.program_id(0), 1)]
        return pltpu.make_async_copy(src, dst_ref, sem.at[order])

    gate_copies = [gate_copy(0, pg1_ref), gate_copy(1, pg2_ref)]
    for cp in gate_copies:
        cp.start()

    new_block = pl.program_id(1) == 0
    spec_copies = [pltpu.make_async_copy(khat_hbm.at[o, pl.program_id(0)], khat_ref.at[o], ksem.at[o])
                   for o in range(khat_ref.shape[0])]

    @pl.when(new_block)
    def _():
        for cp in spec_copies:
            cp.start()
    n2 = FFT_N2
    n1 = 2 * length // n2
    h1 = n1 // 2
    tiles = rc // n2
    cb = za_ref.shape[2]
    side = MXU_COLS // cb
    sub = SUBLANES

    def conv3(p_ref, b, r0, w_ref, bias_ref):
        p = p_ref[b, 0, pl.ds(r0, rc), :]
        before = p_ref[b, 0, pl.ds(pl.multiple_of(jnp.maximum(r0 - sub, 0), sub), sub), :][sub - 1:sub, :]
        after = p_ref[b, 0, pl.ds(pl.multiple_of(jnp.minimum(r0 + rc, length - sub), sub), sub), :][0:1, :]
        before = jnp.where(r0 == 0, 0.0, before)
        after = jnp.where(r0 + rc == length, 0.0, after)
        row = lax.broadcasted_iota(jnp.int32, p.shape, 0)
        prev = jnp.where(row == 0, before, pltpu.roll(p, 1, 0))
        nxt = jnp.where(row == rc - 1, after, pltpu.roll(p, rc - 1, 0))
        return prev * w_ref[0:1, :] + p * w_ref[1:2, :] + nxt * w_ref[2:3, :] + bias_ref[...]

    def stage2(o):
        def body(i, c):
            k1s = [i * FFT_UNROLL + u for u in range(FFT_UNROLL)]
            blks = [work_ref[pl.ds(pl.multiple_of(2 * k1, 2), 2), :n2, :].reshape(2 * n2, cb).astype(BF16)
                    for k1 in k1s]
            groups = range(0, FFT_UNROLL, side)
            xs = [_dot(ffwd_ref[...], jnp.concatenate(blks[u:u + side], axis=1)) for u in groups]

            def filtered(u, x):
                ys = []
                for v in range(side):
                    k1 = k1s[u + v]
                    kh = khat_ref[o, pl.ds(pl.multiple_of(k1 * (2 * n2), 2 * n2), 2 * n2), :].astype(F32)
                    xr, xi = x[:n2, v * cb:(v + 1) * cb], x[n2:, v * cb:(v + 1) * cb]
                    kr, ki = kh[:n2], kh[n2:]
                    ys.append(jnp.concatenate([xr * kr - xi * ki, xr * ki + xi * kr], axis=0).astype(BF16))
                return jnp.concatenate(ys, axis=1)

            ys = [filtered(u, x) for u, x in zip(groups, xs)]
            outs = []
            for y in ys:
                out = _dot(finv_ref[...], y)
                outs.extend(out[:, v * cb:(v + 1) * cb] for v in range(side))
            for k1, out in zip(k1s, outs):
                work_ref[pl.ds(pl.multiple_of(2 * k1, 2), 2), :n2, :] = out.reshape(2, n2, cb)
            return c
        lax.fori_loop(0, n1 // FFT_UNROLL, body, 0)

    def stage3():
        grp = SUBLANES

        def body(i, c):
            r0 = pl.multiple_of(i * grp, grp)
            wk = jnp.swapaxes(work_ref[:, pl.ds(r0, grp), :], 0, 1)
            res = jnp.stack([_dot(ginv_ref[r0 + s], wk[s].astype(BF16)) for s in range(grp)], axis=0)
            ya_ref[:, pl.ds(r0, grp), :] = jnp.swapaxes(res[:, :h1], 0, 1)
            yb_ref[:, pl.ds(r0, grp), :] = jnp.swapaxes(res[:, h1:], 0, 1)
            return c
        lax.fori_loop(0, n2 // grp, body, 0, unroll=2)

    def rows3(ref, i):
        return ref.at[pl.ds(pl.multiple_of(i * tiles, tiles), tiles), :n2, :]

    def load_z(i, c):
        r0 = pl.multiple_of(i * rc, rc)
        rows3(za_ref, i)[...] = conv3(pz_ref, 0, r0, cwz_ref, cbz_ref).reshape(tiles, n2, cb)
        rows3(zb_ref, i)[...] = conv3(pz_ref, 1, r0, cwz_ref, cbz_ref).reshape(tiles, n2, cb)
        return c

    lax.fori_loop(0, length // rc, load_z, 0)
    for o, (pg_ref, cw_ref, cb_ref) in enumerate(((pg1_ref, cwg1_ref, cbg1_ref), (pg2_ref, cwg2_ref, cbg2_ref))):
        _fft_stage1(za_ref, zb_ref, gsig_ref, work_ref, n1, False)

        @pl.when(new_block)
        def _():
            spec_copies[o].wait()

        stage2(o)
        stage3()
        skip = skip_ref[o:o + 1, :]
        gate_copies[o].wait()

        def gate(i, c):
            r0 = pl.multiple_of(i * rc, rc)
            rows = pl.ds(r0, rc)
            ya, yb = (rows3(r, i)[...].reshape(rc, cb) for r in (ya_ref, yb_ref))
            za, zb = (rows3(r, i)[...].reshape(rc, cb) for r in (za_ref, zb_ref))
            new_a = conv3(pg_ref, 0, r0, cw_ref, cb_ref) * (ya + skip * za)
            new_b = conv3(pg_ref, 1, r0, cw_ref, cb_ref) * (yb + skip * zb)
            if o == 0:
                rows3(za_ref, i)[...] = new_a.reshape(tiles, n2, cb)
                rows3(zb_ref, i)[...] = new_b.reshape(tiles, n2, cb)
            else:
                out_ref[0, rows, :] = new_a.astype(out_ref.dtype)
                out_ref[1, rows, :] = new_b.astype(out_ref.dtype)
            return c

        lax.fori_loop(0, length // rc, gate, 0)


def _hyena(p_hy, conv_w, conv_b, skip, khat, tables, rc=512):
    b, _, s, _ = p_hy.shape
    w = HY_WIDTH
    cb = HY_CB
    ncb = w // cb
    n2 = FFT_N2
    n1 = 2 * s // n2
    g_sig, _, g_inv, f_fwd, f_inv = tables
    one = pl.Buffered(1)

    def cwspec(off):
        return pl.BlockSpec((3, cb), lambda c, p: (0, off + c))

    def cbspec(off):
        return pl.BlockSpec((1, cb), lambda c, p: (0, off + c))

    def full(a):
        nd = a.ndim
        return pl.BlockSpec(a.shape, lambda c, p: (0,) * nd, pipeline_mode=one)

    in_specs = [
        pl.BlockSpec((2, 1, s, cb), lambda c, p: (p, c, 0, 0)),
        pl.BlockSpec(memory_space=pl.ANY),
        cwspec(0), cwspec(ncb), cwspec(2 * ncb),
        cbspec(0), cbspec(ncb), cbspec(2 * ncb),
        pl.BlockSpec((2, cb), lambda c, p: (0, c)),
        pl.BlockSpec(memory_space=pl.ANY),
        full(g_sig), full(g_inv), full(f_fwd), full(f_inv),
    ]
    return pl.pallas_call(
        functools.partial(_hyena_body, length=s, rc=rc),
        grid=(ncb, b // 2),
        in_specs=in_specs,
        out_specs=pl.BlockSpec((2, s, cb), lambda c, p: (p, 0, c)),
        out_shape=jax.ShapeDtypeStruct((b, s, w), BF16),
        scratch_shapes=[pltpu.VMEM((n1 // 2, n2 + FFT_PAD, cb), F32)] * 4
        + [pltpu.VMEM((2 * n1, n2 + FFT_PAD, cb), F32)]
        + [pltpu.VMEM((2, 1, s, cb), F32)] * 2 + [pltpu.VMEM((khat.shape[0], 2 * n1 * n2, cb), BF16)]
        + [pltpu.SemaphoreType.DMA((2,)), pltpu.SemaphoreType.DMA((khat.shape[0],))],
        compiler_params=_cparams(("arbitrary", "arbitrary")),
        name="hyena",
    )(p_hy, p_hy, conv_w, conv_w, conv_w, conv_b, conv_b, conv_b,
      skip, khat, g_sig, g_inv, f_fwd, f_inv)


def _attn_body(q_ref, k_ref, v_ref, o_ref, lse_ref, *, n_side, qb):
    lr = q_ref.shape[2]
    kw = qb + 2 * n_side
    nh = HEADS_PER_GROUP
    lane_head = lax.broadcasted_iota(jnp.int32, (qb, ATTN_OUT), 1) // HEAD_DIM
    qi = lax.broadcasted_iota(jnp.int32, (nh * qb, kw), 0) % qb
    ki = lax.broadcasted_iota(jnp.int32, (nh * qb, kw), 1)
    band = [jnp.where(jnp.abs(ki - qi - shift) <= n_side, 0.0, -1e30) for shift in (0, n_side, 2 * n_side)]

    def scores(cls, i):
        q0 = pl.multiple_of(i * qb, qb)
        w0 = pl.multiple_of(jnp.clip(q0 - n_side, 0, lr - kw), n_side)
        q = q_ref[0, cls, pl.ds(q0, qb), :]
        shift = q0 - w0
        bias = jnp.where(shift == n_side, band[1], jnp.where(shift == 0, band[0], band[2]))
        qs = jnp.concatenate([jnp.where(lane_head == h, q, jnp.zeros_like(q)) for h in range(nh)], axis=0)
        return q0, w0, _dot_t(qs, k_ref[0, cls, pl.ds(w0, kw), :]) + bias

    def softmax(s):
        m = jnp.max(s, axis=-1, keepdims=True)
        p = jnp.exp(s - m)
        l = jnp.sum(p, axis=-1, keepdims=True)
        return p.astype(BF16), l, m + jnp.log(l)

    def store(cls, q0, pv, lse_rows):
        o = jnp.zeros((qb, ATTN_OUT), F32)
        lse = jnp.zeros((qb, ATTN_OUT), F32)
        for h in range(nh):
            mine = lane_head == h
            o = jnp.where(mine, pv[h * qb:(h + 1) * qb], o)
            lse = jnp.where(mine, lse_rows[h * qb:(h + 1) * qb], lse)
        o_ref[0, pl.ds(q0, qb), cls * ATTN_OUT:(cls + 1) * ATTN_OUT] = o
        lse_ref[0, pl.ds(q0, qb), cls * ATTN_OUT:(cls + 1) * ATTN_OUT] = lse

    per_trip = min(ATTN_BLOCKS, lr // qb)

    def body(cls, j, c):
        sc = [scores(cls, per_trip * j + t) for t in range(per_trip)]
        sm = [softmax(s) for _, _, s in sc]
        pvs = [_dot(p, v_ref[0, cls, pl.ds(w0, kw), :]) / l for (_, w0, _), (p, l, _) in zip(sc, sm)]
        for (q0, _, _), (_, _, lse_rows), pv in zip(sc, sm, pvs):
            store(cls, q0, pv, lse_rows)
        return c

    for cls in range(q_ref.shape[1]):
        lax.fori_loop(0, lr // (qb * per_trip), functools.partial(body, cls), 0)


def _attn_group(q, k, v, g, n_side, qb=128, max_classes=8):
    b, dil, lr, _ = q.shape
    cg = min(dil, max_classes)
    in_spec = pl.BlockSpec((1, cg, lr, ATTN_OUT), lambda bi, r: (bi, r, 0, 0))
    out_spec = pl.BlockSpec((1, lr, cg * ATTN_OUT), lambda bi, r: (bi, 0, r))
    o, lse = pl.pallas_call(
        functools.partial(_attn_body, n_side=n_side, qb=qb),
        grid=(b, dil // cg),
        in_specs=[in_spec, in_spec, in_spec],
        out_specs=[out_spec, out_spec],
        out_shape=[jax.ShapeDtypeStruct((b, lr, dil * ATTN_OUT), F32)] * 2,
        compiler_params=_cparams(("arbitrary", "arbitrary")),
        name=f"attn{g}",
    )(q, k, v)
    return o, lse


def _memkv_body(mem_ref, g_ref, w_ref, kv_ref):
    mn = _rms(mem_ref[0], g_ref[...]).astype(BF16)
    kv_ref[0] = _dot(mn, w_ref[...]).astype(BF16)


def _memkv(mem, g, w_kv):
    b, m, d = mem.shape
    n = w_kv.shape[1]
    return pl.pallas_call(
        _memkv_body,
        grid=(b,),
        in_specs=[pl.BlockSpec((1, m, d), lambda i: (i, 0, 0)),
                  pl.BlockSpec((1, d), lambda i: (0, 0)),
                  pl.BlockSpec((d, n), lambda i: (0, 0))],
        out_specs=pl.BlockSpec((1, m, n), lambda i: (i, 0, 0)),
        out_shape=jax.ShapeDtypeStruct((b, m, n), BF16),
        compiler_params=_cparams(("arbitrary",)),
        name="memkv",
    )(mem, g, w_kv)


def _merge_staged_body(x_ref, z_ref, o0_ref, o1_ref, o2_ref, l0_ref, l1_ref, l2_ref, gate_ref,
                       wuh_ref, wua_ref, wout_ref, gx_ref, wq_ref, kv_ref, wo_ref, gm_ref, wr_ref,
                       p1_ref, p2_ref, x2_ref, hm_ref, aff_ref, affh_ref, affl_ref, *, nsplit):
    d = x_ref.shape[1]
    rows = x_ref.shape[0] // nsplit
    parts = range(nsplit)
    sls = [slice(p * rows, (p + 1) * rows) for p in parts]

    def natural(ref, perm_ref, part):
        dil = ref.shape[2] // ATTN_OUT
        rr = rows // dil
        blk = ref[0, part * rr:(part + 1) * rr, :]
        if dil == 1:
            return blk
        pieces = [blk[:, r * ATTN_OUT:(r + 1) * ATTN_OUT] for r in range(dil)]
        if rr % SUBLANES == 0 and dil % SUBLANES == 0:
            return jnp.swapaxes(jnp.stack(pieces, axis=0), 0, 1).reshape(rows, ATTN_OUT)
        hi, lo = _split(jnp.concatenate(pieces, axis=0))
        return _dot(perm_ref[...], hi) + _dot(perm_ref[...], lo)

    def combine(part):
        o0, o1, o2 = (natural(r, p, part) for r, p in ((o0_ref, None), (o1_ref, p1_ref), (o2_ref, p2_ref)))
        l0, l1, l2 = (natural(r, p, part) for r, p in ((l0_ref, None), (l1_ref, p1_ref), (l2_ref, p2_ref)))
        mx = jnp.maximum(jnp.maximum(l0, l1), l2)
        e0, e1, e2 = jnp.exp(l0 - mx), jnp.exp(l1 - mx), jnp.exp(l2 - mx)
        return ((e0 * o0 + e1 * o1 + e2 * o2) / (e0 + e1 + e2)).astype(BF16)

    attn = [combine(p) for p in parts]
    y_hy = [_dot(z_ref[sl, :].astype(BF16), wuh_ref[...]) for sl in sls]
    y_at = [_dot(a, wua_ref[...]) for a in attn]

    def mixed(sl, yh, ya):
        gates = gate_ref[sl, :].astype(F32)
        return (gates[:, :d] * yh + gates[:, d:] * ya).astype(BF16)

    mix = [mixed(sl, yh, ya) for sl, yh, ya in zip(sls, y_hy, y_at)]
    x1 = [x_ref[sl, :] + _dot(m, wout_ref[...]) for sl, m in zip(sls, mix)]

    hx = [_rms(v, gx_ref[...]).astype(BF16) for v in x1]
    qx = [_dot(h, wq_ref[...]).astype(BF16) for h in hx]
    kv = kv_ref[0]
    xw = X_HEADS * X_HEAD_DIM

    def cross_attention(q):
        heads = []
        for h in range(X_HEADS):
            hc = slice(h * X_HEAD_DIM, (h + 1) * X_HEAD_DIM)
            s = _dot_t(q[:, hc], kv[:, hc]) * (X_HEAD_DIM ** -0.5)
            m = jnp.max(s, axis=-1, keepdims=True)
            p = jnp.exp(s - m)
            p = p / jnp.sum(p, axis=-1, keepdims=True)
            heads.append(_dot(p.astype(BF16), kv[:, xw + h * X_HEAD_DIM:xw + (h + 1) * X_HEAD_DIM]))
        return jnp.concatenate(heads, axis=-1).astype(BF16)

    ox = [cross_attention(q) for q in qx]
    x2 = [v + _dot(o, wo_ref[...]) for v, o in zip(x1, ox)]
    for sl, v in zip(sls, x2):
        x2_ref[sl, :] = v

    hm = [_split(_rms(v, gm_ref[...])) for v in x2]
    cross = [_dot(jnp.concatenate([hi, lo], axis=0), wr_ref[...]) for hi, lo in hm]
    for sl, (hi, _), c in zip(sls, hm, cross):
        hm_ref[sl, :] = hi
        logits = (c[:rows, :LANES] + c[:rows, LANES:]) + (c[rows:, :LANES] + c[rows:, LANES:])
        lane = lax.broadcasted_iota(jnp.int32, logits.shape, 1)
        logits = jnp.where(lane < N_EXPERTS, logits, -1e30)
        m = jnp.max(logits, axis=-1, keepdims=True)
        p = jnp.exp(logits - m)
        aff = p / jnp.sum(p, axis=-1, keepdims=True)
        aff_ref[sl, :] = aff
        affh_ref[sl, :], affl_ref[sl, :] = _split(aff)


def _merge(x2d, z2d, outs, lses, gates, wuh, wua, wout, gx, wq, kv, wo, gm, wr_cat, seq, tm=1024, nsplit=4):
    n, d = x2d.shape
    nseq = seq // tm
    row = lambda i: (i, 0)
    const = lambda i: (0, 0)

    def rspec(a):
        return pl.BlockSpec((tm, a.shape[1]), row)

    def aspec(a):
        dil = a.shape[2] // ATTN_OUT
        return pl.BlockSpec((1, tm // dil, a.shape[2]), lambda i: (i // nseq, i % nseq, 0))

    def cspec(a):
        return pl.BlockSpec(a.shape, const)

    rows = tm // nsplit
    perms = []
    for a in outs[1:]:
        dil = a.shape[2] // ATTN_OUT
        src = (np.arange(rows) % dil) * (rows // dil) + np.arange(rows) // dil
        perms.append(jnp.asarray(np.eye(rows)[src], dtype=BF16))

    in_arrays = [x2d, z2d, *outs, *lses, gates, wuh, wua, wout, gx, wq, kv, wo, gm, wr_cat, *perms]
    in_specs = ([rspec(a) for a in in_arrays[:2]] + [aspec(a) for a in in_arrays[2:8]] + [rspec(gates)]
                + [cspec(a) for a in in_arrays[9:14]]
                + [pl.BlockSpec((1,) + kv.shape[1:], lambda i: (i // nseq, 0, 0))]
                + [cspec(a) for a in in_arrays[15:]])
    return pl.pallas_call(
        functools.partial(_merge_staged_body, nsplit=nsplit),
        grid=(n // tm,),
        in_specs=in_specs,
        out_specs=[pl.BlockSpec((tm, d), row), pl.BlockSpec((tm, d), row)] + [pl.BlockSpec((tm, LANES), row)] * 3,
        out_shape=[jax.ShapeDtypeStruct((n, d), F32), jax.ShapeDtypeStruct((n, d), BF16),
                   jax.ShapeDtypeStruct((n, LANES), F32), jax.ShapeDtypeStruct((n, LANES), BF16),
                   jax.ShapeDtypeStruct((n, LANES), BF16)],
        compiler_params=_cparams(("arbitrary",), vmem=VMEM_LIMIT_LARGE),
        name="merge",
    )(*in_arrays)


def _topk_body(aff_ref, tri_ref, rank_ref, bounds_ref, *, cap, tok_block):
    a = aff_ref[0]
    e, s = a.shape

    def count(mask):
        return jnp.sum(jnp.where(mask, 1.0, 0.0), axis=-1, keepdims=True)

    def as_float(bits):
        return pltpu.bitcast(jnp.broadcast_to(bits, (e, LANES)), F32)[:, 0:1]

    def search(i, thr):
        cand = thr | (jnp.int32(1) << (30 - i))
        return jnp.where(count(a >= as_float(cand)) >= cap, cand, thr)

    thr = as_float(lax.fori_loop(0, 31, search, jnp.zeros((e, 1), jnp.int32)))
    gt = a > thr
    eq = a == thr
    need = cap - count(gt)

    def prefix_excl(mask):
        mf = jnp.where(mask, 1.0, 0.0)
        parts = []
        carry = jnp.zeros((e, 1), F32)
        for c in range(s // LANES):
            blk = mf[:, c * LANES:(c + 1) * LANES]
            inc = _dot(blk.astype(BF16), tri_ref[...])
            parts.append(inc - blk + carry)
            carry = carry + inc[:, LANES - 1:LANES]
        return jnp.concatenate(parts, axis=-1)

    sel = gt | (eq & (prefix_excl(eq) < need))
    excl = prefix_excl(sel)
    rank_ref[0] = jnp.where(sel, excl, -1.0)

    tok = lax.broadcasted_iota(jnp.int32, (e, s), 1)
    lane = lax.broadcasted_iota(jnp.int32, (e, LANES), 1)
    bounds = jnp.zeros((e, LANES), F32)
    for j in range(s // tok_block + 1):
        bounds = jnp.where(lane == j, count(sel & (tok < j * tok_block)), bounds)
    bounds_ref[0] = bounds.astype(jnp.int32)


def _topk(aff_t, cap, tok_block):
    b, e, s = aff_t.shape
    assert s // tok_block + 1 <= LANES
    tri = jnp.asarray(np.triu(np.ones((LANES, LANES))), dtype=BF16)
    return pl.pallas_call(
        functools.partial(_topk_body, cap=cap, tok_block=tok_block),
        grid=(b,),
        in_specs=[pl.BlockSpec((1, e, s), lambda i: (i, 0, 0)),
                  pl.BlockSpec((LANES, LANES), lambda i: (0, 0))],
        out_specs=[pl.BlockSpec((1, e, s), lambda i: (i, 0, 0)),
                   pl.BlockSpec((1, e, LANES), lambda i: (i, 0, 0))],
        out_shape=[jax.ShapeDtypeStruct((b, e, s), F32),
                   jax.ShapeDtypeStruct((b, e, LANES), jnp.int32)],
        compiler_params=_cparams(("arbitrary",)),
        name="topk",
    )(aff_t, tri)


def _moe_windows(bounds_ref, base, ne, step, tokens, window):
    per = tokens // MOE_BOUND_STEP
    starts = []
    passes = jnp.int32(0)
    for e in range(ne):
        r_lo = bounds_ref[base + e * LANES + step * per]
        r_hi = bounds_ref[base + e * LANES + (step + 1) * per]
        ws = (r_lo // BF16_ROWS) * BF16_ROWS
        starts.append(ws)
        passes = jnp.maximum(passes, (r_hi - ws + window - 1) // window)
    return starts, passes


def _gather_body(bounds_ref, rank_ref, hm_ref, affh_ref, affl_ref, xe_ref, gs_ref, *, cap, eg):
    w = GATHER_WINDOW
    ne, tc = rank_ref.shape[1], rank_ref.shape[2]
    bi = pl.program_id(0)
    ci = pl.program_id(1)

    @pl.when(ci == 0)
    def _():
        xe_ref[...] = jnp.zeros_like(xe_ref)
        gs_ref[...] = jnp.zeros_like(gs_ref)

    starts, passes = _moe_windows(bounds_ref, bi * ne * LANES, ne, ci, tc, w)
    iota_w = lax.broadcasted_iota(jnp.int32, (w, 1), 0)

    def selector(p, g0):
        lo = [starts[e] + p * w for e in range(g0, g0 + eg)]
        phys = [pl.multiple_of(jnp.minimum(v, cap - w), BF16_ROWS) for v in lo]
        rk = jnp.concatenate([jnp.broadcast_to(rank_ref[0, e:e + 1, :], (w, tc))
                              for e in range(g0, g0 + eg)], axis=0)
        slot = jnp.concatenate([iota_w + v for v in phys], axis=0).astype(F32)
        lom = jnp.concatenate([jnp.zeros_like(iota_w) + v for v in lo], axis=0).astype(F32)
        return phys, jnp.where((rk == slot) & (slot >= lom), 1.0, 0.0).astype(BF16)

    def one_pass(p, carry):
        groups = range(0, ne, eg)
        sels = [selector(p, g0) for g0 in groups]
        res = [_dot(onehot, hm_ref[0]) for _, onehot in sels]
        resg = [_dot(onehot, affh_ref[0]) + _dot(onehot, affl_ref[0]) for _, onehot in sels]
        for g0, (phys, _), r, rg in zip(groups, sels, res, resg):
            for k, e in enumerate(range(g0, g0 + eg)):
                rows = pl.ds(phys[k], w)
                xe_ref[0, e, rows, :] = (xe_ref[0, e, rows, :].astype(F32) + r[k * w:(k + 1) * w]).astype(BF16)
                gs_ref[0, e, rows, :] = gs_ref[0, e, rows, :] + rg[k * w:(k + 1) * w]
        return carry

    lax.fori_loop(0, passes, one_pass, 0)


def _gather(bounds, rank, hm, aff_hi, aff_lo, cap, eg=16):
    b, e, s = rank.shape
    d = hm.shape[2]
    tc = GATHER_TOKENS
    return pl.pallas_call(
        functools.partial(_gather_body, cap=cap, eg=eg),
        grid_spec=pltpu.PrefetchScalarGridSpec(
            num_scalar_prefetch=1,
            grid=(b, s // tc),
            in_specs=[pl.BlockSpec((1, e, tc), lambda bi, ci, bnd: (bi, 0, ci)),
                      pl.BlockSpec((1, tc, d), lambda bi, ci, bnd: (bi, ci, 0)),
                      pl.BlockSpec((1, tc, LANES), lambda bi, ci, bnd: (bi, ci, 0)),
                      pl.BlockSpec((1, tc, LANES), lambda bi, ci, bnd: (bi, ci, 0))],
            out_specs=[pl.BlockSpec((1, e, cap, d), lambda bi, ci, bnd: (bi, 0, 0, 0)),
                       pl.BlockSpec((1, e, cap, LANES), lambda bi, ci, bnd: (bi, 0, 0, 0))],
        ),
        out_shape=[jax.ShapeDtypeStruct((b, e, cap, d), BF16),
                   jax.ShapeDtypeStruct((b, e, cap, LANES), F32)],
        compiler_params=_cparams(("arbitrary", "arbitrary")),
        name="gather",
    )(bounds.reshape(-1), rank, hm, aff_hi, aff_lo)


def _ffn_body(xe_ref, gs_ref, wg_ref, wu_ref, wd_ref, ye_ref, acc_ref, *, mb):
    f = pl.program_id(1)
    nb, _, cap, d = xe_ref.shape
    wg = wg_ref[0].astype(BF16)
    wu = wu_ref[0].astype(BF16)
    wd = wd_ref[0].astype(BF16)

    @pl.when(f == 0)
    def _():
        acc_ref[...] = jnp.zeros_like(acc_ref)

    for b0 in range(0, nb, mb):
        xe = xe_ref[b0:b0 + mb, 0].reshape(mb * cap, d)
        a = _dot(xe, wg)
        u = _dot(xe, wu)
        hsw = (a * jax.nn.sigmoid(a) * u).astype(BF16)
        acc_ref[b0:b0 + mb] += _dot(hsw, wd).reshape(mb, cap, d)

    @pl.when(f == pl.num_programs(1) - 1)
    def _():
        lane = lax.broadcasted_iota(jnp.int32, gs_ref.shape[2:], 1)
        mine = lane == pl.program_id(0)
        for b in range(nb):
            gate = jnp.sum(jnp.where(mine, gs_ref[b, 0], 0.0), axis=-1, keepdims=True)
            ye_ref[b, 0] = (acc_ref[b] * gate).astype(BF16)


def _ffn(xe, gs, wg, wu, wd, ft=1024, mb=1):
    b, e, cap, d = xe.shape
    ff = wg.shape[2]
    return pl.pallas_call(
        functools.partial(_ffn_body, mb=mb),
        grid=(e, ff // ft),
        in_specs=[pl.BlockSpec((b, 1, cap, d), lambda ei, fi: (0, ei, 0, 0)),
                  pl.BlockSpec((b, 1, cap, LANES), lambda ei, fi: (0, ei, 0, 0)),
                  pl.BlockSpec((1, d, ft), lambda ei, fi: (ei, 0, fi)),
                  pl.BlockSpec((1, d, ft), lambda ei, fi: (ei, 0, fi)),
                  pl.BlockSpec((1, ft, d), lambda ei, fi: (ei, fi, 0))],
        out_specs=pl.BlockSpec((b, 1, cap, d), lambda ei, fi: (0, ei, 0, 0)),
        out_shape=jax.ShapeDtypeStruct((b, e, cap, d), BF16),
        scratch_shapes=[pltpu.VMEM((b, cap, d), F32)],
        compiler_params=_cparams(("arbitrary", "arbitrary"), vmem=VMEM_LIMIT_LARGE),
        name="ffn",
    )(xe, gs, wg, wu, wd)


def _scatter_body(bounds_ref, rank_ref, ye_ref, x2_ref, g_ref, out_ref, acc_ref, *, eg):
    w = SCATTER_WINDOW
    ts = x2_ref.shape[1]
    ne, cap = ye_ref.shape[1], ye_ref.shape[2]
    starts, passes = _moe_windows(bounds_ref, pl.program_id(0) * ne * LANES, ne, pl.program_id(1), ts, w)
    rank = rank_ref[0]
    acc_ref[...] = x2_ref[0]
    iota_w = lax.broadcasted_iota(jnp.int32, (1, w), 1)

    def operands(p, g0):
        onehots, rows = [], []
        for e in range(g0, g0 + eg):
            lo = starts[e] + p * w
            phys = pl.multiple_of(jnp.minimum(lo, cap - w), BF16_ROWS)
            slot = (iota_w + phys).astype(F32)
            hit = (rank[:, e:e + 1] == slot) & (slot >= lo.astype(F32))
            onehots.append(jnp.where(hit, 1.0, 0.0).astype(BF16))
            rows.append(ye_ref[0, e, pl.ds(phys, w), :])
        return jnp.concatenate(onehots, axis=1), jnp.concatenate(rows, axis=0)

    def one_pass(p, carry):
        ops = [operands(p, g0) for g0 in range(0, ne, eg)]
        total = _dot(*ops[0])
        for lhs, rhs in ops[1:]:
            total = total + _dot(lhs, rhs)
        acc_ref[...] += total
        return carry

    lax.fori_loop(0, passes, one_pass, 0)
    out_ref[0] = _rms(acc_ref[...], g_ref[...])


def _scatter(bounds, rank_t, ye, x2, g, eg=16):
    b, s, e = rank_t.shape
    cap, d = ye.shape[2], ye.shape[3]
    ts = SCATTER_TOKENS
    return pl.pallas_call(
        functools.partial(_scatter_body, eg=eg),
        grid_spec=pltpu.PrefetchScalarGridSpec(
            num_scalar_prefetch=1,
            grid=(b, s // ts),
            in_specs=[pl.BlockSpec((1, ts, e), lambda bi, ti, bnd: (bi, ti, 0)),
                      pl.BlockSpec((1, e, cap, d), lambda bi, ti, bnd: (bi, 0, 0, 0)),
                      pl.BlockSpec((1, ts, d), lambda bi, ti, bnd: (bi, ti, 0)),
                      pl.BlockSpec((1, d), lambda bi, ti, bnd: (0, 0))],
            out_specs=pl.BlockSpec((1, ts, d), lambda bi, ti, bnd: (bi, ti, 0)),
            scratch_shapes=[pltpu.VMEM((ts, d), F32)],
        ),
        out_shape=jax.ShapeDtypeStruct((b, s, d), F32),
        compiler_params=_cparams(("arbitrary", "arbitrary")),
        name="scatter",
    )(bounds.reshape(-1), rank_t, ye, x2, g)


def kernel(x, mem, ln_mix_g, w_in, hy_conv_w, hy_conv_b, filt_w1, filt_b1, filt_freq1, filt_w2, filt_b2,
           filt_freq2, filt_w3, hy_skip, w_up_hy, w_up_attn, w_gate, b_gate, w_out, ln_x_g, ln_mem_g,
           w_q_x, w_kv_mem, w_o_x, ln_moe_g, w_router, w_e_gate, w_e_up, w_e_down, ln_f_g):
    b, s, d = x.shape
    n = b * s
    x2d = x.reshape(n, d)

    half = HEAD_DIM // 2
    inv = ROPE_THETA ** (-jnp.arange(0, HEAD_DIM, 2, dtype=F32) / HEAD_DIM)
    ang = jnp.arange(s, dtype=F32)[:, None] * inv[None, :]
    lane = np.arange(LANES)
    cos_t = jnp.cos(ang)[:, lane % half]
    sin_t = jnp.sin(ang)[:, lane % half] * jnp.asarray(np.where(lane % HEAD_DIM < half, -1.0, 1.0), F32)[None, :]

    p_hy, *qkv, gates = _proj(x2d, ln_mix_g[None], w_in.astype(BF16), w_gate.astype(BF16), b_gate[None],
                              cos_t, sin_t, b, s)

    t_col = jnp.linspace(0.0, 1.0, s, dtype=F32)[:, None]
    grid_col = 2.0 * math.pi * jnp.arange(s, dtype=F32)[:, None] / s
    bands = jnp.linspace(1e-4, HY_BANDS - 1, HY_BANDS, dtype=F32)[None, :]
    feat = _filt(t_col, grid_col, bands, filt_w1, filt_b1, filt_freq1, filt_w2, filt_b2, filt_freq2)
    delta = jnp.linspace(math.log(HY_TARGET) / HY_SLOW_PCT, math.log(HY_TARGET) / HY_FAST_PCT,
                         HY_WIDTH, dtype=F32)[None, :]
    tables = _fft_tables(s)
    khat = _hyfilt(t_col, delta, feat, filt_w3, tables[1], tables[3])
    z = _hyena(p_hy, hy_conv_w, hy_conv_b[None], hy_skip, khat, tables)

    outs, lses = [], []
    for g, (window, dil) in enumerate(DIL_PAIRS):
        o_g, l_g = _attn_group(*qkv[3 * g:3 * g + 3], g, window // (2 * dil))
        outs.append(o_g)
        lses.append(l_g)

    kv = _memkv(mem, ln_mem_g[None], w_kv_mem.astype(BF16))
    wr_pad = jnp.pad(w_router, ((0, 0), (0, LANES - N_EXPERTS)))
    wr_cat = jnp.concatenate(_split(wr_pad), axis=1)
    x2, hm, aff, aff_hi, aff_lo = _merge(x2d, z.reshape(n, HY_WIDTH), outs, lses, gates,
                         w_up_hy.astype(BF16), w_up_attn.astype(BF16), w_out.astype(BF16),
                         ln_x_g[None], w_q_x.astype(BF16), kv, w_o_x.astype(BF16), ln_moe_g[None],
                         wr_cat, s)

    cap = max(1, EC_FACTOR * s // N_EXPERTS)
    aff3 = aff.reshape(b, s, LANES)
    rank, bounds = _topk(aff3[:, :, :N_EXPERTS].transpose(0, 2, 1), cap, MOE_BOUND_STEP)
    xe, gs = _gather(bounds, rank, hm.reshape(b, s, d), aff_hi.reshape(b, s, LANES), aff_lo.reshape(b, s, LANES), cap)
    ye = _ffn(xe, gs, w_e_gate, w_e_up, w_e_down)
    return _scatter(bounds, rank.transpose(0, 2, 1), ye, x2.reshape(b, s, d), ln_f_g[None])
```

```python
import functools
import math

import numpy as np
import jax
import jax.numpy as jnp
from jax import lax
from jax.experimental import pallas as pl
from jax.experimental.pallas import tpu as pltpu

F32 = jnp.float32
BF16 = jnp.bfloat16
HIGHEST = lax.Precision.HIGHEST

EPS = 1e-6
HY_WIDTH = 768
HY_BANDS = 16
HY_FFN = 64
HY_FAST_PCT = 0.3
HY_SLOW_PCT = 1.5
HY_TARGET = 1e-2
HEAD_DIM = 64
HEADS_PER_GROUP = 4
DIL_PAIRS = ((128, 1), (512, 4), (2048, 16))
N_GROUPS = len(DIL_PAIRS)
ATTN_WIDTH = N_GROUPS * HEADS_PER_GROUP * HEAD_DIM
ATTN_OUT = HEADS_PER_GROUP * HEAD_DIM
ROPE_THETA = 10000.0
X_HEADS = 4
X_HEAD_DIM = 128
N_EXPERTS = 16
EC_FACTOR = 2

LANES = 128
SUBLANES = 8
MXU_COLS = 256
VMEM_LIMIT = 56 * 1024 * 1024
VMEM_LIMIT_LARGE = 60 * 1024 * 1024

FFT_N2 = 128
HY_CB = 128
ATTN_BLOCKS = 8
FFT_UNROLL = 8
FFT_PAD = 8
MOE_BOUND_STEP = 256
GATHER_TOKENS, GATHER_WINDOW = 256, 64
SCATTER_TOKENS, SCATTER_WINDOW = 512, 128
BF16_ROWS = 16


def _cparams(sem, vmem=VMEM_LIMIT):
    return pltpu.CompilerParams(dimension_semantics=sem, vmem_limit_bytes=vmem)


def _rms(x, g):
    return x * lax.rsqrt(jnp.mean(x * x, axis=-1, keepdims=True) + EPS) * g


def _dot(a, b):
    return jnp.dot(a, b, preferred_element_type=F32)


def _dot_hi(a, b):
    return jnp.dot(a, b, precision=HIGHEST, preferred_element_type=F32)


def _split(a):
    hi = a.astype(BF16)
    return hi, (a - hi.astype(F32)).astype(BF16)


def _dot_split(a, b):
    a_hi, a_lo = _split(a)
    b_hi, b_lo = _split(b)
    return _dot(a_hi, b_hi) + _dot(a_lo, b_hi) + _dot(a_hi, b_lo)


def _dot_t(a, b):
    return lax.dot_general(a, b, (((1,), (1,)), ((), ())), preferred_element_type=F32)


def _proj_body(x_ref, g_ref, w_ref, wg_ref, bg_ref, cos_ref, sin_ref, *rest, hyc, aw, gr):
    ng = N_GROUPS
    perm_refs = (None,) + rest[:ng - 1]
    phy_ref = rest[ng - 1]
    qkv_refs = rest[ng:ng + 3 * ng]
    gate_ref = rest[ng + 3 * ng]
    tm = x_ref.shape[0]
    sls = [slice(r0, r0 + gr) for r0 in range(0, tm, gr)]
    lane = lax.broadcasted_iota(jnp.int32, (gr, LANES), 1)
    first = (lane % HEAD_DIM) < (HEAD_DIM // 2)
    gl = ATTN_OUT // LANES

    def rope(t, sl, scale):
        cos, sin = cos_ref[sl, :], sin_ref[sl, :]
        chunks = []
        for j in range(aw // LANES):
            tj = t[:, j * LANES:(j + 1) * LANES]
            partner = jnp.where(first, pltpu.roll(tj, LANES - HEAD_DIM // 2, 1),
                                pltpu.roll(tj, HEAD_DIM // 2, 1))
            chunks.append(((tj * cos + partner * sin) * scale).astype(BF16))
        return chunks

    hs = [_rms(x_ref[sl, :], g_ref[...]).astype(BF16) for sl in sls]
    phys = [_dot(h, w_ref[:, :hyc]) for h in hs]
    qf = [_dot(h, w_ref[:, hyc:hyc + aw]) for h in hs]
    kf = [_dot(h, w_ref[:, hyc + aw:hyc + 2 * aw]) for h in hs]
    vf = [_dot(h, w_ref[:, hyc + 2 * aw:hyc + 3 * aw]).astype(BF16) for h in hs]
    gf = [_dot(h, wg_ref[...]) for h in hs]
    for sl, phy in zip(sls, phys):
        for j in range(hyc // HY_CB):
            phy_ref[0, j, sl, :] = phy[:, j * HY_CB:(j + 1) * HY_CB]
    qs = [rope(t, sl, HEAD_DIM ** -0.5) for t, sl in zip(qf, sls)]
    ks = [rope(t, sl, 1.0) for t, sl in zip(kf, sls)]
    vs = [[t[:, j * LANES:(j + 1) * LANES] for j in range(aw // LANES)] for t in vf]
    for p, (q, k, v) in enumerate(zip(qs, ks, vs)):
        for g in range(ng):
            for i, t in enumerate((q, k, v)):
                out_ref = qkv_refs[3 * g + i]
                tg = jnp.concatenate(t[g * gl:(g + 1) * gl], axis=1)
                if perm_refs[g] is not None:
                    tg = _dot(perm_refs[g][...], tg).astype(BF16)
                dil = out_ref.shape[1]
                rows = gr // dil
                for r in range(dil):
                    out_ref[0, r, p * rows:(p + 1) * rows, :] = tg[r * rows:(r + 1) * rows, :]
    for sl, t in zip(sls, gf):
        gate_ref[sl, :] = jax.nn.sigmoid(t + bg_ref[...]).astype(BF16)


def _proj(x2d, g, w_in, w_gate, b_gate, cos_t, sin_t, batch, seq, tm=512, gr=256):
    n, d = x2d.shape
    hyc = 3 * HY_WIDTH
    aw = ATTN_WIDTH
    gd = w_gate.shape[1]
    assert w_in.shape[1] == hyc + 3 * aw
    nseq = seq // tm
    row = lambda i: (i, 0)
    const = lambda i: (0, 0)
    perms = []
    for _, dil in DIL_PAIRS[1:]:
        rows = gr // dil
        src = (np.arange(gr) % rows) * dil + np.arange(gr) // rows
        perms.append(jnp.asarray(np.eye(gr)[src], dtype=BF16))
    qkv_specs, qkv_shapes = [], []
    for _, dil in DIL_PAIRS:
        for _ in range(3):
            qkv_specs.append(pl.BlockSpec((1, dil, tm // dil, ATTN_OUT), lambda i: (i // nseq, 0, i % nseq, 0)))
            qkv_shapes.append(jax.ShapeDtypeStruct((batch, dil, seq // dil, ATTN_OUT), BF16))
    return pl.pallas_call(
        functools.partial(_proj_body, hyc=hyc, aw=aw, gr=gr),
        grid=(n // tm,),
        in_specs=[
            pl.BlockSpec((tm, d), row),
            pl.BlockSpec((1, d), const),
            pl.BlockSpec(w_in.shape, const, pipeline_mode=pl.Buffered(1)),
            pl.BlockSpec(w_gate.shape, const, pipeline_mode=pl.Buffered(1)),
            pl.BlockSpec((1, gd), const),
            pl.BlockSpec((tm, LANES), lambda i: (i % nseq, 0)),
            pl.BlockSpec((tm, LANES), lambda i: (i % nseq, 0)),
        ] + [pl.BlockSpec((gr, gr), const) for _ in perms],
        out_specs=([pl.BlockSpec((1, hyc // HY_CB, tm, HY_CB), lambda i: (i // nseq, 0, i % nseq, 0))]
                   + qkv_specs + [pl.BlockSpec((tm, gd), row)]),
        out_shape=([jax.ShapeDtypeStruct((batch, hyc // HY_CB, seq, HY_CB), F32)] + qkv_shapes
                   + [jax.ShapeDtypeStruct((n, gd), BF16)]),
        compiler_params=_cparams(("arbitrary",)),
        name="proj",
    )(x2d, g, w_in, w_gate, b_gate, cos_t, sin_t, *perms)


def _filt_body(t_ref, grid_ref, bands_ref, w1t_ref, w1c_ref, w1s_ref, b1_ref, f1_ref,
               w2_ref, b2_ref, f2_ref, feat_ref):
    ang = bands_ref[...] * grid_ref[...]
    pre = (t_ref[...] * w1t_ref[...] + _dot_hi(jnp.cos(ang), w1c_ref[...])
           + _dot_hi(-jnp.sin(ang), w1s_ref[...]))
    h = jnp.sin(f1_ref[...] * (pre + b1_ref[...]))
    feat_ref[...] = jnp.sin(f2_ref[...] * (_dot_hi(h, w2_ref[...]) + b2_ref[...]))


def _filt(t_col, grid_col, bands, w1, b1, f1, w2, b2, f2):
    length = t_col.shape[0]
    nb = bands.shape[1]
    args = (t_col, grid_col, bands, w1[0:1], w1[1:1 + nb], w1[1 + nb:], b1[None], f1[None],
            w2, b2[None], f2[None])
    return pl.pallas_call(
        _filt_body,
        out_shape=jax.ShapeDtypeStruct((length, HY_FFN), F32),
        compiler_params=_cparams(None),
        name="filt",
    )(*args)


def _fft_tables(length):
    n = 2 * length
    n2 = FFT_N2
    n1 = n // n2
    h1 = n1 // 2
    k1 = np.arange(n1)[:, None]
    g_sig = np.zeros((n2, 2 * n1, 2 * h1))
    g_flt = np.zeros((n2, 2 * n1, 2 * h1))
    g_inv = np.zeros((n2, 2 * h1, 2 * n1))
    j = np.arange(h1)[None, :]
    for r in range(n2):
        th = 2 * np.pi * (k1 * (n2 * j + r) % n) / n
        gr, gi = np.cos(th), -np.sin(th)
        g_sig[r, 0::2, :h1] = gr
        g_sig[r, 0::2, h1:] = -gi
        g_sig[r, 1::2, :h1] = gi
        g_sig[r, 1::2, h1:] = gr
        g_flt[r, 0::2, :h1] = gr
        g_flt[r, 1::2, :h1] = gi
        m = n2 * (j + 1) - r
        thb = 2 * np.pi * (k1 * m % n) / n
        live = (m < length).astype(np.float64)
        g_flt[r, 0::2, h1:] = np.cos(thb) * live
        g_flt[r, 1::2, h1:] = np.sin(thb) * live
        wr, wi = (np.cos(th) / n).T, (np.sin(th) / n).T
        g_inv[r, :h1, 0::2] = wr
        g_inv[r, :h1, 1::2] = -wi
        g_inv[r, h1:, 0::2] = wi
        g_inv[r, h1:, 1::2] = wr
    a = np.arange(n2)
    th2 = 2 * np.pi * (np.outer(a, a) % n2) / n2
    c2, s2 = np.cos(th2), np.sin(th2)
    f_fwd = np.block([[c2, s2], [-s2, c2]])
    f_inv = np.block([[c2, -s2], [s2, c2]])
    cast = lambda z: jnp.asarray(z, dtype=F32).astype(BF16)
    return cast(g_sig), cast(g_flt), cast(g_inv), cast(f_fwd), cast(f_inv)


def _fft_stage1(src_a_ref, src_b_ref, tab_ref, work_ref, n1, is_filter):
    n2 = FFT_N2
    grp = SUBLANES

    def body(i, c):
        r0 = pl.multiple_of(i * grp, grp)
        b0 = pl.multiple_of(n2 - grp - r0, grp) if is_filter else r0
        a = jnp.swapaxes(src_a_ref[:, pl.ds(r0, grp), :], 0, 1)
        b = jnp.swapaxes(src_b_ref[:, pl.ds(b0, grp), :], 0, 1)
        outs = []
        for s in range(grp):
            rhs = jnp.concatenate([a[s], b[grp - 1 - s] if is_filter else b[s]], axis=0).astype(BF16)
            outs.append(_dot(tab_ref[r0 + s], rhs))
        work_ref[:, pl.ds(r0, grp), :] = jnp.swapaxes(jnp.stack(outs, axis=0), 0, 1)
        return c

    lax.fori_loop(0, n2 // grp, body, 0, unroll=2)


def _hyfilt_body(t_ref, ts_ref, delta_ref, feat_ref, feats_ref, w3f_ref, w3b_ref, gflt_ref, ffwd_ref, khat_ref,
                 hf_ref, hb_ref, work_ref, *, length, rc):
    n2 = FFT_N2
    n1 = 2 * length // n2
    tiles = rc // n2
    cb = hf_ref.shape[2]

    def gen(i, c):
        r0 = pl.multiple_of(i * rc, rc)
        j0 = pl.multiple_of(i * tiles, tiles)
        delta = jnp.abs(delta_ref[...])
        hf = _dot_split(feat_ref[pl.ds(r0, rc), :], w3f_ref[...]) * jnp.exp(-t_ref[pl.ds(r0, rc), :] * delta)
        hb = _dot_split(feats_ref[pl.ds(r0, rc), :], w3b_ref[...]) * jnp.exp(-ts_ref[pl.ds(r0, rc), :] * delta)
        hf_ref[pl.ds(j0, tiles), :n2, :] = hf.reshape(tiles, n2, cb)
        hb_ref[pl.ds(j0, tiles), :n2, :] = hb.reshape(tiles, n2, cb)
        return c

    lax.fori_loop(0, length // rc, gen, 0)
    _fft_stage1(hf_ref, hb_ref, gflt_ref, work_ref, n1, True)

    side = MXU_COLS // cb

    def stage2(i, c):
        k1s = [i * side + v for v in range(side)]
        blks = [work_ref[pl.ds(pl.multiple_of(2 * k1, 2), 2), :n2, :].reshape(2 * n2, cb).astype(BF16)
                for k1 in k1s]
        x = _dot(ffwd_ref[...], jnp.concatenate(blks, axis=1)).astype(BF16)
        for v, k1 in enumerate(k1s):
            khat_ref[0, 0, pl.ds(pl.multiple_of(k1 * (2 * n2), 2 * n2), 2 * n2), :] = x[:, v * cb:(v + 1) * cb]
        return c

    lax.fori_loop(0, n1 // side, stage2, 0, unroll=FFT_UNROLL // side)


def _hyfilt(t_col, delta, feat, w3, g_flt, f_fwd, rc=512):
    s = t_col.shape[0]
    cb = HY_CB
    ncb = HY_WIDTH // cb
    n2 = FFT_N2
    n1 = 2 * s // n2
    norder = w3.shape[1] // (2 * HY_WIDTH)
    one = pl.Buffered(1)

    def full(a):
        nd = a.ndim
        return pl.BlockSpec(a.shape, lambda o, c: (0,) * nd, pipeline_mode=one)

    shift = lambda a: jnp.concatenate([a[1:], jnp.zeros_like(a[:1])], axis=0)
    return pl.pallas_call(
        functools.partial(_hyfilt_body, length=s, rc=rc),
        grid=(norder, ncb),
        in_specs=[full(t_col), full(t_col),
                  pl.BlockSpec((1, cb), lambda o, c: (0, c)),
                  full(feat), full(feat),
                  pl.BlockSpec((HY_FFN, cb), lambda o, c: (0, 2 * ncb * o + c)),
                  pl.BlockSpec((HY_FFN, cb), lambda o, c: (0, 2 * ncb * o + ncb + c)),
                  full(g_flt), full(f_fwd)],
        out_specs=pl.BlockSpec((1, 1, 2 * n1 * n2, cb), lambda o, c: (o, c, 0, 0)),
        out_shape=jax.ShapeDtypeStruct((norder, ncb, 2 * n1 * n2, cb), BF16),
        scratch_shapes=[pltpu.VMEM((n1 // 2, n2 + FFT_PAD, cb), F32)] * 2
        + [pltpu.VMEM((2 * n1, n2 + FFT_PAD, cb), F32)],
        compiler_params=_cparams(("arbitrary", "arbitrary")),
        name="hyfilt",
    )(t_col, shift(t_col), delta, feat, shift(feat), w3, w3, g_flt, f_fwd)


def _hyena_body(pz_ref, phy_hbm, cwz_ref, cwg1_ref, cwg2_ref, cbz_ref, cbg1_ref, cbg2_ref,
                skip_ref, khat_hbm, gsig_ref, ginv_ref, ffwd_ref, finv_ref,
                out_ref,
                za_ref, zb_ref, ya_ref, yb_ref, work_ref, pg1_ref, pg2_ref, khat_ref, sem, ksem, *, length, rc):
    ncb = pl.num_programs(0)
    pair = pl.ds(2 * pl.program_id(1), 2)

    def gate_copy(order, dst_ref):
        src = phy_hbm.at[pair, pl.ds((order + 1) * ncb + pl.program_id(0), 1)]
        return pltpu.make_async_copy(src, dst_ref, sem.at[order])

    gate_copies = [gate_copy(0, pg1_ref), gate_copy(1, pg2_ref)]
    for cp in gate_copies:
        cp.start()

    new_block = pl.program_id(1) == 0
    spec_copies = [pltpu.make_async_copy(khat_hbm.at[o, pl.program_id(0)], khat_ref.at[o], ksem.at[o])
                   for o in range(khat_ref.shape[0])]

    @pl.when(new_block)
    def _():
        for cp in spec_copies:
            cp.start()
    n2 = FFT_N2
    n1 = 2 * length // n2
    h1 = n1 // 2
    tiles = rc // n2
    cb = za_ref.shape[2]
    side = MXU_COLS // cb
    sub = SUBLANES

    def conv3(p_ref, b, r0, w_ref, bias_ref):
        p = p_ref[b, 0, pl.ds(r0, rc), :]
        before = p_ref[b, 0, pl.ds(pl.multiple_of(jnp.maximum(r0 - sub, 0), sub), sub), :][sub - 1:sub, :]
        after = p_ref[b, 0, pl.ds(pl.multiple_of(jnp.minimum(r0 + rc, length - sub), sub), sub), :][0:1, :]
        before = jnp.where(r0 == 0, 0.0, before)
        after = jnp.where(r0 + rc == length, 0.0, after)
        row = lax.broadcasted_iota(jnp.int32, p.shape, 0)
        prev = jnp.where(row == 0, before, pltpu.roll(p, 1, 0))
        nxt = jnp.where(row == rc - 1, after, pltpu.roll(p, rc - 1, 0))
        return prev * w_ref[0:1, :] + p * w_ref[1:2, :] + nxt * w_ref[2:3, :] + bias_ref[...]

    def stage2(o):
        def body(i, c):
            k1s = [i * FFT_UNROLL + u for u in range(FFT_UNROLL)]
            blks = [work_ref[pl.ds(pl.multiple_of(2 * k1, 2), 2), :n2, :].reshape(2 * n2, cb).astype(BF16)
                    for k1 in k1s]
            groups = range(0, FFT_UNROLL, side)
            xs = [_dot(ffwd_ref[...], jnp.concatenate(blks[u:u + side], axis=1)) for u in groups]

            def filtered(u, x):
                ys = []
                for v in range(side):
                    k1 = k1s[u + v]
                    kh = khat_ref[o, pl.ds(pl.multiple_of(k1 * (2 * n2), 2 * n2), 2 * n2), :].astype(F32)
                    xr, xi = x[:n2, v * cb:(v + 1) * cb], x[n2:, v * cb:(v + 1) * cb]
                    kr, ki = kh[:n2], kh[n2:]
                    ys.append(jnp.concatenate([xr * kr - xi * ki, xr * ki + xi * kr], axis=0).astype(BF16))
                return jnp.concatenate(ys, axis=1)

            ys = [filtered(u, x) for u, x in zip(groups, xs)]
            outs = []
            for y in ys:
                out = _dot(finv_ref[...], y)
                outs.extend(out[:, v * cb:(v + 1) * cb] for v in range(side))
            for k1, out in zip(k1s, outs):
                work_ref[pl.ds(pl.multiple_of(2 * k1, 2), 2), :n2, :] = out.reshape(2, n2, cb)
            return c
        lax.fori_loop(0, n1 // FFT_UNROLL, body, 0)

    def stage3():
        grp = SUBLANES

        def body(i, c):
            r0 = pl.multiple_of(i * grp, grp)
            wk = jnp.swapaxes(work_ref[:, pl.ds(r0, grp), :], 0, 1)
            res = jnp.stack([_dot(ginv_ref[r0 + s], wk[s].astype(BF16)) for s in range(grp)], axis=0)
            ya_ref[:, pl.ds(r0, grp), :] = jnp.swapaxes(res[:, :h1], 0, 1)
            yb_ref[:, pl.ds(r0, grp), :] = jnp.swapaxes(res[:, h1:], 0, 1)
            return c
        lax.fori_loop(0, n2 // grp, body, 0, unroll=2)

    def rows3(ref, i):
        return ref.at[pl.ds(pl.multiple_of(i * tiles, tiles), tiles), :n2, :]

    def load_z(i, c):
        r0 = pl.multiple_of(i * rc, rc)
        rows3(za_ref, i)[...] = conv3(pz_ref, 0, r0, cwz_ref, cbz_ref).reshape(tiles, n2, cb)
        rows3(zb_ref, i)[...] = conv3(pz_ref, 1, r0, cwz_ref, cbz_ref).reshape(tiles, n2, cb)
        return c

    lax.fori_loop(0, length // rc, load_z, 0)
    for o, (pg_ref, cw_ref, cb_ref) in enumerate(((pg1_ref, cwg1_ref, cbg1_ref), (pg2_ref, cwg2_ref, cbg2_ref))):
        _fft_stage1(za_ref, zb_ref, gsig_ref, work_ref, n1, False)

        @pl.when(new_block)
        def _():
            spec_copies[o].wait()

        stage2(o)
        stage3()
        skip = skip_ref[o:o + 1, :]
        gate_copies[o].wait()

        def gate(i, c):
            r0 = pl.multiple_of(i * rc, rc)
            rows = pl.ds(r0, rc)
            ya, yb = (rows3(r, i)[...].reshape(rc, cb) for r in (ya_ref, yb_ref))
            za, zb = (rows3(r, i)[...].reshape(rc, cb) for r in (za_ref, zb_ref))
            new_a = conv3(pg_ref, 0, r0, cw_ref, cb_ref) * (ya + skip * za)
            new_b = conv3(pg_ref, 1, r0, cw_ref, cb_ref) * (yb + skip * zb)
            if o == 0:
                rows3(za_ref, i)[...] = new_a.reshape(tiles, n2, cb)
                rows3(zb_ref, i)[...] = new_b.reshape(tiles, n2, cb)
            else:
                out_ref[0, rows, :] = new_a.astype(out_ref.dtype)
                out_ref[1, rows, :] = new_b.astype(out_ref.dtype)
            return c

        lax.fori_loop(0, length // rc, gate, 0)


def _hyena(p_hy, conv_w, conv_b, skip, khat, tables, rc=512):
    b, _, s, _ = p_hy.shape
    w = HY_WIDTH
    cb = HY_CB
    ncb = w // cb
    n2 = FFT_N2
    n1 = 2 * s // n2
    g_sig, _, g_inv, f_fwd, f_inv = tables
    one = pl.Buffered(1)

    def cwspec(off):
        return pl.BlockSpec((3, cb), lambda c, p: (0, off + c))

    def cbspec(off):
        return pl.BlockSpec((1, cb), lambda c, p: (0, off + c))

    def full(a):
        nd = a.ndim
        return pl.BlockSpec(a.shape, lambda c, p: (0,) * nd, pipeline_mode=one)

    in_specs = [
        pl.BlockSpec((2, 1, s, cb), lambda c, p: (p, c, 0, 0)),
        pl.BlockSpec(memory_space=pl.ANY),
        cwspec(0), cwspec(ncb), cwspec(2 * ncb),
        cbspec(0), cbspec(ncb), cbspec(2 * ncb),
        pl.BlockSpec((2, cb), lambda c, p: (0, c)),
        pl.BlockSpec(memory_space=pl.ANY),
        full(g_sig), full(g_inv), full(f_fwd), full(f_inv),
    ]
    return pl.pallas_call(
        functools.partial(_hyena_body, length=s, rc=rc),
        grid=(ncb, b // 2),
        in_specs=in_specs,
        out_specs=pl.BlockSpec((2, s, cb), lambda c, p: (p, 0, c)),
        out_shape=jax.ShapeDtypeStruct((b, s, w), BF16),
        scratch_shapes=[pltpu.VMEM((n1 // 2, n2 + FFT_PAD, cb), F32)] * 4
        + [pltpu.VMEM((2 * n1, n2 + FFT_PAD, cb), F32)]
        + [pltpu.VMEM((2, 1, s, cb), F32)] * 2 + [pltpu.VMEM((khat.shape[0], 2 * n1 * n2, cb), BF16)]
        + [pltpu.SemaphoreType.DMA((2,)), pltpu.SemaphoreType.DMA((khat.shape[0],))],
        compiler_params=_cparams(("arbitrary", "arbitrary")),
        name="hyena",
    )(p_hy, p_hy, conv_w, conv_w, conv_w, conv_b, conv_b, conv_b,
      skip, khat, g_sig, g_inv, f_fwd, f_inv)


def _attn_body(q_ref, k_ref, v_ref, o_ref, lse_ref, *, n_side, qb):
    lr = q_ref.shape[2]
    kw = qb + 2 * n_side
    nh = HEADS_PER_GROUP
    lane_head = lax.broadcasted_iota(jnp.int32, (qb, ATTN_OUT), 1) // HEAD_DIM
    qi = lax.broadcasted_iota(jnp.int32, (nh * qb, kw), 0) % qb
    ki = lax.broadcasted_iota(jnp.int32, (nh * qb, kw), 1)
    band = [jnp.where(jnp.abs(ki - qi - shift) <= n_side, 0.0, -1e30) for shift in (0, n_side, 2 * n_side)]

    def scores(cls, i):
        q0 = pl.multiple_of(i * qb, qb)
        w0 = pl.multiple_of(jnp.clip(q0 - n_side, 0, lr - kw), n_side)
        q = q_ref[0, cls, pl.ds(q0, qb), :]
        shift = q0 - w0
        bias = jnp.where(shift == n_side, band[1], jnp.where(shift == 0, band[0], band[2]))
        qs = jnp.concatenate([jnp.where(lane_head == h, q, jnp.zeros_like(q)) for h in range(nh)], axis=0)
        return q0, w0, _dot_t(qs, k_ref[0, cls, pl.ds(w0, kw), :]) + bias

    def softmax(s):
        m = jnp.max(s, axis=-1, keepdims=True)
        p = jnp.exp(s - m)
        l = jnp.sum(p, axis=-1, keepdims=True)
        return p.astype(BF16), l, m + jnp.log(l)

    def store(cls, q0, pv, lse_rows):
        o = jnp.zeros((qb, ATTN_OUT), F32)
        lse = jnp.zeros((qb, ATTN_OUT), F32)
        for h in range(nh):
            mine = lane_head == h
            o = jnp.where(mine, pv[h * qb:(h + 1) * qb], o)
            lse = jnp.where(mine, lse_rows[h * qb:(h + 1) * qb], lse)
        o_ref[0, pl.ds(q0, qb), cls * ATTN_OUT:(cls + 1) * ATTN_OUT] = o
        lse_ref[0, pl.ds(q0, qb), cls * ATTN_OUT:(cls + 1) * ATTN_OUT] = lse

    per_trip = min(ATTN_BLOCKS, lr // qb)

    def body(cls, j, c):
        sc = [scores(cls, per_trip * j + t) for t in range(per_trip)]
        sm = [softmax(s) for _, _, s in sc]
        pvs = [_dot(p, v_ref[0, cls, pl.ds(w0, kw), :]) / l for (_, w0, _), (p, l, _) in zip(sc, sm)]
        for (q0, _, _), (_, _, lse_rows), pv in zip(sc, sm, pvs):
            store(cls, q0, pv, lse_rows)
        return c

    for cls in range(q_ref.shape[1]):
        lax.fori_loop(0, lr // (qb * per_trip), functools.partial(body, cls), 0)


def _attn_group(q, k, v, g, n_side, qb=128, max_classes=8):
    b, dil, lr, _ = q.shape
    cg = min(dil, max_classes)
    in_spec = pl.BlockSpec((1, cg, lr, ATTN_OUT), lambda bi, r: (bi, r, 0, 0))
    out_spec = pl.BlockSpec((1, lr, cg * ATTN_OUT), lambda bi, r: (bi, 0, r))
    o, lse = pl.pallas_call(
        functools.partial(_attn_body, n_side=n_side, qb=qb),
        grid=(b, dil // cg),
        in_specs=[in_spec, in_spec, in_spec],
        out_specs=[out_spec, out_spec],
        out_shape=[jax.ShapeDtypeStruct((b, lr, dil * ATTN_OUT), F32)] * 2,
        compiler_params=_cparams(("arbitrary", "arbitrary")),
        name=f"attn{g}",
    )(q, k, v)
    return o, lse


def _memkv_body(mem_ref, g_ref, w_ref, kv_ref):
    mn = _rms(mem_ref[0], g_ref[...]).astype(BF16)
    kv_ref[0] = _dot(mn, w_ref[...]).astype(BF16)


def _memkv(mem, g, w_kv):
    b, m, d = mem.shape
    n = w_kv.shape[1]
    return pl.pallas_call(
        _memkv_body,
        grid=(b,),
        in_specs=[pl.BlockSpec((1, m, d), lambda i: (i, 0, 0)),
                  pl.BlockSpec((1, d), lambda i: (0, 0)),
                  pl.BlockSpec((d, n), lambda i: (0, 0))],
        out_specs=pl.BlockSpec((1, m, n), lambda i: (i, 0, 0)),
        out_shape=jax.ShapeDtypeStruct((b, m, n), BF16),
        compiler_params=_cparams(("arbitrary",)),
        name="memkv",
    )(mem, g, w_kv)


def _merge_staged_body(x_ref, z_ref, o0_ref, o1_ref, o2_ref, l0_ref, l1_ref, l2_ref, gate_ref,
                       wuh_ref, wua_ref, wout_ref, gx_ref, wq_ref, kv_ref, wo_ref, gm_ref, wr_ref,
                       p1_ref, p2_ref, x2_ref, hm_ref, aff_ref, affh_ref, affl_ref, *, nsplit):
    d = x_ref.shape[1]
    rows = x_ref.shape[0] // nsplit
    parts = range(nsplit)
    sls = [slice(p * rows, (p + 1) * rows) for p in parts]

    def natural(ref, perm_ref, part):
        dil = ref.shape[2] // ATTN_OUT
        rr = rows // dil
        blk = ref[0, part * rr:(part + 1) * rr, :]
        if dil == 1:
            return blk
        pieces = [blk[:, r * ATTN_OUT:(r + 1) * ATTN_OUT] for r in range(dil)]
        if rr % SUBLANES == 0 and dil % SUBLANES == 0:
            return jnp.swapaxes(jnp.stack(pieces, axis=0), 0, 1).reshape(rows, ATTN_OUT)
        hi, lo = _split(jnp.concatenate(pieces, axis=0))
        return _dot(perm_ref[...], hi) + _dot(perm_ref[...], lo)

    def combine(part):
        o0, o1, o2 = (natural(r, p, part) for r, p in ((o0_ref, None), (o1_ref, p1_ref), (o2_ref, p2_ref)))
        l0, l1, l2 = (natural(r, p, part) for r, p in ((l0_ref, None), (l1_ref, p1_ref), (l2_ref, p2_ref)))
        mx = jnp.maximum(jnp.maximum(l0, l1), l2)
        e0, e1, e2 = jnp.exp(l0 - mx), jnp.exp(l1 - mx), jnp.exp(l2 - mx)
        return ((e0 * o0 + e1 * o1 + e2 * o2) / (e0 + e1 + e2)).astype(BF16)

    attn = [combine(p) for p in parts]
    y_hy = [_dot(z_ref[sl, :].astype(BF16), wuh_ref[...]) for sl in sls]
    y_at = [_dot(a, wua_ref[...]) for a in attn]

    def mixed(sl, yh, ya):
        gates = gate_ref[sl, :].astype(F32)
        return (gates[:, :d] * yh + gates[:, d:] * ya).astype(BF16)

    mix = [mixed(sl, yh, ya) for sl, yh, ya in zip(sls, y_hy, y_at)]
    x1 = [x_ref[sl, :] + _dot(m, wout_ref[...]) for sl, m in zip(sls, mix)]

    hx = [_rms(v, gx_ref[...]).astype(BF16) for v in x1]
    qx = [_dot(h, wq_ref[...]).astype(BF16) for h in hx]
    kv = kv_ref[0]
    xw = X_HEADS * X_HEAD_DIM

    def cross_attention(q):
        heads = []
        for h in range(X_HEADS):
            hc = slice(h * X_HEAD_DIM, (h + 1) * X_HEAD_DIM)
            s = _dot_t(q[:, hc], kv[:, hc]) * (X_HEAD_DIM ** -0.5)
            m = jnp.max(s, axis=-1, keepdims=True)
            p = jnp.exp(s - m)
            p = p / jnp.sum(p, axis=-1, keepdims=True)
            heads.append(_dot(p.astype(BF16), kv[:, xw + h * X_HEAD_DIM:xw + (h + 1) * X_HEAD_DIM]))
        return jnp.concatenate(heads, axis=-1).astype(BF16)

    ox = [cross_attention(q) for q in qx]
    x2 = [v + _dot(o, wo_ref[...]) for v, o in zip(x1, ox)]
    for sl, v in zip(sls, x2):
        x2_ref[sl, :] = v

    hm = [_split(_rms(v, gm_ref[...])) for v in x2]
    cross = [_dot(jnp.concatenate([hi, lo], axis=0), wr_ref[...]) for hi, lo in hm]
    for sl, (hi, _), c in zip(sls, hm, cross):
        hm_ref[sl, :] = hi
        logits = (c[:rows, :LANES] + c[:rows, LANES:]) + (c[rows:, :LANES] + c[rows:, LANES:])
        lane = lax.broadcasted_iota(jnp.int32, logits.shape, 1)
        logits = jnp.where(lane < N_EXPERTS, logits, -1e30)
        m = jnp.max(logits, axis=-1, keepdims=True)
        p = jnp.exp(logits - m)
        aff = p / jnp.sum(p, axis=-1, keepdims=True)
        aff_ref[sl, :] = aff
        affh_ref[sl, :], affl_ref[sl, :] = _split(aff)


def _merge(x2d, z2d, outs, lses, gates, wuh, wua, wout, gx, wq, kv, wo, gm, wr_cat, seq, tm=1024, nsplit=4):
    n, d = x2d.shape
    nseq = seq // tm
    row = lambda i: (i, 0)
    const = lambda i: (0, 0)

    def rspec(a):
        return pl.BlockSpec((tm, a.shape[1]), row)

    def aspec(a):
        dil = a.shape[2] // ATTN_OUT
        return pl.BlockSpec((1, tm // dil, a.shape[2]), lambda i: (i // nseq, i % nseq, 0))

    def cspec(a):
        return pl.BlockSpec(a.shape, const)

    rows = tm // nsplit
    perms = []
    for a in outs[1:]:
        dil = a.shape[2] // ATTN_OUT
        src = (np.arange(rows) % dil) * (rows // dil) + np.arange(rows) // dil
        perms.append(jnp.asarray(np.eye(rows)[src], dtype=BF16))

    in_arrays = [x2d, z2d, *outs, *lses, gates, wuh, wua, wout, gx, wq, kv, wo, gm, wr_cat, *perms]
    in_specs = ([rspec(a) for a in in_arrays[:2]] + [aspec(a) for a in in_arrays[2:8]] + [rspec(gates)]
                + [cspec(a) for a in in_arrays[9:14]]
                + [pl.BlockSpec((1,) + kv.shape[1:], lambda i: (i // nseq, 0, 0))]
                + [cspec(a) for a in in_arrays[15:]])
    return pl.pallas_call(
        functools.partial(_merge_staged_body, nsplit=nsplit),
        grid=(n // tm,),
        in_specs=in_specs,
        out_specs=[pl.BlockSpec((tm, d), row), pl.BlockSpec((tm, d), row)] + [pl.BlockSpec((tm, LANES), row)] * 3,
        out_shape=[jax.ShapeDtypeStruct((n, d), F32), jax.ShapeDtypeStruct((n, d), BF16),
                   jax.ShapeDtypeStruct((n, LANES), F32), jax.ShapeDtypeStruct((n, LANES), BF16),
                   jax.ShapeDtypeStruct((n, LANES), BF16)],
        compiler_params=_cparams(("arbitrary",), vmem=VMEM_LIMIT_LARGE),
        name="merge",
    )(*in_arrays)


def _topk_body(aff_ref, tri_ref, rank_ref, bounds_ref, *, cap, tok_block):
    a = aff_ref[0]
    e, s = a.shape

    def count(mask):
        return jnp.sum(jnp.where(mask, 1.0, 0.0), axis=-1, keepdims=True)

    def as_float(bits):
        return pltpu.bitcast(jnp.broadcast_to(bits, (e, LANES)), F32)[:, 0:1]

    def search(i, thr):
        cand = thr | (jnp.int32(1) << (30 - i))
        return jnp.where(count(a >= as_float(cand)) >= cap, cand, thr)

    thr = as_float(lax.fori_loop(0, 31, search, jnp.zeros((e, 1), jnp.int32)))
    gt = a > thr
    eq = a == thr
    need = cap - count(gt)

    def prefix_excl(mask):
        mf = jnp.where(mask, 1.0, 0.0)
        parts = []
        carry = jnp.zeros((e, 1), F32)
        for c in range(s // LANES):
            blk = mf[:, c * LANES:(c + 1) * LANES]
            inc = _dot(blk.astype(BF16), tri_ref[...])
            parts.append(inc - blk + carry)
            carry = carry + inc[:, LANES - 1:LANES]
        return jnp.concatenate(parts, axis=-1)

    sel = gt | (eq & (prefix_excl(eq) < need))
    excl = prefix_excl(sel)
    rank_ref[0] = jnp.where(sel, excl, -1.0)

    tok = lax.broadcasted_iota(jnp.int32, (e, s), 1)
    lane = lax.broadcasted_iota(jnp.int32, (e, LANES), 1)
    bounds = jnp.zeros((e, LANES), F32)
    for j in range(s // tok_block + 1):
        bounds = jnp.where(lane == j, count(sel & (tok < j * tok_block)), bounds)
    bounds_ref[0] = bounds.astype(jnp.int32)


def _topk(aff_t, cap, tok_block):
    b, e, s = aff_t.shape
    assert s // tok_block + 1 <= LANES
    tri = jnp.asarray(np.triu(np.ones((LANES, LANES))), dtype=BF16)
    return pl.pallas_call(
        functools.partial(_topk_body, cap=cap, tok_block=tok_block),
        grid=(b,),
        in_specs=[pl.BlockSpec((1, e, s), lambda i: (i, 0, 0)),
                  pl.BlockSpec((LANES, LANES), lambda i: (0, 0))],
        out_specs=[pl.BlockSpec((1, e, s), lambda i: (i, 0, 0)),
                   pl.BlockSpec((1, e, LANES), lambda i: (i, 0, 0))],
        out_shape=[jax.ShapeDtypeStruct((b, e, s), F32),
                   jax.ShapeDtypeStruct((b, e, LANES), jnp.int32)],
        compiler_params=_cparams(("arbitrary",)),
        name="topk",
    )(aff_t, tri)


def _moe_windows(bounds_ref, base, ne, step, tokens, window):
    per = tokens // MOE_BOUND_STEP
    starts = []
    passes = jnp.int32(0)
    for e in range(ne):
        r_lo = bounds_ref[base + e * LANES + step * per]
        r_hi = bounds_ref[base + e * LANES + (step + 1) * per]
        ws = (r_lo // BF16_ROWS) * BF16_ROWS
        starts.append(ws)
        passes = jnp.maximum(passes, (r_hi - ws + window - 1) // window)
    return starts, passes


def _gather_body(bounds_ref, rank_ref, hm_ref, affh_ref, affl_ref, xe_ref, gs_ref, *, cap, eg):
    w = GATHER_WINDOW
    ne, tc = rank_ref.shape[1], rank_ref.shape[2]
    bi = pl.program_id(0)
    ci = pl.program_id(1)

    @pl.when(ci == 0)
    def _():
        xe_ref[...] = jnp.zeros_like(xe_ref)
        gs_ref[...] = jnp.zeros_like(gs_ref)

    starts, passes = _moe_windows(bounds_ref, bi * ne * LANES, ne, ci, tc, w)
    iota_w = lax.broadcasted_iota(jnp.int32, (w, 1), 0)

    def selector(p, g0):
        lo = [starts[e] + p * w for e in range(g0, g0 + eg)]
        phys = [pl.multiple_of(jnp.minimum(v, cap - w), BF16_ROWS) for v in lo]
        rk = jnp.concatenate([jnp.broadcast_to(rank_ref[0, e:e + 1, :], (w, tc))
                              for e in range(g0, g0 + eg)], axis=0)
        slot = jnp.concatenate([iota_w + v for v in phys], axis=0).astype(F32)
        lom = jnp.concatenate([jnp.zeros_like(iota_w) + v for v in lo], axis=0).astype(F32)
        return phys, jnp.where((rk == slot) & (slot >= lom), 1.0, 0.0).astype(BF16)

    def one_pass(p, carry):
        groups = range(0, ne, eg)
        sels = [selector(p, g0) for g0 in groups]
        res = [_dot(onehot, hm_ref[0]) for _, onehot in sels]
        resg = [_dot(onehot, affh_ref[0]) + _dot(onehot, affl_ref[0]) for _, onehot in sels]
        for g0, (phys, _), r, rg in zip(groups, sels, res, resg):
            for k, e in enumerate(range(g0, g0 + eg)):
                rows = pl.ds(phys[k], w)
                xe_ref[0, e, rows, :] = (xe_ref[0, e, rows, :].astype(F32) + r[k * w:(k + 1) * w]).astype(BF16)
                gs_ref[0, e, rows, :] = gs_ref[0, e, rows, :] + rg[k * w:(k + 1) * w]
        return carry

    lax.fori_loop(0, passes, one_pass, 0)


def _gather(bounds, rank, hm, aff_hi, aff_lo, cap, eg=16):
    b, e, s = rank.shape
    d = hm.shape[2]
    tc = GATHER_TOKENS
    return pl.pallas_call(
        functools.partial(_gather_body, cap=cap, eg=eg),
        grid_spec=pltpu.PrefetchScalarGridSpec(
            num_scalar_prefetch=1,
            grid=(b, s // tc),
            in_specs=[pl.BlockSpec((1, e, tc), lambda bi, ci, bnd: (bi, 0, ci)),
                      pl.BlockSpec((1, tc, d), lambda bi, ci, bnd: (bi, ci, 0)),
                      pl.BlockSpec((1, tc, LANES), lambda bi, ci, bnd: (bi, ci, 0)),
                      pl.BlockSpec((1, tc, LANES), lambda bi, ci, bnd: (bi, ci, 0))],
            out_specs=[pl.BlockSpec((1, e, cap, d), lambda bi, ci, bnd: (bi, 0, 0, 0)),
                       pl.BlockSpec((1, e, cap, LANES), lambda bi, ci, bnd: (bi, 0, 0, 0))],
        ),
        out_shape=[jax.ShapeDtypeStruct((b, e, cap, d), BF16),
                   jax.ShapeDtypeStruct((b, e, cap, LANES), F32)],
        compiler_params=_cparams(("arbitrary", "arbitrary")),
        name="gather",
    )(bounds.reshape(-1), rank, hm, aff_hi, aff_lo)


def _ffn_body(xe_ref, gs_ref, wg_ref, wu_ref, wd_ref, ye_ref, acc_ref, *, mb):
    f = pl.program_id(1)
    nb, _, cap, d = xe_ref.shape
    wg = wg_ref[0].astype(BF16)
    wu = wu_ref[0].astype(BF16)
    wd = wd_ref[0].astype(BF16)

    @pl.when(f == 0)
    def _():
        acc_ref[...] = jnp.zeros_like(acc_ref)

    for b0 in range(0, nb, mb):
        xe = xe_ref[b0:b0 + mb, 0].reshape(mb * cap, d)
        a = _dot(xe, wg)
        u = _dot(xe, wu)
        hsw = (a * jax.nn.sigmoid(a) * u).astype(BF16)
        acc_ref[b0:b0 + mb] += _dot(hsw, wd).reshape(mb, cap, d)

    @pl.when(f == pl.num_programs(1) - 1)
    def _():
        lane = lax.broadcasted_iota(jnp.int32, gs_ref.shape[2:], 1)
        mine = lane == pl.program_id(0)
        for b in range(nb):
            gate = jnp.sum(jnp.where(mine, gs_ref[b, 0], 0.0), axis=-1, keepdims=True)
            ye_ref[b, 0] = (acc_ref[b] * gate).astype(BF16)


def _ffn(xe, gs, wg, wu, wd, ft=1024, mb=1):
    b, e, cap, d = xe.shape
    ff = wg.shape[2]
    return pl.pallas_call(
        functools.partial(_ffn_body, mb=mb),
        grid=(e, ff // ft),
        in_specs=[pl.BlockSpec((b, 1, cap, d), lambda ei, fi: (0, ei, 0, 0)),
                  pl.BlockSpec((b, 1, cap, LANES), lambda ei, fi: (0, ei, 0, 0)),
                  pl.BlockSpec((1, d, ft), lambda ei, fi: (ei, 0, fi)),
                  pl.BlockSpec((1, d, ft), lambda ei, fi: (ei, 0, fi)),
                  pl.BlockSpec((1, ft, d), lambda ei, fi: (ei, fi, 0))],
        out_specs=pl.BlockSpec((b, 1, cap, d), lambda ei, fi: (0, ei, 0, 0)),
        out_shape=jax.ShapeDtypeStruct((b, e, cap, d), BF16),
        scratch_shapes=[pltpu.VMEM((b, cap, d), F32)],
        compiler_params=_cparams(("arbitrary", "arbitrary"), vmem=VMEM_LIMIT_LARGE),
        name="ffn",
    )(xe, gs, wg, wu, wd)


def _scatter_body(bounds_ref, rank_ref, ye_ref, x2_ref, g_ref, out_ref, acc_ref, *, eg):
    w = SCATTER_WINDOW
    ts = x2_ref.shape[1]
    ne, cap = ye_ref.shape[1], ye_ref.shape[2]
    starts, passes = _moe_windows(bounds_ref, pl.program_id(0) * ne * LANES, ne, pl.program_id(1), ts, w)
    rank = rank_ref[0]
    acc_ref[...] = x2_ref[0]
    iota_w = lax.broadcasted_iota(jnp.int32, (1, w), 1)

    def operands(p, g0):
        onehots, rows = [], []
        for e in range(g0, g0 + eg):
            lo = starts[e] + p * w
            phys = pl.multiple_of(jnp.minimum(lo, cap - w), BF16_ROWS)
            slot = (iota_w + phys).astype(F32)
            hit = (rank[:, e:e + 1] == slot) & (slot >= lo.astype(F32))
            onehots.append(jnp.where(hit, 1.0, 0.0).astype(BF16))
            rows.append(ye_ref[0, e, pl.ds(phys, w), :])
        return jnp.concatenate(onehots, axis=1), jnp.concatenate(rows, axis=0)

    def one_pass(p, carry):
        ops = [operands(p, g0) for g0 in range(0, ne, eg)]
        total = _dot(*ops[0])
        for lhs, rhs in ops[1:]:
            total = total + _dot(lhs, rhs)
        acc_ref[...] += total
        return carry

    lax.fori_loop(0, passes, one_pass, 0)
    out_ref[0] = _rms(acc_ref[...], g_ref[...])


def _scatter(bounds, rank_t, ye, x2, g, eg=8):
    b, s, e = rank_t.shape
    cap, d = ye.shape[2], ye.shape[3]
    ts = SCATTER_TOKENS
    return pl.pallas_call(
        functools.partial(_scatter_body, eg=eg),
        grid_spec=pltpu.PrefetchScalarGridSpec(
            num_scalar_prefetch=1,
            grid=(b, s // ts),
            in_specs=[pl.BlockSpec((1, ts, e), lambda bi, ti, bnd: (bi, ti, 0)),
                      pl.BlockSpec((1, e, cap, d), lambda bi, ti, bnd: (bi, 0, 0, 0)),
                      pl.BlockSpec((1, ts, d), lambda bi, ti, bnd: (bi, ti, 0)),
                      pl.BlockSpec((1, d), lambda bi, ti, bnd: (0, 0))],
            out_specs=pl.BlockSpec((1, ts, d), lambda bi, ti, bnd: (bi, ti, 0)),
            scratch_shapes=[pltpu.VMEM((ts, d), F32)],
        ),
        out_shape=jax.ShapeDtypeStruct((b, s, d), F32),
        compiler_params=_cparams(("arbitrary", "arbitrary")),
        name="scatter",
    )(bounds.reshape(-1), rank_t, ye, x2, g)


def kernel(x, mem, ln_mix_g, w_in, hy_conv_w, hy_conv_b, filt_w1, filt_b1, filt_freq1, filt_w2, filt_b2,
           filt_freq2, filt_w3, hy_skip, w_up_hy, w_up_attn, w_gate, b_gate, w_out, ln_x_g, ln_mem_g,
           w_q_x, w_kv_mem, w_o_x, ln_moe_g, w_router, w_e_gate, w_e_up, w_e_down, ln_f_g):
    b, s, d = x.shape
    n = b * s
    x2d = x.reshape(n, d)

    half = HEAD_DIM // 2
    inv = ROPE_THETA ** (-jnp.arange(0, HEAD_DIM, 2, dtype=F32) / HEAD_DIM)
    ang = jnp.arange(s, dtype=F32)[:, None] * inv[None, :]
    lane = np.arange(LANES)
    cos_t = jnp.cos(ang)[:, lane % half]
    sin_t = jnp.sin(ang)[:, lane % half] * jnp.asarray(np.where(lane % HEAD_DIM < half, -1.0, 1.0), F32)[None, :]

    p_hy, *qkv, gates = _proj(x2d, ln_mix_g[None], w_in.astype(BF16), w_gate.astype(BF16), b_gate[None],
                              cos_t, sin_t, b, s)

    t_col = jnp.linspace(0.0, 1.0, s, dtype=F32)[:, None]
    grid_col = 2.0 * math.pi * jnp.arange(s, dtype=F32)[:, None] / s
    bands = jnp.linspace(1e-4, HY_BANDS - 1, HY_BANDS, dtype=F32)[None, :]
    feat = _filt(t_col, grid_col, bands, filt_w1, filt_b1, filt_freq1, filt_w2, filt_b2, filt_freq2)
    delta = jnp.linspace(math.log(HY_TARGET) / HY_SLOW_PCT, math.log(HY_TARGET) / HY_FAST_PCT,
                         HY_WIDTH, dtype=F32)[None, :]
    tables = _fft_tables(s)
    khat = _hyfilt(t_col, delta, feat, filt_w3, tables[1], tables[3])
    z = _hyena(p_hy, hy_conv_w, hy_conv_b[None], hy_skip, khat, tables)

    outs, lses = [], []
    for g, (window, dil) in enumerate(DIL_PAIRS):
        o_g, l_g = _attn_group(*qkv[3 * g:3 * g + 3], g, window // (2 * dil))
        outs.append(o_g)
        lses.append(l_g)

    kv = _memkv(mem, ln_mem_g[None], w_kv_mem.astype(BF16))
    wr_pad = jnp.pad(w_router, ((0, 0), (0, LANES - N_EXPERTS)))
    wr_cat = jnp.concatenate(_split(wr_pad), axis=1)
    x2, hm, aff, aff_hi, aff_lo = _merge(x2d, z.reshape(n, HY_WIDTH), outs, lses, gates,
                         w_up_hy.astype(BF16), w_up_attn.astype(BF16), w_out.astype(BF16),
                         ln_x_g[None], w_q_x.astype(BF16), kv, w_o_x.astype(BF16), ln_moe_g[None],
                         wr_cat, s)

    cap = max(1, EC_FACTOR * s // N_EXPERTS)
    aff3 = aff.reshape(b, s, LANES)
    rank, bounds = _topk(aff3[:, :, :N_EXPERTS].transpose(0, 2, 1), cap, MOE_BOUND_STEP)
    xe, gs = _gather(bounds, rank, hm.reshape(b, s, d), aff_hi.reshape(b, s, LANES), aff_lo.reshape(b, s, LANES), cap)
    ye = _ffn(xe, gs, w_e_gate, w_e_up, w_e_down)
    return _scatter(bounds, rank.transpose(0, 2, 1), ye, x2.reshape(b, s, d), ln_f_g[None])
```

```python
import functools
import math

import numpy as np
import jax
import jax.numpy as jnp
from jax import lax
from jax.experimental import pallas as pl
from jax.experimental.pallas import tpu as pltpu

F32 = jnp.float32
BF16 = jnp.bfloat16
HIGHEST = lax.Precision.HIGHEST

EPS = 1e-6
HY_WIDTH = 768
HY_BANDS = 16
HY_FFN = 64
HY_FAST_PCT = 0.3
HY_SLOW_PCT = 1.5
HY_TARGET = 1e-2
HEAD_DIM = 64
HEADS_PER_GROUP = 4
DIL_PAIRS = ((128, 1), (512, 4), (2048, 16))
N_GROUPS = len(DIL_PAIRS)
ATTN_WIDTH = N_GROUPS * HEADS_PER_GROUP * HEAD_DIM
ATTN_OUT = HEADS_PER_GROUP * HEAD_DIM
ROPE_THETA = 10000.0
X_HEADS = 4
X_HEAD_DIM = 128
N_EXPERTS = 16
EC_FACTOR = 2

LANES = 128
SUBLANES = 8
MXU_COLS = 256
VMEM_LIMIT = 56 * 1024 * 1024
VMEM_LIMIT_LARGE = 60 * 1024 * 1024

FFT_N2 = 128
HY_CB = 128
ATTN_BLOCKS = 8
FFT_UNROLL = 8
FFT_PAD = 8
MOE_BOUND_STEP = 256
GATHER_TOKENS, GATHER_WINDOW = 256, 64
SCATTER_TOKENS, SCATTER_WINDOW = 512, 128
BF16_ROWS = 16


def _cparams(sem, vmem=VMEM_LIMIT):
    return pltpu.CompilerParams(dimension_semantics=sem, vmem_limit_bytes=vmem)


def _rms(x, g):
    return x * lax.rsqrt(jnp.mean(x * x, axis=-1, keepdims=True) + EPS) * g


def _dot(a, b):
    return jnp.dot(a, b, preferred_element_type=F32)


def _dot_hi(a, b):
    return jnp.dot(a, b, precision=HIGHEST, preferred_element_type=F32)


def _split(a):
    hi = a.astype(BF16)
    return hi, (a - hi.astype(F32)).astype(BF16)


def _dot_split(a, b):
    a_hi, a_lo = _split(a)
    b_hi, b_lo = _split(b)
    return _dot(a_hi, b_hi) + _dot(a_lo, b_hi) + _dot(a_hi, b_lo)


def _dot_t(a, b):
    return lax.dot_general(a, b, (((1,), (1,)), ((), ())), preferred_element_type=F32)


def _proj_body(x_ref, g_ref, w_ref, wg_ref, bg_ref, cos_ref, sin_ref, *rest, hyc, aw, gr):
    ng = N_GROUPS
    perm_refs = (None,) + rest[:ng - 1]
    phy_ref = rest[ng - 1]
    qkv_refs = rest[ng:ng + 3 * ng]
    gate_ref = rest[ng + 3 * ng]
    tm = x_ref.shape[0]
    sls = [slice(r0, r0 + gr) for r0 in range(0, tm, gr)]
    lane = lax.broadcasted_iota(jnp.int32, (gr, LANES), 1)
    first = (lane % HEAD_DIM) < (HEAD_DIM // 2)
    gl = ATTN_OUT // LANES

    def rope(t, sl, scale):
        cos, sin = cos_ref[sl, :], sin_ref[sl, :]
        chunks = []
        for j in range(aw // LANES):
            tj = t[:, j * LANES:(j + 1) * LANES]
            partner = jnp.where(first, pltpu.roll(tj, LANES - HEAD_DIM // 2, 1),
                                pltpu.roll(tj, HEAD_DIM // 2, 1))
            chunks.append(((tj * cos + partner * sin) * scale).astype(BF16))
        return chunks

    hs = [_rms(x_ref[sl, :], g_ref[...]).astype(BF16) for sl in sls]
    phys = [_dot(h, w_ref[:, :hyc]) for h in hs]
    qf = [_dot(h, w_ref[:, hyc:hyc + aw]) for h in hs]
    kf = [_dot(h, w_ref[:, hyc + aw:hyc + 2 * aw]) for h in hs]
    vf = [_dot(h, w_ref[:, hyc + 2 * aw:hyc + 3 * aw]).astype(BF16) for h in hs]
    gf = [_dot(h, wg_ref[...]) for h in hs]
    for sl, phy in zip(sls, phys):
        for j in range(hyc // HY_CB):
            phy_ref[0, j, sl, :] = phy[:, j * HY_CB:(j + 1) * HY_CB]
    qs = [rope(t, sl, HEAD_DIM ** -0.5) for t, sl in zip(qf, sls)]
    ks = [rope(t, sl, 1.0) for t, sl in zip(kf, sls)]
    vs = [[t[:, j * LANES:(j + 1) * LANES] for j in range(aw // LANES)] for t in vf]
    for p, (q, k, v) in enumerate(zip(qs, ks, vs)):
        for g in range(ng):
            for i, t in enumerate((q, k, v)):
                out_ref = qkv_refs[3 * g + i]
                tg = jnp.concatenate(t[g * gl:(g + 1) * gl], axis=1)
                if perm_refs[g] is not None:
                    tg = _dot(perm_refs[g][...], tg).astype(BF16)
                dil = out_ref.shape[1]
                rows = gr // dil
                for r in range(dil):
                    out_ref[0, r, p * rows:(p + 1) * rows, :] = tg[r * rows:(r + 1) * rows, :]
    for sl, t in zip(sls, gf):
        gate_ref[sl, :] = jax.nn.sigmoid(t + bg_ref[...]).astype(BF16)


def _proj(x2d, g, w_in, w_gate, b_gate, cos_t, sin_t, batch, seq, tm=512, gr=256):
    n, d = x2d.shape
    hyc = 3 * HY_WIDTH
    aw = ATTN_WIDTH
    gd = w_gate.shape[1]
    assert w_in.shape[1] == hyc + 3 * aw
    nseq = seq // tm
    row = lambda i: (i, 0)
    const = lambda i: (0, 0)
    perms = []
    for _, dil in DIL_PAIRS[1:]:
        rows = gr // dil
        src = (np.arange(gr) % rows) * dil + np.arange(gr) // rows
        perms.append(jnp.asarray(np.eye(gr)[src], dtype=BF16))
    qkv_specs, qkv_shapes = [], []
    for _, dil in DIL_PAIRS:
        for _ in range(3):
            qkv_specs.append(pl.BlockSpec((1, dil, tm // dil, ATTN_OUT), lambda i: (i // nseq, 0, i % nseq, 0)))
            qkv_shapes.append(jax.ShapeDtypeStruct((batch, dil, seq // dil, ATTN_OUT), BF16))
    return pl.pallas_call(
        functools.partial(_proj_body, hyc=hyc, aw=aw, gr=gr),
        grid=(n // tm,),
        in_specs=[
            pl.BlockSpec((tm, d), row),
            pl.BlockSpec((1, d), const),
            pl.BlockSpec(w_in.shape, const, pipeline_mode=pl.Buffered(1)),
            pl.BlockSpec(w_gate.shape, const, pipeline_mode=pl.Buffered(1)),
            pl.BlockSpec((1, gd), const),
            pl.BlockSpec((tm, LANES), lambda i: (i % nseq, 0)),
            pl.BlockSpec((tm, LANES), lambda i: (i % nseq, 0)),
        ] + [pl.BlockSpec((gr, gr), const) for _ in perms],
        out_specs=([pl.BlockSpec((1, hyc // HY_CB, tm, HY_CB), lambda i: (i // nseq, 0, i % nseq, 0))]
                   + qkv_specs + [pl.BlockSpec((tm, gd), row)]),
        out_shape=([jax.ShapeDtypeStruct((batch, hyc // HY_CB, seq, HY_CB), F32)] + qkv_shapes
                   + [jax.ShapeDtypeStruct((n, gd), BF16)]),
        compiler_params=_cparams(("arbitrary",)),
        name="proj",
    )(x2d, g, w_in, w_gate, b_gate, cos_t, sin_t, *perms)


def _filt_body(t_ref, grid_ref, bands_ref, w1t_ref, w1c_ref, w1s_ref, b1_ref, f1_ref,
               w2_ref, b2_ref, f2_ref, feat_ref):
    ang = bands_ref[...] * grid_ref[...]
    pre = (t_ref[...] * w1t_ref[...] + _dot_hi(jnp.cos(ang), w1c_ref[...])
           + _dot_hi(-jnp.sin(ang), w1s_ref[...]))
    h = jnp.sin(f1_ref[...] * (pre + b1_ref[...]))
    feat_ref[...] = jnp.sin(f2_ref[...] * (_dot_hi(h, w2_ref[...]) + b2_ref[...]))


def _filt(t_col, grid_col, bands, w1, b1, f1, w2, b2, f2):
    length = t_col.shape[0]
    nb = bands.shape[1]
    args = (t_col, grid_col, bands, w1[0:1], w1[1:1 + nb], w1[1 + nb:], b1[None], f1[None],
            w2, b2[None], f2[None])
    return pl.pallas_call(
        _filt_body,
        out_shape=jax.ShapeDtypeStruct((length, HY_FFN), F32),
        compiler_params=_cparams(None),
        name="filt",
    )(*args)


def _fft_tables(length):
    n = 2 * length
    n2 = FFT_N2
    n1 = n // n2
    h1 = n1 // 2
    k1 = np.arange(n1)[:, None]
    g_sig = np.zeros((n2, 2 * n1, 2 * h1))
    g_flt = np.zeros((n2, 2 * n1, 2 * h1))
    g_inv = np.zeros((n2, 2 * h1, 2 * n1))
    j = np.arange(h1)[None, :]
    for r in range(n2):
        th = 2 * np.pi * (k1 * (n2 * j + r) % n) / n
        gr, gi = np.cos(th), -np.sin(th)
        g_sig[r, 0::2, :h1] = gr
        g_sig[r, 0::2, h1:] = -gi
        g_sig[r, 1::2, :h1] = gi
        g_sig[r, 1::2, h1:] = gr
        g_flt[r, 0::2, :h1] = gr
        g_flt[r, 1::2, :h1] = gi
        m = n2 * (j + 1) - r
        thb = 2 * np.pi * (k1 * m % n) / n
        live = (m < length).astype(np.float64)
        g_flt[r, 0::2, h1:] = np.cos(thb) * live
        g_flt[r, 1::2, h1:] = np.sin(thb) * live
        wr, wi = (np.cos(th) / n).T, (np.sin(th) / n).T
        g_inv[r, :h1, 0::2] = wr
        g_inv[r, :h1, 1::2] = -wi
        g_inv[r, h1:, 0::2] = wi
        g_inv[r, h1:, 1::2] = wr
    a = np.arange(n2)
    th2 = 2 * np.pi * (np.outer(a, a) % n2) / n2
    c2, s2 = np.cos(th2), np.sin(th2)
    f_fwd = np.block([[c2, s2], [-s2, c2]])
    f_inv = np.block([[c2, -s2], [s2, c2]])
    cast = lambda z: jnp.asarray(z, dtype=F32).astype(BF16)
    return cast(g_sig), cast(g_flt), cast(g_inv), cast(f_fwd), cast(f_inv)


def _fft_stage1(src_a_ref, src_b_ref, tab_ref, work_ref, n1, is_filter):
    n2 = FFT_N2
    grp = SUBLANES

    def body(i, c):
        r0 = pl.multiple_of(i * grp, grp)
        b0 = pl.multiple_of(n2 - grp - r0, grp) if is_filter else r0
        a = jnp.swapaxes(src_a_ref[:, pl.ds(r0, grp), :], 0, 1)
        b = jnp.swapaxes(src_b_ref[:, pl.ds(b0, grp), :], 0, 1)
        outs = []
        for s in range(grp):
            rhs = jnp.concatenate([a[s], b[grp - 1 - s] if is_filter else b[s]], axis=0).astype(BF16)
            outs.append(_dot(tab_ref[r0 + s], rhs))
        work_ref[:, pl.ds(r0, grp), :] = jnp.swapaxes(jnp.stack(outs, axis=0), 0, 1)
        return c

    lax.fori_loop(0, n2 // grp, body, 0, unroll=4)


def _hyfilt_body(t_ref, ts_ref, delta_ref, feat_ref, feats_ref, w3f_ref, w3b_ref, gflt_ref, ffwd_ref, khat_ref,
                 hf_ref, hb_ref, work_ref, *, length, rc):
    n2 = FFT_N2
    n1 = 2 * length // n2
    tiles = rc // n2
    cb = hf_ref.shape[2]

    def gen(i, c):
        r0 = pl.multiple_of(i * rc, rc)
        j0 = pl.multiple_of(i * tiles, tiles)
        delta = jnp.abs(delta_ref[...])
        hf = _dot_split(feat_ref[pl.ds(r0, rc), :], w3f_ref[...]) * jnp.exp(-t_ref[pl.ds(r0, rc), :] * delta)
        hb = _dot_split(feats_ref[pl.ds(r0, rc), :], w3b_ref[...]) * jnp.exp(-ts_ref[pl.ds(r0, rc), :] * delta)
        hf_ref[pl.ds(j0, tiles), :n2, :] = hf.reshape(tiles, n2, cb)
        hb_ref[pl.ds(j0, tiles), :n2, :] = hb.reshape(tiles, n2, cb)
        return c

    lax.fori_loop(0, length // rc, gen, 0)
    _fft_stage1(hf_ref, hb_ref, gflt_ref, work_ref, n1, True)

    side = MXU_COLS // cb

    def stage2(i, c):
        k1s = [i * side + v for v in range(side)]
        blks = [work_ref[pl.ds(pl.multiple_of(2 * k1, 2), 2), :n2, :].reshape(2 * n2, cb).astype(BF16)
                for k1 in k1s]
        x = _dot(ffwd_ref[...], jnp.concatenate(blks, axis=1)).astype(BF16)
        for v, k1 in enumerate(k1s):
            khat_ref[0, 0, pl.ds(pl.multiple_of(k1 * (2 * n2), 2 * n2), 2 * n2), :] = x[:, v * cb:(v + 1) * cb]
        return c

    lax.fori_loop(0, n1 // side, stage2, 0, unroll=FFT_UNROLL // side)


def _hyfilt(t_col, delta, feat, w3, g_flt, f_fwd, rc=512):
    s = t_col.shape[0]
    cb = HY_CB
    ncb = HY_WIDTH // cb
    n2 = FFT_N2
    n1 = 2 * s // n2
    norder = w3.shape[1] // (2 * HY_WIDTH)
    one = pl.Buffered(1)

    def full(a):
        nd = a.ndim
        return pl.BlockSpec(a.shape, lambda o, c: (0,) * nd, pipeline_mode=one)

    shift = lambda a: jnp.concatenate([a[1:], jnp.zeros_like(a[:1])], axis=0)
    return pl.pallas_call(
        functools.partial(_hyfilt_body, length=s, rc=rc),
        grid=(norder, ncb),
        in_specs=[full(t_col), full(t_col),
                  pl.BlockSpec((1, cb), lambda o, c: (0, c)),
                  full(feat), full(feat),
                  pl.BlockSpec((HY_FFN, cb), lambda o, c: (0, 2 * ncb * o + c)),
                  pl.BlockSpec((HY_FFN, cb), lambda o, c: (0, 2 * ncb * o + ncb + c)),
                  full(g_flt), full(f_fwd)],
        out_specs=pl.BlockSpec((1, 1, 2 * n1 * n2, cb), lambda o, c: (o, c, 0, 0)),
        out_shape=jax.ShapeDtypeStruct((norder, ncb, 2 * n1 * n2, cb), BF16),
        scratch_shapes=[pltpu.VMEM((n1 // 2, n2 + FFT_PAD, cb), F32)] * 2
        + [pltpu.VMEM((2 * n1, n2 + FFT_PAD, cb), F32)],
        compiler_params=_cparams(("arbitrary", "arbitrary")),
        name="hyfilt",
    )(t_col, shift(t_col), delta, feat, shift(feat), w3, w3, g_flt, f_fwd)


def _hyena_body(pz_ref, phy_hbm, cwz_ref, cwg1_ref, cwg2_ref, cbz_ref, cbg1_ref, cbg2_ref,
                skip_ref, khat_hbm, gsig_ref, ginv_ref, ffwd_ref, finv_ref,
                out_ref,
                za_ref, zb_ref, ya_ref, yb_ref, work_ref, pg1_ref, pg2_ref, khat_ref, sem, ksem, *, length, rc):
    ncb = pl.num_programs(0)
    pair = pl.ds(2 * pl.program_id(1), 2)

    def gate_copy(order, dst_ref):
        src = phy_hbm.at[pair, pl.ds((order + 1) * ncb + pl.program_id(0), 1)]
        return pltpu.make_async_copy(src, dst_ref, sem.at[order])

    gate_copies = [gate_copy(0, pg1_ref), gate_copy(1, pg2_ref)]
    for cp in gate_copies:
        cp.start()

    new_block = pl.program_id(1) == 0
    spec_copies = [pltpu.make_async_copy(khat_hbm.at[o, pl.program_id(0)], khat_ref.at[o], ksem.at[o])
                   for o in range(khat_ref.shape[0])]

    @pl.when(new_block)
    def _():
        for cp in spec_copies:
            cp.start()
    n2 = FFT_N2
    n1 = 2 * length // n2
    h1 = n1 // 2
    tiles = rc // n2
    cb = za_ref.shape[2]
    side = MXU_COLS // cb
    sub = SUBLANES

    def conv3(p_ref, b, r0, w_ref, bias_ref):
        p = p_ref[b, 0, pl.ds(r0, rc), :]
        before = p_ref[b, 0, pl.ds(pl.multiple_of(jnp.maximum(r0 - sub, 0), sub), sub), :][sub - 1:sub, :]
        after = p_ref[b, 0, pl.ds(pl.multiple_of(jnp.minimum(r0 + rc, length - sub), sub), sub), :][0:1, :]
        before = jnp.where(r0 == 0, 0.0, before)
        after = jnp.where(r0 + rc == length, 0.0, after)
        row = lax.broadcasted_iota(jnp.int32, p.shape, 0)
        prev = jnp.where(row == 0, before, pltpu.roll(p, 1, 0))
        nxt = jnp.where(row == rc - 1, after, pltpu.roll(p, rc - 1, 0))
        return prev * w_ref[0:1, :] + p * w_ref[1:2, :] + nxt * w_ref[2:3, :] + bias_ref[...]

    def stage2(o):
        def body(i, c):
            k1s = [i * FFT_UNROLL + u for u in range(FFT_UNROLL)]
            blks = [work_ref[pl.ds(pl.multiple_of(2 * k1, 2), 2), :n2, :].reshape(2 * n2, cb).astype(BF16)
                    for k1 in k1s]
            groups = range(0, FFT_UNROLL, side)
            xs = [_dot(ffwd_ref[...], jnp.concatenate(blks[u:u + side], axis=1)) for u in groups]

            def filtered(u, x):
                ys = []
                for v in range(side):
                    k1 = k1s[u + v]
                    kh = khat_ref[o, pl.ds(pl.multiple_of(k1 * (2 * n2), 2 * n2), 2 * n2), :].astype(F32)
                    xr, xi = x[:n2, v * cb:(v + 1) * cb], x[n2:, v * cb:(v + 1) * cb]
                    kr, ki = kh[:n2], kh[n2:]
                    ys.append(jnp.concatenate([xr * kr - xi * ki, xr * ki + xi * kr], axis=0).astype(BF16))
                return jnp.concatenate(ys, axis=1)

            ys = [filtered(u, x) for u, x in zip(groups, xs)]
            outs = []
            for y in ys:
                out = _dot(finv_ref[...], y)
                outs.extend(out[:, v * cb:(v + 1) * cb] for v in range(side))
            for k1, out in zip(k1s, outs):
                work_ref[pl.ds(pl.multiple_of(2 * k1, 2), 2), :n2, :] = out.reshape(2, n2, cb)
            return c
        lax.fori_loop(0, n1 // FFT_UNROLL, body, 0)

    def stage3():
        grp = SUBLANES

        def body(i, c):
            r0 = pl.multiple_of(i * grp, grp)
            wk = jnp.swapaxes(work_ref[:, pl.ds(r0, grp), :], 0, 1)
            res = jnp.stack([_dot(ginv_ref[r0 + s], wk[s].astype(BF16)) for s in range(grp)], axis=0)
            ya_ref[:, pl.ds(r0, grp), :] = jnp.swapaxes(res[:, :h1], 0, 1)
            yb_ref[:, pl.ds(r0, grp), :] = jnp.swapaxes(res[:, h1:], 0, 1)
            return c
        lax.fori_loop(0, n2 // grp, body, 0, unroll=4)

    def rows3(ref, i):
        return ref.at[pl.ds(pl.multiple_of(i * tiles, tiles), tiles), :n2, :]

    def load_z(i, c):
        r0 = pl.multiple_of(i * rc, rc)
        rows3(za_ref, i)[...] = conv3(pz_ref, 0, r0, cwz_ref, cbz_ref).reshape(tiles, n2, cb)
        rows3(zb_ref, i)[...] = conv3(pz_ref, 1, r0, cwz_ref, cbz_ref).reshape(tiles, n2, cb)
        return c

    lax.fori_loop(0, length // rc, load_z, 0)
    for o, (pg_ref, cw_ref, cb_ref) in enumerate(((pg1_ref, cwg1_ref, cbg1_ref), (pg2_ref, cwg2_ref, cbg2_ref))):
        _fft_stage1(za_ref, zb_ref, gsig_ref, work_ref, n1, False)

        @pl.when(new_block)
        def _():
            spec_copies[o].wait()

        stage2(o)
        stage3()
        skip = skip_ref[o:o + 1, :]
        gate_copies[o].wait()

        def gate(i, c):
            r0 = pl.multiple_of(i * rc, rc)
            rows = pl.ds(r0, rc)
            ya, yb = (rows3(r, i)[...].reshape(rc, cb) for r in (ya_ref, yb_ref))
            za, zb = (rows3(r, i)[...].reshape(rc, cb) for r in (za_ref, zb_ref))
            new_a = conv3(pg_ref, 0, r0, cw_ref, cb_ref) * (ya + skip * za)
            new_b = conv3(pg_ref, 1, r0, cw_ref, cb_ref) * (yb + skip * zb)
            if o == 0:
                rows3(za_ref, i)[...] = new_a.reshape(tiles, n2, cb)
                rows3(zb_ref, i)[...] = new_b.reshape(tiles, n2, cb)
            else:
                out_ref[0, rows, :] = new_a.astype(out_ref.dtype)
                out_ref[1, rows, :] = new_b.astype(out_ref.dtype)
            return c

        lax.fori_loop(0, length // rc, gate, 0)


def _hyena(p_hy, conv_w, conv_b, skip, khat, tables, rc=512):
    b, _, s, _ = p_hy.shape
    w = HY_WIDTH
    cb = HY_CB
    ncb = w // cb
    n2 = FFT_N2
    n1 = 2 * s // n2
    g_sig, _, g_inv, f_fwd, f_inv = tables
    one = pl.Buffered(1)

    def cwspec(off):
        return pl.BlockSpec((3, cb), lambda c, p: (0, off + c))

    def cbspec(off):
        return pl.BlockSpec((1, cb), lambda c, p: (0, off + c))

    def full(a):
        nd = a.ndim
        return pl.BlockSpec(a.shape, lambda c, p: (0,) * nd, pipeline_mode=one)

    in_specs = [
        pl.BlockSpec((2, 1, s, cb), lambda c, p: (p, c, 0, 0)),
        pl.BlockSpec(memory_space=pl.ANY),
        cwspec(0), cwspec(ncb), cwspec(2 * ncb),
        cbspec(0), cbspec(ncb), cbspec(2 * ncb),
        pl.BlockSpec((2, cb), lambda c, p: (0, c)),
        pl.BlockSpec(memory_space=pl.ANY),
        full(g_sig), full(g_inv), full(f_fwd), full(f_inv),
    ]
    return pl.pallas_call(
        functools.partial(_hyena_body, length=s, rc=rc),
        grid=(ncb, b // 2),
        in_specs=in_specs,
        out_specs=pl.BlockSpec((2, s, cb), lambda c, p: (p, 0, c)),
        out_shape=jax.ShapeDtypeStruct((b, s, w), BF16),
        scratch_shapes=[pltpu.VMEM((n1 // 2, n2 + FFT_PAD, cb), F32)] * 4
        + [pltpu.VMEM((2 * n1, n2 + FFT_PAD, cb), F32)]
        + [pltpu.VMEM((2, 1, s, cb), F32)] * 2 + [pltpu.VMEM((khat.shape[0], 2 * n1 * n2, cb), BF16)]
        + [pltpu.SemaphoreType.DMA((2,)), pltpu.SemaphoreType.DMA((khat.shape[0],))],
        compiler_params=_cparams(("arbitrary", "arbitrary")),
        name="hyena",
    )(p_hy, p_hy, conv_w, conv_w, conv_w, conv_b, conv_b, conv_b,
      skip, khat, g_sig, g_inv, f_fwd, f_inv)


def _attn_body(q_ref, k_ref, v_ref, o_ref, lse_ref, *, n_side, qb):
    lr = q_ref.shape[2]
    kw = qb + 2 * n_side
    nh = HEADS_PER_GROUP
    lane_head = lax.broadcasted_iota(jnp.int32, (qb, ATTN_OUT), 1) // HEAD_DIM
    qi = lax.broadcasted_iota(jnp.int32, (nh * qb, kw), 0) % qb
    ki = lax.broadcasted_iota(jnp.int32, (nh * qb, kw), 1)
    band = [jnp.where(jnp.abs(ki - qi - shift) <= n_side, 0.0, -1e30) for shift in (0, n_side, 2 * n_side)]

    def scores(cls, i):
        q0 = pl.multiple_of(i * qb, qb)
        w0 = pl.multiple_of(jnp.clip(q0 - n_side, 0, lr - kw), n_side)
        q = q_ref[0, cls, pl.ds(q0, qb), :]
        shift = q0 - w0
        bias = jnp.where(shift == n_side, band[1], jnp.where(shift == 0, band[0], band[2]))
        qs = jnp.concatenate([jnp.where(lane_head == h, q, jnp.zeros_like(q)) for h in range(nh)], axis=0)
        return q0, w0, _dot_t(qs, k_ref[0, cls, pl.ds(w0, kw), :]) + bias

    def softmax(s):
        m = jnp.max(s, axis=-1, keepdims=True)
        p = jnp.exp(s - m)
        l = jnp.sum(p, axis=-1, keepdims=True)
        return p.astype(BF16), l, m + jnp.log(l)

    def store(cls, q0, pv, lse_rows):
        o = jnp.zeros((qb, ATTN_OUT), F32)
        lse = jnp.zeros((qb, ATTN_OUT), F32)
        for h in range(nh):
            mine = lane_head == h
            o = jnp.where(mine, pv[h * qb:(h + 1) * qb], o)
            lse = jnp.where(mine, lse_rows[h * qb:(h + 1) * qb], lse)
        o_ref[0, pl.ds(q0, qb), cls * ATTN_OUT:(cls + 1) * ATTN_OUT] = o
        lse_ref[0, pl.ds(q0, qb), cls * ATTN_OUT:(cls + 1) * ATTN_OUT] = lse

    per_trip = min(ATTN_BLOCKS, lr // qb)

    def body(cls, j, c):
        sc = [scores(cls, per_trip * j + t) for t in range(per_trip)]
        sm = [softmax(s) for _, _, s in sc]
        pvs = [_dot(p, v_ref[0, cls, pl.ds(w0, kw), :]) / l for (_, w0, _), (p, l, _) in zip(sc, sm)]
        for (q0, _, _), (_, _, lse_rows), pv in zip(sc, sm, pvs):
            store(cls, q0, pv, lse_rows)
        return c

    for cls in range(q_ref.shape[1]):
        lax.fori_loop(0, lr // (qb * per_trip), functools.partial(body, cls), 0)


def _attn_group(q, k, v, g, n_side, qb=128, max_classes=8):
    b, dil, lr, _ = q.shape
    cg = min(dil, max_classes)
    in_spec = pl.BlockSpec((1, cg, lr, ATTN_OUT), lambda bi, r: (bi, r, 0, 0))
    out_spec = pl.BlockSpec((1, lr, cg * ATTN_OUT), lambda bi, r: (bi, 0, r))
    o, lse = pl.pallas_call(
        functools.partial(_attn_body, n_side=n_side, qb=qb),
        grid=(b, dil // cg),
        in_specs=[in_spec, in_spec, in_spec],
        out_specs=[out_spec, out_spec],
        out_shape=[jax.ShapeDtypeStruct((b, lr, dil * ATTN_OUT), F32)] * 2,
        compiler_params=_cparams(("arbitrary", "arbitrary")),
        name=f"attn{g}",
    )(q, k, v)
    return o, lse


def _memkv_body(mem_ref, g_ref, w_ref, kv_ref):
    mn = _rms(mem_ref[0], g_ref[...]).astype(BF16)
    kv_ref[0] = _dot(mn, w_ref[...]).astype(BF16)


def _memkv(mem, g, w_kv):
    b, m, d = mem.shape
    n = w_kv.shape[1]
    return pl.pallas_call(
        _memkv_body,
        grid=(b,),
        in_specs=[pl.BlockSpec((1, m, d), lambda i: (i, 0, 0)),
                  pl.BlockSpec((1, d), lambda i: (0, 0)),
                  pl.BlockSpec((d, n), lambda i: (0, 0))],
        out_specs=pl.BlockSpec((1, m, n), lambda i: (i, 0, 0)),
        out_shape=jax.ShapeDtypeStruct((b, m, n), BF16),
        compiler_params=_cparams(("arbitrary",)),
        name="memkv",
    )(mem, g, w_kv)


def _merge_staged_body(x_ref, z_ref, o0_ref, o1_ref, o2_ref, l0_ref, l1_ref, l2_ref, gate_ref,
                       wuh_ref, wua_ref, wout_ref, gx_ref, wq_ref, kv_ref, wo_ref, gm_ref, wr_ref,
                       p1_ref, p2_ref, x2_ref, hm_ref, aff_ref, affh_ref, affl_ref, *, nsplit):
    d = x_ref.shape[1]
    rows = x_ref.shape[0] // nsplit
    parts = range(nsplit)
    sls = [slice(p * rows, (p + 1) * rows) for p in parts]

    def natural(ref, perm_ref, part):
        dil = ref.shape[2] // ATTN_OUT
        rr = rows // dil
        blk = ref[0, part * rr:(part + 1) * rr, :]
        if dil == 1:
            return blk
        pieces = [blk[:, r * ATTN_OUT:(r + 1) * ATTN_OUT] for r in range(dil)]
        if rr % SUBLANES == 0 and dil % SUBLANES == 0:
            return jnp.swapaxes(jnp.stack(pieces, axis=0), 0, 1).reshape(rows, ATTN_OUT)
        hi, lo = _split(jnp.concatenate(pieces, axis=0))
        return _dot(perm_ref[...], hi) + _dot(perm_ref[...], lo)

    def combine(part):
        o0, o1, o2 = (natural(r, p, part) for r, p in ((o0_ref, None), (o1_ref, p1_ref), (o2_ref, p2_ref)))
        l0, l1, l2 = (natural(r, p, part) for r, p in ((l0_ref, None), (l1_ref, p1_ref), (l2_ref, p2_ref)))
        mx = jnp.maximum(jnp.maximum(l0, l1), l2)
        e0, e1, e2 = jnp.exp(l0 - mx), jnp.exp(l1 - mx), jnp.exp(l2 - mx)
        return ((e0 * o0 + e1 * o1 + e2 * o2) / (e0 + e1 + e2)).astype(BF16)

    attn = [combine(p) for p in parts]
    y_hy = [_dot(z_ref[sl, :].astype(BF16), wuh_ref[...]) for sl in sls]
    y_at = [_dot(a, wua_ref[...]) for a in attn]

    def mixed(sl, yh, ya):
        gates = gate_ref[sl, :].astype(F32)
        return (gates[:, :d] * yh + gates[:, d:] * ya).astype(BF16)

    mix = [mixed(sl, yh, ya) for sl, yh, ya in zip(sls, y_hy, y_at)]
    x1 = [x_ref[sl, :] + _dot(m, wout_ref[...]) for sl, m in zip(sls, mix)]

    hx = [_rms(v, gx_ref[...]).astype(BF16) for v in x1]
    qx = [_dot(h, wq_ref[...]).astype(BF16) for h in hx]
    kv = kv_ref[0]
    xw = X_HEADS * X_HEAD_DIM

    def cross_attention(q):
        heads = []
        for h in range(X_HEADS):
            hc = slice(h * X_HEAD_DIM, (h + 1) * X_HEAD_DIM)
            s = _dot_t(q[:, hc], kv[:, hc]) * (X_HEAD_DIM ** -0.5)
            m = jnp.max(s, axis=-1, keepdims=True)
            p = jnp.exp(s - m)
            p = p / jnp.sum(p, axis=-1, keepdims=True)
            heads.append(_dot(p.astype(BF16), kv[:, xw + h * X_HEAD_DIM:xw + (h + 1) * X_HEAD_DIM]))
        return jnp.concatenate(heads, axis=-1).astype(BF16)

    ox = [cross_attention(q) for q in qx]
    x2 = [v + _dot(o, wo_ref[...]) for v, o in zip(x1, ox)]
    for sl, v in zip(sls, x2):
        x2_ref[sl, :] = v

    hm = [_split(_rms(v, gm_ref[...])) for v in x2]
    cross = [_dot(jnp.concatenate([hi, lo], axis=0), wr_ref[...]) for hi, lo in hm]
    for sl, (hi, _), c in zip(sls, hm, cross):
        hm_ref[sl, :] = hi
        logits = (c[:rows, :LANES] + c[:rows, LANES:]) + (c[rows:, :LANES] + c[rows:, LANES:])
        lane = lax.broadcasted_iota(jnp.int32, logits.shape, 1)
        logits = jnp.where(lane < N_EXPERTS, logits, -1e30)
        m = jnp.max(logits, axis=-1, keepdims=True)
        p = jnp.exp(logits - m)
        aff = p / jnp.sum(p, axis=-1, keepdims=True)
        aff_ref[sl, :] = aff
        affh_ref[sl, :], affl_ref[sl, :] = _split(aff)


def _merge(x2d, z2d, outs, lses, gates, wuh, wua, wout, gx, wq, kv, wo, gm, wr_cat, seq, tm=1024, nsplit=4):
    n, d = x2d.shape
    nseq = seq // tm
    row = lambda i: (i, 0)
    const = lambda i: (0, 0)

    def rspec(a):
        return pl.BlockSpec((tm, a.shape[1]), row)

    def aspec(a):
        dil = a.shape[2] // ATTN_OUT
        return pl.BlockSpec((1, tm // dil, a.shape[2]), lambda i: (i // nseq, i % nseq, 0))

    def cspec(a):
        return pl.BlockSpec(a.shape, const)

    rows = tm // nsplit
    perms = []
    for a in outs[1:]:
        dil = a.shape[2] // ATTN_OUT
        src = (np.arange(rows) % dil) * (rows // dil) + np.arange(rows) // dil
        perms.append(jnp.asarray(np.eye(rows)[src], dtype=BF16))

    in_arrays = [x2d, z2d, *outs, *lses, gates, wuh, wua, wout, gx, wq, kv, wo, gm, wr_cat, *perms]
    in_specs = ([rspec(a) for a in in_arrays[:2]] + [aspec(a) for a in in_arrays[2:8]] + [rspec(gates)]
                + [cspec(a) for a in in_arrays[9:14]]
                + [pl.BlockSpec((1,) + kv.shape[1:], lambda i: (i // nseq, 0, 0))]
                + [cspec(a) for a in in_arrays[15:]])
    return pl.pallas_call(
        functools.partial(_merge_staged_body, nsplit=nsplit),
        grid=(n // tm,),
        in_specs=in_specs,
        out_specs=[pl.BlockSpec((tm, d), row), pl.BlockSpec((tm, d), row)] + [pl.BlockSpec((tm, LANES), row)] * 3,
        out_shape=[jax.ShapeDtypeStruct((n, d), F32), jax.ShapeDtypeStruct((n, d), BF16),
                   jax.ShapeDtypeStruct((n, LANES), F32), jax.ShapeDtypeStruct((n, LANES), BF16),
                   jax.ShapeDtypeStruct((n, LANES), BF16)],
        compiler_params=_cparams(("arbitrary",), vmem=VMEM_LIMIT_LARGE),
        name="merge",
    )(*in_arrays)


def _topk_body(aff_ref, tri_ref, rank_ref, bounds_ref, *, cap, tok_block):
    a = aff_ref[0]
    e, s = a.shape

    def count(mask):
        return jnp.sum(jnp.where(mask, 1.0, 0.0), axis=-1, keepdims=True)

    def as_float(bits):
        return pltpu.bitcast(jnp.broadcast_to(bits, (e, LANES)), F32)[:, 0:1]

    def search(i, thr):
        cand = thr | (jnp.int32(1) << (30 - i))
        return jnp.where(count(a >= as_float(cand)) >= cap, cand, thr)

    thr = as_float(lax.fori_loop(0, 31, search, jnp.zeros((e, 1), jnp.int32)))
    gt = a > thr
    eq = a == thr
    need = cap - count(gt)

    def prefix_excl(mask):
        mf = jnp.where(mask, 1.0, 0.0)
        parts = []
        carry = jnp.zeros((e, 1), F32)
        for c in range(s // LANES):
            blk = mf[:, c * LANES:(c + 1) * LANES]
            inc = _dot(blk.astype(BF16), tri_ref[...])
            parts.append(inc - blk + carry)
            carry = carry + inc[:, LANES - 1:LANES]
        return jnp.concatenate(parts, axis=-1)

    sel = gt | (eq & (prefix_excl(eq) < need))
    excl = prefix_excl(sel)
    rank_ref[0] = jnp.where(sel, excl, -1.0)

    tok = lax.broadcasted_iota(jnp.int32, (e, s), 1)
    lane = lax.broadcasted_iota(jnp.int32, (e, LANES), 1)
    bounds = jnp.zeros((e, LANES), F32)
    for j in range(s // tok_block + 1):
        bounds = jnp.where(lane == j, count(sel & (tok < j * tok_block)), bounds)
    bounds_ref[0] = bounds.astype(jnp.int32)


def _topk(aff_t, cap, tok_block):
    b, e, s = aff_t.shape
    assert s // tok_block + 1 <= LANES
    tri = jnp.asarray(np.triu(np.ones((LANES, LANES))), dtype=BF16)
    return pl.pallas_call(
        functools.partial(_topk_body, cap=cap, tok_block=tok_block),
        grid=(b,),
        in_specs=[pl.BlockSpec((1, e, s), lambda i: (i, 0, 0)),
                  pl.BlockSpec((LANES, LANES), lambda i: (0, 0))],
        out_specs=[pl.BlockSpec((1, e, s), lambda i: (i, 0, 0)),
                   pl.BlockSpec((1, e, LANES), lambda i: (i, 0, 0))],
        out_shape=[jax.ShapeDtypeStruct((b, e, s), F32),
                   jax.ShapeDtypeStruct((b, e, LANES), jnp.int32)],
        compiler_params=_cparams(("arbitrary",)),
        name="topk",
    )(aff_t, tri)


def _moe_windows(bounds_ref, base, ne, step, tokens, window):
    per = tokens // MOE_BOUND_STEP
    starts = []
    passes = jnp.int32(0)
    for e in range(ne):
        r_lo = bounds_ref[base + e * LANES + step * per]
        r_hi = bounds_ref[base + e * LANES + (step + 1) * per]
        ws = (r_lo // BF16_ROWS) * BF16_ROWS
        starts.append(ws)
        passes = jnp.maximum(passes, (r_hi - ws + window - 1) // window)
    return starts, passes


def _gather_body(bounds_ref, rank_ref, hm_ref, affh_ref, affl_ref, xe_ref, gs_ref, *, cap, eg):
    w = GATHER_WINDOW
    ne, tc = rank_ref.shape[1], rank_ref.shape[2]
    bi = pl.program_id(0)
    ci = pl.program_id(1)

    @pl.when(ci == 0)
    def _():
        xe_ref[...] = jnp.zeros_like(xe_ref)
        gs_ref[...] = jnp.zeros_like(gs_ref)

    starts, passes = _moe_windows(bounds_ref, bi * ne * LANES, ne, ci, tc, w)
    iota_w = lax.broadcasted_iota(jnp.int32, (w, 1), 0)

    def selector(p, g0):
        lo = [starts[e] + p * w for e in range(g0, g0 + eg)]
        phys = [pl.multiple_of(jnp.minimum(v, cap - w), BF16_ROWS) for v in lo]
        rk = jnp.concatenate([jnp.broadcast_to(rank_ref[0, e:e + 1, :], (w, tc))
                              for e in range(g0, g0 + eg)], axis=0)
        slot = jnp.concatenate([iota_w + v for v in phys], axis=0).astype(F32)
        lom = jnp.concatenate([jnp.zeros_like(iota_w) + v for v in lo], axis=0).astype(F32)
        return phys, jnp.where((rk == slot) & (slot >= lom), 1.0, 0.0).astype(BF16)

    def one_pass(p, carry):
        groups = range(0, ne, eg)
        sels = [selector(p, g0) for g0 in groups]
        res = [_dot(onehot, hm_ref[0]) for _, onehot in sels]
        resg = [_dot(onehot, affh_ref[0]) + _dot(onehot, affl_ref[0]) for _, onehot in sels]
        for g0, (phys, _), r, rg in zip(groups, sels, res, resg):
            for k, e in enumerate(range(g0, g0 + eg)):
                rows = pl.ds(phys[k], w)
                xe_ref[0, e, rows, :] = (xe_ref[0, e, rows, :].astype(F32) + r[k * w:(k + 1) * w]).astype(BF16)
                gs_ref[0, e, rows, :] = gs_ref[0, e, rows, :] + rg[k * w:(k + 1) * w]
        return carry

    lax.fori_loop(0, passes, one_pass, 0)


def _gather(bounds, rank, hm, aff_hi, aff_lo, cap, eg=16):
    b, e, s = rank.shape
    d = hm.shape[2]
    tc = GATHER_TOKENS
    return pl.pallas_call(
        functools.partial(_gather_body, cap=cap, eg=eg),
        grid_spec=pltpu.PrefetchScalarGridSpec(
            num_scalar_prefetch=1,
            grid=(b, s // tc),
            in_specs=[pl.BlockSpec((1, e, tc), lambda bi, ci, bnd: (bi, 0, ci)),
                      pl.BlockSpec((1, tc, d), lambda bi, ci, bnd: (bi, ci, 0)),
                      pl.BlockSpec((1, tc, LANES), lambda bi, ci, bnd: (bi, ci, 0)),
                      pl.BlockSpec((1, tc, LANES), lambda bi, ci, bnd: (bi, ci, 0))],
            out_specs=[pl.BlockSpec((1, e, cap, d), lambda bi, ci, bnd: (bi, 0, 0, 0)),
                       pl.BlockSpec((1, e, cap, LANES), lambda bi, ci, bnd: (bi, 0, 0, 0))],
        ),
        out_shape=[jax.ShapeDtypeStruct((b, e, cap, d), BF16),
                   jax.ShapeDtypeStruct((b, e, cap, LANES), F32)],
        compiler_params=_cparams(("arbitrary", "arbitrary")),
        name="gather",
    )(bounds.reshape(-1), rank, hm, aff_hi, aff_lo)


def _ffn_body(xe_ref, gs_ref, wg_ref, wu_ref, wd_ref, ye_ref, acc_ref, *, mb):
    f = pl.program_id(1)
    nb, _, cap, d = xe_ref.shape
    wg = wg_ref[0].astype(BF16)
    wu = wu_ref[0].astype(BF16)
    wd = wd_ref[0].astype(BF16)

    @pl.when(f == 0)
    def _():
        acc_ref[...] = jnp.zeros_like(acc_ref)

    for b0 in range(0, nb, mb):
        xe = xe_ref[b0:b0 + mb, 0].reshape(mb * cap, d)
        a = _dot(xe, wg)
        u = _dot(xe, wu)
        hsw = (a * jax.nn.sigmoid(a) * u).astype(BF16)
        acc_ref[b0:b0 + mb] += _dot(hsw, wd).reshape(mb, cap, d)

    @pl.when(f == pl.num_programs(1) - 1)
    def _():
        lane = lax.broadcasted_iota(jnp.int32, gs_ref.shape[2:], 1)
        mine = lane == pl.program_id(0)
        for b in range(nb):
            gate = jnp.sum(jnp.where(mine, gs_ref[b, 0], 0.0), axis=-1, keepdims=True)
            ye_ref[b, 0] = (acc_ref[b] * gate).astype(BF16)


def _ffn(xe, gs, wg, wu, wd, ft=1024, mb=1):
    b, e, cap, d = xe.shape
    ff = wg.shape[2]
    return pl.pallas_call(
        functools.partial(_ffn_body, mb=mb),
        grid=(e, ff // ft),
        in_specs=[pl.BlockSpec((b, 1, cap, d), lambda ei, fi: (0, ei, 0, 0)),
                  pl.BlockSpec((b, 1, cap, LANES), lambda ei, fi: (0, ei, 0, 0)),
                  pl.BlockSpec((1, d, ft), lambda ei, fi: (ei, 0, fi)),
                  pl.BlockSpec((1, d, ft), lambda ei, fi: (ei, 0, fi)),
                  pl.BlockSpec((1, ft, d), lambda ei, fi: (ei, fi, 0))],
        out_specs=pl.BlockSpec((b, 1, cap, d), lambda ei, fi: (0, ei, 0, 0)),
        out_shape=jax.ShapeDtypeStruct((b, e, cap, d), BF16),
        scratch_shapes=[pltpu.VMEM((b, cap, d), F32)],
        compiler_params=_cparams(("arbitrary", "arbitrary"), vmem=VMEM_LIMIT_LARGE),
        name="ffn",
    )(xe, gs, wg, wu, wd)


def _scatter_body(bounds_ref, rank_ref, ye_ref, x2_ref, g_ref, out_ref, acc_ref, *, eg):
    w = SCATTER_WINDOW
    ts = x2_ref.shape[1]
    ne, cap = ye_ref.shape[1], ye_ref.shape[2]
    starts, passes = _moe_windows(bounds_ref, pl.program_id(0) * ne * LANES, ne, pl.program_id(1), ts, w)
    rank = rank_ref[0]
    acc_ref[...] = x2_ref[0]
    iota_w = lax.broadcasted_iota(jnp.int32, (1, w), 1)

    def operands(p, g0):
        onehots, rows = [], []
        for e in range(g0, g0 + eg):
            lo = starts[e] + p * w
            phys = pl.multiple_of(jnp.minimum(lo, cap - w), BF16_ROWS)
            slot = (iota_w + phys).astype(F32)
            hit = (rank[:, e:e + 1] == slot) & (slot >= lo.astype(F32))
            onehots.append(jnp.where(hit, 1.0, 0.0).astype(BF16))
            rows.append(ye_ref[0, e, pl.ds(phys, w), :])
        return jnp.concatenate(onehots, axis=1), jnp.concatenate(rows, axis=0)

    def one_pass(p, carry):
        ops = [operands(p, g0) for g0 in range(0, ne, eg)]
        total = _dot(*ops[0])
        for lhs, rhs in ops[1:]:
            total = total + _dot(lhs, rhs)
        acc_ref[...] += total
        return carry

    lax.fori_loop(0, passes, one_pass, 0)
    out_ref[0] = _rms(acc_ref[...], g_ref[...])


def _scatter(bounds, rank_t, ye, x2, g, eg=8):
    b, s, e = rank_t.shape
    cap, d = ye.shape[2], ye.shape[3]
    ts = SCATTER_TOKENS
    return pl.pallas_call(
        functools.partial(_scatter_body, eg=eg),
        grid_spec=pltpu.PrefetchScalarGridSpec(
            num_scalar_prefetch=1,
            grid=(b, s // ts),
            in_specs=[pl.BlockSpec((1, ts, e), lambda bi, ti, bnd: (bi, ti, 0)),
                      pl.BlockSpec((1, e, cap, d), lambda bi, ti, bnd: (bi, 0, 0, 0)),
                      pl.BlockSpec((1, ts, d), lambda bi, ti, bnd: (bi, ti, 0)),
                      pl.BlockSpec((1, d), lambda bi, ti, bnd: (0, 0))],
            out_specs=pl.BlockSpec((1, ts, d), lambda bi, ti, bnd: (bi, ti, 0)),
            scratch_shapes=[pltpu.VMEM((ts, d), F32)],
        ),
        out_shape=jax.ShapeDtypeStruct((b, s, d), F32),
        compiler_params=_cparams(("arbitrary", "arbitrary")),
        name="scatter",
    )(bounds.reshape(-1), rank_t, ye, x2, g)


def kernel(x, mem, ln_mix_g, w_in, hy_conv_w, hy_conv_b, filt_w1, filt_b1, filt_freq1, filt_w2, filt_b2,
           filt_freq2, filt_w3, hy_skip, w_up_hy, w_up_attn, w_gate, b_gate, w_out, ln_x_g, ln_mem_g,
           w_q_x, w_kv_mem, w_o_x, ln_moe_g, w_router, w_e_gate, w_e_up, w_e_down, ln_f_g):
    b, s, d = x.shape
    n = b * s
    x2d = x.reshape(n, d)

    half = HEAD_DIM // 2
    inv = ROPE_THETA ** (-jnp.arange(0, HEAD_DIM, 2, dtype=F32) / HEAD_DIM)
    ang = jnp.arange(s, dtype=F32)[:, None] * inv[None, :]
    lane = np.arange(LANES)
    cos_t = jnp.cos(ang)[:, lane % half]
    sin_t = jnp.sin(ang)[:, lane % half] * jnp.asarray(np.where(lane % HEAD_DIM < half, -1.0, 1.0), F32)[None, :]

    p_hy, *qkv, gates = _proj(x2d, ln_mix_g[None], w_in.astype(BF16), w_gate.astype(BF16), b_gate[None],
                              cos_t, sin_t, b, s)

    t_col = jnp.linspace(0.0, 1.0, s, dtype=F32)[:, None]
    grid_col = 2.0 * math.pi * jnp.arange(s, dtype=F32)[:, None] / s
    bands = jnp.linspace(1e-4, HY_BANDS - 1, HY_BANDS, dtype=F32)[None, :]
    feat = _filt(t_col, grid_col, bands, filt_w1, filt_b1, filt_freq1, filt_w2, filt_b2, filt_freq2)
    delta = jnp.linspace(math.log(HY_TARGET) / HY_SLOW_PCT, math.log(HY_TARGET) / HY_FAST_PCT,
                         HY_WIDTH, dtype=F32)[None, :]
    tables = _fft_tables(s)
    khat = _hyfilt(t_col, delta, feat, filt_w3, tables[1], tables[3])
    z = _hyena(p_hy, hy_conv_w, hy_conv_b[None], hy_skip, khat, tables)

    outs, lses = [], []
    for g, (window, dil) in enumerate(DIL_PAIRS):
        o_g, l_g = _attn_group(*qkv[3 * g:3 * g + 3], g, window // (2 * dil))
        outs.append(o_g)
        lses.append(l_g)

    kv = _memkv(mem, ln_mem_g[None], w_kv_mem.astype(BF16))
    wr_pad = jnp.pad(w_router, ((0, 0), (0, LANES - N_EXPERTS)))
    wr_cat = jnp.concatenate(_split(wr_pad), axis=1)
    x2, hm, aff, aff_hi, aff_lo = _merge(x2d, z.reshape(n, HY_WIDTH), outs, lses, gates,
                         w_up_hy.astype(BF16), w_up_attn.astype(BF16), w_out.astype(BF16),
                         ln_x_g[None], w_q_x.astype(BF16), kv, w_o_x.astype(BF16), ln_moe_g[None],
                         wr_cat, s)

    cap = max(1, EC_FACTOR * s // N_EXPERTS)
    aff3 = aff.reshape(b, s, LANES)
    rank, bounds = _topk(aff3[:, :, :N_EXPERTS].transpose(0, 2, 1), cap, MOE_BOUND_STEP)
    xe, gs = _gather(bounds, rank, hm.reshape(b, s, d), aff_hi.reshape(b, s, LANES), aff_lo.reshape(b, s, LANES), cap)
    ye = _ffn(xe, gs, w_e_gate, w_e_up, w_e_down)
    return _scatter(bounds, rank.transpose(0, 2, 1), ye, x2.reshape(b, s, d), ln_f_g[None])
```
